```python
import math
import jax
import jax.numpy as jnp
from jax import lax
import numpy as np

D_MODEL = 1024
BATCH = 8
SEQ = 2048
DEPTH = 2

N_EVEN = (DEPTH + 1) // 2
N_ODD = DEPTH // 2
EPS = 1e-6
ROPE_THETA = 10000.0
POS_OFFSET_MAX = 1024

GLA_HEADS = 4
GLA_DK = D_MODEL // 16
GLA_DV = D_MODEL // 8
GLA_RANK = 16
GLA_CHUNK = 64
GLA_GATE_NORMALIZER = 16.0

DIFF_HEADS = 4
DIFF_D = D_MODEL // 16
DIFF_DV = 2 * DIFF_D
ATTN_BLOCK = 128

SGU_GROUPS = 4
SGU_CH = D_MODEL // 8
SGU_CHUNK = 128

RET_HEADS = 4
RET_DK = D_MODEL // 16
RET_DV = D_MODEL // 8
RET_CHUNK = 128

N_EXPERTS = 32
TOP_K = 4
D_FF = D_MODEL
SWIGLU_ALPHA = 1.702
SWIGLU_LIMIT = 7.0
MOE_BLOCK = 256

GLA_QK_W = GLA_HEADS * GLA_DK
GLA_V_W = GLA_HEADS * GLA_DV
DIFF_QK_W = DIFF_HEADS * 2 * DIFF_D
DIFF_V_W = DIFF_HEADS * DIFF_DV
SGU_W = SGU_GROUPS * SGU_CH
RET_QK_W = RET_HEADS * RET_DK
RET_V_W = RET_HEADS * RET_DV
EVEN_WIDTHS = (GLA_QK_W, GLA_QK_W, GLA_V_W, GLA_RANK, GLA_V_W, DIFF_QK_W, DIFF_QK_W, DIFF_V_W)
ODD_WIDTHS = (SGU_W, SGU_W, RET_QK_W, RET_QK_W, RET_V_W, RET_V_W)
EVEN_IN = sum(EVEN_WIDTHS)
ODD_IN = sum(ODD_WIDTHS)
EVEN_OUT = GLA_V_W + DIFF_V_W
ODD_OUT = SGU_W + RET_V_W

kernel_name = 'hybrid_gla_diff_sgu_retention_moe'


def _split(z, widths):
    out, start = [], 0
    for w in widths:
        out.append(z[..., start:start + w])
        start += w
    return out


def _rms(x):
    xf = x.astype(jnp.float32)
    return (xf * lax.rsqrt(jnp.mean(jnp.square(xf), axis=-1, keepdims=True) + EPS)).astype(x.dtype)


def _layer_norm(x, g, b):
    xf = x.astype(jnp.float32)
    mu = jnp.mean(xf, axis=-1, keepdims=True)
    var = jnp.mean(jnp.square(xf - mu), axis=-1, keepdims=True)
    return ((xf - mu) * lax.rsqrt(var + EPS)).astype(x.dtype) * g + b


def _heads(t, n):
    B, L, _ = t.shape
    return t.reshape(B, L, n, -1).transpose(0, 2, 1, 3)


def _merge(t):
    B, H, L, e = t.shape
    return t.transpose(0, 2, 1, 3).reshape(B, L, H * e)


def _rope(x, pos):
    half = x.shape[-1] // 2
    inv_freq = ROPE_THETA ** (-jnp.arange(half, dtype=jnp.float32) / half)
    ang = pos.astype(jnp.float32)[..., None] * inv_freq
    ang = ang.reshape((ang.shape[0],) + (1,) * (x.ndim - 3) + ang.shape[1:])
    cos, sin = jnp.cos(ang), jnp.sin(ang)
    xf = x.astype(jnp.float32)
    x1, x2 = xf[..., :half], xf[..., half:]
    return jnp.concatenate([x1 * cos - x2 * sin, x2 * cos + x1 * sin], axis=-1).astype(x.dtype)


def _gla_chunked(q, k, v, log_a):
    B, H, L, dk = q.shape
    dv = v.shape[-1]
    C = GLA_CHUNK
    n = L // C
    f32 = jnp.float32
    qc = q.astype(f32).reshape(B, H, n, C, dk) * dk ** -0.5
    kc = k.astype(f32).reshape(B, H, n, C, dk)
    vc = v.astype(f32).reshape(B, H, n, C, dv)
    b = jnp.cumsum(log_a.astype(f32).reshape(B, H, n, C, dk), axis=3)
    b_last = b[:, :, :, -1:, :]
    q_dec = qc * jnp.exp(b)
    s = jnp.einsum('bhnid,bhnjd->bhnij', q_dec, kc * jnp.exp(-b))
    causal = jnp.tril(jnp.ones((C, C), dtype=bool))
    o_intra = jnp.einsum('bhnij,bhnje->bhnie', jnp.where(causal, s, 0.0), vc)
    chunk_kv = jnp.einsum('bhnjd,bhnje->bhnde', kc * jnp.exp(b_last - b), vc)
    chunk_decay = jnp.exp(b_last[:, :, :, 0, :])[..., None]

    def step(state, inp):
        kv, dec = inp
        return dec * state + kv, state

    _, prev = lax.scan(step, jnp.zeros((B, H, dk, dv), f32),
                       (jnp.moveaxis(chunk_kv, 2, 0), jnp.moveaxis(chunk_decay, 2, 0)))
    prev = jnp.moveaxis(prev, 0, 2)
    o_inter = jnp.einsum('bhnid,bhnde->bhnie', q_dec, prev)
    return (o_intra + o_inter).reshape(B, H, L, dv).astype(v.dtype)


def _diff_attention(q, k, v, lam):
    B, H, _, L, d = q.shape
    nb = L // ATTN_BLOCK
    q_blocks = jnp.moveaxis(q.reshape(B, H, 2, nb, ATTN_BLOCK, d), 3, 0)
    k_pos = jnp.arange(L)
    scale = d ** -0.5

    def one_block(args):
        q_blk, blk = args
        s = jnp.einsum('bhmqd,bhmkd->bhmqk', q_blk, k, preferred_element_type=jnp.float32) * scale
        q_pos = blk * ATTN_BLOCK + jnp.arange(ATTN_BLOCK)
        s = jnp.where(k_pos[None, :] <= q_pos[:, None], s, -jnp.inf)
        p = jax.nn.softmax(s, axis=-1)
        w = (p[:, :, 0] - lam * p[:, :, 1]).astype(v.dtype)
        return jnp.einsum('bhqk,bhke->bhqe', w, v)

    out = lax.map(one_block, (q_blocks, jnp.arange(nb)))
    return jnp.moveaxis(out, 0, 2).reshape(B, H, L, v.shape[-1])


def _spatial_gating(u, v, w_s, b_s):
    B, L, G, ch = v.shape
    n = L // SGU_CHUNK
    w = jnp.tril(w_s)
    vc = v.reshape(B, n, SGU_CHUNK, G, ch)
    s = jnp.einsum('gij,bnjgc->bnigc', w, vc) + b_s.T[None, None, :, :, None]
    return u * s.reshape(B, L, G, ch)


def _retention_chunked(q, k, v):
    B, H, L, dk = q.shape
    dv = v.shape[-1]
    C = RET_CHUNK
    n = L // C
    f32 = jnp.float32
    log_g = jnp.log(1.0 - 2.0 ** (-5.0 - jnp.arange(H, dtype=f32)))
    qc = q.astype(f32).reshape(B, H, n, C, dk)
    kc = k.astype(f32).reshape(B, H, n, C, dk) * dk ** -0.5
    vc = v.astype(f32).reshape(B, H, n, C, dv)
    idx = jnp.arange(C, dtype=f32)
    rel = idx[:, None] - idx[None, :]
    decay = jnp.where(rel >= 0, jnp.exp(log_g[:, None, None] * jnp.maximum(rel, 0.0)), 0.0)
    s = jnp.einsum('bhnid,bhnjd->bhnij', qc, kc) * decay[None, :, None]
    o_intra = jnp.einsum('bhnij,bhnje->bhnie', s, vc)
    k_dec = jnp.exp(log_g[:, None] * (C - 1.0 - idx))
    chunk_kv = jnp.einsum('bhnjd,bhnje->bhnde', kc * k_dec[None, :, None, :, None], vc)
    chunk_g = jnp.exp(log_g * C)[None, :, None, None]

    def step(state, kv):
        return chunk_g * state + kv, state

    _, prev = lax.scan(step, jnp.zeros((B, H, dk, dv), f32), jnp.moveaxis(chunk_kv, 2, 0))
    prev = jnp.moveaxis(prev, 0, 2)
    q_dec = jnp.exp(log_g[:, None] * (idx + 1.0))
    o_inter = jnp.einsum('bhnid,bhnde->bhnie', qc * q_dec[None, :, None, :, None], prev)
    return (o_intra + o_inter).reshape(B, H, L, dv).astype(v.dtype)


def _moe(h, w_router, b_router, w_in, b_in, w_out, b_out):
    T, D = h.shape
    logits = (h @ w_router + b_router).astype(jnp.float32)
    top_vals, top_idx = lax.top_k(logits, TOP_K)
    gates = jax.nn.softmax(top_vals, axis=-1).astype(h.dtype)
    TK = T * TOP_K
    e_flat = top_idx.reshape(-1)
    tok_flat = jnp.repeat(jnp.arange(T, dtype=jnp.int32), TOP_K)
    g_flat = gates.reshape(-1)
    order = jnp.argsort(e_flat)
    e_sorted, tok_sorted, g_sorted = e_flat[order], tok_flat[order], g_flat[order]
    counts = jnp.bincount(e_flat, length=N_EXPERTS)
    padded = (counts + MOE_BLOCK - 1) // MOE_BLOCK * MOE_BLOCK
    start = jnp.cumsum(counts) - counts
    pad_end = jnp.cumsum(padded)
    pad_start = pad_end - padded
    dest = pad_start[e_sorted] + (jnp.arange(TK) - start[e_sorted])
    n_blocks = (TK + MOE_BLOCK - 1) // MOE_BLOCK + N_EXPERTS
    R = n_blocks * MOE_BLOCK
    row_tok = jnp.full((R,), T, dtype=jnp.int32).at[dest].set(tok_sorted)
    row_gate = jnp.zeros((R,), h.dtype).at[dest].set(g_sorted)
    block_e = jnp.minimum(jnp.searchsorted(pad_end, jnp.arange(n_blocks) * MOE_BLOCK, side='right'),
                          N_EXPERTS - 1)
    h_pad = jnp.concatenate([h, jnp.zeros((1, D), h.dtype)], axis=0)
    xb = h_pad[row_tok].reshape(n_blocks, MOE_BLOCK, D)

    def expert_block(args):
        x_blk, e = args
        z = x_blk @ w_in[e] + b_in[e]
        glu = jnp.minimum(z[:, :D_FF], SWIGLU_LIMIT)
        lin = jnp.clip(z[:, D_FF:], -SWIGLU_LIMIT, SWIGLU_LIMIT)
        a = glu * jax.nn.sigmoid(SWIGLU_ALPHA * glu) * (lin + 1.0)
        return a @ w_out[e] + b_out[e]

    yb = lax.map(expert_block, (xb, block_e)).reshape(R, D)
    y = jnp.zeros((T + 1, D), h.dtype).at[row_tok].add(yb * row_gate[:, None])
    return y[:T]


def _even_mixer(h, positions, layer, w_in, w_gate, b_gate, gla_g, lq1, lk1, lq2, lk2, diff_g, w_out):
    B, L, _ = h.shape
    gq, gk, gv, gr, gg, dq, dk, dv = _split(h @ w_in, EVEN_WIDTHS)
    log_a = jax.nn.log_sigmoid((gr @ w_gate + b_gate).astype(jnp.float32)) / GLA_GATE_NORMALIZER
    o_gla = _gla_chunked(_heads(gq, GLA_HEADS), _heads(gk, GLA_HEADS), _heads(gv, GLA_HEADS),
                         _heads(log_a, GLA_HEADS))
    o_gla = _merge(_rms(o_gla) * gla_g) * jax.nn.silu(gg)
    lam_init = 0.8 - 0.6 * math.exp(-0.3 * layer)
    lam = (jnp.exp(jnp.sum(lq1 * lk1).astype(jnp.float32))
           - jnp.exp(jnp.sum(lq2 * lk2).astype(jnp.float32)) + lam_init)
    q = _rope(dq.reshape(B, L, DIFF_HEADS, 2, DIFF_D).transpose(0, 2, 3, 1, 4), positions)
    k = _rope(dk.reshape(B, L, DIFF_HEADS, 2, DIFF_D).transpose(0, 2, 3, 1, 4), positions)
    o_diff = _diff_attention(q, k, _heads(dv, DIFF_HEADS), lam)
    o_diff = _merge(_rms(o_diff) * diff_g) * (1.0 - lam_init)
    return jnp.concatenate([o_gla, o_diff], axis=-1) @ w_out


def _odd_mixer(h, positions, w_in, ln_g, ln_b, w_s, b_s, w_out):
    B, L, _ = h.shape
    su, sv, rq, rk, rv, rg = _split(h @ w_in, ODD_WIDTHS)
    su = jax.nn.gelu(su, approximate=False)
    sv = _layer_norm(jax.nn.gelu(sv, approximate=False), ln_g, ln_b)
    o_sgu = _spatial_gating(su.reshape(B, L, SGU_GROUPS, SGU_CH), sv.reshape(B, L, SGU_GROUPS, SGU_CH),
                            w_s, b_s).reshape(B, L, SGU_W)
    q = _rope(_heads(rq, RET_HEADS), positions)
    k = _rope(_heads(rk, RET_HEADS), positions)
    o_ret = _merge(_rms(_retention_chunked(q, k, _heads(rv, RET_HEADS)))) * jax.nn.silu(rg)
    return jnp.concatenate([o_sgu, o_ret], axis=-1) @ w_out


def setup_inputs(seed: int = 0) -> dict:
    key = jax.random.key(seed)
    k = jax.random.split(key, 32)
    f32 = jnp.float32
    D = D_MODEL

    def nrm(i, shape, scale):
        return jax.random.normal(k[i], shape, f32) * scale

    return {
        'x': nrm(0, (BATCH, SEQ, D), 1.0),
        'c': nrm(1, (BATCH, D), 1.0),
        'positions': (jnp.arange(SEQ, dtype=jnp.int32)[None, :]
                      + jax.random.randint(k[2], (BATCH, 1), 0, POS_OFFSET_MAX, dtype=jnp.int32)),
        'w_ada': nrm(3, (DEPTH, D, 6 * D), 0.5 * D ** -0.5),
        'b_ada': nrm(4, (DEPTH, 6 * D), 0.02),
        'even_w_in': nrm(5, (N_EVEN, D, EVEN_IN), D ** -0.5),
        'gla_w_gate': nrm(6, (N_EVEN, GLA_RANK, GLA_QK_W), GLA_RANK ** -0.5),
        'gla_b_gate': nrm(7, (N_EVEN, GLA_QK_W), 0.02),
        'gla_norm_g': 1.0 + nrm(8, (N_EVEN, GLA_DV), 0.02),
        'diff_lam_q1': nrm(9, (N_EVEN, DIFF_D), 0.1),
        'diff_lam_k1': nrm(10, (N_EVEN, DIFF_D), 0.1),
        'diff_lam_q2': nrm(11, (N_EVEN, DIFF_D), 0.1),
        'diff_lam_k2': nrm(12, (N_EVEN, DIFF_D), 0.1),
        'diff_norm_g': 1.0 + nrm(13, (N_EVEN, DIFF_DV), 0.02),
        'even_w_out': nrm(14, (N_EVEN, EVEN_OUT, D), EVEN_OUT ** -0.5),
        'odd_w_in': nrm(15, (N_ODD, D, ODD_IN), D ** -0.5),
        'sgu_ln_g': 1.0 + nrm(16, (N_ODD, SGU_W), 0.02),
        'sgu_ln_b': nrm(17, (N_ODD, SGU_W), 0.02),
        'sgu_w': nrm(18, (N_ODD, SGU_GROUPS, SGU_CHUNK, SGU_CHUNK), SGU_CHUNK ** -0.5),
        'sgu_b': 1.0 + nrm(19, (N_ODD, SGU_GROUPS, SGU_CHUNK), 0.02),
        'odd_w_out': nrm(20, (N_ODD, ODD_OUT, D), ODD_OUT ** -0.5),
        'router_w': nrm(21, (DEPTH, D, N_EXPERTS), D ** -0.5),
        'router_b': nrm(22, (DEPTH, N_EXPERTS), 0.01),
        'expert_w_in': nrm(23, (DEPTH, N_EXPERTS, D, 2 * D_FF), D ** -0.5),
        'expert_b_in': nrm(24, (DEPTH, N_EXPERTS, 2 * D_FF), 0.02),
        'expert_w_out': nrm(25, (DEPTH, N_EXPERTS, D_FF, D), D_FF ** -0.5),
        'expert_b_out': nrm(26, (DEPTH, N_EXPERTS, D), 0.02),
        'final_norm_g': 1.0 + nrm(27, (D,), 0.02),
    }


def reference(x, c, positions, w_ada, b_ada, even_w_in, gla_w_gate, gla_b_gate, gla_norm_g,
              diff_lam_q1, diff_lam_k1, diff_lam_q2, diff_lam_k2, diff_norm_g, even_w_out,
              odd_w_in, sgu_ln_g, sgu_ln_b, sgu_w, sgu_b, odd_w_out,
              router_w, router_b, expert_w_in, expert_b_in, expert_w_out, expert_b_out,
              final_norm_g):
    B, L, D = x.shape
    c_act = jax.nn.silu(c)
    for layer in range(DEPTH):
        mod = (c_act @ w_ada[layer] + b_ada[layer])[:, None, :]
        shift1, scale1, gate1, shift2, scale2, gate2 = jnp.split(mod, 6, axis=-1)
        h = _rms(x) * (1.0 + scale1) + shift1
        j = layer // 2
        if layer % 2 == 0:
            y = _even_mixer(h, positions, layer, even_w_in[j], gla_w_gate[j], gla_b_gate[j], gla_norm_g[j],
                            diff_lam_q1[j], diff_lam_k1[j], diff_lam_q2[j], diff_lam_k2[j],
                            diff_norm_g[j], even_w_out[j])
        else:
            y = _odd_mixer(h, positions, odd_w_in[j], sgu_ln_g[j], sgu_ln_b[j], sgu_w[j], sgu_b[j],
                           odd_w_out[j])
        x = x + gate1 * y
        h = _rms(x) * (1.0 + scale2) + shift2
        y = _moe(h.reshape(B * L, D), router_w[layer], router_b[layer], expert_w_in[layer],
                 expert_b_in[layer], expert_w_out[layer], expert_b_out[layer]).reshape(B, L, D)
        x = x + gate2 * y
    return _rms(x) * final_norm_g
```

```python
import functools
import math

import jax
import jax.numpy as jnp
from jax import lax
from jax.experimental import pallas as pl
from jax.experimental.pallas import tpu as pltpu

F32 = jnp.float32
BF16 = jnp.bfloat16

D_MODEL = 1024
EPS = 1e-6
ROPE_THETA = 10000.0
ROPE_HALF = 32

GLA_HEADS = 4
GLA_DK = 64
GLA_DV = 128
GLA_RANK = 16
GLA_CHUNK = 64
GLA_GATE_NORMALIZER = 16.0
GLA_QK_W = GLA_HEADS * GLA_DK
GLA_V_W = GLA_HEADS * GLA_DV

DIFF_HEADS = 4
DIFF_D = 64
DIFF_DV = 128
DIFF_QK_W = DIFF_HEADS * 2 * DIFF_D
DIFF_V_W = DIFF_HEADS * DIFF_DV

SGU_GROUPS = 4
SGU_CH = 128
SGU_CHUNK = 128
SGU_W = SGU_GROUPS * SGU_CH

RET_HEADS = 4
RET_DK = 64
RET_DV = 128
RET_CHUNK = 128
RET_QK_W = RET_HEADS * RET_DK
RET_V_W = RET_HEADS * RET_DV

N_EXPERTS = 32
TOP_K = 4
D_FF = D_MODEL
SWIGLU_ALPHA = 1.702
SWIGLU_LIMIT = 7.0

LANES = 128
SUBLANES = 8
ROW_TILES = D_MODEL // LANES

PROJ_ROWS = 512
GLA_ROWS = 256
ATT_Q_ROWS = 256
ROUTE_ROWS = 512
DISPATCH_ROWS = 256
EXPERT_ROWS = 256
COMBINE_ROWS = 256
MOD_COLS = 1536

MIB = 1024 * 1024


def _cp(semantics, vmem_mib):
    return pltpu.CompilerParams(dimension_semantics=semantics, vmem_limit_bytes=vmem_mib * MIB)


def _dot(a, b):
    return jnp.dot(a, b, preferred_element_type=F32)


def _dot_nt(a, b):
    return lax.dot_general(a, b, (((1,), (1,)), ((), ())), preferred_element_type=F32)


def _dot_tn(a, b):
    return lax.dot_general(a, b, (((0,), (0,)), ((), ())), preferred_element_type=F32)


def _rms_rows(x):
    return x * lax.rsqrt(jnp.mean(x * x, axis=-1, keepdims=True) + EPS)


def _silu(x):
    return x * jax.nn.sigmoid(x)


def _rope_chunk(v, cos, sin, lo_half):
    rot = jnp.where(lo_half, -pltpu.roll(v, 96, 1), pltpu.roll(v, 32, 1))
    return v * cos + rot * sin


def _mod_kernel(c_ref, w_ref, b_ref, o_ref):
    c = c_ref[...]
    ca = _silu(c).astype(BF16)
    o_ref[0] = _dot(ca, w_ref[0].astype(BF16)) + b_ref[0]


def _modulation(c, w_ada, b_ada):
    depth, d, n = w_ada.shape
    bsz = c.shape[0]
    return pl.pallas_call(
        _mod_kernel,
        grid=(depth, n // MOD_COLS),
        in_specs=[
            pl.BlockSpec((bsz, d), lambda l, j: (0, 0)),
            pl.BlockSpec((1, d, MOD_COLS), lambda l, j: (l, 0, j)),
            pl.BlockSpec((1, 1, MOD_COLS), lambda l, j: (l, 0, j)),
        ],
        out_specs=pl.BlockSpec((1, bsz, MOD_COLS), lambda l, j: (l, 0, j)),
        out_shape=jax.ShapeDtypeStruct((depth, bsz, n), F32),
        compiler_params=_cp(("arbitrary", "arbitrary"), 40),
        name="adaln_mod",
    )(c, w_ada, b_ada.reshape(depth, 1, n))


def _rope_table_kernel(p_ref, f_ref, c_ref, s_ref):
    ang = p_ref[...].astype(F32) * f_ref[...]
    c_ref[...] = jnp.cos(ang)
    s_ref[...] = jnp.sin(ang)


def _rope_tables(positions):
    t = positions.size
    per_row = LANES // ROPE_HALF
    rows = t // per_row
    pos_d = jnp.repeat(positions.reshape(rows, per_row), ROPE_HALF, axis=1)
    inv_freq = ROPE_THETA ** (-jnp.arange(ROPE_HALF, dtype=F32) / ROPE_HALF)
    freq_d = jnp.tile(inv_freq, per_row).reshape(1, LANES)
    tr = min(512, rows)
    cos_d, sin_d = pl.pallas_call(
        _rope_table_kernel,
        grid=(rows // tr,),
        in_specs=[pl.BlockSpec((tr, LANES), lambda i: (i, 0)),
                  pl.BlockSpec((1, LANES), lambda i: (0, 0))],
        out_specs=[pl.BlockSpec((tr, LANES), lambda i: (i, 0))] * 2,
        out_shape=[jax.ShapeDtypeStruct((rows, LANES), F32)] * 2,
        compiler_params=_cp(("arbitrary",), 32),
        name="rope_tables",
    )(pos_d, freq_d)
    cos = jnp.tile(cos_d.reshape(t, ROPE_HALF), (1, per_row))
    sin = jnp.tile(sin_d.reshape(t, ROPE_HALF), (1, per_row))
    return cos, sin


def _modulated_rms(x_ref, mod_ref, which):
    x = x_ref[...]
    shift = mod_ref[0, 3 * which:3 * which + 1, :]
    scale = mod_ref[0, 3 * which + 1:3 * which + 2, :]
    return _rms_rows(x) * (1.0 + scale) + shift


EVEN_Z = (GLA_V_W, GLA_V_W, DIFF_QK_W, DIFF_QK_W, DIFF_V_W)
EVEN_Z_W = sum(EVEN_Z)
EVEN_F_W = 3 * GLA_QK_W
EVEN_W_COLS = EVEN_Z_W + 2 * GLA_QK_W + LANES


def _proj_even_kernel(x_ref, mod_ref, w_ref, wg_ref, bg_ref, cos_ref, sin_ref, z_ref, f_ref):
    hb = _modulated_rms(x_ref, mod_ref, 0).astype(BF16)

    def sec(a, b):
        return _dot(hb, w_ref[:, a:b])

    for a, b in ((0, 512), (512, 1024), (2048, 2560)):
        z_ref[:, a:b] = sec(a, b).astype(BF16)
    cos = cos_ref[...]
    sin = sin_ref[...]
    lo_half = (lax.broadcasted_iota(jnp.int32, cos.shape, 1) % 64) < ROPE_HALF
    q_scale = DIFF_D ** -0.5
    for a, scl in ((1024, q_scale), (1536, 1.0)):
        for c in range(DIFF_QK_W // LANES):
            v = sec(a + c * LANES, a + (c + 1) * LANES)
            z_ref[:, a + c * LANES:a + (c + 1) * LANES] = (
                _rope_chunk(v, cos, sin, lo_half) * scl).astype(BF16)
    f_ref[:, 0:256] = sec(2560, 2816) * (GLA_DK ** -0.5)
    f_ref[:, 256:512] = sec(2816, 3072)
    gr = sec(3072, 3200).astype(BF16)
    pre = _dot(gr, wg_ref[...]) + bg_ref[...]
    log_sig = jnp.minimum(pre, 0.0) - jnp.log1p(jnp.exp(-jnp.abs(pre)))
    f_ref[:, 512:768] = log_sig / GLA_GATE_NORMALIZER


def _proj_even(x2, mod, w_in, w_gate, b_gate, cos, sin, seq):
    t, d = x2.shape
    tm = min(PROJ_ROWS, seq)
    gq, gk, gv, gr, gg, dq, dk, dv = jnp.split(
        w_in, [256, 512, 1024, 1040, 1552, 2064, 2576], axis=1)
    gr_pad = jnp.pad(gr, ((0, 0), (0, LANES - GLA_RANK)))
    w_cat = jnp.concatenate([gv, gg, dq, dk, dv, gq, gk, gr_pad], axis=1).astype(BF16)
    wg_pad = jnp.pad(w_gate, ((0, LANES - GLA_RANK), (0, 0))).astype(BF16)
    return pl.pallas_call(
        _proj_even_kernel,
        grid=(t // tm,),
        in_specs=[
            pl.BlockSpec((tm, d), lambda i: (i, 0)),
            pl.BlockSpec((1, 6, d), lambda i: (i // (seq // tm), 0, 0)),
            pl.BlockSpec((d, EVEN_W_COLS), lambda i: (0, 0)),
            pl.BlockSpec((LANES, GLA_QK_W), lambda i: (0, 0)),
            pl.BlockSpec((1, GLA_QK_W), lambda i: (0, 0)),
            pl.BlockSpec((tm, LANES), lambda i: (i, 0)),
            pl.BlockSpec((tm, LANES), lambda i: (i, 0)),
        ],
        out_specs=[pl.BlockSpec((tm, EVEN_Z_W), lambda i: (i, 0)),
                   pl.BlockSpec((tm, EVEN_F_W), lambda i: (i, 0))],
        out_shape=[jax.ShapeDtypeStruct((t, EVEN_Z_W), BF16),
                   jax.ShapeDtypeStruct((t, EVEN_F_W), F32)],
        compiler_params=_cp(("arbitrary",), 48),
        name="proj_even",
    )(x2, mod, w_cat, wg_pad, b_gate.reshape(1, GLA_QK_W), cos, sin)


ODD_Z_W = 2 * SGU_W + 2 * RET_V_W
ODD_F_W = 2 * RET_QK_W
ODD_W_COLS = ODD_Z_W + ODD_F_W


def _gelu_exact(x):
    return 0.5 * x * (1.0 + lax.erf(x * (2.0 ** -0.5)))


def _proj_odd_kernel(x_ref, mod_ref, w_ref, lng_ref, lnb_ref, cos_ref, sin_ref, z_ref, f_ref):
    hb = _modulated_rms(x_ref, mod_ref, 0).astype(BF16)

    def sec(a, b):
        return _dot(hb, w_ref[:, a:b])

    z_ref[:, 0:512] = _gelu_exact(sec(0, 512)).astype(BF16)
    sv = _gelu_exact(sec(512, 1024))
    mu = jnp.mean(sv, axis=-1, keepdims=True)
    cen = sv - mu
    var = jnp.mean(cen * cen, axis=-1, keepdims=True)
    z_ref[:, 512:1024] = (cen * lax.rsqrt(var + EPS) * lng_ref[...] + lnb_ref[...]).astype(BF16)
    for a, b in ((1024, 1536), (1536, 2048)):
        z_ref[:, a:b] = sec(a, b).astype(BF16)
    cos = cos_ref[...]
    sin = sin_ref[...]
    lo_half = (lax.broadcasted_iota(jnp.int32, cos.shape, 1) % 64) < ROPE_HALF
    k_scale = RET_DK ** -0.5
    for a, scl in ((0, 1.0), (256, k_scale)):
        for c in range(RET_QK_W // LANES):
            v = sec(ODD_Z_W + a + c * LANES, ODD_Z_W + a + (c + 1) * LANES)
            f_ref[:, a + c * LANES:a + (c + 1) * LANES] = _rope_chunk(v, cos, sin, lo_half) * scl


def _proj_odd(x2, mod, w_in, ln_g, ln_b, cos, sin, seq):
    t, d = x2.shape
    tm = min(PROJ_ROWS, seq)
    su, sv, rq, rk, rv, rg = jnp.split(w_in, [512, 1024, 1280, 1536, 2048], axis=1)
    w_cat = jnp.concatenate([su, sv, rv, rg, rq, rk], axis=1).astype(BF16)
    return pl.pallas_call(
        _proj_odd_kernel,
        grid=(t // tm,),
        in_specs=[
            pl.BlockSpec((tm, d), lambda i: (i, 0)),
            pl.BlockSpec((1, 6, d), lambda i: (i // (seq // tm), 0, 0)),
            pl.BlockSpec((d, ODD_W_COLS), lambda i: (0, 0)),
            pl.BlockSpec((1, SGU_W), lambda i: (0, 0)),
            pl.BlockSpec((1, SGU_W), lambda i: (0, 0)),
            pl.BlockSpec((tm, LANES), lambda i: (i, 0)),
            pl.BlockSpec((tm, LANES), lambda i: (i, 0)),
        ],
        out_specs=[pl.BlockSpec((tm, ODD_Z_W), lambda i: (i, 0)),
                   pl.BlockSpec((tm, ODD_F_W), lambda i: (i, 0))],
        out_shape=[jax.ShapeDtypeStruct((t, ODD_Z_W), BF16),
                   jax.ShapeDtypeStruct((t, ODD_F_W), F32)],
        compiler_params=_cp(("arbitrary",), 48),
        name="proj_odd",
    )(x2, mod, w_cat, ln_g.reshape(1, SGU_W), ln_b.reshape(1, SGU_W), cos, sin)


def _gla_kernel(q_ref, k_ref, la_ref, v_ref, gg_ref, g_ref, o_ref, st_ref):
    @pl.when(pl.program_id(1) == 0)
    def _():
        st_ref[...] = jnp.zeros_like(st_ref)

    c = GLA_CHUNK
    rows = q_ref.shape[0]
    r_i = lax.broadcasted_iota(jnp.int32, (c, c), 0)
    c_i = lax.broadcasted_iota(jnp.int32, (c, c), 1)
    tril = (r_i >= c_i).astype(BF16)
    r2 = lax.broadcasted_iota(jnp.int32, (2 * c, c), 0) % c
    c2 = lax.broadcasted_iota(jnp.int32, (2 * c, c), 1)
    causal2 = r2 >= c2
    lane = lax.broadcasted_iota(jnp.int32, (c, LANES), 1)
    lo = lane < GLA_DK
    lane_s = lax.broadcasted_iota(jnp.int32, (LANES, LANES), 1) < GLA_DK
    g_row = g_ref[...]
    states = [st_ref[0], st_ref[1]]
    for j in range(rows // c):
        rs = slice(j * c, (j + 1) * c)
        la = la_ref[rs, :]
        la_hi = la.astype(BF16)
        la_lo = (la - la_hi.astype(F32)).astype(BF16)
        b = _dot(tril, la_hi) + _dot(tril, la_lo)
        b_last = b[c - 1:c, :]
        qd = q_ref[rs, :] * jnp.exp(b)
        kk = k_ref[rs, :]
        ki = kk * jnp.exp(-b)
        kd = kk * jnp.exp(b_last - b)
        dec = jnp.exp(b_last)
        for p in range(GLA_HEADS // 2):
            ls = slice(p * LANES, (p + 1) * LANES)
            qd_p = qd[:, ls]
            qm = (jnp.where(lo, qd_p, 0.0).astype(BF16), jnp.where(lo, 0.0, qd_p).astype(BF16))
            s2 = _dot_nt(jnp.concatenate(qm, axis=0), ki[:, ls].astype(BF16))
            s2 = jnp.where(causal2, s2, 0.0).astype(BF16)
            kd_p = kd[:, ls].astype(BF16)
            st_b = states[p].astype(BF16)
            new = []
            for hh in range(2):
                h = 2 * p + hh
                hs = slice(h * GLA_DV, (h + 1) * GLA_DV)
                v_h = v_ref[rs, hs]
                o = _dot(s2[hh * c:(hh + 1) * c, :], v_h) + _dot_nt(qm[hh], st_b)
                o = _rms_rows(o) * g_row
                gate = _silu(gg_ref[rs, hs].astype(F32))
                o_ref[rs, hs] = (o * gate).astype(BF16)
                new.append(_dot_tn(v_h, kd_p))
            states[p] = dec[:, ls] * states[p] + jnp.where(lane_s, new[0], new[1])
    st_ref[0] = states[0]
    st_ref[1] = states[1]


def _gla(z, f, norm_g, bsz, seq):
    t = z.shape[0]
    tc = min(GLA_ROWS, seq)
    nc = seq // tc
    return pl.pallas_call(
        _gla_kernel,
        grid=(bsz, nc),
        in_specs=[
            pl.BlockSpec((tc, GLA_QK_W), lambda b, i: (b * nc + i, 0)),
            pl.BlockSpec((tc, GLA_QK_W), lambda b, i: (b * nc + i, 1)),
            pl.BlockSpec((tc, GLA_QK_W), lambda b, i: (b * nc + i, 2)),
            pl.BlockSpec((tc, GLA_V_W), lambda b, i: (b * nc + i, 0)),
            pl.BlockSpec((tc, GLA_V_W), lambda b, i: (b * nc + i, 1)),
            pl.BlockSpec((1, GLA_DV), lambda b, i: (0, 0)),
        ],
        out_specs=pl.BlockSpec((tc, GLA_V_W), lambda b, i: (b * nc + i, 0)),
        out_shape=jax.ShapeDtypeStruct((t, GLA_V_W), BF16),
        scratch_shapes=[pltpu.VMEM((2, LANES, LANES), F32)],
        compiler_params=_cp(("arbitrary", "arbitrary"), 32),
        name="gla",
    )(f, f, f, z, z, norm_g.reshape(1, GLA_DV))


def _diff_attn_kernel(lam_init, q_ref, k_ref, v_ref, lq1_ref, lk1_ref, lq2_ref, lk2_ref, g_ref,
                      o_ref):
    tq = q_ref.shape[0]
    qi = pl.program_id(2)
    q = q_ref[...]
    lane = lax.broadcasted_iota(jnp.int32, q.shape, 1)
    lo = lane < DIFF_D
    zero = jnp.zeros_like(q)
    qq = jnp.concatenate([jnp.where(lo, q, zero), jnp.where(lo, zero, q)], axis=0)

    def step(j, carry, masked):
        m, l, acc = carry
        start = pl.multiple_of(j * tq, tq)
        kj = k_ref[pl.ds(start, tq), :]
        vj = v_ref[pl.ds(start, tq), :]
        s = _dot_nt(qq, kj)
        if masked:
            r = lax.broadcasted_iota(jnp.int32, s.shape, 0) % tq
            cc = lax.broadcasted_iota(jnp.int32, s.shape, 1)
            s = jnp.where(cc <= r, s, -jnp.inf)
        m_new = jnp.maximum(m, jnp.max(s, axis=-1, keepdims=True))
        alpha = jnp.exp(m - m_new)
        p = jnp.exp(s - m_new)
        l = alpha * l + jnp.sum(p, axis=-1, keepdims=True)
        acc = alpha * acc + _dot(p.astype(BF16), vj)
        return m_new, l, acc

    init = (jnp.full((2 * tq, 1), -jnp.inf, F32), jnp.zeros((2 * tq, 1), F32),
            jnp.zeros((2 * tq, DIFF_DV), F32))
    carry = lax.fori_loop(0, qi, lambda j, cr: step(j, cr, False), init)
    _, l, acc = step(qi, carry, True)
    o12 = acc / l
    lam = (jnp.exp(jnp.sum(lq1_ref[...] * lk1_ref[...], axis=-1, keepdims=True))
           - jnp.exp(jnp.sum(lq2_ref[...] * lk2_ref[...], axis=-1, keepdims=True)) + lam_init)
    o = o12[:tq] - lam * o12[tq:]
    o_ref[...] = (_rms_rows(o) * g_ref[...] * (1.0 - lam_init)).astype(BF16)


def _diff_attn(z, lq1, lk1, lq2, lk2, norm_g, lam_init, bsz, seq):
    t = z.shape[0]
    tq = min(ATT_Q_ROWS, seq)
    nq = seq // tq
    qb, kb, vb = 1024 // LANES, 1536 // LANES, 2048 // LANES
    small = pl.BlockSpec((1, DIFF_D), lambda b, h, i: (0, 0))
    return pl.pallas_call(
        functools.partial(_diff_attn_kernel, lam_init),
        grid=(bsz, DIFF_HEADS, nq),
        in_specs=[
            pl.BlockSpec((tq, LANES), lambda b, h, i: (b * nq + i, qb + h)),
            pl.BlockSpec((seq, LANES), lambda b, h, i: (b, kb + h)),
            pl.BlockSpec((seq, LANES), lambda b, h, i: (b, vb + h)),
            small, small, small, small,
            pl.BlockSpec((1, DIFF_DV), lambda b, h, i: (0, 0)),
        ],
        out_specs=pl.BlockSpec((tq, DIFF_DV), lambda b, h, i: (b * nq + i, h)),
        out_shape=jax.ShapeDtypeStruct((t, DIFF_V_W), BF16),
        compiler_params=_cp(("arbitrary", "arbitrary", "arbitrary"), 32),
        name="diff_attn",
    )(z, z, z, lq1.reshape(1, DIFF_D), lk1.reshape(1, DIFF_D), lq2.reshape(1, DIFF_D),
      lk2.reshape(1, DIFF_D), norm_g.reshape(1, DIFF_DV))


def _sgu_ret_kernel(su_ref, sv_ref, rv_ref, rg_ref, q_ref, k_ref, ws_ref, bs_ref, o_ref, st_ref):
    @pl.when(pl.program_id(1) == 0)
    def _():
        st_ref[...] = jnp.zeros_like(st_ref)

    c = RET_CHUNK
    row = lax.broadcasted_iota(jnp.int32, (c, c), 0)
    col = lax.broadcasted_iota(jnp.int32, (c, c), 1)
    causal = row >= col
    for g in range(SGU_GROUPS):
        gs = slice(g * SGU_CH, (g + 1) * SGU_CH)
        w = jnp.where(causal, ws_ref[g], 0.0).astype(BF16)
        s = _dot(w, sv_ref[:, gs]) + bs_ref[g]
        o_ref[:, gs] = (su_ref[:, gs].astype(F32) * s).astype(BF16)
    log_g = [math.log(1.0 - 2.0 ** (-5.0 - h)) for h in range(RET_HEADS)]
    lo = col < RET_DK
    rel = (row - col).astype(F32)
    pos = row.astype(F32)
    for p in range(RET_HEADS // 2):
        ls = slice(p * LANES, (p + 1) * LANES)
        lg = jnp.where(lo, log_g[2 * p], log_g[2 * p + 1])
        q_p = q_ref[:, ls]
        k_p = k_ref[:, ls]
        qm = (jnp.where(lo, q_p, 0.0).astype(BF16), jnp.where(lo, 0.0, q_p).astype(BF16))
        s2 = _dot_nt(jnp.concatenate(qm, axis=0), k_p.astype(BF16))
        qd = q_p * jnp.exp(lg * (pos + 1.0))
        qdm = (jnp.where(lo, qd, 0.0).astype(BF16), jnp.where(lo, 0.0, qd).astype(BF16))
        kd = (k_p * jnp.exp(lg * (c - 1.0 - pos))).astype(BF16)
        st = st_ref[p]
        st_b = st.astype(BF16)
        new = []
        for hh in range(2):
            h = 2 * p + hh
            hs = slice(h * RET_DV, (h + 1) * RET_DV)
            decay = jnp.where(causal, jnp.exp(log_g[h] * jnp.maximum(rel, 0.0)), 0.0)
            s_h = (s2[hh * c:(hh + 1) * c, :] * decay).astype(BF16)
            v_h = rv_ref[:, hs]
            o = _dot(s_h, v_h) + _dot_nt(qdm[hh], st_b)
            gate = _silu(rg_ref[:, hs].astype(F32))
            o_ref[:, SGU_W + h * RET_DV:SGU_W + (h + 1) * RET_DV] = (
                _rms_rows(o) * gate).astype(BF16)
            new.append(_dot_tn(v_h, kd))
        st_ref[p] = jnp.exp(lg * float(c)) * st + jnp.where(lo, new[0], new[1])


def _sgu_ret(z, f, w_s, b_s, bsz, seq):
    t = z.shape[0]
    c = RET_CHUNK
    nc = seq // c
    return pl.pallas_call(
        _sgu_ret_kernel,
        grid=(bsz, nc),
        in_specs=[
            pl.BlockSpec((c, SGU_W), lambda b, i: (b * nc + i, 0)),
            pl.BlockSpec((c, SGU_W), lambda b, i: (b * nc + i, 1)),
            pl.BlockSpec((c, RET_V_W), lambda b, i: (b * nc + i, 2)),
            pl.BlockSpec((c, RET_V_W), lambda b, i: (b * nc + i, 3)),
            pl.BlockSpec((c, RET_QK_W), lambda b, i: (b * nc + i, 0)),
            pl.BlockSpec((c, RET_QK_W), lambda b, i: (b * nc + i, 1)),
            pl.BlockSpec((SGU_GROUPS, c, c), lambda b, i: (0, 0, 0)),
            pl.BlockSpec((SGU_GROUPS, c, 1), lambda b, i: (0, 0, 0)),
        ],
        out_specs=pl.BlockSpec((c, SGU_W + RET_V_W), lambda b, i: (b * nc + i, 0)),
        out_shape=jax.ShapeDtypeStruct((t, SGU_W + RET_V_W), BF16),
        scratch_shapes=[pltpu.VMEM((2, LANES, LANES), F32)],
        compiler_params=_cp(("arbitrary", "arbitrary"), 32),
        name="sgu_retention",
    )(z, z, z, z, f, f, w_s, b_s.reshape(SGU_GROUPS, c, 1))


def _out_proj_kernel(n_in, *refs):
    o_refs = refs[:n_in]
    w_ref, x_ref, mod_ref, rw_ref, rb_ref, xn_ref, h_ref, lg_ref = refs[n_in:]
    tm = x_ref.shape[0]
    k_each = D_MODEL // n_in
    y = _dot(o_refs[0][...], w_ref[0:k_each, :])
    for n in range(1, n_in):
        y = y + _dot(o_refs[n][...], w_ref[n * k_each:(n + 1) * k_each, :])
    gate1 = mod_ref[0, 2:3, :]
    xn = x_ref[...] + gate1 * y
    xn_ref[...] = xn
    h = _rms_rows(xn) * (1.0 + mod_ref[0, 4:5, :]) + mod_ref[0, 3:4, :]
    for c in range(ROW_TILES):
        h_ref[pl.ds(c, tm, stride=ROW_TILES), :] = h[:, c * LANES:(c + 1) * LANES]
    lg_ref[...] = _dot(h.astype(BF16), rw_ref[...]) + rb_ref[...]


def _out_proj(mixed, w_out, x2, mod, router_w, router_b, seq):
    t, d = x2.shape
    tm = min(PROJ_ROWS, seq)
    n_in = len(mixed)
    k_each = d // n_in
    rw = jnp.pad(router_w, ((0, 0), (0, LANES - N_EXPERTS))).astype(BF16)
    rb = jnp.pad(router_b, (0, LANES - N_EXPERTS)).reshape(1, LANES)
    return pl.pallas_call(
        functools.partial(_out_proj_kernel, n_in),
        grid=(t // tm,),
        in_specs=[pl.BlockSpec((tm, k_each), lambda i: (i, 0)) for _ in mixed] + [
            pl.BlockSpec((d, d), lambda i: (0, 0)),
            pl.BlockSpec((tm, d), lambda i: (i, 0)),
            pl.BlockSpec((1, 6, d), lambda i: (i // (seq // tm), 0, 0)),
            pl.BlockSpec((d, LANES), lambda i: (0, 0)),
            pl.BlockSpec((1, LANES), lambda i: (0, 0)),
        ],
        out_specs=[pl.BlockSpec((tm, d), lambda i: (i, 0)),
                   pl.BlockSpec((tm * ROW_TILES, LANES), lambda i: (i, 0)),
                   pl.BlockSpec((tm, LANES), lambda i: (i, 0))],
        out_shape=[jax.ShapeDtypeStruct((t, d), F32),
                   jax.ShapeDtypeStruct((t * ROW_TILES, LANES), F32),
                   jax.ShapeDtypeStruct((t, LANES), F32)],
        compiler_params=_cp(("arbitrary",), 48),
        name="out_proj",
    )(*mixed, w_out.astype(BF16), x2, mod, rw, rb)


def _route_kernel(lg_ref, idx_ref, gate_ref, cnt_ref, run_ref):
    @pl.when(pl.program_id(0) == 0)
    def _():
        run_ref[...] = jnp.zeros_like(run_ref)

    tm = lg_ref.shape[0]
    lane = lax.broadcasted_iota(jnp.int32, (tm, LANES), 1)
    neg = -jnp.inf
    l = jnp.where(lane < N_EXPERTS, lg_ref[...], neg)
    vals, hots = [], []
    for _ in range(TOP_K):
        m = jnp.max(l, axis=-1, keepdims=True)
        first = jnp.min(jnp.where(l == m, lane, LANES), axis=-1, keepdims=True)
        hot = lane == first
        vals.append(m)
        hots.append(hot)
        l = jnp.where(hot, neg, l)
    sel = hots[0] | hots[1] | hots[2] | hots[3]
    ex = [jnp.exp(v - vals[0]) for v in vals]
    denom = ex[0] + ex[1] + ex[2] + ex[3]
    r_i = lax.broadcasted_iota(jnp.int32, (tm, tm), 0)
    c_i = lax.broadcasted_iota(jnp.int32, (tm, tm), 1)
    before = (r_i > c_i).astype(BF16)
    sel_b = sel.astype(BF16)
    run = run_ref[0:1, :]
    cum = _dot(before, sel_b) + run
    idx_out = jnp.zeros((tm, LANES), jnp.int32)
    gate_out = jnp.zeros((tm, LANES), F32)
    for k in range(TOP_K):
        e_k = jnp.min(jnp.where(hots[k], lane, LANES), axis=-1, keepdims=True)
        rank_k = jnp.sum(jnp.where(hots[k], cum, 0.0), axis=-1, keepdims=True).astype(jnp.int32)
        idx_out = jnp.where(lane == k, e_k, idx_out)
        idx_out = jnp.where(lane == TOP_K + k, rank_k, idx_out)
        gate_out = jnp.where(lane == k, ex[k] / denom, gate_out)
    idx_ref[...] = idx_out
    gate_ref[...] = gate_out
    total = run + jnp.sum(sel.astype(F32), axis=0, keepdims=True)
    run_ref[...] = jnp.broadcast_to(total, run_ref.shape)
    cnt_ref[...] = jnp.broadcast_to(total, cnt_ref.shape)


def _route(logits):
    t = logits.shape[0]
    tm = min(ROUTE_ROWS, t)
    return pl.pallas_call(
        _route_kernel,
        grid=(t // tm,),
        in_specs=[pl.BlockSpec((tm, LANES), lambda i: (i, 0))],
        out_specs=[pl.BlockSpec((tm, LANES), lambda i: (i, 0)),
                   pl.BlockSpec((tm, LANES), lambda i: (i, 0)),
                   pl.BlockSpec((SUBLANES, LANES), lambda i: (0, 0))],
        out_shape=[jax.ShapeDtypeStruct((t, LANES), jnp.int32),
                   jax.ShapeDtypeStruct((t, LANES), F32),
                   jax.ShapeDtypeStruct((SUBLANES, LANES), F32)],
        scratch_shapes=[pltpu.VMEM((SUBLANES, LANES), F32)],
        compiler_params=_cp(("arbitrary",), 32),
        name="route",
    )(logits)


def _dispatch_kernel(h_ref, dest_ref, xs_in_ref, xs_ref, sem):
    del xs_in_ref
    tm = h_ref.shape[0] // ROW_TILES

    def row_copy(r, d):
        return pltpu.make_async_copy(
            h_ref.at[pl.ds(pl.multiple_of(r * ROW_TILES, ROW_TILES), ROW_TILES), :],
            xs_ref.at[pl.ds(pl.multiple_of(d * ROW_TILES, ROW_TILES), ROW_TILES), :], sem)

    def issue(r, carry):
        for k in range(TOP_K):
            row_copy(r, dest_ref[r * TOP_K + k]).start()
        return carry

    lax.fori_loop(0, tm, issue, 0)
    for _ in range(TOP_K):
        pltpu.make_async_copy(h_ref, xs_ref.at[pl.ds(0, tm * ROW_TILES), :], sem).wait()


def _dispatch(h3, dest_flat, n_rows):
    t = h3.shape[0] // ROW_TILES
    tm = min(DISPATCH_ROWS, t)
    xs0 = jnp.zeros((n_rows * ROW_TILES, LANES), F32)
    return pl.pallas_call(
        _dispatch_kernel,
        grid=(t // tm,),
        in_specs=[pl.BlockSpec((tm * ROW_TILES, LANES), lambda i: (i, 0)),
                  pl.BlockSpec((tm * TOP_K,), lambda i: (i,), memory_space=pltpu.SMEM),
                  pl.BlockSpec(memory_space=pl.ANY)],
        out_specs=pl.BlockSpec(memory_space=pl.ANY),
        out_shape=jax.ShapeDtypeStruct((n_rows * ROW_TILES, LANES), F32),
        scratch_shapes=[pltpu.SemaphoreType.DMA(())],
        input_output_aliases={2: 0},
        compiler_params=_cp(("arbitrary",), 32),
        name="dispatch",
    )(h3, dest_flat, xs0)


def _expert_kernel(be_ref, nu_ref, xs_ref, wi_ref, bi_ref, wo_ref, bo_ref, y_ref, wi_b, wo_b):
    i = pl.program_id(0)
    tb = xs_ref.shape[0] // ROW_TILES
    prev = be_ref[jnp.maximum(i - 1, 0)]
    fresh = jnp.logical_or(i == 0, be_ref[i] != prev)

    @pl.when(jnp.logical_and(fresh, i < nu_ref[0]))
    def _():
        wi_b[...] = wi_ref[0].astype(BF16)
        wo_b[...] = wo_ref[0].astype(BF16)

    @pl.when(i < nu_ref[0])
    def _():
        x = jnp.concatenate(
            [xs_ref[pl.ds(c, tb, stride=ROW_TILES), :] for c in range(ROW_TILES)],
            axis=1).astype(BF16)
        y = jnp.zeros((tb, D_MODEL), F32) + bo_ref[0]
        half = 512
        for j in range(D_FF // half):
            a, b = j * half, (j + 1) * half
            glu = _dot(x, wi_b[:, a:b]) + bi_ref[0, :, a:b]
            lin = _dot(x, wi_b[:, D_FF + a:D_FF + b]) + bi_ref[0, :, D_FF + a:D_FF + b]
            glu = jnp.minimum(glu, SWIGLU_LIMIT)
            lin = jnp.clip(lin, -SWIGLU_LIMIT, SWIGLU_LIMIT)
            act = glu * jax.nn.sigmoid(SWIGLU_ALPHA * glu) * (lin + 1.0)
            y = y + _dot(act.astype(BF16), wo_b[a:b, :])
        for c in range(ROW_TILES):
            y_ref[pl.ds(c, tb, stride=ROW_TILES), :] = y[:, c * LANES:(c + 1) * LANES]

    @pl.when(i >= nu_ref[0])
    def _():
        y_ref[...] = jnp.zeros_like(y_ref)


def _experts(xs, block_e, n_used, w_in, b_in, w_out, b_out):
    tb = EXPERT_ROWS
    n_rows = xs.shape[0] // ROW_TILES
    nb = n_rows // tb
    ne, d, f2 = w_in.shape

    def row_map(i, be, nu):
        return (jnp.minimum(i, nu[0] - 1), 0)

    grid_spec = pltpu.PrefetchScalarGridSpec(
        num_scalar_prefetch=2,
        grid=(nb,),
        in_specs=[
            pl.BlockSpec((tb * ROW_TILES, LANES), row_map),
            pl.BlockSpec((1, d, f2), lambda i, be, nu: (be[i], 0, 0)),
            pl.BlockSpec((1, 1, f2), lambda i, be, nu: (be[i], 0, 0)),
            pl.BlockSpec((1, D_FF, d), lambda i, be, nu: (be[i], 0, 0)),
            pl.BlockSpec((1, 1, d), lambda i, be, nu: (be[i], 0, 0)),
        ],
        out_specs=pl.BlockSpec((tb * ROW_TILES, LANES), lambda i, be, nu: (i, 0)),
        scratch_shapes=[pltpu.VMEM((d, f2), BF16), pltpu.VMEM((D_FF, d), BF16)],
    )
    return pl.pallas_call(
        _expert_kernel,
        grid_spec=grid_spec,
        out_shape=jax.ShapeDtypeStruct((n_rows * ROW_TILES, LANES), F32),
        compiler_params=_cp(("arbitrary",), 56),
        name="experts",
    )(block_e, n_used, xs, w_in, b_in.reshape(ne, 1, f2), w_out, b_out.reshape(ne, 1, d))


def _combine_kernel(final, dest_ref, gate_ref, x_ref, mod_ref, fg_ref, yb_ref, o_ref, g_buf, sem):
    tm = x_ref.shape[0]

    def row_copy(slot, d):
        return pltpu.make_async_copy(
            yb_ref.at[pl.ds(pl.multiple_of(d * ROW_TILES, ROW_TILES), ROW_TILES), :],
            g_buf.at[pl.ds(pl.multiple_of(slot * ROW_TILES, ROW_TILES), ROW_TILES), :], sem)

    def issue(r, carry):
        for k in range(TOP_K):
            row_copy(k * tm + r, dest_ref[r * TOP_K + k]).start()
        return carry

    lax.fori_loop(0, tm, issue, 0)
    pltpu.make_async_copy(yb_ref.at[pl.ds(0, TOP_K * tm * ROW_TILES), :], g_buf, sem).wait()

    gates = gate_ref[...]
    gate2 = mod_ref[0, 5:6, :]
    for c in range(ROW_TILES):
        y = jnp.zeros((tm, LANES), F32)
        for k in range(TOP_K):
            rows = g_buf[pl.ds(k * tm * ROW_TILES + c, tm, stride=ROW_TILES), :]
            y = y + gates[:, k:k + 1] * rows
        cs = slice(c * LANES, (c + 1) * LANES)
        o_ref[:, cs] = x_ref[:, cs] + gate2[:, cs] * y
    if final:
        o_ref[...] = _rms_rows(o_ref[...]) * fg_ref[...]


def _combine(yb, dest_flat, gates, x2, mod, final_g, final, seq):
    t, d = x2.shape
    tm = min(COMBINE_ROWS, seq)
    return pl.pallas_call(
        functools.partial(_combine_kernel, final),
        grid=(t // tm,),
        in_specs=[
            pl.BlockSpec((tm * TOP_K,), lambda i: (i,), memory_space=pltpu.SMEM),
            pl.BlockSpec((tm, LANES), lambda i: (i, 0)),
            pl.BlockSpec((tm, d), lambda i: (i, 0)),
            pl.BlockSpec((1, 6, d), lambda i: (i // (seq // tm), 0, 0)),
            pl.BlockSpec((1, d), lambda i: (0, 0)),
            pl.BlockSpec(memory_space=pl.ANY),
        ],
        out_specs=pl.BlockSpec((tm, d), lambda i: (i, 0)),
        out_shape=jax.ShapeDtypeStruct((t, d), F32),
        scratch_shapes=[pltpu.VMEM((TOP_K * tm * ROW_TILES, LANES), F32),
                        pltpu.SemaphoreType.DMA(())],
        compiler_params=_cp(("arbitrary",), 40),
        name="combine",
    )(dest_flat, gates, x2, mod, final_g.reshape(1, d), yb)


def _moe(h3, logits, x2, mod, w_in, b_in, w_out, b_out, final_g, final, seq):
    t = x2.shape[0]
    tb = EXPERT_ROWS
    idx, gates, cnt = _route(logits)
    counts = cnt[0, :N_EXPERTS].astype(jnp.int32)
    nblk = (counts + tb - 1) // tb
    blk_end = jnp.cumsum(nblk)
    pad_start = (blk_end - nblk) * tb
    dest = pad_start[idx[:, :TOP_K]] + idx[:, TOP_K:2 * TOP_K]
    dest_flat = dest.reshape(t * TOP_K)
    n_blocks = (t * TOP_K) // tb + N_EXPERTS
    n_used = blk_end[-1:]
    last_e = jnp.max(jnp.where(nblk > 0, jnp.arange(N_EXPERTS, dtype=jnp.int32), 0))
    block_e = jnp.minimum(
        jnp.searchsorted(blk_end, jnp.arange(n_blocks, dtype=jnp.int32), side='right'),
        last_e).astype(jnp.int32)
    xs = _dispatch(h3, dest_flat, n_blocks * tb)
    yb = _experts(xs, block_e, n_used.astype(jnp.int32), w_in, b_in, w_out, b_out)
    return _combine(yb, dest_flat, gates, x2, mod, final_g, final, seq)


def kernel(x, c, positions, w_ada, b_ada, even_w_in, gla_w_gate, gla_b_gate, gla_norm_g,
           diff_lam_q1, diff_lam_k1, diff_lam_q2, diff_lam_k2, diff_norm_g, even_w_out,
           odd_w_in, sgu_ln_g, sgu_ln_b, sgu_w, sgu_b, odd_w_out,
           router_w, router_b, expert_w_in, expert_b_in, expert_w_out, expert_b_out,
           final_norm_g):
    bsz, seq, d = x.shape
    depth = w_ada.shape[0]
    t = bsz * seq
    mods = _modulation(c, w_ada, b_ada).reshape(depth, bsz, 6, d)
    cos, sin = _rope_tables(positions)
    x2 = x.reshape(t, d)
    for layer in range(depth):
        mod = mods[layer]
        j = layer // 2
        if layer % 2 == 0:
            z, f = _proj_even(x2, mod, even_w_in[j], gla_w_gate[j], gla_b_gate[j], cos, sin, seq)
            o_gla = _gla(z, f, gla_norm_g[j], bsz, seq)
            lam_init = 0.8 - 0.6 * math.exp(-0.3 * layer)
            o_diff = _diff_attn(z, diff_lam_q1[j], diff_lam_k1[j], diff_lam_q2[j], diff_lam_k2[j],
                                diff_norm_g[j], lam_init, bsz, seq)
            mixed, w_out = (o_gla, o_diff), even_w_out[j]
        else:
            z, f = _proj_odd(x2, mod, odd_w_in[j], sgu_ln_g[j], sgu_ln_b[j], cos, sin, seq)
            mixed, w_out = (_sgu_ret(z, f, sgu_w[j], sgu_b[j], bsz, seq),), odd_w_out[j]
        x2, h3, logits = _out_proj(mixed, w_out, x2, mod, router_w[layer], router_b[layer], seq)
        x2 = _moe(h3, logits, x2, mod, expert_w_in[layer], expert_b_in[layer],
                  expert_w_out[layer], expert_b_out[layer], final_norm_g,
                  layer == depth - 1, seq)
    return x2.reshape(bsz, seq, d)
```

```python
import functools
import math

import jax
import jax.numpy as jnp
from jax import lax
from jax.experimental import pallas as pl
from jax.experimental.pallas import tpu as pltpu

F32 = jnp.float32
BF16 = jnp.bfloat16

D_MODEL = 1024
EPS = 1e-6
ROPE_THETA = 10000.0
ROPE_HALF = 32

GLA_HEADS = 4
GLA_DK = 64
GLA_DV = 128
GLA_RANK = 16
GLA_CHUNK = 64
GLA_GATE_NORMALIZER = 16.0
GLA_QK_W = GLA_HEADS * GLA_DK
GLA_V_W = GLA_HEADS * GLA_DV

DIFF_HEADS = 4
DIFF_D = 64
DIFF_DV = 128
DIFF_QK_W = DIFF_HEADS * 2 * DIFF_D
DIFF_V_W = DIFF_HEADS * DIFF_DV

SGU_GROUPS = 4
SGU_CH = 128
SGU_CHUNK = 128
SGU_W = SGU_GROUPS * SGU_CH

RET_HEADS = 4
RET_DK = 64
RET_DV = 128
RET_CHUNK = 128
RET_QK_W = RET_HEADS * RET_DK
RET_V_W = RET_HEADS * RET_DV

N_EXPERTS = 32
TOP_K = 4
D_FF = D_MODEL
SWIGLU_ALPHA = 1.702
SWIGLU_LIMIT = 7.0

LANES = 128
SUBLANES = 8
ROW_TILES = D_MODEL // LANES

PROJ_ROWS = 512
GLA_ROWS = 256
ATT_Q_ROWS = 256
ATT_HEADS_PER_STEP = 2
ROUTE_ROWS = 512
DISPATCH_ROWS = 256
EXPERT_ROWS = 256
COMBINE_ROWS = 256
MOD_COLS = 1536

MIB = 1024 * 1024


def _cp(semantics, vmem_mib):
    return pltpu.CompilerParams(dimension_semantics=semantics, vmem_limit_bytes=vmem_mib * MIB)


def _dot(a, b):
    return jnp.dot(a, b, preferred_element_type=F32)


def _dot_nt(a, b):
    return lax.dot_general(a, b, (((1,), (1,)), ((), ())), preferred_element_type=F32)


def _dot_tn(a, b):
    return lax.dot_general(a, b, (((0,), (0,)), ((), ())), preferred_element_type=F32)


def _rms_rows(x):
    return x * lax.rsqrt(jnp.mean(x * x, axis=-1, keepdims=True) + EPS)


def _silu(x):
    return x * jax.nn.sigmoid(x)


def _rope_chunk(v, cos, sin, lo_half):
    rot = jnp.where(lo_half, -pltpu.roll(v, 96, 1), pltpu.roll(v, 32, 1))
    return v * cos + rot * sin


def _mod_kernel(c_ref, w_ref, b_ref, o_ref):
    c = c_ref[...]
    ca = _silu(c).astype(BF16)
    o_ref[0] = _dot(ca, w_ref[0].astype(BF16)) + b_ref[0]


def _modulation(c, w_ada, b_ada):
    depth, d, n = w_ada.shape
    bsz = c.shape[0]
    return pl.pallas_call(
        _mod_kernel,
        grid=(depth, n // MOD_COLS),
        in_specs=[
            pl.BlockSpec((bsz, d), lambda l, j: (0, 0)),
            pl.BlockSpec((1, d, MOD_COLS), lambda l, j: (l, 0, j)),
            pl.BlockSpec((1, 1, MOD_COLS), lambda l, j: (l, 0, j)),
        ],
        out_specs=pl.BlockSpec((1, bsz, MOD_COLS), lambda l, j: (l, 0, j)),
        out_shape=jax.ShapeDtypeStruct((depth, bsz, n), F32),
        compiler_params=_cp(("arbitrary", "arbitrary"), 40),
        name="adaln_mod",
    )(c, w_ada, b_ada.reshape(depth, 1, n))


def _rope_table_kernel(p_ref, f_ref, c_ref, s_ref):
    ang = p_ref[...].astype(F32) * f_ref[...]
    c_ref[...] = jnp.cos(ang)
    s_ref[...] = jnp.sin(ang)


def _rope_tables(positions):
    t = positions.size
    per_row = LANES // ROPE_HALF
    rows = t // per_row
    pos_d = jnp.repeat(positions.reshape(rows, per_row), ROPE_HALF, axis=1)
    inv_freq = ROPE_THETA ** (-jnp.arange(ROPE_HALF, dtype=F32) / ROPE_HALF)
    freq_d = jnp.tile(inv_freq, per_row).reshape(1, LANES)
    tr = min(512, rows)
    cos_d, sin_d = pl.pallas_call(
        _rope_table_kernel,
        grid=(rows // tr,),
        in_specs=[pl.BlockSpec((tr, LANES), lambda i: (i, 0)),
                  pl.BlockSpec((1, LANES), lambda i: (0, 0))],
        out_specs=[pl.BlockSpec((tr, LANES), lambda i: (i, 0))] * 2,
        out_shape=[jax.ShapeDtypeStruct((rows, LANES), F32)] * 2,
        compiler_params=_cp(("arbitrary",), 32),
        name="rope_tables",
    )(pos_d, freq_d)
    cos = jnp.tile(cos_d.reshape(t, ROPE_HALF), (1, per_row))
    sin = jnp.tile(sin_d.reshape(t, ROPE_HALF), (1, per_row))
    return cos, sin


def _modulated_rms(x_ref, mod_ref, which):
    x = x_ref[...]
    shift = mod_ref[0, 3 * which:3 * which + 1, :]
    scale = mod_ref[0, 3 * which + 1:3 * which + 2, :]
    return _rms_rows(x) * (1.0 + scale) + shift


EVEN_Z = (GLA_V_W, GLA_V_W, DIFF_QK_W, DIFF_QK_W, DIFF_V_W)
EVEN_Z_W = sum(EVEN_Z)
EVEN_F_W = 3 * GLA_QK_W
EVEN_W_COLS = EVEN_Z_W + 2 * GLA_QK_W + LANES


def _proj_even_kernel(x_ref, mod_ref, w_ref, wg_ref, bg_ref, cos_ref, sin_ref, z_ref, f_ref):
    hb = _modulated_rms(x_ref, mod_ref, 0).astype(BF16)

    def sec(a, b):
        return _dot(hb, w_ref[:, a:b])

    for a, b in ((0, 512), (512, 1024), (2048, 2560)):
        z_ref[:, a:b] = sec(a, b).astype(BF16)
    cos = cos_ref[...]
    sin = sin_ref[...]
    lo_half = (lax.broadcasted_iota(jnp.int32, cos.shape, 1) % 64) < ROPE_HALF
    q_scale = DIFF_D ** -0.5
    for a, scl in ((1024, q_scale), (1536, 1.0)):
        for c in range(DIFF_QK_W // LANES):
            v = sec(a + c * LANES, a + (c + 1) * LANES)
            z_ref[:, a + c * LANES:a + (c + 1) * LANES] = (
                _rope_chunk(v, cos, sin, lo_half) * scl).astype(BF16)
    f_ref[:, 0:256] = sec(2560, 2816) * (GLA_DK ** -0.5)
    f_ref[:, 256:512] = sec(2816, 3072)
    gr = sec(3072, 3200).astype(BF16)
    pre = _dot(gr, wg_ref[...]) + bg_ref[...]
    log_sig = jnp.minimum(pre, 0.0) - jnp.log1p(jnp.exp(-jnp.abs(pre)))
    f_ref[:, 512:768] = log_sig / GLA_GATE_NORMALIZER


def _proj_even(x2, mod, w_in, w_gate, b_gate, cos, sin, seq):
    t, d = x2.shape
    tm = min(PROJ_ROWS, seq)
    gq, gk, gv, gr, gg, dq, dk, dv = jnp.split(
        w_in, [256, 512, 1024, 1040, 1552, 2064, 2576], axis=1)
    gr_pad = jnp.pad(gr, ((0, 0), (0, LANES - GLA_RANK)))
    w_cat = jnp.concatenate([gv, gg, dq, dk, dv, gq, gk, gr_pad], axis=1).astype(BF16)
    wg_pad = jnp.pad(w_gate, ((0, LANES - GLA_RANK), (0, 0))).astype(BF16)
    return pl.pallas_call(
        _proj_even_kernel,
        grid=(t // tm,),
        in_specs=[
            pl.BlockSpec((tm, d), lambda i: (i, 0)),
            pl.BlockSpec((1, 6, d), lambda i: (i // (seq // tm), 0, 0)),
            pl.BlockSpec((d, EVEN_W_COLS), lambda i: (0, 0)),
            pl.BlockSpec((LANES, GLA_QK_W), lambda i: (0, 0)),
            pl.BlockSpec((1, GLA_QK_W), lambda i: (0, 0)),
            pl.BlockSpec((tm, LANES), lambda i: (i, 0)),
            pl.BlockSpec((tm, LANES), lambda i: (i, 0)),
        ],
        out_specs=[pl.BlockSpec((tm, EVEN_Z_W), lambda i: (i, 0)),
                   pl.BlockSpec((tm, EVEN_F_W), lambda i: (i, 0))],
        out_shape=[jax.ShapeDtypeStruct((t, EVEN_Z_W), BF16),
                   jax.ShapeDtypeStruct((t, EVEN_F_W), F32)],
        compiler_params=_cp(("arbitrary",), 48),
        name="proj_even",
    )(x2, mod, w_cat, wg_pad, b_gate.reshape(1, GLA_QK_W), cos, sin)


ODD_Z_W = 2 * SGU_W + 2 * RET_V_W
ODD_F_W = 2 * RET_QK_W
ODD_W_COLS = ODD_Z_W + ODD_F_W


def _gelu_exact(x):
    return 0.5 * x * (1.0 + lax.erf(x * (2.0 ** -0.5)))


def _proj_odd_kernel(x_ref, mod_ref, w_ref, lng_ref, lnb_ref, cos_ref, sin_ref, z_ref, f_ref):
    hb = _modulated_rms(x_ref, mod_ref, 0).astype(BF16)

    def sec(a, b):
        return _dot(hb, w_ref[:, a:b])

    z_ref[:, 0:512] = _gelu_exact(sec(0, 512)).astype(BF16)
    sv = _gelu_exact(sec(512, 1024))
    mu = jnp.mean(sv, axis=-1, keepdims=True)
    cen = sv - mu
    var = jnp.mean(cen * cen, axis=-1, keepdims=True)
    z_ref[:, 512:1024] = (cen * lax.rsqrt(var + EPS) * lng_ref[...] + lnb_ref[...]).astype(BF16)
    for a, b in ((1024, 1536), (1536, 2048)):
        z_ref[:, a:b] = sec(a, b).astype(BF16)
    cos = cos_ref[...]
    sin = sin_ref[...]
    lo_half = (lax.broadcasted_iota(jnp.int32, cos.shape, 1) % 64) < ROPE_HALF
    k_scale = RET_DK ** -0.5
    for a, scl in ((0, 1.0), (256, k_scale)):
        for c in range(RET_QK_W // LANES):
            v = sec(ODD_Z_W + a + c * LANES, ODD_Z_W + a + (c + 1) * LANES)
            f_ref[:, a + c * LANES:a + (c + 1) * LANES] = _rope_chunk(v, cos, sin, lo_half) * scl


def _proj_odd(x2, mod, w_in, ln_g, ln_b, cos, sin, seq):
    t, d = x2.shape
    tm = min(PROJ_ROWS, seq)
    su, sv, rq, rk, rv, rg = jnp.split(w_in, [512, 1024, 1280, 1536, 2048], axis=1)
    w_cat = jnp.concatenate([su, sv, rv, rg, rq, rk], axis=1).astype(BF16)
    return pl.pallas_call(
        _proj_odd_kernel,
        grid=(t // tm,),
        in_specs=[
            pl.BlockSpec((tm, d), lambda i: (i, 0)),
            pl.BlockSpec((1, 6, d), lambda i: (i // (seq // tm), 0, 0)),
            pl.BlockSpec((d, ODD_W_COLS), lambda i: (0, 0)),
            pl.BlockSpec((1, SGU_W), lambda i: (0, 0)),
            pl.BlockSpec((1, SGU_W), lambda i: (0, 0)),
            pl.BlockSpec((tm, LANES), lambda i: (i, 0)),
            pl.BlockSpec((tm, LANES), lambda i: (i, 0)),
        ],
        out_specs=[pl.BlockSpec((tm, ODD_Z_W), lambda i: (i, 0)),
                   pl.BlockSpec((tm, ODD_F_W), lambda i: (i, 0))],
        out_shape=[jax.ShapeDtypeStruct((t, ODD_Z_W), BF16),
                   jax.ShapeDtypeStruct((t, ODD_F_W), F32)],
        compiler_params=_cp(("arbitrary",), 48),
        name="proj_odd",
    )(x2, mod, w_cat, ln_g.reshape(1, SGU_W), ln_b.reshape(1, SGU_W), cos, sin)


def _gla_kernel(q_ref, k_ref, la_ref, v_ref, gg_ref, g_ref, o_ref, st_ref):
    @pl.when(pl.program_id(1) == 0)
    def _():
        st_ref[...] = jnp.zeros_like(st_ref)

    c = GLA_CHUNK
    rows = q_ref.shape[0]
    r_i = lax.broadcasted_iota(jnp.int32, (c, c), 0)
    c_i = lax.broadcasted_iota(jnp.int32, (c, c), 1)
    tril = (r_i >= c_i).astype(BF16)
    r2 = lax.broadcasted_iota(jnp.int32, (2 * c, c), 0) % c
    c2 = lax.broadcasted_iota(jnp.int32, (2 * c, c), 1)
    causal2 = r2 >= c2
    lane = lax.broadcasted_iota(jnp.int32, (c, LANES), 1)
    lo = lane < GLA_DK
    lane_s = lax.broadcasted_iota(jnp.int32, (LANES, LANES), 1) < GLA_DK
    g_row = g_ref[...]
    states = [st_ref[0], st_ref[1]]
    for j in range(rows // c):
        rs = slice(j * c, (j + 1) * c)
        la = la_ref[rs, :]
        la_hi = la.astype(BF16)
        la_lo = (la - la_hi.astype(F32)).astype(BF16)
        b = _dot(tril, la_hi) + _dot(tril, la_lo)
        b_last = b[c - 1:c, :]
        qd = q_ref[rs, :] * jnp.exp(b)
        kk = k_ref[rs, :]
        ki = kk * jnp.exp(-b)
        kd = kk * jnp.exp(b_last - b)
        dec = jnp.exp(b_last)
        for p in range(GLA_HEADS // 2):
            ls = slice(p * LANES, (p + 1) * LANES)
            qd_p = qd[:, ls]
            qm = (jnp.where(lo, qd_p, 0.0).astype(BF16), jnp.where(lo, 0.0, qd_p).astype(BF16))
            s2 = _dot_nt(jnp.concatenate(qm, axis=0), ki[:, ls].astype(BF16))
            s2 = jnp.where(causal2, s2, 0.0).astype(BF16)
            kd_p = kd[:, ls].astype(BF16)
            st_b = states[p].astype(BF16)
            new = []
            for hh in range(2):
                h = 2 * p + hh
                hs = slice(h * GLA_DV, (h + 1) * GLA_DV)
                v_h = v_ref[rs, hs]
                o = _dot(s2[hh * c:(hh + 1) * c, :], v_h) + _dot_nt(qm[hh], st_b)
                o = _rms_rows(o) * g_row
                gate = _silu(gg_ref[rs, hs].astype(F32))
                o_ref[rs, hs] = (o * gate).astype(BF16)
                new.append(_dot_tn(v_h, kd_p))
            states[p] = dec[:, ls] * states[p] + jnp.where(lane_s, new[0], new[1])
    st_ref[0] = states[0]
    st_ref[1] = states[1]


def _gla(z, f, norm_g, bsz, seq):
    t = z.shape[0]
    tc = min(GLA_ROWS, seq)
    nc = seq // tc
    return pl.pallas_call(
        _gla_kernel,
        grid=(bsz, nc),
        in_specs=[
            pl.BlockSpec((tc, GLA_QK_W), lambda b, i: (b * nc + i, 0)),
            pl.BlockSpec((tc, GLA_QK_W), lambda b, i: (b * nc + i, 1)),
            pl.BlockSpec((tc, GLA_QK_W), lambda b, i: (b * nc + i, 2)),
            pl.BlockSpec((tc, GLA_V_W), lambda b, i: (b * nc + i, 0)),
            pl.BlockSpec((tc, GLA_V_W), lambda b, i: (b * nc + i, 1)),
            pl.BlockSpec((1, GLA_DV), lambda b, i: (0, 0)),
        ],
        out_specs=pl.BlockSpec((tc, GLA_V_W), lambda b, i: (b * nc + i, 0)),
        out_shape=jax.ShapeDtypeStruct((t, GLA_V_W), BF16),
        scratch_shapes=[pltpu.VMEM((2, LANES, LANES), F32)],
        compiler_params=_cp(("arbitrary", "arbitrary"), 32),
        name="gla",
    )(f, f, f, z, z, norm_g.reshape(1, GLA_DV))


def _diff_attn_kernel(lam_init, q_ref, k_ref, v_ref, lq1_ref, lk1_ref, lq2_ref, lk2_ref, g_ref,
                      o_ref, vt_ref):
    tq = q_ref.shape[0]
    seq = k_ref.shape[0]
    n_heads = q_ref.shape[1] // LANES
    qi = pl.program_id(2)

    @pl.when(qi == 0)
    def _():
        for hh in range(n_heads):
            for cb in range(seq // tq):
                blk = v_ref[cb * tq:(cb + 1) * tq, hh * LANES:(hh + 1) * LANES]
                vt_ref[hh, :, cb * tq:(cb + 1) * tq] = blk.astype(F32).T.astype(BF16)

    lo = lax.broadcasted_iota(jnp.int32, (tq, LANES), 1) < DIFF_D
    qqs = []
    for hh in range(n_heads):
        q = q_ref[:, hh * LANES:(hh + 1) * LANES]
        zero = jnp.zeros_like(q)
        qqs.append(jnp.concatenate([jnp.where(lo, q, zero), jnp.where(lo, zero, q)], axis=0))

    def scores(j, hh):
        start = pl.multiple_of(j * tq, tq)
        kj = k_ref[pl.ds(start, tq), hh * LANES:(hh + 1) * LANES]
        return _dot_nt(kj, qqs[hh])

    def update(j, state, s, hh, masked):
        m, l, acc = state
        start = pl.multiple_of(j * tq, tq)
        vtj = vt_ref[hh, :, pl.ds(start, tq)]
        if masked:
            kv = lax.broadcasted_iota(jnp.int32, s.shape, 0)
            qq_pos = lax.broadcasted_iota(jnp.int32, s.shape, 1) % tq
            s = jnp.where(kv <= qq_pos, s, -jnp.inf)
        m_new = jnp.maximum(m, jnp.max(s, axis=0, keepdims=True))
        alpha = jnp.exp(m - m_new)
        p = jnp.exp(s - m_new)
        l = alpha * l + jnp.sum(p, axis=0, keepdims=True)
        acc = alpha * acc + _dot(vtj, p.astype(BF16))
        return m_new, l, acc

    def body(j, carries):
        nxt = [scores(j + 1, hh) for hh in range(n_heads)]
        return tuple((update(j, carries[hh][0], carries[hh][1], hh, False), nxt[hh])
                     for hh in range(n_heads))

    init = tuple(((jnp.full((1, 2 * tq), -jnp.inf, F32), jnp.zeros((1, 2 * tq), F32),
                   jnp.zeros((DIFF_DV, 2 * tq), F32)), scores(0, hh)) for hh in range(n_heads))
    carries = lax.fori_loop(0, qi, body, init)
    carries = tuple(update(qi, carries[hh][0], carries[hh][1], hh, True)
                    for hh in range(n_heads))
    lam = (jnp.exp(jnp.sum(lq1_ref[...] * lk1_ref[...], axis=-1, keepdims=True))
           - jnp.exp(jnp.sum(lq2_ref[...] * lk2_ref[...], axis=-1, keepdims=True)) + lam_init)
    for hh in range(n_heads):
        _, l, acc = carries[hh]
        o12 = acc / l
        o = o12[:, :tq] - lam * o12[:, tq:]
        o = o * lax.rsqrt(jnp.mean(o * o, axis=0, keepdims=True) + EPS)
        o = o * g_ref[...] * (1.0 - lam_init)
        o_ref[:, hh * DIFF_DV:(hh + 1) * DIFF_DV] = o.T.astype(BF16)


def _diff_attn(z, lq1, lk1, lq2, lk2, norm_g, lam_init, bsz, seq):
    t = z.shape[0]
    tq = min(ATT_Q_ROWS, seq)
    nq = seq // tq
    hw = ATT_HEADS_PER_STEP * LANES
    qb, kb, vb = 1024 // hw, 1536 // hw, 2048 // hw
    small = pl.BlockSpec((1, DIFF_D), lambda b, h, i: (0, 0))
    return pl.pallas_call(
        functools.partial(_diff_attn_kernel, lam_init),
        grid=(bsz, DIFF_HEADS // ATT_HEADS_PER_STEP, nq),
        in_specs=[
            pl.BlockSpec((tq, hw), lambda b, h, i: (b * nq + i, qb + h)),
            pl.BlockSpec((seq, hw), lambda b, h, i: (b, kb + h)),
            pl.BlockSpec((seq, hw), lambda b, h, i: (b, vb + h)),
            small, small, small, small,
            pl.BlockSpec((DIFF_DV, 1), lambda b, h, i: (0, 0)),
        ],
        out_specs=pl.BlockSpec((tq, hw), lambda b, h, i: (b * nq + i, h)),
        out_shape=jax.ShapeDtypeStruct((t, DIFF_V_W), BF16),
        scratch_shapes=[pltpu.VMEM((ATT_HEADS_PER_STEP, DIFF_DV, seq), BF16)],
        compiler_params=_cp(("arbitrary", "arbitrary", "arbitrary"), 32),
        name="diff_attn",
    )(z, z, z, lq1.reshape(1, DIFF_D), lk1.reshape(1, DIFF_D), lq2.reshape(1, DIFF_D),
      lk2.reshape(1, DIFF_D), norm_g.reshape(DIFF_DV, 1))


def _sgu_ret_kernel(su_ref, sv_ref, rv_ref, rg_ref, q_ref, k_ref, ws_ref, bs_ref, o_ref, st_ref):
    @pl.when(pl.program_id(1) == 0)
    def _():
        st_ref[...] = jnp.zeros_like(st_ref)

    c = RET_CHUNK
    row = lax.broadcasted_iota(jnp.int32, (c, c), 0)
    col = lax.broadcasted_iota(jnp.int32, (c, c), 1)
    causal = row >= col
    for g in range(SGU_GROUPS):
        gs = slice(g * SGU_CH, (g + 1) * SGU_CH)
        w = jnp.where(causal, ws_ref[g], 0.0).astype(BF16)
        s = _dot(w, sv_ref[:, gs]) + bs_ref[g]
        o_ref[:, gs] = (su_ref[:, gs].astype(F32) * s).astype(BF16)
    log_g = [math.log(1.0 - 2.0 ** (-5.0 - h)) for h in range(RET_HEADS)]
    lo = col < RET_DK
    rel = (row - col).astype(F32)
    pos = row.astype(F32)
    for p in range(RET_HEADS // 2):
        ls = slice(p * LANES, (p + 1) * LANES)
        lg = jnp.where(lo, log_g[2 * p], log_g[2 * p + 1])
        q_p = q_ref[:, ls]
        k_p = k_ref[:, ls]
        qm = (jnp.where(lo, q_p, 0.0).astype(BF16), jnp.where(lo, 0.0, q_p).astype(BF16))
        s2 = _dot_nt(jnp.concatenate(qm, axis=0), k_p.astype(BF16))
        qd = q_p * jnp.exp(lg * (pos + 1.0))
        qdm = (jnp.where(lo, qd, 0.0).astype(BF16), jnp.where(lo, 0.0, qd).astype(BF16))
        kd = (k_p * jnp.exp(lg * (c - 1.0 - pos))).astype(BF16)
        st = st_ref[p]
        st_b = st.astype(BF16)
        new = []
        for hh in range(2):
            h = 2 * p + hh
            hs = slice(h * RET_DV, (h + 1) * RET_DV)
            decay = jnp.where(causal, jnp.exp(log_g[h] * jnp.maximum(rel, 0.0)), 0.0)
            s_h = (s2[hh * c:(hh + 1) * c, :] * decay).astype(BF16)
            v_h = rv_ref[:, hs]
            o = _dot(s_h, v_h) + _dot_nt(qdm[hh], st_b)
            gate = _silu(rg_ref[:, hs].astype(F32))
            o_ref[:, SGU_W + h * RET_DV:SGU_W + (h + 1) * RET_DV] = (
                _rms_rows(o) * gate).astype(BF16)
            new.append(_dot_tn(v_h, kd))
        st_ref[p] = jnp.exp(lg * float(c)) * st + jnp.where(lo, new[0], new[1])


def _sgu_ret(z, f, w_s, b_s, bsz, seq):
    t = z.shape[0]
    c = RET_CHUNK
    nc = seq // c
    return pl.pallas_call(
        _sgu_ret_kernel,
        grid=(bsz, nc),
        in_specs=[
            pl.BlockSpec((c, SGU_W), lambda b, i: (b * nc + i, 0)),
            pl.BlockSpec((c, SGU_W), lambda b, i: (b * nc + i, 1)),
            pl.BlockSpec((c, RET_V_W), lambda b, i: (b * nc + i, 2)),
            pl.BlockSpec((c, RET_V_W), lambda b, i: (b * nc + i, 3)),
            pl.BlockSpec((c, RET_QK_W), lambda b, i: (b * nc + i, 0)),
            pl.BlockSpec((c, RET_QK_W), lambda b, i: (b * nc + i, 1)),
            pl.BlockSpec((SGU_GROUPS, c, c), lambda b, i: (0, 0, 0)),
            pl.BlockSpec((SGU_GROUPS, c, 1), lambda b, i: (0, 0, 0)),
        ],
        out_specs=pl.BlockSpec((c, SGU_W + RET_V_W), lambda b, i: (b * nc + i, 0)),
        out_shape=jax.ShapeDtypeStruct((t, SGU_W + RET_V_W), BF16),
        scratch_shapes=[pltpu.VMEM((2, LANES, LANES), F32)],
        compiler_params=_cp(("arbitrary", "arbitrary"), 32),
        name="sgu_retention",
    )(z, z, z, z, f, f, w_s, b_s.reshape(SGU_GROUPS, c, 1))


def _out_proj_kernel(n_in, *refs):
    o_refs = refs[:n_in]
    w_ref, x_ref, mod_ref, rw_ref, rb_ref, xn_ref, h_ref, lg_ref = refs[n_in:]
    tm = x_ref.shape[0]
    k_each = D_MODEL // n_in
    y = _dot(o_refs[0][...], w_ref[0:k_each, :])
    for n in range(1, n_in):
        y = y + _dot(o_refs[n][...], w_ref[n * k_each:(n + 1) * k_each, :])
    gate1 = mod_ref[0, 2:3, :]
    xn = x_ref[...] + gate1 * y
    xn_ref[...] = xn
    h = _rms_rows(xn) * (1.0 + mod_ref[0, 4:5, :]) + mod_ref[0, 3:4, :]
    for c in range(ROW_TILES):
        h_ref[pl.ds(c, tm, stride=ROW_TILES), :] = h[:, c * LANES:(c + 1) * LANES]
    lg_ref[...] = _dot(h.astype(BF16), rw_ref[...]) + rb_ref[...]


def _out_proj(mixed, w_out, x2, mod, router_w, router_b, seq):
    t, d = x2.shape
    tm = min(PROJ_ROWS, seq)
    n_in = len(mixed)
    k_each = d // n_in
    rw = jnp.pad(router_w, ((0, 0), (0, LANES - N_EXPERTS))).astype(BF16)
    rb = jnp.pad(router_b, (0, LANES - N_EXPERTS)).reshape(1, LANES)
    return pl.pallas_call(
        functools.partial(_out_proj_kernel, n_in),
        grid=(t // tm,),
        in_specs=[pl.BlockSpec((tm, k_each), lambda i: (i, 0)) for _ in mixed] + [
            pl.BlockSpec((d, d), lambda i: (0, 0)),
            pl.BlockSpec((tm, d), lambda i: (i, 0)),
            pl.BlockSpec((1, 6, d), lambda i: (i // (seq // tm), 0, 0)),
            pl.BlockSpec((d, LANES), lambda i: (0, 0)),
            pl.BlockSpec((1, LANES), lambda i: (0, 0)),
        ],
        out_specs=[pl.BlockSpec((tm, d), lambda i: (i, 0)),
                   pl.BlockSpec((tm * ROW_TILES, LANES), lambda i: (i, 0)),
                   pl.BlockSpec((tm, LANES), lambda i: (i, 0))],
        out_shape=[jax.ShapeDtypeStruct((t, d), F32),
                   jax.ShapeDtypeStruct((t * ROW_TILES, LANES), F32),
                   jax.ShapeDtypeStruct((t, LANES), F32)],
        compiler_params=_cp(("arbitrary",), 48),
        name="out_proj",
    )(*mixed, w_out.astype(BF16), x2, mod, rw, rb)


def _route_kernel(lg_ref, idx_ref, gate_ref, cnt_ref, run_ref):
    @pl.when(pl.program_id(0) == 0)
    def _():
        run_ref[...] = jnp.zeros_like(run_ref)

    tm = lg_ref.shape[0]
    lane = lax.broadcasted_iota(jnp.int32, (tm, LANES), 1)
    neg = -jnp.inf
    l = jnp.where(lane < N_EXPERTS, lg_ref[...], neg)
    vals, hots = [], []
    for _ in range(TOP_K):
        m = jnp.max(l, axis=-1, keepdims=True)
        first = jnp.min(jnp.where(l == m, lane, LANES), axis=-1, keepdims=True)
        hot = lane == first
        vals.append(m)
        hots.append(hot)
        l = jnp.where(hot, neg, l)
    sel = hots[0] | hots[1] | hots[2] | hots[3]
    ex = [jnp.exp(v - vals[0]) for v in vals]
    denom = ex[0] + ex[1] + ex[2] + ex[3]
    r_i = lax.broadcasted_iota(jnp.int32, (tm, tm), 0)
    c_i = lax.broadcasted_iota(jnp.int32, (tm, tm), 1)
    before = (r_i > c_i).astype(BF16)
    sel_b = sel.astype(BF16)
    run = run_ref[0:1, :]
    cum = _dot(before, sel_b) + run
    idx_out = jnp.zeros((tm, LANES), jnp.int32)
    gate_out = jnp.zeros((tm, LANES), F32)
    for k in range(TOP_K):
        e_k = jnp.min(jnp.where(hots[k], lane, LANES), axis=-1, keepdims=True)
        rank_k = jnp.sum(jnp.where(hots[k], cum, 0.0), axis=-1, keepdims=True).astype(jnp.int32)
        idx_out = jnp.where(lane == k, e_k, idx_out)
        idx_out = jnp.where(lane == TOP_K + k, rank_k, idx_out)
        gate_out = jnp.where(lane == k, ex[k] / denom, gate_out)
    idx_ref[...] = idx_out
    gate_ref[...] = gate_out
    total = run + jnp.sum(sel.astype(F32), axis=0, keepdims=True)
    run_ref[...] = jnp.broadcast_to(total, run_ref.shape)
    cnt_ref[...] = jnp.broadcast_to(total, cnt_ref.shape)


def _route(logits):
    t = logits.shape[0]
    tm = min(ROUTE_ROWS, t)
    return pl.pallas_call(
        _route_kernel,
        grid=(t // tm,),
        in_specs=[pl.BlockSpec((tm, LANES), lambda i: (i, 0))],
        out_specs=[pl.BlockSpec((tm, LANES), lambda i: (i, 0)),
                   pl.BlockSpec((tm, LANES), lambda i: (i, 0)),
                   pl.BlockSpec((SUBLANES, LANES), lambda i: (0, 0))],
        out_shape=[jax.ShapeDtypeStruct((t, LANES), jnp.int32),
                   jax.ShapeDtypeStruct((t, LANES), F32),
                   jax.ShapeDtypeStruct((SUBLANES, LANES), F32)],
        scratch_shapes=[pltpu.VMEM((SUBLANES, LANES), F32)],
        compiler_params=_cp(("arbitrary",), 32),
        name="route",
    )(logits)


def _dispatch_kernel(h_ref, dest_ref, xs_in_ref, xs_ref, sem):
    del xs_in_ref
    tm = h_ref.shape[0] // ROW_TILES

    def row_copy(r, d):
        return pltpu.make_async_copy(
            h_ref.at[pl.ds(pl.multiple_of(r * ROW_TILES, ROW_TILES), ROW_TILES), :],
            xs_ref.at[pl.ds(pl.multiple_of(d * ROW_TILES, ROW_TILES), ROW_TILES), :], sem)

    def issue(r, carry):
        for k in range(TOP_K):
            row_copy(r, dest_ref[r * TOP_K + k]).start()
        return carry

    lax.fori_loop(0, tm, issue, 0)
    for _ in range(TOP_K):
        pltpu.make_async_copy(h_ref, xs_ref.at[pl.ds(0, tm * ROW_TILES), :], sem).wait()


def _dispatch(h3, dest_flat, n_rows):
    t = h3.shape[0] // ROW_TILES
    tm = min(DISPATCH_ROWS, t)
    xs0 = jnp.zeros((n_rows * ROW_TILES, LANES), F32)
    return pl.pallas_call(
        _dispatch_kernel,
        grid=(t // tm,),
        in_specs=[pl.BlockSpec((tm * ROW_TILES, LANES), lambda i: (i, 0)),
                  pl.BlockSpec((tm * TOP_K,), lambda i: (i,), memory_space=pltpu.SMEM),
                  pl.BlockSpec(memory_space=pl.ANY)],
        out_specs=pl.BlockSpec(memory_space=pl.ANY),
        out_shape=jax.ShapeDtypeStruct((n_rows * ROW_TILES, LANES), F32),
        scratch_shapes=[pltpu.SemaphoreType.DMA(())],
        input_output_aliases={2: 0},
        compiler_params=_cp(("arbitrary",), 32),
        name="dispatch",
    )(h3, dest_flat, xs0)


def _expert_kernel(be_ref, nu_ref, xs_ref, wi_ref, bi_ref, wo_ref, bo_ref, y_ref, wi_b, wo_b):
    i = pl.program_id(0)
    tb = xs_ref.shape[0] // ROW_TILES
    prev = be_ref[jnp.maximum(i - 1, 0)]
    fresh = jnp.logical_or(i == 0, be_ref[i] != prev)

    @pl.when(jnp.logical_and(fresh, i < nu_ref[0]))
    def _():
        wi_b[...] = wi_ref[0, 0].astype(BF16)
        wo_b[...] = wo_ref[0, 0].astype(BF16)

    @pl.when(i < nu_ref[0])
    def _():
        x = jnp.concatenate(
            [xs_ref[pl.ds(c, tb, stride=ROW_TILES), :] for c in range(ROW_TILES)],
            axis=1).astype(BF16)
        y = jnp.zeros((tb, D_MODEL), F32) + bo_ref[0, 0]
        half = 512
        for j in range(D_FF // half):
            a, b = j * half, (j + 1) * half
            glu = _dot(x, wi_b[:, a:b]) + bi_ref[0, 0, :, a:b]
            lin = _dot(x, wi_b[:, D_FF + a:D_FF + b]) + bi_ref[0, 0, :, D_FF + a:D_FF + b]
            glu = jnp.minimum(glu, SWIGLU_LIMIT)
            lin = jnp.clip(lin, -SWIGLU_LIMIT, SWIGLU_LIMIT)
            act = glu * jax.nn.sigmoid(SWIGLU_ALPHA * glu) * (lin + 1.0)
            y = y + _dot(act.astype(BF16), wo_b[a:b, :])
        for c in range(ROW_TILES):
            y_ref[pl.ds(c, tb, stride=ROW_TILES), :] = y[:, c * LANES:(c + 1) * LANES]

    @pl.when(i >= nu_ref[0])
    def _():
        y_ref[...] = jnp.zeros_like(y_ref)


def _experts(xs, block_e, n_used, layer, w_in, b_in, w_out, b_out):
    tb = EXPERT_ROWS
    n_rows = xs.shape[0] // ROW_TILES
    nb = n_rows // tb
    depth, ne, d, f2 = w_in.shape

    def row_map(i, be, nu):
        return (jnp.minimum(i, nu[0] - 1), 0)

    grid_spec = pltpu.PrefetchScalarGridSpec(
        num_scalar_prefetch=2,
        grid=(nb,),
        in_specs=[
            pl.BlockSpec((tb * ROW_TILES, LANES), row_map),
            pl.BlockSpec((1, 1, d, f2), lambda i, be, nu: (layer, be[i], 0, 0)),
            pl.BlockSpec((1, 1, 1, f2), lambda i, be, nu: (layer, be[i], 0, 0)),
            pl.BlockSpec((1, 1, D_FF, d), lambda i, be, nu: (layer, be[i], 0, 0)),
            pl.BlockSpec((1, 1, 1, d), lambda i, be, nu: (layer, be[i], 0, 0)),
        ],
        out_specs=pl.BlockSpec((tb * ROW_TILES, LANES), lambda i, be, nu: (i, 0)),
        scratch_shapes=[pltpu.VMEM((d, f2), BF16), pltpu.VMEM((D_FF, d), BF16)],
    )
    return pl.pallas_call(
        _expert_kernel,
        grid_spec=grid_spec,
        out_shape=jax.ShapeDtypeStruct((n_rows * ROW_TILES, LANES), F32),
        compiler_params=_cp(("arbitrary",), 56),
        name="experts",
    )(block_e, n_used, xs, w_in, b_in.reshape(depth, ne, 1, f2), w_out,
      b_out.reshape(depth, ne, 1, d))


def _combine_kernel(final, dest_ref, gate_ref, x_ref, mod_ref, fg_ref, yb_ref, o_ref, g_buf, sem):
    tm = x_ref.shape[0]

    def row_copy(slot, d):
        return pltpu.make_async_copy(
            yb_ref.at[pl.ds(pl.multiple_of(d * ROW_TILES, ROW_TILES), ROW_TILES), :],
            g_buf.at[pl.ds(pl.multiple_of(slot * ROW_TILES, ROW_TILES), ROW_TILES), :], sem)

    def issue(r, carry):
        for k in range(TOP_K):
            row_copy(k * tm + r, dest_ref[r * TOP_K + k]).start()
        return carry

    lax.fori_loop(0, tm, issue, 0)
    pltpu.make_async_copy(yb_ref.at[pl.ds(0, TOP_K * tm * ROW_TILES), :], g_buf, sem).wait()

    gates = gate_ref[...]
    gate2 = mod_ref[0, 5:6, :]
    for c in range(ROW_TILES):
        y = jnp.zeros((tm, LANES), F32)
        for k in range(TOP_K):
            rows = g_buf[pl.ds(k * tm * ROW_TILES + c, tm, stride=ROW_TILES), :]
            y = y + gates[:, k:k + 1] * rows
        cs = slice(c * LANES, (c + 1) * LANES)
        o_ref[:, cs] = x_ref[:, cs] + gate2[:, cs] * y
    if final:
        o_ref[...] = _rms_rows(o_ref[...]) * fg_ref[...]


def _combine(yb, dest_flat, gates, x2, mod, final_g, final, seq):
    t, d = x2.shape
    tm = min(COMBINE_ROWS, seq)
    return pl.pallas_call(
        functools.partial(_combine_kernel, final),
        grid=(t // tm,),
        in_specs=[
            pl.BlockSpec((tm * TOP_K,), lambda i: (i,), memory_space=pltpu.SMEM),
            pl.BlockSpec((tm, LANES), lambda i: (i, 0)),
            pl.BlockSpec((tm, d), lambda i: (i, 0)),
            pl.BlockSpec((1, 6, d), lambda i: (i // (seq // tm), 0, 0)),
            pl.BlockSpec((1, d), lambda i: (0, 0)),
            pl.BlockSpec(memory_space=pl.ANY),
        ],
        out_specs=pl.BlockSpec((tm, d), lambda i: (i, 0)),
        out_shape=jax.ShapeDtypeStruct((t, d), F32),
        scratch_shapes=[pltpu.VMEM((TOP_K * tm * ROW_TILES, LANES), F32),
                        pltpu.SemaphoreType.DMA(())],
        compiler_params=_cp(("arbitrary",), 40),
        name="combine",
    )(dest_flat, gates, x2, mod, final_g.reshape(1, d), yb)


def _moe(h3, logits, x2, mod, layer, w_in, b_in, w_out, b_out, final_g, final, seq):
    t = x2.shape[0]
    tb = EXPERT_ROWS
    idx, gates, cnt = _route(logits)
    counts = cnt[0, :N_EXPERTS].astype(jnp.int32)
    nblk = (counts + tb - 1) // tb
    blk_end = jnp.cumsum(nblk)
    pad_start = (blk_end - nblk) * tb
    dest = pad_start[idx[:, :TOP_K]] + idx[:, TOP_K:2 * TOP_K]
    dest_flat = dest.reshape(t * TOP_K)
    n_blocks = (t * TOP_K) // tb + N_EXPERTS
    n_used = blk_end[-1:]
    last_e = jnp.max(jnp.where(nblk > 0, jnp.arange(N_EXPERTS, dtype=jnp.int32), 0))
    blk = jnp.arange(n_blocks, dtype=jnp.int32)
    block_e = jnp.minimum(
        jnp.sum((blk_end[None, :] <= blk[:, None]).astype(jnp.int32), axis=1), last_e)
    xs = _dispatch(h3, dest_flat, n_blocks * tb)
    yb = _experts(xs, block_e, n_used.astype(jnp.int32), layer, w_in, b_in, w_out, b_out)
    return _combine(yb, dest_flat, gates, x2, mod, final_g, final, seq)


def kernel(x, c, positions, w_ada, b_ada, even_w_in, gla_w_gate, gla_b_gate, gla_norm_g,
           diff_lam_q1, diff_lam_k1, diff_lam_q2, diff_lam_k2, diff_norm_g, even_w_out,
           odd_w_in, sgu_ln_g, sgu_ln_b, sgu_w, sgu_b, odd_w_out,
           router_w, router_b, expert_w_in, expert_b_in, expert_w_out, expert_b_out,
           final_norm_g):
    bsz, seq, d = x.shape
    depth = w_ada.shape[0]
    t = bsz * seq
    mods = _modulation(c, w_ada, b_ada).reshape(depth, bsz, 6, d)
    cos, sin = _rope_tables(positions)
    x2 = x.reshape(t, d)
    for layer in range(depth):
        mod = mods[layer]
        j = layer // 2
        if layer % 2 == 0:
            z, f = _proj_even(x2, mod, even_w_in[j], gla_w_gate[j], gla_b_gate[j], cos, sin, seq)
            o_gla = _gla(z, f, gla_norm_g[j], bsz, seq)
            lam_init = 0.8 - 0.6 * math.exp(-0.3 * layer)
            o_diff = _diff_attn(z, diff_lam_q1[j], diff_lam_k1[j], diff_lam_q2[j], diff_lam_k2[j],
                                diff_norm_g[j], lam_init, bsz, seq)
            mixed, w_out = (o_gla, o_diff), even_w_out[j]
        else:
            z, f = _proj_odd(x2, mod, odd_w_in[j], sgu_ln_g[j], sgu_ln_b[j], cos, sin, seq)
            mixed, w_out = (_sgu_ret(z, f, sgu_w[j], sgu_b[j], bsz, seq),), odd_w_out[j]
        x2, h3, logits = _out_proj(mixed, w_out, x2, mod, router_w[layer], router_b[layer], seq)
        x2 = _moe(h3, logits, x2, mod, layer, expert_w_in, expert_b_in, expert_w_out,
                  expert_b_out, final_norm_g, layer == depth - 1, seq)
    return x2.reshape(bsz, seq, d)
```

```python
import functools
import math

import jax
import jax.numpy as jnp
from jax import lax
from jax.experimental import pallas as pl
from jax.experimental.pallas import tpu as pltpu

F32 = jnp.float32
BF16 = jnp.bfloat16

D_MODEL = 1024
EPS = 1e-6
ROPE_THETA = 10000.0
ROPE_HALF = 32

GLA_HEADS = 4
GLA_DK = 64
GLA_DV = 128
GLA_RANK = 16
GLA_CHUNK = 64
GLA_GATE_NORMALIZER = 16.0
GLA_QK_W = GLA_HEADS * GLA_DK
GLA_V_W = GLA_HEADS * GLA_DV

DIFF_HEADS = 4
DIFF_D = 64
DIFF_DV = 128
DIFF_QK_W = DIFF_HEADS * 2 * DIFF_D
DIFF_V_W = DIFF_HEADS * DIFF_DV

SGU_GROUPS = 4
SGU_CH = 128
SGU_CHUNK = 128
SGU_W = SGU_GROUPS * SGU_CH

RET_HEADS = 4
RET_DK = 64
RET_DV = 128
RET_CHUNK = 128
RET_QK_W = RET_HEADS * RET_DK
RET_V_W = RET_HEADS * RET_DV

N_EXPERTS = 32
TOP_K = 4
D_FF = D_MODEL
SWIGLU_ALPHA = 1.702
SWIGLU_LIMIT = 7.0

LANES = 128
SUBLANES = 8
ROW_TILES = D_MODEL // LANES

PROJ_ROWS = 512
GLA_ROWS = 256
ATT_Q_ROWS = 256
ATT_HEADS_PER_STEP = 2
ROUTE_ROWS = 512
DISPATCH_ROWS = 256
FILL_SHIFT = 6
FILL_ROWS = 1 << FILL_SHIFT
EXPERT_ROWS = 512
COMBINE_ROWS = 256
MOD_COLS = 1536

MIB = 1024 * 1024


def _cp(semantics, vmem_mib):
    return pltpu.CompilerParams(dimension_semantics=semantics, vmem_limit_bytes=vmem_mib * MIB)


def _dot(a, b):
    return jnp.dot(a, b, preferred_element_type=F32)


def _dot_nt(a, b):
    return lax.dot_general(a, b, (((1,), (1,)), ((), ())), preferred_element_type=F32)


def _dot_tn(a, b):
    return lax.dot_general(a, b, (((0,), (0,)), ((), ())), preferred_element_type=F32)


def _rms_rows(x):
    return x * lax.rsqrt(jnp.mean(x * x, axis=-1, keepdims=True) + EPS)


def _silu(x):
    return x * jax.nn.sigmoid(x)


def _rope_chunk(v, cos, sin, lo_half):
    rot = jnp.where(lo_half, -pltpu.roll(v, 96, 1), pltpu.roll(v, 32, 1))
    return v * cos + rot * sin


def _mod_kernel(c_ref, w_ref, b_ref, o_ref):
    c = c_ref[...]
    ca = _silu(c).astype(BF16)
    o_ref[0] = _dot(ca, w_ref[0].astype(BF16)) + b_ref[0]


def _modulation(c, w_ada, b_ada):
    depth, d, n = w_ada.shape
    bsz = c.shape[0]
    return pl.pallas_call(
        _mod_kernel,
        grid=(depth, n // MOD_COLS),
        in_specs=[
            pl.BlockSpec((bsz, d), lambda l, j: (0, 0)),
            pl.BlockSpec((1, d, MOD_COLS), lambda l, j: (l, 0, j)),
            pl.BlockSpec((1, 1, MOD_COLS), lambda l, j: (l, 0, j)),
        ],
        out_specs=pl.BlockSpec((1, bsz, MOD_COLS), lambda l, j: (l, 0, j)),
        out_shape=jax.ShapeDtypeStruct((depth, bsz, n), F32),
        compiler_params=_cp(("arbitrary", "arbitrary"), 40),
        name="adaln_mod",
    )(c, w_ada, b_ada.reshape(depth, 1, n))


def _rope_table_kernel(p_ref, f_ref, c_ref, s_ref):
    ang = p_ref[...].astype(F32) * f_ref[...]
    c_ref[...] = jnp.cos(ang)
    s_ref[...] = jnp.sin(ang)


def _rope_tables(positions):
    t = positions.size
    per_row = LANES // ROPE_HALF
    rows = t // per_row
    pos_d = jnp.repeat(positions.reshape(rows, per_row), ROPE_HALF, axis=1)
    inv_freq = ROPE_THETA ** (-jnp.arange(ROPE_HALF, dtype=F32) / ROPE_HALF)
    freq_d = jnp.tile(inv_freq, per_row).reshape(1, LANES)
    tr = min(512, rows)
    cos_d, sin_d = pl.pallas_call(
        _rope_table_kernel,
        grid=(rows // tr,),
        in_specs=[pl.BlockSpec((tr, LANES), lambda i: (i, 0)),
                  pl.BlockSpec((1, LANES), lambda i: (0, 0))],
        out_specs=[pl.BlockSpec((tr, LANES), lambda i: (i, 0))] * 2,
        out_shape=[jax.ShapeDtypeStruct((rows, LANES), F32)] * 2,
        compiler_params=_cp(("arbitrary",), 32),
        name="rope_tables",
    )(pos_d, freq_d)
    cos = jnp.tile(cos_d.reshape(t, ROPE_HALF), (1, per_row))
    sin = jnp.tile(sin_d.reshape(t, ROPE_HALF), (1, per_row))
    return cos, sin


def _modulated_rms(x_ref, mod_ref, which):
    x = x_ref[...]
    shift = mod_ref[0, 3 * which:3 * which + 1, :]
    scale = mod_ref[0, 3 * which + 1:3 * which + 2, :]
    return _rms_rows(x) * (1.0 + scale) + shift


EVEN_Z = (GLA_V_W, GLA_V_W, DIFF_QK_W, DIFF_QK_W, DIFF_V_W)
EVEN_Z_W = sum(EVEN_Z)
EVEN_F_W = 3 * GLA_QK_W
EVEN_W_COLS = EVEN_Z_W + 2 * GLA_QK_W + LANES


def _proj_even_kernel(x_ref, mod_ref, w_ref, wg_ref, bg_ref, cos_ref, sin_ref, z_ref, f_ref):
    hb = _modulated_rms(x_ref, mod_ref, 0).astype(BF16)

    def sec(a, b):
        return _dot(hb, w_ref[:, a:b])

    for a, b in ((0, 512), (512, 1024), (2048, 2560)):
        z_ref[:, a:b] = sec(a, b).astype(BF16)
    cos = cos_ref[...]
    sin = sin_ref[...]
    lo_half = (lax.broadcasted_iota(jnp.int32, cos.shape, 1) % 64) < ROPE_HALF
    q_scale = DIFF_D ** -0.5
    for a, scl in ((1024, q_scale), (1536, 1.0)):
        for c in range(DIFF_QK_W // LANES):
            v = sec(a + c * LANES, a + (c + 1) * LANES)
            z_ref[:, a + c * LANES:a + (c + 1) * LANES] = (
                _rope_chunk(v, cos, sin, lo_half) * scl).astype(BF16)
    f_ref[:, 0:256] = sec(2560, 2816) * (GLA_DK ** -0.5)
    f_ref[:, 256:512] = sec(2816, 3072)
    gr = sec(3072, 3200).astype(BF16)
    pre = _dot(gr, wg_ref[...]) + bg_ref[...]
    log_sig = jnp.minimum(pre, 0.0) - jnp.log1p(jnp.exp(-jnp.abs(pre)))
    f_ref[:, 512:768] = log_sig / GLA_GATE_NORMALIZER


def _proj_even(x2, mod, w_in, w_gate, b_gate, cos, sin, seq):
    t, d = x2.shape
    tm = min(PROJ_ROWS, seq)
    gq, gk, gv, gr, gg, dq, dk, dv = jnp.split(
        w_in, [256, 512, 1024, 1040, 1552, 2064, 2576], axis=1)
    gr_pad = jnp.pad(gr, ((0, 0), (0, LANES - GLA_RANK)))
    w_cat = jnp.concatenate([gv, gg, dq, dk, dv, gq, gk, gr_pad], axis=1).astype(BF16)
    wg_pad = jnp.pad(w_gate, ((0, LANES - GLA_RANK), (0, 0))).astype(BF16)
    return pl.pallas_call(
        _proj_even_kernel,
        grid=(t // tm,),
        in_specs=[
            pl.BlockSpec((tm, d), lambda i: (i, 0)),
            pl.BlockSpec((1, 6, d), lambda i: (i // (seq // tm), 0, 0)),
            pl.BlockSpec((d, EVEN_W_COLS), lambda i: (0, 0)),
            pl.BlockSpec((LANES, GLA_QK_W), lambda i: (0, 0)),
            pl.BlockSpec((1, GLA_QK_W), lambda i: (0, 0)),
            pl.BlockSpec((tm, LANES), lambda i: (i, 0)),
            pl.BlockSpec((tm, LANES), lambda i: (i, 0)),
        ],
        out_specs=[pl.BlockSpec((tm, EVEN_Z_W), lambda i: (i, 0)),
                   pl.BlockSpec((tm, EVEN_F_W), lambda i: (i, 0))],
        out_shape=[jax.ShapeDtypeStruct((t, EVEN_Z_W), BF16),
                   jax.ShapeDtypeStruct((t, EVEN_F_W), F32)],
        compiler_params=_cp(("arbitrary",), 48),
        name="proj_even",
    )(x2, mod, w_cat, wg_pad, b_gate.reshape(1, GLA_QK_W), cos, sin)


ODD_Z_W = 2 * SGU_W + 2 * RET_V_W
ODD_F_W = 2 * RET_QK_W
ODD_W_COLS = ODD_Z_W + ODD_F_W


def _gelu_exact(x):
    return 0.5 * x * (1.0 + lax.erf(x * (2.0 ** -0.5)))


def _proj_odd_kernel(x_ref, mod_ref, w_ref, lng_ref, lnb_ref, cos_ref, sin_ref, z_ref, f_ref):
    hb = _modulated_rms(x_ref, mod_ref, 0).astype(BF16)

    def sec(a, b):
        return _dot(hb, w_ref[:, a:b])

    z_ref[:, 0:512] = _gelu_exact(sec(0, 512)).astype(BF16)
    sv = _gelu_exact(sec(512, 1024))
    mu = jnp.mean(sv, axis=-1, keepdims=True)
    cen = sv - mu
    var = jnp.mean(cen * cen, axis=-1, keepdims=True)
    z_ref[:, 512:1024] = (cen * lax.rsqrt(var + EPS) * lng_ref[...] + lnb_ref[...]).astype(BF16)
    for a, b in ((1024, 1536), (1536, 2048)):
        z_ref[:, a:b] = sec(a, b).astype(BF16)
    cos = cos_ref[...]
    sin = sin_ref[...]
    lo_half = (lax.broadcasted_iota(jnp.int32, cos.shape, 1) % 64) < ROPE_HALF
    k_scale = RET_DK ** -0.5
    for a, scl in ((0, 1.0), (256, k_scale)):
        for c in range(RET_QK_W // LANES):
            v = sec(ODD_Z_W + a + c * LANES, ODD_Z_W + a + (c + 1) * LANES)
            f_ref[:, a + c * LANES:a + (c + 1) * LANES] = _rope_chunk(v, cos, sin, lo_half) * scl


def _proj_odd(x2, mod, w_in, ln_g, ln_b, cos, sin, seq):
    t, d = x2.shape
    tm = min(PROJ_ROWS, seq)
    su, sv, rq, rk, rv, rg = jnp.split(w_in, [512, 1024, 1280, 1536, 2048], axis=1)
    w_cat = jnp.concatenate([su, sv, rv, rg, rq, rk], axis=1).astype(BF16)
    return pl.pallas_call(
        _proj_odd_kernel,
        grid=(t // tm,),
        in_specs=[
            pl.BlockSpec((tm, d), lambda i: (i, 0)),
            pl.BlockSpec((1, 6, d), lambda i: (i // (seq // tm), 0, 0)),
            pl.BlockSpec((d, ODD_W_COLS), lambda i: (0, 0)),
            pl.BlockSpec((1, SGU_W), lambda i: (0, 0)),
            pl.BlockSpec((1, SGU_W), lambda i: (0, 0)),
            pl.BlockSpec((tm, LANES), lambda i: (i, 0)),
            pl.BlockSpec((tm, LANES), lambda i: (i, 0)),
        ],
        out_specs=[pl.BlockSpec((tm, ODD_Z_W), lambda i: (i, 0)),
                   pl.BlockSpec((tm, ODD_F_W), lambda i: (i, 0))],
        out_shape=[jax.ShapeDtypeStruct((t, ODD_Z_W), BF16),
                   jax.ShapeDtypeStruct((t, ODD_F_W), F32)],
        compiler_params=_cp(("arbitrary",), 48),
        name="proj_odd",
    )(x2, mod, w_cat, ln_g.reshape(1, SGU_W), ln_b.reshape(1, SGU_W), cos, sin)


def _gla_kernel(q_ref, k_ref, la_ref, v_ref, gg_ref, g_ref, o_ref, st_ref):
    @pl.when(pl.program_id(1) == 0)
    def _():
        st_ref[...] = jnp.zeros_like(st_ref)

    c = GLA_CHUNK
    rows = q_ref.shape[0]
    r_i = lax.broadcasted_iota(jnp.int32, (c, c), 0)
    c_i = lax.broadcasted_iota(jnp.int32, (c, c), 1)
    tril = (r_i >= c_i).astype(BF16)
    r2 = lax.broadcasted_iota(jnp.int32, (2 * c, c), 0) % c
    c2 = lax.broadcasted_iota(jnp.int32, (2 * c, c), 1)
    causal2 = r2 >= c2
    lane = lax.broadcasted_iota(jnp.int32, (c, LANES), 1)
    lo = lane < GLA_DK
    lane_s = lax.broadcasted_iota(jnp.int32, (LANES, LANES), 1) < GLA_DK
    g_row = g_ref[...]
    states = [st_ref[0], st_ref[1]]
    for j in range(rows // c):
        rs = slice(j * c, (j + 1) * c)
        la = la_ref[rs, :]
        la_hi = la.astype(BF16)
        la_lo = (la - la_hi.astype(F32)).astype(BF16)
        b = _dot(tril, la_hi) + _dot(tril, la_lo)
        b_last = b[c - 1:c, :]
        qd = q_ref[rs, :] * jnp.exp(b)
        kk = k_ref[rs, :]
        ki = kk * jnp.exp(-b)
        kd = kk * jnp.exp(b_last - b)
        dec = jnp.exp(b_last)
        for p in range(GLA_HEADS // 2):
            ls = slice(p * LANES, (p + 1) * LANES)
            qd_p = qd[:, ls]
            qm = (jnp.where(lo, qd_p, 0.0).astype(BF16), jnp.where(lo, 0.0, qd_p).astype(BF16))
            s2 = _dot_nt(jnp.concatenate(qm, axis=0), ki[:, ls].astype(BF16))
            s2 = jnp.where(causal2, s2, 0.0).astype(BF16)
            kd_p = kd[:, ls].astype(BF16)
            st_b = states[p].astype(BF16)
            new = []
            for hh in range(2):
                h = 2 * p + hh
                hs = slice(h * GLA_DV, (h + 1) * GLA_DV)
                v_h = v_ref[rs, hs]
                o = _dot(s2[hh * c:(hh + 1) * c, :], v_h) + _dot_nt(qm[hh], st_b)
                o = _rms_rows(o) * g_row
                gate = _silu(gg_ref[rs, hs].astype(F32))
                o_ref[rs, hs] = (o * gate).astype(BF16)
                new.append(_dot_tn(v_h, kd_p))
            states[p] = dec[:, ls] * states[p] + jnp.where(lane_s, new[0], new[1])
    st_ref[0] = states[0]
    st_ref[1] = states[1]


def _gla(z, f, norm_g, bsz, seq):
    t = z.shape[0]
    tc = min(GLA_ROWS, seq)
    nc = seq // tc
    return pl.pallas_call(
        _gla_kernel,
        grid=(bsz, nc),
        in_specs=[
            pl.BlockSpec((tc, GLA_QK_W), lambda b, i: (b * nc + i, 0)),
            pl.BlockSpec((tc, GLA_QK_W), lambda b, i: (b * nc + i, 1)),
            pl.BlockSpec((tc, GLA_QK_W), lambda b, i: (b * nc + i, 2)),
            pl.BlockSpec((tc, GLA_V_W), lambda b, i: (b * nc + i, 0)),
            pl.BlockSpec((tc, GLA_V_W), lambda b, i: (b * nc + i, 1)),
            pl.BlockSpec((1, GLA_DV), lambda b, i: (0, 0)),
        ],
        out_specs=pl.BlockSpec((tc, GLA_V_W), lambda b, i: (b * nc + i, 0)),
        out_shape=jax.ShapeDtypeStruct((t, GLA_V_W), BF16),
        scratch_shapes=[pltpu.VMEM((2, LANES, LANES), F32)],
        compiler_params=_cp(("arbitrary", "arbitrary"), 32),
        name="gla",
    )(f, f, f, z, z, norm_g.reshape(1, GLA_DV))


def _diff_attn_kernel(lam_init, q_ref, k_ref, v_ref, lq1_ref, lk1_ref, lq2_ref, lk2_ref, g_ref,
                      o_ref, vt_ref):
    tq = q_ref.shape[0]
    seq = k_ref.shape[0]
    n_heads = q_ref.shape[1] // LANES
    qi = pl.program_id(2)

    @pl.when(qi == 0)
    def _():
        for hh in range(n_heads):
            for cb in range(seq // tq):
                blk = v_ref[cb * tq:(cb + 1) * tq, hh * LANES:(hh + 1) * LANES]
                vt_ref[hh, :, cb * tq:(cb + 1) * tq] = blk.astype(F32).T.astype(BF16)

    lo = lax.broadcasted_iota(jnp.int32, (tq, LANES), 1) < DIFF_D
    qqs = []
    for hh in range(n_heads):
        q = q_ref[:, hh * LANES:(hh + 1) * LANES]
        zero = jnp.zeros_like(q)
        qqs.append(jnp.concatenate([jnp.where(lo, q, zero), jnp.where(lo, zero, q)], axis=0))

    def scores(j, hh):
        start = pl.multiple_of(j * tq, tq)
        kj = k_ref[pl.ds(start, tq), hh * LANES:(hh + 1) * LANES]
        return _dot_nt(kj, qqs[hh])

    def update(j, state, s, hh, masked):
        m, l, acc = state
        start = pl.multiple_of(j * tq, tq)
        vtj = vt_ref[hh, :, pl.ds(start, tq)]
        if masked:
            kv = lax.broadcasted_iota(jnp.int32, s.shape, 0)
            qq_pos = lax.broadcasted_iota(jnp.int32, s.shape, 1) % tq
            s = jnp.where(kv <= qq_pos, s, -jnp.inf)
        m_new = jnp.maximum(m, jnp.max(s, axis=0, keepdims=True))
        alpha = jnp.exp(m - m_new)
        p = jnp.exp(s - m_new)
        l = alpha * l + jnp.sum(p, axis=0, keepdims=True)
        acc = alpha * acc + _dot(vtj, p.astype(BF16))
        return m_new, l, acc

    def body(j, carries):
        nxt = [scores(j + 1, hh) for hh in range(n_heads)]
        return tuple((update(j, carries[hh][0], carries[hh][1], hh, False), nxt[hh])
                     for hh in range(n_heads))

    init = tuple(((jnp.full((1, 2 * tq), -jnp.inf, F32), jnp.zeros((1, 2 * tq), F32),
                   jnp.zeros((DIFF_DV, 2 * tq), F32)), scores(0, hh)) for hh in range(n_heads))
    carries = lax.fori_loop(0, qi, body, init)
    carries = tuple(update(qi, carries[hh][0], carries[hh][1], hh, True)
                    for hh in range(n_heads))
    lam = (jnp.exp(jnp.sum(lq1_ref[...] * lk1_ref[...], axis=-1, keepdims=True))
           - jnp.exp(jnp.sum(lq2_ref[...] * lk2_ref[...], axis=-1, keepdims=True)) + lam_init)
    for hh in range(n_heads):
        _, l, acc = carries[hh]
        o12 = acc / l
        o = o12[:, :tq] - lam * o12[:, tq:]
        o = o * lax.rsqrt(jnp.mean(o * o, axis=0, keepdims=True) + EPS)
        o = o * g_ref[...] * (1.0 - lam_init)
        o_ref[:, hh * DIFF_DV:(hh + 1) * DIFF_DV] = o.T.astype(BF16)


def _diff_attn(z, lq1, lk1, lq2, lk2, norm_g, lam_init, bsz, seq):
    t = z.shape[0]
    tq = min(ATT_Q_ROWS, seq)
    nq = seq // tq
    hw = ATT_HEADS_PER_STEP * LANES
    qb, kb, vb = 1024 // hw, 1536 // hw, 2048 // hw
    small = pl.BlockSpec((1, DIFF_D), lambda b, h, i: (0, 0))
    return pl.pallas_call(
        functools.partial(_diff_attn_kernel, lam_init),
        grid=(bsz, DIFF_HEADS // ATT_HEADS_PER_STEP, nq),
        in_specs=[
            pl.BlockSpec((tq, hw), lambda b, h, i: (b * nq + i, qb + h)),
            pl.BlockSpec((seq, hw), lambda b, h, i: (b, kb + h)),
            pl.BlockSpec((seq, hw), lambda b, h, i: (b, vb + h)),
            small, small, small, small,
            pl.BlockSpec((DIFF_DV, 1), lambda b, h, i: (0, 0)),
        ],
        out_specs=pl.BlockSpec((tq, hw), lambda b, h, i: (b * nq + i, h)),
        out_shape=jax.ShapeDtypeStruct((t, DIFF_V_W), BF16),
        scratch_shapes=[pltpu.VMEM((ATT_HEADS_PER_STEP, DIFF_DV, seq), BF16)],
        compiler_params=_cp(("arbitrary", "arbitrary", "arbitrary"), 32),
        name="diff_attn",
    )(z, z, z, lq1.reshape(1, DIFF_D), lk1.reshape(1, DIFF_D), lq2.reshape(1, DIFF_D),
      lk2.reshape(1, DIFF_D), norm_g.reshape(DIFF_DV, 1))


def _sgu_ret_kernel(su_ref, sv_ref, rv_ref, rg_ref, q_ref, k_ref, ws_ref, bs_ref, o_ref, st_ref):
    @pl.when(pl.program_id(1) == 0)
    def _():
        st_ref[...] = jnp.zeros_like(st_ref)

    c = RET_CHUNK
    row = lax.broadcasted_iota(jnp.int32, (c, c), 0)
    col = lax.broadcasted_iota(jnp.int32, (c, c), 1)
    causal = row >= col
    for g in range(SGU_GROUPS):
        gs = slice(g * SGU_CH, (g + 1) * SGU_CH)
        w = jnp.where(causal, ws_ref[g], 0.0).astype(BF16)
        s = _dot(w, sv_ref[:, gs]) + bs_ref[g]
        o_ref[:, gs] = (su_ref[:, gs].astype(F32) * s).astype(BF16)
    log_g = [math.log(1.0 - 2.0 ** (-5.0 - h)) for h in range(RET_HEADS)]
    lo = col < RET_DK
    rel = (row - col).astype(F32)
    pos = row.astype(F32)
    for p in range(RET_HEADS // 2):
        ls = slice(p * LANES, (p + 1) * LANES)
        lg = jnp.where(lo, log_g[2 * p], log_g[2 * p + 1])
        q_p = q_ref[:, ls]
        k_p = k_ref[:, ls]
        qm = (jnp.where(lo, q_p, 0.0).astype(BF16), jnp.where(lo, 0.0, q_p).astype(BF16))
        s2 = _dot_nt(jnp.concatenate(qm, axis=0), k_p.astype(BF16))
        qd = q_p * jnp.exp(lg * (pos + 1.0))
        qdm = (jnp.where(lo, qd, 0.0).astype(BF16), jnp.where(lo, 0.0, qd).astype(BF16))
        kd = (k_p * jnp.exp(lg * (c - 1.0 - pos))).astype(BF16)
        st = st_ref[p]
        st_b = st.astype(BF16)
        new = []
        for hh in range(2):
            h = 2 * p + hh
            hs = slice(h * RET_DV, (h + 1) * RET_DV)
            decay = jnp.where(causal, jnp.exp(log_g[h] * jnp.maximum(rel, 0.0)), 0.0)
            s_h = (s2[hh * c:(hh + 1) * c, :] * decay).astype(BF16)
            v_h = rv_ref[:, hs]
            o = _dot(s_h, v_h) + _dot_nt(qdm[hh], st_b)
            gate = _silu(rg_ref[:, hs].astype(F32))
            o_ref[:, SGU_W + h * RET_DV:SGU_W + (h + 1) * RET_DV] = (
                _rms_rows(o) * gate).astype(BF16)
            new.append(_dot_tn(v_h, kd))
        st_ref[p] = jnp.exp(lg * float(c)) * st + jnp.where(lo, new[0], new[1])


def _sgu_ret(z, f, w_s, b_s, bsz, seq):
    t = z.shape[0]
    c = RET_CHUNK
    nc = seq // c
    return pl.pallas_call(
        _sgu_ret_kernel,
        grid=(bsz, nc),
        in_specs=[
            pl.BlockSpec((c, SGU_W), lambda b, i: (b * nc + i, 0)),
            pl.BlockSpec((c, SGU_W), lambda b, i: (b * nc + i, 1)),
            pl.BlockSpec((c, RET_V_W), lambda b, i: (b * nc + i, 2)),
            pl.BlockSpec((c, RET_V_W), lambda b, i: (b * nc + i, 3)),
            pl.BlockSpec((c, RET_QK_W), lambda b, i: (b * nc + i, 0)),
            pl.BlockSpec((c, RET_QK_W), lambda b, i: (b * nc + i, 1)),
            pl.BlockSpec((SGU_GROUPS, c, c), lambda b, i: (0, 0, 0)),
            pl.BlockSpec((SGU_GROUPS, c, 1), lambda b, i: (0, 0, 0)),
        ],
        out_specs=pl.BlockSpec((c, SGU_W + RET_V_W), lambda b, i: (b * nc + i, 0)),
        out_shape=jax.ShapeDtypeStruct((t, SGU_W + RET_V_W), BF16),
        scratch_shapes=[pltpu.VMEM((2, LANES, LANES), F32)],
        compiler_params=_cp(("arbitrary", "arbitrary"), 32),
        name="sgu_retention",
    )(z, z, z, z, f, f, w_s, b_s.reshape(SGU_GROUPS, c, 1))


def _out_proj_kernel(n_in, *refs):
    o_refs = refs[:n_in]
    w_ref, x_ref, mod_ref, rw_ref, rb_ref, xn_ref, h_ref, lg_ref = refs[n_in:]
    tm = x_ref.shape[0]
    k_each = D_MODEL // n_in
    y = _dot(o_refs[0][...], w_ref[0:k_each, :])
    for n in range(1, n_in):
        y = y + _dot(o_refs[n][...], w_ref[n * k_each:(n + 1) * k_each, :])
    gate1 = mod_ref[0, 2:3, :]
    xn = x_ref[...] + gate1 * y
    xn_ref[...] = xn
    h = _rms_rows(xn) * (1.0 + mod_ref[0, 4:5, :]) + mod_ref[0, 3:4, :]
    for c in range(ROW_TILES):
        h_ref[pl.ds(c, tm, stride=ROW_TILES), :] = h[:, c * LANES:(c + 1) * LANES]
    lg_ref[...] = _dot(h.astype(BF16), rw_ref[...]) + rb_ref[...]


def _out_proj(mixed, w_out, x2, mod, router_w, router_b, seq):
    t, d = x2.shape
    tm = min(PROJ_ROWS, seq)
    n_in = len(mixed)
    k_each = d // n_in
    rw = jnp.pad(router_w, ((0, 0), (0, LANES - N_EXPERTS))).astype(BF16)
    rb = jnp.pad(router_b, (0, LANES - N_EXPERTS)).reshape(1, LANES)
    return pl.pallas_call(
        functools.partial(_out_proj_kernel, n_in),
        grid=(t // tm,),
        in_specs=[pl.BlockSpec((tm, k_each), lambda i: (i, 0)) for _ in mixed] + [
            pl.BlockSpec((d, d), lambda i: (0, 0)),
            pl.BlockSpec((tm, d), lambda i: (i, 0)),
            pl.BlockSpec((1, 6, d), lambda i: (i // (seq // tm), 0, 0)),
            pl.BlockSpec((d, LANES), lambda i: (0, 0)),
            pl.BlockSpec((1, LANES), lambda i: (0, 0)),
        ],
        out_specs=[pl.BlockSpec((tm, d), lambda i: (i, 0)),
                   pl.BlockSpec((tm * ROW_TILES, LANES), lambda i: (i, 0)),
                   pl.BlockSpec((tm, LANES), lambda i: (i, 0))],
        out_shape=[jax.ShapeDtypeStruct((t, d), F32),
                   jax.ShapeDtypeStruct((t * ROW_TILES, LANES), F32),
                   jax.ShapeDtypeStruct((t, LANES), F32)],
        compiler_params=_cp(("arbitrary",), 48),
        name="out_proj",
    )(*mixed, w_out.astype(BF16), x2, mod, rw, rb)


def _route_kernel(lg_ref, idx_ref, gate_ref, cnt_ref, run_ref):
    @pl.when(pl.program_id(0) == 0)
    def _():
        run_ref[...] = jnp.zeros_like(run_ref)

    tm = lg_ref.shape[0]
    lane = lax.broadcasted_iota(jnp.int32, (tm, LANES), 1)
    neg = -jnp.inf
    l = jnp.where(lane < N_EXPERTS, lg_ref[...], neg)
    vals, hots = [], []
    for _ in range(TOP_K):
        m = jnp.max(l, axis=-1, keepdims=True)
        first = jnp.min(jnp.where(l == m, lane, LANES), axis=-1, keepdims=True)
        hot = lane == first
        vals.append(m)
        hots.append(hot)
        l = jnp.where(hot, neg, l)
    sel = hots[0] | hots[1] | hots[2] | hots[3]
    ex = [jnp.exp(v - vals[0]) for v in vals]
    denom = ex[0] + ex[1] + ex[2] + ex[3]
    r_i = lax.broadcasted_iota(jnp.int32, (tm, tm), 0)
    c_i = lax.broadcasted_iota(jnp.int32, (tm, tm), 1)
    before = (r_i > c_i).astype(BF16)
    sel_b = sel.astype(BF16)
    run = run_ref[0:1, :]
    cum = _dot(before, sel_b) + run
    idx_out = jnp.zeros((tm, LANES), jnp.int32)
    gate_out = jnp.zeros((tm, LANES), F32)
    for k in range(TOP_K):
        e_k = jnp.min(jnp.where(hots[k], lane, LANES), axis=-1, keepdims=True)
        rank_k = jnp.sum(jnp.where(hots[k], cum, 0.0), axis=-1, keepdims=True).astype(jnp.int32)
        idx_out = jnp.where(lane == k, e_k, idx_out)
        idx_out = jnp.where(lane == TOP_K + k, rank_k, idx_out)
        gate_out = jnp.where(lane == k, ex[k] / denom, gate_out)
    idx_ref[...] = idx_out
    gate_ref[...] = gate_out
    total = run + jnp.sum(sel.astype(F32), axis=0, keepdims=True)
    run_ref[...] = jnp.broadcast_to(total, run_ref.shape)
    cnt_ref[...] = jnp.broadcast_to(total, cnt_ref.shape)


def _route(logits):
    t = logits.shape[0]
    tm = min(ROUTE_ROWS, t)
    return pl.pallas_call(
        _route_kernel,
        grid=(t // tm,),
        in_specs=[pl.BlockSpec((tm, LANES), lambda i: (i, 0))],
        out_specs=[pl.BlockSpec((tm, LANES), lambda i: (i, 0)),
                   pl.BlockSpec((tm, LANES), lambda i: (i, 0)),
                   pl.BlockSpec((SUBLANES, LANES), lambda i: (0, 0))],
        out_shape=[jax.ShapeDtypeStruct((t, LANES), jnp.int32),
                   jax.ShapeDtypeStruct((t, LANES), F32),
                   jax.ShapeDtypeStruct((SUBLANES, LANES), F32)],
        scratch_shapes=[pltpu.VMEM((SUBLANES, LANES), F32)],
        compiler_params=_cp(("arbitrary",), 32),
        name="route",
    )(logits)


def _dispatch_kernel(fs_ref, fl_ref, h_ref, dest_ref, xs_ref, zero_buf, sem, zsem):
    tm = h_ref.shape[0] // ROW_TILES
    n_fill = fs_ref.shape[0]

    def rows(ref, start, n):
        return ref.at[pl.ds(pl.multiple_of(start * ROW_TILES, ROW_TILES), n * ROW_TILES), :]

    def fill_copies(f, wait):
        start, n = fs_ref[f], fl_ref[f]
        n_chunks = lax.shift_right_logical(n, FILL_SHIFT)
        tail = start + n_chunks * FILL_ROWS

        def chunk(j, carry):
            cp = pltpu.make_async_copy(zero_buf, rows(xs_ref, start + j * FILL_ROWS, FILL_ROWS), zsem)
            cp.wait() if wait else cp.start()
            return carry

        def single(j, carry):
            cp = pltpu.make_async_copy(rows(zero_buf, 0, 1), rows(xs_ref, tail + j, 1), zsem)
            cp.wait() if wait else cp.start()
            return carry

        lax.fori_loop(0, n_chunks, chunk, 0)
        lax.fori_loop(0, n - n_chunks * FILL_ROWS, single, 0)

    @pl.when(pl.program_id(0) == 0)
    def _():
        zero_buf[...] = jnp.zeros_like(zero_buf)
        lax.fori_loop(0, n_fill, lambda f, c: (fill_copies(f, False), c)[1], 0)

    def issue(r, carry):
        for k in range(TOP_K):
            pltpu.make_async_copy(rows(h_ref, r, 1), rows(xs_ref, dest_ref[r * TOP_K + k], 1),
                                  sem).start(priority=k % 2)
        return carry

    lax.fori_loop(0, tm, issue, 0)
    for _ in range(TOP_K):
        pltpu.make_async_copy(h_ref, rows(xs_ref, 0, tm), sem).wait()

    @pl.when(pl.program_id(0) == 0)
    def _():
        lax.fori_loop(0, n_fill, lambda f, c: (fill_copies(f, True), c)[1], 0)


def _dispatch(h3, dest_flat, fill_start, fill_len, n_rows):
    t = h3.shape[0] // ROW_TILES
    tm = min(DISPATCH_ROWS, t)
    grid_spec = pltpu.PrefetchScalarGridSpec(
        num_scalar_prefetch=2,
        grid=(t // tm,),
        in_specs=[pl.BlockSpec((tm * ROW_TILES, LANES), lambda i, fs, fl: (i, 0)),
                  pl.BlockSpec((tm * TOP_K,), lambda i, fs, fl: (i,), memory_space=pltpu.SMEM)],
        out_specs=pl.BlockSpec(memory_space=pl.ANY),
        scratch_shapes=[pltpu.VMEM((FILL_ROWS * ROW_TILES, LANES), F32),
                        pltpu.SemaphoreType.DMA(()), pltpu.SemaphoreType.DMA(())],
    )
    return pl.pallas_call(
        _dispatch_kernel,
        grid_spec=grid_spec,
        out_shape=jax.ShapeDtypeStruct((n_rows * ROW_TILES, LANES), F32),
        compiler_params=_cp(("arbitrary",), 32),
        name="dispatch",
    )(fill_start, fill_len, h3, dest_flat)


def _expert_kernel(layer, be_ref, nu_ref, nx_ref, xs_ref, wi_hbm, bi_ref, wo_hbm, bo_ref, y_ref,
                   wi_st, wo_st, wi_b, wo_b, sems):
    i = pl.program_id(0)
    tb = xs_ref.shape[0] // ROW_TILES
    e = be_ref[i]
    fresh = jnp.logical_or(i == 0, e != be_ref[jnp.maximum(i - 1, 0)])
    used = i < nu_ref[0]

    def fetch(ex):
        return (pltpu.make_async_copy(wi_hbm.at[layer, ex], wi_st, sems.at[0]),
                pltpu.make_async_copy(wo_hbm.at[layer, ex], wo_st, sems.at[1]))

    @pl.when(i == 0)
    def _():
        for cp in fetch(e):
            cp.start()

    @pl.when(jnp.logical_and(fresh, used))
    def _():
        for cp in fetch(e):
            cp.wait()
        wi_b[...] = wi_st[...].astype(BF16)
        wo_b[...] = wo_st[...].astype(BF16)

        @pl.when(nx_ref[i] >= 0)
        def _():
            for cp in fetch(nx_ref[i]):
                cp.start()

    @pl.when(used)
    def _():
        x = jnp.concatenate(
            [xs_ref[pl.ds(c, tb, stride=ROW_TILES), :] for c in range(ROW_TILES)],
            axis=1).astype(BF16)
        y = jnp.zeros((tb, D_MODEL), F32) + bo_ref[0, 0]
        half = 512
        for j in range(D_FF // half):
            a, b = j * half, (j + 1) * half
            glu = _dot(x, wi_b[:, a:b]) + bi_ref[0, 0, :, a:b]
            lin = _dot(x, wi_b[:, D_FF + a:D_FF + b]) + bi_ref[0, 0, :, D_FF + a:D_FF + b]
            glu = jnp.minimum(glu, SWIGLU_LIMIT)
            lin = jnp.clip(lin, -SWIGLU_LIMIT, SWIGLU_LIMIT)
            act = glu * jax.nn.sigmoid(SWIGLU_ALPHA * glu) * (lin + 1.0)
            y = y + _dot(act.astype(BF16), wo_b[a:b, :])
        for c in range(ROW_TILES):
            y_ref[pl.ds(c, tb, stride=ROW_TILES), :] = y[:, c * LANES:(c + 1) * LANES]

    @pl.when(jnp.logical_not(used))
    def _():
        y_ref[...] = jnp.zeros_like(y_ref)


def _experts(xs, block_e, n_used, next_e, layer, w_in, b_in, w_out, b_out):
    tb = EXPERT_ROWS
    n_rows = xs.shape[0] // ROW_TILES
    nb = n_rows // tb
    depth, ne, d, f2 = w_in.shape

    def row_map(i, be, nu, nx):
        return (jnp.minimum(i, nu[0] - 1), 0)

    grid_spec = pltpu.PrefetchScalarGridSpec(
        num_scalar_prefetch=3,
        grid=(nb,),
        in_specs=[
            pl.BlockSpec((tb * ROW_TILES, LANES), row_map),
            pl.BlockSpec(memory_space=pl.ANY),
            pl.BlockSpec((1, 1, 1, f2), lambda i, be, nu, nx: (layer, be[i], 0, 0)),
            pl.BlockSpec(memory_space=pl.ANY),
            pl.BlockSpec((1, 1, 1, d), lambda i, be, nu, nx: (layer, be[i], 0, 0)),
        ],
        out_specs=pl.BlockSpec((tb * ROW_TILES, LANES), lambda i, be, nu, nx: (i, 0)),
        scratch_shapes=[pltpu.VMEM((d, f2), F32), pltpu.VMEM((D_FF, d), F32),
                        pltpu.VMEM((d, f2), BF16), pltpu.VMEM((D_FF, d), BF16),
                        pltpu.SemaphoreType.DMA((2,))],
    )
    return pl.pallas_call(
        functools.partial(_expert_kernel, layer),
        grid_spec=grid_spec,
        out_shape=jax.ShapeDtypeStruct((n_rows * ROW_TILES, LANES), F32),
        compiler_params=_cp(("arbitrary",), 56),
        name="experts",
    )(block_e, n_used, next_e, xs, w_in, b_in.reshape(depth, ne, 1, f2), w_out,
      b_out.reshape(depth, ne, 1, d))


def _combine_kernel(final, dest_ref, gate_ref, x_ref, mod_ref, fg_ref, yb_ref, o_ref, g_buf, sem):
    tm = x_ref.shape[0]

    def row_copy(slot, d):
        return pltpu.make_async_copy(
            yb_ref.at[pl.ds(pl.multiple_of(d * ROW_TILES, ROW_TILES), ROW_TILES), :],
            g_buf.at[pl.ds(pl.multiple_of(slot * ROW_TILES, ROW_TILES), ROW_TILES), :], sem)

    def issue(r, carry):
        for k in range(TOP_K):
            row_copy(k * tm + r, dest_ref[r * TOP_K + k]).start(priority=k % 2)
        return carry

    lax.fori_loop(0, tm, issue, 0)
    pltpu.make_async_copy(yb_ref.at[pl.ds(0, TOP_K * tm * ROW_TILES), :], g_buf, sem).wait()

    gates = gate_ref[...]
    gate2 = mod_ref[0, 5:6, :]
    for c in range(ROW_TILES):
        y = jnp.zeros((tm, LANES), F32)
        for k in range(TOP_K):
            rows = g_buf[pl.ds(k * tm * ROW_TILES + c, tm, stride=ROW_TILES), :]
            y = y + gates[:, k:k + 1] * rows
        cs = slice(c * LANES, (c + 1) * LANES)
        o_ref[:, cs] = x_ref[:, cs] + gate2[:, cs] * y
    if final:
        o_ref[...] = _rms_rows(o_ref[...]) * fg_ref[...]


def _combine(yb, dest_flat, gates, x2, mod, final_g, final, seq):
    t, d = x2.shape
    tm = min(COMBINE_ROWS, seq)
    return pl.pallas_call(
        functools.partial(_combine_kernel, final),
        grid=(t // tm,),
        in_specs=[
            pl.BlockSpec((tm * TOP_K,), lambda i: (i,), memory_space=pltpu.SMEM),
            pl.BlockSpec((tm, LANES), lambda i: (i, 0)),
            pl.BlockSpec((tm, d), lambda i: (i, 0)),
            pl.BlockSpec((1, 6, d), lambda i: (i // (seq // tm), 0, 0)),
            pl.BlockSpec((1, d), lambda i: (0, 0)),
            pl.BlockSpec(memory_space=pl.ANY),
        ],
        out_specs=pl.BlockSpec((tm, d), lambda i: (i, 0)),
        out_shape=jax.ShapeDtypeStruct((t, d), F32),
        scratch_shapes=[pltpu.VMEM((TOP_K * tm * ROW_TILES, LANES), F32),
                        pltpu.SemaphoreType.DMA(())],
        compiler_params=_cp(("arbitrary",), 40),
        name="combine",
    )(dest_flat, gates, x2, mod, final_g.reshape(1, d), yb)


def _moe(h3, logits, x2, mod, layer, w_in, b_in, w_out, b_out, final_g, final, seq):
    t = x2.shape[0]
    tb = EXPERT_ROWS
    idx, gates, cnt = _route(logits)
    counts = cnt[0, :N_EXPERTS].astype(jnp.int32)
    nblk = (counts + tb - 1) // tb
    blk_end = jnp.cumsum(nblk)
    pad_start = (blk_end - nblk) * tb
    dest = pad_start[idx[:, :TOP_K]] + idx[:, TOP_K:2 * TOP_K]
    dest_flat = dest.reshape(t * TOP_K)
    n_blocks = (t * TOP_K) // tb + N_EXPERTS
    n_used = blk_end[-1:]
    last_e = jnp.max(jnp.where(nblk > 0, jnp.arange(N_EXPERTS, dtype=jnp.int32), 0))
    blk = jnp.arange(n_blocks, dtype=jnp.int32)
    block_e = jnp.minimum(
        jnp.sum((blk_end[None, :] <= blk[:, None]).astype(jnp.int32), axis=1), last_e)
    experts = jnp.arange(N_EXPERTS, dtype=jnp.int32)
    later = (experts[None, :] > block_e[:, None]) & (nblk[None, :] > 0)
    next_e = jnp.min(jnp.where(later, experts[None, :], N_EXPERTS), axis=1)
    next_e = jnp.where(next_e == N_EXPERTS, -1, next_e).astype(jnp.int32)
    fill_start = jnp.concatenate([pad_start + counts, blk_end[-1:] * tb]).astype(jnp.int32)
    fill_len = jnp.concatenate([nblk * tb - counts, (n_blocks - blk_end[-1:]) * tb]).astype(jnp.int32)
    xs = _dispatch(h3, dest_flat, fill_start, fill_len, n_blocks * tb)
    yb = _experts(xs, block_e, n_used.astype(jnp.int32), next_e, layer, w_in, b_in, w_out, b_out)
    return _combine(yb, dest_flat, gates, x2, mod, final_g, final, seq)


def kernel(x, c, positions, w_ada, b_ada, even_w_in, gla_w_gate, gla_b_gate, gla_norm_g,
           diff_lam_q1, diff_lam_k1, diff_lam_q2, diff_lam_k2, diff_norm_g, even_w_out,
           odd_w_in, sgu_ln_g, sgu_ln_b, sgu_w, sgu_b, odd_w_out,
           router_w, router_b, expert_w_in, expert_b_in, expert_w_out, expert_b_out,
           final_norm_g):
    bsz, seq, d = x.shape
    depth = w_ada.shape[0]
    t = bsz * seq
    mods = _modulation(c, w_ada, b_ada).reshape(depth, bsz, 6, d)
    cos, sin = _rope_tables(positions)
    x2 = x.reshape(t, d)
    for layer in range(depth):
        mod = mods[layer]
        j = layer // 2
        if layer % 2 == 0:
            z, f = _proj_even(x2, mod, even_w_in[j], gla_w_gate[j], gla_b_gate[j], cos, sin, seq)
            o_gla = _gla(z, f, gla_norm_g[j], bsz, seq)
            lam_init = 0.8 - 0.6 * math.exp(-0.3 * layer)
            o_diff = _diff_attn(z, diff_lam_q1[j], diff_lam_k1[j], diff_lam_q2[j], diff_lam_k2[j],
                                diff_norm_g[j], lam_init, bsz, seq)
            mixed, w_out = (o_gla, o_diff), even_w_out[j]
        else:
            z, f = _proj_odd(x2, mod, odd_w_in[j], sgu_ln_g[j], sgu_ln_b[j], cos, sin, seq)
            mixed, w_out = (_sgu_ret(z, f, sgu_w[j], sgu_b[j], bsz, seq),), odd_w_out[j]
        x2, h3, logits = _out_proj(mixed, w_out, x2, mod, router_w[layer], router_b[layer], seq)
        x2 = _moe(h3, logits, x2, mod, layer, expert_w_in, expert_b_in, expert_w_out,
                  expert_b_out, final_norm_g, layer == depth - 1, seq)
    return x2.reshape(bsz, seq, d)
```

```python
import functools
import math

import jax
import jax.numpy as jnp
from jax import lax
from jax.experimental import pallas as pl
from jax.experimental.pallas import tpu as pltpu

F32 = jnp.float32
BF16 = jnp.bfloat16

D_MODEL = 1024
EPS = 1e-6
ROPE_THETA = 10000.0
ROPE_HALF = 32

GLA_HEADS = 4
GLA_DK = 64
GLA_DV = 128
GLA_RANK = 16
GLA_CHUNK = 64
GLA_GATE_NORMALIZER = 16.0
GLA_QK_W = GLA_HEADS * GLA_DK
GLA_V_W = GLA_HEADS * GLA_DV

DIFF_HEADS = 4
DIFF_D = 64
DIFF_DV = 128
DIFF_QK_W = DIFF_HEADS * 2 * DIFF_D
DIFF_V_W = DIFF_HEADS * DIFF_DV

SGU_GROUPS = 4
SGU_CH = 128
SGU_CHUNK = 128
SGU_W = SGU_GROUPS * SGU_CH

RET_HEADS = 4
RET_DK = 64
RET_DV = 128
RET_CHUNK = 128
RET_QK_W = RET_HEADS * RET_DK
RET_V_W = RET_HEADS * RET_DV

N_EXPERTS = 32
TOP_K = 4
D_FF = D_MODEL
SWIGLU_ALPHA = 1.702
SWIGLU_LIMIT = 7.0

LANES = 128
SUBLANES = 8
ROW_TILES = D_MODEL // LANES

PROJ_ROWS = 512
GLA_ROWS = 256
ATT_Q_ROWS = 256
ATT_HEADS_PER_STEP = 2
ROUTE_ROWS = 512
DISPATCH_ROWS = 512
ISSUE_UNROLL = 4
FILL_SHIFT = 6
FILL_ROWS = 1 << FILL_SHIFT
EXPERT_ROWS = 512
COMBINE_ROWS = 256
MOD_COLS = 1536

MIB = 1024 * 1024


def _cp(semantics, vmem_mib):
    return pltpu.CompilerParams(dimension_semantics=semantics, vmem_limit_bytes=vmem_mib * MIB)


def _dot(a, b):
    return jnp.dot(a, b, preferred_element_type=F32)


def _dot_nt(a, b):
    return lax.dot_general(a, b, (((1,), (1,)), ((), ())), preferred_element_type=F32)


def _dot_tn(a, b):
    return lax.dot_general(a, b, (((0,), (0,)), ((), ())), preferred_element_type=F32)


def _rms_rows(x):
    return x * lax.rsqrt(jnp.mean(x * x, axis=-1, keepdims=True) + EPS)


def _silu(x):
    return x * jax.nn.sigmoid(x)


def _rope_chunk(v, cos, sin, lo_half):
    rot = jnp.where(lo_half, -pltpu.roll(v, 96, 1), pltpu.roll(v, 32, 1))
    return v * cos + rot * sin


def _mod_kernel(c_ref, w_ref, b_ref, o_ref):
    c = c_ref[...]
    ca = _silu(c).astype(BF16)
    o_ref[0] = _dot(ca, w_ref[0].astype(BF16)) + b_ref[0]


def _modulation(c, w_ada, b_ada):
    depth, d, n = w_ada.shape
    bsz = c.shape[0]
    return pl.pallas_call(
        _mod_kernel,
        grid=(depth, n // MOD_COLS),
        in_specs=[
            pl.BlockSpec((bsz, d), lambda l, j: (0, 0)),
            pl.BlockSpec((1, d, MOD_COLS), lambda l, j: (l, 0, j)),
            pl.BlockSpec((1, 1, MOD_COLS), lambda l, j: (l, 0, j)),
        ],
        out_specs=pl.BlockSpec((1, bsz, MOD_COLS), lambda l, j: (l, 0, j)),
        out_shape=jax.ShapeDtypeStruct((depth, bsz, n), F32),
        compiler_params=_cp(("arbitrary", "arbitrary"), 40),
        name="adaln_mod",
    )(c, w_ada, b_ada.reshape(depth, 1, n))


def _rope_table_kernel(p_ref, f_ref, c_ref, s_ref):
    ang = p_ref[...].astype(F32) * f_ref[...]
    c_ref[...] = jnp.cos(ang)
    s_ref[...] = jnp.sin(ang)


def _rope_tables(positions):
    t = positions.size
    per_row = LANES // ROPE_HALF
    rows = t // per_row
    pos_d = jnp.repeat(positions.reshape(rows, per_row), ROPE_HALF, axis=1)
    inv_freq = ROPE_THETA ** (-jnp.arange(ROPE_HALF, dtype=F32) / ROPE_HALF)
    freq_d = jnp.tile(inv_freq, per_row).reshape(1, LANES)
    tr = min(512, rows)
    cos_d, sin_d = pl.pallas_call(
        _rope_table_kernel,
        grid=(rows // tr,),
        in_specs=[pl.BlockSpec((tr, LANES), lambda i: (i, 0)),
                  pl.BlockSpec((1, LANES), lambda i: (0, 0))],
        out_specs=[pl.BlockSpec((tr, LANES), lambda i: (i, 0))] * 2,
        out_shape=[jax.ShapeDtypeStruct((rows, LANES), F32)] * 2,
        compiler_params=_cp(("arbitrary",), 32),
        name="rope_tables",
    )(pos_d, freq_d)
    cos = jnp.tile(cos_d.reshape(t, ROPE_HALF), (1, per_row))
    sin = jnp.tile(sin_d.reshape(t, ROPE_HALF), (1, per_row))
    return cos, sin


def _modulated_rms(x_ref, mod_ref, which):
    x = x_ref[...]
    shift = mod_ref[0, 3 * which:3 * which + 1, :]
    scale = mod_ref[0, 3 * which + 1:3 * which + 2, :]
    return _rms_rows(x) * (1.0 + scale) + shift


EVEN_Z = (GLA_V_W, GLA_V_W, DIFF_QK_W, DIFF_QK_W, DIFF_V_W)
EVEN_Z_W = sum(EVEN_Z)
EVEN_F_W = 3 * GLA_QK_W
EVEN_W_COLS = EVEN_Z_W + 2 * GLA_QK_W + LANES


def _proj_even_kernel(x_ref, mod_ref, w_ref, wg_ref, bg_ref, cos_ref, sin_ref, z_ref, f_ref):
    hb = _modulated_rms(x_ref, mod_ref, 0).astype(BF16)

    def sec(a, b):
        return _dot(hb, w_ref[:, a:b])

    for a, b in ((0, 512), (512, 1024), (2048, 2560)):
        z_ref[:, a:b] = sec(a, b).astype(BF16)
    cos = cos_ref[...]
    sin = sin_ref[...]
    lo_half = (lax.broadcasted_iota(jnp.int32, cos.shape, 1) % 64) < ROPE_HALF
    q_scale = DIFF_D ** -0.5
    for a, scl in ((1024, q_scale), (1536, 1.0)):
        for c in range(DIFF_QK_W // LANES):
            v = sec(a + c * LANES, a + (c + 1) * LANES)
            z_ref[:, a + c * LANES:a + (c + 1) * LANES] = (
                _rope_chunk(v, cos, sin, lo_half) * scl).astype(BF16)
    f_ref[:, 0:256] = sec(2560, 2816) * (GLA_DK ** -0.5)
    f_ref[:, 256:512] = sec(2816, 3072)
    gr = sec(3072, 3200).astype(BF16)
    pre = _dot(gr, wg_ref[...]) + bg_ref[...]
    log_sig = jnp.minimum(pre, 0.0) - jnp.log1p(jnp.exp(-jnp.abs(pre)))
    f_ref[:, 512:768] = log_sig / GLA_GATE_NORMALIZER


def _proj_even(x2, mod, w_in, w_gate, b_gate, cos, sin, seq):
    t, d = x2.shape
    tm = min(PROJ_ROWS, seq)
    gq, gk, gv, gr, gg, dq, dk, dv = jnp.split(
        w_in, [256, 512, 1024, 1040, 1552, 2064, 2576], axis=1)
    gr_pad = jnp.pad(gr, ((0, 0), (0, LANES - GLA_RANK)))
    w_cat = jnp.concatenate([gv, gg, dq, dk, dv, gq, gk, gr_pad], axis=1).astype(BF16)
    wg_pad = jnp.pad(w_gate, ((0, LANES - GLA_RANK), (0, 0))).astype(BF16)
    return pl.pallas_call(
        _proj_even_kernel,
        grid=(t // tm,),
        in_specs=[
            pl.BlockSpec((tm, d), lambda i: (i, 0)),
            pl.BlockSpec((1, 6, d), lambda i: (i // (seq // tm), 0, 0)),
            pl.BlockSpec((d, EVEN_W_COLS), lambda i: (0, 0)),
            pl.BlockSpec((LANES, GLA_QK_W), lambda i: (0, 0)),
            pl.BlockSpec((1, GLA_QK_W), lambda i: (0, 0)),
            pl.BlockSpec((tm, LANES), lambda i: (i, 0)),
            pl.BlockSpec((tm, LANES), lambda i: (i, 0)),
        ],
        out_specs=[pl.BlockSpec((tm, EVEN_Z_W), lambda i: (i, 0)),
                   pl.BlockSpec((tm, EVEN_F_W), lambda i: (i, 0))],
        out_shape=[jax.ShapeDtypeStruct((t, EVEN_Z_W), BF16),
                   jax.ShapeDtypeStruct((t, EVEN_F_W), F32)],
        compiler_params=_cp(("arbitrary",), 48),
        name="proj_even",
    )(x2, mod, w_cat, wg_pad, b_gate.reshape(1, GLA_QK_W), cos, sin)


ODD_Z_W = 2 * SGU_W + 2 * RET_V_W
ODD_F_W = 2 * RET_QK_W
ODD_W_COLS = ODD_Z_W + ODD_F_W


def _gelu_exact(x):
    return 0.5 * x * (1.0 + lax.erf(x * (2.0 ** -0.5)))


def _proj_odd_kernel(x_ref, mod_ref, w_ref, lng_ref, lnb_ref, cos_ref, sin_ref, z_ref, f_ref):
    hb = _modulated_rms(x_ref, mod_ref, 0).astype(BF16)

    def sec(a, b):
        return _dot(hb, w_ref[:, a:b])

    z_ref[:, 0:512] = _gelu_exact(sec(0, 512)).astype(BF16)
    sv = _gelu_exact(sec(512, 1024))
    mu = jnp.mean(sv, axis=-1, keepdims=True)
    cen = sv - mu
    var = jnp.mean(cen * cen, axis=-1, keepdims=True)
    z_ref[:, 512:1024] = (cen * lax.rsqrt(var + EPS) * lng_ref[...] + lnb_ref[...]).astype(BF16)
    for a, b in ((1024, 1536), (1536, 2048)):
        z_ref[:, a:b] = sec(a, b).astype(BF16)
    cos = cos_ref[...]
    sin = sin_ref[...]
    lo_half = (lax.broadcasted_iota(jnp.int32, cos.shape, 1) % 64) < ROPE_HALF
    k_scale = RET_DK ** -0.5
    for a, scl in ((0, 1.0), (256, k_scale)):
        for c in range(RET_QK_W // LANES):
            v = sec(ODD_Z_W + a + c * LANES, ODD_Z_W + a + (c + 1) * LANES)
            f_ref[:, a + c * LANES:a + (c + 1) * LANES] = _rope_chunk(v, cos, sin, lo_half) * scl


def _proj_odd(x2, mod, w_in, ln_g, ln_b, cos, sin, seq):
    t, d = x2.shape
    tm = min(PROJ_ROWS, seq)
    su, sv, rq, rk, rv, rg = jnp.split(w_in, [512, 1024, 1280, 1536, 2048], axis=1)
    w_cat = jnp.concatenate([su, sv, rv, rg, rq, rk], axis=1).astype(BF16)
    return pl.pallas_call(
        _proj_odd_kernel,
        grid=(t // tm,),
        in_specs=[
            pl.BlockSpec((tm, d), lambda i: (i, 0)),
            pl.BlockSpec((1, 6, d), lambda i: (i // (seq // tm), 0, 0)),
            pl.BlockSpec((d, ODD_W_COLS), lambda i: (0, 0)),
            pl.BlockSpec((1, SGU_W), lambda i: (0, 0)),
            pl.BlockSpec((1, SGU_W), lambda i: (0, 0)),
            pl.BlockSpec((tm, LANES), lambda i: (i, 0)),
            pl.BlockSpec((tm, LANES), lambda i: (i, 0)),
        ],
        out_specs=[pl.BlockSpec((tm, ODD_Z_W), lambda i: (i, 0)),
                   pl.BlockSpec((tm, ODD_F_W), lambda i: (i, 0))],
        out_shape=[jax.ShapeDtypeStruct((t, ODD_Z_W), BF16),
                   jax.ShapeDtypeStruct((t, ODD_F_W), F32)],
        compiler_params=_cp(("arbitrary",), 48),
        name="proj_odd",
    )(x2, mod, w_cat, ln_g.reshape(1, SGU_W), ln_b.reshape(1, SGU_W), cos, sin)


def _gla_kernel(q_ref, k_ref, la_ref, v_ref, gg_ref, g_ref, o_ref, st_ref):
    @pl.when(pl.program_id(1) == 0)
    def _():
        st_ref[...] = jnp.zeros_like(st_ref)

    c = GLA_CHUNK
    rows = q_ref.shape[0]
    r_i = lax.broadcasted_iota(jnp.int32, (c, c), 0)
    c_i = lax.broadcasted_iota(jnp.int32, (c, c), 1)
    tril = (r_i >= c_i).astype(BF16)
    r2 = lax.broadcasted_iota(jnp.int32, (2 * c, c), 0) % c
    c2 = lax.broadcasted_iota(jnp.int32, (2 * c, c), 1)
    causal2 = r2 >= c2
    lane = lax.broadcasted_iota(jnp.int32, (c, LANES), 1)
    lo = lane < GLA_DK
    lane_s = lax.broadcasted_iota(jnp.int32, (LANES, LANES), 1) < GLA_DK
    g_row = g_ref[...]
    states = [st_ref[0], st_ref[1]]
    for j in range(rows // c):
        rs = slice(j * c, (j + 1) * c)
        la = la_ref[rs, :]
        la_hi = la.astype(BF16)
        la_lo = (la - la_hi.astype(F32)).astype(BF16)
        b = _dot(tril, la_hi) + _dot(tril, la_lo)
        b_last = b[c - 1:c, :]
        qd = q_ref[rs, :] * jnp.exp(b)
        kk = k_ref[rs, :]
        ki = kk * jnp.exp(-b)
        kd = kk * jnp.exp(b_last - b)
        dec = jnp.exp(b_last)
        for p in range(GLA_HEADS // 2):
            ls = slice(p * LANES, (p + 1) * LANES)
            qd_p = qd[:, ls]
            qm = (jnp.where(lo, qd_p, 0.0).astype(BF16), jnp.where(lo, 0.0, qd_p).astype(BF16))
            s2 = _dot_nt(jnp.concatenate(qm, axis=0), ki[:, ls].astype(BF16))
            s2 = jnp.where(causal2, s2, 0.0).astype(BF16)
            kd_p = kd[:, ls].astype(BF16)
            st_b = states[p].astype(BF16)
            new = []
            for hh in range(2):
                h = 2 * p + hh
                hs = slice(h * GLA_DV, (h + 1) * GLA_DV)
                v_h = v_ref[rs, hs]
                o = _dot(s2[hh * c:(hh + 1) * c, :], v_h) + _dot_nt(qm[hh], st_b)
                o = _rms_rows(o) * g_row
                gate = _silu(gg_ref[rs, hs].astype(F32))
                o_ref[rs, hs] = (o * gate).astype(BF16)
                new.append(_dot_tn(v_h, kd_p))
            states[p] = dec[:, ls] * states[p] + jnp.where(lane_s, new[0], new[1])
    st_ref[0] = states[0]
    st_ref[1] = states[1]


def _gla(z, f, norm_g, bsz, seq):
    t = z.shape[0]
    tc = min(GLA_ROWS, seq)
    nc = seq // tc
    return pl.pallas_call(
        _gla_kernel,
        grid=(bsz, nc),
        in_specs=[
            pl.BlockSpec((tc, GLA_QK_W), lambda b, i: (b * nc + i, 0)),
            pl.BlockSpec((tc, GLA_QK_W), lambda b, i: (b * nc + i, 1)),
            pl.BlockSpec((tc, GLA_QK_W), lambda b, i: (b * nc + i, 2)),
            pl.BlockSpec((tc, GLA_V_W), lambda b, i: (b * nc + i, 0)),
            pl.BlockSpec((tc, GLA_V_W), lambda b, i: (b * nc + i, 1)),
            pl.BlockSpec((1, GLA_DV), lambda b, i: (0, 0)),
        ],
        out_specs=pl.BlockSpec((tc, GLA_V_W), lambda b, i: (b * nc + i, 0)),
        out_shape=jax.ShapeDtypeStruct((t, GLA_V_W), BF16),
        scratch_shapes=[pltpu.VMEM((2, LANES, LANES), F32)],
        compiler_params=_cp(("arbitrary", "arbitrary"), 32),
        name="gla",
    )(f, f, f, z, z, norm_g.reshape(1, GLA_DV))


def _diff_attn_kernel(lam_init, q_ref, k_ref, v_ref, lq1_ref, lk1_ref, lq2_ref, lk2_ref, g_ref,
                      o_ref, vt_ref):
    tq = q_ref.shape[0]
    seq = k_ref.shape[0]
    n_heads = q_ref.shape[1] // LANES
    qi = pl.program_id(2)

    @pl.when(qi == 0)
    def _():
        for hh in range(n_heads):
            for cb in range(seq // tq):
                blk = v_ref[cb * tq:(cb + 1) * tq, hh * LANES:(hh + 1) * LANES]
                vt_ref[hh, :, cb * tq:(cb + 1) * tq] = blk.astype(F32).T.astype(BF16)

    lo = lax.broadcasted_iota(jnp.int32, (tq, LANES), 1) < DIFF_D
    qqs = []
    for hh in range(n_heads):
        q = q_ref[:, hh * LANES:(hh + 1) * LANES]
        zero = jnp.zeros_like(q)
        qqs.append(jnp.concatenate([jnp.where(lo, q, zero), jnp.where(lo, zero, q)], axis=0))

    def scores(j, hh):
        start = pl.multiple_of(j * tq, tq)
        kj = k_ref[pl.ds(start, tq), hh * LANES:(hh + 1) * LANES]
        return _dot_nt(kj, qqs[hh])

    def update(j, state, s, hh, masked):
        m, l, acc = state
        start = pl.multiple_of(j * tq, tq)
        vtj = vt_ref[hh, :, pl.ds(start, tq)]
        if masked:
            kv = lax.broadcasted_iota(jnp.int32, s.shape, 0)
            qq_pos = lax.broadcasted_iota(jnp.int32, s.shape, 1) % tq
            s = jnp.where(kv <= qq_pos, s, -jnp.inf)
        m_new = jnp.maximum(m, jnp.max(s, axis=0, keepdims=True))
        alpha = jnp.exp(m - m_new)
        p = jnp.exp(s - m_new)
        l = alpha * l + jnp.sum(p, axis=0, keepdims=True)
        acc = alpha * acc + _dot(vtj, p.astype(BF16))
        return m_new, l, acc

    def body(j, carries):
        nxt = [scores(j + 1, hh) for hh in range(n_heads)]
        return tuple((update(j, carries[hh][0], carries[hh][1], hh, False), nxt[hh])
                     for hh in range(n_heads))

    init = tuple(((jnp.full((1, 2 * tq), -jnp.inf, F32), jnp.zeros((1, 2 * tq), F32),
                   jnp.zeros((DIFF_DV, 2 * tq), F32)), scores(0, hh)) for hh in range(n_heads))
    carries = lax.fori_loop(0, qi, body, init)
    carries = tuple(update(qi, carries[hh][0], carries[hh][1], hh, True)
                    for hh in range(n_heads))
    lam = (jnp.exp(jnp.sum(lq1_ref[...] * lk1_ref[...], axis=-1, keepdims=True))
           - jnp.exp(jnp.sum(lq2_ref[...] * lk2_ref[...], axis=-1, keepdims=True)) + lam_init)
    for hh in range(n_heads):
        _, l, acc = carries[hh]
        o12 = acc / l
        o = o12[:, :tq] - lam * o12[:, tq:]
        o = o * lax.rsqrt(jnp.mean(o * o, axis=0, keepdims=True) + EPS)
        o = o * g_ref[...] * (1.0 - lam_init)
        o_ref[:, hh * DIFF_DV:(hh + 1) * DIFF_DV] = o.T.astype(BF16)


def _diff_attn(z, lq1, lk1, lq2, lk2, norm_g, lam_init, bsz, seq):
    t = z.shape[0]
    tq = min(ATT_Q_ROWS, seq)
    nq = seq // tq
    hw = ATT_HEADS_PER_STEP * LANES
    qb, kb, vb = 1024 // hw, 1536 // hw, 2048 // hw
    small = pl.BlockSpec((1, DIFF_D), lambda b, h, i: (0, 0))
    return pl.pallas_call(
        functools.partial(_diff_attn_kernel, lam_init),
        grid=(bsz, DIFF_HEADS // ATT_HEADS_PER_STEP, nq),
        in_specs=[
            pl.BlockSpec((tq, hw), lambda b, h, i: (b * nq + i, qb + h)),
            pl.BlockSpec((seq, hw), lambda b, h, i: (b, kb + h)),
            pl.BlockSpec((seq, hw), lambda b, h, i: (b, vb + h)),
            small, small, small, small,
            pl.BlockSpec((DIFF_DV, 1), lambda b, h, i: (0, 0)),
        ],
        out_specs=pl.BlockSpec((tq, hw), lambda b, h, i: (b * nq + i, h)),
        out_shape=jax.ShapeDtypeStruct((t, DIFF_V_W), BF16),
        scratch_shapes=[pltpu.VMEM((ATT_HEADS_PER_STEP, DIFF_DV, seq), BF16)],
        compiler_params=_cp(("arbitrary", "arbitrary", "arbitrary"), 32),
        name="diff_attn",
    )(z, z, z, lq1.reshape(1, DIFF_D), lk1.reshape(1, DIFF_D), lq2.reshape(1, DIFF_D),
      lk2.reshape(1, DIFF_D), norm_g.reshape(DIFF_DV, 1))


def _sgu_ret_kernel(su_ref, sv_ref, rv_ref, rg_ref, q_ref, k_ref, ws_ref, bs_ref, o_ref, st_ref):
    @pl.when(pl.program_id(1) == 0)
    def _():
        st_ref[...] = jnp.zeros_like(st_ref)

    c = RET_CHUNK
    row = lax.broadcasted_iota(jnp.int32, (c, c), 0)
    col = lax.broadcasted_iota(jnp.int32, (c, c), 1)
    causal = row >= col
    for g in range(SGU_GROUPS):
        gs = slice(g * SGU_CH, (g + 1) * SGU_CH)
        w = jnp.where(causal, ws_ref[g], 0.0).astype(BF16)
        s = _dot(w, sv_ref[:, gs]) + bs_ref[g]
        o_ref[:, gs] = (su_ref[:, gs].astype(F32) * s).astype(BF16)
    log_g = [math.log(1.0 - 2.0 ** (-5.0 - h)) for h in range(RET_HEADS)]
    lo = col < RET_DK
    rel = (row - col).astype(F32)
    pos = row.astype(F32)
    for p in range(RET_HEADS // 2):
        ls = slice(p * LANES, (p + 1) * LANES)
        lg = jnp.where(lo, log_g[2 * p], log_g[2 * p + 1])
        q_p = q_ref[:, ls]
        k_p = k_ref[:, ls]
        qm = (jnp.where(lo, q_p, 0.0).astype(BF16), jnp.where(lo, 0.0, q_p).astype(BF16))
        s2 = _dot_nt(jnp.concatenate(qm, axis=0), k_p.astype(BF16))
        qd = q_p * jnp.exp(lg * (pos + 1.0))
        qdm = (jnp.where(lo, qd, 0.0).astype(BF16), jnp.where(lo, 0.0, qd).astype(BF16))
        kd = (k_p * jnp.exp(lg * (c - 1.0 - pos))).astype(BF16)
        st = st_ref[p]
        st_b = st.astype(BF16)
        new = []
        for hh in range(2):
            h = 2 * p + hh
            hs = slice(h * RET_DV, (h + 1) * RET_DV)
            decay = jnp.where(causal, jnp.exp(log_g[h] * jnp.maximum(rel, 0.0)), 0.0)
            s_h = (s2[hh * c:(hh + 1) * c, :] * decay).astype(BF16)
            v_h = rv_ref[:, hs]
            o = _dot(s_h, v_h) + _dot_nt(qdm[hh], st_b)
            gate = _silu(rg_ref[:, hs].astype(F32))
            o_ref[:, SGU_W + h * RET_DV:SGU_W + (h + 1) * RET_DV] = (
                _rms_rows(o) * gate).astype(BF16)
            new.append(_dot_tn(v_h, kd))
        st_ref[p] = jnp.exp(lg * float(c)) * st + jnp.where(lo, new[0], new[1])


def _sgu_ret(z, f, w_s, b_s, bsz, seq):
    t = z.shape[0]
    c = RET_CHUNK
    nc = seq // c
    return pl.pallas_call(
        _sgu_ret_kernel,
        grid=(bsz, nc),
        in_specs=[
            pl.BlockSpec((c, SGU_W), lambda b, i: (b * nc + i, 0)),
            pl.BlockSpec((c, SGU_W), lambda b, i: (b * nc + i, 1)),
            pl.BlockSpec((c, RET_V_W), lambda b, i: (b * nc + i, 2)),
            pl.BlockSpec((c, RET_V_W), lambda b, i: (b * nc + i, 3)),
            pl.BlockSpec((c, RET_QK_W), lambda b, i: (b * nc + i, 0)),
            pl.BlockSpec((c, RET_QK_W), lambda b, i: (b * nc + i, 1)),
            pl.BlockSpec((SGU_GROUPS, c, c), lambda b, i: (0, 0, 0)),
            pl.BlockSpec((SGU_GROUPS, c, 1), lambda b, i: (0, 0, 0)),
        ],
        out_specs=pl.BlockSpec((c, SGU_W + RET_V_W), lambda b, i: (b * nc + i, 0)),
        out_shape=jax.ShapeDtypeStruct((t, SGU_W + RET_V_W), BF16),
        scratch_shapes=[pltpu.VMEM((2, LANES, LANES), F32)],
        compiler_params=_cp(("arbitrary", "arbitrary"), 32),
        name="sgu_retention",
    )(z, z, z, z, f, f, w_s, b_s.reshape(SGU_GROUPS, c, 1))


def _out_proj_kernel(n_in, *refs):
    o_refs = refs[:n_in]
    w_ref, x_ref, mod_ref, rw_ref, rb_ref, xn_ref, h_ref, lg_ref = refs[n_in:]
    tm = x_ref.shape[0]
    k_each = D_MODEL // n_in
    y = _dot(o_refs[0][...], w_ref[0:k_each, :])
    for n in range(1, n_in):
        y = y + _dot(o_refs[n][...], w_ref[n * k_each:(n + 1) * k_each, :])
    gate1 = mod_ref[0, 2:3, :]
    xn = x_ref[...] + gate1 * y
    xn_ref[...] = xn
    h = _rms_rows(xn) * (1.0 + mod_ref[0, 4:5, :]) + mod_ref[0, 3:4, :]
    for c in range(ROW_TILES):
        h_ref[pl.ds(c, tm, stride=ROW_TILES), :] = h[:, c * LANES:(c + 1) * LANES]
    lg_ref[...] = _dot(h.astype(BF16), rw_ref[...]) + rb_ref[...]


def _out_proj(mixed, w_out, x2, mod, router_w, router_b, seq):
    t, d = x2.shape
    tm = min(PROJ_ROWS, seq)
    n_in = len(mixed)
    k_each = d // n_in
    rw = jnp.pad(router_w, ((0, 0), (0, LANES - N_EXPERTS))).astype(BF16)
    rb = jnp.pad(router_b, (0, LANES - N_EXPERTS)).reshape(1, LANES)
    return pl.pallas_call(
        functools.partial(_out_proj_kernel, n_in),
        grid=(t // tm,),
        in_specs=[pl.BlockSpec((tm, k_each), lambda i: (i, 0)) for _ in mixed] + [
            pl.BlockSpec((d, d), lambda i: (0, 0)),
            pl.BlockSpec((tm, d), lambda i: (i, 0)),
            pl.BlockSpec((1, 6, d), lambda i: (i // (seq // tm), 0, 0)),
            pl.BlockSpec((d, LANES), lambda i: (0, 0)),
            pl.BlockSpec((1, LANES), lambda i: (0, 0)),
        ],
        out_specs=[pl.BlockSpec((tm, d), lambda i: (i, 0)),
                   pl.BlockSpec((tm * ROW_TILES, LANES), lambda i: (i, 0)),
                   pl.BlockSpec((tm, LANES), lambda i: (i, 0))],
        out_shape=[jax.ShapeDtypeStruct((t, d), F32),
                   jax.ShapeDtypeStruct((t * ROW_TILES, LANES), F32),
                   jax.ShapeDtypeStruct((t, LANES), F32)],
        compiler_params=_cp(("arbitrary",), 48),
        name="out_proj",
    )(*mixed, w_out.astype(BF16), x2, mod, rw, rb)


def _route_kernel(lg_ref, idx_ref, gate_ref, cnt_ref, run_ref):
    @pl.when(pl.program_id(0) == 0)
    def _():
        run_ref[...] = jnp.zeros_like(run_ref)

    tm = lg_ref.shape[0]
    lane = lax.broadcasted_iota(jnp.int32, (tm, LANES), 1)
    neg = -jnp.inf
    l = jnp.where(lane < N_EXPERTS, lg_ref[...], neg)
    vals, hots = [], []
    for _ in range(TOP_K):
        m = jnp.max(l, axis=-1, keepdims=True)
        first = jnp.min(jnp.where(l == m, lane, LANES), axis=-1, keepdims=True)
        hot = lane == first
        vals.append(m)
        hots.append(hot)
        l = jnp.where(hot, neg, l)
    sel = hots[0] | hots[1] | hots[2] | hots[3]
    ex = [jnp.exp(v - vals[0]) for v in vals]
    denom = ex[0] + ex[1] + ex[2] + ex[3]
    r_i = lax.broadcasted_iota(jnp.int32, (tm, tm), 0)
    c_i = lax.broadcasted_iota(jnp.int32, (tm, tm), 1)
    before = (r_i > c_i).astype(BF16)
    sel_b = sel.astype(BF16)
    run = run_ref[0:1, :]
    cum = _dot(before, sel_b) + run
    idx_out = jnp.zeros((tm, LANES), jnp.int32)
    gate_out = jnp.zeros((tm, LANES), F32)
    for k in range(TOP_K):
        e_k = jnp.min(jnp.where(hots[k], lane, LANES), axis=-1, keepdims=True)
        rank_k = jnp.sum(jnp.where(hots[k], cum, 0.0), axis=-1, keepdims=True).astype(jnp.int32)
        idx_out = jnp.where(lane == k, e_k, idx_out)
        idx_out = jnp.where(lane == TOP_K + k, rank_k, idx_out)
        gate_out = jnp.where(lane == k, ex[k] / denom, gate_out)
    idx_ref[...] = idx_out
    gate_ref[...] = gate_out
    total = run + jnp.sum(sel.astype(F32), axis=0, keepdims=True)
    run_ref[...] = jnp.broadcast_to(total, run_ref.shape)
    cnt_ref[...] = jnp.broadcast_to(total, cnt_ref.shape)


def _route(logits):
    t = logits.shape[0]
    tm = min(ROUTE_ROWS, t)
    return pl.pallas_call(
        _route_kernel,
        grid=(t // tm,),
        in_specs=[pl.BlockSpec((tm, LANES), lambda i: (i, 0))],
        out_specs=[pl.BlockSpec((tm, LANES), lambda i: (i, 0)),
                   pl.BlockSpec((tm, LANES), lambda i: (i, 0)),
                   pl.BlockSpec((SUBLANES, LANES), lambda i: (0, 0))],
        out_shape=[jax.ShapeDtypeStruct((t, LANES), jnp.int32),
                   jax.ShapeDtypeStruct((t, LANES), F32),
                   jax.ShapeDtypeStruct((SUBLANES, LANES), F32)],
        scratch_shapes=[pltpu.VMEM((SUBLANES, LANES), F32)],
        compiler_params=_cp(("arbitrary",), 32),
        name="route",
    )(logits)


def _dispatch_kernel(fs_ref, fl_ref, h_ref, dest_ref, xs_ref, zero_buf, sem, zsem):
    tm = h_ref.shape[0] // ROW_TILES
    n_fill = fs_ref.shape[0]

    def rows(ref, start, n):
        return ref.at[pl.ds(pl.multiple_of(start * ROW_TILES, ROW_TILES), n * ROW_TILES), :]

    def fill_copies(f, wait):
        start, n = fs_ref[f], fl_ref[f]
        n_chunks = lax.shift_right_logical(n, FILL_SHIFT)
        tail = start + n_chunks * FILL_ROWS

        def chunk(j, carry):
            cp = pltpu.make_async_copy(zero_buf, rows(xs_ref, start + j * FILL_ROWS, FILL_ROWS), zsem)
            cp.wait() if wait else cp.start()
            return carry

        def single(j, carry):
            cp = pltpu.make_async_copy(rows(zero_buf, 0, 1), rows(xs_ref, tail + j, 1), zsem)
            cp.wait() if wait else cp.start()
            return carry

        lax.fori_loop(0, n_chunks, chunk, 0)
        lax.fori_loop(0, n - n_chunks * FILL_ROWS, single, 0)

    @pl.when(pl.program_id(0) == 0)
    def _():
        zero_buf[...] = jnp.zeros_like(zero_buf)
        lax.fori_loop(0, n_fill, lambda f, c: (fill_copies(f, False), c)[1], 0)

    def issue(rb, carry):
        for rr in range(ISSUE_UNROLL):
            r = rb * ISSUE_UNROLL + rr
            for k in range(TOP_K):
                pltpu.make_async_copy(rows(h_ref, r, 1), rows(xs_ref, dest_ref[r * TOP_K + k], 1),
                                      sem).start(priority=k % 2)
        return carry

    lax.fori_loop(0, tm // ISSUE_UNROLL, issue, 0)
    for _ in range(TOP_K):
        pltpu.make_async_copy(h_ref, rows(xs_ref, 0, tm), sem).wait()

    @pl.when(pl.program_id(0) == 0)
    def _():
        lax.fori_loop(0, n_fill, lambda f, c: (fill_copies(f, True), c)[1], 0)


def _dispatch(h3, dest_flat, fill_start, fill_len, n_rows):
    t = h3.shape[0] // ROW_TILES
    tm = min(DISPATCH_ROWS, t)
    grid_spec = pltpu.PrefetchScalarGridSpec(
        num_scalar_prefetch=2,
        grid=(t // tm,),
        in_specs=[pl.BlockSpec((tm * ROW_TILES, LANES), lambda i, fs, fl: (i, 0)),
                  pl.BlockSpec((tm * TOP_K,), lambda i, fs, fl: (i,), memory_space=pltpu.SMEM)],
        out_specs=pl.BlockSpec(memory_space=pl.ANY),
        scratch_shapes=[pltpu.VMEM((FILL_ROWS * ROW_TILES, LANES), F32),
                        pltpu.SemaphoreType.DMA(()), pltpu.SemaphoreType.DMA(())],
    )
    return pl.pallas_call(
        _dispatch_kernel,
        grid_spec=grid_spec,
        out_shape=jax.ShapeDtypeStruct((n_rows * ROW_TILES, LANES), F32),
        compiler_params=_cp(("arbitrary",), 32),
        name="dispatch",
    )(fill_start, fill_len, h3, dest_flat)


def _expert_kernel(layer, be_ref, nu_ref, nx_ref, xs_ref, wi_hbm, bi_ref, wo_hbm, bo_ref, y_ref,
                   wi_st, wo_st, wi_b, wo_b, sems):
    i = pl.program_id(0)
    tb = xs_ref.shape[0] // ROW_TILES
    e = be_ref[i]
    fresh = jnp.logical_or(i == 0, e != be_ref[jnp.maximum(i - 1, 0)])
    used = i < nu_ref[0]

    def fetch(ex):
        return (pltpu.make_async_copy(wi_hbm.at[layer, ex], wi_st, sems.at[0]),
                pltpu.make_async_copy(wo_hbm.at[layer, ex], wo_st, sems.at[1]))

    @pl.when(i == 0)
    def _():
        for cp in fetch(e):
            cp.start()

    @pl.when(jnp.logical_and(fresh, used))
    def _():
        for cp in fetch(e):
            cp.wait()
        wi_b[...] = wi_st[...].astype(BF16)
        wo_b[...] = wo_st[...].astype(BF16)

        @pl.when(nx_ref[i] >= 0)
        def _():
            for cp in fetch(nx_ref[i]):
                cp.start()

    @pl.when(used)
    def _():
        x = jnp.concatenate(
            [xs_ref[pl.ds(c, tb, stride=ROW_TILES), :] for c in range(ROW_TILES)],
            axis=1).astype(BF16)
        y = jnp.zeros((tb, D_MODEL), F32) + bo_ref[0, 0]
        half = 512
        for j in range(D_FF // half):
            a, b = j * half, (j + 1) * half
            glu = _dot(x, wi_b[:, a:b]) + bi_ref[0, 0, :, a:b]
            lin = _dot(x, wi_b[:, D_FF + a:D_FF + b]) + bi_ref[0, 0, :, D_FF + a:D_FF + b]
            glu = jnp.minimum(glu, SWIGLU_LIMIT)
            lin = jnp.clip(lin, -SWIGLU_LIMIT, SWIGLU_LIMIT)
            act = glu * jax.nn.sigmoid(SWIGLU_ALPHA * glu) * (lin + 1.0)
            y = y + _dot(act.astype(BF16), wo_b[a:b, :])
        for c in range(ROW_TILES):
            y_ref[pl.ds(c, tb, stride=ROW_TILES), :] = y[:, c * LANES:(c + 1) * LANES]

    @pl.when(jnp.logical_not(used))
    def _():
        y_ref[...] = jnp.zeros_like(y_ref)


def _experts(xs, block_e, n_used, next_e, layer, w_in, b_in, w_out, b_out):
    tb = EXPERT_ROWS
    n_rows = xs.shape[0] // ROW_TILES
    nb = n_rows // tb
    depth, ne, d, f2 = w_in.shape

    def row_map(i, be, nu, nx):
        return (jnp.minimum(i, nu[0] - 1), 0)

    grid_spec = pltpu.PrefetchScalarGridSpec(
        num_scalar_prefetch=3,
        grid=(nb,),
        in_specs=[
            pl.BlockSpec((tb * ROW_TILES, LANES), row_map),
            pl.BlockSpec(memory_space=pl.ANY),
            pl.BlockSpec((1, 1, 1, f2), lambda i, be, nu, nx: (layer, be[i], 0, 0)),
            pl.BlockSpec(memory_space=pl.ANY),
            pl.BlockSpec((1, 1, 1, d), lambda i, be, nu, nx: (layer, be[i], 0, 0)),
        ],
        out_specs=pl.BlockSpec((tb * ROW_TILES, LANES), lambda i, be, nu, nx: (i, 0)),
        scratch_shapes=[pltpu.VMEM((d, f2), F32), pltpu.VMEM((D_FF, d), F32),
                        pltpu.VMEM((d, f2), BF16), pltpu.VMEM((D_FF, d), BF16),
                        pltpu.SemaphoreType.DMA((2,))],
    )
    return pl.pallas_call(
        functools.partial(_expert_kernel, layer),
        grid_spec=grid_spec,
        out_shape=jax.ShapeDtypeStruct((n_rows * ROW_TILES, LANES), F32),
        compiler_params=_cp(("arbitrary",), 56),
        name="experts",
    )(block_e, n_used, next_e, xs, w_in, b_in.reshape(depth, ne, 1, f2), w_out,
      b_out.reshape(depth, ne, 1, d))


def _combine_kernel(final, dest_ref, dnext_ref, gate_ref, x_ref, mod_ref, fg_ref, yb_ref, o_ref,
                    g_buf, sems):
    tm = x_ref.shape[0]
    i = pl.program_id(0)
    cur = lax.rem(i, 2)

    def issue_all(d_ref, buf):
        def issue(rb, carry):
            for rr in range(ISSUE_UNROLL):
                r = rb * ISSUE_UNROLL + rr
                for k in range(TOP_K):
                    d = d_ref[r * TOP_K + k]
                    pltpu.make_async_copy(
                        yb_ref.at[pl.ds(pl.multiple_of(d * ROW_TILES, ROW_TILES), ROW_TILES), :],
                        g_buf.at[buf, pl.ds(pl.multiple_of((k * tm + r) * ROW_TILES, ROW_TILES),
                                            ROW_TILES), :],
                        sems.at[buf]).start(priority=k % 2)
            return carry

        lax.fori_loop(0, tm // ISSUE_UNROLL, issue, 0)

    @pl.when(i == 0)
    def _():
        issue_all(dest_ref, 0)

    @pl.when(i + 1 < pl.num_programs(0))
    def _():
        issue_all(dnext_ref, 1 - cur)

    pltpu.make_async_copy(yb_ref.at[pl.ds(0, TOP_K * tm * ROW_TILES), :], g_buf.at[cur],
                          sems.at[cur]).wait()

    gates = gate_ref[...]
    gate2 = mod_ref[0, 5:6, :]
    for c in range(ROW_TILES):
        y = jnp.zeros((tm, LANES), F32)
        for k in range(TOP_K):
            rows = g_buf[cur, pl.ds(k * tm * ROW_TILES + c, tm, stride=ROW_TILES), :]
            y = y + gates[:, k:k + 1] * rows
        cs = slice(c * LANES, (c + 1) * LANES)
        o_ref[:, cs] = x_ref[:, cs] + gate2[:, cs] * y
    if final:
        o_ref[...] = _rms_rows(o_ref[...]) * fg_ref[...]


def _combine(yb, dest_flat, gates, x2, mod, final_g, final, seq):
    t, d = x2.shape
    tm = min(COMBINE_ROWS, seq)
    last = t // tm - 1
    return pl.pallas_call(
        functools.partial(_combine_kernel, final),
        grid=(t // tm,),
        in_specs=[
            pl.BlockSpec((tm * TOP_K,), lambda i: (i,), memory_space=pltpu.SMEM),
            pl.BlockSpec((tm * TOP_K,), lambda i: (jnp.minimum(i + 1, last),),
                         memory_space=pltpu.SMEM),
            pl.BlockSpec((tm, LANES), lambda i: (i, 0)),
            pl.BlockSpec((tm, d), lambda i: (i, 0)),
            pl.BlockSpec((1, 6, d), lambda i: (i // (seq // tm), 0, 0)),
            pl.BlockSpec((1, d), lambda i: (0, 0)),
            pl.BlockSpec(memory_space=pl.ANY),
        ],
        out_specs=pl.BlockSpec((tm, d), lambda i: (i, 0)),
        out_shape=jax.ShapeDtypeStruct((t, d), F32),
        scratch_shapes=[pltpu.VMEM((2, TOP_K * tm * ROW_TILES, LANES), F32),
                        pltpu.SemaphoreType.DMA((2,))],
        compiler_params=_cp(("arbitrary",), 40),
        name="combine",
    )(dest_flat, dest_flat, gates, x2, mod, final_g.reshape(1, d), yb)


def _moe(h3, logits, x2, mod, layer, w_in, b_in, w_out, b_out, final_g, final, seq):
    t = x2.shape[0]
    tb = EXPERT_ROWS
    idx, gates, cnt = _route(logits)
    counts = cnt[0, :N_EXPERTS].astype(jnp.int32)
    nblk = (counts + tb - 1) // tb
    blk_end = jnp.cumsum(nblk)
    pad_start = (blk_end - nblk) * tb
    dest = pad_start[idx[:, :TOP_K]] + idx[:, TOP_K:2 * TOP_K]
    dest_flat = dest.reshape(t * TOP_K)
    n_blocks = (t * TOP_K) // tb + N_EXPERTS
    n_used = blk_end[-1:]
    last_e = jnp.max(jnp.where(nblk > 0, jnp.arange(N_EXPERTS, dtype=jnp.int32), 0))
    blk = jnp.arange(n_blocks, dtype=jnp.int32)
    block_e = jnp.minimum(
        jnp.sum((blk_end[None, :] <= blk[:, None]).astype(jnp.int32), axis=1), last_e)
    experts = jnp.arange(N_EXPERTS, dtype=jnp.int32)
    later = (experts[None, :] > block_e[:, None]) & (nblk[None, :] > 0)
    next_e = jnp.min(jnp.where(later, experts[None, :], N_EXPERTS), axis=1)
    next_e = jnp.where(next_e == N_EXPERTS, -1, next_e).astype(jnp.int32)
    fill_start = jnp.concatenate([pad_start + counts, blk_end[-1:] * tb]).astype(jnp.int32)
    fill_len = jnp.concatenate([nblk * tb - counts, (n_blocks - blk_end[-1:]) * tb]).astype(jnp.int32)
    xs = _dispatch(h3, dest_flat, fill_start, fill_len, n_blocks * tb)
    yb = _experts(xs, block_e, n_used.astype(jnp.int32), next_e, layer, w_in, b_in, w_out, b_out)
    return _combine(yb, dest_flat, gates, x2, mod, final_g, final, seq)


def kernel(x, c, positions, w_ada, b_ada, even_w_in, gla_w_gate, gla_b_gate, gla_norm_g,
           diff_lam_q1, diff_lam_k1, diff_lam_q2, diff_lam_k2, diff_norm_g, even_w_out,
           odd_w_in, sgu_ln_g, sgu_ln_b, sgu_w, sgu_b, odd_w_out,
           router_w, router_b, expert_w_in, expert_b_in, expert_w_out, expert_b_out,
           final_norm_g):
    bsz, seq, d = x.shape
    depth = w_ada.shape[0]
    t = bsz * seq
    mods = _modulation(c, w_ada, b_ada).reshape(depth, bsz, 6, d)
    cos, sin = _rope_tables(positions)
    x2 = x.reshape(t, d)
    for layer in range(depth):
        mod = mods[layer]
        j = layer // 2
        if layer % 2 == 0:
            z, f = _proj_even(x2, mod, even_w_in[j], gla_w_gate[j], gla_b_gate[j], cos, sin, seq)
            o_gla = _gla(z, f, gla_norm_g[j], bsz, seq)
            lam_init = 0.8 - 0.6 * math.exp(-0.3 * layer)
            o_diff = _diff_attn(z, diff_lam_q1[j], diff_lam_k1[j], diff_lam_q2[j], diff_lam_k2[j],
                                diff_norm_g[j], lam_init, bsz, seq)
            mixed, w_out = (o_gla, o_diff), even_w_out[j]
        else:
            z, f = _proj_odd(x2, mod, odd_w_in[j], sgu_ln_g[j], sgu_ln_b[j], cos, sin, seq)
            mixed, w_out = (_sgu_ret(z, f, sgu_w[j], sgu_b[j], bsz, seq),), odd_w_out[j]
        x2, h3, logits = _out_proj(mixed, w_out, x2, mod, router_w[layer], router_b[layer], seq)
        x2 = _moe(h3, logits, x2, mod, layer, expert_w_in, expert_b_in, expert_w_out,
                  expert_b_out, final_norm_g, layer == depth - 1, seq)
    return x2.reshape(bsz, seq, d)
```

```python
import functools
import math

import jax
import jax.numpy as jnp
from jax import lax
from jax.experimental import pallas as pl
from jax.experimental.pallas import tpu as pltpu

F32 = jnp.float32
BF16 = jnp.bfloat16

D_MODEL = 1024
EPS = 1e-6
ROPE_THETA = 10000.0
ROPE_HALF = 32

GLA_HEADS = 4
GLA_DK = 64
GLA_DV = 128
GLA_RANK = 16
GLA_CHUNK = 64
GLA_GATE_NORMALIZER = 16.0
GLA_QK_W = GLA_HEADS * GLA_DK
GLA_V_W = GLA_HEADS * GLA_DV

DIFF_HEADS = 4
DIFF_D = 64
DIFF_DV = 128
DIFF_QK_W = DIFF_HEADS * 2 * DIFF_D
DIFF_V_W = DIFF_HEADS * DIFF_DV

SGU_GROUPS = 4
SGU_CH = 128
SGU_CHUNK = 128
SGU_W = SGU_GROUPS * SGU_CH

RET_HEADS = 4
RET_DK = 64
RET_DV = 128
RET_CHUNK = 128
RET_QK_W = RET_HEADS * RET_DK
RET_V_W = RET_HEADS * RET_DV

N_EXPERTS = 32
TOP_K = 4
D_FF = D_MODEL
SWIGLU_ALPHA = 1.702
SWIGLU_LIMIT = 7.0

LANES = 128
SUBLANES = 8
ROW_TILES = D_MODEL // LANES

PROJ_ROWS = 512
GLA_ROWS = 256
SGU_RET_ROWS = 512
ATT_Q_ROWS = 256
ATT_HEADS_PER_STEP = 2
ROUTE_ROWS = 512
DISPATCH_ROWS = 512
ISSUE_UNROLL = 32
FILL_SHIFT = 6
FILL_ROWS = 1 << FILL_SHIFT
EXPERT_ROWS = 512
COMBINE_ROWS = 256
MOD_COLS = 1536

MIB = 1024 * 1024


def _cp(semantics, vmem_mib):
    return pltpu.CompilerParams(dimension_semantics=semantics, vmem_limit_bytes=vmem_mib * MIB)


def _dot(a, b):
    return jnp.dot(a, b, preferred_element_type=F32)


def _dot_nt(a, b):
    return lax.dot_general(a, b, (((1,), (1,)), ((), ())), preferred_element_type=F32)


def _dot_tn(a, b):
    return lax.dot_general(a, b, (((0,), (0,)), ((), ())), preferred_element_type=F32)


def _rms_rows(x):
    return x * lax.rsqrt(jnp.mean(x * x, axis=-1, keepdims=True) + EPS)


def _silu(x):
    return x * jax.nn.sigmoid(x)


def _rope_chunk(v, cos, sin, lo_half):
    rot = jnp.where(lo_half, -pltpu.roll(v, 96, 1), pltpu.roll(v, 32, 1))
    return v * cos + rot * sin


def _mod_kernel(c_ref, w_ref, b_ref, o_ref):
    c = c_ref[...]
    ca = _silu(c).astype(BF16)
    o_ref[0] = _dot(ca, w_ref[0].astype(BF16)) + b_ref[0]


def _modulation(c, w_ada, b_ada):
    depth, d, n = w_ada.shape
    bsz = c.shape[0]
    return pl.pallas_call(
        _mod_kernel,
        grid=(depth, n // MOD_COLS),
        in_specs=[
            pl.BlockSpec((bsz, d), lambda l, j: (0, 0)),
            pl.BlockSpec((1, d, MOD_COLS), lambda l, j: (l, 0, j)),
            pl.BlockSpec((1, 1, MOD_COLS), lambda l, j: (l, 0, j)),
        ],
        out_specs=pl.BlockSpec((1, bsz, MOD_COLS), lambda l, j: (l, 0, j)),
        out_shape=jax.ShapeDtypeStruct((depth, bsz, n), F32),
        compiler_params=_cp(("arbitrary", "arbitrary"), 40),
        name="adaln_mod",
    )(c, w_ada, b_ada.reshape(depth, 1, n))


def _rope_table_kernel(p_ref, f_ref, c_ref, s_ref):
    ang = p_ref[...].astype(F32) * f_ref[...]
    c_ref[...] = jnp.cos(ang)
    s_ref[...] = jnp.sin(ang)


def _rope_tables(positions):
    t = positions.size
    per_row = LANES // ROPE_HALF
    rows = t // per_row
    pos_d = jnp.repeat(positions.reshape(rows, per_row), ROPE_HALF, axis=1)
    inv_freq = ROPE_THETA ** (-jnp.arange(ROPE_HALF, dtype=F32) / ROPE_HALF)
    freq_d = jnp.tile(inv_freq, per_row).reshape(1, LANES)
    tr = min(512, rows)
    cos_d, sin_d = pl.pallas_call(
        _rope_table_kernel,
        grid=(rows // tr,),
        in_specs=[pl.BlockSpec((tr, LANES), lambda i: (i, 0)),
                  pl.BlockSpec((1, LANES), lambda i: (0, 0))],
        out_specs=[pl.BlockSpec((tr, LANES), lambda i: (i, 0))] * 2,
        out_shape=[jax.ShapeDtypeStruct((rows, LANES), F32)] * 2,
        compiler_params=_cp(("arbitrary",), 32),
        name="rope_tables",
    )(pos_d, freq_d)
    cos = jnp.tile(cos_d.reshape(t, ROPE_HALF), (1, per_row))
    sin = jnp.tile(sin_d.reshape(t, ROPE_HALF), (1, per_row))
    return cos, sin


def _modulated_rms(x_ref, mod_ref, which):
    x = x_ref[...]
    shift = mod_ref[0, 3 * which:3 * which + 1, :]
    scale = mod_ref[0, 3 * which + 1:3 * which + 2, :]
    return _rms_rows(x) * (1.0 + scale) + shift


EVEN_Z = (GLA_V_W, GLA_V_W, DIFF_QK_W, DIFF_QK_W, DIFF_V_W)
EVEN_Z_W = sum(EVEN_Z)
EVEN_F_W = 3 * GLA_QK_W
EVEN_W_COLS = EVEN_Z_W + 2 * GLA_QK_W + LANES


def _proj_even_kernel(x_ref, mod_ref, w_ref, wg_ref, bg_ref, cos_ref, sin_ref, z_ref, f_ref):
    hb = _modulated_rms(x_ref, mod_ref, 0).astype(BF16)

    def sec(a, b):
        return _dot(hb, w_ref[:, a:b])

    for a, b in ((0, 512), (512, 1024), (2048, 2560)):
        z_ref[:, a:b] = sec(a, b).astype(BF16)
    cos = cos_ref[...]
    sin = sin_ref[...]
    lo_half = (lax.broadcasted_iota(jnp.int32, cos.shape, 1) % 64) < ROPE_HALF
    q_scale = DIFF_D ** -0.5
    for a, scl in ((1024, q_scale), (1536, 1.0)):
        full = sec(a, a + DIFF_QK_W)
        for c in range(DIFF_QK_W // LANES):
            v = full[:, c * LANES:(c + 1) * LANES]
            z_ref[:, a + c * LANES:a + (c + 1) * LANES] = (
                _rope_chunk(v, cos, sin, lo_half) * scl).astype(BF16)
    f_ref[:, 0:256] = sec(2560, 2816) * (GLA_DK ** -0.5)
    f_ref[:, 256:512] = sec(2816, 3072)
    gr = sec(3072, 3200).astype(BF16)
    pre = _dot(gr, wg_ref[...]) + bg_ref[...]
    log_sig = jnp.minimum(pre, 0.0) - jnp.log1p(jnp.exp(-jnp.abs(pre)))
    f_ref[:, 512:768] = log_sig / GLA_GATE_NORMALIZER


def _proj_even(x2, mod, w_in, w_gate, b_gate, cos, sin, seq):
    t, d = x2.shape
    tm = min(PROJ_ROWS, seq)
    gq, gk, gv, gr, gg, dq, dk, dv = jnp.split(
        w_in, [256, 512, 1024, 1040, 1552, 2064, 2576], axis=1)
    gr_pad = jnp.pad(gr, ((0, 0), (0, LANES - GLA_RANK)))
    w_cat = jnp.concatenate([gv, gg, dq, dk, dv, gq, gk, gr_pad], axis=1).astype(BF16)
    wg_pad = jnp.pad(w_gate, ((0, LANES - GLA_RANK), (0, 0))).astype(BF16)
    return pl.pallas_call(
        _proj_even_kernel,
        grid=(t // tm,),
        in_specs=[
            pl.BlockSpec((tm, d), lambda i: (i, 0)),
            pl.BlockSpec((1, 6, d), lambda i: (i // (seq // tm), 0, 0)),
            pl.BlockSpec((d, EVEN_W_COLS), lambda i: (0, 0)),
            pl.BlockSpec((LANES, GLA_QK_W), lambda i: (0, 0)),
            pl.BlockSpec((1, GLA_QK_W), lambda i: (0, 0)),
            pl.BlockSpec((tm, LANES), lambda i: (i, 0)),
            pl.BlockSpec((tm, LANES), lambda i: (i, 0)),
        ],
        out_specs=[pl.BlockSpec((tm, EVEN_Z_W), lambda i: (i, 0)),
                   pl.BlockSpec((tm, EVEN_F_W), lambda i: (i, 0))],
        out_shape=[jax.ShapeDtypeStruct((t, EVEN_Z_W), BF16),
                   jax.ShapeDtypeStruct((t, EVEN_F_W), F32)],
        compiler_params=_cp(("arbitrary",), 48),
        name="proj_even",
    )(x2, mod, w_cat, wg_pad, b_gate.reshape(1, GLA_QK_W), cos, sin)


ODD_Z_W = 2 * SGU_W + 2 * RET_V_W
ODD_F_W = 2 * RET_QK_W
ODD_W_COLS = ODD_Z_W + ODD_F_W


def _gelu_exact(x):
    return 0.5 * x * (1.0 + lax.erf(x * (2.0 ** -0.5)))


def _proj_odd_kernel(x_ref, mod_ref, w_ref, lng_ref, lnb_ref, cos_ref, sin_ref, z_ref, f_ref):
    hb = _modulated_rms(x_ref, mod_ref, 0).astype(BF16)

    def sec(a, b):
        return _dot(hb, w_ref[:, a:b])

    z_ref[:, 0:512] = _gelu_exact(sec(0, 512)).astype(BF16)
    sv = _gelu_exact(sec(512, 1024))
    mu = jnp.mean(sv, axis=-1, keepdims=True)
    cen = sv - mu
    var = jnp.mean(cen * cen, axis=-1, keepdims=True)
    z_ref[:, 512:1024] = (cen * lax.rsqrt(var + EPS) * lng_ref[...] + lnb_ref[...]).astype(BF16)
    for a, b in ((1024, 1536), (1536, 2048)):
        z_ref[:, a:b] = sec(a, b).astype(BF16)
    cos = cos_ref[...]
    sin = sin_ref[...]
    lo_half = (lax.broadcasted_iota(jnp.int32, cos.shape, 1) % 64) < ROPE_HALF
    k_scale = RET_DK ** -0.5
    for a, scl in ((0, 1.0), (256, k_scale)):
        full = sec(ODD_Z_W + a, ODD_Z_W + a + RET_QK_W)
        for c in range(RET_QK_W // LANES):
            v = full[:, c * LANES:(c + 1) * LANES]
            f_ref[:, a + c * LANES:a + (c + 1) * LANES] = _rope_chunk(v, cos, sin, lo_half) * scl


def _proj_odd(x2, mod, w_in, ln_g, ln_b, cos, sin, seq):
    t, d = x2.shape
    tm = min(PROJ_ROWS, seq)
    su, sv, rq, rk, rv, rg = jnp.split(w_in, [512, 1024, 1280, 1536, 2048], axis=1)
    w_cat = jnp.concatenate([su, sv, rv, rg, rq, rk], axis=1).astype(BF16)
    return pl.pallas_call(
        _proj_odd_kernel,
        grid=(t // tm,),
        in_specs=[
            pl.BlockSpec((tm, d), lambda i: (i, 0)),
            pl.BlockSpec((1, 6, d), lambda i: (i // (seq // tm), 0, 0)),
            pl.BlockSpec((d, ODD_W_COLS), lambda i: (0, 0)),
            pl.BlockSpec((1, SGU_W), lambda i: (0, 0)),
            pl.BlockSpec((1, SGU_W), lambda i: (0, 0)),
            pl.BlockSpec((tm, LANES), lambda i: (i, 0)),
            pl.BlockSpec((tm, LANES), lambda i: (i, 0)),
        ],
        out_specs=[pl.BlockSpec((tm, ODD_Z_W), lambda i: (i, 0)),
                   pl.BlockSpec((tm, ODD_F_W), lambda i: (i, 0))],
        out_shape=[jax.ShapeDtypeStruct((t, ODD_Z_W), BF16),
                   jax.ShapeDtypeStruct((t, ODD_F_W), F32)],
        compiler_params=_cp(("arbitrary",), 48),
        name="proj_odd",
    )(x2, mod, w_cat, ln_g.reshape(1, SGU_W), ln_b.reshape(1, SGU_W), cos, sin)


def _gla_kernel(q_ref, k_ref, la_ref, v_ref, gg_ref, g_ref, o_ref, st_ref):
    @pl.when(pl.program_id(1) == 0)
    def _():
        st_ref[...] = jnp.zeros_like(st_ref)

    c = GLA_CHUNK
    rows = q_ref.shape[0]
    r_i = lax.broadcasted_iota(jnp.int32, (c, c), 0)
    c_i = lax.broadcasted_iota(jnp.int32, (c, c), 1)
    tril = (r_i >= c_i).astype(BF16)
    r2 = lax.broadcasted_iota(jnp.int32, (2 * c, c), 0) % c
    c2 = lax.broadcasted_iota(jnp.int32, (2 * c, c), 1)
    causal2 = r2 >= c2
    lane = lax.broadcasted_iota(jnp.int32, (c, LANES), 1)
    lo = lane < GLA_DK
    lane_s = lax.broadcasted_iota(jnp.int32, (LANES, LANES), 1) < GLA_DK
    g_row = g_ref[...]
    states = [st_ref[0], st_ref[1]]
    for j in range(rows // c):
        rs = slice(j * c, (j + 1) * c)
        la = la_ref[rs, :]
        la_hi = la.astype(BF16)
        la_lo = (la - la_hi.astype(F32)).astype(BF16)
        b = _dot(tril, la_hi) + _dot(tril, la_lo)
        b_last = b[c - 1:c, :]
        qd = q_ref[rs, :] * jnp.exp(b)
        kk = k_ref[rs, :]
        ki = kk * jnp.exp(-b)
        kd = kk * jnp.exp(b_last - b)
        dec = jnp.exp(b_last)
        for p in range(GLA_HEADS // 2):
            ls = slice(p * LANES, (p + 1) * LANES)
            qd_p = qd[:, ls]
            qm = (jnp.where(lo, qd_p, 0.0).astype(BF16), jnp.where(lo, 0.0, qd_p).astype(BF16))
            s2 = _dot_nt(jnp.concatenate(qm, axis=0), ki[:, ls].astype(BF16))
            s2 = jnp.where(causal2, s2, 0.0).astype(BF16)
            kd_p = kd[:, ls].astype(BF16)
            st_b = states[p].astype(BF16)
            new = []
            for hh in range(2):
                h = 2 * p + hh
                hs = slice(h * GLA_DV, (h + 1) * GLA_DV)
                v_h = v_ref[rs, hs]
                o = _dot(s2[hh * c:(hh + 1) * c, :], v_h) + _dot_nt(qm[hh], st_b)
                o = _rms_rows(o) * g_row
                gate = _silu(gg_ref[rs, hs].astype(F32))
                o_ref[rs, hs] = (o * gate).astype(BF16)
                new.append(_dot_tn(v_h, kd_p))
            states[p] = dec[:, ls] * states[p] + jnp.where(lane_s, new[0], new[1])
    st_ref[0] = states[0]
    st_ref[1] = states[1]


def _gla(z, f, norm_g, bsz, seq):
    t = z.shape[0]
    tc = min(GLA_ROWS, seq)
    nc = seq // tc
    return pl.pallas_call(
        _gla_kernel,
        grid=(bsz, nc),
        in_specs=[
            pl.BlockSpec((tc, GLA_QK_W), lambda b, i: (b * nc + i, 0)),
            pl.BlockSpec((tc, GLA_QK_W), lambda b, i: (b * nc + i, 1)),
            pl.BlockSpec((tc, GLA_QK_W), lambda b, i: (b * nc + i, 2)),
            pl.BlockSpec((tc, GLA_V_W), lambda b, i: (b * nc + i, 0)),
            pl.BlockSpec((tc, GLA_V_W), lambda b, i: (b * nc + i, 1)),
            pl.BlockSpec((1, GLA_DV), lambda b, i: (0, 0)),
        ],
        out_specs=pl.BlockSpec((tc, GLA_V_W), lambda b, i: (b * nc + i, 0)),
        out_shape=jax.ShapeDtypeStruct((t, GLA_V_W), BF16),
        scratch_shapes=[pltpu.VMEM((2, LANES, LANES), F32)],
        compiler_params=_cp(("arbitrary", "arbitrary"), 32),
        name="gla",
    )(f, f, f, z, z, norm_g.reshape(1, GLA_DV))


def _diff_attn_kernel(lam_init, q_ref, k_ref, v_ref, lq1_ref, lk1_ref, lq2_ref, lk2_ref, g_ref,
                      o_ref, vt_ref):
    tq = q_ref.shape[0]
    seq = k_ref.shape[0]
    n_heads = q_ref.shape[1] // LANES
    qi = pl.program_id(2)

    @pl.when(qi == 0)
    def _():
        for hh in range(n_heads):
            for cb in range(seq // tq):
                blk = v_ref[cb * tq:(cb + 1) * tq, hh * LANES:(hh + 1) * LANES]
                vt_ref[hh, :, cb * tq:(cb + 1) * tq] = blk.astype(F32).T.astype(BF16)

    lo = lax.broadcasted_iota(jnp.int32, (tq, LANES), 1) < DIFF_D
    qqs = []
    for hh in range(n_heads):
        q = q_ref[:, hh * LANES:(hh + 1) * LANES]
        zero = jnp.zeros_like(q)
        qqs.append(jnp.concatenate([jnp.where(lo, q, zero), jnp.where(lo, zero, q)], axis=0))

    def scores(j, hh):
        start = pl.multiple_of(j * tq, tq)
        kj = k_ref[pl.ds(start, tq), hh * LANES:(hh + 1) * LANES]
        return _dot_nt(kj, qqs[hh])

    def update(j, state, s, hh, masked):
        m, l, acc = state
        start = pl.multiple_of(j * tq, tq)
        vtj = vt_ref[hh, :, pl.ds(start, tq)]
        if masked:
            kv = lax.broadcasted_iota(jnp.int32, s.shape, 0)
            qq_pos = lax.broadcasted_iota(jnp.int32, s.shape, 1) % tq
            s = jnp.where(kv <= qq_pos, s, -jnp.inf)
        m_new = jnp.maximum(m, jnp.max(s, axis=0, keepdims=True))
        alpha = jnp.exp(m - m_new)
        p = jnp.exp(s - m_new)
        l = alpha * l + jnp.sum(p, axis=0, keepdims=True)
        acc = alpha * acc + _dot(vtj, p.astype(BF16))
        return m_new, l, acc

    def body(j, carries):
        nxt = [scores(j + 1, hh) for hh in range(n_heads)]
        return tuple((update(j, carries[hh][0], carries[hh][1], hh, False), nxt[hh])
                     for hh in range(n_heads))

    init = tuple(((jnp.full((1, 2 * tq), -jnp.inf, F32), jnp.zeros((1, 2 * tq), F32),
                   jnp.zeros((DIFF_DV, 2 * tq), F32)), scores(0, hh)) for hh in range(n_heads))
    carries = lax.fori_loop(0, qi, body, init)
    carries = tuple(update(qi, carries[hh][0], carries[hh][1], hh, True)
                    for hh in range(n_heads))
    lam = (jnp.exp(jnp.sum(lq1_ref[...] * lk1_ref[...], axis=-1, keepdims=True))
           - jnp.exp(jnp.sum(lq2_ref[...] * lk2_ref[...], axis=-1, keepdims=True)) + lam_init)
    for hh in range(n_heads):
        _, l, acc = carries[hh]
        o12 = acc / l
        o = o12[:, :tq] - lam * o12[:, tq:]
        o = o * lax.rsqrt(jnp.mean(o * o, axis=0, keepdims=True) + EPS)
        o = o * g_ref[...] * (1.0 - lam_init)
        o_ref[:, hh * DIFF_DV:(hh + 1) * DIFF_DV] = o.T.astype(BF16)


def _diff_attn(z, lq1, lk1, lq2, lk2, norm_g, lam_init, bsz, seq):
    t = z.shape[0]
    tq = min(ATT_Q_ROWS, seq)
    nq = seq // tq
    hw = ATT_HEADS_PER_STEP * LANES
    qb, kb, vb = 1024 // hw, 1536 // hw, 2048 // hw
    small = pl.BlockSpec((1, DIFF_D), lambda b, h, i: (0, 0))
    return pl.pallas_call(
        functools.partial(_diff_attn_kernel, lam_init),
        grid=(bsz, DIFF_HEADS // ATT_HEADS_PER_STEP, nq),
        in_specs=[
            pl.BlockSpec((tq, hw), lambda b, h, i: (b * nq + i, qb + h)),
            pl.BlockSpec((seq, hw), lambda b, h, i: (b, kb + h)),
            pl.BlockSpec((seq, hw), lambda b, h, i: (b, vb + h)),
            small, small, small, small,
            pl.BlockSpec((DIFF_DV, 1), lambda b, h, i: (0, 0)),
        ],
        out_specs=pl.BlockSpec((tq, hw), lambda b, h, i: (b * nq + i, h)),
        out_shape=jax.ShapeDtypeStruct((t, DIFF_V_W), BF16),
        scratch_shapes=[pltpu.VMEM((ATT_HEADS_PER_STEP, DIFF_DV, seq), BF16)],
        compiler_params=_cp(("arbitrary", "arbitrary", "arbitrary"), 32),
        name="diff_attn",
    )(z, z, z, lq1.reshape(1, DIFF_D), lk1.reshape(1, DIFF_D), lq2.reshape(1, DIFF_D),
      lk2.reshape(1, DIFF_D), norm_g.reshape(DIFF_DV, 1))


def _sgu_ret_kernel(su_ref, sv_ref, rv_ref, rg_ref, q_ref, k_ref, ws_ref, bs_ref, o_ref, st_ref):
    @pl.when(pl.program_id(1) == 0)
    def _():
        st_ref[...] = jnp.zeros_like(st_ref)

    c = RET_CHUNK
    row = lax.broadcasted_iota(jnp.int32, (c, c), 0)
    col = lax.broadcasted_iota(jnp.int32, (c, c), 1)
    causal = row >= col
    log_g = [math.log(1.0 - 2.0 ** (-5.0 - h)) for h in range(RET_HEADS)]
    lo = col < RET_DK
    rel = (row - col).astype(F32)
    pos = row.astype(F32)
    w_sgu = [jnp.where(causal, ws_ref[g], 0.0).astype(BF16) for g in range(SGU_GROUPS)]
    decays = [jnp.where(causal, jnp.exp(log_g[h] * jnp.maximum(rel, 0.0)), 0.0)
              for h in range(RET_HEADS)]
    lgs = [jnp.where(lo, log_g[2 * p], log_g[2 * p + 1]) for p in range(RET_HEADS // 2)]
    q_decs = [jnp.exp(lg * (pos + 1.0)) for lg in lgs]
    k_decs = [jnp.exp(lg * (c - 1.0 - pos)) for lg in lgs]
    states = [st_ref[p] for p in range(RET_HEADS // 2)]
    for j in range(su_ref.shape[0] // c):
        rs = slice(j * c, (j + 1) * c)
        for g in range(SGU_GROUPS):
            gs = slice(g * SGU_CH, (g + 1) * SGU_CH)
            s = _dot(w_sgu[g], sv_ref[rs, gs]) + bs_ref[g]
            o_ref[rs, gs] = (su_ref[rs, gs].astype(F32) * s).astype(BF16)
        for p in range(RET_HEADS // 2):
            ls = slice(p * LANES, (p + 1) * LANES)
            q_p = q_ref[rs, ls]
            k_p = k_ref[rs, ls]
            qm = (jnp.where(lo, q_p, 0.0).astype(BF16), jnp.where(lo, 0.0, q_p).astype(BF16))
            s2 = _dot_nt(jnp.concatenate(qm, axis=0), k_p.astype(BF16))
            qd = q_p * q_decs[p]
            qdm = (jnp.where(lo, qd, 0.0).astype(BF16), jnp.where(lo, 0.0, qd).astype(BF16))
            kd = (k_p * k_decs[p]).astype(BF16)
            st_b = states[p].astype(BF16)
            new = []
            for hh in range(2):
                h = 2 * p + hh
                hs = slice(h * RET_DV, (h + 1) * RET_DV)
                s_h = (s2[hh * c:(hh + 1) * c, :] * decays[h]).astype(BF16)
                v_h = rv_ref[rs, hs]
                o = _dot(s_h, v_h) + _dot_nt(qdm[hh], st_b)
                gate = _silu(rg_ref[rs, hs].astype(F32))
                o_ref[rs, SGU_W + h * RET_DV:SGU_W + (h + 1) * RET_DV] = (
                    _rms_rows(o) * gate).astype(BF16)
                new.append(_dot_tn(v_h, kd))
            states[p] = jnp.exp(lgs[p] * float(c)) * states[p] + jnp.where(lo, new[0], new[1])
    for p in range(RET_HEADS // 2):
        st_ref[p] = states[p]


def _sgu_ret(z, f, w_s, b_s, bsz, seq):
    t = z.shape[0]
    cc = RET_CHUNK
    c = min(SGU_RET_ROWS, seq)
    nc = seq // c
    return pl.pallas_call(
        _sgu_ret_kernel,
        grid=(bsz, nc),
        in_specs=[
            pl.BlockSpec((c, SGU_W), lambda b, i: (b * nc + i, 0)),
            pl.BlockSpec((c, SGU_W), lambda b, i: (b * nc + i, 1)),
            pl.BlockSpec((c, RET_V_W), lambda b, i: (b * nc + i, 2)),
            pl.BlockSpec((c, RET_V_W), lambda b, i: (b * nc + i, 3)),
            pl.BlockSpec((c, RET_QK_W), lambda b, i: (b * nc + i, 0)),
            pl.BlockSpec((c, RET_QK_W), lambda b, i: (b * nc + i, 1)),
            pl.BlockSpec((SGU_GROUPS, cc, cc), lambda b, i: (0, 0, 0)),
            pl.BlockSpec((SGU_GROUPS, cc, 1), lambda b, i: (0, 0, 0)),
        ],
        out_specs=pl.BlockSpec((c, SGU_W + RET_V_W), lambda b, i: (b * nc + i, 0)),
        out_shape=jax.ShapeDtypeStruct((t, SGU_W + RET_V_W), BF16),
        scratch_shapes=[pltpu.VMEM((2, LANES, LANES), F32)],
        compiler_params=_cp(("arbitrary", "arbitrary"), 32),
        name="sgu_retention",
    )(z, z, z, z, f, f, w_s, b_s.reshape(SGU_GROUPS, cc, 1))


def _out_proj_kernel(n_in, *refs):
    o_refs = refs[:n_in]
    w_ref, x_ref, mod_ref, rw_ref, rb_ref, xn_ref, h_ref, lg_ref = refs[n_in:]
    tm = x_ref.shape[0]
    k_each = D_MODEL // n_in
    y = _dot(o_refs[0][...], w_ref[0:k_each, :])
    for n in range(1, n_in):
        y = y + _dot(o_refs[n][...], w_ref[n * k_each:(n + 1) * k_each, :])
    gate1 = mod_ref[0, 2:3, :]
    xn = x_ref[...] + gate1 * y
    xn_ref[...] = xn
    h = _rms_rows(xn) * (1.0 + mod_ref[0, 4:5, :]) + mod_ref[0, 3:4, :]
    for c in range(ROW_TILES):
        h_ref[pl.ds(c, tm, stride=ROW_TILES), :] = h[:, c * LANES:(c + 1) * LANES]
    lg_ref[...] = _dot_nt(rw_ref[...], h.astype(BF16)) + rb_ref[...]


def _out_proj(mixed, w_out, x2, mod, router_w, router_b, seq):
    t, d = x2.shape
    tm = min(PROJ_ROWS, seq)
    n_in = len(mixed)
    k_each = d // n_in
    rw = router_w.T.astype(BF16)
    rb = router_b.reshape(N_EXPERTS, 1)
    return pl.pallas_call(
        functools.partial(_out_proj_kernel, n_in),
        grid=(t // tm,),
        in_specs=[pl.BlockSpec((tm, k_each), lambda i: (i, 0)) for _ in mixed] + [
            pl.BlockSpec((d, d), lambda i: (0, 0)),
            pl.BlockSpec((tm, d), lambda i: (i, 0)),
            pl.BlockSpec((1, 6, d), lambda i: (i // (seq // tm), 0, 0)),
            pl.BlockSpec((N_EXPERTS, d), lambda i: (0, 0)),
            pl.BlockSpec((N_EXPERTS, 1), lambda i: (0, 0)),
        ],
        out_specs=[pl.BlockSpec((tm, d), lambda i: (i, 0)),
                   pl.BlockSpec((tm * ROW_TILES, LANES), lambda i: (i, 0)),
                   pl.BlockSpec((N_EXPERTS, tm), lambda i: (0, i))],
        out_shape=[jax.ShapeDtypeStruct((t, d), F32),
                   jax.ShapeDtypeStruct((t * ROW_TILES, LANES), F32),
                   jax.ShapeDtypeStruct((N_EXPERTS, t), F32)],
        compiler_params=_cp(("arbitrary",), 48),
        name="out_proj",
    )(*mixed, w_out.astype(BF16), x2, mod, rw, rb)


def _route_kernel(lg_ref, idx_ref, gate_ref, cnt_ref, run_ref):
    @pl.when(pl.program_id(0) == 0)
    def _():
        run_ref[...] = jnp.zeros_like(run_ref)

    tm = lg_ref.shape[1]
    row = lax.broadcasted_iota(jnp.int32, (N_EXPERTS, tm), 0)
    neg = -jnp.inf
    l = lg_ref[...]
    vals, firsts, hots = [], [], []
    for _ in range(TOP_K):
        m = jnp.max(l, axis=0, keepdims=True)
        first = jnp.min(jnp.where(l == m, row, N_EXPERTS), axis=0, keepdims=True)
        hot = row == first
        vals.append(m)
        firsts.append(first)
        hots.append(hot)
        l = jnp.where(hot, neg, l)
    sel = hots[0] | hots[1] | hots[2] | hots[3]
    ex = [jnp.exp(v - vals[0]) for v in vals]
    denom = ex[0] + ex[1] + ex[2] + ex[3]
    r_i = lax.broadcasted_iota(jnp.int32, (tm, tm), 0)
    c_i = lax.broadcasted_iota(jnp.int32, (tm, tm), 1)
    before = (r_i < c_i).astype(BF16)
    run = run_ref[:, 0:1]
    cum = _dot(sel.astype(BF16), before) + run
    row8 = lax.broadcasted_iota(jnp.int32, (2 * TOP_K, tm), 0)
    idx_out = jnp.zeros((2 * TOP_K, tm), jnp.int32)
    gate_out = jnp.zeros((2 * TOP_K, tm), F32)
    for k in range(TOP_K):
        rank_k = jnp.sum(jnp.where(hots[k], cum, 0.0), axis=0, keepdims=True).astype(jnp.int32)
        idx_out = jnp.where(row8 == k, firsts[k], idx_out)
        idx_out = jnp.where(row8 == TOP_K + k, rank_k, idx_out)
        gate_out = jnp.where(row8 == k, ex[k] / denom, gate_out)
    idx_ref[...] = idx_out
    gate_ref[...] = gate_out
    total = run + jnp.sum(sel.astype(F32), axis=1, keepdims=True)
    run_ref[...] = jnp.broadcast_to(total, run_ref.shape)
    cnt_ref[...] = jnp.broadcast_to(total, cnt_ref.shape)


def _route(logits_t):
    t = logits_t.shape[1]
    tm = min(ROUTE_ROWS, t)
    return pl.pallas_call(
        _route_kernel,
        grid=(t // tm,),
        in_specs=[pl.BlockSpec((N_EXPERTS, tm), lambda i: (0, i))],
        out_specs=[pl.BlockSpec((2 * TOP_K, tm), lambda i: (0, i)),
                   pl.BlockSpec((2 * TOP_K, tm), lambda i: (0, i)),
                   pl.BlockSpec((N_EXPERTS, LANES), lambda i: (0, 0))],
        out_shape=[jax.ShapeDtypeStruct((2 * TOP_K, t), jnp.int32),
                   jax.ShapeDtypeStruct((2 * TOP_K, t), F32),
                   jax.ShapeDtypeStruct((N_EXPERTS, LANES), F32)],
        scratch_shapes=[pltpu.VMEM((N_EXPERTS, LANES), F32)],
        compiler_params=_cp(("arbitrary",), 32),
        name="route",
    )(logits_t)


def _dispatch_kernel(fs_ref, fl_ref, h_ref, dest_ref, xs_ref, zero_buf, sem, zsem):
    tm = h_ref.shape[0] // ROW_TILES
    n_fill = fs_ref.shape[0]

    def rows(ref, start, n):
        return ref.at[pl.ds(pl.multiple_of(start * ROW_TILES, ROW_TILES), n * ROW_TILES), :]

    def fill_copies(f, wait):
        start, n = fs_ref[f], fl_ref[f]
        n_chunks = lax.shift_right_logical(n, FILL_SHIFT)
        tail = start + n_chunks * FILL_ROWS

        def chunk(j, carry):
            cp = pltpu.make_async_copy(zero_buf, rows(xs_ref, start + j * FILL_ROWS, FILL_ROWS), zsem)
            cp.wait() if wait else cp.start()
            return carry

        def single(j, carry):
            cp = pltpu.make_async_copy(rows(zero_buf, 0, 1), rows(xs_ref, tail + j, 1), zsem)
            cp.wait() if wait else cp.start()
            return carry

        lax.fori_loop(0, n_chunks, chunk, 0)
        lax.fori_loop(0, n - n_chunks * FILL_ROWS, single, 0)

    @pl.when(pl.program_id(0) == 0)
    def _():
        zero_buf[...] = jnp.zeros_like(zero_buf)
        lax.fori_loop(0, n_fill, lambda f, c: (fill_copies(f, False), c)[1], 0)

    def issue(rb, carry):
        for rr in range(ISSUE_UNROLL):
            r = rb * ISSUE_UNROLL + rr
            for k in range(TOP_K):
                pltpu.make_async_copy(rows(h_ref, r, 1), rows(xs_ref, dest_ref[r * TOP_K + k], 1),
                                      sem).start(priority=k % 2)
        return carry

    lax.fori_loop(0, tm // ISSUE_UNROLL, issue, 0)
    for _ in range(TOP_K):
        pltpu.make_async_copy(h_ref, rows(xs_ref, 0, tm), sem).wait()

    @pl.when(pl.program_id(0) == 0)
    def _():
        lax.fori_loop(0, n_fill, lambda f, c: (fill_copies(f, True), c)[1], 0)


def _dispatch(h3, dest_flat, fill_start, fill_len, n_rows):
    t = h3.shape[0] // ROW_TILES
    tm = min(DISPATCH_ROWS, t)
    grid_spec = pltpu.PrefetchScalarGridSpec(
        num_scalar_prefetch=2,
        grid=(t // tm,),
        in_specs=[pl.BlockSpec((tm * ROW_TILES, LANES), lambda i, fs, fl: (i, 0)),
                  pl.BlockSpec((tm * TOP_K,), lambda i, fs, fl: (i,), memory_space=pltpu.SMEM)],
        out_specs=pl.BlockSpec(memory_space=pl.ANY),
        scratch_shapes=[pltpu.VMEM((FILL_ROWS * ROW_TILES, LANES), F32),
                        pltpu.SemaphoreType.DMA(()), pltpu.SemaphoreType.DMA(())],
    )
    return pl.pallas_call(
        _dispatch_kernel,
        grid_spec=grid_spec,
        out_shape=jax.ShapeDtypeStruct((n_rows * ROW_TILES, LANES), F32),
        compiler_params=_cp(("arbitrary",), 32),
        name="dispatch",
    )(fill_start, fill_len, h3, dest_flat)


def _expert_kernel(layer, be_ref, nu_ref, nx_ref, xs_ref, wi_hbm, bi_ref, wo_hbm, bo_ref, y_ref,
                   wi_st, wo_st, wi_b, wo_b, sems):
    i = pl.program_id(0)
    tb = xs_ref.shape[0] // ROW_TILES
    e = be_ref[i]
    fresh = jnp.logical_or(i == 0, e != be_ref[jnp.maximum(i - 1, 0)])
    used = i < nu_ref[0]

    def fetch(ex):
        return (pltpu.make_async_copy(wi_hbm.at[layer, ex], wi_st, sems.at[0]),
                pltpu.make_async_copy(wo_hbm.at[layer, ex], wo_st, sems.at[1]))

    @pl.when(i == 0)
    def _():
        for cp in fetch(e):
            cp.start()

    @pl.when(jnp.logical_and(fresh, used))
    def _():
        for cp in fetch(e):
            cp.wait()
        wi_b[...] = wi_st[...].astype(BF16)
        wo_b[...] = wo_st[...].astype(BF16)

        @pl.when(nx_ref[i] >= 0)
        def _():
            for cp in fetch(nx_ref[i]):
                cp.start()

    @pl.when(used)
    def _():
        x = jnp.concatenate(
            [xs_ref[pl.ds(c, tb, stride=ROW_TILES), :] for c in range(ROW_TILES)],
            axis=1).astype(BF16)
        y = jnp.zeros((tb, D_MODEL), F32) + bo_ref[0, 0]
        half = 512
        for j in range(D_FF // half):
            a, b = j * half, (j + 1) * half
            glu = _dot(x, wi_b[:, a:b]) + bi_ref[0, 0, :, a:b]
            lin = _dot(x, wi_b[:, D_FF + a:D_FF + b]) + bi_ref[0, 0, :, D_FF + a:D_FF + b]
            glu = jnp.minimum(glu, SWIGLU_LIMIT)
            lin = jnp.clip(lin, -SWIGLU_LIMIT, SWIGLU_LIMIT)
            act = glu * jax.nn.sigmoid(SWIGLU_ALPHA * glu) * (lin + 1.0)
            y = y + _dot(act.astype(BF16), wo_b[a:b, :])
        for c in range(ROW_TILES):
            y_ref[pl.ds(c, tb, stride=ROW_TILES), :] = y[:, c * LANES:(c + 1) * LANES]

    @pl.when(jnp.logical_not(used))
    def _():
        y_ref[...] = jnp.zeros_like(y_ref)


def _experts(xs, block_e, n_used, next_e, layer, w_in, b_in, w_out, b_out):
    tb = EXPERT_ROWS
    n_rows = xs.shape[0] // ROW_TILES
    nb = n_rows // tb
    depth, ne, d, f2 = w_in.shape

    def row_map(i, be, nu, nx):
        return (jnp.minimum(i, nu[0] - 1), 0)

    grid_spec = pltpu.PrefetchScalarGridSpec(
        num_scalar_prefetch=3,
        grid=(nb,),
        in_specs=[
            pl.BlockSpec((tb * ROW_TILES, LANES), row_map),
            pl.BlockSpec(memory_space=pl.ANY),
            pl.BlockSpec((1, 1, 1, f2), lambda i, be, nu, nx: (layer, be[i], 0, 0)),
            pl.BlockSpec(memory_space=pl.ANY),
            pl.BlockSpec((1, 1, 1, d), lambda i, be, nu, nx: (layer, be[i], 0, 0)),
        ],
        out_specs=pl.BlockSpec((tb * ROW_TILES, LANES), lambda i, be, nu, nx: (i, 0)),
        scratch_shapes=[pltpu.VMEM((d, f2), F32), pltpu.VMEM((D_FF, d), F32),
                        pltpu.VMEM((d, f2), BF16), pltpu.VMEM((D_FF, d), BF16),
                        pltpu.SemaphoreType.DMA((2,))],
    )
    return pl.pallas_call(
        functools.partial(_expert_kernel, layer),
        grid_spec=grid_spec,
        out_shape=jax.ShapeDtypeStruct((n_rows * ROW_TILES, LANES), F32),
        compiler_params=_cp(("arbitrary",), 56),
        name="experts",
    )(block_e, n_used, next_e, xs, w_in, b_in.reshape(depth, ne, 1, f2), w_out,
      b_out.reshape(depth, ne, 1, d))


def _combine_kernel(final, dest_ref, dnext_ref, gate_ref, x_ref, mod_ref, fg_ref, yb_ref, o_ref,
                    g_buf, sems):
    tm = x_ref.shape[0]
    i = pl.program_id(0)
    cur = lax.rem(i, 2)

    def issue_all(d_ref, buf):
        def issue(rb, carry):
            for rr in range(ISSUE_UNROLL):
                r = rb * ISSUE_UNROLL + rr
                for k in range(TOP_K):
                    d = d_ref[r * TOP_K + k]
                    pltpu.make_async_copy(
                        yb_ref.at[pl.ds(pl.multiple_of(d * ROW_TILES, ROW_TILES), ROW_TILES), :],
                        g_buf.at[buf, pl.ds(pl.multiple_of((k * tm + r) * ROW_TILES, ROW_TILES),
                                            ROW_TILES), :],
                        sems.at[buf]).start(priority=k % 2)
            return carry

        lax.fori_loop(0, tm // ISSUE_UNROLL, issue, 0)

    @pl.when(i == 0)
    def _():
        issue_all(dest_ref, 0)

    @pl.when(i + 1 < pl.num_programs(0))
    def _():
        issue_all(dnext_ref, 1 - cur)

    pltpu.make_async_copy(yb_ref.at[pl.ds(0, TOP_K * tm * ROW_TILES), :], g_buf.at[cur],
                          sems.at[cur]).wait()

    gates = gate_ref[...]
    gate2 = mod_ref[0, 5:6, :]
    for c in range(ROW_TILES):
        y = jnp.zeros((tm, LANES), F32)
        for k in range(TOP_K):
            rows = g_buf[cur, pl.ds(k * tm * ROW_TILES + c, tm, stride=ROW_TILES), :]
            y = y + gates[:, k:k + 1] * rows
        cs = slice(c * LANES, (c + 1) * LANES)
        o_ref[:, cs] = x_ref[:, cs] + gate2[:, cs] * y
    if final:
        o_ref[...] = _rms_rows(o_ref[...]) * fg_ref[...]


def _combine(yb, dest_flat, gates, x2, mod, final_g, final, seq):
    t, d = x2.shape
    tm = min(COMBINE_ROWS, seq)
    last = t // tm - 1
    return pl.pallas_call(
        functools.partial(_combine_kernel, final),
        grid=(t // tm,),
        in_specs=[
            pl.BlockSpec((tm * TOP_K,), lambda i: (i,), memory_space=pltpu.SMEM),
            pl.BlockSpec((tm * TOP_K,), lambda i: (jnp.minimum(i + 1, last),),
                         memory_space=pltpu.SMEM),
            pl.BlockSpec((tm, 2 * TOP_K), lambda i: (i, 0)),
            pl.BlockSpec((tm, d), lambda i: (i, 0)),
            pl.BlockSpec((1, 6, d), lambda i: (i // (seq // tm), 0, 0)),
            pl.BlockSpec((1, d), lambda i: (0, 0)),
            pl.BlockSpec(memory_space=pl.ANY),
        ],
        out_specs=pl.BlockSpec((tm, d), lambda i: (i, 0)),
        out_shape=jax.ShapeDtypeStruct((t, d), F32),
        scratch_shapes=[pltpu.VMEM((2, TOP_K * tm * ROW_TILES, LANES), F32),
                        pltpu.SemaphoreType.DMA((2,))],
        compiler_params=_cp(("arbitrary",), 40),
        name="combine",
    )(dest_flat, dest_flat, gates, x2, mod, final_g.reshape(1, d), yb)


def _moe(h3, logits, x2, mod, layer, w_in, b_in, w_out, b_out, final_g, final, seq):
    t = x2.shape[0]
    tb = EXPERT_ROWS
    idx, gates_t, cnt = _route(logits)
    counts = cnt[:, 0].astype(jnp.int32)
    nblk = (counts + tb - 1) // tb
    blk_end = jnp.cumsum(nblk)
    pad_start = (blk_end - nblk) * tb
    dest = pad_start[idx[:TOP_K]] + idx[TOP_K:]
    dest_flat = dest.T.reshape(t * TOP_K)
    gates = gates_t.T
    n_blocks = (t * TOP_K) // tb + N_EXPERTS
    n_used = blk_end[-1:]
    last_e = jnp.max(jnp.where(nblk > 0, jnp.arange(N_EXPERTS, dtype=jnp.int32), 0))
    blk = jnp.arange(n_blocks, dtype=jnp.int32)
    block_e = jnp.minimum(
        jnp.sum((blk_end[None, :] <= blk[:, None]).astype(jnp.int32), axis=1), last_e)
    experts = jnp.arange(N_EXPERTS, dtype=jnp.int32)
    later = (experts[None, :] > block_e[:, None]) & (nblk[None, :] > 0)
    next_e = jnp.min(jnp.where(later, experts[None, :], N_EXPERTS), axis=1)
    next_e = jnp.where(next_e == N_EXPERTS, -1, next_e).astype(jnp.int32)
    fill_start = jnp.concatenate([pad_start + counts, blk_end[-1:] * tb]).astype(jnp.int32)
    fill_len = jnp.concatenate([nblk * tb - counts, (n_blocks - blk_end[-1:]) * tb]).astype(jnp.int32)
    xs = _dispatch(h3, dest_flat, fill_start, fill_len, n_blocks * tb)
    yb = _experts(xs, block_e, n_used.astype(jnp.int32), next_e, layer, w_in, b_in, w_out, b_out)
    return _combine(yb, dest_flat, gates, x2, mod, final_g, final, seq)


def kernel(x, c, positions, w_ada, b_ada, even_w_in, gla_w_gate, gla_b_gate, gla_norm_g,
           diff_lam_q1, diff_lam_k1, diff_lam_q2, diff_lam_k2, diff_norm_g, even_w_out,
           odd_w_in, sgu_ln_g, sgu_ln_b, sgu_w, sgu_b, odd_w_out,
           router_w, router_b, expert_w_in, expert_b_in, expert_w_out, expert_b_out,
           final_norm_g):
    bsz, seq, d = x.shape
    depth = w_ada.shape[0]
    t = bsz * seq
    mods = _modulation(c, w_ada, b_ada).reshape(depth, bsz, 6, d)
    cos, sin = _rope_tables(positions)
    x2 = x.reshape(t, d)
    for layer in range(depth):
        mod = mods[layer]
        j = layer // 2
        if layer % 2 == 0:
            z, f = _proj_even(x2, mod, even_w_in[j], gla_w_gate[j], gla_b_gate[j], cos, sin, seq)
            o_gla = _gla(z, f, gla_norm_g[j], bsz, seq)
            lam_init = 0.8 - 0.6 * math.exp(-0.3 * layer)
            o_diff = _diff_attn(z, diff_lam_q1[j], diff_lam_k1[j], diff_lam_q2[j], diff_lam_k2[j],
                                diff_norm_g[j], lam_init, bsz, seq)
            mixed, w_out = (o_gla, o_diff), even_w_out[j]
        else:
            z, f = _proj_odd(x2, mod, odd_w_in[j], sgu_ln_g[j], sgu_ln_b[j], cos, sin, seq)
            mixed, w_out = (_sgu_ret(z, f, sgu_w[j], sgu_b[j], bsz, seq),), odd_w_out[j]
        x2, h3, logits = _out_proj(mixed, w_out, x2, mod, router_w[layer], router_b[layer], seq)
        x2 = _moe(h3, logits, x2, mod, layer, expert_w_in, expert_b_in, expert_w_out,
                  expert_b_out, final_norm_g, layer == depth - 1, seq)
    return x2.reshape(bsz, seq, d)
```

```python
import functools
import math

import jax
import jax.numpy as jnp
from jax import lax
from jax.experimental import pallas as pl
from jax.experimental.pallas import tpu as pltpu

F32 = jnp.float32
BF16 = jnp.bfloat16

D_MODEL = 1024
EPS = 1e-6
ROPE_THETA = 10000.0
ROPE_HALF = 32

GLA_HEADS = 4
GLA_DK = 64
GLA_DV = 128
GLA_RANK = 16
GLA_CHUNK = 64
GLA_GATE_NORMALIZER = 16.0
GLA_QK_W = GLA_HEADS * GLA_DK
GLA_V_W = GLA_HEADS * GLA_DV

DIFF_HEADS = 4
DIFF_D = 64
DIFF_DV = 128
DIFF_QK_W = DIFF_HEADS * 2 * DIFF_D
DIFF_V_W = DIFF_HEADS * DIFF_DV

SGU_GROUPS = 4
SGU_CH = 128
SGU_CHUNK = 128
SGU_W = SGU_GROUPS * SGU_CH

RET_HEADS = 4
RET_DK = 64
RET_DV = 128
RET_CHUNK = 128
RET_QK_W = RET_HEADS * RET_DK
RET_V_W = RET_HEADS * RET_DV

N_EXPERTS = 32
TOP_K = 4
D_FF = D_MODEL
SWIGLU_ALPHA = 1.702
SWIGLU_LIMIT = 7.0

LANES = 128
SUBLANES = 8
ROW_TILES = D_MODEL // LANES

PROJ_ROWS = 512
GLA_ROWS = 256
SGU_RET_ROWS = 512
ATT_Q_ROWS = 256
ATT_HEADS_PER_STEP = 2
ROUTE_ROWS = 512
DISPATCH_ROWS = 512
ISSUE_UNROLL = 32
FILL_SHIFT = 6
FILL_ROWS = 1 << FILL_SHIFT
EXPERT_ROWS = 512
COMBINE_ROWS = 256
MOD_COLS = 1536

MIB = 1024 * 1024


def _cp(semantics, vmem_mib):
    return pltpu.CompilerParams(dimension_semantics=semantics, vmem_limit_bytes=vmem_mib * MIB)


def _dot(a, b):
    return jnp.dot(a, b, preferred_element_type=F32)


def _dot_nt(a, b):
    return lax.dot_general(a, b, (((1,), (1,)), ((), ())), preferred_element_type=F32)


def _dot_tn(a, b):
    return lax.dot_general(a, b, (((0,), (0,)), ((), ())), preferred_element_type=F32)


def _rms_rows(x):
    return x * lax.rsqrt(jnp.mean(x * x, axis=-1, keepdims=True) + EPS)


def _silu(x):
    return x * jax.nn.sigmoid(x)


def _rope_chunk(v, cos, sin, lo_half):
    rot = jnp.where(lo_half, -pltpu.roll(v, 96, 1), pltpu.roll(v, 32, 1))
    return v * cos + rot * sin


def _mod_kernel(c_ref, w_ref, b_ref, o_ref):
    c = c_ref[...]
    ca = _silu(c).astype(BF16)
    o_ref[0] = _dot(ca, w_ref[0].astype(BF16)) + b_ref[0]


def _modulation(c, w_ada, b_ada):
    depth, d, n = w_ada.shape
    bsz = c.shape[0]
    return pl.pallas_call(
        _mod_kernel,
        grid=(depth, n // MOD_COLS),
        in_specs=[
            pl.BlockSpec((bsz, d), lambda l, j: (0, 0)),
            pl.BlockSpec((1, d, MOD_COLS), lambda l, j: (l, 0, j)),
            pl.BlockSpec((1, 1, MOD_COLS), lambda l, j: (l, 0, j)),
        ],
        out_specs=pl.BlockSpec((1, bsz, MOD_COLS), lambda l, j: (l, 0, j)),
        out_shape=jax.ShapeDtypeStruct((depth, bsz, n), F32),
        compiler_params=_cp(("arbitrary", "arbitrary"), 40),
        name="adaln_mod",
    )(c, w_ada, b_ada.reshape(depth, 1, n))


def _rope_table_kernel(p_ref, f_ref, c_ref, s_ref):
    ang = p_ref[...].astype(F32) * f_ref[...]
    c_ref[...] = jnp.cos(ang)
    s_ref[...] = jnp.sin(ang)


def _rope_tables(positions):
    t = positions.size
    per_row = LANES // ROPE_HALF
    rows = t // per_row
    pos_d = jnp.repeat(positions.reshape(rows, per_row), ROPE_HALF, axis=1)
    inv_freq = ROPE_THETA ** (-jnp.arange(ROPE_HALF, dtype=F32) / ROPE_HALF)
    freq_d = jnp.tile(inv_freq, per_row).reshape(1, LANES)
    tr = min(512, rows)
    cos_d, sin_d = pl.pallas_call(
        _rope_table_kernel,
        grid=(rows // tr,),
        in_specs=[pl.BlockSpec((tr, LANES), lambda i: (i, 0)),
                  pl.BlockSpec((1, LANES), lambda i: (0, 0))],
        out_specs=[pl.BlockSpec((tr, LANES), lambda i: (i, 0))] * 2,
        out_shape=[jax.ShapeDtypeStruct((rows, LANES), F32)] * 2,
        compiler_params=_cp(("arbitrary",), 32),
        name="rope_tables",
    )(pos_d, freq_d)
    cos = jnp.tile(cos_d.reshape(t, ROPE_HALF), (1, per_row))
    sin = jnp.tile(sin_d.reshape(t, ROPE_HALF), (1, per_row))
    return cos, sin


def _modulated_rms(x_ref, mod_ref, which):
    x = x_ref[...]
    shift = mod_ref[0, 3 * which:3 * which + 1, :]
    scale = mod_ref[0, 3 * which + 1:3 * which + 2, :]
    return _rms_rows(x) * (1.0 + scale) + shift


EVEN_Z = (GLA_V_W, GLA_V_W, DIFF_QK_W, DIFF_QK_W, DIFF_V_W)
EVEN_Z_W = sum(EVEN_Z)
EVEN_F_W = 3 * GLA_QK_W
EVEN_W_COLS = EVEN_Z_W + 2 * GLA_QK_W + LANES


def _proj_even_kernel(x_ref, mod_ref, w_ref, wg_ref, bg_ref, cos_ref, sin_ref, z_ref, f_ref):
    hb = _modulated_rms(x_ref, mod_ref, 0).astype(BF16)

    def sec(a, b):
        return _dot(hb, w_ref[:, a:b])

    for a, b in ((0, 512), (512, 1024), (2048, 2560)):
        z_ref[:, a:b] = sec(a, b).astype(BF16)
    cos = cos_ref[...]
    sin = sin_ref[...]
    lo_half = (lax.broadcasted_iota(jnp.int32, cos.shape, 1) % 64) < ROPE_HALF
    q_scale = DIFF_D ** -0.5
    for a, scl in ((1024, q_scale), (1536, 1.0)):
        full = sec(a, a + DIFF_QK_W)
        for c in range(DIFF_QK_W // LANES):
            v = full[:, c * LANES:(c + 1) * LANES]
            z_ref[:, a + c * LANES:a + (c + 1) * LANES] = (
                _rope_chunk(v, cos, sin, lo_half) * scl).astype(BF16)
    f_ref[:, 0:256] = sec(2560, 2816) * (GLA_DK ** -0.5)
    f_ref[:, 256:512] = sec(2816, 3072)
    gr = sec(3072, 3200).astype(BF16)
    pre = _dot(gr, wg_ref[...]) + bg_ref[...]
    log_sig = jnp.minimum(pre, 0.0) - jnp.log1p(jnp.exp(-jnp.abs(pre)))
    f_ref[:, 512:768] = log_sig / GLA_GATE_NORMALIZER


def _proj_even(x2, mod, w_in, w_gate, b_gate, cos, sin, seq):
    t, d = x2.shape
    tm = min(PROJ_ROWS, seq)
    gq, gk, gv, gr, gg, dq, dk, dv = jnp.split(
        w_in, [256, 512, 1024, 1040, 1552, 2064, 2576], axis=1)
    gr_pad = jnp.pad(gr, ((0, 0), (0, LANES - GLA_RANK)))
    w_cat = jnp.concatenate([gv, gg, dq, dk, dv, gq, gk, gr_pad], axis=1).astype(BF16)
    wg_pad = jnp.pad(w_gate, ((0, LANES - GLA_RANK), (0, 0))).astype(BF16)
    return pl.pallas_call(
        _proj_even_kernel,
        grid=(t // tm,),
        in_specs=[
            pl.BlockSpec((tm, d), lambda i: (i, 0)),
            pl.BlockSpec((1, 6, d), lambda i: (i // (seq // tm), 0, 0)),
            pl.BlockSpec((d, EVEN_W_COLS), lambda i: (0, 0)),
            pl.BlockSpec((LANES, GLA_QK_W), lambda i: (0, 0)),
            pl.BlockSpec((1, GLA_QK_W), lambda i: (0, 0)),
            pl.BlockSpec((tm, LANES), lambda i: (i, 0)),
            pl.BlockSpec((tm, LANES), lambda i: (i, 0)),
        ],
        out_specs=[pl.BlockSpec((tm, EVEN_Z_W), lambda i: (i, 0)),
                   pl.BlockSpec((tm, EVEN_F_W), lambda i: (i, 0))],
        out_shape=[jax.ShapeDtypeStruct((t, EVEN_Z_W), BF16),
                   jax.ShapeDtypeStruct((t, EVEN_F_W), F32)],
        compiler_params=_cp(("arbitrary",), 48),
        name="proj_even",
    )(x2, mod, w_cat, wg_pad, b_gate.reshape(1, GLA_QK_W), cos, sin)


ODD_Z_W = 2 * SGU_W + 2 * RET_V_W
ODD_F_W = 2 * RET_QK_W
ODD_W_COLS = ODD_Z_W + ODD_F_W


def _gelu_exact(x):
    return 0.5 * x * (1.0 + lax.erf(x * (2.0 ** -0.5)))


def _proj_odd_kernel(x_ref, mod_ref, w_ref, lng_ref, lnb_ref, cos_ref, sin_ref, z_ref, f_ref):
    hb = _modulated_rms(x_ref, mod_ref, 0).astype(BF16)

    def sec(a, b):
        return _dot(hb, w_ref[:, a:b])

    z_ref[:, 0:512] = _gelu_exact(sec(0, 512)).astype(BF16)
    sv = _gelu_exact(sec(512, 1024))
    mu = jnp.mean(sv, axis=-1, keepdims=True)
    cen = sv - mu
    var = jnp.mean(cen * cen, axis=-1, keepdims=True)
    z_ref[:, 512:1024] = (cen * lax.rsqrt(var + EPS) * lng_ref[...] + lnb_ref[...]).astype(BF16)
    for a, b in ((1024, 1536), (1536, 2048)):
        z_ref[:, a:b] = sec(a, b).astype(BF16)
    cos = cos_ref[...]
    sin = sin_ref[...]
    lo_half = (lax.broadcasted_iota(jnp.int32, cos.shape, 1) % 64) < ROPE_HALF
    k_scale = RET_DK ** -0.5
    for a, scl in ((0, 1.0), (256, k_scale)):
        full = sec(ODD_Z_W + a, ODD_Z_W + a + RET_QK_W)
        for c in range(RET_QK_W // LANES):
            v = full[:, c * LANES:(c + 1) * LANES]
            f_ref[:, a + c * LANES:a + (c + 1) * LANES] = _rope_chunk(v, cos, sin, lo_half) * scl


def _proj_odd(x2, mod, w_in, ln_g, ln_b, cos, sin, seq):
    t, d = x2.shape
    tm = min(PROJ_ROWS, seq)
    su, sv, rq, rk, rv, rg = jnp.split(w_in, [512, 1024, 1280, 1536, 2048], axis=1)
    w_cat = jnp.concatenate([su, sv, rv, rg, rq, rk], axis=1).astype(BF16)
    return pl.pallas_call(
        _proj_odd_kernel,
        grid=(t // tm,),
        in_specs=[
            pl.BlockSpec((tm, d), lambda i: (i, 0)),
            pl.BlockSpec((1, 6, d), lambda i: (i // (seq // tm), 0, 0)),
            pl.BlockSpec((d, ODD_W_COLS), lambda i: (0, 0)),
            pl.BlockSpec((1, SGU_W), lambda i: (0, 0)),
            pl.BlockSpec((1, SGU_W), lambda i: (0, 0)),
            pl.BlockSpec((tm, LANES), lambda i: (i, 0)),
            pl.BlockSpec((tm, LANES), lambda i: (i, 0)),
        ],
        out_specs=[pl.BlockSpec((tm, ODD_Z_W), lambda i: (i, 0)),
                   pl.BlockSpec((tm, ODD_F_W), lambda i: (i, 0))],
        out_shape=[jax.ShapeDtypeStruct((t, ODD_Z_W), BF16),
                   jax.ShapeDtypeStruct((t, ODD_F_W), F32)],
        compiler_params=_cp(("arbitrary",), 48),
        name="proj_odd",
    )(x2, mod, w_cat, ln_g.reshape(1, SGU_W), ln_b.reshape(1, SGU_W), cos, sin)


def _gla_kernel(q_ref, k_ref, la_ref, v_ref, gg_ref, g_ref, o_ref, st_ref):
    @pl.when(pl.program_id(1) == 0)
    def _():
        st_ref[...] = jnp.zeros_like(st_ref)

    c = GLA_CHUNK
    rows = q_ref.shape[0]
    r_i = lax.broadcasted_iota(jnp.int32, (c, c), 0)
    c_i = lax.broadcasted_iota(jnp.int32, (c, c), 1)
    tril = (r_i >= c_i).astype(BF16)
    r2 = lax.broadcasted_iota(jnp.int32, (2 * c, c), 0) % c
    c2 = lax.broadcasted_iota(jnp.int32, (2 * c, c), 1)
    causal2 = r2 >= c2
    lane = lax.broadcasted_iota(jnp.int32, (c, LANES), 1)
    lo = lane < GLA_DK
    lane_s = lax.broadcasted_iota(jnp.int32, (LANES, LANES), 1) < GLA_DK
    g_row = g_ref[...]
    states = [st_ref[0], st_ref[1]]
    for j in range(rows // c):
        rs = slice(j * c, (j + 1) * c)
        la = la_ref[rs, :]
        la_hi = la.astype(BF16)
        la_lo = (la - la_hi.astype(F32)).astype(BF16)
        b = _dot(tril, la_hi) + _dot(tril, la_lo)
        b_last = b[c - 1:c, :]
        qd = q_ref[rs, :] * jnp.exp(b)
        kk = k_ref[rs, :]
        ki = kk * jnp.exp(-b)
        kd = kk * jnp.exp(b_last - b)
        dec = jnp.exp(b_last)
        for p in range(GLA_HEADS // 2):
            ls = slice(p * LANES, (p + 1) * LANES)
            qd_p = qd[:, ls]
            qm = (jnp.where(lo, qd_p, 0.0).astype(BF16), jnp.where(lo, 0.0, qd_p).astype(BF16))
            s2 = _dot_nt(jnp.concatenate(qm, axis=0), ki[:, ls].astype(BF16))
            s2 = jnp.where(causal2, s2, 0.0).astype(BF16)
            kd_p = kd[:, ls].astype(BF16)
            st_b = states[p].astype(BF16)
            new = []
            for hh in range(2):
                h = 2 * p + hh
                hs = slice(h * GLA_DV, (h + 1) * GLA_DV)
                v_h = v_ref[rs, hs]
                o = _dot(s2[hh * c:(hh + 1) * c, :], v_h) + _dot_nt(qm[hh], st_b)
                o = _rms_rows(o) * g_row
                gate = _silu(gg_ref[rs, hs].astype(F32))
                o_ref[rs, hs] = (o * gate).astype(BF16)
                new.append(_dot_tn(v_h, kd_p))
            states[p] = dec[:, ls] * states[p] + jnp.where(lane_s, new[0], new[1])
    st_ref[0] = states[0]
    st_ref[1] = states[1]


def _gla(z, f, norm_g, bsz, seq):
    t = z.shape[0]
    tc = min(GLA_ROWS, seq)
    nc = seq // tc
    return pl.pallas_call(
        _gla_kernel,
        grid=(bsz, nc),
        in_specs=[
            pl.BlockSpec((tc, GLA_QK_W), lambda b, i: (b * nc + i, 0)),
            pl.BlockSpec((tc, GLA_QK_W), lambda b, i: (b * nc + i, 1)),
            pl.BlockSpec((tc, GLA_QK_W), lambda b, i: (b * nc + i, 2)),
            pl.BlockSpec((tc, GLA_V_W), lambda b, i: (b * nc + i, 0)),
            pl.BlockSpec((tc, GLA_V_W), lambda b, i: (b * nc + i, 1)),
            pl.BlockSpec((1, GLA_DV), lambda b, i: (0, 0)),
        ],
        out_specs=pl.BlockSpec((tc, GLA_V_W), lambda b, i: (b * nc + i, 0)),
        out_shape=jax.ShapeDtypeStruct((t, GLA_V_W), BF16),
        scratch_shapes=[pltpu.VMEM((2, LANES, LANES), F32)],
        compiler_params=_cp(("arbitrary", "arbitrary"), 32),
        name="gla",
    )(f, f, f, z, z, norm_g.reshape(1, GLA_DV))


def _diff_attn_kernel(lam_init, q_ref, k_ref, v_ref, lq1_ref, lk1_ref, lq2_ref, lk2_ref, g_ref,
                      o_ref, vt_ref, s_ref, m_ref, l_ref, acc_ref):
    tq = q_ref.shape[0]
    seq = k_ref.shape[0]
    n_heads = q_ref.shape[1] // LANES
    qi = pl.program_id(2)

    @pl.when(qi == 0)
    def _():
        for hh in range(n_heads):
            for cb in range(seq // tq):
                blk = v_ref[cb * tq:(cb + 1) * tq, hh * LANES:(hh + 1) * LANES]
                vt_ref[hh, :, cb * tq:(cb + 1) * tq] = blk.astype(F32).T.astype(BF16)

    lo = lax.broadcasted_iota(jnp.int32, (tq, LANES), 1) < DIFF_D
    qqs = []
    for hh in range(n_heads):
        q = q_ref[:, hh * LANES:(hh + 1) * LANES]
        zero = jnp.zeros_like(q)
        qqs.append(jnp.concatenate([jnp.where(lo, q, zero), jnp.where(lo, zero, q)], axis=0))

    m_ref[...] = jnp.full(m_ref.shape, -jnp.inf, F32)
    l_ref[...] = jnp.zeros_like(l_ref)
    acc_ref[...] = jnp.zeros_like(acc_ref)

    def scores(j, slot):
        start = pl.multiple_of(j * tq, tq)
        for hh in range(n_heads):
            kj = k_ref[pl.ds(start, tq), hh * LANES:(hh + 1) * LANES]
            s_ref[slot, hh] = _dot_nt(kj, qqs[hh])

    def update(j, slot, masked):
        start = pl.multiple_of(j * tq, tq)
        for hh in range(n_heads):
            vtj = vt_ref[hh, :, pl.ds(start, tq)]
            alpha, p = [], []
            for c in range(2 * tq // LANES):
                cs = slice(c * LANES, (c + 1) * LANES)
                s_c = s_ref[slot, hh, :, cs]
                if masked:
                    kv = lax.broadcasted_iota(jnp.int32, s_c.shape, 0)
                    q_pos = lax.broadcasted_iota(jnp.int32, s_c.shape, 1) + (c * LANES) % tq
                    s_c = jnp.where(kv <= q_pos, s_c, -jnp.inf)
                m_old = m_ref[hh, :, cs]
                m_c = jnp.maximum(m_old, jnp.max(s_c, axis=0, keepdims=True))
                a_c = jnp.exp(m_old - m_c)
                p_c = jnp.exp(s_c - m_c)
                m_ref[hh, :, cs] = m_c
                l_ref[hh, :, cs] = a_c * l_ref[hh, :, cs] + jnp.sum(p_c, axis=0, keepdims=True)
                alpha.append(a_c)
                p.append(p_c.astype(BF16))
            acc_ref[hh] = (jnp.concatenate(alpha, axis=1) * acc_ref[hh]
                           + _dot(vtj, jnp.concatenate(p, axis=1)))

    scores(0, 0)

    def body(i, carry):
        j = 2 * i
        scores(j + 1, 1)
        update(j, 0, False)
        scores(j + 2, 0)
        update(j + 1, 1, False)
        return carry

    n_pairs = qi // 2
    lax.fori_loop(0, n_pairs, body, 0)
    j0 = 2 * n_pairs

    @pl.when(qi % 2 == 1)
    def _():
        scores(j0 + 1, 1)
        update(j0, 0, False)
        update(j0 + 1, 1, True)

    @pl.when(qi % 2 == 0)
    def _():
        update(j0, 0, True)

    lam = (jnp.exp(jnp.sum(lq1_ref[...] * lk1_ref[...], axis=-1, keepdims=True))
           - jnp.exp(jnp.sum(lq2_ref[...] * lk2_ref[...], axis=-1, keepdims=True)) + lam_init)
    for hh in range(n_heads):
        l = l_ref[hh]
        acc = acc_ref[hh]
        o12 = acc / l
        o = o12[:, :tq] - lam * o12[:, tq:]
        o = o * lax.rsqrt(jnp.mean(o * o, axis=0, keepdims=True) + EPS)
        o = o * g_ref[...] * (1.0 - lam_init)
        o_ref[:, hh * DIFF_DV:(hh + 1) * DIFF_DV] = o.T.astype(BF16)


def _diff_attn(z, lq1, lk1, lq2, lk2, norm_g, lam_init, bsz, seq):
    t = z.shape[0]
    tq = min(ATT_Q_ROWS, seq)
    nq = seq // tq
    hw = ATT_HEADS_PER_STEP * LANES
    qb, kb, vb = 1024 // hw, 1536 // hw, 2048 // hw
    small = pl.BlockSpec((1, DIFF_D), lambda b, h, i: (0, 0))
    return pl.pallas_call(
        functools.partial(_diff_attn_kernel, lam_init),
        grid=(bsz, DIFF_HEADS // ATT_HEADS_PER_STEP, nq),
        in_specs=[
            pl.BlockSpec((tq, hw), lambda b, h, i: (b * nq + i, qb + h)),
            pl.BlockSpec((seq, hw), lambda b, h, i: (b, kb + h)),
            pl.BlockSpec((seq, hw), lambda b, h, i: (b, vb + h)),
            small, small, small, small,
            pl.BlockSpec((DIFF_DV, 1), lambda b, h, i: (0, 0)),
        ],
        out_specs=pl.BlockSpec((tq, hw), lambda b, h, i: (b * nq + i, h)),
        out_shape=jax.ShapeDtypeStruct((t, DIFF_V_W), BF16),
        scratch_shapes=[pltpu.VMEM((ATT_HEADS_PER_STEP, DIFF_DV, seq), BF16),
                        pltpu.VMEM((2, ATT_HEADS_PER_STEP, tq, 2 * tq), F32),
                        pltpu.VMEM((ATT_HEADS_PER_STEP, 1, 2 * tq), F32),
                        pltpu.VMEM((ATT_HEADS_PER_STEP, 1, 2 * tq), F32),
                        pltpu.VMEM((ATT_HEADS_PER_STEP, DIFF_DV, 2 * tq), F32)],
        compiler_params=_cp(("arbitrary", "arbitrary", "arbitrary"), 32),
        name="diff_attn",
    )(z, z, z, lq1.reshape(1, DIFF_D), lk1.reshape(1, DIFF_D), lq2.reshape(1, DIFF_D),
      lk2.reshape(1, DIFF_D), norm_g.reshape(DIFF_DV, 1))


def _sgu_ret_kernel(su_ref, sv_ref, rv_ref, rg_ref, q_ref, k_ref, ws_ref, bs_ref, o_ref, st_ref):
    @pl.when(pl.program_id(1) == 0)
    def _():
        st_ref[...] = jnp.zeros_like(st_ref)

    c = RET_CHUNK
    row = lax.broadcasted_iota(jnp.int32, (c, c), 0)
    col = lax.broadcasted_iota(jnp.int32, (c, c), 1)
    causal = row >= col
    log_g = [math.log(1.0 - 2.0 ** (-5.0 - h)) for h in range(RET_HEADS)]
    lo = col < RET_DK
    rel = (row - col).astype(F32)
    pos = row.astype(F32)
    w_sgu = [jnp.where(causal, ws_ref[g], 0.0).astype(BF16) for g in range(SGU_GROUPS)]
    decays = [jnp.where(causal, jnp.exp(log_g[h] * jnp.maximum(rel, 0.0)), 0.0)
              for h in range(RET_HEADS)]
    lgs = [jnp.where(lo, log_g[2 * p], log_g[2 * p + 1]) for p in range(RET_HEADS // 2)]
    q_decs = [jnp.exp(lg * (pos + 1.0)) for lg in lgs]
    k_decs = [jnp.exp(lg * (c - 1.0 - pos)) for lg in lgs]
    states = [st_ref[p] for p in range(RET_HEADS // 2)]
    for j in range(su_ref.shape[0] // c):
        rs = slice(j * c, (j + 1) * c)
        for g in range(SGU_GROUPS):
            gs = slice(g * SGU_CH, (g + 1) * SGU_CH)
            s = _dot(w_sgu[g], sv_ref[rs, gs]) + bs_ref[g]
            o_ref[rs, gs] = (su_ref[rs, gs].astype(F32) * s).astype(BF16)
        for p in range(RET_HEADS // 2):
            ls = slice(p * LANES, (p + 1) * LANES)
            q_p = q_ref[rs, ls]
            k_p = k_ref[rs, ls]
            qm = (jnp.where(lo, q_p, 0.0).astype(BF16), jnp.where(lo, 0.0, q_p).astype(BF16))
            s2 = _dot_nt(jnp.concatenate(qm, axis=0), k_p.astype(BF16))
            qd = q_p * q_decs[p]
            qdm = (jnp.where(lo, qd, 0.0).astype(BF16), jnp.where(lo, 0.0, qd).astype(BF16))
            kd = (k_p * k_decs[p]).astype(BF16)
            st_b = states[p].astype(BF16)
            new = []
            for hh in range(2):
                h = 2 * p + hh
                hs = slice(h * RET_DV, (h + 1) * RET_DV)
                s_h = (s2[hh * c:(hh + 1) * c, :] * decays[h]).astype(BF16)
                v_h = rv_ref[rs, hs]
                o = _dot(s_h, v_h) + _dot_nt(qdm[hh], st_b)
                gate = _silu(rg_ref[rs, hs].astype(F32))
                o_ref[rs, SGU_W + h * RET_DV:SGU_W + (h + 1) * RET_DV] = (
                    _rms_rows(o) * gate).astype(BF16)
                new.append(_dot_tn(v_h, kd))
            states[p] = jnp.exp(lgs[p] * float(c)) * states[p] + jnp.where(lo, new[0], new[1])
    for p in range(RET_HEADS // 2):
        st_ref[p] = states[p]


def _sgu_ret(z, f, w_s, b_s, bsz, seq):
    t = z.shape[0]
    cc = RET_CHUNK
    c = min(SGU_RET_ROWS, seq)
    nc = seq // c
    return pl.pallas_call(
        _sgu_ret_kernel,
        grid=(bsz, nc),
        in_specs=[
            pl.BlockSpec((c, SGU_W), lambda b, i: (b * nc + i, 0)),
            pl.BlockSpec((c, SGU_W), lambda b, i: (b * nc + i, 1)),
            pl.BlockSpec((c, RET_V_W), lambda b, i: (b * nc + i, 2)),
            pl.BlockSpec((c, RET_V_W), lambda b, i: (b * nc + i, 3)),
            pl.BlockSpec((c, RET_QK_W), lambda b, i: (b * nc + i, 0)),
            pl.BlockSpec((c, RET_QK_W), lambda b, i: (b * nc + i, 1)),
            pl.BlockSpec((SGU_GROUPS, cc, cc), lambda b, i: (0, 0, 0)),
            pl.BlockSpec((SGU_GROUPS, cc, 1), lambda b, i: (0, 0, 0)),
        ],
        out_specs=pl.BlockSpec((c, SGU_W + RET_V_W), lambda b, i: (b * nc + i, 0)),
        out_shape=jax.ShapeDtypeStruct((t, SGU_W + RET_V_W), BF16),
        scratch_shapes=[pltpu.VMEM((2, LANES, LANES), F32)],
        compiler_params=_cp(("arbitrary", "arbitrary"), 32),
        name="sgu_retention",
    )(z, z, z, z, f, f, w_s, b_s.reshape(SGU_GROUPS, cc, 1))


def _out_proj_kernel(n_in, *refs):
    o_refs = refs[:n_in]
    w_ref, x_ref, mod_ref, rw_ref, rb_ref, xn_ref, h_ref, lg_ref = refs[n_in:]
    tm = x_ref.shape[0]
    k_each = D_MODEL // n_in
    y = _dot(o_refs[0][...], w_ref[0:k_each, :])
    for n in range(1, n_in):
        y = y + _dot(o_refs[n][...], w_ref[n * k_each:(n + 1) * k_each, :])
    gate1 = mod_ref[0, 2:3, :]
    xn = x_ref[...] + gate1 * y
    xn_ref[...] = xn
    h = _rms_rows(xn) * (1.0 + mod_ref[0, 4:5, :]) + mod_ref[0, 3:4, :]
    for c in range(ROW_TILES):
        h_ref[pl.ds(c, tm, stride=ROW_TILES), :] = h[:, c * LANES:(c + 1) * LANES]
    lg_ref[...] = _dot_nt(rw_ref[...], h.astype(BF16)) + rb_ref[...]


def _out_proj(mixed, w_out, x2, mod, router_w, router_b, seq):
    t, d = x2.shape
    tm = min(PROJ_ROWS, seq)
    n_in = len(mixed)
    k_each = d // n_in
    rw = router_w.T.astype(BF16)
    rb = router_b.reshape(N_EXPERTS, 1)
    return pl.pallas_call(
        functools.partial(_out_proj_kernel, n_in),
        grid=(t // tm,),
        in_specs=[pl.BlockSpec((tm, k_each), lambda i: (i, 0)) for _ in mixed] + [
            pl.BlockSpec((d, d), lambda i: (0, 0)),
            pl.BlockSpec((tm, d), lambda i: (i, 0)),
            pl.BlockSpec((1, 6, d), lambda i: (i // (seq // tm), 0, 0)),
            pl.BlockSpec((N_EXPERTS, d), lambda i: (0, 0)),
            pl.BlockSpec((N_EXPERTS, 1), lambda i: (0, 0)),
        ],
        out_specs=[pl.BlockSpec((tm, d), lambda i: (i, 0)),
                   pl.BlockSpec((tm * ROW_TILES, LANES), lambda i: (i, 0)),
                   pl.BlockSpec((N_EXPERTS, tm), lambda i: (0, i))],
        out_shape=[jax.ShapeDtypeStruct((t, d), F32),
                   jax.ShapeDtypeStruct((t * ROW_TILES, LANES), F32),
                   jax.ShapeDtypeStruct((N_EXPERTS, t), F32)],
        compiler_params=_cp(("arbitrary",), 48),
        name="out_proj",
    )(*mixed, w_out.astype(BF16), x2, mod, rw, rb)


def _route_kernel(lg_ref, idx_ref, gate_ref, cnt_ref, run_ref):
    @pl.when(pl.program_id(0) == 0)
    def _():
        run_ref[...] = jnp.zeros_like(run_ref)

    tm = lg_ref.shape[1]
    row = lax.broadcasted_iota(jnp.int32, (N_EXPERTS, tm), 0)
    neg = -jnp.inf
    l = lg_ref[...]
    vals, firsts, hots = [], [], []
    for _ in range(TOP_K):
        m = jnp.max(l, axis=0, keepdims=True)
        first = jnp.min(jnp.where(l == m, row, N_EXPERTS), axis=0, keepdims=True)
        hot = row == first
        vals.append(m)
        firsts.append(first)
        hots.append(hot)
        l = jnp.where(hot, neg, l)
    sel = hots[0] | hots[1] | hots[2] | hots[3]
    ex = [jnp.exp(v - vals[0]) for v in vals]
    denom = ex[0] + ex[1] + ex[2] + ex[3]
    r_i = lax.broadcasted_iota(jnp.int32, (tm, tm), 0)
    c_i = lax.broadcasted_iota(jnp.int32, (tm, tm), 1)
    before = (r_i < c_i).astype(BF16)
    run = run_ref[:, 0:1]
    cum = _dot(sel.astype(BF16), before) + run
    row8 = lax.broadcasted_iota(jnp.int32, (2 * TOP_K, tm), 0)
    idx_out = jnp.zeros((2 * TOP_K, tm), jnp.int32)
    gate_out = jnp.zeros((2 * TOP_K, tm), F32)
    for k in range(TOP_K):
        rank_k = jnp.sum(jnp.where(hots[k], cum, 0.0), axis=0, keepdims=True).astype(jnp.int32)
        idx_out = jnp.where(row8 == k, firsts[k], idx_out)
        idx_out = jnp.where(row8 == TOP_K + k, rank_k, idx_out)
        gate_out = jnp.where(row8 == k, ex[k] / denom, gate_out)
    idx_ref[...] = idx_out
    gate_ref[...] = gate_out
    total = run + jnp.sum(sel.astype(F32), axis=1, keepdims=True)
    run_ref[...] = jnp.broadcast_to(total, run_ref.shape)
    cnt_ref[...] = jnp.broadcast_to(total, cnt_ref.shape)


def _route(logits_t):
    t = logits_t.shape[1]
    tm = min(ROUTE_ROWS, t)
    return pl.pallas_call(
        _route_kernel,
        grid=(t // tm,),
        in_specs=[pl.BlockSpec((N_EXPERTS, tm), lambda i: (0, i))],
        out_specs=[pl.BlockSpec((2 * TOP_K, tm), lambda i: (0, i)),
                   pl.BlockSpec((2 * TOP_K, tm), lambda i: (0, i)),
                   pl.BlockSpec((N_EXPERTS, LANES), lambda i: (0, 0))],
        out_shape=[jax.ShapeDtypeStruct((2 * TOP_K, t), jnp.int32),
                   jax.ShapeDtypeStruct((2 * TOP_K, t), F32),
                   jax.ShapeDtypeStruct((N_EXPERTS, LANES), F32)],
        scratch_shapes=[pltpu.VMEM((N_EXPERTS, LANES), F32)],
        compiler_params=_cp(("arbitrary",), 32),
        name="route",
    )(logits_t)


def _dispatch_kernel(fs_ref, fl_ref, h_ref, dest_ref, xs_ref, zero_buf, sem, zsem):
    tm = h_ref.shape[0] // ROW_TILES
    n_fill = fs_ref.shape[0]

    def rows(ref, start, n):
        return ref.at[pl.ds(pl.multiple_of(start * ROW_TILES, ROW_TILES), n * ROW_TILES), :]

    def fill_copies(f, wait):
        start, n = fs_ref[f], fl_ref[f]
        n_chunks = lax.shift_right_logical(n, FILL_SHIFT)
        tail = start + n_chunks * FILL_ROWS

        def chunk(j, carry):
            cp = pltpu.make_async_copy(zero_buf, rows(xs_ref, start + j * FILL_ROWS, FILL_ROWS), zsem)
            cp.wait() if wait else cp.start()
            return carry

        def single(j, carry):
            cp = pltpu.make_async_copy(rows(zero_buf, 0, 1), rows(xs_ref, tail + j, 1), zsem)
            cp.wait() if wait else cp.start()
            return carry

        lax.fori_loop(0, n_chunks, chunk, 0)
        lax.fori_loop(0, n - n_chunks * FILL_ROWS, single, 0)

    @pl.when(pl.program_id(0) == 0)
    def _():
        zero_buf[...] = jnp.zeros_like(zero_buf)
        lax.fori_loop(0, n_fill, lambda f, c: (fill_copies(f, False), c)[1], 0)

    def issue(rb, carry):
        for rr in range(ISSUE_UNROLL):
            r = rb * ISSUE_UNROLL + rr
            for k in range(TOP_K):
                pltpu.make_async_copy(rows(h_ref, r, 1), rows(xs_ref, dest_ref[r * TOP_K + k], 1),
                                      sem).start(priority=k % 2)
        return carry

    lax.fori_loop(0, tm // ISSUE_UNROLL, issue, 0)
    for _ in range(TOP_K):
        pltpu.make_async_copy(h_ref, rows(xs_ref, 0, tm), sem).wait()

    @pl.when(pl.program_id(0) == 0)
    def _():
        lax.fori_loop(0, n_fill, lambda f, c: (fill_copies(f, True), c)[1], 0)


def _dispatch(h3, dest_flat, fill_start, fill_len, n_rows):
    t = h3.shape[0] // ROW_TILES
    tm = min(DISPATCH_ROWS, t)
    grid_spec = pltpu.PrefetchScalarGridSpec(
        num_scalar_prefetch=2,
        grid=(t // tm,),
        in_specs=[pl.BlockSpec((tm * ROW_TILES, LANES), lambda i, fs, fl: (i, 0)),
                  pl.BlockSpec((tm * TOP_K,), lambda i, fs, fl: (i,), memory_space=pltpu.SMEM)],
        out_specs=pl.BlockSpec(memory_space=pl.ANY),
        scratch_shapes=[pltpu.VMEM((FILL_ROWS * ROW_TILES, LANES), F32),
                        pltpu.SemaphoreType.DMA(()), pltpu.SemaphoreType.DMA(())],
    )
    return pl.pallas_call(
        _dispatch_kernel,
        grid_spec=grid_spec,
        out_shape=jax.ShapeDtypeStruct((n_rows * ROW_TILES, LANES), F32),
        compiler_params=_cp(("arbitrary",), 32),
        name="dispatch",
    )(fill_start, fill_len, h3, dest_flat)


def _expert_kernel(layer, be_ref, nu_ref, nx_ref, xs_ref, wi_hbm, bi_ref, wo_hbm, bo_ref, y_ref,
                   wi_st, wo_st, wi_b, wo_b, sems):
    i = pl.program_id(0)
    tb = xs_ref.shape[0] // ROW_TILES
    e = be_ref[i]
    fresh = jnp.logical_or(i == 0, e != be_ref[jnp.maximum(i - 1, 0)])
    used = i < nu_ref[0]

    def fetch(ex):
        return (pltpu.make_async_copy(wi_hbm.at[layer, ex], wi_st, sems.at[0]),
                pltpu.make_async_copy(wo_hbm.at[layer, ex], wo_st, sems.at[1]))

    @pl.when(i == 0)
    def _():
        for cp in fetch(e):
            cp.start()

    @pl.when(jnp.logical_and(fresh, used))
    def _():
        for cp in fetch(e):
            cp.wait()
        wi_b[...] = wi_st[...].astype(BF16)
        wo_b[...] = wo_st[...].astype(BF16)

        @pl.when(nx_ref[i] >= 0)
        def _():
            for cp in fetch(nx_ref[i]):
                cp.start()

    @pl.when(used)
    def _():
        x = jnp.concatenate(
            [xs_ref[pl.ds(c, tb, stride=ROW_TILES), :] for c in range(ROW_TILES)],
            axis=1).astype(BF16)
        y = jnp.zeros((tb, D_MODEL), F32) + bo_ref[0, 0]
        half = 512
        for j in range(D_FF // half):
            a, b = j * half, (j + 1) * half
            glu = _dot(x, wi_b[:, a:b]) + bi_ref[0, 0, :, a:b]
            lin = _dot(x, wi_b[:, D_FF + a:D_FF + b]) + bi_ref[0, 0, :, D_FF + a:D_FF + b]
            glu = jnp.minimum(glu, SWIGLU_LIMIT)
            lin = jnp.clip(lin, -SWIGLU_LIMIT, SWIGLU_LIMIT)
            act = glu * jax.nn.sigmoid(SWIGLU_ALPHA * glu) * (lin + 1.0)
            y = y + _dot(act.astype(BF16), wo_b[a:b, :])
        for c in range(ROW_TILES):
            y_ref[pl.ds(c, tb, stride=ROW_TILES), :] = y[:, c * LANES:(c + 1) * LANES]

    @pl.when(jnp.logical_not(used))
    def _():
        y_ref[...] = jnp.zeros_like(y_ref)


def _experts(xs, block_e, n_used, next_e, layer, w_in, b_in, w_out, b_out):
    tb = EXPERT_ROWS
    n_rows = xs.shape[0] // ROW_TILES
    nb = n_rows // tb
    depth, ne, d, f2 = w_in.shape

    def row_map(i, be, nu, nx):
        return (jnp.minimum(i, nu[0] - 1), 0)

    grid_spec = pltpu.PrefetchScalarGridSpec(
        num_scalar_prefetch=3,
        grid=(nb,),
        in_specs=[
            pl.BlockSpec((tb * ROW_TILES, LANES), row_map),
            pl.BlockSpec(memory_space=pl.ANY),
            pl.BlockSpec((1, 1, 1, f2), lambda i, be, nu, nx: (layer, be[i], 0, 0)),
            pl.BlockSpec(memory_space=pl.ANY),
            pl.BlockSpec((1, 1, 1, d), lambda i, be, nu, nx: (layer, be[i], 0, 0)),
        ],
        out_specs=pl.BlockSpec((tb * ROW_TILES, LANES), lambda i, be, nu, nx: (i, 0)),
        scratch_shapes=[pltpu.VMEM((d, f2), F32), pltpu.VMEM((D_FF, d), F32),
                        pltpu.VMEM((d, f2), BF16), pltpu.VMEM((D_FF, d), BF16),
                        pltpu.SemaphoreType.DMA((2,))],
    )
    return pl.pallas_call(
        functools.partial(_expert_kernel, layer),
        grid_spec=grid_spec,
        out_shape=jax.ShapeDtypeStruct((n_rows * ROW_TILES, LANES), F32),
        compiler_params=_cp(("arbitrary",), 56),
        name="experts",
    )(block_e, n_used, next_e, xs, w_in, b_in.reshape(depth, ne, 1, f2), w_out,
      b_out.reshape(depth, ne, 1, d))


def _combine_kernel(final, dest_ref, dnext_ref, gate_ref, x_ref, mod_ref, fg_ref, yb_ref, o_ref,
                    g_buf, sems):
    tm = x_ref.shape[0]
    i = pl.program_id(0)
    cur = lax.rem(i, 2)

    def issue_all(d_ref, buf):
        def issue(rb, carry):
            for rr in range(ISSUE_UNROLL):
                r = rb * ISSUE_UNROLL + rr
                for k in range(TOP_K):
                    d = d_ref[r * TOP_K + k]
                    pltpu.make_async_copy(
                        yb_ref.at[pl.ds(pl.multiple_of(d * ROW_TILES, ROW_TILES), ROW_TILES), :],
                        g_buf.at[buf, pl.ds(pl.multiple_of((k * tm + r) * ROW_TILES, ROW_TILES),
                                            ROW_TILES), :],
                        sems.at[buf]).start(priority=k % 2)
            return carry

        lax.fori_loop(0, tm // ISSUE_UNROLL, issue, 0)

    @pl.when(i == 0)
    def _():
        issue_all(dest_ref, 0)

    @pl.when(i + 1 < pl.num_programs(0))
    def _():
        issue_all(dnext_ref, 1 - cur)

    pltpu.make_async_copy(yb_ref.at[pl.ds(0, TOP_K * tm * ROW_TILES), :], g_buf.at[cur],
                          sems.at[cur]).wait()

    gates = gate_ref[...]
    gate2 = mod_ref[0, 5:6, :]
    for c in range(ROW_TILES):
        y = jnp.zeros((tm, LANES), F32)
        for k in range(TOP_K):
            rows = g_buf[cur, pl.ds(k * tm * ROW_TILES + c, tm, stride=ROW_TILES), :]
            y = y + gates[:, k:k + 1] * rows
        cs = slice(c * LANES, (c + 1) * LANES)
        o_ref[:, cs] = x_ref[:, cs] + gate2[:, cs] * y
    if final:
        o_ref[...] = _rms_rows(o_ref[...]) * fg_ref[...]


def _combine(yb, dest_flat, gates, x2, mod, final_g, final, seq):
    t, d = x2.shape
    tm = min(COMBINE_ROWS, seq)
    last = t // tm - 1
    return pl.pallas_call(
        functools.partial(_combine_kernel, final),
        grid=(t // tm,),
        in_specs=[
            pl.BlockSpec((tm * TOP_K,), lambda i: (i,), memory_space=pltpu.SMEM),
            pl.BlockSpec((tm * TOP_K,), lambda i: (jnp.minimum(i + 1, last),),
                         memory_space=pltpu.SMEM),
            pl.BlockSpec((tm, 2 * TOP_K), lambda i: (i, 0)),
            pl.BlockSpec((tm, d), lambda i: (i, 0)),
            pl.BlockSpec((1, 6, d), lambda i: (i // (seq // tm), 0, 0)),
            pl.BlockSpec((1, d), lambda i: (0, 0)),
            pl.BlockSpec(memory_space=pl.ANY),
        ],
        out_specs=pl.BlockSpec((tm, d), lambda i: (i, 0)),
        out_shape=jax.ShapeDtypeStruct((t, d), F32),
        scratch_shapes=[pltpu.VMEM((2, TOP_K * tm * ROW_TILES, LANES), F32),
                        pltpu.SemaphoreType.DMA((2,))],
        compiler_params=_cp(("arbitrary",), 40),
        name="combine",
    )(dest_flat, dest_flat, gates, x2, mod, final_g.reshape(1, d), yb)


def _moe(h3, logits, x2, mod, layer, w_in, b_in, w_out, b_out, final_g, final, seq):
    t = x2.shape[0]
    tb = EXPERT_ROWS
    idx, gates_t, cnt = _route(logits)
    counts = cnt[:, 0].astype(jnp.int32)
    nblk = (counts + tb - 1) // tb
    blk_end = jnp.cumsum(nblk)
    pad_start = (blk_end - nblk) * tb
    e_hot = idx[:TOP_K, None, :] == jnp.arange(N_EXPERTS, dtype=jnp.int32)[None, :, None]
    dest = jnp.sum(jnp.where(e_hot, pad_start[None, :, None], 0), axis=1) + idx[TOP_K:]
    dest_flat = dest.T.reshape(t * TOP_K)
    gates = gates_t.T
    n_blocks = (t * TOP_K) // tb + N_EXPERTS
    n_used = blk_end[-1:]
    last_e = jnp.max(jnp.where(nblk > 0, jnp.arange(N_EXPERTS, dtype=jnp.int32), 0))
    blk = jnp.arange(n_blocks, dtype=jnp.int32)
    block_e = jnp.minimum(
        jnp.sum((blk_end[None, :] <= blk[:, None]).astype(jnp.int32), axis=1), last_e)
    experts = jnp.arange(N_EXPERTS, dtype=jnp.int32)
    later = (experts[None, :] > block_e[:, None]) & (nblk[None, :] > 0)
    next_e = jnp.min(jnp.where(later, experts[None, :], N_EXPERTS), axis=1)
    next_e = jnp.where(next_e == N_EXPERTS, -1, next_e).astype(jnp.int32)
    fill_start = jnp.concatenate([pad_start + counts, blk_end[-1:] * tb]).astype(jnp.int32)
    fill_len = jnp.concatenate([nblk * tb - counts, (n_blocks - blk_end[-1:]) * tb]).astype(jnp.int32)
    xs = _dispatch(h3, dest_flat, fill_start, fill_len, n_blocks * tb)
    yb = _experts(xs, block_e, n_used.astype(jnp.int32), next_e, layer, w_in, b_in, w_out, b_out)
    return _combine(yb, dest_flat, gates, x2, mod, final_g, final, seq)


def kernel(x, c, positions, w_ada, b_ada, even_w_in, gla_w_gate, gla_b_gate, gla_norm_g,
           diff_lam_q1, diff_lam_k1, diff_lam_q2, diff_lam_k2, diff_norm_g, even_w_out,
           odd_w_in, sgu_ln_g, sgu_ln_b, sgu_w, sgu_b, odd_w_out,
           router_w, router_b, expert_w_in, expert_b_in, expert_w_out, expert_b_out,
           final_norm_g):
    bsz, seq, d = x.shape
    depth = w_ada.shape[0]
    t = bsz * seq
    mods = _modulation(c, w_ada, b_ada).reshape(depth, bsz, 6, d)
    cos, sin = _rope_tables(positions)
    x2 = x.reshape(t, d)
    for layer in range(depth):
        mod = mods[layer]
        j = layer // 2
        if layer % 2 == 0:
            z, f = _proj_even(x2, mod, even_w_in[j], gla_w_gate[j], gla_b_gate[j], cos, sin, seq)
            o_gla = _gla(z, f, gla_norm_g[j], bsz, seq)
            lam_init = 0.8 - 0.6 * math.exp(-0.3 * layer)
            o_diff = _diff_attn(z, diff_lam_q1[j], diff_lam_k1[j], diff_lam_q2[j], diff_lam_k2[j],
                                diff_norm_g[j], lam_init, bsz, seq)
            mixed, w_out = (o_gla, o_diff), even_w_out[j]
        else:
            z, f = _proj_odd(x2, mod, odd_w_in[j], sgu_ln_g[j], sgu_ln_b[j], cos, sin, seq)
            mixed, w_out = (_sgu_ret(z, f, sgu_w[j], sgu_b[j], bsz, seq),), odd_w_out[j]
        x2, h3, logits = _out_proj(mixed, w_out, x2, mod, router_w[layer], router_b[layer], seq)
        x2 = _moe(h3, logits, x2, mod, layer, expert_w_in, expert_b_in, expert_w_out,
                  expert_b_out, final_norm_g, layer == depth - 1, seq)
    return x2.reshape(bsz, seq, d)
```

```python
import functools
import math

import jax
import jax.numpy as jnp
from jax import lax
from jax.experimental import pallas as pl
from jax.experimental.pallas import tpu as pltpu

F32 = jnp.float32
BF16 = jnp.bfloat16

D_MODEL = 1024
EPS = 1e-6
ROPE_THETA = 10000.0
ROPE_HALF = 32

GLA_HEADS = 4
GLA_DK = 64
GLA_DV = 128
GLA_RANK = 16
GLA_CHUNK = 64
GLA_GATE_NORMALIZER = 16.0
GLA_QK_W = GLA_HEADS * GLA_DK
GLA_V_W = GLA_HEADS * GLA_DV

DIFF_HEADS = 4
DIFF_D = 64
DIFF_DV = 128
DIFF_QK_W = DIFF_HEADS * 2 * DIFF_D
DIFF_V_W = DIFF_HEADS * DIFF_DV

SGU_GROUPS = 4
SGU_CH = 128
SGU_CHUNK = 128
SGU_W = SGU_GROUPS * SGU_CH

RET_HEADS = 4
RET_DK = 64
RET_DV = 128
RET_CHUNK = 128
RET_QK_W = RET_HEADS * RET_DK
RET_V_W = RET_HEADS * RET_DV

N_EXPERTS = 32
TOP_K = 4
D_FF = D_MODEL
SWIGLU_ALPHA = 1.702
SWIGLU_LIMIT = 7.0

LANES = 128
SUBLANES = 8
ROW_TILES = D_MODEL // LANES

PROJ_ROWS = 512
OUT_PROJ_SUB_ROWS = 256
GLA_ROWS = 256
SGU_RET_ROWS = 512
ATT_Q_ROWS = 256
ATT_HEADS_PER_STEP = 2
ROUTE_ROWS = 512
DISPATCH_ROWS = 512
ISSUE_UNROLL = 32
FILL_SHIFT = 6
FILL_ROWS = 1 << FILL_SHIFT
EXPERT_ROWS = 512
COMBINE_ROWS = 256
MOD_COLS = 1536

MIB = 1024 * 1024


def _cp(semantics, vmem_mib):
    return pltpu.CompilerParams(dimension_semantics=semantics, vmem_limit_bytes=vmem_mib * MIB)


def _dot(a, b):
    return jnp.dot(a, b, preferred_element_type=F32)


def _dot_nt(a, b):
    return lax.dot_general(a, b, (((1,), (1,)), ((), ())), preferred_element_type=F32)


def _dot_tn(a, b):
    return lax.dot_general(a, b, (((0,), (0,)), ((), ())), preferred_element_type=F32)


def _rms_rows(x):
    return x * lax.rsqrt(jnp.mean(x * x, axis=-1, keepdims=True) + EPS)


def _silu(x):
    return x * jax.nn.sigmoid(x)


def _rope_chunk(v, cos, sin, lo_half):
    rot = jnp.where(lo_half, -pltpu.roll(v, 96, 1), pltpu.roll(v, 32, 1))
    return v * cos + rot * sin


def _mod_kernel(c_ref, w_ref, b_ref, o_ref):
    c = c_ref[...]
    ca = _silu(c).astype(BF16)
    o_ref[0] = _dot(ca, w_ref[0].astype(BF16)) + b_ref[0]


def _modulation(c, w_ada, b_ada):
    depth, d, n = w_ada.shape
    bsz = c.shape[0]
    return pl.pallas_call(
        _mod_kernel,
        grid=(depth, n // MOD_COLS),
        in_specs=[
            pl.BlockSpec((bsz, d), lambda l, j: (0, 0)),
            pl.BlockSpec((1, d, MOD_COLS), lambda l, j: (l, 0, j)),
            pl.BlockSpec((1, 1, MOD_COLS), lambda l, j: (l, 0, j)),
        ],
        out_specs=pl.BlockSpec((1, bsz, MOD_COLS), lambda l, j: (l, 0, j)),
        out_shape=jax.ShapeDtypeStruct((depth, bsz, n), F32),
        compiler_params=_cp(("arbitrary", "arbitrary"), 40),
        name="adaln_mod",
    )(c, w_ada, b_ada.reshape(depth, 1, n))


def _rope_table_kernel(p_ref, f_ref, c_ref, s_ref):
    ang = p_ref[...].astype(F32) * f_ref[...]
    c_ref[...] = jnp.cos(ang)
    s_ref[...] = jnp.sin(ang)


def _rope_tables(positions):
    t = positions.size
    per_row = LANES // ROPE_HALF
    rows = t // per_row
    pos_d = jnp.repeat(positions.reshape(rows, per_row), ROPE_HALF, axis=1)
    inv_freq = ROPE_THETA ** (-jnp.arange(ROPE_HALF, dtype=F32) / ROPE_HALF)
    freq_d = jnp.tile(inv_freq, per_row).reshape(1, LANES)
    tr = min(512, rows)
    cos_d, sin_d = pl.pallas_call(
        _rope_table_kernel,
        grid=(rows // tr,),
        in_specs=[pl.BlockSpec((tr, LANES), lambda i: (i, 0)),
                  pl.BlockSpec((1, LANES), lambda i: (0, 0))],
        out_specs=[pl.BlockSpec((tr, LANES), lambda i: (i, 0))] * 2,
        out_shape=[jax.ShapeDtypeStruct((rows, LANES), F32)] * 2,
        compiler_params=_cp(("arbitrary",), 32),
        name="rope_tables",
    )(pos_d, freq_d)
    cos = jnp.tile(cos_d.reshape(t, ROPE_HALF), (1, per_row))
    sin = jnp.tile(sin_d.reshape(t, ROPE_HALF), (1, per_row))
    return cos, sin


def _modulated_rms(x_ref, mod_ref, which):
    x = x_ref[...]
    shift = mod_ref[0, 3 * which:3 * which + 1, :]
    scale = mod_ref[0, 3 * which + 1:3 * which + 2, :]
    return _rms_rows(x) * (1.0 + scale) + shift


EVEN_Z = (GLA_V_W, GLA_V_W, DIFF_QK_W, DIFF_QK_W, DIFF_V_W)
EVEN_Z_W = sum(EVEN_Z)
EVEN_F_W = 3 * GLA_QK_W
EVEN_W_COLS = EVEN_Z_W + 2 * GLA_QK_W + LANES


def _proj_even_kernel(x_ref, mod_ref, w_ref, wg_ref, bg_ref, cos_ref, sin_ref, z_ref, f_ref):
    hb = _modulated_rms(x_ref, mod_ref, 0).astype(BF16)

    def sec(a, b):
        return _dot(hb, w_ref[:, a:b])

    for a, b in ((0, 512), (512, 1024), (2048, 2560)):
        z_ref[:, a:b] = sec(a, b).astype(BF16)
    cos = cos_ref[...]
    sin = sin_ref[...]
    lo_half = (lax.broadcasted_iota(jnp.int32, cos.shape, 1) % 64) < ROPE_HALF
    q_scale = DIFF_D ** -0.5 * math.log2(math.e)
    for a, scl in ((1024, q_scale), (1536, 1.0)):
        full = sec(a, a + DIFF_QK_W)
        for c in range(DIFF_QK_W // LANES):
            v = full[:, c * LANES:(c + 1) * LANES]
            z_ref[:, a + c * LANES:a + (c + 1) * LANES] = (
                _rope_chunk(v, cos, sin, lo_half) * scl).astype(BF16)
    f_ref[:, 0:256] = sec(2560, 2816) * (GLA_DK ** -0.5)
    f_ref[:, 256:512] = sec(2816, 3072)
    gr = sec(3072, 3200).astype(BF16)
    pre = _dot(gr, wg_ref[...]) + bg_ref[...]
    log_sig = jnp.minimum(pre, 0.0) - jnp.log1p(jnp.exp(-jnp.abs(pre)))
    f_ref[:, 512:768] = log_sig / GLA_GATE_NORMALIZER


def _proj_even(x2, mod, w_in, w_gate, b_gate, cos, sin, seq):
    t, d = x2.shape
    tm = min(PROJ_ROWS, seq)
    gq, gk, gv, gr, gg, dq, dk, dv = jnp.split(
        w_in, [256, 512, 1024, 1040, 1552, 2064, 2576], axis=1)
    gr_pad = jnp.pad(gr, ((0, 0), (0, LANES - GLA_RANK)))
    w_cat = jnp.concatenate([gv, gg, dq, dk, dv, gq, gk, gr_pad], axis=1).astype(BF16)
    wg_pad = jnp.pad(w_gate, ((0, LANES - GLA_RANK), (0, 0))).astype(BF16)
    return pl.pallas_call(
        _proj_even_kernel,
        grid=(t // tm,),
        in_specs=[
            pl.BlockSpec((tm, d), lambda i: (i, 0)),
            pl.BlockSpec((1, 6, d), lambda i: (i // (seq // tm), 0, 0)),
            pl.BlockSpec((d, EVEN_W_COLS), lambda i: (0, 0)),
            pl.BlockSpec((LANES, GLA_QK_W), lambda i: (0, 0)),
            pl.BlockSpec((1, GLA_QK_W), lambda i: (0, 0)),
            pl.BlockSpec((tm, LANES), lambda i: (i, 0)),
            pl.BlockSpec((tm, LANES), lambda i: (i, 0)),
        ],
        out_specs=[pl.BlockSpec((tm, EVEN_Z_W), lambda i: (i, 0)),
                   pl.BlockSpec((tm, EVEN_F_W), lambda i: (i, 0))],
        out_shape=[jax.ShapeDtypeStruct((t, EVEN_Z_W), BF16),
                   jax.ShapeDtypeStruct((t, EVEN_F_W), F32)],
        compiler_params=_cp(("arbitrary",), 48),
        name="proj_even",
    )(x2, mod, w_cat, wg_pad, b_gate.reshape(1, GLA_QK_W), cos, sin)


ODD_Z_W = 2 * SGU_W + 2 * RET_V_W
ODD_F_W = 2 * RET_QK_W
ODD_W_COLS = ODD_Z_W + ODD_F_W


def _gelu_exact(x):
    return 0.5 * x * (1.0 + lax.erf(x * (2.0 ** -0.5)))


def _proj_odd_kernel(x_ref, mod_ref, w_ref, lng_ref, lnb_ref, cos_ref, sin_ref, z_ref, f_ref):
    hb = _modulated_rms(x_ref, mod_ref, 0).astype(BF16)

    def sec(a, b):
        return _dot(hb, w_ref[:, a:b])

    z_ref[:, 0:512] = _gelu_exact(sec(0, 512)).astype(BF16)
    sv = _gelu_exact(sec(512, 1024))
    mu = jnp.mean(sv, axis=-1, keepdims=True)
    cen = sv - mu
    var = jnp.mean(cen * cen, axis=-1, keepdims=True)
    z_ref[:, 512:1024] = (cen * lax.rsqrt(var + EPS) * lng_ref[...] + lnb_ref[...]).astype(BF16)
    for a, b in ((1024, 1536), (1536, 2048)):
        z_ref[:, a:b] = sec(a, b).astype(BF16)
    cos = cos_ref[...]
    sin = sin_ref[...]
    lo_half = (lax.broadcasted_iota(jnp.int32, cos.shape, 1) % 64) < ROPE_HALF
    k_scale = RET_DK ** -0.5
    for a, scl in ((0, 1.0), (256, k_scale)):
        full = sec(ODD_Z_W + a, ODD_Z_W + a + RET_QK_W)
        for c in range(RET_QK_W // LANES):
            v = full[:, c * LANES:(c + 1) * LANES]
            f_ref[:, a + c * LANES:a + (c + 1) * LANES] = _rope_chunk(v, cos, sin, lo_half) * scl


def _proj_odd(x2, mod, w_in, ln_g, ln_b, cos, sin, seq):
    t, d = x2.shape
    tm = min(PROJ_ROWS, seq)
    su, sv, rq, rk, rv, rg = jnp.split(w_in, [512, 1024, 1280, 1536, 2048], axis=1)
    w_cat = jnp.concatenate([su, sv, rv, rg, rq, rk], axis=1).astype(BF16)
    return pl.pallas_call(
        _proj_odd_kernel,
        grid=(t // tm,),
        in_specs=[
            pl.BlockSpec((tm, d), lambda i: (i, 0)),
            pl.BlockSpec((1, 6, d), lambda i: (i // (seq // tm), 0, 0)),
            pl.BlockSpec((d, ODD_W_COLS), lambda i: (0, 0)),
            pl.BlockSpec((1, SGU_W), lambda i: (0, 0)),
            pl.BlockSpec((1, SGU_W), lambda i: (0, 0)),
            pl.BlockSpec((tm, LANES), lambda i: (i, 0)),
            pl.BlockSpec((tm, LANES), lambda i: (i, 0)),
        ],
        out_specs=[pl.BlockSpec((tm, ODD_Z_W), lambda i: (i, 0)),
                   pl.BlockSpec((tm, ODD_F_W), lambda i: (i, 0))],
        out_shape=[jax.ShapeDtypeStruct((t, ODD_Z_W), BF16),
                   jax.ShapeDtypeStruct((t, ODD_F_W), F32)],
        compiler_params=_cp(("arbitrary",), 48),
        name="proj_odd",
    )(x2, mod, w_cat, ln_g.reshape(1, SGU_W), ln_b.reshape(1, SGU_W), cos, sin)


def _gla_kernel(q_ref, k_ref, la_ref, v_ref, gg_ref, g_ref, o_ref, st_ref):
    @pl.when(pl.program_id(1) == 0)
    def _():
        st_ref[...] = jnp.zeros_like(st_ref)

    c = GLA_CHUNK
    rows = q_ref.shape[0]
    r_i = lax.broadcasted_iota(jnp.int32, (c, c), 0)
    c_i = lax.broadcasted_iota(jnp.int32, (c, c), 1)
    tril = (r_i >= c_i).astype(BF16)
    r2 = lax.broadcasted_iota(jnp.int32, (2 * c, c), 0) % c
    c2 = lax.broadcasted_iota(jnp.int32, (2 * c, c), 1)
    causal2 = r2 >= c2
    lane = lax.broadcasted_iota(jnp.int32, (c, LANES), 1)
    lo = lane < GLA_DK
    lane_s = lax.broadcasted_iota(jnp.int32, (LANES, LANES), 1) < GLA_DK
    g_row = g_ref[...]
    states = [st_ref[0], st_ref[1]]
    for j in range(rows // c):
        rs = slice(j * c, (j + 1) * c)
        la = la_ref[rs, :]
        la_hi = la.astype(BF16)
        la_lo = (la - la_hi.astype(F32)).astype(BF16)
        b = _dot(tril, la_hi) + _dot(tril, la_lo)
        b_last = b[c - 1:c, :]
        qd = q_ref[rs, :] * jnp.exp(b)
        kk = k_ref[rs, :]
        ki = kk * jnp.exp(-b)
        kd = kk * jnp.exp(b_last - b)
        dec = jnp.exp(b_last)
        for p in range(GLA_HEADS // 2):
            ls = slice(p * LANES, (p + 1) * LANES)
            qd_p = qd[:, ls]
            qm = (jnp.where(lo, qd_p, 0.0).astype(BF16), jnp.where(lo, 0.0, qd_p).astype(BF16))
            s2 = _dot_nt(jnp.concatenate(qm, axis=0), ki[:, ls].astype(BF16))
            s2 = jnp.where(causal2, s2, 0.0).astype(BF16)
            kd_p = kd[:, ls].astype(BF16)
            st_b = states[p].astype(BF16)
            new = []
            for hh in range(2):
                h = 2 * p + hh
                hs = slice(h * GLA_DV, (h + 1) * GLA_DV)
                v_h = v_ref[rs, hs]
                o = _dot(s2[hh * c:(hh + 1) * c, :], v_h) + _dot_nt(qm[hh], st_b)
                o = _rms_rows(o) * g_row
                gate = _silu(gg_ref[rs, hs].astype(F32))
                o_ref[rs, hs] = (o * gate).astype(BF16)
                new.append(_dot_tn(v_h, kd_p))
            states[p] = dec[:, ls] * states[p] + jnp.where(lane_s, new[0], new[1])
    st_ref[0] = states[0]
    st_ref[1] = states[1]


def _gla(z, f, norm_g, bsz, seq):
    t = z.shape[0]
    tc = min(GLA_ROWS, seq)
    nc = seq // tc
    return pl.pallas_call(
        _gla_kernel,
        grid=(bsz, nc),
        in_specs=[
            pl.BlockSpec((tc, GLA_QK_W), lambda b, i: (b * nc + i, 0)),
            pl.BlockSpec((tc, GLA_QK_W), lambda b, i: (b * nc + i, 1)),
            pl.BlockSpec((tc, GLA_QK_W), lambda b, i: (b * nc + i, 2)),
            pl.BlockSpec((tc, GLA_V_W), lambda b, i: (b * nc + i, 0)),
            pl.BlockSpec((tc, GLA_V_W), lambda b, i: (b * nc + i, 1)),
            pl.BlockSpec((1, GLA_DV), lambda b, i: (0, 0)),
        ],
        out_specs=pl.BlockSpec((tc, GLA_V_W), lambda b, i: (b * nc + i, 0)),
        out_shape=jax.ShapeDtypeStruct((t, GLA_V_W), BF16),
        scratch_shapes=[pltpu.VMEM((2, LANES, LANES), F32)],
        compiler_params=_cp(("arbitrary", "arbitrary"), 32),
        name="gla",
    )(f, f, f, z, z, norm_g.reshape(1, GLA_DV))


def _diff_attn_kernel(lam_init, q_ref, k_ref, v_ref, lq1_ref, lk1_ref, lq2_ref, lk2_ref, g_ref,
                      o_ref, vt_ref, s_ref, m_ref, l_ref, acc_ref):
    tq = q_ref.shape[0]
    seq = k_ref.shape[0]
    n_heads = q_ref.shape[1] // LANES
    qi = pl.program_id(2)

    @pl.when(qi == 0)
    def _():
        for hh in range(n_heads):
            for cb in range(seq // tq):
                blk = v_ref[cb * tq:(cb + 1) * tq, hh * LANES:(hh + 1) * LANES]
                vt_ref[hh, :, cb * tq:(cb + 1) * tq] = blk.astype(F32).T.astype(BF16)

    lo = lax.broadcasted_iota(jnp.int32, (tq, LANES), 1) < DIFF_D
    qqs = []
    for hh in range(n_heads):
        q = q_ref[:, hh * LANES:(hh + 1) * LANES]
        zero = jnp.zeros_like(q)
        qqs.append(jnp.concatenate([jnp.where(lo, q, zero), jnp.where(lo, zero, q)], axis=0))

    m_ref[...] = jnp.full(m_ref.shape, -jnp.inf, F32)
    l_ref[...] = jnp.zeros_like(l_ref)
    acc_ref[...] = jnp.zeros_like(acc_ref)

    def scores(j, slot):
        start = pl.multiple_of(j * tq, tq)
        for hh in range(n_heads):
            kj = k_ref[pl.ds(start, tq), hh * LANES:(hh + 1) * LANES]
            s_ref[slot, hh] = _dot_nt(kj, qqs[hh])

    def update(j, slot, masked):
        start = pl.multiple_of(j * tq, tq)
        for hh in range(n_heads):
            vtj = vt_ref[hh, :, pl.ds(start, tq)]
            alpha, p = [], []
            for c in range(2 * tq // LANES):
                cs = slice(c * LANES, (c + 1) * LANES)
                s_c = s_ref[slot, hh, :, cs]
                if masked:
                    kv = lax.broadcasted_iota(jnp.int32, s_c.shape, 0)
                    q_pos = lax.broadcasted_iota(jnp.int32, s_c.shape, 1) + (c * LANES) % tq
                    s_c = jnp.where(kv <= q_pos, s_c, -jnp.inf)
                m_old = m_ref[hh, :, cs]
                m_c = jnp.maximum(m_old, jnp.max(s_c, axis=0, keepdims=True))
                a_c = jnp.exp2(m_old - m_c)
                p_c = jnp.exp2(s_c - m_c)
                m_ref[hh, :, cs] = m_c
                l_ref[hh, :, cs] = a_c * l_ref[hh, :, cs] + jnp.sum(p_c, axis=0, keepdims=True)
                alpha.append(a_c)
                p.append(p_c.astype(BF16))
            acc_ref[hh] = (jnp.concatenate(alpha, axis=1) * acc_ref[hh]
                           + _dot(vtj, jnp.concatenate(p, axis=1)))

    scores(0, 0)

    def body(i, carry):
        j = 2 * i
        scores(j + 1, 1)
        update(j, 0, False)
        scores(j + 2, 0)
        update(j + 1, 1, False)
        return carry

    n_pairs = qi // 2
    lax.fori_loop(0, n_pairs, body, 0)
    j0 = 2 * n_pairs

    @pl.when(qi % 2 == 1)
    def _():
        scores(j0 + 1, 1)
        update(j0, 0, False)
        update(j0 + 1, 1, True)

    @pl.when(qi % 2 == 0)
    def _():
        update(j0, 0, True)

    lam = (jnp.exp(jnp.sum(lq1_ref[...] * lk1_ref[...], axis=-1, keepdims=True))
           - jnp.exp(jnp.sum(lq2_ref[...] * lk2_ref[...], axis=-1, keepdims=True)) + lam_init)
    for hh in range(n_heads):
        l = l_ref[hh]
        acc = acc_ref[hh]
        o12 = acc / l
        o = o12[:, :tq] - lam * o12[:, tq:]
        o = o * lax.rsqrt(jnp.mean(o * o, axis=0, keepdims=True) + EPS)
        o = o * g_ref[...] * (1.0 - lam_init)
        o_ref[:, hh * DIFF_DV:(hh + 1) * DIFF_DV] = o.T.astype(BF16)


def _diff_attn(z, lq1, lk1, lq2, lk2, norm_g, lam_init, bsz, seq):
    t = z.shape[0]
    tq = min(ATT_Q_ROWS, seq)
    nq = seq // tq
    hw = ATT_HEADS_PER_STEP * LANES
    qb, kb, vb = 1024 // hw, 1536 // hw, 2048 // hw
    small = pl.BlockSpec((1, DIFF_D), lambda b, h, i: (0, 0))
    return pl.pallas_call(
        functools.partial(_diff_attn_kernel, lam_init),
        grid=(bsz, DIFF_HEADS // ATT_HEADS_PER_STEP, nq),
        in_specs=[
            pl.BlockSpec((tq, hw), lambda b, h, i: (b * nq + i, qb + h)),
            pl.BlockSpec((seq, hw), lambda b, h, i: (b, kb + h)),
            pl.BlockSpec((seq, hw), lambda b, h, i: (b, vb + h)),
            small, small, small, small,
            pl.BlockSpec((DIFF_DV, 1), lambda b, h, i: (0, 0)),
        ],
        out_specs=pl.BlockSpec((tq, hw), lambda b, h, i: (b * nq + i, h)),
        out_shape=jax.ShapeDtypeStruct((t, DIFF_V_W), BF16),
        scratch_shapes=[pltpu.VMEM((ATT_HEADS_PER_STEP, DIFF_DV, seq), BF16),
                        pltpu.VMEM((2, ATT_HEADS_PER_STEP, tq, 2 * tq), F32),
                        pltpu.VMEM((ATT_HEADS_PER_STEP, 1, 2 * tq), F32),
                        pltpu.VMEM((ATT_HEADS_PER_STEP, 1, 2 * tq), F32),
                        pltpu.VMEM((ATT_HEADS_PER_STEP, DIFF_DV, 2 * tq), F32)],
        compiler_params=_cp(("arbitrary", "arbitrary", "arbitrary"), 32),
        name="diff_attn",
    )(z, z, z, lq1.reshape(1, DIFF_D), lk1.reshape(1, DIFF_D), lq2.reshape(1, DIFF_D),
      lk2.reshape(1, DIFF_D), norm_g.reshape(DIFF_DV, 1))


def _sgu_ret_kernel(su_ref, sv_ref, rv_ref, rg_ref, q_ref, k_ref, ws_ref, bs_ref, o_ref, st_ref):
    @pl.when(pl.program_id(1) == 0)
    def _():
        st_ref[...] = jnp.zeros_like(st_ref)

    c = RET_CHUNK
    row = lax.broadcasted_iota(jnp.int32, (c, c), 0)
    col = lax.broadcasted_iota(jnp.int32, (c, c), 1)
    causal = row >= col
    log_g = [math.log(1.0 - 2.0 ** (-5.0 - h)) for h in range(RET_HEADS)]
    lo = col < RET_DK
    rel = (row - col).astype(F32)
    pos = row.astype(F32)
    w_sgu = [jnp.where(causal, ws_ref[g], 0.0).astype(BF16) for g in range(SGU_GROUPS)]
    decays = [jnp.where(causal, jnp.exp(log_g[h] * jnp.maximum(rel, 0.0)), 0.0)
              for h in range(RET_HEADS)]
    lgs = [jnp.where(lo, log_g[2 * p], log_g[2 * p + 1]) for p in range(RET_HEADS // 2)]
    q_decs = [jnp.exp(lg * (pos + 1.0)) for lg in lgs]
    k_decs = [jnp.exp(lg * (c - 1.0 - pos)) for lg in lgs]
    states = [st_ref[p] for p in range(RET_HEADS // 2)]
    for j in range(su_ref.shape[0] // c):
        rs = slice(j * c, (j + 1) * c)
        for g in range(SGU_GROUPS):
            gs = slice(g * SGU_CH, (g + 1) * SGU_CH)
            s = _dot(w_sgu[g], sv_ref[rs, gs]) + bs_ref[g]
            o_ref[rs, gs] = (su_ref[rs, gs].astype(F32) * s).astype(BF16)
        for p in range(RET_HEADS // 2):
            ls = slice(p * LANES, (p + 1) * LANES)
            q_p = q_ref[rs, ls]
            k_p = k_ref[rs, ls]
            qm = (jnp.where(lo, q_p, 0.0).astype(BF16), jnp.where(lo, 0.0, q_p).astype(BF16))
            s2 = _dot_nt(jnp.concatenate(qm, axis=0), k_p.astype(BF16))
            qd = q_p * q_decs[p]
            qdm = (jnp.where(lo, qd, 0.0).astype(BF16), jnp.where(lo, 0.0, qd).astype(BF16))
            kd = (k_p * k_decs[p]).astype(BF16)
            st_b = states[p].astype(BF16)
            new = []
            for hh in range(2):
                h = 2 * p + hh
                hs = slice(h * RET_DV, (h + 1) * RET_DV)
                s_h = (s2[hh * c:(hh + 1) * c, :] * decays[h]).astype(BF16)
                v_h = rv_ref[rs, hs]
                o = _dot(s_h, v_h) + _dot_nt(qdm[hh], st_b)
                gate = _silu(rg_ref[rs, hs].astype(F32))
                o_ref[rs, SGU_W + h * RET_DV:SGU_W + (h + 1) * RET_DV] = (
                    _rms_rows(o) * gate).astype(BF16)
                new.append(_dot_tn(v_h, kd))
            states[p] = jnp.exp(lgs[p] * float(c)) * states[p] + jnp.where(lo, new[0], new[1])
    for p in range(RET_HEADS // 2):
        st_ref[p] = states[p]


def _sgu_ret(z, f, w_s, b_s, bsz, seq):
    t = z.shape[0]
    cc = RET_CHUNK
    c = min(SGU_RET_ROWS, seq)
    nc = seq // c
    return pl.pallas_call(
        _sgu_ret_kernel,
        grid=(bsz, nc),
        in_specs=[
            pl.BlockSpec((c, SGU_W), lambda b, i: (b * nc + i, 0)),
            pl.BlockSpec((c, SGU_W), lambda b, i: (b * nc + i, 1)),
            pl.BlockSpec((c, RET_V_W), lambda b, i: (b * nc + i, 2)),
            pl.BlockSpec((c, RET_V_W), lambda b, i: (b * nc + i, 3)),
            pl.BlockSpec((c, RET_QK_W), lambda b, i: (b * nc + i, 0)),
            pl.BlockSpec((c, RET_QK_W), lambda b, i: (b * nc + i, 1)),
            pl.BlockSpec((SGU_GROUPS, cc, cc), lambda b, i: (0, 0, 0)),
            pl.BlockSpec((SGU_GROUPS, cc, 1), lambda b, i: (0, 0, 0)),
        ],
        out_specs=pl.BlockSpec((c, SGU_W + RET_V_W), lambda b, i: (b * nc + i, 0)),
        out_shape=jax.ShapeDtypeStruct((t, SGU_W + RET_V_W), BF16),
        scratch_shapes=[pltpu.VMEM((2, LANES, LANES), F32)],
        compiler_params=_cp(("arbitrary", "arbitrary"), 32),
        name="sgu_retention",
    )(z, z, z, z, f, f, w_s, b_s.reshape(SGU_GROUPS, cc, 1))


def _out_proj_kernel(n_in, *refs):
    o_refs = refs[:n_in]
    w_ref, x_ref, mod_ref, rw_ref, rb_ref, xn_ref, h_ref, lg_ref = refs[n_in:]
    tm = x_ref.shape[0]
    k_each = D_MODEL // n_in
    gate1 = mod_ref[0, 2:3, :]
    sub = min(tm, OUT_PROJ_SUB_ROWS)
    for r0 in range(0, tm, sub):
        rs = slice(r0, r0 + sub)
        y = _dot(o_refs[0][rs, :], w_ref[0:k_each, :])
        for n in range(1, n_in):
            y = y + _dot(o_refs[n][rs, :], w_ref[n * k_each:(n + 1) * k_each, :])
        xn = x_ref[rs, :] + gate1 * y
        xn_ref[rs, :] = xn
        h = _rms_rows(xn) * (1.0 + mod_ref[0, 4:5, :]) + mod_ref[0, 3:4, :]
        for c in range(ROW_TILES):
            h_ref[pl.ds(r0 * ROW_TILES + c, sub, stride=ROW_TILES), :] = (
                h[:, c * LANES:(c + 1) * LANES])
        lg_ref[:, rs] = _dot_nt(rw_ref[...], h.astype(BF16)) + rb_ref[...]


def _out_proj(mixed, w_out, x2, mod, router_w, router_b, seq):
    t, d = x2.shape
    tm = min(PROJ_ROWS, seq)
    n_in = len(mixed)
    k_each = d // n_in
    rw = router_w.T.astype(BF16)
    rb = router_b.reshape(N_EXPERTS, 1)
    return pl.pallas_call(
        functools.partial(_out_proj_kernel, n_in),
        grid=(t // tm,),
        in_specs=[pl.BlockSpec((tm, k_each), lambda i: (i, 0)) for _ in mixed] + [
            pl.BlockSpec((d, d), lambda i: (0, 0)),
            pl.BlockSpec((tm, d), lambda i: (i, 0)),
            pl.BlockSpec((1, 6, d), lambda i: (i // (seq // tm), 0, 0)),
            pl.BlockSpec((N_EXPERTS, d), lambda i: (0, 0)),
            pl.BlockSpec((N_EXPERTS, 1), lambda i: (0, 0)),
        ],
        out_specs=[pl.BlockSpec((tm, d), lambda i: (i, 0)),
                   pl.BlockSpec((tm * ROW_TILES, LANES), lambda i: (i, 0)),
                   pl.BlockSpec((N_EXPERTS, tm), lambda i: (0, i))],
        out_shape=[jax.ShapeDtypeStruct((t, d), F32),
                   jax.ShapeDtypeStruct((t * ROW_TILES, LANES), F32),
                   jax.ShapeDtypeStruct((N_EXPERTS, t), F32)],
        compiler_params=_cp(("arbitrary",), 48),
        name="out_proj",
    )(*mixed, w_out.astype(BF16), x2, mod, rw, rb)


def _route_kernel(lg_ref, idx_ref, gate_ref, cnt_ref, run_ref):
    @pl.when(pl.program_id(0) == 0)
    def _():
        run_ref[...] = jnp.zeros_like(run_ref)

    tm = lg_ref.shape[1]
    row = lax.broadcasted_iota(jnp.int32, (N_EXPERTS, tm), 0)
    neg = -jnp.inf
    l = lg_ref[...]
    vals, firsts, hots = [], [], []
    for _ in range(TOP_K):
        m = jnp.max(l, axis=0, keepdims=True)
        first = jnp.min(jnp.where(l == m, row, N_EXPERTS), axis=0, keepdims=True)
        hot = row == first
        vals.append(m)
        firsts.append(first)
        hots.append(hot)
        l = jnp.where(hot, neg, l)
    sel = hots[0] | hots[1] | hots[2] | hots[3]
    ex = [jnp.exp(v - vals[0]) for v in vals]
    denom = ex[0] + ex[1] + ex[2] + ex[3]
    r_i = lax.broadcasted_iota(jnp.int32, (tm, tm), 0)
    c_i = lax.broadcasted_iota(jnp.int32, (tm, tm), 1)
    before = (r_i < c_i).astype(BF16)
    run = run_ref[:, 0:1]
    cum = _dot(sel.astype(BF16), before) + run
    row8 = lax.broadcasted_iota(jnp.int32, (2 * TOP_K, tm), 0)
    idx_out = jnp.zeros((2 * TOP_K, tm), jnp.int32)
    gate_out = jnp.zeros((2 * TOP_K, tm), F32)
    for k in range(TOP_K):
        rank_k = jnp.sum(jnp.where(hots[k], cum, 0.0), axis=0, keepdims=True).astype(jnp.int32)
        idx_out = jnp.where(row8 == k, firsts[k], idx_out)
        idx_out = jnp.where(row8 == TOP_K + k, rank_k, idx_out)
        gate_out = jnp.where(row8 == k, ex[k] / denom, gate_out)
    idx_ref[...] = idx_out
    gate_ref[...] = gate_out
    total = run + jnp.sum(sel.astype(F32), axis=1, keepdims=True)
    run_ref[...] = jnp.broadcast_to(total, run_ref.shape)
    cnt_ref[...] = jnp.broadcast_to(total, cnt_ref.shape)


def _route(logits_t):
    t = logits_t.shape[1]
    tm = min(ROUTE_ROWS, t)
    return pl.pallas_call(
        _route_kernel,
        grid=(t // tm,),
        in_specs=[pl.BlockSpec((N_EXPERTS, tm), lambda i: (0, i))],
        out_specs=[pl.BlockSpec((2 * TOP_K, tm), lambda i: (0, i)),
                   pl.BlockSpec((2 * TOP_K, tm), lambda i: (0, i)),
                   pl.BlockSpec((N_EXPERTS, LANES), lambda i: (0, 0))],
        out_shape=[jax.ShapeDtypeStruct((2 * TOP_K, t), jnp.int32),
                   jax.ShapeDtypeStruct((2 * TOP_K, t), F32),
                   jax.ShapeDtypeStruct((N_EXPERTS, LANES), F32)],
        scratch_shapes=[pltpu.VMEM((N_EXPERTS, LANES), F32)],
        compiler_params=_cp(("arbitrary",), 32),
        name="route",
    )(logits_t)


def _dispatch_kernel(fs_ref, fl_ref, h_ref, dest_ref, xs_ref, zero_buf, sem, zsem):
    tm = h_ref.shape[0] // ROW_TILES
    n_fill = fs_ref.shape[0]

    def rows(ref, start, n):
        return ref.at[pl.ds(pl.multiple_of(start * ROW_TILES, ROW_TILES), n * ROW_TILES), :]

    def fill_copies(f, wait):
        start, n = fs_ref[f], fl_ref[f]
        n_chunks = lax.shift_right_logical(n, FILL_SHIFT)
        tail = start + n_chunks * FILL_ROWS

        def chunk(j, carry):
            cp = pltpu.make_async_copy(zero_buf, rows(xs_ref, start + j * FILL_ROWS, FILL_ROWS), zsem)
            cp.wait() if wait else cp.start()
            return carry

        def single(j, carry):
            cp = pltpu.make_async_copy(rows(zero_buf, 0, 1), rows(xs_ref, tail + j, 1), zsem)
            cp.wait() if wait else cp.start()
            return carry

        lax.fori_loop(0, n_chunks, chunk, 0)
        lax.fori_loop(0, n - n_chunks * FILL_ROWS, single, 0)

    @pl.when(pl.program_id(0) == 0)
    def _():
        zero_buf[...] = jnp.zeros_like(zero_buf)
        lax.fori_loop(0, n_fill, lambda f, c: (fill_copies(f, False), c)[1], 0)

    def issue(rb, carry):
        for rr in range(ISSUE_UNROLL):
            r = rb * ISSUE_UNROLL + rr
            for k in range(TOP_K):
                pltpu.make_async_copy(rows(h_ref, r, 1), rows(xs_ref, dest_ref[r * TOP_K + k], 1),
                                      sem).start(priority=k % 2)
        return carry

    lax.fori_loop(0, tm // ISSUE_UNROLL, issue, 0)
    for _ in range(TOP_K):
        pltpu.make_async_copy(h_ref, rows(xs_ref, 0, tm), sem).wait()

    @pl.when(pl.program_id(0) == 0)
    def _():
        lax.fori_loop(0, n_fill, lambda f, c: (fill_copies(f, True), c)[1], 0)


def _dispatch(h3, dest_flat, fill_start, fill_len, n_rows):
    t = h3.shape[0] // ROW_TILES
    tm = min(DISPATCH_ROWS, t)
    grid_spec = pltpu.PrefetchScalarGridSpec(
        num_scalar_prefetch=2,
        grid=(t // tm,),
        in_specs=[pl.BlockSpec((tm * ROW_TILES, LANES), lambda i, fs, fl: (i, 0)),
                  pl.BlockSpec((tm * TOP_K,), lambda i, fs, fl: (i,), memory_space=pltpu.SMEM)],
        out_specs=pl.BlockSpec(memory_space=pl.ANY),
        scratch_shapes=[pltpu.VMEM((FILL_ROWS * ROW_TILES, LANES), F32),
                        pltpu.SemaphoreType.DMA(()), pltpu.SemaphoreType.DMA(())],
    )
    return pl.pallas_call(
        _dispatch_kernel,
        grid_spec=grid_spec,
        out_shape=jax.ShapeDtypeStruct((n_rows * ROW_TILES, LANES), F32),
        compiler_params=_cp(("arbitrary",), 32),
        name="dispatch",
    )(fill_start, fill_len, h3, dest_flat)


def _expert_kernel(layer, be_ref, nu_ref, nx_ref, nv_ref, xs_ref, wi_hbm, bi_ref, wo_hbm, bo_ref, y_ref,
                   wi_st, wo_st, wi_b, wo_b, sems):
    i = pl.program_id(0)
    tb = xs_ref.shape[0] // ROW_TILES
    e = be_ref[i]
    fresh = jnp.logical_or(i == 0, e != be_ref[jnp.maximum(i - 1, 0)])
    used = i < nu_ref[0]

    def fetch(ex):
        return (pltpu.make_async_copy(wi_hbm.at[layer, ex], wi_st, sems.at[0]),
                pltpu.make_async_copy(wo_hbm.at[layer, ex], wo_st, sems.at[1]))

    @pl.when(i == 0)
    def _():
        for cp in fetch(e):
            cp.start()

    @pl.when(jnp.logical_and(fresh, used))
    def _():
        for cp in fetch(e):
            cp.wait()
        wi_b[...] = wi_st[...].astype(BF16)
        wo_b[...] = wo_st[...].astype(BF16)

        @pl.when(nx_ref[i] >= 0)
        def _():
            for cp in fetch(nx_ref[i]):
                cp.start()

    def compute(n):
        x = jnp.concatenate(
            [xs_ref[pl.ds(c, n, stride=ROW_TILES), :] for c in range(ROW_TILES)],
            axis=1).astype(BF16)
        y = jnp.zeros((n, D_MODEL), F32) + bo_ref[0, 0]
        half = 512
        for j in range(D_FF // half):
            a, b = j * half, (j + 1) * half
            glu = _dot(x, wi_b[:, a:b]) + bi_ref[0, 0, :, a:b]
            lin = _dot(x, wi_b[:, D_FF + a:D_FF + b]) + bi_ref[0, 0, :, D_FF + a:D_FF + b]
            glu = jnp.minimum(glu, SWIGLU_LIMIT)
            lin = jnp.clip(lin, -SWIGLU_LIMIT, SWIGLU_LIMIT)
            act = glu * jax.nn.sigmoid(SWIGLU_ALPHA * glu) * (lin + 1.0)
            y = y + _dot(act.astype(BF16), wo_b[a:b, :])
        for c in range(ROW_TILES):
            y_ref[pl.ds(c, n, stride=ROW_TILES), :] = y[:, c * LANES:(c + 1) * LANES]

    full = nv_ref[i] > tb // 2

    @pl.when(jnp.logical_and(used, full))
    def _():
        compute(tb)

    @pl.when(jnp.logical_and(used, jnp.logical_not(full)))
    def _():
        compute(tb // 2)
        y_ref[pl.ds(tb // 2 * ROW_TILES, tb // 2 * ROW_TILES), :] = jnp.zeros(
            (tb // 2 * ROW_TILES, LANES), F32)

    @pl.when(jnp.logical_not(used))
    def _():
        y_ref[...] = jnp.zeros_like(y_ref)


def _experts(xs, block_e, n_used, next_e, n_valid, layer, w_in, b_in, w_out, b_out):
    tb = EXPERT_ROWS
    n_rows = xs.shape[0] // ROW_TILES
    nb = n_rows // tb
    depth, ne, d, f2 = w_in.shape

    def row_map(i, be, nu, nx, nv):
        return (jnp.minimum(i, nu[0] - 1), 0)

    grid_spec = pltpu.PrefetchScalarGridSpec(
        num_scalar_prefetch=4,
        grid=(nb,),
        in_specs=[
            pl.BlockSpec((tb * ROW_TILES, LANES), row_map),
            pl.BlockSpec(memory_space=pl.ANY),
            pl.BlockSpec((1, 1, 1, f2), lambda i, be, nu, nx, nv: (layer, be[i], 0, 0)),
            pl.BlockSpec(memory_space=pl.ANY),
            pl.BlockSpec((1, 1, 1, d), lambda i, be, nu, nx, nv: (layer, be[i], 0, 0)),
        ],
        out_specs=pl.BlockSpec((tb * ROW_TILES, LANES), lambda i, be, nu, nx, nv: (i, 0)),
        scratch_shapes=[pltpu.VMEM((d, f2), F32), pltpu.VMEM((D_FF, d), F32),
                        pltpu.VMEM((d, f2), BF16), pltpu.VMEM((D_FF, d), BF16),
                        pltpu.SemaphoreType.DMA((2,))],
    )
    return pl.pallas_call(
        functools.partial(_expert_kernel, layer),
        grid_spec=grid_spec,
        out_shape=jax.ShapeDtypeStruct((n_rows * ROW_TILES, LANES), F32),
        compiler_params=_cp(("arbitrary",), 56),
        name="experts",
    )(block_e, n_used, next_e, n_valid, xs, w_in, b_in.reshape(depth, ne, 1, f2), w_out,
      b_out.reshape(depth, ne, 1, d))


def _combine_kernel(final, dest_ref, dnext_ref, gate_ref, x_ref, mod_ref, fg_ref, yb_ref, o_ref,
                    g_buf, sems):
    tm = x_ref.shape[0]
    i = pl.program_id(0)
    cur = lax.rem(i, 2)

    def issue_all(d_ref, buf):
        def issue(rb, carry):
            for rr in range(ISSUE_UNROLL):
                r = rb * ISSUE_UNROLL + rr
                for k in range(TOP_K):
                    d = d_ref[r * TOP_K + k]
                    pltpu.make_async_copy(
                        yb_ref.at[pl.ds(pl.multiple_of(d * ROW_TILES, ROW_TILES), ROW_TILES), :],
                        g_buf.at[buf, pl.ds(pl.multiple_of((k * tm + r) * ROW_TILES, ROW_TILES),
                                            ROW_TILES), :],
                        sems.at[buf]).start(priority=k % 2)
            return carry

        lax.fori_loop(0, tm // ISSUE_UNROLL, issue, 0)

    @pl.when(i == 0)
    def _():
        issue_all(dest_ref, 0)

    @pl.when(i + 1 < pl.num_programs(0))
    def _():
        issue_all(dnext_ref, 1 - cur)

    pltpu.make_async_copy(yb_ref.at[pl.ds(0, TOP_K * tm * ROW_TILES), :], g_buf.at[cur],
                          sems.at[cur]).wait()

    gates = gate_ref[...]
    gate2 = mod_ref[0, 5:6, :]
    for c in range(ROW_TILES):
        y = jnp.zeros((tm, LANES), F32)
        for k in range(TOP_K):
            rows = g_buf[cur, pl.ds(k * tm * ROW_TILES + c, tm, stride=ROW_TILES), :]
            y = y + gates[:, k:k + 1] * rows
        cs = slice(c * LANES, (c + 1) * LANES)
        o_ref[:, cs] = x_ref[:, cs] + gate2[:, cs] * y
    if final:
        o_ref[...] = _rms_rows(o_ref[...]) * fg_ref[...]


def _combine(yb, dest_flat, gates, x2, mod, final_g, final, seq):
    t, d = x2.shape
    tm = min(COMBINE_ROWS, seq)
    last = t // tm - 1
    return pl.pallas_call(
        functools.partial(_combine_kernel, final),
        grid=(t // tm,),
        in_specs=[
            pl.BlockSpec((tm * TOP_K,), lambda i: (i,), memory_space=pltpu.SMEM),
            pl.BlockSpec((tm * TOP_K,), lambda i: (jnp.minimum(i + 1, last),),
                         memory_space=pltpu.SMEM),
            pl.BlockSpec((tm, 2 * TOP_K), lambda i: (i, 0)),
            pl.BlockSpec((tm, d), lambda i: (i, 0)),
            pl.BlockSpec((1, 6, d), lambda i: (i // (seq // tm), 0, 0)),
            pl.BlockSpec((1, d), lambda i: (0, 0)),
            pl.BlockSpec(memory_space=pl.ANY),
        ],
        out_specs=pl.BlockSpec((tm, d), lambda i: (i, 0)),
        out_shape=jax.ShapeDtypeStruct((t, d), F32),
        scratch_shapes=[pltpu.VMEM((2, TOP_K * tm * ROW_TILES, LANES), F32),
                        pltpu.SemaphoreType.DMA((2,))],
        compiler_params=_cp(("arbitrary",), 40),
        name="combine",
    )(dest_flat, dest_flat, gates, x2, mod, final_g.reshape(1, d), yb)


def _moe(h3, logits, x2, mod, layer, w_in, b_in, w_out, b_out, final_g, final, seq):
    t = x2.shape[0]
    tb = EXPERT_ROWS
    idx, gates_t, cnt = _route(logits)
    counts = cnt[:, 0].astype(jnp.int32)
    nblk = (counts + tb - 1) // tb
    blk_end = jnp.cumsum(nblk)
    pad_start = (blk_end - nblk) * tb
    e_hot = idx[:TOP_K, None, :] == jnp.arange(N_EXPERTS, dtype=jnp.int32)[None, :, None]
    dest = jnp.sum(jnp.where(e_hot, pad_start[None, :, None], 0), axis=1) + idx[TOP_K:]
    dest_flat = dest.T.reshape(t * TOP_K)
    gates = gates_t.T
    n_blocks = (t * TOP_K) // tb + N_EXPERTS
    n_used = blk_end[-1:]
    last_e = jnp.max(jnp.where(nblk > 0, jnp.arange(N_EXPERTS, dtype=jnp.int32), 0))
    blk = jnp.arange(n_blocks, dtype=jnp.int32)
    block_e = jnp.minimum(
        jnp.sum((blk_end[None, :] <= blk[:, None]).astype(jnp.int32), axis=1), last_e)
    experts = jnp.arange(N_EXPERTS, dtype=jnp.int32)
    later = (experts[None, :] > block_e[:, None]) & (nblk[None, :] > 0)
    next_e = jnp.min(jnp.where(later, experts[None, :], N_EXPERTS), axis=1)
    next_e = jnp.where(next_e == N_EXPERTS, -1, next_e).astype(jnp.int32)
    fill_start = jnp.concatenate([pad_start + counts, blk_end[-1:] * tb]).astype(jnp.int32)
    fill_len = jnp.concatenate([nblk * tb - counts, (n_blocks - blk_end[-1:]) * tb]).astype(jnp.int32)
    xs = _dispatch(h3, dest_flat, fill_start, fill_len, n_blocks * tb)
    n_valid = jnp.clip(counts[block_e] - (blk * tb - pad_start[block_e]), 0, tb).astype(jnp.int32)
    yb = _experts(xs, block_e, n_used.astype(jnp.int32), next_e, n_valid, layer, w_in, b_in, w_out, b_out)
    return _combine(yb, dest_flat, gates, x2, mod, final_g, final, seq)


def kernel(x, c, positions, w_ada, b_ada, even_w_in, gla_w_gate, gla_b_gate, gla_norm_g,
           diff_lam_q1, diff_lam_k1, diff_lam_q2, diff_lam_k2, diff_norm_g, even_w_out,
           odd_w_in, sgu_ln_g, sgu_ln_b, sgu_w, sgu_b, odd_w_out,
           router_w, router_b, expert_w_in, expert_b_in, expert_w_out, expert_b_out,
           final_norm_g):
    bsz, seq, d = x.shape
    depth = w_ada.shape[0]
    t = bsz * seq
    mods = _modulation(c, w_ada, b_ada).reshape(depth, bsz, 6, d)
    cos, sin = _rope_tables(positions)
    x2 = x.reshape(t, d)
    for layer in range(depth):
        mod = mods[layer]
        j = layer // 2
        if layer % 2 == 0:
            z, f = _proj_even(x2, mod, even_w_in[j], gla_w_gate[j], gla_b_gate[j], cos, sin, seq)
            o_gla = _gla(z, f, gla_norm_g[j], bsz, seq)
            lam_init = 0.8 - 0.6 * math.exp(-0.3 * layer)
            o_diff = _diff_attn(z, diff_lam_q1[j], diff_lam_k1[j], diff_lam_q2[j], diff_lam_k2[j],
                                diff_norm_g[j], lam_init, bsz, seq)
            mixed, w_out = (o_gla, o_diff), even_w_out[j]
        else:
            z, f = _proj_odd(x2, mod, odd_w_in[j], sgu_ln_g[j], sgu_ln_b[j], cos, sin, seq)
            mixed, w_out = (_sgu_ret(z, f, sgu_w[j], sgu_b[j], bsz, seq),), odd_w_out[j]
        x2, h3, logits = _out_proj(mixed, w_out, x2, mod, router_w[layer], router_b[layer], seq)
        x2 = _moe(h3, logits, x2, mod, layer, expert_w_in, expert_b_in, expert_w_out,
                  expert_b_out, final_norm_g, layer == depth - 1, seq)
    return x2.reshape(bsz, seq, d)
```

```python
import functools
import math

import jax
import jax.numpy as jnp
from jax import lax
from jax.experimental import pallas as pl
from jax.experimental.pallas import tpu as pltpu

F32 = jnp.float32
BF16 = jnp.bfloat16

D_MODEL = 1024
EPS = 1e-6
ROPE_THETA = 10000.0
ROPE_HALF = 32

GLA_HEADS = 4
GLA_DK = 64
GLA_DV = 128
GLA_RANK = 16
GLA_CHUNK = 64
GLA_GATE_NORMALIZER = 16.0
GLA_QK_W = GLA_HEADS * GLA_DK
GLA_V_W = GLA_HEADS * GLA_DV

DIFF_HEADS = 4
DIFF_D = 64
DIFF_DV = 128
DIFF_QK_W = DIFF_HEADS * 2 * DIFF_D
DIFF_V_W = DIFF_HEADS * DIFF_DV

SGU_GROUPS = 4
SGU_CH = 128
SGU_CHUNK = 128
SGU_W = SGU_GROUPS * SGU_CH

RET_HEADS = 4
RET_DK = 64
RET_DV = 128
RET_CHUNK = 128
RET_QK_W = RET_HEADS * RET_DK
RET_V_W = RET_HEADS * RET_DV

N_EXPERTS = 32
TOP_K = 4
D_FF = D_MODEL
SWIGLU_ALPHA = 1.702
SWIGLU_LIMIT = 7.0

LANES = 128
SUBLANES = 8
ROW_TILES = D_MODEL // LANES

PROJ_ROWS = 512
OUT_PROJ_SUB_ROWS = 512
GLA_ROWS = 512
GLA_GROUP = 4
SGU_RET_ROWS = 512
ATT_Q_ROWS = 256
ATT_HEADS_PER_STEP = 2
ROUTE_ROWS = 512
DISPATCH_ROWS = 512
ISSUE_UNROLL = 32
FILL_SHIFT = 6
FILL_ROWS = 1 << FILL_SHIFT
EXPERT_ROWS = 512
COMBINE_ROWS = 256
MOD_COLS = 1536

MIB = 1024 * 1024


def _cp(semantics, vmem_mib):
    return pltpu.CompilerParams(dimension_semantics=semantics, vmem_limit_bytes=vmem_mib * MIB)


def _dot(a, b):
    return jnp.dot(a, b, preferred_element_type=F32)


def _dot_nt(a, b):
    return lax.dot_general(a, b, (((1,), (1,)), ((), ())), preferred_element_type=F32)


def _dot_tn(a, b):
    return lax.dot_general(a, b, (((0,), (0,)), ((), ())), preferred_element_type=F32)


def _rms_rows(x):
    return x * lax.rsqrt(jnp.mean(x * x, axis=-1, keepdims=True) + EPS)


def _silu(x):
    return x * jax.nn.sigmoid(x)


def _rope_chunk(v, cos, sin, lo_half):
    rot = jnp.where(lo_half, -pltpu.roll(v, 96, 1), pltpu.roll(v, 32, 1))
    return v * cos + rot * sin


def _mod_kernel(c_ref, w_ref, b_ref, o_ref):
    c = c_ref[...]
    ca = _silu(c).astype(BF16)
    o_ref[0] = _dot(ca, w_ref[0].astype(BF16)) + b_ref[0]


def _modulation(c, w_ada, b_ada):
    depth, d, n = w_ada.shape
    bsz = c.shape[0]
    return pl.pallas_call(
        _mod_kernel,
        grid=(depth, n // MOD_COLS),
        in_specs=[
            pl.BlockSpec((bsz, d), lambda l, j: (0, 0)),
            pl.BlockSpec((1, d, MOD_COLS), lambda l, j: (l, 0, j)),
            pl.BlockSpec((1, 1, MOD_COLS), lambda l, j: (l, 0, j)),
        ],
        out_specs=pl.BlockSpec((1, bsz, MOD_COLS), lambda l, j: (l, 0, j)),
        out_shape=jax.ShapeDtypeStruct((depth, bsz, n), F32),
        compiler_params=_cp(("arbitrary", "arbitrary"), 40),
        name="adaln_mod",
    )(c, w_ada, b_ada.reshape(depth, 1, n))


def _rope_table_kernel(p_ref, f_ref, c_ref, s_ref):
    ang = p_ref[...].astype(F32) * f_ref[...]
    c_ref[...] = jnp.cos(ang)
    s_ref[...] = jnp.sin(ang)


def _rope_tables(positions):
    t = positions.size
    per_row = LANES // ROPE_HALF
    rows = t // per_row
    pos_d = jnp.repeat(positions.reshape(rows, per_row), ROPE_HALF, axis=1)
    inv_freq = ROPE_THETA ** (-jnp.arange(ROPE_HALF, dtype=F32) / ROPE_HALF)
    freq_d = jnp.tile(inv_freq, per_row).reshape(1, LANES)
    tr = min(512, rows)
    cos_d, sin_d = pl.pallas_call(
        _rope_table_kernel,
        grid=(rows // tr,),
        in_specs=[pl.BlockSpec((tr, LANES), lambda i: (i, 0)),
                  pl.BlockSpec((1, LANES), lambda i: (0, 0))],
        out_specs=[pl.BlockSpec((tr, LANES), lambda i: (i, 0))] * 2,
        out_shape=[jax.ShapeDtypeStruct((rows, LANES), F32)] * 2,
        compiler_params=_cp(("arbitrary",), 32),
        name="rope_tables",
    )(pos_d, freq_d)
    cos = jnp.tile(cos_d.reshape(t, ROPE_HALF), (1, per_row))
    sin = jnp.tile(sin_d.reshape(t, ROPE_HALF), (1, per_row))
    return cos, sin


def _modulated_rms(x_ref, mod_ref, which):
    x = x_ref[...]
    shift = mod_ref[0, 3 * which:3 * which + 1, :]
    scale = mod_ref[0, 3 * which + 1:3 * which + 2, :]
    return _rms_rows(x) * (1.0 + scale) + shift


EVEN_Z = (GLA_V_W, GLA_V_W, DIFF_QK_W, DIFF_QK_W, DIFF_V_W)
EVEN_Z_W = sum(EVEN_Z)
EVEN_F_W = 3 * GLA_QK_W
EVEN_W_COLS = EVEN_Z_W + 2 * GLA_QK_W + LANES


def _proj_even_kernel(x_ref, mod_ref, w_ref, wg_ref, bg_ref, cos_ref, sin_ref, z_ref, f_ref):
    hb = _modulated_rms(x_ref, mod_ref, 0).astype(BF16)

    def sec(a, b):
        return _dot(hb, w_ref[:, a:b])

    for a, b in ((0, 512), (512, 1024), (2048, 2560)):
        z_ref[:, a:b] = sec(a, b).astype(BF16)
    cos = cos_ref[...]
    sin = sin_ref[...]
    lo_half = (lax.broadcasted_iota(jnp.int32, cos.shape, 1) % 64) < ROPE_HALF
    q_scale = DIFF_D ** -0.5 * math.log2(math.e)
    for a, scl in ((1024, q_scale), (1536, 1.0)):
        full = sec(a, a + DIFF_QK_W)
        for c in range(DIFF_QK_W // LANES):
            v = full[:, c * LANES:(c + 1) * LANES]
            z_ref[:, a + c * LANES:a + (c + 1) * LANES] = (
                _rope_chunk(v, cos, sin, lo_half) * scl).astype(BF16)
    f_ref[:, 0:256] = sec(2560, 2816) * (GLA_DK ** -0.5)
    f_ref[:, 256:512] = sec(2816, 3072)
    gr = sec(3072, 3200).astype(BF16)
    pre = _dot(gr, wg_ref[...]) + bg_ref[...]
    log_sig = jnp.minimum(pre, 0.0) - jnp.log1p(jnp.exp(-jnp.abs(pre)))
    f_ref[:, 512:768] = log_sig / GLA_GATE_NORMALIZER


def _proj_even(x2, mod, w_in, w_gate, b_gate, cos, sin, seq):
    t, d = x2.shape
    tm = min(PROJ_ROWS, seq)
    gq, gk, gv, gr, gg, dq, dk, dv = jnp.split(
        w_in, [256, 512, 1024, 1040, 1552, 2064, 2576], axis=1)
    gr_pad = jnp.pad(gr, ((0, 0), (0, LANES - GLA_RANK)))
    w_cat = jnp.concatenate([gv, gg, dq, dk, dv, gq, gk, gr_pad], axis=1).astype(BF16)
    wg_pad = jnp.pad(w_gate, ((0, LANES - GLA_RANK), (0, 0))).astype(BF16)
    return pl.pallas_call(
        _proj_even_kernel,
        grid=(t // tm,),
        in_specs=[
            pl.BlockSpec((tm, d), lambda i: (i, 0)),
            pl.BlockSpec((1, 6, d), lambda i: (i // (seq // tm), 0, 0)),
            pl.BlockSpec((d, EVEN_W_COLS), lambda i: (0, 0)),
            pl.BlockSpec((LANES, GLA_QK_W), lambda i: (0, 0)),
            pl.BlockSpec((1, GLA_QK_W), lambda i: (0, 0)),
            pl.BlockSpec((tm, LANES), lambda i: (i, 0)),
            pl.BlockSpec((tm, LANES), lambda i: (i, 0)),
        ],
        out_specs=[pl.BlockSpec((tm, EVEN_Z_W), lambda i: (i, 0)),
                   pl.BlockSpec((tm, EVEN_F_W), lambda i: (i, 0))],
        out_shape=[jax.ShapeDtypeStruct((t, EVEN_Z_W), BF16),
                   jax.ShapeDtypeStruct((t, EVEN_F_W), F32)],
        compiler_params=_cp(("arbitrary",), 48),
        name="proj_even",
    )(x2, mod, w_cat, wg_pad, b_gate.reshape(1, GLA_QK_W), cos, sin)


ODD_Z_W = 2 * SGU_W + 2 * RET_V_W
ODD_F_W = 2 * RET_QK_W
ODD_W_COLS = ODD_Z_W + ODD_F_W


def _gelu_exact(x):
    return 0.5 * x * (1.0 + lax.erf(x * (2.0 ** -0.5)))


def _proj_odd_kernel(x_ref, mod_ref, w_ref, lng_ref, lnb_ref, cos_ref, sin_ref, z_ref, f_ref):
    hb = _modulated_rms(x_ref, mod_ref, 0).astype(BF16)

    def sec(a, b):
        return _dot(hb, w_ref[:, a:b])

    z_ref[:, 0:512] = _gelu_exact(sec(0, 512)).astype(BF16)
    sv = _gelu_exact(sec(512, 1024))
    mu = jnp.mean(sv, axis=-1, keepdims=True)
    cen = sv - mu
    var = jnp.mean(cen * cen, axis=-1, keepdims=True)
    z_ref[:, 512:1024] = (cen * lax.rsqrt(var + EPS) * lng_ref[...] + lnb_ref[...]).astype(BF16)
    for a, b in ((1024, 1536), (1536, 2048)):
        z_ref[:, a:b] = sec(a, b).astype(BF16)
    cos = cos_ref[...]
    sin = sin_ref[...]
    lo_half = (lax.broadcasted_iota(jnp.int32, cos.shape, 1) % 64) < ROPE_HALF
    k_scale = RET_DK ** -0.5
    for a, scl in ((0, 1.0), (256, k_scale)):
        full = sec(ODD_Z_W + a, ODD_Z_W + a + RET_QK_W)
        for c in range(RET_QK_W // LANES):
            v = full[:, c * LANES:(c + 1) * LANES]
            f_ref[:, a + c * LANES:a + (c + 1) * LANES] = _rope_chunk(v, cos, sin, lo_half) * scl


def _proj_odd(x2, mod, w_in, ln_g, ln_b, cos, sin, seq):
    t, d = x2.shape
    tm = min(PROJ_ROWS, seq)
    su, sv, rq, rk, rv, rg = jnp.split(w_in, [512, 1024, 1280, 1536, 2048], axis=1)
    w_cat = jnp.concatenate([su, sv, rv, rg, rq, rk], axis=1).astype(BF16)
    return pl.pallas_call(
        _proj_odd_kernel,
        grid=(t // tm,),
        in_specs=[
            pl.BlockSpec((tm, d), lambda i: (i, 0)),
            pl.BlockSpec((1, 6, d), lambda i: (i // (seq // tm), 0, 0)),
            pl.BlockSpec((d, ODD_W_COLS), lambda i: (0, 0)),
            pl.BlockSpec((1, SGU_W), lambda i: (0, 0)),
            pl.BlockSpec((1, SGU_W), lambda i: (0, 0)),
            pl.BlockSpec((tm, LANES), lambda i: (i, 0)),
            pl.BlockSpec((tm, LANES), lambda i: (i, 0)),
        ],
        out_specs=[pl.BlockSpec((tm, ODD_Z_W), lambda i: (i, 0)),
                   pl.BlockSpec((tm, ODD_F_W), lambda i: (i, 0))],
        out_shape=[jax.ShapeDtypeStruct((t, ODD_Z_W), BF16),
                   jax.ShapeDtypeStruct((t, ODD_F_W), F32)],
        compiler_params=_cp(("arbitrary",), 48),
        name="proj_odd",
    )(x2, mod, w_cat, ln_g.reshape(1, SGU_W), ln_b.reshape(1, SGU_W), cos, sin)


def _gla_kernel(q_ref, k_ref, la_ref, v_ref, gg_ref, g_ref, o_ref, st_ref):
    @pl.when(pl.program_id(1) == 0)
    def _():
        st_ref[...] = jnp.zeros_like(st_ref)

    c = GLA_CHUNK
    sc = GLA_GROUP * c
    rows = q_ref.shape[0]
    r_i = lax.broadcasted_iota(jnp.int32, (sc, sc), 0)
    c_i = lax.broadcasted_iota(jnp.int32, (sc, sc), 1)
    block_causal = (r_i // c == c_i // c) & (r_i >= c_i)
    tril = block_causal.astype(BF16)
    causal2 = jnp.concatenate([block_causal, block_causal], axis=0)
    lo = lax.broadcasted_iota(jnp.int32, (sc, LANES), 1) < GLA_DK
    lane_s = lax.broadcasted_iota(jnp.int32, (LANES, LANES), 1) < GLA_DK
    g_row = g_ref[...]
    states = [st_ref[0], st_ref[1]]
    for j in range(rows // sc):
        rs = slice(j * sc, (j + 1) * sc)
        la = la_ref[rs, :]
        la_hi = la.astype(BF16)
        la_lo = (la - la_hi.astype(F32)).astype(BF16)
        b = _dot(tril, la_hi) + _dot(tril, la_lo)
        b_last = [b[(g + 1) * c - 1:(g + 1) * c, :] for g in range(GLA_GROUP)]
        b_last_rows = jnp.concatenate(
            [jnp.broadcast_to(bl, (c, bl.shape[1])) for bl in b_last], axis=0)
        qd = q_ref[rs, :] * jnp.exp(b)
        kk = k_ref[rs, :]
        ki = kk * jnp.exp(-b)
        kd = kk * jnp.exp(b_last_rows - b)
        for p in range(GLA_HEADS // 2):
            ls = slice(p * LANES, (p + 1) * LANES)
            qd_p = qd[:, ls]
            qm = (jnp.where(lo, qd_p, 0.0).astype(BF16), jnp.where(lo, 0.0, qd_p).astype(BF16))
            s2 = _dot_nt(jnp.concatenate(qm, axis=0), ki[:, ls].astype(BF16))
            s2 = jnp.where(causal2, s2, 0.0).astype(BF16)
            kd_p = kd[:, ls].astype(BF16)
            heads = (2 * p, 2 * p + 1)
            v = [v_ref[rs, h * GLA_DV:(h + 1) * GLA_DV] for h in heads]
            inter = ([], [])
            for g in range(GLA_GROUP):
                gs = slice(g * c, (g + 1) * c)
                st_b = states[p].astype(BF16)
                for hh in range(2):
                    inter[hh].append(_dot_nt(qm[hh][gs, :], st_b))
                new = [_dot_tn(v[hh][gs, :], kd_p[gs, :]) for hh in range(2)]
                states[p] = (jnp.exp(b_last[g][:, ls]) * states[p]
                             + jnp.where(lane_s, new[0], new[1]))
            for hh in range(2):
                hs = slice(heads[hh] * GLA_DV, (heads[hh] + 1) * GLA_DV)
                o = _dot(s2[hh * sc:(hh + 1) * sc, :], v[hh]) + jnp.concatenate(inter[hh], axis=0)
                o = _rms_rows(o) * g_row
                gate = _silu(gg_ref[rs, hs].astype(F32))
                o_ref[rs, hs] = (o * gate).astype(BF16)
    st_ref[0] = states[0]
    st_ref[1] = states[1]


def _gla(z, f, norm_g, bsz, seq):
    t = z.shape[0]
    tc = min(GLA_ROWS, seq)
    nc = seq // tc
    return pl.pallas_call(
        _gla_kernel,
        grid=(bsz, nc),
        in_specs=[
            pl.BlockSpec((tc, GLA_QK_W), lambda b, i: (b * nc + i, 0)),
            pl.BlockSpec((tc, GLA_QK_W), lambda b, i: (b * nc + i, 1)),
            pl.BlockSpec((tc, GLA_QK_W), lambda b, i: (b * nc + i, 2)),
            pl.BlockSpec((tc, GLA_V_W), lambda b, i: (b * nc + i, 0)),
            pl.BlockSpec((tc, GLA_V_W), lambda b, i: (b * nc + i, 1)),
            pl.BlockSpec((1, GLA_DV), lambda b, i: (0, 0)),
        ],
        out_specs=pl.BlockSpec((tc, GLA_V_W), lambda b, i: (b * nc + i, 0)),
        out_shape=jax.ShapeDtypeStruct((t, GLA_V_W), BF16),
        scratch_shapes=[pltpu.VMEM((2, LANES, LANES), F32)],
        compiler_params=_cp(("arbitrary", "arbitrary"), 32),
        name="gla",
    )(f, f, f, z, z, norm_g.reshape(1, GLA_DV))


def _diff_attn_kernel(lam_init, q_ref, k_ref, v_ref, lq1_ref, lk1_ref, lq2_ref, lk2_ref, g_ref,
                      o_ref, vt_ref, s_ref, m_ref, l_ref, acc_ref):
    tq = q_ref.shape[0]
    seq = k_ref.shape[0]
    n_heads = q_ref.shape[1] // LANES
    qi = pl.program_id(2)

    @pl.when(qi == 0)
    def _():
        for hh in range(n_heads):
            for cb in range(seq // tq):
                blk = v_ref[cb * tq:(cb + 1) * tq, hh * LANES:(hh + 1) * LANES]
                vt_ref[hh, :, cb * tq:(cb + 1) * tq] = blk.astype(F32).T.astype(BF16)

    lo = lax.broadcasted_iota(jnp.int32, (tq, LANES), 1) < DIFF_D
    qqs = []
    for hh in range(n_heads):
        q = q_ref[:, hh * LANES:(hh + 1) * LANES]
        zero = jnp.zeros_like(q)
        qqs.append(jnp.concatenate([jnp.where(lo, q, zero), jnp.where(lo, zero, q)], axis=0))

    m_ref[...] = jnp.full(m_ref.shape, -jnp.inf, F32)
    l_ref[...] = jnp.zeros_like(l_ref)
    acc_ref[...] = jnp.zeros_like(acc_ref)

    def scores(j, slot):
        start = pl.multiple_of(j * tq, tq)
        for hh in range(n_heads):
            kj = k_ref[pl.ds(start, tq), hh * LANES:(hh + 1) * LANES]
            s_ref[slot, hh] = _dot_nt(kj, qqs[hh])

    def update(j, slot, masked):
        start = pl.multiple_of(j * tq, tq)
        for hh in range(n_heads):
            vtj = vt_ref[hh, :, pl.ds(start, tq)]
            alpha, p = [], []
            for c in range(2 * tq // LANES):
                cs = slice(c * LANES, (c + 1) * LANES)
                s_c = s_ref[slot, hh, :, cs]
                if masked:
                    kv = lax.broadcasted_iota(jnp.int32, s_c.shape, 0)
                    q_pos = lax.broadcasted_iota(jnp.int32, s_c.shape, 1) + (c * LANES) % tq
                    s_c = jnp.where(kv <= q_pos, s_c, -jnp.inf)
                m_old = m_ref[hh, :, cs]
                m_c = jnp.maximum(m_old, jnp.max(s_c, axis=0, keepdims=True))
                a_c = jnp.exp2(m_old - m_c)
                p_c = jnp.exp2(s_c - m_c)
                m_ref[hh, :, cs] = m_c
                l_ref[hh, :, cs] = a_c * l_ref[hh, :, cs] + jnp.sum(p_c, axis=0, keepdims=True)
                alpha.append(a_c)
                p.append(p_c.astype(BF16))
            acc_ref[hh] = (jnp.concatenate(alpha, axis=1) * acc_ref[hh]
                           + _dot(vtj, jnp.concatenate(p, axis=1)))

    scores(0, 0)

    def body(i, carry):
        j = 2 * i
        scores(j + 1, 1)
        update(j, 0, False)
        scores(j + 2, 0)
        update(j + 1, 1, False)
        return carry

    n_pairs = qi // 2
    lax.fori_loop(0, n_pairs, body, 0)
    j0 = 2 * n_pairs

    @pl.when(qi % 2 == 1)
    def _():
        scores(j0 + 1, 1)
        update(j0, 0, False)
        update(j0 + 1, 1, True)

    @pl.when(qi % 2 == 0)
    def _():
        update(j0, 0, True)

    lam = (jnp.exp(jnp.sum(lq1_ref[...] * lk1_ref[...], axis=-1, keepdims=True))
           - jnp.exp(jnp.sum(lq2_ref[...] * lk2_ref[...], axis=-1, keepdims=True)) + lam_init)
    for hh in range(n_heads):
        l = l_ref[hh]
        acc = acc_ref[hh]
        o12 = acc / l
        o = o12[:, :tq] - lam * o12[:, tq:]
        o = o * lax.rsqrt(jnp.mean(o * o, axis=0, keepdims=True) + EPS)
        o = o * g_ref[...] * (1.0 - lam_init)
        o_ref[:, hh * DIFF_DV:(hh + 1) * DIFF_DV] = o.T.astype(BF16)


def _diff_attn(z, lq1, lk1, lq2, lk2, norm_g, lam_init, bsz, seq):
    t = z.shape[0]
    tq = min(ATT_Q_ROWS, seq)
    nq = seq // tq
    hw = ATT_HEADS_PER_STEP * LANES
    qb, kb, vb = 1024 // hw, 1536 // hw, 2048 // hw
    small = pl.BlockSpec((1, DIFF_D), lambda b, h, i: (0, 0))
    return pl.pallas_call(
        functools.partial(_diff_attn_kernel, lam_init),
        grid=(bsz, DIFF_HEADS // ATT_HEADS_PER_STEP, nq),
        in_specs=[
            pl.BlockSpec((tq, hw), lambda b, h, i: (b * nq + i, qb + h)),
            pl.BlockSpec((seq, hw), lambda b, h, i: (b, kb + h)),
            pl.BlockSpec((seq, hw), lambda b, h, i: (b, vb + h)),
            small, small, small, small,
            pl.BlockSpec((DIFF_DV, 1), lambda b, h, i: (0, 0)),
        ],
        out_specs=pl.BlockSpec((tq, hw), lambda b, h, i: (b * nq + i, h)),
        out_shape=jax.ShapeDtypeStruct((t, DIFF_V_W), BF16),
        scratch_shapes=[pltpu.VMEM((ATT_HEADS_PER_STEP, DIFF_DV, seq), BF16),
                        pltpu.VMEM((2, ATT_HEADS_PER_STEP, tq, 2 * tq), F32),
                        pltpu.VMEM((ATT_HEADS_PER_STEP, 1, 2 * tq), F32),
                        pltpu.VMEM((ATT_HEADS_PER_STEP, 1, 2 * tq), F32),
                        pltpu.VMEM((ATT_HEADS_PER_STEP, DIFF_DV, 2 * tq), F32)],
        compiler_params=_cp(("arbitrary", "arbitrary", "arbitrary"), 32),
        name="diff_attn",
    )(z, z, z, lq1.reshape(1, DIFF_D), lk1.reshape(1, DIFF_D), lq2.reshape(1, DIFF_D),
      lk2.reshape(1, DIFF_D), norm_g.reshape(DIFF_DV, 1))


def _sgu_ret_kernel(su_ref, sv_ref, rv_ref, rg_ref, q_ref, k_ref, ws_ref, bs_ref, o_ref, st_ref):
    @pl.when(pl.program_id(1) == 0)
    def _():
        st_ref[...] = jnp.zeros_like(st_ref)

    c = RET_CHUNK
    row = lax.broadcasted_iota(jnp.int32, (c, c), 0)
    col = lax.broadcasted_iota(jnp.int32, (c, c), 1)
    causal = row >= col
    log_g = [math.log(1.0 - 2.0 ** (-5.0 - h)) for h in range(RET_HEADS)]
    lo = col < RET_DK
    rel = (row - col).astype(F32)
    pos = row.astype(F32)
    w_sgu = [jnp.where(causal, ws_ref[g], 0.0).astype(BF16) for g in range(SGU_GROUPS)]
    decays = [jnp.where(causal, jnp.exp(log_g[h] * jnp.maximum(rel, 0.0)), 0.0)
              for h in range(RET_HEADS)]
    lgs = [jnp.where(lo, log_g[2 * p], log_g[2 * p + 1]) for p in range(RET_HEADS // 2)]
    q_decs = [jnp.exp(lg * (pos + 1.0)) for lg in lgs]
    k_decs = [jnp.exp(lg * (c - 1.0 - pos)) for lg in lgs]
    states = [st_ref[p] for p in range(RET_HEADS // 2)]
    for j in range(su_ref.shape[0] // c):
        rs = slice(j * c, (j + 1) * c)
        for g in range(SGU_GROUPS):
            gs = slice(g * SGU_CH, (g + 1) * SGU_CH)
            s = _dot(w_sgu[g], sv_ref[rs, gs]) + bs_ref[g]
            o_ref[rs, gs] = (su_ref[rs, gs].astype(F32) * s).astype(BF16)
        for p in range(RET_HEADS // 2):
            ls = slice(p * LANES, (p + 1) * LANES)
            q_p = q_ref[rs, ls]
            k_p = k_ref[rs, ls]
            qm = (jnp.where(lo, q_p, 0.0).astype(BF16), jnp.where(lo, 0.0, q_p).astype(BF16))
            s2 = _dot_nt(jnp.concatenate(qm, axis=0), k_p.astype(BF16))
            qd = q_p * q_decs[p]
            qdm = (jnp.where(lo, qd, 0.0).astype(BF16), jnp.where(lo, 0.0, qd).astype(BF16))
            kd = (k_p * k_decs[p]).astype(BF16)
            st_b = states[p].astype(BF16)
            new = []
            for hh in range(2):
                h = 2 * p + hh
                hs = slice(h * RET_DV, (h + 1) * RET_DV)
                s_h = (s2[hh * c:(hh + 1) * c, :] * decays[h]).astype(BF16)
                v_h = rv_ref[rs, hs]
                o = _dot(s_h, v_h) + _dot_nt(qdm[hh], st_b)
                gate = _silu(rg_ref[rs, hs].astype(F32))
                o_ref[rs, SGU_W + h * RET_DV:SGU_W + (h + 1) * RET_DV] = (
                    _rms_rows(o) * gate).astype(BF16)
                new.append(_dot_tn(v_h, kd))
            states[p] = jnp.exp(lgs[p] * float(c)) * states[p] + jnp.where(lo, new[0], new[1])
    for p in range(RET_HEADS // 2):
        st_ref[p] = states[p]


def _sgu_ret(z, f, w_s, b_s, bsz, seq):
    t = z.shape[0]
    cc = RET_CHUNK
    c = min(SGU_RET_ROWS, seq)
    nc = seq // c
    return pl.pallas_call(
        _sgu_ret_kernel,
        grid=(bsz, nc),
        in_specs=[
            pl.BlockSpec((c, SGU_W), lambda b, i: (b * nc + i, 0)),
            pl.BlockSpec((c, SGU_W), lambda b, i: (b * nc + i, 1)),
            pl.BlockSpec((c, RET_V_W), lambda b, i: (b * nc + i, 2)),
            pl.BlockSpec((c, RET_V_W), lambda b, i: (b * nc + i, 3)),
            pl.BlockSpec((c, RET_QK_W), lambda b, i: (b * nc + i, 0)),
            pl.BlockSpec((c, RET_QK_W), lambda b, i: (b * nc + i, 1)),
            pl.BlockSpec((SGU_GROUPS, cc, cc), lambda b, i: (0, 0, 0)),
            pl.BlockSpec((SGU_GROUPS, cc, 1), lambda b, i: (0, 0, 0)),
        ],
        out_specs=pl.BlockSpec((c, SGU_W + RET_V_W), lambda b, i: (b * nc + i, 0)),
        out_shape=jax.ShapeDtypeStruct((t, SGU_W + RET_V_W), BF16),
        scratch_shapes=[pltpu.VMEM((2, LANES, LANES), F32)],
        compiler_params=_cp(("arbitrary", "arbitrary"), 32),
        name="sgu_retention",
    )(z, z, z, z, f, f, w_s, b_s.reshape(SGU_GROUPS, cc, 1))


def _out_proj_kernel(n_in, *refs):
    o_refs = refs[:n_in]
    w_ref, x_ref, mod_ref, rw_ref, rb_ref, xn_ref, h_ref, lg_ref = refs[n_in:]
    tm = x_ref.shape[0]
    k_each = D_MODEL // n_in
    gate1 = mod_ref[0, 2:3, :]
    sub = min(tm, OUT_PROJ_SUB_ROWS)
    for r0 in range(0, tm, sub):
        rs = slice(r0, r0 + sub)
        y = _dot(o_refs[0][rs, :], w_ref[0:k_each, :])
        for n in range(1, n_in):
            y = y + _dot(o_refs[n][rs, :], w_ref[n * k_each:(n + 1) * k_each, :])
        xn = x_ref[rs, :] + gate1 * y
        xn_ref[rs, :] = xn
        h = _rms_rows(xn) * (1.0 + mod_ref[0, 4:5, :]) + mod_ref[0, 3:4, :]
        for c in range(ROW_TILES):
            h_ref[pl.ds(r0 * ROW_TILES + c, sub, stride=ROW_TILES), :] = (
                h[:, c * LANES:(c + 1) * LANES])
        lg_ref[:, rs] = _dot_nt(rw_ref[...], h.astype(BF16)) + rb_ref[...]


def _out_proj(mixed, w_out, x2, mod, router_w, router_b, seq):
    t, d = x2.shape
    tm = min(PROJ_ROWS, seq)
    n_in = len(mixed)
    k_each = d // n_in
    rw = router_w.T.astype(BF16)
    rb = router_b.reshape(N_EXPERTS, 1)
    return pl.pallas_call(
        functools.partial(_out_proj_kernel, n_in),
        grid=(t // tm,),
        in_specs=[pl.BlockSpec((tm, k_each), lambda i: (i, 0)) for _ in mixed] + [
            pl.BlockSpec((d, d), lambda i: (0, 0)),
            pl.BlockSpec((tm, d), lambda i: (i, 0)),
            pl.BlockSpec((1, 6, d), lambda i: (i // (seq // tm), 0, 0)),
            pl.BlockSpec((N_EXPERTS, d), lambda i: (0, 0)),
            pl.BlockSpec((N_EXPERTS, 1), lambda i: (0, 0)),
        ],
        out_specs=[pl.BlockSpec((tm, d), lambda i: (i, 0)),
                   pl.BlockSpec((tm * ROW_TILES, LANES), lambda i: (i, 0)),
                   pl.BlockSpec((N_EXPERTS, tm), lambda i: (0, i))],
        out_shape=[jax.ShapeDtypeStruct((t, d), F32),
                   jax.ShapeDtypeStruct((t * ROW_TILES, LANES), F32),
                   jax.ShapeDtypeStruct((N_EXPERTS, t), F32)],
        compiler_params=_cp(("arbitrary",), 48),
        name="out_proj",
    )(*mixed, w_out.astype(BF16), x2, mod, rw, rb)


def _route_kernel(lg_ref, idx_ref, gate_ref, cnt_ref, run_ref):
    @pl.when(pl.program_id(0) == 0)
    def _():
        run_ref[...] = jnp.zeros_like(run_ref)

    tm = lg_ref.shape[1]
    row = lax.broadcasted_iota(jnp.int32, (N_EXPERTS, tm), 0)
    neg = -jnp.inf
    l = lg_ref[...]
    vals, firsts, hots = [], [], []
    for _ in range(TOP_K):
        m = jnp.max(l, axis=0, keepdims=True)
        first = jnp.min(jnp.where(l == m, row, N_EXPERTS), axis=0, keepdims=True)
        hot = row == first
        vals.append(m)
        firsts.append(first)
        hots.append(hot)
        l = jnp.where(hot, neg, l)
    sel = hots[0] | hots[1] | hots[2] | hots[3]
    ex = [jnp.exp(v - vals[0]) for v in vals]
    denom = ex[0] + ex[1] + ex[2] + ex[3]
    r_i = lax.broadcasted_iota(jnp.int32, (tm, tm), 0)
    c_i = lax.broadcasted_iota(jnp.int32, (tm, tm), 1)
    before = (r_i < c_i).astype(BF16)
    run = run_ref[:, 0:1]
    cum = _dot(sel.astype(BF16), before) + run
    row8 = lax.broadcasted_iota(jnp.int32, (2 * TOP_K, tm), 0)
    idx_out = jnp.zeros((2 * TOP_K, tm), jnp.int32)
    gate_out = jnp.zeros((2 * TOP_K, tm), F32)
    for k in range(TOP_K):
        rank_k = jnp.sum(jnp.where(hots[k], cum, 0.0), axis=0, keepdims=True).astype(jnp.int32)
        idx_out = jnp.where(row8 == k, firsts[k], idx_out)
        idx_out = jnp.where(row8 == TOP_K + k, rank_k, idx_out)
        gate_out = jnp.where(row8 == k, ex[k] / denom, gate_out)
    idx_ref[...] = idx_out
    gate_ref[...] = gate_out
    total = run + jnp.sum(sel.astype(F32), axis=1, keepdims=True)
    run_ref[...] = jnp.broadcast_to(total, run_ref.shape)
    cnt_ref[...] = jnp.broadcast_to(total, cnt_ref.shape)


def _route(logits_t):
    t = logits_t.shape[1]
    tm = min(ROUTE_ROWS, t)
    return pl.pallas_call(
        _route_kernel,
        grid=(t // tm,),
        in_specs=[pl.BlockSpec((N_EXPERTS, tm), lambda i: (0, i))],
        out_specs=[pl.BlockSpec((2 * TOP_K, tm), lambda i: (0, i)),
                   pl.BlockSpec((2 * TOP_K, tm), lambda i: (0, i)),
                   pl.BlockSpec((N_EXPERTS, LANES), lambda i: (0, 0))],
        out_shape=[jax.ShapeDtypeStruct((2 * TOP_K, t), jnp.int32),
                   jax.ShapeDtypeStruct((2 * TOP_K, t), F32),
                   jax.ShapeDtypeStruct((N_EXPERTS, LANES), F32)],
        scratch_shapes=[pltpu.VMEM((N_EXPERTS, LANES), F32)],
        compiler_params=_cp(("arbitrary",), 32),
        name="route",
    )(logits_t)


def _dispatch_kernel(fs_ref, fl_ref, h_ref, dest_ref, xs_ref, zero_buf, sem, zsem):
    tm = h_ref.shape[0] // ROW_TILES
    n_fill = fs_ref.shape[0]

    def rows(ref, start, n):
        return ref.at[pl.ds(pl.multiple_of(start * ROW_TILES, ROW_TILES), n * ROW_TILES), :]

    def fill_copies(f, wait):
        start, n = fs_ref[f], fl_ref[f]
        n_chunks = lax.shift_right_logical(n, FILL_SHIFT)
        tail = start + n_chunks * FILL_ROWS

        def chunk(j, carry):
            cp = pltpu.make_async_copy(zero_buf, rows(xs_ref, start + j * FILL_ROWS, FILL_ROWS), zsem)
            cp.wait() if wait else cp.start()
            return carry

        def single(j, carry):
            cp = pltpu.make_async_copy(rows(zero_buf, 0, 1), rows(xs_ref, tail + j, 1), zsem)
            cp.wait() if wait else cp.start()
            return carry

        lax.fori_loop(0, n_chunks, chunk, 0)
        lax.fori_loop(0, n - n_chunks * FILL_ROWS, single, 0)

    @pl.when(pl.program_id(0) == 0)
    def _():
        zero_buf[...] = jnp.zeros_like(zero_buf)
        lax.fori_loop(0, n_fill, lambda f, c: (fill_copies(f, False), c)[1], 0)

    def issue(rb, carry):
        for rr in range(ISSUE_UNROLL):
            r = rb * ISSUE_UNROLL + rr
            for k in range(TOP_K):
                pltpu.make_async_copy(rows(h_ref, r, 1), rows(xs_ref, dest_ref[r * TOP_K + k], 1),
                                      sem).start(priority=k % 2)
        return carry

    lax.fori_loop(0, tm // ISSUE_UNROLL, issue, 0)
    for _ in range(TOP_K):
        pltpu.make_async_copy(h_ref, rows(xs_ref, 0, tm), sem).wait()

    @pl.when(pl.program_id(0) == 0)
    def _():
        lax.fori_loop(0, n_fill, lambda f, c: (fill_copies(f, True), c)[1], 0)


def _dispatch(h3, dest_flat, fill_start, fill_len, n_rows):
    t = h3.shape[0] // ROW_TILES
    tm = min(DISPATCH_ROWS, t)
    grid_spec = pltpu.PrefetchScalarGridSpec(
        num_scalar_prefetch=2,
        grid=(t // tm,),
        in_specs=[pl.BlockSpec((tm * ROW_TILES, LANES), lambda i, fs, fl: (i, 0)),
                  pl.BlockSpec((tm * TOP_K,), lambda i, fs, fl: (i,), memory_space=pltpu.SMEM)],
        out_specs=pl.BlockSpec(memory_space=pl.ANY),
        scratch_shapes=[pltpu.VMEM((FILL_ROWS * ROW_TILES, LANES), F32),
                        pltpu.SemaphoreType.DMA(()), pltpu.SemaphoreType.DMA(())],
    )
    return pl.pallas_call(
        _dispatch_kernel,
        grid_spec=grid_spec,
        out_shape=jax.ShapeDtypeStruct((n_rows * ROW_TILES, LANES), F32),
        compiler_params=_cp(("arbitrary",), 32),
        name="dispatch",
    )(fill_start, fill_len, h3, dest_flat)


def _expert_kernel(layer, be_ref, nu_ref, nx_ref, nv_ref, xs_ref, wi_hbm, bi_ref, wo_hbm, bo_ref, y_ref,
                   wi_st, wo_st, wi_b, wo_b, sems):
    i = pl.program_id(0)
    tb = xs_ref.shape[0] // ROW_TILES
    e = be_ref[i]
    fresh = jnp.logical_or(i == 0, e != be_ref[jnp.maximum(i - 1, 0)])
    used = i < nu_ref[0]

    def fetch(ex):
        return (pltpu.make_async_copy(wi_hbm.at[layer, ex], wi_st, sems.at[0]),
                pltpu.make_async_copy(wo_hbm.at[layer, ex], wo_st, sems.at[1]))

    @pl.when(i == 0)
    def _():
        for cp in fetch(e):
            cp.start()

    @pl.when(jnp.logical_and(fresh, used))
    def _():
        for cp in fetch(e):
            cp.wait()
        wi_b[...] = wi_st[...].astype(BF16)
        wo_b[...] = wo_st[...].astype(BF16)

        @pl.when(nx_ref[i] >= 0)
        def _():
            for cp in fetch(nx_ref[i]):
                cp.start()

    def compute(n):
        x = jnp.concatenate(
            [xs_ref[pl.ds(c, n, stride=ROW_TILES), :] for c in range(ROW_TILES)],
            axis=1).astype(BF16)
        y = jnp.zeros((n, D_MODEL), F32) + bo_ref[0, 0]
        half = 512
        for j in range(D_FF // half):
            a, b = j * half, (j + 1) * half
            glu = _dot(x, wi_b[:, a:b]) + bi_ref[0, 0, :, a:b]
            lin = _dot(x, wi_b[:, D_FF + a:D_FF + b]) + bi_ref[0, 0, :, D_FF + a:D_FF + b]
            glu = jnp.minimum(glu, SWIGLU_LIMIT)
            lin = jnp.clip(lin, -SWIGLU_LIMIT, SWIGLU_LIMIT)
            act = glu * jax.nn.sigmoid(SWIGLU_ALPHA * glu) * (lin + 1.0)
            y = y + _dot(act.astype(BF16), wo_b[a:b, :])
        for c in range(ROW_TILES):
            y_ref[pl.ds(c, n, stride=ROW_TILES), :] = y[:, c * LANES:(c + 1) * LANES]

    full = nv_ref[i] > tb // 2

    @pl.when(jnp.logical_and(used, full))
    def _():
        compute(tb)

    @pl.when(jnp.logical_and(used, jnp.logical_not(full)))
    def _():
        compute(tb // 2)
        y_ref[pl.ds(tb // 2 * ROW_TILES, tb // 2 * ROW_TILES), :] = jnp.zeros(
            (tb // 2 * ROW_TILES, LANES), F32)

    @pl.when(jnp.logical_not(used))
    def _():
        y_ref[...] = jnp.zeros_like(y_ref)


def _experts(xs, block_e, n_used, next_e, n_valid, layer, w_in, b_in, w_out, b_out):
    tb = EXPERT_ROWS
    n_rows = xs.shape[0] // ROW_TILES
    nb = n_rows // tb
    depth, ne, d, f2 = w_in.shape

    def row_map(i, be, nu, nx, nv):
        return (jnp.minimum(i, nu[0] - 1), 0)

    grid_spec = pltpu.PrefetchScalarGridSpec(
        num_scalar_prefetch=4,
        grid=(nb,),
        in_specs=[
            pl.BlockSpec((tb * ROW_TILES, LANES), row_map),
            pl.BlockSpec(memory_space=pl.ANY),
            pl.BlockSpec((1, 1, 1, f2), lambda i, be, nu, nx, nv: (layer, be[i], 0, 0)),
            pl.BlockSpec(memory_space=pl.ANY),
            pl.BlockSpec((1, 1, 1, d), lambda i, be, nu, nx, nv: (layer, be[i], 0, 0)),
        ],
        out_specs=pl.BlockSpec((tb * ROW_TILES, LANES), lambda i, be, nu, nx, nv: (i, 0)),
        scratch_shapes=[pltpu.VMEM((d, f2), F32), pltpu.VMEM((D_FF, d), F32),
                        pltpu.VMEM((d, f2), BF16), pltpu.VMEM((D_FF, d), BF16),
                        pltpu.SemaphoreType.DMA((2,))],
    )
    return pl.pallas_call(
        functools.partial(_expert_kernel, layer),
        grid_spec=grid_spec,
        out_shape=jax.ShapeDtypeStruct((n_rows * ROW_TILES, LANES), F32),
        compiler_params=_cp(("arbitrary",), 56),
        name="experts",
    )(block_e, n_used, next_e, n_valid, xs, w_in, b_in.reshape(depth, ne, 1, f2), w_out,
      b_out.reshape(depth, ne, 1, d))


def _combine_kernel(final, dest_ref, dnext_ref, gate_ref, x_ref, mod_ref, fg_ref, yb_ref, o_ref,
                    g_buf, sems):
    tm = x_ref.shape[0]
    i = pl.program_id(0)
    cur = lax.rem(i, 2)

    def issue_all(d_ref, buf):
        def issue(rb, carry):
            for rr in range(ISSUE_UNROLL):
                r = rb * ISSUE_UNROLL + rr
                for k in range(TOP_K):
                    d = d_ref[r * TOP_K + k]
                    pltpu.make_async_copy(
                        yb_ref.at[pl.ds(pl.multiple_of(d * ROW_TILES, ROW_TILES), ROW_TILES), :],
                        g_buf.at[buf, pl.ds(pl.multiple_of((k * tm + r) * ROW_TILES, ROW_TILES),
                                            ROW_TILES), :],
                        sems.at[buf]).start(priority=k % 2)
            return carry

        lax.fori_loop(0, tm // ISSUE_UNROLL, issue, 0)

    @pl.when(i == 0)
    def _():
        issue_all(dest_ref, 0)

    @pl.when(i + 1 < pl.num_programs(0))
    def _():
        issue_all(dnext_ref, 1 - cur)

    pltpu.make_async_copy(yb_ref.at[pl.ds(0, TOP_K * tm * ROW_TILES), :], g_buf.at[cur],
                          sems.at[cur]).wait()

    gates = gate_ref[...]
    gate2 = mod_ref[0, 5:6, :]
    for c in range(ROW_TILES):
        y = jnp.zeros((tm, LANES), F32)
        for k in range(TOP_K):
            rows = g_buf[cur, pl.ds(k * tm * ROW_TILES + c, tm, stride=ROW_TILES), :]
            y = y + gates[:, k:k + 1] * rows
        cs = slice(c * LANES, (c + 1) * LANES)
        o_ref[:, cs] = x_ref[:, cs] + gate2[:, cs] * y
    if final:
        o_ref[...] = _rms_rows(o_ref[...]) * fg_ref[...]


def _combine(yb, dest_flat, gates, x2, mod, final_g, final, seq):
    t, d = x2.shape
    tm = min(COMBINE_ROWS, seq)
    last = t // tm - 1
    return pl.pallas_call(
        functools.partial(_combine_kernel, final),
        grid=(t // tm,),
        in_specs=[
            pl.BlockSpec((tm * TOP_K,), lambda i: (i,), memory_space=pltpu.SMEM),
            pl.BlockSpec((tm * TOP_K,), lambda i: (jnp.minimum(i + 1, last),),
                         memory_space=pltpu.SMEM),
            pl.BlockSpec((tm, 2 * TOP_K), lambda i: (i, 0)),
            pl.BlockSpec((tm, d), lambda i: (i, 0)),
            pl.BlockSpec((1, 6, d), lambda i: (i // (seq // tm), 0, 0)),
            pl.BlockSpec((1, d), lambda i: (0, 0)),
            pl.BlockSpec(memory_space=pl.ANY),
        ],
        out_specs=pl.BlockSpec((tm, d), lambda i: (i, 0)),
        out_shape=jax.ShapeDtypeStruct((t, d), F32),
        scratch_shapes=[pltpu.VMEM((2, TOP_K * tm * ROW_TILES, LANES), F32),
                        pltpu.SemaphoreType.DMA((2,))],
        compiler_params=_cp(("arbitrary",), 40),
        name="combine",
    )(dest_flat, dest_flat, gates, x2, mod, final_g.reshape(1, d), yb)


def _moe(h3, logits, x2, mod, layer, w_in, b_in, w_out, b_out, final_g, final, seq):
    t = x2.shape[0]
    tb = EXPERT_ROWS
    idx, gates_t, cnt = _route(logits)
    counts = cnt[:, 0].astype(jnp.int32)
    nblk = (counts + tb - 1) // tb
    blk_end = jnp.cumsum(nblk)
    pad_start = (blk_end - nblk) * tb
    e_hot = idx[:TOP_K, None, :] == jnp.arange(N_EXPERTS, dtype=jnp.int32)[None, :, None]
    dest = jnp.sum(jnp.where(e_hot, pad_start[None, :, None], 0), axis=1) + idx[TOP_K:]
    dest_flat = dest.T.reshape(t * TOP_K)
    gates = gates_t.T
    n_blocks = (t * TOP_K) // tb + N_EXPERTS
    n_used = blk_end[-1:]
    last_e = jnp.max(jnp.where(nblk > 0, jnp.arange(N_EXPERTS, dtype=jnp.int32), 0))
    blk = jnp.arange(n_blocks, dtype=jnp.int32)
    block_e = jnp.minimum(
        jnp.sum((blk_end[None, :] <= blk[:, None]).astype(jnp.int32), axis=1), last_e)
    experts = jnp.arange(N_EXPERTS, dtype=jnp.int32)
    later = (experts[None, :] > block_e[:, None]) & (nblk[None, :] > 0)
    next_e = jnp.min(jnp.where(later, experts[None, :], N_EXPERTS), axis=1)
    next_e = jnp.where(next_e == N_EXPERTS, -1, next_e).astype(jnp.int32)
    fill_start = jnp.concatenate([pad_start + counts, blk_end[-1:] * tb]).astype(jnp.int32)
    fill_len = jnp.concatenate([nblk * tb - counts, (n_blocks - blk_end[-1:]) * tb]).astype(jnp.int32)
    xs = _dispatch(h3, dest_flat, fill_start, fill_len, n_blocks * tb)
    row_end = jnp.sum(jnp.where(block_e[:, None] == experts[None, :],
                                (pad_start + counts)[None, :], 0), axis=1)
    n_valid = jnp.clip(row_end - blk * tb, 0, tb).astype(jnp.int32)
    yb = _experts(xs, block_e, n_used.astype(jnp.int32), next_e, n_valid, layer, w_in, b_in, w_out, b_out)
    return _combine(yb, dest_flat, gates, x2, mod, final_g, final, seq)


def kernel(x, c, positions, w_ada, b_ada, even_w_in, gla_w_gate, gla_b_gate, gla_norm_g,
           diff_lam_q1, diff_lam_k1, diff_lam_q2, diff_lam_k2, diff_norm_g, even_w_out,
           odd_w_in, sgu_ln_g, sgu_ln_b, sgu_w, sgu_b, odd_w_out,
           router_w, router_b, expert_w_in, expert_b_in, expert_w_out, expert_b_out,
           final_norm_g):
    bsz, seq, d = x.shape
    depth = w_ada.shape[0]
    t = bsz * seq
    mods = _modulation(c, w_ada, b_ada).reshape(depth, bsz, 6, d)
    cos, sin = _rope_tables(positions)
    x2 = x.reshape(t, d)
    for layer in range(depth):
        mod = mods[layer]
        j = layer // 2
        if layer % 2 == 0:
            z, f = _proj_even(x2, mod, even_w_in[j], gla_w_gate[j], gla_b_gate[j], cos, sin, seq)
            o_gla = _gla(z, f, gla_norm_g[j], bsz, seq)
            lam_init = 0.8 - 0.6 * math.exp(-0.3 * layer)
            o_diff = _diff_attn(z, diff_lam_q1[j], diff_lam_k1[j], diff_lam_q2[j], diff_lam_k2[j],
                                diff_norm_g[j], lam_init, bsz, seq)
            mixed, w_out = (o_gla, o_diff), even_w_out[j]
        else:
            z, f = _proj_odd(x2, mod, odd_w_in[j], sgu_ln_g[j], sgu_ln_b[j], cos, sin, seq)
            mixed, w_out = (_sgu_ret(z, f, sgu_w[j], sgu_b[j], bsz, seq),), odd_w_out[j]
        x2, h3, logits = _out_proj(mixed, w_out, x2, mod, router_w[layer], router_b[layer], seq)
        x2 = _moe(h3, logits, x2, mod, layer, expert_w_in, expert_b_in, expert_w_out,
                  expert_b_out, final_norm_g, layer == depth - 1, seq)
    return x2.reshape(bsz, seq, d)
```

```python
import functools
import math

import jax
import jax.numpy as jnp
from jax import lax
from jax.experimental import pallas as pl
from jax.experimental.pallas import tpu as pltpu

F32 = jnp.float32
BF16 = jnp.bfloat16

D_MODEL = 1024
EPS = 1e-6
ROPE_THETA = 10000.0
ROPE_HALF = 32

GLA_HEADS = 4
GLA_DK = 64
GLA_DV = 128
GLA_RANK = 16
GLA_CHUNK = 64
GLA_GATE_NORMALIZER = 16.0
GLA_QK_W = GLA_HEADS * GLA_DK
GLA_V_W = GLA_HEADS * GLA_DV

DIFF_HEADS = 4
DIFF_D = 64
DIFF_DV = 128
DIFF_QK_W = DIFF_HEADS * 2 * DIFF_D
DIFF_V_W = DIFF_HEADS * DIFF_DV

SGU_GROUPS = 4
SGU_CH = 128
SGU_CHUNK = 128
SGU_W = SGU_GROUPS * SGU_CH

RET_HEADS = 4
RET_DK = 64
RET_DV = 128
RET_CHUNK = 128
RET_QK_W = RET_HEADS * RET_DK
RET_V_W = RET_HEADS * RET_DV

N_EXPERTS = 32
TOP_K = 4
D_FF = D_MODEL
SWIGLU_ALPHA = 1.702
SWIGLU_LIMIT = 7.0

LANES = 128
SUBLANES = 8
ROW_TILES = D_MODEL // LANES

PROJ_ROWS = 512
GLA_ROWS = 512
GLA_GROUP = 4
SGU_RET_ROWS = 512
ATT_Q_ROWS = 256
ATT_HEADS_PER_STEP = 2
MOE_TILE = 512
SORT_SLAB = 512
RUN_BITS = 10
FILL_SHIFT = 6
FILL_ROWS = 1 << FILL_SHIFT
EXPERT_ROWS = 512
MOD_COLS = 1536

MIB = 1024 * 1024


def _cp(semantics, vmem_mib):
    return pltpu.CompilerParams(dimension_semantics=semantics, vmem_limit_bytes=vmem_mib * MIB)


def _dot(a, b):
    return jnp.dot(a, b, preferred_element_type=F32)


def _dot_nt(a, b):
    return lax.dot_general(a, b, (((1,), (1,)), ((), ())), preferred_element_type=F32)


def _dot_tn(a, b):
    return lax.dot_general(a, b, (((0,), (0,)), ((), ())), preferred_element_type=F32)


def _rms_rows(x):
    return x * lax.rsqrt(jnp.mean(x * x, axis=-1, keepdims=True) + EPS)


def _silu(x):
    return x * jax.nn.sigmoid(x)


def _rope_chunk(v, cos, sin, lo_half):
    rot = jnp.where(lo_half, -pltpu.roll(v, 96, 1), pltpu.roll(v, 32, 1))
    return v * cos + rot * sin


def _mod_kernel(c_ref, w_ref, b_ref, o_ref):
    c = c_ref[...]
    ca = _silu(c).astype(BF16)
    o_ref[0] = _dot(ca, w_ref[0].astype(BF16)) + b_ref[0]


def _modulation(c, w_ada, b_ada):
    depth, d, n = w_ada.shape
    bsz = c.shape[0]
    return pl.pallas_call(
        _mod_kernel,
        grid=(depth, n // MOD_COLS),
        in_specs=[
            pl.BlockSpec((bsz, d), lambda l, j: (0, 0)),
            pl.BlockSpec((1, d, MOD_COLS), lambda l, j: (l, 0, j)),
            pl.BlockSpec((1, 1, MOD_COLS), lambda l, j: (l, 0, j)),
        ],
        out_specs=pl.BlockSpec((1, bsz, MOD_COLS), lambda l, j: (l, 0, j)),
        out_shape=jax.ShapeDtypeStruct((depth, bsz, n), F32),
        compiler_params=_cp(("arbitrary", "arbitrary"), 40),
        name="adaln_mod",
    )(c, w_ada, b_ada.reshape(depth, 1, n))


def _rope_table_kernel(p_ref, f_ref, c_ref, s_ref):
    ang = p_ref[...].astype(F32) * f_ref[...]
    c_ref[...] = jnp.cos(ang)
    s_ref[...] = jnp.sin(ang)


def _rope_tables(positions):
    t = positions.size
    per_row = LANES // ROPE_HALF
    rows = t // per_row
    pos_d = jnp.repeat(positions.reshape(rows, per_row), ROPE_HALF, axis=1)
    inv_freq = ROPE_THETA ** (-jnp.arange(ROPE_HALF, dtype=F32) / ROPE_HALF)
    freq_d = jnp.tile(inv_freq, per_row).reshape(1, LANES)
    tr = min(512, rows)
    cos_d, sin_d = pl.pallas_call(
        _rope_table_kernel,
        grid=(rows // tr,),
        in_specs=[pl.BlockSpec((tr, LANES), lambda i: (i, 0)),
                  pl.BlockSpec((1, LANES), lambda i: (0, 0))],
        out_specs=[pl.BlockSpec((tr, LANES), lambda i: (i, 0))] * 2,
        out_shape=[jax.ShapeDtypeStruct((rows, LANES), F32)] * 2,
        compiler_params=_cp(("arbitrary",), 32),
        name="rope_tables",
    )(pos_d, freq_d)
    cos = jnp.tile(cos_d.reshape(t, ROPE_HALF), (1, per_row))
    sin = jnp.tile(sin_d.reshape(t, ROPE_HALF), (1, per_row))
    return cos, sin


def _modulated_rms(x_ref, mod_ref, which):
    x = x_ref[...]
    shift = mod_ref[0, 3 * which:3 * which + 1, :]
    scale = mod_ref[0, 3 * which + 1:3 * which + 2, :]
    return _rms_rows(x) * (1.0 + scale) + shift


EVEN_Z = (GLA_V_W, GLA_V_W, DIFF_QK_W, DIFF_QK_W, DIFF_V_W)
EVEN_Z_W = sum(EVEN_Z)
EVEN_F_W = 3 * GLA_QK_W
EVEN_W_COLS = EVEN_Z_W + 2 * GLA_QK_W + LANES


def _proj_even_kernel(x_ref, mod_ref, w_ref, wg_ref, bg_ref, cos_ref, sin_ref, z_ref, f_ref):
    hb = _modulated_rms(x_ref, mod_ref, 0).astype(BF16)

    def sec(a, b):
        return _dot(hb, w_ref[:, a:b])

    for a, b in ((0, 512), (512, 1024), (2048, 2560)):
        z_ref[:, a:b] = sec(a, b).astype(BF16)
    cos = cos_ref[...]
    sin = sin_ref[...]
    lo_half = (lax.broadcasted_iota(jnp.int32, cos.shape, 1) % 64) < ROPE_HALF
    q_scale = DIFF_D ** -0.5 * math.log2(math.e)
    for a, scl in ((1024, q_scale), (1536, 1.0)):
        full = sec(a, a + DIFF_QK_W)
        for c in range(DIFF_QK_W // LANES):
            v = full[:, c * LANES:(c + 1) * LANES]
            z_ref[:, a + c * LANES:a + (c + 1) * LANES] = (
                _rope_chunk(v, cos, sin, lo_half) * scl).astype(BF16)
    f_ref[:, 0:256] = sec(2560, 2816) * (GLA_DK ** -0.5)
    f_ref[:, 256:512] = sec(2816, 3072)
    gr = sec(3072, 3200).astype(BF16)
    pre = _dot(gr, wg_ref[...]) + bg_ref[...]
    log_sig = jnp.minimum(pre, 0.0) - jnp.log1p(jnp.exp(-jnp.abs(pre)))
    f_ref[:, 512:768] = log_sig / GLA_GATE_NORMALIZER


def _proj_even(x2, mod, w_in, w_gate, b_gate, cos, sin, seq):
    t, d = x2.shape
    tm = min(PROJ_ROWS, seq)
    gq, gk, gv, gr, gg, dq, dk, dv = jnp.split(
        w_in, [256, 512, 1024, 1040, 1552, 2064, 2576], axis=1)
    gr_pad = jnp.pad(gr, ((0, 0), (0, LANES - GLA_RANK)))
    w_cat = jnp.concatenate([gv, gg, dq, dk, dv, gq, gk, gr_pad], axis=1).astype(BF16)
    wg_pad = jnp.pad(w_gate, ((0, LANES - GLA_RANK), (0, 0))).astype(BF16)
    return pl.pallas_call(
        _proj_even_kernel,
        grid=(t // tm,),
        in_specs=[
            pl.BlockSpec((tm, d), lambda i: (i, 0)),
            pl.BlockSpec((1, 6, d), lambda i: (i // (seq // tm), 0, 0)),
            pl.BlockSpec((d, EVEN_W_COLS), lambda i: (0, 0)),
            pl.BlockSpec((LANES, GLA_QK_W), lambda i: (0, 0)),
            pl.BlockSpec((1, GLA_QK_W), lambda i: (0, 0)),
            pl.BlockSpec((tm, LANES), lambda i: (i, 0)),
            pl.BlockSpec((tm, LANES), lambda i: (i, 0)),
        ],
        out_specs=[pl.BlockSpec((tm, EVEN_Z_W), lambda i: (i, 0)),
                   pl.BlockSpec((tm, EVEN_F_W), lambda i: (i, 0))],
        out_shape=[jax.ShapeDtypeStruct((t, EVEN_Z_W), BF16),
                   jax.ShapeDtypeStruct((t, EVEN_F_W), F32)],
        compiler_params=_cp(("arbitrary",), 48),
        name="proj_even",
    )(x2, mod, w_cat, wg_pad, b_gate.reshape(1, GLA_QK_W), cos, sin)


ODD_Z_W = 2 * SGU_W + 2 * RET_V_W
ODD_F_W = 2 * RET_QK_W
ODD_W_COLS = ODD_Z_W + ODD_F_W


def _gelu_exact(x):
    return 0.5 * x * (1.0 + lax.erf(x * (2.0 ** -0.5)))


def _proj_odd_kernel(x_ref, mod_ref, w_ref, lng_ref, lnb_ref, cos_ref, sin_ref, z_ref, f_ref):
    hb = _modulated_rms(x_ref, mod_ref, 0).astype(BF16)

    def sec(a, b):
        return _dot(hb, w_ref[:, a:b])

    z_ref[:, 0:512] = _gelu_exact(sec(0, 512)).astype(BF16)
    sv = _gelu_exact(sec(512, 1024))
    mu = jnp.mean(sv, axis=-1, keepdims=True)
    cen = sv - mu
    var = jnp.mean(cen * cen, axis=-1, keepdims=True)
    z_ref[:, 512:1024] = (cen * lax.rsqrt(var + EPS) * lng_ref[...] + lnb_ref[...]).astype(BF16)
    for a, b in ((1024, 1536), (1536, 2048)):
        z_ref[:, a:b] = sec(a, b).astype(BF16)
    cos = cos_ref[...]
    sin = sin_ref[...]
    lo_half = (lax.broadcasted_iota(jnp.int32, cos.shape, 1) % 64) < ROPE_HALF
    k_scale = RET_DK ** -0.5
    for a, scl in ((0, 1.0), (256, k_scale)):
        full = sec(ODD_Z_W + a, ODD_Z_W + a + RET_QK_W)
        for c in range(RET_QK_W // LANES):
            v = full[:, c * LANES:(c + 1) * LANES]
            f_ref[:, a + c * LANES:a + (c + 1) * LANES] = _rope_chunk(v, cos, sin, lo_half) * scl


def _proj_odd(x2, mod, w_in, ln_g, ln_b, cos, sin, seq):
    t, d = x2.shape
    tm = min(PROJ_ROWS, seq)
    su, sv, rq, rk, rv, rg = jnp.split(w_in, [512, 1024, 1280, 1536, 2048], axis=1)
    w_cat = jnp.concatenate([su, sv, rv, rg, rq, rk], axis=1).astype(BF16)
    return pl.pallas_call(
        _proj_odd_kernel,
        grid=(t // tm,),
        in_specs=[
            pl.BlockSpec((tm, d), lambda i: (i, 0)),
            pl.BlockSpec((1, 6, d), lambda i: (i // (seq // tm), 0, 0)),
            pl.BlockSpec((d, ODD_W_COLS), lambda i: (0, 0)),
            pl.BlockSpec((1, SGU_W), lambda i: (0, 0)),
            pl.BlockSpec((1, SGU_W), lambda i: (0, 0)),
            pl.BlockSpec((tm, LANES), lambda i: (i, 0)),
            pl.BlockSpec((tm, LANES), lambda i: (i, 0)),
        ],
        out_specs=[pl.BlockSpec((tm, ODD_Z_W), lambda i: (i, 0)),
                   pl.BlockSpec((tm, ODD_F_W), lambda i: (i, 0))],
        out_shape=[jax.ShapeDtypeStruct((t, ODD_Z_W), BF16),
                   jax.ShapeDtypeStruct((t, ODD_F_W), F32)],
        compiler_params=_cp(("arbitrary",), 48),
        name="proj_odd",
    )(x2, mod, w_cat, ln_g.reshape(1, SGU_W), ln_b.reshape(1, SGU_W), cos, sin)


def _gla_kernel(q_ref, k_ref, la_ref, v_ref, gg_ref, g_ref, o_ref, st_ref):
    @pl.when(pl.program_id(1) == 0)
    def _():
        st_ref[...] = jnp.zeros_like(st_ref)

    c = GLA_CHUNK
    sc = GLA_GROUP * c
    rows = q_ref.shape[0]
    r_i = lax.broadcasted_iota(jnp.int32, (sc, sc), 0)
    c_i = lax.broadcasted_iota(jnp.int32, (sc, sc), 1)
    block_causal = (r_i // c == c_i // c) & (r_i >= c_i)
    tril = block_causal.astype(BF16)
    causal2 = jnp.concatenate([block_causal, block_causal], axis=0)
    lo = lax.broadcasted_iota(jnp.int32, (sc, LANES), 1) < GLA_DK
    lane_s = lax.broadcasted_iota(jnp.int32, (LANES, LANES), 1) < GLA_DK
    g_row = g_ref[...]
    states = [st_ref[0], st_ref[1]]
    for j in range(rows // sc):
        rs = slice(j * sc, (j + 1) * sc)
        la = la_ref[rs, :]
        la_hi = la.astype(BF16)
        la_lo = (la - la_hi.astype(F32)).astype(BF16)
        b = _dot(tril, la_hi) + _dot(tril, la_lo)
        b_last = [b[(g + 1) * c - 1:(g + 1) * c, :] for g in range(GLA_GROUP)]
        b_last_rows = jnp.concatenate(
            [jnp.broadcast_to(bl, (c, bl.shape[1])) for bl in b_last], axis=0)
        qd = q_ref[rs, :] * jnp.exp(b)
        kk = k_ref[rs, :]
        ki = kk * jnp.exp(-b)
        kd = kk * jnp.exp(b_last_rows - b)
        for p in range(GLA_HEADS // 2):
            ls = slice(p * LANES, (p + 1) * LANES)
            qd_p = qd[:, ls]
            qm = (jnp.where(lo, qd_p, 0.0).astype(BF16), jnp.where(lo, 0.0, qd_p).astype(BF16))
            s2 = _dot_nt(jnp.concatenate(qm, axis=0), ki[:, ls].astype(BF16))
            s2 = jnp.where(causal2, s2, 0.0).astype(BF16)
            kd_p = kd[:, ls].astype(BF16)
            heads = (2 * p, 2 * p + 1)
            v = [v_ref[rs, h * GLA_DV:(h + 1) * GLA_DV] for h in heads]
            inter = ([], [])
            for g in range(GLA_GROUP):
                gs = slice(g * c, (g + 1) * c)
                st_b = states[p].astype(BF16)
                for hh in range(2):
                    inter[hh].append(_dot_nt(qm[hh][gs, :], st_b))
                new = [_dot_tn(v[hh][gs, :], kd_p[gs, :]) for hh in range(2)]
                states[p] = (jnp.exp(b_last[g][:, ls]) * states[p]
                             + jnp.where(lane_s, new[0], new[1]))
            for hh in range(2):
                hs = slice(heads[hh] * GLA_DV, (heads[hh] + 1) * GLA_DV)
                o = _dot(s2[hh * sc:(hh + 1) * sc, :], v[hh]) + jnp.concatenate(inter[hh], axis=0)
                o = _rms_rows(o) * g_row
                gate = _silu(gg_ref[rs, hs].astype(F32))
                o_ref[rs, hs] = (o * gate).astype(BF16)
    st_ref[0] = states[0]
    st_ref[1] = states[1]


def _gla(z, f, norm_g, bsz, seq):
    t = z.shape[0]
    tc = min(GLA_ROWS, seq)
    nc = seq // tc
    return pl.pallas_call(
        _gla_kernel,
        grid=(bsz, nc),
        in_specs=[
            pl.BlockSpec((tc, GLA_QK_W), lambda b, i: (b * nc + i, 0)),
            pl.BlockSpec((tc, GLA_QK_W), lambda b, i: (b * nc + i, 1)),
            pl.BlockSpec((tc, GLA_QK_W), lambda b, i: (b * nc + i, 2)),
            pl.BlockSpec((tc, GLA_V_W), lambda b, i: (b * nc + i, 0)),
            pl.BlockSpec((tc, GLA_V_W), lambda b, i: (b * nc + i, 1)),
            pl.BlockSpec((1, GLA_DV), lambda b, i: (0, 0)),
        ],
        out_specs=pl.BlockSpec((tc, GLA_V_W), lambda b, i: (b * nc + i, 0)),
        out_shape=jax.ShapeDtypeStruct((t, GLA_V_W), BF16),
        scratch_shapes=[pltpu.VMEM((2, LANES, LANES), F32)],
        compiler_params=_cp(("arbitrary", "arbitrary"), 32),
        name="gla",
    )(f, f, f, z, z, norm_g.reshape(1, GLA_DV))


def _diff_attn_kernel(lam_init, q_ref, k_ref, v_ref, lq1_ref, lk1_ref, lq2_ref, lk2_ref, g_ref,
                      o_ref, vt_ref, s_ref, m_ref, l_ref, acc_ref):
    tq = q_ref.shape[0]
    seq = k_ref.shape[0]
    n_heads = q_ref.shape[1] // LANES
    qi = pl.program_id(2)

    @pl.when(qi == 0)
    def _():
        for hh in range(n_heads):
            for cb in range(seq // tq):
                blk = v_ref[cb * tq:(cb + 1) * tq, hh * LANES:(hh + 1) * LANES]
                vt_ref[hh, :, cb * tq:(cb + 1) * tq] = blk.astype(F32).T.astype(BF16)

    lo = lax.broadcasted_iota(jnp.int32, (tq, LANES), 1) < DIFF_D
    qqs = []
    for hh in range(n_heads):
        q = q_ref[:, hh * LANES:(hh + 1) * LANES]
        zero = jnp.zeros_like(q)
        qqs.append(jnp.concatenate([jnp.where(lo, q, zero), jnp.where(lo, zero, q)], axis=0))

    m_ref[...] = jnp.full(m_ref.shape, -jnp.inf, F32)
    l_ref[...] = jnp.zeros_like(l_ref)
    acc_ref[...] = jnp.zeros_like(acc_ref)

    def scores(j, slot):
        start = pl.multiple_of(j * tq, tq)
        for hh in range(n_heads):
            kj = k_ref[pl.ds(start, tq), hh * LANES:(hh + 1) * LANES]
            s_ref[slot, hh] = _dot_nt(kj, qqs[hh])

    def update(j, slot, masked):
        start = pl.multiple_of(j * tq, tq)
        for hh in range(n_heads):
            vtj = vt_ref[hh, :, pl.ds(start, tq)]
            alpha, p = [], []
            for c in range(2 * tq // LANES):
                cs = slice(c * LANES, (c + 1) * LANES)
                s_c = s_ref[slot, hh, :, cs]
                if masked:
                    kv = lax.broadcasted_iota(jnp.int32, s_c.shape, 0)
                    q_pos = lax.broadcasted_iota(jnp.int32, s_c.shape, 1) + (c * LANES) % tq
                    s_c = jnp.where(kv <= q_pos, s_c, -jnp.inf)
                m_old = m_ref[hh, :, cs]
                m_c = jnp.maximum(m_old, jnp.max(s_c, axis=0, keepdims=True))
                a_c = jnp.exp2(m_old - m_c)
                p_c = jnp.exp2(s_c - m_c)
                m_ref[hh, :, cs] = m_c
                l_ref[hh, :, cs] = a_c * l_ref[hh, :, cs] + jnp.sum(p_c, axis=0, keepdims=True)
                alpha.append(a_c)
                p.append(p_c.astype(BF16))
            acc_ref[hh] = (jnp.concatenate(alpha, axis=1) * acc_ref[hh]
                           + _dot(vtj, jnp.concatenate(p, axis=1)))

    scores(0, 0)

    def body(i, carry):
        j = 2 * i
        scores(j + 1, 1)
        update(j, 0, False)
        scores(j + 2, 0)
        update(j + 1, 1, False)
        return carry

    n_pairs = qi // 2
    lax.fori_loop(0, n_pairs, body, 0)
    j0 = 2 * n_pairs

    @pl.when(qi % 2 == 1)
    def _():
        scores(j0 + 1, 1)
        update(j0, 0, False)
        update(j0 + 1, 1, True)

    @pl.when(qi % 2 == 0)
    def _():
        update(j0, 0, True)

    lam = (jnp.exp(jnp.sum(lq1_ref[...] * lk1_ref[...], axis=-1, keepdims=True))
           - jnp.exp(jnp.sum(lq2_ref[...] * lk2_ref[...], axis=-1, keepdims=True)) + lam_init)
    for hh in range(n_heads):
        l = l_ref[hh]
        acc = acc_ref[hh]
        o12 = acc / l
        o = o12[:, :tq] - lam * o12[:, tq:]
        o = o * lax.rsqrt(jnp.mean(o * o, axis=0, keepdims=True) + EPS)
        o = o * g_ref[...] * (1.0 - lam_init)
        o_ref[:, hh * DIFF_DV:(hh + 1) * DIFF_DV] = o.T.astype(BF16)


def _diff_attn(z, lq1, lk1, lq2, lk2, norm_g, lam_init, bsz, seq):
    t = z.shape[0]
    tq = min(ATT_Q_ROWS, seq)
    nq = seq // tq
    hw = ATT_HEADS_PER_STEP * LANES
    qb, kb, vb = 1024 // hw, 1536 // hw, 2048 // hw
    small = pl.BlockSpec((1, DIFF_D), lambda b, h, i: (0, 0))
    return pl.pallas_call(
        functools.partial(_diff_attn_kernel, lam_init),
        grid=(bsz, DIFF_HEADS // ATT_HEADS_PER_STEP, nq),
        in_specs=[
            pl.BlockSpec((tq, hw), lambda b, h, i: (b * nq + i, qb + h)),
            pl.BlockSpec((seq, hw), lambda b, h, i: (b, kb + h)),
            pl.BlockSpec((seq, hw), lambda b, h, i: (b, vb + h)),
            small, small, small, small,
            pl.BlockSpec((DIFF_DV, 1), lambda b, h, i: (0, 0)),
        ],
        out_specs=pl.BlockSpec((tq, hw), lambda b, h, i: (b * nq + i, h)),
        out_shape=jax.ShapeDtypeStruct((t, DIFF_V_W), BF16),
        scratch_shapes=[pltpu.VMEM((ATT_HEADS_PER_STEP, DIFF_DV, seq), BF16),
                        pltpu.VMEM((2, ATT_HEADS_PER_STEP, tq, 2 * tq), F32),
                        pltpu.VMEM((ATT_HEADS_PER_STEP, 1, 2 * tq), F32),
                        pltpu.VMEM((ATT_HEADS_PER_STEP, 1, 2 * tq), F32),
                        pltpu.VMEM((ATT_HEADS_PER_STEP, DIFF_DV, 2 * tq), F32)],
        compiler_params=_cp(("arbitrary", "arbitrary", "arbitrary"), 32),
        name="diff_attn",
    )(z, z, z, lq1.reshape(1, DIFF_D), lk1.reshape(1, DIFF_D), lq2.reshape(1, DIFF_D),
      lk2.reshape(1, DIFF_D), norm_g.reshape(DIFF_DV, 1))


def _sgu_ret_kernel(su_ref, sv_ref, rv_ref, rg_ref, q_ref, k_ref, ws_ref, bs_ref, o_ref, st_ref):
    @pl.when(pl.program_id(1) == 0)
    def _():
        st_ref[...] = jnp.zeros_like(st_ref)

    c = RET_CHUNK
    row = lax.broadcasted_iota(jnp.int32, (c, c), 0)
    col = lax.broadcasted_iota(jnp.int32, (c, c), 1)
    causal = row >= col
    log_g = [math.log(1.0 - 2.0 ** (-5.0 - h)) for h in range(RET_HEADS)]
    lo = col < RET_DK
    rel = (row - col).astype(F32)
    pos = row.astype(F32)
    w_sgu = [jnp.where(causal, ws_ref[g], 0.0).astype(BF16) for g in range(SGU_GROUPS)]
    decays = [jnp.where(causal, jnp.exp(log_g[h] * jnp.maximum(rel, 0.0)), 0.0)
              for h in range(RET_HEADS)]
    lgs = [jnp.where(lo, log_g[2 * p], log_g[2 * p + 1]) for p in range(RET_HEADS // 2)]
    q_decs = [jnp.exp(lg * (pos + 1.0)) for lg in lgs]
    k_decs = [jnp.exp(lg * (c - 1.0 - pos)) for lg in lgs]
    states = [st_ref[p] for p in range(RET_HEADS // 2)]
    for j in range(su_ref.shape[0] // c):
        rs = slice(j * c, (j + 1) * c)
        for g in range(SGU_GROUPS):
            gs = slice(g * SGU_CH, (g + 1) * SGU_CH)
            s = _dot(w_sgu[g], sv_ref[rs, gs]) + bs_ref[g]
            o_ref[rs, gs] = (su_ref[rs, gs].astype(F32) * s).astype(BF16)
        for p in range(RET_HEADS // 2):
            ls = slice(p * LANES, (p + 1) * LANES)
            q_p = q_ref[rs, ls]
            k_p = k_ref[rs, ls]
            qm = (jnp.where(lo, q_p, 0.0).astype(BF16), jnp.where(lo, 0.0, q_p).astype(BF16))
            s2 = _dot_nt(jnp.concatenate(qm, axis=0), k_p.astype(BF16))
            qd = q_p * q_decs[p]
            qdm = (jnp.where(lo, qd, 0.0).astype(BF16), jnp.where(lo, 0.0, qd).astype(BF16))
            kd = (k_p * k_decs[p]).astype(BF16)
            st_b = states[p].astype(BF16)
            new = []
            for hh in range(2):
                h = 2 * p + hh
                hs = slice(h * RET_DV, (h + 1) * RET_DV)
                s_h = (s2[hh * c:(hh + 1) * c, :] * decays[h]).astype(BF16)
                v_h = rv_ref[rs, hs]
                o = _dot(s_h, v_h) + _dot_nt(qdm[hh], st_b)
                gate = _silu(rg_ref[rs, hs].astype(F32))
                o_ref[rs, SGU_W + h * RET_DV:SGU_W + (h + 1) * RET_DV] = (
                    _rms_rows(o) * gate).astype(BF16)
                new.append(_dot_tn(v_h, kd))
            states[p] = jnp.exp(lgs[p] * float(c)) * states[p] + jnp.where(lo, new[0], new[1])
    for p in range(RET_HEADS // 2):
        st_ref[p] = states[p]


def _sgu_ret(z, f, w_s, b_s, bsz, seq):
    t = z.shape[0]
    cc = RET_CHUNK
    c = min(SGU_RET_ROWS, seq)
    nc = seq // c
    return pl.pallas_call(
        _sgu_ret_kernel,
        grid=(bsz, nc),
        in_specs=[
            pl.BlockSpec((c, SGU_W), lambda b, i: (b * nc + i, 0)),
            pl.BlockSpec((c, SGU_W), lambda b, i: (b * nc + i, 1)),
            pl.BlockSpec((c, RET_V_W), lambda b, i: (b * nc + i, 2)),
            pl.BlockSpec((c, RET_V_W), lambda b, i: (b * nc + i, 3)),
            pl.BlockSpec((c, RET_QK_W), lambda b, i: (b * nc + i, 0)),
            pl.BlockSpec((c, RET_QK_W), lambda b, i: (b * nc + i, 1)),
            pl.BlockSpec((SGU_GROUPS, cc, cc), lambda b, i: (0, 0, 0)),
            pl.BlockSpec((SGU_GROUPS, cc, 1), lambda b, i: (0, 0, 0)),
        ],
        out_specs=pl.BlockSpec((c, SGU_W + RET_V_W), lambda b, i: (b * nc + i, 0)),
        out_shape=jax.ShapeDtypeStruct((t, SGU_W + RET_V_W), BF16),
        scratch_shapes=[pltpu.VMEM((2, LANES, LANES), F32)],
        compiler_params=_cp(("arbitrary", "arbitrary"), 32),
        name="sgu_retention",
    )(z, z, z, z, f, f, w_s, b_s.reshape(SGU_GROUPS, cc, 1))


def _out_proj_kernel(n_in, *refs):
    o_refs = refs[:n_in]
    w_ref, x_ref, mod_ref, rw_ref, rb_ref, xn_ref, h_ref, lg_ref = refs[n_in:]
    k_each = D_MODEL // n_in
    y = _dot(o_refs[0][...], w_ref[0:k_each, :])
    for n in range(1, n_in):
        y = y + _dot(o_refs[n][...], w_ref[n * k_each:(n + 1) * k_each, :])
    xn = x_ref[...] + mod_ref[0, 2:3, :] * y
    xn_ref[...] = xn
    h = _rms_rows(xn) * (1.0 + mod_ref[0, 4:5, :]) + mod_ref[0, 3:4, :]
    hb = h.astype(BF16)
    h_ref[...] = hb
    lg_ref[...] = _dot_nt(rw_ref[...], hb) + rb_ref[...]


def _out_proj(mixed, w_out, x2, mod, router_w, router_b, seq):
    t, d = x2.shape
    tm = min(PROJ_ROWS, seq)
    n_in = len(mixed)
    k_each = d // n_in
    rw = router_w.T.astype(BF16)
    rb = router_b.reshape(N_EXPERTS, 1)
    return pl.pallas_call(
        functools.partial(_out_proj_kernel, n_in),
        grid=(t // tm,),
        in_specs=[pl.BlockSpec((tm, k_each), lambda i: (i, 0)) for _ in mixed] + [
            pl.BlockSpec((d, d), lambda i: (0, 0)),
            pl.BlockSpec((tm, d), lambda i: (i, 0)),
            pl.BlockSpec((1, 6, d), lambda i: (i // (seq // tm), 0, 0)),
            pl.BlockSpec((N_EXPERTS, d), lambda i: (0, 0)),
            pl.BlockSpec((N_EXPERTS, 1), lambda i: (0, 0)),
        ],
        out_specs=[pl.BlockSpec((tm, d), lambda i: (i, 0)),
                   pl.BlockSpec((tm, d), lambda i: (i, 0)),
                   pl.BlockSpec((N_EXPERTS, tm), lambda i: (0, i))],
        out_shape=[jax.ShapeDtypeStruct((t, d), F32),
                   jax.ShapeDtypeStruct((t, d), BF16),
                   jax.ShapeDtypeStruct((N_EXPERTS, t), F32)],
        compiler_params=_cp(("arbitrary",), 48),
        name="out_proj",
    )(*mixed, w_out.astype(BF16), x2, mod, rw, rb)


def _route_kernel(lg_ref, idx_ref, gate_ref, tcnt_ref, toff_ref, trun_ref, cnt_ref, run_ref):
    @pl.when(pl.program_id(0) == 0)
    def _():
        run_ref[...] = jnp.zeros_like(run_ref)

    tm = lg_ref.shape[1]
    row = lax.broadcasted_iota(jnp.int32, (N_EXPERTS, tm), 0)
    neg = -jnp.inf
    l = lg_ref[...]
    vals, firsts, hots = [], [], []
    for _ in range(TOP_K):
        m = jnp.max(l, axis=0, keepdims=True)
        first = jnp.min(jnp.where(l == m, row, N_EXPERTS), axis=0, keepdims=True)
        hot = row == first
        vals.append(m)
        firsts.append(first)
        hots.append(hot)
        l = jnp.where(hot, neg, l)
    sel = hots[0] | hots[1] | hots[2] | hots[3]
    ex = [jnp.exp(v - vals[0]) for v in vals]
    denom = ex[0] + ex[1] + ex[2] + ex[3]
    r_i = lax.broadcasted_iota(jnp.int32, (tm, tm), 0)
    c_i = lax.broadcasted_iota(jnp.int32, (tm, tm), 1)
    before = (r_i < c_i).astype(BF16)
    earlier = _dot(sel.astype(BF16), before)
    tile_cnt = jnp.sum(sel.astype(F32), axis=1, keepdims=True)
    lower = jnp.zeros((N_EXPERTS, tm), F32)
    for k in range(TOP_K):
        lower = lower + (firsts[k] < row).astype(F32)
    tile_off = jnp.sum(lower, axis=1, keepdims=True)
    slot = tile_off + earlier
    row8 = lax.broadcasted_iota(jnp.int32, (2 * TOP_K, tm), 0)
    idx_out = jnp.zeros((2 * TOP_K, tm), jnp.int32)
    gate_out = jnp.zeros((2 * TOP_K, tm), F32)
    for k in range(TOP_K):
        pos_k = jnp.sum(jnp.where(hots[k], slot, 0.0), axis=0, keepdims=True).astype(jnp.int32)
        idx_out = jnp.where(row8 == k, firsts[k], idx_out)
        idx_out = jnp.where(row8 == TOP_K + k, pos_k, idx_out)
        gate_out = jnp.where(row8 == k, ex[k] / denom, gate_out)
    idx_ref[...] = idx_out
    gate_ref[...] = gate_out
    run = run_ref[:, 0:1]
    tcnt_ref[...] = jnp.broadcast_to(tile_cnt, tcnt_ref.shape)
    toff_ref[...] = jnp.broadcast_to(tile_off, toff_ref.shape)
    trun_ref[...] = jnp.broadcast_to(run, trun_ref.shape)
    total = run + tile_cnt
    run_ref[...] = jnp.broadcast_to(total, run_ref.shape)
    cnt_ref[...] = jnp.broadcast_to(total, cnt_ref.shape)


def _route(logits_t):
    t = logits_t.shape[1]
    tm = min(MOE_TILE, t)
    nt = t // tm
    per_tile = pl.BlockSpec((N_EXPERTS, LANES), lambda i: (i, 0))
    per_tile_shape = jax.ShapeDtypeStruct((nt * N_EXPERTS, LANES), F32)
    return pl.pallas_call(
        _route_kernel,
        grid=(nt,),
        in_specs=[pl.BlockSpec((N_EXPERTS, tm), lambda i: (0, i))],
        out_specs=[pl.BlockSpec((2 * TOP_K, tm), lambda i: (0, i)),
                   pl.BlockSpec((2 * TOP_K, tm), lambda i: (0, i)),
                   per_tile, per_tile, per_tile,
                   pl.BlockSpec((N_EXPERTS, LANES), lambda i: (0, 0))],
        out_shape=[jax.ShapeDtypeStruct((2 * TOP_K, t), jnp.int32),
                   jax.ShapeDtypeStruct((2 * TOP_K, t), F32),
                   per_tile_shape, per_tile_shape, per_tile_shape,
                   jax.ShapeDtypeStruct((N_EXPERTS, LANES), F32)],
        scratch_shapes=[pltpu.VMEM((N_EXPERTS, LANES), F32)],
        compiler_params=_cp(("arbitrary",), 32),
        name="route",
    )(logits_t)


def _rows(ref, start, n):
    return ref.at[pl.ds(pl.multiple_of(start * ROW_TILES, ROW_TILES), n * ROW_TILES), :]


def _run_copies(tile, cnt_ref, off_ref, dst_ref, make_copy):
    def per_expert(e, carry):
        j = tile * N_EXPERTS + e
        cnt, off, dst = cnt_ref[j], off_ref[j], dst_ref[j]
        for b in reversed(range(RUN_BITS)):
            size = 1 << b

            @pl.when((cnt & size) != 0)
            def _():
                done = lax.shift_left(lax.shift_right_logical(cnt, b + 1), b + 1)
                make_copy(off + done, dst + done, size).start()
        return carry

    lax.fori_loop(0, N_EXPERTS, per_expert, 0)


def _dispatch_kernel(fs_ref, fl_ref, cnt_ref, off_ref, dst_ref, h_ref, idx_ref, xs_ref,
                     stage, zero_buf, sems, zsem):
    i = pl.program_id(0)
    n_steps = pl.num_programs(0)
    cur = lax.rem(i, 2)
    tm = h_ref.shape[0]
    n_sorted = TOP_K * tm
    n_fill = fs_ref.shape[0]

    def fill_copies(f, wait):
        start, n = fs_ref[f], fl_ref[f]
        n_chunks = lax.shift_right_logical(n, FILL_SHIFT)
        tail = start + n_chunks * FILL_ROWS

        def chunk(j, carry):
            cp = pltpu.make_async_copy(zero_buf, _rows(xs_ref, start + j * FILL_ROWS, FILL_ROWS),
                                       zsem)
            cp.wait() if wait else cp.start()
            return carry

        def single(j, carry):
            cp = pltpu.make_async_copy(_rows(zero_buf, 0, 1), _rows(xs_ref, tail + j, 1), zsem)
            cp.wait() if wait else cp.start()
            return carry

        lax.fori_loop(0, n_chunks, chunk, 0)
        lax.fori_loop(0, n - n_chunks * FILL_ROWS, single, 0)

    @pl.when(i == 0)
    def _():
        zero_buf[...] = jnp.zeros_like(zero_buf)
        lax.fori_loop(0, n_fill, lambda f, c: (fill_copies(f, False), c)[1], 0)

    h = h_ref[...]
    pos = [idx_ref[TOP_K + k:TOP_K + k + 1, :] for k in range(TOP_K)]
    for sb in range(n_sorted // SORT_SLAB):
        j = lax.broadcasted_iota(jnp.int32, (SORT_SLAB, tm), 0) + sb * SORT_SLAB
        hit = (pos[0] == j) | (pos[1] == j) | (pos[2] == j) | (pos[3] == j)
        perm = jnp.where(hit, 1.0, 0.0).astype(BF16)
        rows = _dot(perm, h)
        for c in range(ROW_TILES):
            stage[cur, pl.ds(sb * SORT_SLAB * ROW_TILES + c, SORT_SLAB, stride=ROW_TILES), :] = (
                rows[:, c * LANES:(c + 1) * LANES])

    def make_copy(src_row, dst_row, size):
        return pltpu.make_async_copy(_rows(stage.at[cur], src_row, size),
                                     _rows(xs_ref, dst_row, size), sems.at[cur])

    _run_copies(i, cnt_ref, off_ref, dst_ref, make_copy)

    def wait_slot(slot):
        pltpu.make_async_copy(stage.at[slot], _rows(xs_ref, 0, n_sorted), sems.at[slot]).wait()

    @pl.when(i > 0)
    def _():
        wait_slot(1 - cur)

    @pl.when(i == n_steps - 1)
    def _():
        wait_slot(cur)

    @pl.when(i == 0)
    def _():
        lax.fori_loop(0, n_fill, lambda f, c: (fill_copies(f, True), c)[1], 0)


def _dispatch(h2, idx, tile_cnt, tile_off, tile_dst, fill_start, fill_len, n_rows):
    t, d = h2.shape
    tm = min(MOE_TILE, t)
    grid_spec = pltpu.PrefetchScalarGridSpec(
        num_scalar_prefetch=5,
        grid=(t // tm,),
        in_specs=[pl.BlockSpec((tm, d), lambda i, *_: (i, 0)),
                  pl.BlockSpec((2 * TOP_K, tm), lambda i, *_: (0, i))],
        out_specs=pl.BlockSpec(memory_space=pl.ANY),
        scratch_shapes=[pltpu.VMEM((2, TOP_K * tm * ROW_TILES, LANES), F32),
                        pltpu.VMEM((FILL_ROWS * ROW_TILES, LANES), F32),
                        pltpu.SemaphoreType.DMA((2,)), pltpu.SemaphoreType.DMA(())],
    )
    return pl.pallas_call(
        _dispatch_kernel,
        grid_spec=grid_spec,
        out_shape=jax.ShapeDtypeStruct((n_rows * ROW_TILES, LANES), F32),
        compiler_params=_cp(("arbitrary",), 48),
        name="dispatch",
    )(fill_start, fill_len, tile_cnt, tile_off, tile_dst, h2, idx)


def _expert_kernel(layer, be_ref, nu_ref, nx_ref, nv_ref, xs_ref, wi_hbm, bi_ref, wo_hbm, bo_ref,
                   y_ref, wi_st, wo_st, wi_b, wo_b, sems):
    i = pl.program_id(0)
    tb = xs_ref.shape[0] // ROW_TILES
    e = be_ref[i]
    fresh = jnp.logical_or(i == 0, e != be_ref[jnp.maximum(i - 1, 0)])
    used = i < nu_ref[0]

    def fetch(ex):
        return (pltpu.make_async_copy(wi_hbm.at[layer, ex], wi_st, sems.at[0]),
                pltpu.make_async_copy(wo_hbm.at[layer, ex], wo_st, sems.at[1]))

    @pl.when(i == 0)
    def _():
        for cp in fetch(e):
            cp.start()

    @pl.when(jnp.logical_and(fresh, used))
    def _():
        for cp in fetch(e):
            cp.wait()
        wi_b[...] = wi_st[...].astype(BF16)
        wo_b[...] = wo_st[...].astype(BF16)

        @pl.when(nx_ref[i] >= 0)
        def _():
            for cp in fetch(nx_ref[i]):
                cp.start()

    def compute(n):
        x = jnp.concatenate(
            [xs_ref[pl.ds(c, n, stride=ROW_TILES), :] for c in range(ROW_TILES)],
            axis=1).astype(BF16)
        y = jnp.zeros((n, D_MODEL), F32) + bo_ref[0, 0]
        half = 512
        for j in range(D_FF // half):
            a, b = j * half, (j + 1) * half
            glu = _dot(x, wi_b[:, a:b]) + bi_ref[0, 0, :, a:b]
            lin = _dot(x, wi_b[:, D_FF + a:D_FF + b]) + bi_ref[0, 0, :, D_FF + a:D_FF + b]
            glu = jnp.minimum(glu, SWIGLU_LIMIT)
            lin = jnp.clip(lin, -SWIGLU_LIMIT, SWIGLU_LIMIT)
            act = glu * jax.nn.sigmoid(SWIGLU_ALPHA * glu) * (lin + 1.0)
            y = y + _dot(act.astype(BF16), wo_b[a:b, :])
        for c in range(ROW_TILES):
            y_ref[pl.ds(c, n, stride=ROW_TILES), :] = y[:, c * LANES:(c + 1) * LANES]

    full = nv_ref[i] > tb // 2

    @pl.when(jnp.logical_and(used, full))
    def _():
        compute(tb)

    @pl.when(jnp.logical_and(used, jnp.logical_not(full)))
    def _():
        compute(tb // 2)
        y_ref[pl.ds(tb // 2 * ROW_TILES, tb // 2 * ROW_TILES), :] = jnp.zeros(
            (tb // 2 * ROW_TILES, LANES), F32)

    @pl.when(jnp.logical_not(used))
    def _():
        y_ref[...] = jnp.zeros_like(y_ref)


def _experts(xs, block_e, n_used, next_e, n_valid, layer, w_in, b_in, w_out, b_out):
    tb = EXPERT_ROWS
    n_rows = xs.shape[0] // ROW_TILES
    nb = n_rows // tb
    depth, ne, d, f2 = w_in.shape

    def row_map(i, be, nu, nx, nv):
        return (jnp.minimum(i, nu[0] - 1), 0)

    grid_spec = pltpu.PrefetchScalarGridSpec(
        num_scalar_prefetch=4,
        grid=(nb,),
        in_specs=[
            pl.BlockSpec((tb * ROW_TILES, LANES), row_map),
            pl.BlockSpec(memory_space=pl.ANY),
            pl.BlockSpec((1, 1, 1, f2), lambda i, be, nu, nx, nv: (layer, be[i], 0, 0)),
            pl.BlockSpec(memory_space=pl.ANY),
            pl.BlockSpec((1, 1, 1, d), lambda i, be, nu, nx, nv: (layer, be[i], 0, 0)),
        ],
        out_specs=pl.BlockSpec((tb * ROW_TILES, LANES), lambda i, be, nu, nx, nv: (i, 0)),
        scratch_shapes=[pltpu.VMEM((d, f2), F32), pltpu.VMEM((D_FF, d), F32),
                        pltpu.VMEM((d, f2), BF16), pltpu.VMEM((D_FF, d), BF16),
                        pltpu.SemaphoreType.DMA((2,))],
    )
    return pl.pallas_call(
        functools.partial(_expert_kernel, layer),
        grid_spec=grid_spec,
        out_shape=jax.ShapeDtypeStruct((n_rows * ROW_TILES, LANES), F32),
        compiler_params=_cp(("arbitrary",), 56),
        name="experts",
    )(block_e, n_used, next_e, n_valid, xs, w_in, b_in.reshape(depth, ne, 1, f2), w_out,
      b_out.reshape(depth, ne, 1, d))


def _combine_kernel(final, cnt_ref, off_ref, dst_ref, pos_ref, gate_ref, x_ref, mod_ref, fg_ref,
                    yb_ref, o_ref, stage, sems):
    tm = x_ref.shape[0]
    n_sorted = TOP_K * tm
    i = pl.program_id(0)
    cur = lax.rem(i, 2)

    def fetch(tile, slot):
        def make_copy(sorted_row, src_row, size):
            return pltpu.make_async_copy(_rows(yb_ref, src_row, size),
                                         _rows(stage.at[slot], sorted_row, size), sems.at[slot])

        _run_copies(tile, cnt_ref, off_ref, dst_ref, make_copy)

    @pl.when(i == 0)
    def _():
        fetch(0, 0)

    @pl.when(i + 1 < pl.num_programs(0))
    def _():
        fetch(i + 1, 1 - cur)

    pltpu.make_async_copy(_rows(yb_ref, 0, n_sorted), stage.at[cur], sems.at[cur]).wait()

    pos = [pos_ref[:, TOP_K + k:TOP_K + k + 1] for k in range(TOP_K)]
    gate = [gate_ref[:, k:k + 1] for k in range(TOP_K)]
    y = jnp.zeros((tm, D_MODEL), F32)
    for sb in range(n_sorted // SORT_SLAB):
        rows = jnp.concatenate(
            [stage[cur, pl.ds(sb * SORT_SLAB * ROW_TILES + c, SORT_SLAB, stride=ROW_TILES), :]
             for c in range(ROW_TILES)], axis=1).astype(BF16)
        j = lax.broadcasted_iota(jnp.int32, (tm, SORT_SLAB), 1) + sb * SORT_SLAB
        w = jnp.zeros((tm, SORT_SLAB), F32)
        for k in range(TOP_K):
            w = w + jnp.where(pos[k] == j, gate[k], 0.0)
        y = y + _dot(w.astype(BF16), rows)
    o = x_ref[...] + mod_ref[0, 5:6, :] * y
    if final:
        o = _rms_rows(o) * fg_ref[...]
    o_ref[...] = o


def _combine(yb, tile_cnt, tile_off, tile_dst, pos_t, gates_t, x2, mod, final_g, final, seq):
    t, d = x2.shape
    tm = min(MOE_TILE, seq)
    grid_spec = pltpu.PrefetchScalarGridSpec(
        num_scalar_prefetch=3,
        grid=(t // tm,),
        in_specs=[
            pl.BlockSpec((tm, 2 * TOP_K), lambda i, *_: (i, 0)),
            pl.BlockSpec((tm, 2 * TOP_K), lambda i, *_: (i, 0)),
            pl.BlockSpec((tm, d), lambda i, *_: (i, 0)),
            pl.BlockSpec((1, 6, d), lambda i, *_: (i // (seq // tm), 0, 0)),
            pl.BlockSpec((1, d), lambda i, *_: (0, 0)),
            pl.BlockSpec(memory_space=pl.ANY),
        ],
        out_specs=pl.BlockSpec((tm, d), lambda i, *_: (i, 0)),
        scratch_shapes=[pltpu.VMEM((2, TOP_K * tm * ROW_TILES, LANES), F32),
                        pltpu.SemaphoreType.DMA((2,))],
    )
    return pl.pallas_call(
        functools.partial(_combine_kernel, final),
        grid_spec=grid_spec,
        out_shape=jax.ShapeDtypeStruct((t, d), F32),
        compiler_params=_cp(("arbitrary",), 48),
        name="combine",
    )(tile_cnt, tile_off, tile_dst, pos_t, gates_t, x2, mod, final_g.reshape(1, d), yb)


def _moe(h2, logits, x2, mod, layer, w_in, b_in, w_out, b_out, final_g, final, seq):
    t = x2.shape[0]
    tb = EXPERT_ROWS
    idx, gates, tcnt, toff, trun, cnt = _route(logits)
    counts = cnt[:, 0].astype(jnp.int32)
    nblk = (counts + tb - 1) // tb
    blk_end = jnp.cumsum(nblk)
    pad_start = (blk_end - nblk) * tb
    n_blocks = (t * TOP_K) // tb + N_EXPERTS
    n_used = blk_end[-1:]
    experts = jnp.arange(N_EXPERTS, dtype=jnp.int32)
    last_e = jnp.max(jnp.where(nblk > 0, experts, 0))
    blk = jnp.arange(n_blocks, dtype=jnp.int32)
    block_e = jnp.minimum(
        jnp.sum((blk_end[None, :] <= blk[:, None]).astype(jnp.int32), axis=1), last_e)
    later = (experts[None, :] > block_e[:, None]) & (nblk[None, :] > 0)
    next_e = jnp.min(jnp.where(later, experts[None, :], N_EXPERTS), axis=1)
    next_e = jnp.where(next_e == N_EXPERTS, -1, next_e).astype(jnp.int32)
    tile_cnt = tcnt[:, 0].astype(jnp.int32)
    tile_off = toff[:, 0].astype(jnp.int32)
    tile_dst = (trun[:, 0].astype(jnp.int32).reshape(-1, N_EXPERTS) + pad_start[None, :]).reshape(-1)
    fill_start = jnp.concatenate([pad_start + counts, blk_end[-1:] * tb]).astype(jnp.int32)
    fill_len = jnp.concatenate([nblk * tb - counts, (n_blocks - blk_end[-1:]) * tb]).astype(jnp.int32)
    xs = _dispatch(h2, idx, tile_cnt, tile_off, tile_dst, fill_start, fill_len, n_blocks * tb)
    row_end = jnp.sum(jnp.where(block_e[:, None] == experts[None, :],
                                (pad_start + counts)[None, :], 0), axis=1)
    n_valid = jnp.clip(row_end - blk * tb, 0, tb).astype(jnp.int32)
    yb = _experts(xs, block_e, n_used.astype(jnp.int32), next_e, n_valid, layer,
                  w_in, b_in, w_out, b_out)
    return _combine(yb, tile_cnt, tile_off, tile_dst, idx.T, gates.T, x2, mod, final_g, final, seq)


def kernel(x, c, positions, w_ada, b_ada, even_w_in, gla_w_gate, gla_b_gate, gla_norm_g,
           diff_lam_q1, diff_lam_k1, diff_lam_q2, diff_lam_k2, diff_norm_g, even_w_out,
           odd_w_in, sgu_ln_g, sgu_ln_b, sgu_w, sgu_b, odd_w_out,
           router_w, router_b, expert_w_in, expert_b_in, expert_w_out, expert_b_out,
           final_norm_g):
    bsz, seq, d = x.shape
    depth = w_ada.shape[0]
    t = bsz * seq
    mods = _modulation(c, w_ada, b_ada).reshape(depth, bsz, 6, d)
    cos, sin = _rope_tables(positions)
    x2 = x.reshape(t, d)
    for layer in range(depth):
        mod = mods[layer]
        j = layer // 2
        if layer % 2 == 0:
            z, f = _proj_even(x2, mod, even_w_in[j], gla_w_gate[j], gla_b_gate[j], cos, sin, seq)
            o_gla = _gla(z, f, gla_norm_g[j], bsz, seq)
            lam_init = 0.8 - 0.6 * math.exp(-0.3 * layer)
            o_diff = _diff_attn(z, diff_lam_q1[j], diff_lam_k1[j], diff_lam_q2[j], diff_lam_k2[j],
                                diff_norm_g[j], lam_init, bsz, seq)
            mixed, w_out = (o_gla, o_diff), even_w_out[j]
        else:
            z, f = _proj_odd(x2, mod, odd_w_in[j], sgu_ln_g[j], sgu_ln_b[j], cos, sin, seq)
            mixed, w_out = (_sgu_ret(z, f, sgu_w[j], sgu_b[j], bsz, seq),), odd_w_out[j]
        x2, h2, logits = _out_proj(mixed, w_out, x2, mod, router_w[layer], router_b[layer], seq)
        x2 = _moe(h2, logits, x2, mod, layer, expert_w_in, expert_b_in, expert_w_out,
                  expert_b_out, final_norm_g, layer == depth - 1, seq)
    return x2.reshape(bsz, seq, d)
```

```python
import functools
import math

import jax
import jax.numpy as jnp
from jax import lax
from jax.experimental import pallas as pl
from jax.experimental.pallas import tpu as pltpu

F32 = jnp.float32
BF16 = jnp.bfloat16

D_MODEL = 1024
EPS = 1e-6
ROPE_THETA = 10000.0
ROPE_HALF = 32

GLA_HEADS = 4
GLA_DK = 64
GLA_DV = 128
GLA_RANK = 16
GLA_CHUNK = 64
GLA_GATE_NORMALIZER = 16.0
GLA_QK_W = GLA_HEADS * GLA_DK
GLA_V_W = GLA_HEADS * GLA_DV

DIFF_HEADS = 4
DIFF_D = 64
DIFF_DV = 128
DIFF_QK_W = DIFF_HEADS * 2 * DIFF_D
DIFF_V_W = DIFF_HEADS * DIFF_DV

SGU_GROUPS = 4
SGU_CH = 128
SGU_CHUNK = 128
SGU_W = SGU_GROUPS * SGU_CH

RET_HEADS = 4
RET_DK = 64
RET_DV = 128
RET_CHUNK = 128
RET_QK_W = RET_HEADS * RET_DK
RET_V_W = RET_HEADS * RET_DV

N_EXPERTS = 32
TOP_K = 4
D_FF = D_MODEL
SWIGLU_ALPHA = 1.702
SWIGLU_LIMIT = 7.0

LANES = 128
SUBLANES = 8
ROW_TILES = D_MODEL // LANES

PROJ_ROWS = 512
GLA_ROWS = 512
GLA_GROUP = 4
SGU_RET_ROWS = 512
ATT_Q_ROWS = 256
ATT_HEADS_PER_STEP = 2
MOE_TILE = 512
SORT_SLAB = 512
RUN_BITS = 10
RUN_LONG_BIT = 7
FILL_SHIFT = 6
FILL_ROWS = 1 << FILL_SHIFT
EXPERT_ROWS = 512
MOD_COLS = 1536

MIB = 1024 * 1024


def _cp(semantics, vmem_mib):
    return pltpu.CompilerParams(dimension_semantics=semantics, vmem_limit_bytes=vmem_mib * MIB)


def _dot(a, b):
    return jnp.dot(a, b, preferred_element_type=F32)


def _dot_nt(a, b):
    return lax.dot_general(a, b, (((1,), (1,)), ((), ())), preferred_element_type=F32)


def _dot_tn(a, b):
    return lax.dot_general(a, b, (((0,), (0,)), ((), ())), preferred_element_type=F32)


def _rms_rows(x):
    return x * lax.rsqrt(jnp.mean(x * x, axis=-1, keepdims=True) + EPS)


def _silu(x):
    return x * jax.nn.sigmoid(x)


def _rope_chunk(v, cos, sin, lo_half):
    rot = jnp.where(lo_half, -pltpu.roll(v, 96, 1), pltpu.roll(v, 32, 1))
    return v * cos + rot * sin


def _mod_kernel(c_ref, w_ref, b_ref, o_ref):
    c = c_ref[...]
    ca = _silu(c).astype(BF16)
    o_ref[0] = _dot(ca, w_ref[0].astype(BF16)) + b_ref[0]


def _modulation(c, w_ada, b_ada):
    depth, d, n = w_ada.shape
    bsz = c.shape[0]
    return pl.pallas_call(
        _mod_kernel,
        grid=(depth, n // MOD_COLS),
        in_specs=[
            pl.BlockSpec((bsz, d), lambda l, j: (0, 0)),
            pl.BlockSpec((1, d, MOD_COLS), lambda l, j: (l, 0, j)),
            pl.BlockSpec((1, 1, MOD_COLS), lambda l, j: (l, 0, j)),
        ],
        out_specs=pl.BlockSpec((1, bsz, MOD_COLS), lambda l, j: (l, 0, j)),
        out_shape=jax.ShapeDtypeStruct((depth, bsz, n), F32),
        compiler_params=_cp(("arbitrary", "arbitrary"), 40),
        name="adaln_mod",
    )(c, w_ada, b_ada.reshape(depth, 1, n))


def _rope_table_kernel(p_ref, f_ref, c_ref, s_ref):
    ang = p_ref[...].astype(F32) * f_ref[...]
    c_ref[...] = jnp.cos(ang)
    s_ref[...] = jnp.sin(ang)


def _rope_tables(positions):
    t = positions.size
    per_row = LANES // ROPE_HALF
    rows = t // per_row
    pos_d = jnp.repeat(positions.reshape(rows, per_row), ROPE_HALF, axis=1)
    inv_freq = ROPE_THETA ** (-jnp.arange(ROPE_HALF, dtype=F32) / ROPE_HALF)
    freq_d = jnp.tile(inv_freq, per_row).reshape(1, LANES)
    tr = min(512, rows)
    cos_d, sin_d = pl.pallas_call(
        _rope_table_kernel,
        grid=(rows // tr,),
        in_specs=[pl.BlockSpec((tr, LANES), lambda i: (i, 0)),
                  pl.BlockSpec((1, LANES), lambda i: (0, 0))],
        out_specs=[pl.BlockSpec((tr, LANES), lambda i: (i, 0))] * 2,
        out_shape=[jax.ShapeDtypeStruct((rows, LANES), F32)] * 2,
        compiler_params=_cp(("arbitrary",), 32),
        name="rope_tables",
    )(pos_d, freq_d)
    cos = jnp.tile(cos_d.reshape(t, ROPE_HALF), (1, per_row))
    sin = jnp.tile(sin_d.reshape(t, ROPE_HALF), (1, per_row))
    return cos, sin


def _modulated_rms(x_ref, mod_ref, which):
    x = x_ref[...]
    shift = mod_ref[0, 3 * which:3 * which + 1, :]
    scale = mod_ref[0, 3 * which + 1:3 * which + 2, :]
    return _rms_rows(x) * (1.0 + scale) + shift


EVEN_Z = (GLA_V_W, GLA_V_W, DIFF_QK_W, DIFF_QK_W, DIFF_V_W)
EVEN_Z_W = sum(EVEN_Z)
EVEN_F_W = 3 * GLA_QK_W
EVEN_W_COLS = EVEN_Z_W + 2 * GLA_QK_W + LANES


def _proj_even_kernel(x_ref, mod_ref, w_ref, wg_ref, bg_ref, cos_ref, sin_ref, z_ref, f_ref):
    hb = _modulated_rms(x_ref, mod_ref, 0).astype(BF16)

    def sec(a, b):
        return _dot(hb, w_ref[:, a:b])

    for a, b in ((0, 512), (512, 1024), (2048, 2560)):
        z_ref[:, a:b] = sec(a, b).astype(BF16)
    cos = cos_ref[...]
    sin = sin_ref[...]
    lo_half = (lax.broadcasted_iota(jnp.int32, cos.shape, 1) % 64) < ROPE_HALF
    q_scale = DIFF_D ** -0.5 * math.log2(math.e)
    for a, scl in ((1024, q_scale), (1536, 1.0)):
        full = sec(a, a + DIFF_QK_W)
        for c in range(DIFF_QK_W // LANES):
            v = full[:, c * LANES:(c + 1) * LANES]
            z_ref[:, a + c * LANES:a + (c + 1) * LANES] = (
                _rope_chunk(v, cos, sin, lo_half) * scl).astype(BF16)
    f_ref[:, 0:256] = sec(2560, 2816) * (GLA_DK ** -0.5)
    f_ref[:, 256:512] = sec(2816, 3072)
    gr = sec(3072, 3200).astype(BF16)
    pre = _dot(gr, wg_ref[...]) + bg_ref[...]
    log_sig = jnp.minimum(pre, 0.0) - jnp.log1p(jnp.exp(-jnp.abs(pre)))
    f_ref[:, 512:768] = log_sig / GLA_GATE_NORMALIZER


def _proj_even(x2, mod, w_in, w_gate, b_gate, cos, sin, seq):
    t, d = x2.shape
    tm = min(PROJ_ROWS, seq)
    gq, gk, gv, gr, gg, dq, dk, dv = jnp.split(
        w_in, [256, 512, 1024, 1040, 1552, 2064, 2576], axis=1)
    gr_pad = jnp.pad(gr, ((0, 0), (0, LANES - GLA_RANK)))
    w_cat = jnp.concatenate([gv, gg, dq, dk, dv, gq, gk, gr_pad], axis=1).astype(BF16)
    wg_pad = jnp.pad(w_gate, ((0, LANES - GLA_RANK), (0, 0))).astype(BF16)
    return pl.pallas_call(
        _proj_even_kernel,
        grid=(t // tm,),
        in_specs=[
            pl.BlockSpec((tm, d), lambda i: (i, 0)),
            pl.BlockSpec((1, 6, d), lambda i: (i // (seq // tm), 0, 0)),
            pl.BlockSpec((d, EVEN_W_COLS), lambda i: (0, 0)),
            pl.BlockSpec((LANES, GLA_QK_W), lambda i: (0, 0)),
            pl.BlockSpec((1, GLA_QK_W), lambda i: (0, 0)),
            pl.BlockSpec((tm, LANES), lambda i: (i, 0)),
            pl.BlockSpec((tm, LANES), lambda i: (i, 0)),
        ],
        out_specs=[pl.BlockSpec((tm, EVEN_Z_W), lambda i: (i, 0)),
                   pl.BlockSpec((tm, EVEN_F_W), lambda i: (i, 0))],
        out_shape=[jax.ShapeDtypeStruct((t, EVEN_Z_W), BF16),
                   jax.ShapeDtypeStruct((t, EVEN_F_W), F32)],
        compiler_params=_cp(("arbitrary",), 48),
        name="proj_even",
    )(x2, mod, w_cat, wg_pad, b_gate.reshape(1, GLA_QK_W), cos, sin)


ODD_Z_W = 2 * SGU_W + 2 * RET_V_W
ODD_F_W = 2 * RET_QK_W
ODD_W_COLS = ODD_Z_W + ODD_F_W


def _gelu_exact(x):
    return 0.5 * x * (1.0 + lax.erf(x * (2.0 ** -0.5)))


def _proj_odd_kernel(x_ref, mod_ref, w_ref, lng_ref, lnb_ref, cos_ref, sin_ref, z_ref, f_ref):
    hb = _modulated_rms(x_ref, mod_ref, 0).astype(BF16)

    def sec(a, b):
        return _dot(hb, w_ref[:, a:b])

    z_ref[:, 0:512] = _gelu_exact(sec(0, 512)).astype(BF16)
    sv = _gelu_exact(sec(512, 1024))
    mu = jnp.mean(sv, axis=-1, keepdims=True)
    cen = sv - mu
    var = jnp.mean(cen * cen, axis=-1, keepdims=True)
    z_ref[:, 512:1024] = (cen * lax.rsqrt(var + EPS) * lng_ref[...] + lnb_ref[...]).astype(BF16)
    for a, b in ((1024, 1536), (1536, 2048)):
        z_ref[:, a:b] = sec(a, b).astype(BF16)
    cos = cos_ref[...]
    sin = sin_ref[...]
    lo_half = (lax.broadcasted_iota(jnp.int32, cos.shape, 1) % 64) < ROPE_HALF
    k_scale = RET_DK ** -0.5
    for a, scl in ((0, 1.0), (256, k_scale)):
        full = sec(ODD_Z_W + a, ODD_Z_W + a + RET_QK_W)
        for c in range(RET_QK_W // LANES):
            v = full[:, c * LANES:(c + 1) * LANES]
            f_ref[:, a + c * LANES:a + (c + 1) * LANES] = _rope_chunk(v, cos, sin, lo_half) * scl


def _proj_odd(x2, mod, w_in, ln_g, ln_b, cos, sin, seq):
    t, d = x2.shape
    tm = min(PROJ_ROWS, seq)
    su, sv, rq, rk, rv, rg = jnp.split(w_in, [512, 1024, 1280, 1536, 2048], axis=1)
    w_cat = jnp.concatenate([su, sv, rv, rg, rq, rk], axis=1).astype(BF16)
    return pl.pallas_call(
        _proj_odd_kernel,
        grid=(t // tm,),
        in_specs=[
            pl.BlockSpec((tm, d), lambda i: (i, 0)),
            pl.BlockSpec((1, 6, d), lambda i: (i // (seq // tm), 0, 0)),
            pl.BlockSpec((d, ODD_W_COLS), lambda i: (0, 0)),
            pl.BlockSpec((1, SGU_W), lambda i: (0, 0)),
            pl.BlockSpec((1, SGU_W), lambda i: (0, 0)),
            pl.BlockSpec((tm, LANES), lambda i: (i, 0)),
            pl.BlockSpec((tm, LANES), lambda i: (i, 0)),
        ],
        out_specs=[pl.BlockSpec((tm, ODD_Z_W), lambda i: (i, 0)),
                   pl.BlockSpec((tm, ODD_F_W), lambda i: (i, 0))],
        out_shape=[jax.ShapeDtypeStruct((t, ODD_Z_W), BF16),
                   jax.ShapeDtypeStruct((t, ODD_F_W), F32)],
        compiler_params=_cp(("arbitrary",), 48),
        name="proj_odd",
    )(x2, mod, w_cat, ln_g.reshape(1, SGU_W), ln_b.reshape(1, SGU_W), cos, sin)


def _gla_kernel(q_ref, k_ref, la_ref, v_ref, gg_ref, g_ref, o_ref, st_ref):
    @pl.when(pl.program_id(1) == 0)
    def _():
        st_ref[...] = jnp.zeros_like(st_ref)

    c = GLA_CHUNK
    sc = GLA_GROUP * c
    rows = q_ref.shape[0]
    r_i = lax.broadcasted_iota(jnp.int32, (sc, sc), 0)
    c_i = lax.broadcasted_iota(jnp.int32, (sc, sc), 1)
    block_causal = (r_i // c == c_i // c) & (r_i >= c_i)
    tril = block_causal.astype(BF16)
    causal2 = jnp.concatenate([block_causal, block_causal], axis=0)
    lo = lax.broadcasted_iota(jnp.int32, (sc, LANES), 1) < GLA_DK
    lane_s = lax.broadcasted_iota(jnp.int32, (LANES, LANES), 1) < GLA_DK
    g_row = g_ref[...]
    states = [st_ref[0], st_ref[1]]
    for j in range(rows // sc):
        rs = slice(j * sc, (j + 1) * sc)
        la = la_ref[rs, :]
        la_hi = la.astype(BF16)
        la_lo = (la - la_hi.astype(F32)).astype(BF16)
        b = _dot(tril, la_hi) + _dot(tril, la_lo)
        b_last = [b[(g + 1) * c - 1:(g + 1) * c, :] for g in range(GLA_GROUP)]
        b_last_rows = jnp.concatenate(
            [jnp.broadcast_to(bl, (c, bl.shape[1])) for bl in b_last], axis=0)
        qd = q_ref[rs, :] * jnp.exp(b)
        kk = k_ref[rs, :]
        ki = kk * jnp.exp(-b)
        kd = kk * jnp.exp(b_last_rows - b)
        for p in range(GLA_HEADS // 2):
            ls = slice(p * LANES, (p + 1) * LANES)
            qd_p = qd[:, ls]
            qm = (jnp.where(lo, qd_p, 0.0).astype(BF16), jnp.where(lo, 0.0, qd_p).astype(BF16))
            s2 = _dot_nt(jnp.concatenate(qm, axis=0), ki[:, ls].astype(BF16))
            s2 = jnp.where(causal2, s2, 0.0).astype(BF16)
            kd_p = kd[:, ls].astype(BF16)
            heads = (2 * p, 2 * p + 1)
            v = [v_ref[rs, h * GLA_DV:(h + 1) * GLA_DV] for h in heads]
            inter = ([], [])
            for g in range(GLA_GROUP):
                gs = slice(g * c, (g + 1) * c)
                st_b = states[p].astype(BF16)
                for hh in range(2):
                    inter[hh].append(_dot_nt(qm[hh][gs, :], st_b))
                new = [_dot_tn(v[hh][gs, :], kd_p[gs, :]) for hh in range(2)]
                states[p] = (jnp.exp(b_last[g][:, ls]) * states[p]
                             + jnp.where(lane_s, new[0], new[1]))
            for hh in range(2):
                hs = slice(heads[hh] * GLA_DV, (heads[hh] + 1) * GLA_DV)
                o = _dot(s2[hh * sc:(hh + 1) * sc, :], v[hh]) + jnp.concatenate(inter[hh], axis=0)
                o = _rms_rows(o) * g_row
                gate = _silu(gg_ref[rs, hs].astype(F32))
                o_ref[rs, hs] = (o * gate).astype(BF16)
    st_ref[0] = states[0]
    st_ref[1] = states[1]


def _gla(z, f, norm_g, bsz, seq):
    t = z.shape[0]
    tc = min(GLA_ROWS, seq)
    nc = seq // tc
    return pl.pallas_call(
        _gla_kernel,
        grid=(bsz, nc),
        in_specs=[
            pl.BlockSpec((tc, GLA_QK_W), lambda b, i: (b * nc + i, 0)),
            pl.BlockSpec((tc, GLA_QK_W), lambda b, i: (b * nc + i, 1)),
            pl.BlockSpec((tc, GLA_QK_W), lambda b, i: (b * nc + i, 2)),
            pl.BlockSpec((tc, GLA_V_W), lambda b, i: (b * nc + i, 0)),
            pl.BlockSpec((tc, GLA_V_W), lambda b, i: (b * nc + i, 1)),
            pl.BlockSpec((1, GLA_DV), lambda b, i: (0, 0)),
        ],
        out_specs=pl.BlockSpec((tc, GLA_V_W), lambda b, i: (b * nc + i, 0)),
        out_shape=jax.ShapeDtypeStruct((t, GLA_V_W), BF16),
        scratch_shapes=[pltpu.VMEM((2, LANES, LANES), F32)],
        compiler_params=_cp(("arbitrary", "arbitrary"), 32),
        name="gla",
    )(f, f, f, z, z, norm_g.reshape(1, GLA_DV))


def _diff_attn_kernel(lam_init, q_ref, k_ref, v_ref, lq1_ref, lk1_ref, lq2_ref, lk2_ref, g_ref,
                      o_ref, vt_ref, s_ref, m_ref, l_ref, acc_ref):
    tq = q_ref.shape[0]
    seq = k_ref.shape[0]
    n_heads = q_ref.shape[1] // LANES
    qi = pl.program_id(2)

    @pl.when(qi == 0)
    def _():
        for hh in range(n_heads):
            for cb in range(seq // tq):
                blk = v_ref[cb * tq:(cb + 1) * tq, hh * LANES:(hh + 1) * LANES]
                vt_ref[hh, :, cb * tq:(cb + 1) * tq] = blk.astype(F32).T.astype(BF16)

    lo = lax.broadcasted_iota(jnp.int32, (tq, LANES), 1) < DIFF_D
    qqs = []
    for hh in range(n_heads):
        q = q_ref[:, hh * LANES:(hh + 1) * LANES]
        zero = jnp.zeros_like(q)
        qqs.append(jnp.concatenate([jnp.where(lo, q, zero), jnp.where(lo, zero, q)], axis=0))

    m_ref[...] = jnp.full(m_ref.shape, -jnp.inf, F32)
    l_ref[...] = jnp.zeros_like(l_ref)
    acc_ref[...] = jnp.zeros_like(acc_ref)

    def scores(j, slot):
        start = pl.multiple_of(j * tq, tq)
        for hh in range(n_heads):
            kj = k_ref[pl.ds(start, tq), hh * LANES:(hh + 1) * LANES]
            s_ref[slot, hh] = _dot_nt(kj, qqs[hh])

    def update(j, slot, masked):
        start = pl.multiple_of(j * tq, tq)
        for hh in range(n_heads):
            vtj = vt_ref[hh, :, pl.ds(start, tq)]
            alpha, p = [], []
            for c in range(2 * tq // LANES):
                cs = slice(c * LANES, (c + 1) * LANES)
                s_c = s_ref[slot, hh, :, cs]
                if masked:
                    kv = lax.broadcasted_iota(jnp.int32, s_c.shape, 0)
                    q_pos = lax.broadcasted_iota(jnp.int32, s_c.shape, 1) + (c * LANES) % tq
                    s_c = jnp.where(kv <= q_pos, s_c, -jnp.inf)
                m_old = m_ref[hh, :, cs]
                m_c = jnp.maximum(m_old, jnp.max(s_c, axis=0, keepdims=True))
                a_c = jnp.exp2(m_old - m_c)
                p_c = jnp.exp2(s_c - m_c)
                m_ref[hh, :, cs] = m_c
                l_ref[hh, :, cs] = a_c * l_ref[hh, :, cs] + jnp.sum(p_c, axis=0, keepdims=True)
                alpha.append(a_c)
                p.append(p_c.astype(BF16))
            acc_ref[hh] = (jnp.concatenate(alpha, axis=1) * acc_ref[hh]
                           + _dot(vtj, jnp.concatenate(p, axis=1)))

    scores(0, 0)

    def body(i, carry):
        j = 2 * i
        scores(j + 1, 1)
        update(j, 0, False)
        scores(j + 2, 0)
        update(j + 1, 1, False)
        return carry

    n_pairs = qi // 2
    lax.fori_loop(0, n_pairs, body, 0)
    j0 = 2 * n_pairs

    @pl.when(qi % 2 == 1)
    def _():
        scores(j0 + 1, 1)
        update(j0, 0, False)
        update(j0 + 1, 1, True)

    @pl.when(qi % 2 == 0)
    def _():
        update(j0, 0, True)

    lam = (jnp.exp(jnp.sum(lq1_ref[...] * lk1_ref[...], axis=-1, keepdims=True))
           - jnp.exp(jnp.sum(lq2_ref[...] * lk2_ref[...], axis=-1, keepdims=True)) + lam_init)
    for hh in range(n_heads):
        l = l_ref[hh]
        acc = acc_ref[hh]
        o12 = acc / l
        o = o12[:, :tq] - lam * o12[:, tq:]
        o = o * lax.rsqrt(jnp.mean(o * o, axis=0, keepdims=True) + EPS)
        o = o * g_ref[...] * (1.0 - lam_init)
        o_ref[:, hh * DIFF_DV:(hh + 1) * DIFF_DV] = o.T.astype(BF16)


def _diff_attn(z, lq1, lk1, lq2, lk2, norm_g, lam_init, bsz, seq):
    t = z.shape[0]
    tq = min(ATT_Q_ROWS, seq)
    nq = seq // tq
    hw = ATT_HEADS_PER_STEP * LANES
    qb, kb, vb = 1024 // hw, 1536 // hw, 2048 // hw
    small = pl.BlockSpec((1, DIFF_D), lambda b, h, i: (0, 0))
    return pl.pallas_call(
        functools.partial(_diff_attn_kernel, lam_init),
        grid=(bsz, DIFF_HEADS // ATT_HEADS_PER_STEP, nq),
        in_specs=[
            pl.BlockSpec((tq, hw), lambda b, h, i: (b * nq + i, qb + h)),
            pl.BlockSpec((seq, hw), lambda b, h, i: (b, kb + h)),
            pl.BlockSpec((seq, hw), lambda b, h, i: (b, vb + h)),
            small, small, small, small,
            pl.BlockSpec((DIFF_DV, 1), lambda b, h, i: (0, 0)),
        ],
        out_specs=pl.BlockSpec((tq, hw), lambda b, h, i: (b * nq + i, h)),
        out_shape=jax.ShapeDtypeStruct((t, DIFF_V_W), BF16),
        scratch_shapes=[pltpu.VMEM((ATT_HEADS_PER_STEP, DIFF_DV, seq), BF16),
                        pltpu.VMEM((2, ATT_HEADS_PER_STEP, tq, 2 * tq), F32),
                        pltpu.VMEM((ATT_HEADS_PER_STEP, 1, 2 * tq), F32),
                        pltpu.VMEM((ATT_HEADS_PER_STEP, 1, 2 * tq), F32),
                        pltpu.VMEM((ATT_HEADS_PER_STEP, DIFF_DV, 2 * tq), F32)],
        compiler_params=_cp(("arbitrary", "arbitrary", "arbitrary"), 32),
        name="diff_attn",
    )(z, z, z, lq1.reshape(1, DIFF_D), lk1.reshape(1, DIFF_D), lq2.reshape(1, DIFF_D),
      lk2.reshape(1, DIFF_D), norm_g.reshape(DIFF_DV, 1))


def _sgu_ret_kernel(su_ref, sv_ref, rv_ref, rg_ref, q_ref, k_ref, ws_ref, bs_ref, o_ref, st_ref):
    @pl.when(pl.program_id(1) == 0)
    def _():
        st_ref[...] = jnp.zeros_like(st_ref)

    c = RET_CHUNK
    row = lax.broadcasted_iota(jnp.int32, (c, c), 0)
    col = lax.broadcasted_iota(jnp.int32, (c, c), 1)
    causal = row >= col
    log_g = [math.log(1.0 - 2.0 ** (-5.0 - h)) for h in range(RET_HEADS)]
    lo = col < RET_DK
    rel = (row - col).astype(F32)
    pos = row.astype(F32)
    w_sgu = [jnp.where(causal, ws_ref[g], 0.0).astype(BF16) for g in range(SGU_GROUPS)]
    decays = [jnp.where(causal, jnp.exp(log_g[h] * jnp.maximum(rel, 0.0)), 0.0)
              for h in range(RET_HEADS)]
    lgs = [jnp.where(lo, log_g[2 * p], log_g[2 * p + 1]) for p in range(RET_HEADS // 2)]
    q_decs = [jnp.exp(lg * (pos + 1.0)) for lg in lgs]
    k_decs = [jnp.exp(lg * (c - 1.0 - pos)) for lg in lgs]
    states = [st_ref[p] for p in range(RET_HEADS // 2)]
    for j in range(su_ref.shape[0] // c):
        rs = slice(j * c, (j + 1) * c)
        for g in range(SGU_GROUPS):
            gs = slice(g * SGU_CH, (g + 1) * SGU_CH)
            s = _dot(w_sgu[g], sv_ref[rs, gs]) + bs_ref[g]
            o_ref[rs, gs] = (su_ref[rs, gs].astype(F32) * s).astype(BF16)
        for p in range(RET_HEADS // 2):
            ls = slice(p * LANES, (p + 1) * LANES)
            q_p = q_ref[rs, ls]
            k_p = k_ref[rs, ls]
            qm = (jnp.where(lo, q_p, 0.0).astype(BF16), jnp.where(lo, 0.0, q_p).astype(BF16))
            s2 = _dot_nt(jnp.concatenate(qm, axis=0), k_p.astype(BF16))
            qd = q_p * q_decs[p]
            qdm = (jnp.where(lo, qd, 0.0).astype(BF16), jnp.where(lo, 0.0, qd).astype(BF16))
            kd = (k_p * k_decs[p]).astype(BF16)
            st_b = states[p].astype(BF16)
            new = []
            for hh in range(2):
                h = 2 * p + hh
                hs = slice(h * RET_DV, (h + 1) * RET_DV)
                s_h = (s2[hh * c:(hh + 1) * c, :] * decays[h]).astype(BF16)
                v_h = rv_ref[rs, hs]
                o = _dot(s_h, v_h) + _dot_nt(qdm[hh], st_b)
                gate = _silu(rg_ref[rs, hs].astype(F32))
                o_ref[rs, SGU_W + h * RET_DV:SGU_W + (h + 1) * RET_DV] = (
                    _rms_rows(o) * gate).astype(BF16)
                new.append(_dot_tn(v_h, kd))
            states[p] = jnp.exp(lgs[p] * float(c)) * states[p] + jnp.where(lo, new[0], new[1])
    for p in range(RET_HEADS // 2):
        st_ref[p] = states[p]


def _sgu_ret(z, f, w_s, b_s, bsz, seq):
    t = z.shape[0]
    cc = RET_CHUNK
    c = min(SGU_RET_ROWS, seq)
    nc = seq // c
    return pl.pallas_call(
        _sgu_ret_kernel,
        grid=(bsz, nc),
        in_specs=[
            pl.BlockSpec((c, SGU_W), lambda b, i: (b * nc + i, 0)),
            pl.BlockSpec((c, SGU_W), lambda b, i: (b * nc + i, 1)),
            pl.BlockSpec((c, RET_V_W), lambda b, i: (b * nc + i, 2)),
            pl.BlockSpec((c, RET_V_W), lambda b, i: (b * nc + i, 3)),
            pl.BlockSpec((c, RET_QK_W), lambda b, i: (b * nc + i, 0)),
            pl.BlockSpec((c, RET_QK_W), lambda b, i: (b * nc + i, 1)),
            pl.BlockSpec((SGU_GROUPS, cc, cc), lambda b, i: (0, 0, 0)),
            pl.BlockSpec((SGU_GROUPS, cc, 1), lambda b, i: (0, 0, 0)),
        ],
        out_specs=pl.BlockSpec((c, SGU_W + RET_V_W), lambda b, i: (b * nc + i, 0)),
        out_shape=jax.ShapeDtypeStruct((t, SGU_W + RET_V_W), BF16),
        scratch_shapes=[pltpu.VMEM((2, LANES, LANES), F32)],
        compiler_params=_cp(("arbitrary", "arbitrary"), 32),
        name="sgu_retention",
    )(z, z, z, z, f, f, w_s, b_s.reshape(SGU_GROUPS, cc, 1))


def _out_proj_kernel(n_in, *refs):
    o_refs = refs[:n_in]
    w_ref, x_ref, mod_ref, rw_ref, rb_ref, xn_ref, h_ref, lg_ref = refs[n_in:]
    k_each = D_MODEL // n_in
    y = _dot(o_refs[0][...], w_ref[0:k_each, :])
    for n in range(1, n_in):
        y = y + _dot(o_refs[n][...], w_ref[n * k_each:(n + 1) * k_each, :])
    xn = x_ref[...] + mod_ref[0, 2:3, :] * y
    xn_ref[...] = xn
    h = _rms_rows(xn) * (1.0 + mod_ref[0, 4:5, :]) + mod_ref[0, 3:4, :]
    hb = h.astype(BF16)
    h_ref[...] = hb
    lg_ref[...] = _dot_nt(rw_ref[...], hb) + rb_ref[...]


def _out_proj(mixed, w_out, x2, mod, router_w, router_b, seq):
    t, d = x2.shape
    tm = min(PROJ_ROWS, seq)
    n_in = len(mixed)
    k_each = d // n_in
    rw = router_w.T.astype(BF16)
    rb = router_b.reshape(N_EXPERTS, 1)
    return pl.pallas_call(
        functools.partial(_out_proj_kernel, n_in),
        grid=(t // tm,),
        in_specs=[pl.BlockSpec((tm, k_each), lambda i: (i, 0)) for _ in mixed] + [
            pl.BlockSpec((d, d), lambda i: (0, 0)),
            pl.BlockSpec((tm, d), lambda i: (i, 0)),
            pl.BlockSpec((1, 6, d), lambda i: (i // (seq // tm), 0, 0)),
            pl.BlockSpec((N_EXPERTS, d), lambda i: (0, 0)),
            pl.BlockSpec((N_EXPERTS, 1), lambda i: (0, 0)),
        ],
        out_specs=[pl.BlockSpec((tm, d), lambda i: (i, 0)),
                   pl.BlockSpec((tm, d), lambda i: (i, 0)),
                   pl.BlockSpec((N_EXPERTS, tm), lambda i: (0, i))],
        out_shape=[jax.ShapeDtypeStruct((t, d), F32),
                   jax.ShapeDtypeStruct((t, d), BF16),
                   jax.ShapeDtypeStruct((N_EXPERTS, t), F32)],
        compiler_params=_cp(("arbitrary",), 48),
        name="out_proj",
    )(*mixed, w_out.astype(BF16), x2, mod, rw, rb)


def _route_kernel(lg_ref, idx_ref, gate_ref, tcnt_ref, toff_ref, trun_ref, cnt_ref, run_ref):
    @pl.when(pl.program_id(0) == 0)
    def _():
        run_ref[...] = jnp.zeros_like(run_ref)

    tm = lg_ref.shape[1]
    row = lax.broadcasted_iota(jnp.int32, (N_EXPERTS, tm), 0)
    neg = -jnp.inf
    l = lg_ref[...]
    vals, firsts, hots = [], [], []
    for _ in range(TOP_K):
        m = jnp.max(l, axis=0, keepdims=True)
        first = jnp.min(jnp.where(l == m, row, N_EXPERTS), axis=0, keepdims=True)
        hot = row == first
        vals.append(m)
        firsts.append(first)
        hots.append(hot)
        l = jnp.where(hot, neg, l)
    sel = hots[0] | hots[1] | hots[2] | hots[3]
    ex = [jnp.exp(v - vals[0]) for v in vals]
    denom = ex[0] + ex[1] + ex[2] + ex[3]
    r_i = lax.broadcasted_iota(jnp.int32, (tm, tm), 0)
    c_i = lax.broadcasted_iota(jnp.int32, (tm, tm), 1)
    before = (r_i < c_i).astype(BF16)
    earlier = _dot(sel.astype(BF16), before)
    tile_cnt = jnp.sum(sel.astype(F32), axis=1, keepdims=True)
    lower = jnp.zeros((N_EXPERTS, tm), F32)
    for k in range(TOP_K):
        lower = lower + (firsts[k] < row).astype(F32)
    tile_off = jnp.sum(lower, axis=1, keepdims=True)
    slot = tile_off + earlier
    row8 = lax.broadcasted_iota(jnp.int32, (2 * TOP_K, tm), 0)
    idx_out = jnp.zeros((2 * TOP_K, tm), jnp.int32)
    gate_out = jnp.zeros((2 * TOP_K, tm), F32)
    for k in range(TOP_K):
        pos_k = jnp.sum(jnp.where(hots[k], slot, 0.0), axis=0, keepdims=True).astype(jnp.int32)
        idx_out = jnp.where(row8 == k, firsts[k], idx_out)
        idx_out = jnp.where(row8 == TOP_K + k, pos_k, idx_out)
        gate_out = jnp.where(row8 == k, ex[k] / denom, gate_out)
    idx_ref[...] = idx_out
    gate_ref[...] = gate_out
    run = run_ref[:, 0:1]
    tcnt_ref[...] = jnp.broadcast_to(tile_cnt, tcnt_ref.shape)
    toff_ref[...] = jnp.broadcast_to(tile_off, toff_ref.shape)
    trun_ref[...] = jnp.broadcast_to(run, trun_ref.shape)
    total = run + tile_cnt
    run_ref[...] = jnp.broadcast_to(total, run_ref.shape)
    cnt_ref[...] = jnp.broadcast_to(total, cnt_ref.shape)


def _route(logits_t):
    t = logits_t.shape[1]
    tm = min(MOE_TILE, t)
    nt = t // tm
    per_tile = pl.BlockSpec((N_EXPERTS, LANES), lambda i: (i, 0))
    per_tile_shape = jax.ShapeDtypeStruct((nt * N_EXPERTS, LANES), F32)
    return pl.pallas_call(
        _route_kernel,
        grid=(nt,),
        in_specs=[pl.BlockSpec((N_EXPERTS, tm), lambda i: (0, i))],
        out_specs=[pl.BlockSpec((2 * TOP_K, tm), lambda i: (0, i)),
                   pl.BlockSpec((2 * TOP_K, tm), lambda i: (0, i)),
                   per_tile, per_tile, per_tile,
                   pl.BlockSpec((N_EXPERTS, LANES), lambda i: (0, 0))],
        out_shape=[jax.ShapeDtypeStruct((2 * TOP_K, t), jnp.int32),
                   jax.ShapeDtypeStruct((2 * TOP_K, t), F32),
                   per_tile_shape, per_tile_shape, per_tile_shape,
                   jax.ShapeDtypeStruct((N_EXPERTS, LANES), F32)],
        scratch_shapes=[pltpu.VMEM((N_EXPERTS, LANES), F32)],
        compiler_params=_cp(("arbitrary",), 32),
        name="route",
    )(logits_t)


def _rows(ref, start, n):
    return ref.at[pl.ds(pl.multiple_of(start * ROW_TILES, ROW_TILES), n * ROW_TILES), :]


def _run_copies(tile, cnt_ref, off_ref, dst_ref, make_copy):
    def per_expert(e, carry):
        j = tile * N_EXPERTS + e
        cnt, off, dst = cnt_ref[j], off_ref[j], dst_ref[j]

        def piece(b):
            size = 1 << b

            @pl.when((cnt & size) != 0)
            def _():
                done = lax.shift_left(lax.shift_right_logical(cnt, b + 1), b + 1)
                make_copy(off + done, dst + done, size).start()

        @pl.when(cnt >= (1 << RUN_LONG_BIT))
        def _():
            for b in range(RUN_LONG_BIT, RUN_BITS):
                piece(b)

        for b in range(RUN_LONG_BIT):
            piece(b)
        return carry

    lax.fori_loop(0, N_EXPERTS, per_expert, 0)


def _dispatch_kernel(fs_ref, fl_ref, cnt_ref, off_ref, dst_ref, h_ref, idx_ref, xs_ref,
                     stage, zero_buf, sems, zsem):
    i = pl.program_id(0)
    n_steps = pl.num_programs(0)
    cur = lax.rem(i, 2)
    tm = h_ref.shape[0]
    n_sorted = TOP_K * tm
    n_fill = fs_ref.shape[0]

    def fill_copies(f, wait):
        start, n = fs_ref[f], fl_ref[f]
        n_chunks = lax.shift_right_logical(n, FILL_SHIFT)
        tail = start + n_chunks * FILL_ROWS

        def chunk(j, carry):
            cp = pltpu.make_async_copy(zero_buf, _rows(xs_ref, start + j * FILL_ROWS, FILL_ROWS),
                                       zsem)
            cp.wait() if wait else cp.start()
            return carry

        def single(j, carry):
            cp = pltpu.make_async_copy(_rows(zero_buf, 0, 1), _rows(xs_ref, tail + j, 1), zsem)
            cp.wait() if wait else cp.start()
            return carry

        lax.fori_loop(0, n_chunks, chunk, 0)
        lax.fori_loop(0, n - n_chunks * FILL_ROWS, single, 0)

    @pl.when(i == 0)
    def _():
        zero_buf[...] = jnp.zeros_like(zero_buf)
        lax.fori_loop(0, n_fill, lambda f, c: (fill_copies(f, False), c)[1], 0)

    h = h_ref[...]
    pos = [idx_ref[TOP_K + k:TOP_K + k + 1, :] for k in range(TOP_K)]
    for sb in range(n_sorted // SORT_SLAB):
        j = lax.broadcasted_iota(jnp.int32, (SORT_SLAB, tm), 0) + sb * SORT_SLAB
        hit = (pos[0] == j) | (pos[1] == j) | (pos[2] == j) | (pos[3] == j)
        perm = jnp.where(hit, 1.0, 0.0).astype(BF16)
        rows = _dot(perm, h)
        for c in range(ROW_TILES):
            stage[cur, pl.ds(sb * SORT_SLAB * ROW_TILES + c, SORT_SLAB, stride=ROW_TILES), :] = (
                rows[:, c * LANES:(c + 1) * LANES])

    def make_copy(src_row, dst_row, size):
        return pltpu.make_async_copy(_rows(stage.at[cur], src_row, size),
                                     _rows(xs_ref, dst_row, size), sems.at[cur])

    _run_copies(i, cnt_ref, off_ref, dst_ref, make_copy)

    def wait_slot(slot):
        pltpu.make_async_copy(stage.at[slot], _rows(xs_ref, 0, n_sorted), sems.at[slot]).wait()

    @pl.when(i > 0)
    def _():
        wait_slot(1 - cur)

    @pl.when(i == n_steps - 1)
    def _():
        wait_slot(cur)

    @pl.when(i == 0)
    def _():
        lax.fori_loop(0, n_fill, lambda f, c: (fill_copies(f, True), c)[1], 0)


def _dispatch(h2, idx, tile_cnt, tile_off, tile_dst, fill_start, fill_len, n_rows):
    t, d = h2.shape
    tm = min(MOE_TILE, t)
    grid_spec = pltpu.PrefetchScalarGridSpec(
        num_scalar_prefetch=5,
        grid=(t // tm,),
        in_specs=[pl.BlockSpec((tm, d), lambda i, *_: (i, 0)),
                  pl.BlockSpec((2 * TOP_K, tm), lambda i, *_: (0, i))],
        out_specs=pl.BlockSpec(memory_space=pl.ANY),
        scratch_shapes=[pltpu.VMEM((2, TOP_K * tm * ROW_TILES, LANES), F32),
                        pltpu.VMEM((FILL_ROWS * ROW_TILES, LANES), F32),
                        pltpu.SemaphoreType.DMA((2,)), pltpu.SemaphoreType.DMA(())],
    )
    return pl.pallas_call(
        _dispatch_kernel,
        grid_spec=grid_spec,
        out_shape=jax.ShapeDtypeStruct((n_rows * ROW_TILES, LANES), F32),
        compiler_params=_cp(("arbitrary",), 48),
        name="dispatch",
    )(fill_start, fill_len, tile_cnt, tile_off, tile_dst, h2, idx)


def _expert_kernel(layer, be_ref, nu_ref, nx_ref, nv_ref, xs_ref, wi_hbm, bi_ref, wo_hbm, bo_ref,
                   y_ref, wi_st, wo_st, wi_b, wo_b, sems):
    i = pl.program_id(0)
    tb = xs_ref.shape[0] // ROW_TILES
    e = be_ref[i]
    fresh = jnp.logical_or(i == 0, e != be_ref[jnp.maximum(i - 1, 0)])
    used = i < nu_ref[0]

    def fetch(ex):
        return (pltpu.make_async_copy(wi_hbm.at[layer, ex], wi_st, sems.at[0]),
                pltpu.make_async_copy(wo_hbm.at[layer, ex], wo_st, sems.at[1]))

    @pl.when(i == 0)
    def _():
        for cp in fetch(e):
            cp.start()

    @pl.when(jnp.logical_and(fresh, used))
    def _():
        for cp in fetch(e):
            cp.wait()
        wi_b[...] = wi_st[...].astype(BF16)
        wo_b[...] = wo_st[...].astype(BF16)

        @pl.when(nx_ref[i] >= 0)
        def _():
            for cp in fetch(nx_ref[i]):
                cp.start()

    def compute(n):
        x = jnp.concatenate(
            [xs_ref[pl.ds(c, n, stride=ROW_TILES), :] for c in range(ROW_TILES)],
            axis=1).astype(BF16)
        y = jnp.zeros((n, D_MODEL), F32) + bo_ref[0, 0]
        half = 1024
        for j in range(D_FF // half):
            a, b = j * half, (j + 1) * half
            glu = _dot(x, wi_b[:, a:b]) + bi_ref[0, 0, :, a:b]
            lin = _dot(x, wi_b[:, D_FF + a:D_FF + b]) + bi_ref[0, 0, :, D_FF + a:D_FF + b]
            glu = jnp.minimum(glu, SWIGLU_LIMIT)
            lin = jnp.clip(lin, -SWIGLU_LIMIT, SWIGLU_LIMIT)
            act = glu * jax.nn.sigmoid(SWIGLU_ALPHA * glu) * (lin + 1.0)
            y = y + _dot(act.astype(BF16), wo_b[a:b, :])
        for c in range(ROW_TILES):
            y_ref[pl.ds(c, n, stride=ROW_TILES), :] = y[:, c * LANES:(c + 1) * LANES]

    full = nv_ref[i] > tb // 2

    @pl.when(jnp.logical_and(used, full))
    def _():
        compute(tb)

    @pl.when(jnp.logical_and(used, jnp.logical_not(full)))
    def _():
        compute(tb // 2)
        y_ref[pl.ds(tb // 2 * ROW_TILES, tb // 2 * ROW_TILES), :] = jnp.zeros(
            (tb // 2 * ROW_TILES, LANES), F32)

    @pl.when(jnp.logical_not(used))
    def _():
        y_ref[...] = jnp.zeros_like(y_ref)


def _experts(xs, block_e, n_used, next_e, n_valid, layer, w_in, b_in, w_out, b_out):
    tb = EXPERT_ROWS
    n_rows = xs.shape[0] // ROW_TILES
    nb = n_rows // tb
    depth, ne, d, f2 = w_in.shape

    def row_map(i, be, nu, nx, nv):
        return (jnp.minimum(i, nu[0] - 1), 0)

    grid_spec = pltpu.PrefetchScalarGridSpec(
        num_scalar_prefetch=4,
        grid=(nb,),
        in_specs=[
            pl.BlockSpec((tb * ROW_TILES, LANES), row_map),
            pl.BlockSpec(memory_space=pl.ANY),
            pl.BlockSpec((1, 1, 1, f2), lambda i, be, nu, nx, nv: (layer, be[i], 0, 0)),
            pl.BlockSpec(memory_space=pl.ANY),
            pl.BlockSpec((1, 1, 1, d), lambda i, be, nu, nx, nv: (layer, be[i], 0, 0)),
        ],
        out_specs=pl.BlockSpec((tb * ROW_TILES, LANES), lambda i, be, nu, nx, nv: (i, 0)),
        scratch_shapes=[pltpu.VMEM((d, f2), F32), pltpu.VMEM((D_FF, d), F32),
                        pltpu.VMEM((d, f2), BF16), pltpu.VMEM((D_FF, d), BF16),
                        pltpu.SemaphoreType.DMA((2,))],
    )
    return pl.pallas_call(
        functools.partial(_expert_kernel, layer),
        grid_spec=grid_spec,
        out_shape=jax.ShapeDtypeStruct((n_rows * ROW_TILES, LANES), F32),
        compiler_params=_cp(("arbitrary",), 56),
        name="experts",
    )(block_e, n_used, next_e, n_valid, xs, w_in, b_in.reshape(depth, ne, 1, f2), w_out,
      b_out.reshape(depth, ne, 1, d))


def _combine_kernel(final, cnt_ref, off_ref, dst_ref, pos_ref, gate_ref, x_ref, mod_ref, fg_ref,
                    yb_ref, o_ref, stage, sems):
    tm = x_ref.shape[0]
    n_sorted = TOP_K * tm
    i = pl.program_id(0)
    cur = lax.rem(i, 2)

    def fetch(tile, slot):
        def make_copy(sorted_row, src_row, size):
            return pltpu.make_async_copy(_rows(yb_ref, src_row, size),
                                         _rows(stage.at[slot], sorted_row, size), sems.at[slot])

        _run_copies(tile, cnt_ref, off_ref, dst_ref, make_copy)

    @pl.when(i == 0)
    def _():
        fetch(0, 0)

    @pl.when(i + 1 < pl.num_programs(0))
    def _():
        fetch(i + 1, 1 - cur)

    pltpu.make_async_copy(_rows(yb_ref, 0, n_sorted), stage.at[cur], sems.at[cur]).wait()

    pos = [pos_ref[:, TOP_K + k:TOP_K + k + 1] for k in range(TOP_K)]
    gate = [gate_ref[:, k:k + 1] for k in range(TOP_K)]
    y = jnp.zeros((tm, D_MODEL), F32)
    for sb in range(n_sorted // SORT_SLAB):
        rows = jnp.concatenate(
            [stage[cur, pl.ds(sb * SORT_SLAB * ROW_TILES + c, SORT_SLAB, stride=ROW_TILES), :]
             for c in range(ROW_TILES)], axis=1).astype(BF16)
        j = lax.broadcasted_iota(jnp.int32, (tm, SORT_SLAB), 1) + sb * SORT_SLAB
        w = jnp.zeros((tm, SORT_SLAB), F32)
        for k in range(TOP_K):
            w = w + jnp.where(pos[k] == j, gate[k], 0.0)
        y = y + _dot(w.astype(BF16), rows)
    o = x_ref[...] + mod_ref[0, 5:6, :] * y
    if final:
        o = _rms_rows(o) * fg_ref[...]
    o_ref[...] = o


def _combine(yb, tile_cnt, tile_off, tile_dst, pos_t, gates_t, x2, mod, final_g, final, seq):
    t, d = x2.shape
    tm = min(MOE_TILE, seq)
    grid_spec = pltpu.PrefetchScalarGridSpec(
        num_scalar_prefetch=3,
        grid=(t // tm,),
        in_specs=[
            pl.BlockSpec((tm, 2 * TOP_K), lambda i, *_: (i, 0)),
            pl.BlockSpec((tm, 2 * TOP_K), lambda i, *_: (i, 0)),
            pl.BlockSpec((tm, d), lambda i, *_: (i, 0)),
            pl.BlockSpec((1, 6, d), lambda i, *_: (i // (seq // tm), 0, 0)),
            pl.BlockSpec((1, d), lambda i, *_: (0, 0)),
            pl.BlockSpec(memory_space=pl.ANY),
        ],
        out_specs=pl.BlockSpec((tm, d), lambda i, *_: (i, 0)),
        scratch_shapes=[pltpu.VMEM((2, TOP_K * tm * ROW_TILES, LANES), F32),
                        pltpu.SemaphoreType.DMA((2,))],
    )
    return pl.pallas_call(
        functools.partial(_combine_kernel, final),
        grid_spec=grid_spec,
        out_shape=jax.ShapeDtypeStruct((t, d), F32),
        compiler_params=_cp(("arbitrary",), 48),
        name="combine",
    )(tile_cnt, tile_off, tile_dst, pos_t, gates_t, x2, mod, final_g.reshape(1, d), yb)


def _moe(h2, logits, x2, mod, layer, w_in, b_in, w_out, b_out, final_g, final, seq):
    t = x2.shape[0]
    tb = EXPERT_ROWS
    idx, gates, tcnt, toff, trun, cnt = _route(logits)
    counts = cnt[:, 0].astype(jnp.int32)
    nblk = (counts + tb - 1) // tb
    blk_end = jnp.cumsum(nblk)
    pad_start = (blk_end - nblk) * tb
    n_blocks = (t * TOP_K) // tb + N_EXPERTS
    n_used = blk_end[-1:]
    experts = jnp.arange(N_EXPERTS, dtype=jnp.int32)
    last_e = jnp.max(jnp.where(nblk > 0, experts, 0))
    blk = jnp.arange(n_blocks, dtype=jnp.int32)
    block_e = jnp.minimum(
        jnp.sum((blk_end[None, :] <= blk[:, None]).astype(jnp.int32), axis=1), last_e)
    later = (experts[None, :] > block_e[:, None]) & (nblk[None, :] > 0)
    next_e = jnp.min(jnp.where(later, experts[None, :], N_EXPERTS), axis=1)
    next_e = jnp.where(next_e == N_EXPERTS, -1, next_e).astype(jnp.int32)
    tile_cnt = tcnt[:, 0].astype(jnp.int32)
    tile_off = toff[:, 0].astype(jnp.int32)
    tile_dst = (trun[:, 0].astype(jnp.int32).reshape(-1, N_EXPERTS) + pad_start[None, :]).reshape(-1)
    fill_start = jnp.concatenate([pad_start + counts, blk_end[-1:] * tb]).astype(jnp.int32)
    fill_len = jnp.concatenate([nblk * tb - counts, (n_blocks - blk_end[-1:]) * tb]).astype(jnp.int32)
    xs = _dispatch(h2, idx, tile_cnt, tile_off, tile_dst, fill_start, fill_len, n_blocks * tb)
    row_end = jnp.sum(jnp.where(block_e[:, None] == experts[None, :],
                                (pad_start + counts)[None, :], 0), axis=1)
    n_valid = jnp.clip(row_end - blk * tb, 0, tb).astype(jnp.int32)
    yb = _experts(xs, block_e, n_used.astype(jnp.int32), next_e, n_valid, layer,
                  w_in, b_in, w_out, b_out)
    return _combine(yb, tile_cnt, tile_off, tile_dst, idx.T, gates.T, x2, mod, final_g, final, seq)


def kernel(x, c, positions, w_ada, b_ada, even_w_in, gla_w_gate, gla_b_gate, gla_norm_g,
           diff_lam_q1, diff_lam_k1, diff_lam_q2, diff_lam_k2, diff_norm_g, even_w_out,
           odd_w_in, sgu_ln_g, sgu_ln_b, sgu_w, sgu_b, odd_w_out,
           router_w, router_b, expert_w_in, expert_b_in, expert_w_out, expert_b_out,
           final_norm_g):
    bsz, seq, d = x.shape
    depth = w_ada.shape[0]
    t = bsz * seq
    mods = _modulation(c, w_ada, b_ada).reshape(depth, bsz, 6, d)
    cos, sin = _rope_tables(positions)
    x2 = x.reshape(t, d)
    for layer in range(depth):
        mod = mods[layer]
        j = layer // 2
        if layer % 2 == 0:
            z, f = _proj_even(x2, mod, even_w_in[j], gla_w_gate[j], gla_b_gate[j], cos, sin, seq)
            o_gla = _gla(z, f, gla_norm_g[j], bsz, seq)
            lam_init = 0.8 - 0.6 * math.exp(-0.3 * layer)
            o_diff = _diff_attn(z, diff_lam_q1[j], diff_lam_k1[j], diff_lam_q2[j], diff_lam_k2[j],
                                diff_norm_g[j], lam_init, bsz, seq)
            mixed, w_out = (o_gla, o_diff), even_w_out[j]
        else:
            z, f = _proj_odd(x2, mod, odd_w_in[j], sgu_ln_g[j], sgu_ln_b[j], cos, sin, seq)
            mixed, w_out = (_sgu_ret(z, f, sgu_w[j], sgu_b[j], bsz, seq),), odd_w_out[j]
        x2, h2, logits = _out_proj(mixed, w_out, x2, mod, router_w[layer], router_b[layer], seq)
        x2 = _moe(h2, logits, x2, mod, layer, expert_w_in, expert_b_in, expert_w_out,
                  expert_b_out, final_norm_g, layer == depth - 1, seq)
    return x2.reshape(bsz, seq, d)
```

```python
import functools
import math

import jax
import jax.numpy as jnp
from jax import lax
from jax.experimental import pallas as pl
from jax.experimental.pallas import tpu as pltpu

F32 = jnp.float32
BF16 = jnp.bfloat16

D_MODEL = 1024
EPS = 1e-6
ROPE_THETA = 10000.0
ROPE_HALF = 32

GLA_HEADS = 4
GLA_DK = 64
GLA_DV = 128
GLA_RANK = 16
GLA_CHUNK = 64
GLA_GATE_NORMALIZER = 16.0
GLA_QK_W = GLA_HEADS * GLA_DK
GLA_V_W = GLA_HEADS * GLA_DV

DIFF_HEADS = 4
DIFF_D = 64
DIFF_DV = 128
DIFF_QK_W = DIFF_HEADS * 2 * DIFF_D
DIFF_V_W = DIFF_HEADS * DIFF_DV

SGU_GROUPS = 4
SGU_CH = 128
SGU_CHUNK = 128
SGU_W = SGU_GROUPS * SGU_CH

RET_HEADS = 4
RET_DK = 64
RET_DV = 128
RET_CHUNK = 128
RET_QK_W = RET_HEADS * RET_DK
RET_V_W = RET_HEADS * RET_DV

N_EXPERTS = 32
TOP_K = 4
D_FF = D_MODEL
SWIGLU_ALPHA = 1.702
SWIGLU_LIMIT = 7.0

LANES = 128
SUBLANES = 8
ROW_TILES = D_MODEL // LANES

PROJ_ROWS = 512
GLA_ROWS = 512
GLA_GROUP = 4
SGU_RET_ROWS = 512
ATT_Q_ROWS = 256
ATT_HEADS_PER_STEP = 4
MOE_TILE = 512
SORT_SLAB = 512
RUN_BITS = 10
FILL_SHIFT = 6
FILL_ROWS = 1 << FILL_SHIFT
EXPERT_ROWS = 512
MOD_COLS = 1536

MIB = 1024 * 1024


def _cp(semantics, vmem_mib):
    return pltpu.CompilerParams(dimension_semantics=semantics, vmem_limit_bytes=vmem_mib * MIB)


def _dot(a, b):
    return jnp.dot(a, b, preferred_element_type=F32)


def _dot_nt(a, b):
    return lax.dot_general(a, b, (((1,), (1,)), ((), ())), preferred_element_type=F32)


def _dot_tn(a, b):
    return lax.dot_general(a, b, (((0,), (0,)), ((), ())), preferred_element_type=F32)


def _rms_rows(x):
    return x * lax.rsqrt(jnp.mean(x * x, axis=-1, keepdims=True) + EPS)


def _silu(x):
    return x * jax.nn.sigmoid(x)


def _rope_chunk(v, cos, sin, lo_half):
    rot = jnp.where(lo_half, -pltpu.roll(v, 96, 1), pltpu.roll(v, 32, 1))
    return v * cos + rot * sin


def _mod_kernel(c_ref, w_ref, b_ref, o_ref):
    c = c_ref[...]
    ca = _silu(c).astype(BF16)
    o_ref[0] = _dot(ca, w_ref[0].astype(BF16)) + b_ref[0]


def _modulation(c, w_ada, b_ada):
    depth, d, n = w_ada.shape
    bsz = c.shape[0]
    return pl.pallas_call(
        _mod_kernel,
        grid=(depth, n // MOD_COLS),
        in_specs=[
            pl.BlockSpec((bsz, d), lambda l, j: (0, 0)),
            pl.BlockSpec((1, d, MOD_COLS), lambda l, j: (l, 0, j)),
            pl.BlockSpec((1, 1, MOD_COLS), lambda l, j: (l, 0, j)),
        ],
        out_specs=pl.BlockSpec((1, bsz, MOD_COLS), lambda l, j: (l, 0, j)),
        out_shape=jax.ShapeDtypeStruct((depth, bsz, n), F32),
        compiler_params=_cp(("arbitrary", "arbitrary"), 40),
        name="adaln_mod",
    )(c, w_ada, b_ada.reshape(depth, 1, n))


def _rope_table_kernel(p_ref, f_ref, c_ref, s_ref):
    ang = p_ref[...].astype(F32) * f_ref[...]
    c_ref[...] = jnp.cos(ang)
    s_ref[...] = jnp.sin(ang)


def _rope_tables(positions):
    t = positions.size
    per_row = LANES // ROPE_HALF
    rows = t // per_row
    pos_d = jnp.repeat(positions.reshape(rows, per_row), ROPE_HALF, axis=1)
    inv_freq = ROPE_THETA ** (-jnp.arange(ROPE_HALF, dtype=F32) / ROPE_HALF)
    freq_d = jnp.tile(inv_freq, per_row).reshape(1, LANES)
    tr = min(512, rows)
    cos_d, sin_d = pl.pallas_call(
        _rope_table_kernel,
        grid=(rows // tr,),
        in_specs=[pl.BlockSpec((tr, LANES), lambda i: (i, 0)),
                  pl.BlockSpec((1, LANES), lambda i: (0, 0))],
        out_specs=[pl.BlockSpec((tr, LANES), lambda i: (i, 0))] * 2,
        out_shape=[jax.ShapeDtypeStruct((rows, LANES), F32)] * 2,
        compiler_params=_cp(("arbitrary",), 32),
        name="rope_tables",
    )(pos_d, freq_d)
    cos = jnp.tile(cos_d.reshape(t, ROPE_HALF), (1, per_row))
    sin = jnp.tile(sin_d.reshape(t, ROPE_HALF), (1, per_row))
    return cos, sin


def _modulated_rms(x_ref, mod_ref, which):
    x = x_ref[...]
    shift = mod_ref[0, 3 * which:3 * which + 1, :]
    scale = mod_ref[0, 3 * which + 1:3 * which + 2, :]
    return _rms_rows(x) * (1.0 + scale) + shift


EVEN_Z = (GLA_V_W, GLA_V_W, DIFF_QK_W, DIFF_QK_W, DIFF_V_W)
EVEN_Z_W = sum(EVEN_Z)
EVEN_F_W = 3 * GLA_QK_W
EVEN_W_COLS = EVEN_Z_W + 2 * GLA_QK_W + LANES


def _proj_even_kernel(x_ref, mod_ref, w_ref, wg_ref, bg_ref, cos_ref, sin_ref, z_ref, f_ref):
    hb = _modulated_rms(x_ref, mod_ref, 0).astype(BF16)

    def sec(a, b):
        return _dot(hb, w_ref[:, a:b])

    for a, b in ((0, 512), (512, 1024), (2048, 2560)):
        z_ref[:, a:b] = sec(a, b).astype(BF16)
    cos = cos_ref[...]
    sin = sin_ref[...]
    lo_half = (lax.broadcasted_iota(jnp.int32, cos.shape, 1) % 64) < ROPE_HALF
    q_scale = DIFF_D ** -0.5 * math.log2(math.e)
    for a, scl in ((1024, q_scale), (1536, 1.0)):
        full = sec(a, a + DIFF_QK_W)
        for c in range(DIFF_QK_W // LANES):
            v = full[:, c * LANES:(c + 1) * LANES]
            z_ref[:, a + c * LANES:a + (c + 1) * LANES] = (
                _rope_chunk(v, cos, sin, lo_half) * scl).astype(BF16)
    f_ref[:, 0:256] = sec(2560, 2816) * (GLA_DK ** -0.5)
    f_ref[:, 256:512] = sec(2816, 3072)
    gr = sec(3072, 3200).astype(BF16)
    pre = _dot(gr, wg_ref[...]) + bg_ref[...]
    log_sig = jnp.minimum(pre, 0.0) - jnp.log1p(jnp.exp(-jnp.abs(pre)))
    f_ref[:, 512:768] = log_sig / GLA_GATE_NORMALIZER


def _proj_even(x2, mod, w_in, w_gate, b_gate, cos, sin, seq):
    t, d = x2.shape
    tm = min(PROJ_ROWS, seq)
    gq, gk, gv, gr, gg, dq, dk, dv = jnp.split(
        w_in, [256, 512, 1024, 1040, 1552, 2064, 2576], axis=1)
    gr_pad = jnp.pad(gr, ((0, 0), (0, LANES - GLA_RANK)))
    w_cat = jnp.concatenate([gv, gg, dq, dk, dv, gq, gk, gr_pad], axis=1).astype(BF16)
    wg_pad = jnp.pad(w_gate, ((0, LANES - GLA_RANK), (0, 0))).astype(BF16)
    return pl.pallas_call(
        _proj_even_kernel,
        grid=(t // tm,),
        in_specs=[
            pl.BlockSpec((tm, d), lambda i: (i, 0)),
            pl.BlockSpec((1, 6, d), lambda i: (i // (seq // tm), 0, 0)),
            pl.BlockSpec((d, EVEN_W_COLS), lambda i: (0, 0)),
            pl.BlockSpec((LANES, GLA_QK_W), lambda i: (0, 0)),
            pl.BlockSpec((1, GLA_QK_W), lambda i: (0, 0)),
            pl.BlockSpec((tm, LANES), lambda i: (i, 0)),
            pl.BlockSpec((tm, LANES), lambda i: (i, 0)),
        ],
        out_specs=[pl.BlockSpec((tm, EVEN_Z_W), lambda i: (i, 0)),
                   pl.BlockSpec((tm, EVEN_F_W), lambda i: (i, 0))],
        out_shape=[jax.ShapeDtypeStruct((t, EVEN_Z_W), BF16),
                   jax.ShapeDtypeStruct((t, EVEN_F_W), F32)],
        compiler_params=_cp(("arbitrary",), 48),
        name="proj_even",
    )(x2, mod, w_cat, wg_pad, b_gate.reshape(1, GLA_QK_W), cos, sin)


ODD_Z_W = 2 * SGU_W + 2 * RET_V_W
ODD_F_W = 2 * RET_QK_W
ODD_W_COLS = ODD_Z_W + ODD_F_W


def _gelu_exact(x):
    return 0.5 * x * (1.0 + lax.erf(x * (2.0 ** -0.5)))


def _proj_odd_kernel(x_ref, mod_ref, w_ref, lng_ref, lnb_ref, cos_ref, sin_ref, z_ref, f_ref):
    hb = _modulated_rms(x_ref, mod_ref, 0).astype(BF16)

    def sec(a, b):
        return _dot(hb, w_ref[:, a:b])

    z_ref[:, 0:512] = _gelu_exact(sec(0, 512)).astype(BF16)
    sv = _gelu_exact(sec(512, 1024))
    mu = jnp.mean(sv, axis=-1, keepdims=True)
    cen = sv - mu
    var = jnp.mean(cen * cen, axis=-1, keepdims=True)
    z_ref[:, 512:1024] = (cen * lax.rsqrt(var + EPS) * lng_ref[...] + lnb_ref[...]).astype(BF16)
    for a, b in ((1024, 1536), (1536, 2048)):
        z_ref[:, a:b] = sec(a, b).astype(BF16)
    cos = cos_ref[...]
    sin = sin_ref[...]
    lo_half = (lax.broadcasted_iota(jnp.int32, cos.shape, 1) % 64) < ROPE_HALF
    k_scale = RET_DK ** -0.5
    for a, scl in ((0, 1.0), (256, k_scale)):
        full = sec(ODD_Z_W + a, ODD_Z_W + a + RET_QK_W)
        for c in range(RET_QK_W // LANES):
            v = full[:, c * LANES:(c + 1) * LANES]
            f_ref[:, a + c * LANES:a + (c + 1) * LANES] = _rope_chunk(v, cos, sin, lo_half) * scl


def _proj_odd(x2, mod, w_in, ln_g, ln_b, cos, sin, seq):
    t, d = x2.shape
    tm = min(PROJ_ROWS, seq)
    su, sv, rq, rk, rv, rg = jnp.split(w_in, [512, 1024, 1280, 1536, 2048], axis=1)
    w_cat = jnp.concatenate([su, sv, rv, rg, rq, rk], axis=1).astype(BF16)
    return pl.pallas_call(
        _proj_odd_kernel,
        grid=(t // tm,),
        in_specs=[
            pl.BlockSpec((tm, d), lambda i: (i, 0)),
            pl.BlockSpec((1, 6, d), lambda i: (i // (seq // tm), 0, 0)),
            pl.BlockSpec((d, ODD_W_COLS), lambda i: (0, 0)),
            pl.BlockSpec((1, SGU_W), lambda i: (0, 0)),
            pl.BlockSpec((1, SGU_W), lambda i: (0, 0)),
            pl.BlockSpec((tm, LANES), lambda i: (i, 0)),
            pl.BlockSpec((tm, LANES), lambda i: (i, 0)),
        ],
        out_specs=[pl.BlockSpec((tm, ODD_Z_W), lambda i: (i, 0)),
                   pl.BlockSpec((tm, ODD_F_W), lambda i: (i, 0))],
        out_shape=[jax.ShapeDtypeStruct((t, ODD_Z_W), BF16),
                   jax.ShapeDtypeStruct((t, ODD_F_W), F32)],
        compiler_params=_cp(("arbitrary",), 48),
        name="proj_odd",
    )(x2, mod, w_cat, ln_g.reshape(1, SGU_W), ln_b.reshape(1, SGU_W), cos, sin)


def _gla_kernel(q_ref, k_ref, la_ref, v_ref, gg_ref, g_ref, o_ref, st_ref):
    @pl.when(pl.program_id(1) == 0)
    def _():
        st_ref[...] = jnp.zeros_like(st_ref)

    c = GLA_CHUNK
    sc = GLA_GROUP * c
    rows = q_ref.shape[0]
    r_i = lax.broadcasted_iota(jnp.int32, (sc, sc), 0)
    c_i = lax.broadcasted_iota(jnp.int32, (sc, sc), 1)
    block_causal = (r_i // c == c_i // c) & (r_i >= c_i)
    tril = block_causal.astype(BF16)
    causal2 = jnp.concatenate([block_causal, block_causal], axis=0)
    lo = lax.broadcasted_iota(jnp.int32, (sc, LANES), 1) < GLA_DK
    lane_s = lax.broadcasted_iota(jnp.int32, (LANES, LANES), 1) < GLA_DK
    g_row = g_ref[...]
    states = [st_ref[0], st_ref[1]]
    for j in range(rows // sc):
        rs = slice(j * sc, (j + 1) * sc)
        la = la_ref[rs, :]
        la_hi = la.astype(BF16)
        la_lo = (la - la_hi.astype(F32)).astype(BF16)
        b = _dot(tril, la_hi) + _dot(tril, la_lo)
        b_last = [b[(g + 1) * c - 1:(g + 1) * c, :] for g in range(GLA_GROUP)]
        b_last_rows = jnp.concatenate(
            [jnp.broadcast_to(bl, (c, bl.shape[1])) for bl in b_last], axis=0)
        qd = q_ref[rs, :] * jnp.exp(b)
        kk = k_ref[rs, :]
        ki = kk * jnp.exp(-b)
        kd = kk * jnp.exp(b_last_rows - b)
        for p in range(GLA_HEADS // 2):
            ls = slice(p * LANES, (p + 1) * LANES)
            qd_p = qd[:, ls]
            qm = (jnp.where(lo, qd_p, 0.0).astype(BF16), jnp.where(lo, 0.0, qd_p).astype(BF16))
            s2 = _dot_nt(jnp.concatenate(qm, axis=0), ki[:, ls].astype(BF16))
            s2 = jnp.where(causal2, s2, 0.0).astype(BF16)
            kd_p = kd[:, ls].astype(BF16)
            heads = (2 * p, 2 * p + 1)
            v = [v_ref[rs, h * GLA_DV:(h + 1) * GLA_DV] for h in heads]
            inter = ([], [])
            for g in range(GLA_GROUP):
                gs = slice(g * c, (g + 1) * c)
                st_b = states[p].astype(BF16)
                for hh in range(2):
                    inter[hh].append(_dot_nt(qm[hh][gs, :], st_b))
                new = [_dot_tn(v[hh][gs, :], kd_p[gs, :]) for hh in range(2)]
                states[p] = (jnp.exp(b_last[g][:, ls]) * states[p]
                             + jnp.where(lane_s, new[0], new[1]))
            for hh in range(2):
                hs = slice(heads[hh] * GLA_DV, (heads[hh] + 1) * GLA_DV)
                o = _dot(s2[hh * sc:(hh + 1) * sc, :], v[hh]) + jnp.concatenate(inter[hh], axis=0)
                o = _rms_rows(o) * g_row
                gate = _silu(gg_ref[rs, hs].astype(F32))
                o_ref[rs, hs] = (o * gate).astype(BF16)
    st_ref[0] = states[0]
    st_ref[1] = states[1]


def _gla(z, f, norm_g, bsz, seq):
    t = z.shape[0]
    tc = min(GLA_ROWS, seq)
    nc = seq // tc
    return pl.pallas_call(
        _gla_kernel,
        grid=(bsz, nc),
        in_specs=[
            pl.BlockSpec((tc, GLA_QK_W), lambda b, i: (b * nc + i, 0)),
            pl.BlockSpec((tc, GLA_QK_W), lambda b, i: (b * nc + i, 1)),
            pl.BlockSpec((tc, GLA_QK_W), lambda b, i: (b * nc + i, 2)),
            pl.BlockSpec((tc, GLA_V_W), lambda b, i: (b * nc + i, 0)),
            pl.BlockSpec((tc, GLA_V_W), lambda b, i: (b * nc + i, 1)),
            pl.BlockSpec((1, GLA_DV), lambda b, i: (0, 0)),
        ],
        out_specs=pl.BlockSpec((tc, GLA_V_W), lambda b, i: (b * nc + i, 0)),
        out_shape=jax.ShapeDtypeStruct((t, GLA_V_W), BF16),
        scratch_shapes=[pltpu.VMEM((2, LANES, LANES), F32)],
        compiler_params=_cp(("arbitrary", "arbitrary"), 32),
        name="gla",
    )(f, f, f, z, z, norm_g.reshape(1, GLA_DV))


def _diff_attn_kernel(lam_init, q_ref, k_ref, v_ref, lq1_ref, lk1_ref, lq2_ref, lk2_ref, g_ref,
                      o_ref, vt_ref, s_ref, m_ref, l_ref, acc_ref):
    tq = q_ref.shape[0]
    seq = k_ref.shape[0]
    n_heads = q_ref.shape[1] // LANES
    qi = pl.program_id(2)

    @pl.when(qi == 0)
    def _():
        for hh in range(n_heads):
            for cb in range(seq // tq):
                blk = v_ref[cb * tq:(cb + 1) * tq, hh * LANES:(hh + 1) * LANES]
                vt_ref[hh, :, cb * tq:(cb + 1) * tq] = blk.astype(F32).T.astype(BF16)

    lo = lax.broadcasted_iota(jnp.int32, (tq, LANES), 1) < DIFF_D
    qqs = []
    for hh in range(n_heads):
        q = q_ref[:, hh * LANES:(hh + 1) * LANES]
        zero = jnp.zeros_like(q)
        qqs.append(jnp.concatenate([jnp.where(lo, q, zero), jnp.where(lo, zero, q)], axis=0))

    m_ref[...] = jnp.full(m_ref.shape, -jnp.inf, F32)
    l_ref[...] = jnp.zeros_like(l_ref)
    acc_ref[...] = jnp.zeros_like(acc_ref)

    def scores(j, slot):
        start = pl.multiple_of(j * tq, tq)
        for hh in range(n_heads):
            kj = k_ref[pl.ds(start, tq), hh * LANES:(hh + 1) * LANES]
            s_ref[slot, hh] = _dot_nt(kj, qqs[hh])

    def update(j, slot, masked):
        start = pl.multiple_of(j * tq, tq)
        for hh in range(n_heads):
            vtj = vt_ref[hh, :, pl.ds(start, tq)]
            alpha, p = [], []
            for c in range(2 * tq // LANES):
                cs = slice(c * LANES, (c + 1) * LANES)
                s_c = s_ref[slot, hh, :, cs]
                if masked:
                    kv = lax.broadcasted_iota(jnp.int32, s_c.shape, 0)
                    q_pos = lax.broadcasted_iota(jnp.int32, s_c.shape, 1) + (c * LANES) % tq
                    s_c = jnp.where(kv <= q_pos, s_c, -jnp.inf)
                m_old = m_ref[hh, :, cs]
                m_c = jnp.maximum(m_old, jnp.max(s_c, axis=0, keepdims=True))
                a_c = jnp.exp2(m_old - m_c)
                p_c = jnp.exp2(s_c - m_c)
                m_ref[hh, :, cs] = m_c
                l_ref[hh, :, cs] = a_c * l_ref[hh, :, cs] + jnp.sum(p_c, axis=0, keepdims=True)
                alpha.append(a_c)
                p.append(p_c.astype(BF16))
            acc_ref[hh] = (jnp.concatenate(alpha, axis=1) * acc_ref[hh]
                           + _dot(vtj, jnp.concatenate(p, axis=1)))

    scores(0, 0)

    def body(i, carry):
        j = 2 * i
        scores(j + 1, 1)
        update(j, 0, False)
        scores(j + 2, 0)
        update(j + 1, 1, False)
        return carry

    n_pairs = qi // 2
    lax.fori_loop(0, n_pairs, body, 0)
    j0 = 2 * n_pairs

    @pl.when(qi % 2 == 1)
    def _():
        scores(j0 + 1, 1)
        update(j0, 0, False)
        update(j0 + 1, 1, True)

    @pl.when(qi % 2 == 0)
    def _():
        update(j0, 0, True)

    lam = (jnp.exp(jnp.sum(lq1_ref[...] * lk1_ref[...], axis=-1, keepdims=True))
           - jnp.exp(jnp.sum(lq2_ref[...] * lk2_ref[...], axis=-1, keepdims=True)) + lam_init)
    for hh in range(n_heads):
        l = l_ref[hh]
        acc = acc_ref[hh]
        o12 = acc / l
        o = o12[:, :tq] - lam * o12[:, tq:]
        o = o * lax.rsqrt(jnp.mean(o * o, axis=0, keepdims=True) + EPS)
        o = o * g_ref[...] * (1.0 - lam_init)
        o_ref[:, hh * DIFF_DV:(hh + 1) * DIFF_DV] = o.T.astype(BF16)


def _diff_attn(z, lq1, lk1, lq2, lk2, norm_g, lam_init, bsz, seq):
    t = z.shape[0]
    tq = min(ATT_Q_ROWS, seq)
    nq = seq // tq
    hw = ATT_HEADS_PER_STEP * LANES
    qb, kb, vb = 1024 // hw, 1536 // hw, 2048 // hw
    small = pl.BlockSpec((1, DIFF_D), lambda b, h, i: (0, 0))
    return pl.pallas_call(
        functools.partial(_diff_attn_kernel, lam_init),
        grid=(bsz, DIFF_HEADS // ATT_HEADS_PER_STEP, nq),
        in_specs=[
            pl.BlockSpec((tq, hw), lambda b, h, i: (b * nq + i, qb + h)),
            pl.BlockSpec((seq, hw), lambda b, h, i: (b, kb + h)),
            pl.BlockSpec((seq, hw), lambda b, h, i: (b, vb + h)),
            small, small, small, small,
            pl.BlockSpec((DIFF_DV, 1), lambda b, h, i: (0, 0)),
        ],
        out_specs=pl.BlockSpec((tq, hw), lambda b, h, i: (b * nq + i, h)),
        out_shape=jax.ShapeDtypeStruct((t, DIFF_V_W), BF16),
        scratch_shapes=[pltpu.VMEM((ATT_HEADS_PER_STEP, DIFF_DV, seq), BF16),
                        pltpu.VMEM((2, ATT_HEADS_PER_STEP, tq, 2 * tq), F32),
                        pltpu.VMEM((ATT_HEADS_PER_STEP, 1, 2 * tq), F32),
                        pltpu.VMEM((ATT_HEADS_PER_STEP, 1, 2 * tq), F32),
                        pltpu.VMEM((ATT_HEADS_PER_STEP, DIFF_DV, 2 * tq), F32)],
        compiler_params=_cp(("arbitrary", "arbitrary", "arbitrary"), 32),
        name="diff_attn",
    )(z, z, z, lq1.reshape(1, DIFF_D), lk1.reshape(1, DIFF_D), lq2.reshape(1, DIFF_D),
      lk2.reshape(1, DIFF_D), norm_g.reshape(DIFF_DV, 1))


def _sgu_ret_kernel(su_ref, sv_ref, rv_ref, rg_ref, q_ref, k_ref, ws_ref, bs_ref, o_ref, st_ref):
    @pl.when(pl.program_id(1) == 0)
    def _():
        st_ref[...] = jnp.zeros_like(st_ref)

    c = RET_CHUNK
    row = lax.broadcasted_iota(jnp.int32, (c, c), 0)
    col = lax.broadcasted_iota(jnp.int32, (c, c), 1)
    causal = row >= col
    log_g = [math.log(1.0 - 2.0 ** (-5.0 - h)) for h in range(RET_HEADS)]
    lo = col < RET_DK
    rel = (row - col).astype(F32)
    pos = row.astype(F32)
    w_sgu = [jnp.where(causal, ws_ref[g], 0.0).astype(BF16) for g in range(SGU_GROUPS)]
    decays = [jnp.where(causal, jnp.exp(log_g[h] * jnp.maximum(rel, 0.0)), 0.0)
              for h in range(RET_HEADS)]
    lgs = [jnp.where(lo, log_g[2 * p], log_g[2 * p + 1]) for p in range(RET_HEADS // 2)]
    q_decs = [jnp.exp(lg * (pos + 1.0)) for lg in lgs]
    k_decs = [jnp.exp(lg * (c - 1.0 - pos)) for lg in lgs]
    states = [st_ref[p] for p in range(RET_HEADS // 2)]
    for j in range(su_ref.shape[0] // c):
        rs = slice(j * c, (j + 1) * c)
        for g in range(SGU_GROUPS):
            gs = slice(g * SGU_CH, (g + 1) * SGU_CH)
            s = _dot(w_sgu[g], sv_ref[rs, gs]) + bs_ref[g]
            o_ref[rs, gs] = (su_ref[rs, gs].astype(F32) * s).astype(BF16)
        for p in range(RET_HEADS // 2):
            ls = slice(p * LANES, (p + 1) * LANES)
            q_p = q_ref[rs, ls]
            k_p = k_ref[rs, ls]
            qm = (jnp.where(lo, q_p, 0.0).astype(BF16), jnp.where(lo, 0.0, q_p).astype(BF16))
            s2 = _dot_nt(jnp.concatenate(qm, axis=0), k_p.astype(BF16))
            qd = q_p * q_decs[p]
            qdm = (jnp.where(lo, qd, 0.0).astype(BF16), jnp.where(lo, 0.0, qd).astype(BF16))
            kd = (k_p * k_decs[p]).astype(BF16)
            st_b = states[p].astype(BF16)
            new = []
            for hh in range(2):
                h = 2 * p + hh
                hs = slice(h * RET_DV, (h + 1) * RET_DV)
                s_h = (s2[hh * c:(hh + 1) * c, :] * decays[h]).astype(BF16)
                v_h = rv_ref[rs, hs]
                o = _dot(s_h, v_h) + _dot_nt(qdm[hh], st_b)
                gate = _silu(rg_ref[rs, hs].astype(F32))
                o_ref[rs, SGU_W + h * RET_DV:SGU_W + (h + 1) * RET_DV] = (
                    _rms_rows(o) * gate).astype(BF16)
                new.append(_dot_tn(v_h, kd))
            states[p] = jnp.exp(lgs[p] * float(c)) * states[p] + jnp.where(lo, new[0], new[1])
    for p in range(RET_HEADS // 2):
        st_ref[p] = states[p]


def _sgu_ret(z, f, w_s, b_s, bsz, seq):
    t = z.shape[0]
    cc = RET_CHUNK
    c = min(SGU_RET_ROWS, seq)
    nc = seq // c
    return pl.pallas_call(
        _sgu_ret_kernel,
        grid=(bsz, nc),
        in_specs=[
            pl.BlockSpec((c, SGU_W), lambda b, i: (b * nc + i, 0)),
            pl.BlockSpec((c, SGU_W), lambda b, i: (b * nc + i, 1)),
            pl.BlockSpec((c, RET_V_W), lambda b, i: (b * nc + i, 2)),
            pl.BlockSpec((c, RET_V_W), lambda b, i: (b * nc + i, 3)),
            pl.BlockSpec((c, RET_QK_W), lambda b, i: (b * nc + i, 0)),
            pl.BlockSpec((c, RET_QK_W), lambda b, i: (b * nc + i, 1)),
            pl.BlockSpec((SGU_GROUPS, cc, cc), lambda b, i: (0, 0, 0)),
            pl.BlockSpec((SGU_GROUPS, cc, 1), lambda b, i: (0, 0, 0)),
        ],
        out_specs=pl.BlockSpec((c, SGU_W + RET_V_W), lambda b, i: (b * nc + i, 0)),
        out_shape=jax.ShapeDtypeStruct((t, SGU_W + RET_V_W), BF16),
        scratch_shapes=[pltpu.VMEM((2, LANES, LANES), F32)],
        compiler_params=_cp(("arbitrary", "arbitrary"), 32),
        name="sgu_retention",
    )(z, z, z, z, f, f, w_s, b_s.reshape(SGU_GROUPS, cc, 1))


def _out_proj_kernel(n_in, *refs):
    o_refs = refs[:n_in]
    w_ref, x_ref, mod_ref, rw_ref, rb_ref, xn_ref, h_ref, lg_ref = refs[n_in:]
    k_each = D_MODEL // n_in
    y = _dot(o_refs[0][...], w_ref[0:k_each, :])
    for n in range(1, n_in):
        y = y + _dot(o_refs[n][...], w_ref[n * k_each:(n + 1) * k_each, :])
    xn = x_ref[...] + mod_ref[0, 2:3, :] * y
    xn_ref[...] = xn
    h = _rms_rows(xn) * (1.0 + mod_ref[0, 4:5, :]) + mod_ref[0, 3:4, :]
    hb = h.astype(BF16)
    h_ref[...] = hb
    lg_ref[...] = _dot_nt(rw_ref[...], hb) + rb_ref[...]


def _out_proj(mixed, w_out, x2, mod, router_w, router_b, seq):
    t, d = x2.shape
    tm = min(PROJ_ROWS, seq)
    n_in = len(mixed)
    k_each = d // n_in
    rw = router_w.T.astype(BF16)
    rb = router_b.reshape(N_EXPERTS, 1)
    return pl.pallas_call(
        functools.partial(_out_proj_kernel, n_in),
        grid=(t // tm,),
        in_specs=[pl.BlockSpec((tm, k_each), lambda i: (i, 0)) for _ in mixed] + [
            pl.BlockSpec((d, d), lambda i: (0, 0)),
            pl.BlockSpec((tm, d), lambda i: (i, 0)),
            pl.BlockSpec((1, 6, d), lambda i: (i // (seq // tm), 0, 0)),
            pl.BlockSpec((N_EXPERTS, d), lambda i: (0, 0)),
            pl.BlockSpec((N_EXPERTS, 1), lambda i: (0, 0)),
        ],
        out_specs=[pl.BlockSpec((tm, d), lambda i: (i, 0)),
                   pl.BlockSpec((tm, d), lambda i: (i, 0)),
                   pl.BlockSpec((N_EXPERTS, tm), lambda i: (0, i))],
        out_shape=[jax.ShapeDtypeStruct((t, d), F32),
                   jax.ShapeDtypeStruct((t, d), BF16),
                   jax.ShapeDtypeStruct((N_EXPERTS, t), F32)],
        compiler_params=_cp(("arbitrary",), 48),
        name="out_proj",
    )(*mixed, w_out.astype(BF16), x2, mod, rw, rb)


def _route_kernel(lg_ref, idx_ref, gate_ref, tcnt_ref, toff_ref, trun_ref, cnt_ref, run_ref):
    @pl.when(pl.program_id(0) == 0)
    def _():
        run_ref[...] = jnp.zeros_like(run_ref)

    tm = lg_ref.shape[1]
    row = lax.broadcasted_iota(jnp.int32, (N_EXPERTS, tm), 0)
    neg = -jnp.inf
    l = lg_ref[...]
    vals, firsts, hots = [], [], []
    for _ in range(TOP_K):
        m = jnp.max(l, axis=0, keepdims=True)
        first = jnp.min(jnp.where(l == m, row, N_EXPERTS), axis=0, keepdims=True)
        hot = row == first
        vals.append(m)
        firsts.append(first)
        hots.append(hot)
        l = jnp.where(hot, neg, l)
    sel = hots[0] | hots[1] | hots[2] | hots[3]
    ex = [jnp.exp(v - vals[0]) for v in vals]
    denom = ex[0] + ex[1] + ex[2] + ex[3]
    r_i = lax.broadcasted_iota(jnp.int32, (tm, tm), 0)
    c_i = lax.broadcasted_iota(jnp.int32, (tm, tm), 1)
    before = (r_i < c_i).astype(BF16)
    earlier = _dot(sel.astype(BF16), before)
    tile_cnt = jnp.sum(sel.astype(F32), axis=1, keepdims=True)
    lower = jnp.zeros((N_EXPERTS, tm), F32)
    for k in range(TOP_K):
        lower = lower + (firsts[k] < row).astype(F32)
    tile_off = jnp.sum(lower, axis=1, keepdims=True)
    slot = tile_off + earlier
    row8 = lax.broadcasted_iota(jnp.int32, (2 * TOP_K, tm), 0)
    idx_out = jnp.zeros((2 * TOP_K, tm), jnp.int32)
    gate_out = jnp.zeros((2 * TOP_K, tm), F32)
    for k in range(TOP_K):
        pos_k = jnp.sum(jnp.where(hots[k], slot, 0.0), axis=0, keepdims=True).astype(jnp.int32)
        idx_out = jnp.where(row8 == k, firsts[k], idx_out)
        idx_out = jnp.where(row8 == TOP_K + k, pos_k, idx_out)
        gate_out = jnp.where(row8 == k, ex[k] / denom, gate_out)
    idx_ref[...] = idx_out
    gate_ref[...] = gate_out
    run = run_ref[:, 0:1]
    tcnt_ref[...] = jnp.broadcast_to(tile_cnt, tcnt_ref.shape)
    toff_ref[...] = jnp.broadcast_to(tile_off, toff_ref.shape)
    trun_ref[...] = jnp.broadcast_to(run, trun_ref.shape)
    total = run + tile_cnt
    run_ref[...] = jnp.broadcast_to(total, run_ref.shape)
    cnt_ref[...] = jnp.broadcast_to(total, cnt_ref.shape)


def _route(logits_t):
    t = logits_t.shape[1]
    tm = min(MOE_TILE, t)
    nt = t // tm
    per_tile = pl.BlockSpec((N_EXPERTS, LANES), lambda i: (i, 0))
    per_tile_shape = jax.ShapeDtypeStruct((nt * N_EXPERTS, LANES), F32)
    return pl.pallas_call(
        _route_kernel,
        grid=(nt,),
        in_specs=[pl.BlockSpec((N_EXPERTS, tm), lambda i: (0, i))],
        out_specs=[pl.BlockSpec((2 * TOP_K, tm), lambda i: (0, i)),
                   pl.BlockSpec((2 * TOP_K, tm), lambda i: (0, i)),
                   per_tile, per_tile, per_tile,
                   pl.BlockSpec((N_EXPERTS, LANES), lambda i: (0, 0))],
        out_shape=[jax.ShapeDtypeStruct((2 * TOP_K, t), jnp.int32),
                   jax.ShapeDtypeStruct((2 * TOP_K, t), F32),
                   per_tile_shape, per_tile_shape, per_tile_shape,
                   jax.ShapeDtypeStruct((N_EXPERTS, LANES), F32)],
        scratch_shapes=[pltpu.VMEM((N_EXPERTS, LANES), F32)],
        compiler_params=_cp(("arbitrary",), 32),
        name="route",
    )(logits_t)


def _rows(ref, start, n):
    return ref.at[pl.ds(pl.multiple_of(start * ROW_TILES, ROW_TILES), n * ROW_TILES), :]


def _run_copies(tile, cnt_ref, off_ref, dst_ref, make_copy):
    def per_expert(e, carry):
        j = tile * N_EXPERTS + e
        cnt, off, dst = cnt_ref[j], off_ref[j], dst_ref[j]

        for b in reversed(range(RUN_BITS)):
            size = 1 << b

            @pl.when((cnt & size) != 0)
            def _():
                done = lax.shift_left(lax.shift_right_logical(cnt, b + 1), b + 1)
                make_copy(off + done, dst + done, size).start()
        return carry

    lax.fori_loop(0, N_EXPERTS, per_expert, 0)


def _dispatch_kernel(fs_ref, fl_ref, cnt_ref, off_ref, dst_ref, h_ref, idx_ref, xs_ref,
                     stage, zero_buf, sems, zsem):
    i = pl.program_id(0)
    n_steps = pl.num_programs(0)
    cur = lax.rem(i, 2)
    tm = h_ref.shape[0]
    n_sorted = TOP_K * tm
    n_fill = fs_ref.shape[0]

    def fill_copies(f, wait):
        start, n = fs_ref[f], fl_ref[f]
        n_chunks = lax.shift_right_logical(n, FILL_SHIFT)
        tail = start + n_chunks * FILL_ROWS

        def chunk(j, carry):
            cp = pltpu.make_async_copy(zero_buf, _rows(xs_ref, start + j * FILL_ROWS, FILL_ROWS),
                                       zsem)
            cp.wait() if wait else cp.start()
            return carry

        def single(j, carry):
            cp = pltpu.make_async_copy(_rows(zero_buf, 0, 1), _rows(xs_ref, tail + j, 1), zsem)
            cp.wait() if wait else cp.start()
            return carry

        lax.fori_loop(0, n_chunks, chunk, 0)
        lax.fori_loop(0, n - n_chunks * FILL_ROWS, single, 0)

    @pl.when(i == 0)
    def _():
        zero_buf[...] = jnp.zeros_like(zero_buf)
        lax.fori_loop(0, n_fill, lambda f, c: (fill_copies(f, False), c)[1], 0)

    h = h_ref[...]
    pos = [idx_ref[TOP_K + k:TOP_K + k + 1, :] for k in range(TOP_K)]
    for sb in range(n_sorted // SORT_SLAB):
        j = lax.broadcasted_iota(jnp.int32, (SORT_SLAB, tm), 0) + sb * SORT_SLAB
        hit = (pos[0] == j) | (pos[1] == j) | (pos[2] == j) | (pos[3] == j)
        perm = jnp.where(hit, 1.0, 0.0).astype(BF16)
        rows = _dot(perm, h)
        for c in range(ROW_TILES):
            stage[cur, pl.ds(sb * SORT_SLAB * ROW_TILES + c, SORT_SLAB, stride=ROW_TILES), :] = (
                rows[:, c * LANES:(c + 1) * LANES])

    def make_copy(src_row, dst_row, size):
        return pltpu.make_async_copy(_rows(stage.at[cur], src_row, size),
                                     _rows(xs_ref, dst_row, size), sems.at[cur])

    _run_copies(i, cnt_ref, off_ref, dst_ref, make_copy)

    def wait_slot(slot):
        pltpu.make_async_copy(stage.at[slot], _rows(xs_ref, 0, n_sorted), sems.at[slot]).wait()

    @pl.when(i > 0)
    def _():
        wait_slot(1 - cur)

    @pl.when(i == n_steps - 1)
    def _():
        wait_slot(cur)

    @pl.when(i == 0)
    def _():
        lax.fori_loop(0, n_fill, lambda f, c: (fill_copies(f, True), c)[1], 0)


def _dispatch(h2, idx, tile_cnt, tile_off, tile_dst, fill_start, fill_len, n_rows):
    t, d = h2.shape
    tm = min(MOE_TILE, t)
    grid_spec = pltpu.PrefetchScalarGridSpec(
        num_scalar_prefetch=5,
        grid=(t // tm,),
        in_specs=[pl.BlockSpec((tm, d), lambda i, *_: (i, 0)),
                  pl.BlockSpec((2 * TOP_K, tm), lambda i, *_: (0, i))],
        out_specs=pl.BlockSpec(memory_space=pl.ANY),
        scratch_shapes=[pltpu.VMEM((2, TOP_K * tm * ROW_TILES, LANES), F32),
                        pltpu.VMEM((FILL_ROWS * ROW_TILES, LANES), F32),
                        pltpu.SemaphoreType.DMA((2,)), pltpu.SemaphoreType.DMA(())],
    )
    return pl.pallas_call(
        _dispatch_kernel,
        grid_spec=grid_spec,
        out_shape=jax.ShapeDtypeStruct((n_rows * ROW_TILES, LANES), F32),
        compiler_params=_cp(("arbitrary",), 48),
        name="dispatch",
    )(fill_start, fill_len, tile_cnt, tile_off, tile_dst, h2, idx)


def _expert_kernel(layer, be_ref, nu_ref, nx_ref, nv_ref, xs_ref, wi_hbm, bi_ref, wo_hbm, bo_ref,
                   y_ref, wi_st, wo_st, wi_b, wo_b, sems):
    i = pl.program_id(0)
    tb = xs_ref.shape[0] // ROW_TILES
    e = be_ref[i]
    fresh = jnp.logical_or(i == 0, e != be_ref[jnp.maximum(i - 1, 0)])
    used = i < nu_ref[0]

    def fetch(ex):
        return (pltpu.make_async_copy(wi_hbm.at[layer, ex], wi_st, sems.at[0]),
                pltpu.make_async_copy(wo_hbm.at[layer, ex], wo_st, sems.at[1]))

    @pl.when(i == 0)
    def _():
        for cp in fetch(e):
            cp.start()

    @pl.when(jnp.logical_and(fresh, used))
    def _():
        for cp in fetch(e):
            cp.wait()
        wi_b[...] = wi_st[...].astype(BF16)
        wo_b[...] = wo_st[...].astype(BF16)

        @pl.when(nx_ref[i] >= 0)
        def _():
            for cp in fetch(nx_ref[i]):
                cp.start()

    def compute(n):
        x = jnp.concatenate(
            [xs_ref[pl.ds(c, n, stride=ROW_TILES), :] for c in range(ROW_TILES)],
            axis=1).astype(BF16)
        y = jnp.zeros((n, D_MODEL), F32) + bo_ref[0, 0]
        half = 1024
        for j in range(D_FF // half):
            a, b = j * half, (j + 1) * half
            glu = _dot(x, wi_b[:, a:b]) + bi_ref[0, 0, :, a:b]
            lin = _dot(x, wi_b[:, D_FF + a:D_FF + b]) + bi_ref[0, 0, :, D_FF + a:D_FF + b]
            glu = jnp.minimum(glu, SWIGLU_LIMIT)
            lin = jnp.clip(lin, -SWIGLU_LIMIT, SWIGLU_LIMIT)
            act = glu * jax.nn.sigmoid(SWIGLU_ALPHA * glu) * (lin + 1.0)
            y = y + _dot(act.astype(BF16), wo_b[a:b, :])
        for c in range(ROW_TILES):
            y_ref[pl.ds(c, n, stride=ROW_TILES), :] = y[:, c * LANES:(c + 1) * LANES]

    full = nv_ref[i] > tb // 2

    @pl.when(jnp.logical_and(used, full))
    def _():
        compute(tb)

    @pl.when(jnp.logical_and(used, jnp.logical_not(full)))
    def _():
        compute(tb // 2)
        y_ref[pl.ds(tb // 2 * ROW_TILES, tb // 2 * ROW_TILES), :] = jnp.zeros(
            (tb // 2 * ROW_TILES, LANES), F32)

    @pl.when(jnp.logical_not(used))
    def _():
        y_ref[...] = jnp.zeros_like(y_ref)


def _experts(xs, block_e, n_used, next_e, n_valid, layer, w_in, b_in, w_out, b_out):
    tb = EXPERT_ROWS
    n_rows = xs.shape[0] // ROW_TILES
    nb = n_rows // tb
    depth, ne, d, f2 = w_in.shape

    def row_map(i, be, nu, nx, nv):
        return (jnp.minimum(i, nu[0] - 1), 0)

    grid_spec = pltpu.PrefetchScalarGridSpec(
        num_scalar_prefetch=4,
        grid=(nb,),
        in_specs=[
            pl.BlockSpec((tb * ROW_TILES, LANES), row_map),
            pl.BlockSpec(memory_space=pl.ANY),
            pl.BlockSpec((1, 1, 1, f2), lambda i, be, nu, nx, nv: (layer, be[i], 0, 0)),
            pl.BlockSpec(memory_space=pl.ANY),
            pl.BlockSpec((1, 1, 1, d), lambda i, be, nu, nx, nv: (layer, be[i], 0, 0)),
        ],
        out_specs=pl.BlockSpec((tb * ROW_TILES, LANES), lambda i, be, nu, nx, nv: (i, 0)),
        scratch_shapes=[pltpu.VMEM((d, f2), F32), pltpu.VMEM((D_FF, d), F32),
                        pltpu.VMEM((d, f2), BF16), pltpu.VMEM((D_FF, d), BF16),
                        pltpu.SemaphoreType.DMA((2,))],
    )
    return pl.pallas_call(
        functools.partial(_expert_kernel, layer),
        grid_spec=grid_spec,
        out_shape=jax.ShapeDtypeStruct((n_rows * ROW_TILES, LANES), F32),
        compiler_params=_cp(("arbitrary",), 56),
        name="experts",
    )(block_e, n_used, next_e, n_valid, xs, w_in, b_in.reshape(depth, ne, 1, f2), w_out,
      b_out.reshape(depth, ne, 1, d))


def _combine_kernel(final, cnt_ref, off_ref, dst_ref, pos_ref, gate_ref, x_ref, mod_ref, fg_ref,
                    yb_ref, o_ref, stage, sems):
    tm = x_ref.shape[0]
    n_sorted = TOP_K * tm
    i = pl.program_id(0)
    cur = lax.rem(i, 2)

    def fetch(tile, slot):
        def make_copy(sorted_row, src_row, size):
            return pltpu.make_async_copy(_rows(yb_ref, src_row, size),
                                         _rows(stage.at[slot], sorted_row, size), sems.at[slot])

        _run_copies(tile, cnt_ref, off_ref, dst_ref, make_copy)

    @pl.when(i == 0)
    def _():
        fetch(0, 0)

    @pl.when(i + 1 < pl.num_programs(0))
    def _():
        fetch(i + 1, 1 - cur)

    pltpu.make_async_copy(_rows(yb_ref, 0, n_sorted), stage.at[cur], sems.at[cur]).wait()

    pos = [pos_ref[:, TOP_K + k:TOP_K + k + 1] for k in range(TOP_K)]
    gate = [gate_ref[:, k:k + 1] for k in range(TOP_K)]
    y = jnp.zeros((tm, D_MODEL), F32)
    for sb in range(n_sorted // SORT_SLAB):
        rows = jnp.concatenate(
            [stage[cur, pl.ds(sb * SORT_SLAB * ROW_TILES + c, SORT_SLAB, stride=ROW_TILES), :]
             for c in range(ROW_TILES)], axis=1).astype(BF16)
        j = lax.broadcasted_iota(jnp.int32, (tm, SORT_SLAB), 1) + sb * SORT_SLAB
        w = jnp.zeros((tm, SORT_SLAB), F32)
        for k in range(TOP_K):
            w = w + jnp.where(pos[k] == j, gate[k], 0.0)
        y = y + _dot(w.astype(BF16), rows)
    o = x_ref[...] + mod_ref[0, 5:6, :] * y
    if final:
        o = _rms_rows(o) * fg_ref[...]
    o_ref[...] = o


def _combine(yb, tile_cnt, tile_off, tile_dst, pos_t, gates_t, x2, mod, final_g, final, seq):
    t, d = x2.shape
    tm = min(MOE_TILE, seq)
    grid_spec = pltpu.PrefetchScalarGridSpec(
        num_scalar_prefetch=3,
        grid=(t // tm,),
        in_specs=[
            pl.BlockSpec((tm, 2 * TOP_K), lambda i, *_: (i, 0)),
            pl.BlockSpec((tm, 2 * TOP_K), lambda i, *_: (i, 0)),
            pl.BlockSpec((tm, d), lambda i, *_: (i, 0)),
            pl.BlockSpec((1, 6, d), lambda i, *_: (i // (seq // tm), 0, 0)),
            pl.BlockSpec((1, d), lambda i, *_: (0, 0)),
            pl.BlockSpec(memory_space=pl.ANY),
        ],
        out_specs=pl.BlockSpec((tm, d), lambda i, *_: (i, 0)),
        scratch_shapes=[pltpu.VMEM((2, TOP_K * tm * ROW_TILES, LANES), F32),
                        pltpu.SemaphoreType.DMA((2,))],
    )
    return pl.pallas_call(
        functools.partial(_combine_kernel, final),
        grid_spec=grid_spec,
        out_shape=jax.ShapeDtypeStruct((t, d), F32),
        compiler_params=_cp(("arbitrary",), 48),
        name="combine",
    )(tile_cnt, tile_off, tile_dst, pos_t, gates_t, x2, mod, final_g.reshape(1, d), yb)


def _moe(h2, logits, x2, mod, layer, w_in, b_in, w_out, b_out, final_g, final, seq):
    t = x2.shape[0]
    tb = EXPERT_ROWS
    idx, gates, tcnt, toff, trun, cnt = _route(logits)
    counts = cnt[:, 0].astype(jnp.int32)
    nblk = (counts + tb - 1) // tb
    blk_end = jnp.cumsum(nblk)
    pad_start = (blk_end - nblk) * tb
    n_blocks = (t * TOP_K) // tb + N_EXPERTS
    n_used = blk_end[-1:]
    experts = jnp.arange(N_EXPERTS, dtype=jnp.int32)
    last_e = jnp.max(jnp.where(nblk > 0, experts, 0))
    blk = jnp.arange(n_blocks, dtype=jnp.int32)
    block_e = jnp.minimum(
        jnp.sum((blk_end[None, :] <= blk[:, None]).astype(jnp.int32), axis=1), last_e)
    later = (experts[None, :] > block_e[:, None]) & (nblk[None, :] > 0)
    next_e = jnp.min(jnp.where(later, experts[None, :], N_EXPERTS), axis=1)
    next_e = jnp.where(next_e == N_EXPERTS, -1, next_e).astype(jnp.int32)
    tile_cnt = tcnt[:, 0].astype(jnp.int32)
    tile_off = toff[:, 0].astype(jnp.int32)
    tile_dst = (trun[:, 0].astype(jnp.int32).reshape(-1, N_EXPERTS) + pad_start[None, :]).reshape(-1)
    fill_start = jnp.concatenate([pad_start + counts, blk_end[-1:] * tb]).astype(jnp.int32)
    fill_len = jnp.concatenate([nblk * tb - counts, (n_blocks - blk_end[-1:]) * tb]).astype(jnp.int32)
    xs = _dispatch(h2, idx, tile_cnt, tile_off, tile_dst, fill_start, fill_len, n_blocks * tb)
    row_end = jnp.sum(jnp.where(block_e[:, None] == experts[None, :],
                                (pad_start + counts)[None, :], 0), axis=1)
    n_valid = jnp.clip(row_end - blk * tb, 0, tb).astype(jnp.int32)
    yb = _experts(xs, block_e, n_used.astype(jnp.int32), next_e, n_valid, layer,
                  w_in, b_in, w_out, b_out)
    return _combine(yb, tile_cnt, tile_off, tile_dst, idx.T, gates.T, x2, mod, final_g, final, seq)


def kernel(x, c, positions, w_ada, b_ada, even_w_in, gla_w_gate, gla_b_gate, gla_norm_g,
           diff_lam_q1, diff_lam_k1, diff_lam_q2, diff_lam_k2, diff_norm_g, even_w_out,
           odd_w_in, sgu_ln_g, sgu_ln_b, sgu_w, sgu_b, odd_w_out,
           router_w, router_b, expert_w_in, expert_b_in, expert_w_out, expert_b_out,
           final_norm_g):
    bsz, seq, d = x.shape
    depth = w_ada.shape[0]
    t = bsz * seq
    mods = _modulation(c, w_ada, b_ada).reshape(depth, bsz, 6, d)
    cos, sin = _rope_tables(positions)
    x2 = x.reshape(t, d)
    for layer in range(depth):
        mod = mods[layer]
        j = layer // 2
        if layer % 2 == 0:
            z, f = _proj_even(x2, mod, even_w_in[j], gla_w_gate[j], gla_b_gate[j], cos, sin, seq)
            o_gla = _gla(z, f, gla_norm_g[j], bsz, seq)
            lam_init = 0.8 - 0.6 * math.exp(-0.3 * layer)
            o_diff = _diff_attn(z, diff_lam_q1[j], diff_lam_k1[j], diff_lam_q2[j], diff_lam_k2[j],
                                diff_norm_g[j], lam_init, bsz, seq)
            mixed, w_out = (o_gla, o_diff), even_w_out[j]
        else:
            z, f = _proj_odd(x2, mod, odd_w_in[j], sgu_ln_g[j], sgu_ln_b[j], cos, sin, seq)
            mixed, w_out = (_sgu_ret(z, f, sgu_w[j], sgu_b[j], bsz, seq),), odd_w_out[j]
        x2, h2, logits = _out_proj(mixed, w_out, x2, mod, router_w[layer], router_b[layer], seq)
        x2 = _moe(h2, logits, x2, mod, layer, expert_w_in, expert_b_in, expert_w_out,
                  expert_b_out, final_norm_g, layer == depth - 1, seq)
    return x2.reshape(bsz, seq, d)
```

```python
import functools
import math

import jax
import jax.numpy as jnp
from jax import lax
from jax.experimental import pallas as pl
from jax.experimental.pallas import tpu as pltpu

F32 = jnp.float32
BF16 = jnp.bfloat16

D_MODEL = 1024
EPS = 1e-6
ROPE_THETA = 10000.0
ROPE_HALF = 32

GLA_HEADS = 4
GLA_DK = 64
GLA_DV = 128
GLA_RANK = 16
GLA_CHUNK = 64
GLA_GATE_NORMALIZER = 16.0
GLA_QK_W = GLA_HEADS * GLA_DK
GLA_V_W = GLA_HEADS * GLA_DV

DIFF_HEADS = 4
DIFF_D = 64
DIFF_DV = 128
DIFF_QK_W = DIFF_HEADS * 2 * DIFF_D
DIFF_V_W = DIFF_HEADS * DIFF_DV

SGU_GROUPS = 4
SGU_CH = 128
SGU_CHUNK = 128
SGU_W = SGU_GROUPS * SGU_CH

RET_HEADS = 4
RET_DK = 64
RET_DV = 128
RET_CHUNK = 128
RET_QK_W = RET_HEADS * RET_DK
RET_V_W = RET_HEADS * RET_DV

N_EXPERTS = 32
TOP_K = 4
D_FF = D_MODEL
SWIGLU_ALPHA = 1.702
SWIGLU_LIMIT = 7.0

LANES = 128
SUBLANES = 8
ROW_TILES = D_MODEL // LANES

PROJ_ROWS = 1024
GLA_ROWS = 512
GLA_SEQS_PER_STEP = 2
GLA_GROUP = 4
SGU_RET_ROWS = 512
ATT_Q_ROWS = 256
ATT_HEADS_PER_STEP = 4
MOE_TILE = 512
SORT_SLAB = 512
RUN_BITS = 10
FILL_SHIFT = 6
FILL_ROWS = 1 << FILL_SHIFT
EXPERT_ROWS = 512
MOD_COLS = 1536

MIB = 1024 * 1024


def _cp(semantics, vmem_mib):
    return pltpu.CompilerParams(dimension_semantics=semantics, vmem_limit_bytes=vmem_mib * MIB)


def _dot(a, b):
    return jnp.dot(a, b, preferred_element_type=F32)


def _dot_nt(a, b):
    return lax.dot_general(a, b, (((1,), (1,)), ((), ())), preferred_element_type=F32)


def _dot_tn(a, b):
    return lax.dot_general(a, b, (((0,), (0,)), ((), ())), preferred_element_type=F32)


def _rms_rows(x):
    return x * lax.rsqrt(jnp.mean(x * x, axis=-1, keepdims=True) + EPS)


def _silu(x):
    return x * jax.nn.sigmoid(x)


def _rope_chunk(v, cos, sin, lo_half):
    rot = jnp.where(lo_half, -pltpu.roll(v, 96, 1), pltpu.roll(v, 32, 1))
    return v * cos + rot * sin


def _mod_kernel(c_ref, w_ref, b_ref, o_ref):
    c = c_ref[...]
    ca = _silu(c).astype(BF16)
    o_ref[0] = _dot(ca, w_ref[0].astype(BF16)) + b_ref[0]


def _modulation(c, w_ada, b_ada):
    depth, d, n = w_ada.shape
    bsz = c.shape[0]
    return pl.pallas_call(
        _mod_kernel,
        grid=(depth, n // MOD_COLS),
        in_specs=[
            pl.BlockSpec((bsz, d), lambda l, j: (0, 0)),
            pl.BlockSpec((1, d, MOD_COLS), lambda l, j: (l, 0, j)),
            pl.BlockSpec((1, 1, MOD_COLS), lambda l, j: (l, 0, j)),
        ],
        out_specs=pl.BlockSpec((1, bsz, MOD_COLS), lambda l, j: (l, 0, j)),
        out_shape=jax.ShapeDtypeStruct((depth, bsz, n), F32),
        compiler_params=_cp(("arbitrary", "arbitrary"), 40),
        name="adaln_mod",
    )(c, w_ada, b_ada.reshape(depth, 1, n))


def _rope_table_kernel(p_ref, f_ref, c_ref, s_ref):
    ang = p_ref[...].astype(F32) * f_ref[...]
    c_ref[...] = jnp.cos(ang)
    s_ref[...] = jnp.sin(ang)


def _rope_tables(positions):
    t = positions.size
    per_row = LANES // ROPE_HALF
    rows = t // per_row
    pos_d = jnp.repeat(positions.reshape(rows, per_row), ROPE_HALF, axis=1)
    inv_freq = ROPE_THETA ** (-jnp.arange(ROPE_HALF, dtype=F32) / ROPE_HALF)
    freq_d = jnp.tile(inv_freq, per_row).reshape(1, LANES)
    tr = min(512, rows)
    cos_d, sin_d = pl.pallas_call(
        _rope_table_kernel,
        grid=(rows // tr,),
        in_specs=[pl.BlockSpec((tr, LANES), lambda i: (i, 0)),
                  pl.BlockSpec((1, LANES), lambda i: (0, 0))],
        out_specs=[pl.BlockSpec((tr, LANES), lambda i: (i, 0))] * 2,
        out_shape=[jax.ShapeDtypeStruct((rows, LANES), F32)] * 2,
        compiler_params=_cp(("arbitrary",), 32),
        name="rope_tables",
    )(pos_d, freq_d)
    cos = jnp.tile(cos_d.reshape(t, ROPE_HALF), (1, per_row))
    sin = jnp.tile(sin_d.reshape(t, ROPE_HALF), (1, per_row))
    return cos, sin


def _modulated_rms(x_ref, mod_ref, which):
    x = x_ref[...]
    shift = mod_ref[0, 3 * which:3 * which + 1, :]
    scale = mod_ref[0, 3 * which + 1:3 * which + 2, :]
    return _rms_rows(x) * (1.0 + scale) + shift


EVEN_Z = (GLA_V_W, GLA_V_W, DIFF_QK_W, DIFF_QK_W, DIFF_V_W)
EVEN_Z_W = sum(EVEN_Z)
EVEN_F_W = 3 * GLA_QK_W
EVEN_W_COLS = EVEN_Z_W + 2 * GLA_QK_W + LANES


def _proj_even_kernel(x_ref, mod_ref, w_ref, wg_ref, bg_ref, cos_ref, sin_ref, z_ref, f_ref):
    hb = _modulated_rms(x_ref, mod_ref, 0).astype(BF16)

    def sec(a, b):
        return _dot(hb, w_ref[:, a:b])

    for a, b in ((0, 512), (512, 1024), (2048, 2560)):
        z_ref[:, a:b] = sec(a, b).astype(BF16)
    cos = cos_ref[...]
    sin = sin_ref[...]
    lo_half = (lax.broadcasted_iota(jnp.int32, cos.shape, 1) % 64) < ROPE_HALF
    q_scale = DIFF_D ** -0.5 * math.log2(math.e)
    for a, scl in ((1024, q_scale), (1536, 1.0)):
        full = sec(a, a + DIFF_QK_W)
        for c in range(DIFF_QK_W // LANES):
            v = full[:, c * LANES:(c + 1) * LANES]
            z_ref[:, a + c * LANES:a + (c + 1) * LANES] = (
                _rope_chunk(v, cos, sin, lo_half) * scl).astype(BF16)
    f_ref[:, 0:256] = sec(2560, 2816) * (GLA_DK ** -0.5)
    f_ref[:, 256:512] = sec(2816, 3072)
    gr = sec(3072, 3200).astype(BF16)
    pre = _dot(gr, wg_ref[...]) + bg_ref[...]
    log_sig = jnp.minimum(pre, 0.0) - jnp.log1p(jnp.exp(-jnp.abs(pre)))
    f_ref[:, 512:768] = log_sig / GLA_GATE_NORMALIZER


def _proj_even(x2, mod, w_in, w_gate, b_gate, cos, sin, seq):
    t, d = x2.shape
    tm = min(PROJ_ROWS, seq)
    gq, gk, gv, gr, gg, dq, dk, dv = jnp.split(
        w_in, [256, 512, 1024, 1040, 1552, 2064, 2576], axis=1)
    gr_pad = jnp.pad(gr, ((0, 0), (0, LANES - GLA_RANK)))
    w_cat = jnp.concatenate([gv, gg, dq, dk, dv, gq, gk, gr_pad], axis=1).astype(BF16)
    wg_pad = jnp.pad(w_gate, ((0, LANES - GLA_RANK), (0, 0))).astype(BF16)
    return pl.pallas_call(
        _proj_even_kernel,
        grid=(t // tm,),
        in_specs=[
            pl.BlockSpec((tm, d), lambda i: (i, 0)),
            pl.BlockSpec((1, 6, d), lambda i: (i // (seq // tm), 0, 0)),
            pl.BlockSpec((d, EVEN_W_COLS), lambda i: (0, 0)),
            pl.BlockSpec((LANES, GLA_QK_W), lambda i: (0, 0)),
            pl.BlockSpec((1, GLA_QK_W), lambda i: (0, 0)),
            pl.BlockSpec((tm, LANES), lambda i: (i, 0)),
            pl.BlockSpec((tm, LANES), lambda i: (i, 0)),
        ],
        out_specs=[pl.BlockSpec((tm, EVEN_Z_W), lambda i: (i, 0)),
                   pl.BlockSpec((tm, EVEN_F_W), lambda i: (i, 0))],
        out_shape=[jax.ShapeDtypeStruct((t, EVEN_Z_W), BF16),
                   jax.ShapeDtypeStruct((t, EVEN_F_W), F32)],
        compiler_params=_cp(("arbitrary",), 48),
        name="proj_even",
    )(x2, mod, w_cat, wg_pad, b_gate.reshape(1, GLA_QK_W), cos, sin)


ODD_Z_W = 2 * SGU_W + 2 * RET_V_W
ODD_F_W = 2 * RET_QK_W
ODD_W_COLS = ODD_Z_W + ODD_F_W


def _gelu_exact(x):
    return 0.5 * x * (1.0 + lax.erf(x * (2.0 ** -0.5)))


def _proj_odd_kernel(x_ref, mod_ref, w_ref, lng_ref, lnb_ref, cos_ref, sin_ref, z_ref, f_ref):
    hb = _modulated_rms(x_ref, mod_ref, 0).astype(BF16)

    def sec(a, b):
        return _dot(hb, w_ref[:, a:b])

    z_ref[:, 0:512] = _gelu_exact(sec(0, 512)).astype(BF16)
    sv = _gelu_exact(sec(512, 1024))
    mu = jnp.mean(sv, axis=-1, keepdims=True)
    cen = sv - mu
    var = jnp.mean(cen * cen, axis=-1, keepdims=True)
    z_ref[:, 512:1024] = (cen * lax.rsqrt(var + EPS) * lng_ref[...] + lnb_ref[...]).astype(BF16)
    for a, b in ((1024, 1536), (1536, 2048)):
        z_ref[:, a:b] = sec(a, b).astype(BF16)
    cos = cos_ref[...]
    sin = sin_ref[...]
    lo_half = (lax.broadcasted_iota(jnp.int32, cos.shape, 1) % 64) < ROPE_HALF
    k_scale = RET_DK ** -0.5
    for a, scl in ((0, 1.0), (256, k_scale)):
        full = sec(ODD_Z_W + a, ODD_Z_W + a + RET_QK_W)
        for c in range(RET_QK_W // LANES):
            v = full[:, c * LANES:(c + 1) * LANES]
            f_ref[:, a + c * LANES:a + (c + 1) * LANES] = _rope_chunk(v, cos, sin, lo_half) * scl


def _proj_odd(x2, mod, w_in, ln_g, ln_b, cos, sin, seq):
    t, d = x2.shape
    tm = min(PROJ_ROWS, seq)
    su, sv, rq, rk, rv, rg = jnp.split(w_in, [512, 1024, 1280, 1536, 2048], axis=1)
    w_cat = jnp.concatenate([su, sv, rv, rg, rq, rk], axis=1).astype(BF16)
    return pl.pallas_call(
        _proj_odd_kernel,
        grid=(t // tm,),
        in_specs=[
            pl.BlockSpec((tm, d), lambda i: (i, 0)),
            pl.BlockSpec((1, 6, d), lambda i: (i // (seq // tm), 0, 0)),
            pl.BlockSpec((d, ODD_W_COLS), lambda i: (0, 0)),
            pl.BlockSpec((1, SGU_W), lambda i: (0, 0)),
            pl.BlockSpec((1, SGU_W), lambda i: (0, 0)),
            pl.BlockSpec((tm, LANES), lambda i: (i, 0)),
            pl.BlockSpec((tm, LANES), lambda i: (i, 0)),
        ],
        out_specs=[pl.BlockSpec((tm, ODD_Z_W), lambda i: (i, 0)),
                   pl.BlockSpec((tm, ODD_F_W), lambda i: (i, 0))],
        out_shape=[jax.ShapeDtypeStruct((t, ODD_Z_W), BF16),
                   jax.ShapeDtypeStruct((t, ODD_F_W), F32)],
        compiler_params=_cp(("arbitrary",), 48),
        name="proj_odd",
    )(x2, mod, w_cat, ln_g.reshape(1, SGU_W), ln_b.reshape(1, SGU_W), cos, sin)


def _gla_kernel(q_ref, k_ref, la_ref, v_ref, gg_ref, g_ref, o_ref, st_ref):
    n_seq = q_ref.shape[0]
    n_pairs = GLA_HEADS // 2

    @pl.when(pl.program_id(1) == 0)
    def _():
        st_ref[...] = jnp.zeros_like(st_ref)

    c = GLA_CHUNK
    sc = GLA_GROUP * c
    rows = q_ref.shape[1]
    r_i = lax.broadcasted_iota(jnp.int32, (sc, sc), 0)
    c_i = lax.broadcasted_iota(jnp.int32, (sc, sc), 1)
    block_causal = (r_i // c == c_i // c) & (r_i >= c_i)
    tril = block_causal.astype(BF16)
    causal2 = jnp.concatenate([block_causal, block_causal], axis=0)
    lo = lax.broadcasted_iota(jnp.int32, (sc, LANES), 1) < GLA_DK
    lane_s = lax.broadcasted_iota(jnp.int32, (LANES, LANES), 1) < GLA_DK
    g_row = g_ref[...]
    states = [st_ref[n] for n in range(n_seq * n_pairs)]
    for j, s in [(j, s) for j in range(rows // sc) for s in range(n_seq)]:
        rs = slice(j * sc, (j + 1) * sc)
        la = la_ref[s, rs, :]
        la_hi = la.astype(BF16)
        la_lo = (la - la_hi.astype(F32)).astype(BF16)
        b = _dot(tril, la_hi) + _dot(tril, la_lo)
        b_last = [b[(g + 1) * c - 1:(g + 1) * c, :] for g in range(GLA_GROUP)]
        b_last_rows = jnp.concatenate(
            [jnp.broadcast_to(bl, (c, bl.shape[1])) for bl in b_last], axis=0)
        qd = q_ref[s, rs, :] * jnp.exp(b)
        kk = k_ref[s, rs, :]
        ki = kk * jnp.exp(-b)
        kd = kk * jnp.exp(b_last_rows - b)
        for p in range(GLA_HEADS // 2):
            ls = slice(p * LANES, (p + 1) * LANES)
            qd_p = qd[:, ls]
            qm = (jnp.where(lo, qd_p, 0.0).astype(BF16), jnp.where(lo, 0.0, qd_p).astype(BF16))
            s2 = _dot_nt(jnp.concatenate(qm, axis=0), ki[:, ls].astype(BF16))
            s2 = jnp.where(causal2, s2, 0.0).astype(BF16)
            kd_p = kd[:, ls].astype(BF16)
            heads = (2 * p, 2 * p + 1)
            v = [v_ref[s, rs, h * GLA_DV:(h + 1) * GLA_DV] for h in heads]
            inter = ([], [])
            sp = s * n_pairs + p
            for g in range(GLA_GROUP):
                gs = slice(g * c, (g + 1) * c)
                st_b = states[sp].astype(BF16)
                for hh in range(2):
                    inter[hh].append(_dot_nt(qm[hh][gs, :], st_b))
                new = [_dot_tn(v[hh][gs, :], kd_p[gs, :]) for hh in range(2)]
                states[sp] = (jnp.exp(b_last[g][:, ls]) * states[sp]
                              + jnp.where(lane_s, new[0], new[1]))
            for hh in range(2):
                hs = slice(heads[hh] * GLA_DV, (heads[hh] + 1) * GLA_DV)
                o = _dot(s2[hh * sc:(hh + 1) * sc, :], v[hh]) + jnp.concatenate(inter[hh], axis=0)
                o = _rms_rows(o) * g_row
                gate = _silu(gg_ref[s, rs, hs].astype(F32))
                o_ref[s, rs, hs] = (o * gate).astype(BF16)
    for n in range(n_seq * n_pairs):
        st_ref[n] = states[n]


def _gla(z, f, norm_g, bsz, seq):
    t = z.shape[0]
    tc = min(GLA_ROWS, seq)
    nc = seq // tc
    n_seq = math.gcd(GLA_SEQS_PER_STEP, bsz)
    f3 = f.reshape(bsz, seq, f.shape[1])
    z3 = z.reshape(bsz, seq, z.shape[1])

    def spec(width, col):
        return pl.BlockSpec((n_seq, tc, width), lambda b, i: (b, i, col))

    out = pl.pallas_call(
        _gla_kernel,
        grid=(bsz // n_seq, nc),
        in_specs=[spec(GLA_QK_W, 0), spec(GLA_QK_W, 1), spec(GLA_QK_W, 2),
                  spec(GLA_V_W, 0), spec(GLA_V_W, 1),
                  pl.BlockSpec((1, GLA_DV), lambda b, i: (0, 0))],
        out_specs=spec(GLA_V_W, 0),
        out_shape=jax.ShapeDtypeStruct((bsz, seq, GLA_V_W), BF16),
        scratch_shapes=[pltpu.VMEM((n_seq * (GLA_HEADS // 2), LANES, LANES), F32)],
        compiler_params=_cp(("arbitrary", "arbitrary"), 40),
        name="gla",
    )(f3, f3, f3, z3, z3, norm_g.reshape(1, GLA_DV))
    return out.reshape(t, GLA_V_W)


def _diff_attn_kernel(lam_init, q_ref, k_ref, v_ref, lq1_ref, lk1_ref, lq2_ref, lk2_ref, g_ref,
                      o_ref, vt_ref, s_ref, m_ref, l_ref, acc_ref):
    tq = q_ref.shape[0]
    seq = k_ref.shape[0]
    n_heads = q_ref.shape[1] // LANES
    qi = pl.program_id(2)

    @pl.when(qi == 0)
    def _():
        for hh in range(n_heads):
            for cb in range(seq // tq):
                blk = v_ref[cb * tq:(cb + 1) * tq, hh * LANES:(hh + 1) * LANES]
                vt_ref[hh, :, cb * tq:(cb + 1) * tq] = blk.astype(F32).T.astype(BF16)

    lo = lax.broadcasted_iota(jnp.int32, (tq, LANES), 1) < DIFF_D
    qqs = []
    for hh in range(n_heads):
        q = q_ref[:, hh * LANES:(hh + 1) * LANES]
        zero = jnp.zeros_like(q)
        qqs.append(jnp.concatenate([jnp.where(lo, q, zero), jnp.where(lo, zero, q)], axis=0))

    m_ref[...] = jnp.full(m_ref.shape, -jnp.inf, F32)
    l_ref[...] = jnp.zeros_like(l_ref)
    acc_ref[...] = jnp.zeros_like(acc_ref)

    def scores(j, slot):
        start = pl.multiple_of(j * tq, tq)
        for hh in range(n_heads):
            kj = k_ref[pl.ds(start, tq), hh * LANES:(hh + 1) * LANES]
            s_ref[slot, hh] = _dot_nt(kj, qqs[hh])

    def update(j, slot, masked):
        start = pl.multiple_of(j * tq, tq)
        for hh in range(n_heads):
            vtj = vt_ref[hh, :, pl.ds(start, tq)]
            alpha, p = [], []
            for c in range(2 * tq // LANES):
                cs = slice(c * LANES, (c + 1) * LANES)
                s_c = s_ref[slot, hh, :, cs]
                if masked:
                    kv = lax.broadcasted_iota(jnp.int32, s_c.shape, 0)
                    q_pos = lax.broadcasted_iota(jnp.int32, s_c.shape, 1) + (c * LANES) % tq
                    s_c = jnp.where(kv <= q_pos, s_c, -jnp.inf)
                m_old = m_ref[hh, :, cs]
                m_c = jnp.maximum(m_old, jnp.max(s_c, axis=0, keepdims=True))
                a_c = jnp.exp2(m_old - m_c)
                p_c = jnp.exp2(s_c - m_c)
                m_ref[hh, :, cs] = m_c
                l_ref[hh, :, cs] = a_c * l_ref[hh, :, cs] + jnp.sum(p_c, axis=0, keepdims=True)
                alpha.append(a_c)
                p.append(p_c.astype(BF16))
            acc_ref[hh] = (jnp.concatenate(alpha, axis=1) * acc_ref[hh]
                           + _dot(vtj, jnp.concatenate(p, axis=1)))

    scores(0, 0)

    def body(i, carry):
        j = 2 * i
        scores(j + 1, 1)
        update(j, 0, False)
        scores(j + 2, 0)
        update(j + 1, 1, False)
        return carry

    n_pairs = qi // 2
    lax.fori_loop(0, n_pairs, body, 0)
    j0 = 2 * n_pairs

    @pl.when(qi % 2 == 1)
    def _():
        scores(j0 + 1, 1)
        update(j0, 0, False)
        update(j0 + 1, 1, True)

    @pl.when(qi % 2 == 0)
    def _():
        update(j0, 0, True)

    lam = (jnp.exp(jnp.sum(lq1_ref[...] * lk1_ref[...], axis=-1, keepdims=True))
           - jnp.exp(jnp.sum(lq2_ref[...] * lk2_ref[...], axis=-1, keepdims=True)) + lam_init)
    for hh in range(n_heads):
        l = l_ref[hh]
        acc = acc_ref[hh]
        o12 = acc / l
        o = o12[:, :tq] - lam * o12[:, tq:]
        o = o * lax.rsqrt(jnp.mean(o * o, axis=0, keepdims=True) + EPS)
        o = o * g_ref[...] * (1.0 - lam_init)
        o_ref[:, hh * DIFF_DV:(hh + 1) * DIFF_DV] = o.T.astype(BF16)


def _diff_attn(z, lq1, lk1, lq2, lk2, norm_g, lam_init, bsz, seq):
    t = z.shape[0]
    tq = min(ATT_Q_ROWS, seq)
    nq = seq // tq
    hw = ATT_HEADS_PER_STEP * LANES
    qb, kb, vb = 1024 // hw, 1536 // hw, 2048 // hw
    small = pl.BlockSpec((1, DIFF_D), lambda b, h, i: (0, 0))
    return pl.pallas_call(
        functools.partial(_diff_attn_kernel, lam_init),
        grid=(bsz, DIFF_HEADS // ATT_HEADS_PER_STEP, nq),
        in_specs=[
            pl.BlockSpec((tq, hw), lambda b, h, i: (b * nq + i, qb + h)),
            pl.BlockSpec((seq, hw), lambda b, h, i: (b, kb + h)),
            pl.BlockSpec((seq, hw), lambda b, h, i: (b, vb + h)),
            small, small, small, small,
            pl.BlockSpec((DIFF_DV, 1), lambda b, h, i: (0, 0)),
        ],
        out_specs=pl.BlockSpec((tq, hw), lambda b, h, i: (b * nq + i, h)),
        out_shape=jax.ShapeDtypeStruct((t, DIFF_V_W), BF16),
        scratch_shapes=[pltpu.VMEM((ATT_HEADS_PER_STEP, DIFF_DV, seq), BF16),
                        pltpu.VMEM((2, ATT_HEADS_PER_STEP, tq, 2 * tq), F32),
                        pltpu.VMEM((ATT_HEADS_PER_STEP, 1, 2 * tq), F32),
                        pltpu.VMEM((ATT_HEADS_PER_STEP, 1, 2 * tq), F32),
                        pltpu.VMEM((ATT_HEADS_PER_STEP, DIFF_DV, 2 * tq), F32)],
        compiler_params=_cp(("arbitrary", "arbitrary", "arbitrary"), 32),
        name="diff_attn",
    )(z, z, z, lq1.reshape(1, DIFF_D), lk1.reshape(1, DIFF_D), lq2.reshape(1, DIFF_D),
      lk2.reshape(1, DIFF_D), norm_g.reshape(DIFF_DV, 1))


def _sgu_ret_kernel(su_ref, sv_ref, rv_ref, rg_ref, q_ref, k_ref, ws_ref, bs_ref, o_ref, st_ref):
    @pl.when(pl.program_id(1) == 0)
    def _():
        st_ref[...] = jnp.zeros_like(st_ref)

    c = RET_CHUNK
    row = lax.broadcasted_iota(jnp.int32, (c, c), 0)
    col = lax.broadcasted_iota(jnp.int32, (c, c), 1)
    causal = row >= col
    log_g = [math.log(1.0 - 2.0 ** (-5.0 - h)) for h in range(RET_HEADS)]
    lo = col < RET_DK
    rel = (row - col).astype(F32)
    pos = row.astype(F32)
    w_sgu = [jnp.where(causal, ws_ref[g], 0.0).astype(BF16) for g in range(SGU_GROUPS)]
    decays = [jnp.where(causal, jnp.exp(log_g[h] * jnp.maximum(rel, 0.0)), 0.0)
              for h in range(RET_HEADS)]
    lgs = [jnp.where(lo, log_g[2 * p], log_g[2 * p + 1]) for p in range(RET_HEADS // 2)]
    q_decs = [jnp.exp(lg * (pos + 1.0)) for lg in lgs]
    k_decs = [jnp.exp(lg * (c - 1.0 - pos)) for lg in lgs]
    states = [st_ref[p] for p in range(RET_HEADS // 2)]
    for j in range(su_ref.shape[0] // c):
        rs = slice(j * c, (j + 1) * c)
        for g in range(SGU_GROUPS):
            gs = slice(g * SGU_CH, (g + 1) * SGU_CH)
            s = _dot(w_sgu[g], sv_ref[rs, gs]) + bs_ref[g]
            o_ref[rs, gs] = (su_ref[rs, gs].astype(F32) * s).astype(BF16)
        for p in range(RET_HEADS // 2):
            ls = slice(p * LANES, (p + 1) * LANES)
            q_p = q_ref[rs, ls]
            k_p = k_ref[rs, ls]
            qm = (jnp.where(lo, q_p, 0.0).astype(BF16), jnp.where(lo, 0.0, q_p).astype(BF16))
            s2 = _dot_nt(jnp.concatenate(qm, axis=0), k_p.astype(BF16))
            qd = q_p * q_decs[p]
            qdm = (jnp.where(lo, qd, 0.0).astype(BF16), jnp.where(lo, 0.0, qd).astype(BF16))
            kd = (k_p * k_decs[p]).astype(BF16)
            st_b = states[p].astype(BF16)
            new = []
            for hh in range(2):
                h = 2 * p + hh
                hs = slice(h * RET_DV, (h + 1) * RET_DV)
                s_h = (s2[hh * c:(hh + 1) * c, :] * decays[h]).astype(BF16)
                v_h = rv_ref[rs, hs]
                o = _dot(s_h, v_h) + _dot_nt(qdm[hh], st_b)
                gate = _silu(rg_ref[rs, hs].astype(F32))
                o_ref[rs, SGU_W + h * RET_DV:SGU_W + (h + 1) * RET_DV] = (
                    _rms_rows(o) * gate).astype(BF16)
                new.append(_dot_tn(v_h, kd))
            states[p] = jnp.exp(lgs[p] * float(c)) * states[p] + jnp.where(lo, new[0], new[1])
    for p in range(RET_HEADS // 2):
        st_ref[p] = states[p]


def _sgu_ret(z, f, w_s, b_s, bsz, seq):
    t = z.shape[0]
    cc = RET_CHUNK
    c = min(SGU_RET_ROWS, seq)
    nc = seq // c
    return pl.pallas_call(
        _sgu_ret_kernel,
        grid=(bsz, nc),
        in_specs=[
            pl.BlockSpec((c, SGU_W), lambda b, i: (b * nc + i, 0)),
            pl.BlockSpec((c, SGU_W), lambda b, i: (b * nc + i, 1)),
            pl.BlockSpec((c, RET_V_W), lambda b, i: (b * nc + i, 2)),
            pl.BlockSpec((c, RET_V_W), lambda b, i: (b * nc + i, 3)),
            pl.BlockSpec((c, RET_QK_W), lambda b, i: (b * nc + i, 0)),
            pl.BlockSpec((c, RET_QK_W), lambda b, i: (b * nc + i, 1)),
            pl.BlockSpec((SGU_GROUPS, cc, cc), lambda b, i: (0, 0, 0)),
            pl.BlockSpec((SGU_GROUPS, cc, 1), lambda b, i: (0, 0, 0)),
        ],
        out_specs=pl.BlockSpec((c, SGU_W + RET_V_W), lambda b, i: (b * nc + i, 0)),
        out_shape=jax.ShapeDtypeStruct((t, SGU_W + RET_V_W), BF16),
        scratch_shapes=[pltpu.VMEM((2, LANES, LANES), F32)],
        compiler_params=_cp(("arbitrary", "arbitrary"), 32),
        name="sgu_retention",
    )(z, z, z, z, f, f, w_s, b_s.reshape(SGU_GROUPS, cc, 1))


def _out_proj_kernel(n_in, *refs):
    o_refs = refs[:n_in]
    w_ref, x_ref, mod_ref, rw_ref, rb_ref, xn_ref, h_ref, lg_ref = refs[n_in:]
    k_each = D_MODEL // n_in
    y = _dot(o_refs[0][...], w_ref[0:k_each, :])
    for n in range(1, n_in):
        y = y + _dot(o_refs[n][...], w_ref[n * k_each:(n + 1) * k_each, :])
    xn = x_ref[...] + mod_ref[0, 2:3, :] * y
    xn_ref[...] = xn
    h = _rms_rows(xn) * (1.0 + mod_ref[0, 4:5, :]) + mod_ref[0, 3:4, :]
    hb = h.astype(BF16)
    h_ref[...] = hb
    lg_ref[...] = _dot_nt(rw_ref[...], hb) + rb_ref[...]


def _out_proj(mixed, w_out, x2, mod, router_w, router_b, seq):
    t, d = x2.shape
    tm = min(PROJ_ROWS, seq)
    n_in = len(mixed)
    k_each = d // n_in
    rw = router_w.T.astype(BF16)
    rb = router_b.reshape(N_EXPERTS, 1)
    return pl.pallas_call(
        functools.partial(_out_proj_kernel, n_in),
        grid=(t // tm,),
        in_specs=[pl.BlockSpec((tm, k_each), lambda i: (i, 0)) for _ in mixed] + [
            pl.BlockSpec((d, d), lambda i: (0, 0)),
            pl.BlockSpec((tm, d), lambda i: (i, 0)),
            pl.BlockSpec((1, 6, d), lambda i: (i // (seq // tm), 0, 0)),
            pl.BlockSpec((N_EXPERTS, d), lambda i: (0, 0)),
            pl.BlockSpec((N_EXPERTS, 1), lambda i: (0, 0)),
        ],
        out_specs=[pl.BlockSpec((tm, d), lambda i: (i, 0)),
                   pl.BlockSpec((tm, d), lambda i: (i, 0)),
                   pl.BlockSpec((N_EXPERTS, tm), lambda i: (0, i))],
        out_shape=[jax.ShapeDtypeStruct((t, d), F32),
                   jax.ShapeDtypeStruct((t, d), BF16),
                   jax.ShapeDtypeStruct((N_EXPERTS, t), F32)],
        compiler_params=_cp(("arbitrary",), 48),
        name="out_proj",
    )(*mixed, w_out.astype(BF16), x2, mod, rw, rb)


def _route_kernel(lg_ref, idx_ref, gate_ref, tcnt_ref, toff_ref, trun_ref, cnt_ref, run_ref):
    @pl.when(pl.program_id(0) == 0)
    def _():
        run_ref[...] = jnp.zeros_like(run_ref)

    tm = lg_ref.shape[1]
    row = lax.broadcasted_iota(jnp.int32, (N_EXPERTS, tm), 0)
    neg = -jnp.inf
    l = lg_ref[...]
    vals, firsts, hots = [], [], []
    for _ in range(TOP_K):
        m = jnp.max(l, axis=0, keepdims=True)
        first = jnp.min(jnp.where(l == m, row, N_EXPERTS), axis=0, keepdims=True)
        hot = row == first
        vals.append(m)
        firsts.append(first)
        hots.append(hot)
        l = jnp.where(hot, neg, l)
    sel = hots[0] | hots[1] | hots[2] | hots[3]
    ex = [jnp.exp(v - vals[0]) for v in vals]
    denom = ex[0] + ex[1] + ex[2] + ex[3]
    r_i = lax.broadcasted_iota(jnp.int32, (tm, tm), 0)
    c_i = lax.broadcasted_iota(jnp.int32, (tm, tm), 1)
    before = (r_i < c_i).astype(BF16)
    earlier = _dot(sel.astype(BF16), before)
    tile_cnt = jnp.sum(sel.astype(F32), axis=1, keepdims=True)
    lower = jnp.zeros((N_EXPERTS, tm), F32)
    for k in range(TOP_K):
        lower = lower + (firsts[k] < row).astype(F32)
    tile_off = jnp.sum(lower, axis=1, keepdims=True)
    slot = tile_off + earlier
    row8 = lax.broadcasted_iota(jnp.int32, (2 * TOP_K, tm), 0)
    idx_out = jnp.zeros((2 * TOP_K, tm), jnp.int32)
    gate_out = jnp.zeros((2 * TOP_K, tm), F32)
    for k in range(TOP_K):
        pos_k = jnp.sum(jnp.where(hots[k], slot, 0.0), axis=0, keepdims=True).astype(jnp.int32)
        idx_out = jnp.where(row8 == k, firsts[k], idx_out)
        idx_out = jnp.where(row8 == TOP_K + k, pos_k, idx_out)
        gate_out = jnp.where(row8 == k, ex[k] / denom, gate_out)
    idx_ref[...] = idx_out
    gate_ref[...] = gate_out
    run = run_ref[:, 0:1]
    tcnt_ref[...] = jnp.broadcast_to(tile_cnt, tcnt_ref.shape)
    toff_ref[...] = jnp.broadcast_to(tile_off, toff_ref.shape)
    trun_ref[...] = jnp.broadcast_to(run, trun_ref.shape)
    total = run + tile_cnt
    run_ref[...] = jnp.broadcast_to(total, run_ref.shape)
    cnt_ref[...] = jnp.broadcast_to(total, cnt_ref.shape)


def _route(logits_t):
    t = logits_t.shape[1]
    tm = min(MOE_TILE, t)
    nt = t // tm
    per_tile = pl.BlockSpec((N_EXPERTS, LANES), lambda i: (i, 0))
    per_tile_shape = jax.ShapeDtypeStruct((nt * N_EXPERTS, LANES), F32)
    return pl.pallas_call(
        _route_kernel,
        grid=(nt,),
        in_specs=[pl.BlockSpec((N_EXPERTS, tm), lambda i: (0, i))],
        out_specs=[pl.BlockSpec((2 * TOP_K, tm), lambda i: (0, i)),
                   pl.BlockSpec((2 * TOP_K, tm), lambda i: (0, i)),
                   per_tile, per_tile, per_tile,
                   pl.BlockSpec((N_EXPERTS, LANES), lambda i: (0, 0))],
        out_shape=[jax.ShapeDtypeStruct((2 * TOP_K, t), jnp.int32),
                   jax.ShapeDtypeStruct((2 * TOP_K, t), F32),
                   per_tile_shape, per_tile_shape, per_tile_shape,
                   jax.ShapeDtypeStruct((N_EXPERTS, LANES), F32)],
        scratch_shapes=[pltpu.VMEM((N_EXPERTS, LANES), F32)],
        compiler_params=_cp(("arbitrary",), 32),
        name="route",
    )(logits_t)


def _rows(ref, start, n):
    return ref.at[pl.ds(pl.multiple_of(start * ROW_TILES, ROW_TILES), n * ROW_TILES), :]


def _run_copies(tile, cnt_ref, off_ref, dst_ref, make_copy):
    def per_expert(e, carry):
        j = tile * N_EXPERTS + e
        cnt, off, dst = cnt_ref[j], off_ref[j], dst_ref[j]

        for b in reversed(range(RUN_BITS)):
            size = 1 << b

            @pl.when((cnt & size) != 0)
            def _():
                done = lax.shift_left(lax.shift_right_logical(cnt, b + 1), b + 1)
                make_copy(off + done, dst + done, size).start()
        return carry

    lax.fori_loop(0, N_EXPERTS, per_expert, 0)


def _dispatch_kernel(fs_ref, fl_ref, cnt_ref, off_ref, dst_ref, h_ref, idx_ref, xs_ref,
                     stage, zero_buf, sems, zsem):
    i = pl.program_id(0)
    n_steps = pl.num_programs(0)
    cur = lax.rem(i, 2)
    tm = h_ref.shape[0]
    n_sorted = TOP_K * tm
    n_fill = fs_ref.shape[0]

    def fill_copies(f, wait):
        start, n = fs_ref[f], fl_ref[f]
        n_chunks = lax.shift_right_logical(n, FILL_SHIFT)
        tail = start + n_chunks * FILL_ROWS

        def chunk(j, carry):
            cp = pltpu.make_async_copy(zero_buf, _rows(xs_ref, start + j * FILL_ROWS, FILL_ROWS),
                                       zsem)
            cp.wait() if wait else cp.start()
            return carry

        def single(j, carry):
            cp = pltpu.make_async_copy(_rows(zero_buf, 0, 1), _rows(xs_ref, tail + j, 1), zsem)
            cp.wait() if wait else cp.start()
            return carry

        lax.fori_loop(0, n_chunks, chunk, 0)
        lax.fori_loop(0, n - n_chunks * FILL_ROWS, single, 0)

    @pl.when(i == 0)
    def _():
        zero_buf[...] = jnp.zeros_like(zero_buf)
        lax.fori_loop(0, n_fill, lambda f, c: (fill_copies(f, False), c)[1], 0)

    h = h_ref[...]
    pos = [idx_ref[TOP_K + k:TOP_K + k + 1, :] for k in range(TOP_K)]
    for sb in range(n_sorted // SORT_SLAB):
        j = lax.broadcasted_iota(jnp.int32, (SORT_SLAB, tm), 0) + sb * SORT_SLAB
        hit = (pos[0] == j) | (pos[1] == j) | (pos[2] == j) | (pos[3] == j)
        perm = jnp.where(hit, 1.0, 0.0).astype(BF16)
        rows = _dot(perm, h)
        for c in range(ROW_TILES):
            stage[cur, pl.ds(sb * SORT_SLAB * ROW_TILES + c, SORT_SLAB, stride=ROW_TILES), :] = (
                rows[:, c * LANES:(c + 1) * LANES])

    def make_copy(src_row, dst_row, size):
        return pltpu.make_async_copy(_rows(stage.at[cur], src_row, size),
                                     _rows(xs_ref, dst_row, size), sems.at[cur])

    _run_copies(i, cnt_ref, off_ref, dst_ref, make_copy)

    def wait_slot(slot):
        pltpu.make_async_copy(stage.at[slot], _rows(xs_ref, 0, n_sorted), sems.at[slot]).wait()

    @pl.when(i > 0)
    def _():
        wait_slot(1 - cur)

    @pl.when(i == n_steps - 1)
    def _():
        wait_slot(cur)

    @pl.when(i == 0)
    def _():
        lax.fori_loop(0, n_fill, lambda f, c: (fill_copies(f, True), c)[1], 0)


def _dispatch(h2, idx, tile_cnt, tile_off, tile_dst, fill_start, fill_len, n_rows):
    t, d = h2.shape
    tm = min(MOE_TILE, t)
    grid_spec = pltpu.PrefetchScalarGridSpec(
        num_scalar_prefetch=5,
        grid=(t // tm,),
        in_specs=[pl.BlockSpec((tm, d), lambda i, *_: (i, 0)),
                  pl.BlockSpec((2 * TOP_K, tm), lambda i, *_: (0, i))],
        out_specs=pl.BlockSpec(memory_space=pl.ANY),
        scratch_shapes=[pltpu.VMEM((2, TOP_K * tm * ROW_TILES, LANES), F32),
                        pltpu.VMEM((FILL_ROWS * ROW_TILES, LANES), F32),
                        pltpu.SemaphoreType.DMA((2,)), pltpu.SemaphoreType.DMA(())],
    )
    return pl.pallas_call(
        _dispatch_kernel,
        grid_spec=grid_spec,
        out_shape=jax.ShapeDtypeStruct((n_rows * ROW_TILES, LANES), F32),
        compiler_params=_cp(("arbitrary",), 48),
        name="dispatch",
    )(fill_start, fill_len, tile_cnt, tile_off, tile_dst, h2, idx)


def _expert_kernel(layer, be_ref, nu_ref, nx_ref, nv_ref, xs_ref, wi_hbm, bi_ref, wo_hbm, bo_ref,
                   y_ref, wi_st, wo_st, wi_b, wo_b, sems):
    i = pl.program_id(0)
    tb = xs_ref.shape[0] // ROW_TILES
    e = be_ref[i]
    fresh = jnp.logical_or(i == 0, e != be_ref[jnp.maximum(i - 1, 0)])
    used = i < nu_ref[0]

    def fetch(ex):
        return (pltpu.make_async_copy(wi_hbm.at[layer, ex], wi_st, sems.at[0]),
                pltpu.make_async_copy(wo_hbm.at[layer, ex], wo_st, sems.at[1]))

    @pl.when(i == 0)
    def _():
        for cp in fetch(e):
            cp.start()

    @pl.when(jnp.logical_and(fresh, used))
    def _():
        for cp in fetch(e):
            cp.wait()
        wi_b[...] = wi_st[...].astype(BF16)
        wo_b[...] = wo_st[...].astype(BF16)

        @pl.when(nx_ref[i] >= 0)
        def _():
            for cp in fetch(nx_ref[i]):
                cp.start()

    def compute(n):
        x = jnp.concatenate(
            [xs_ref[pl.ds(c, n, stride=ROW_TILES), :] for c in range(ROW_TILES)],
            axis=1).astype(BF16)
        y = jnp.zeros((n, D_MODEL), F32) + bo_ref[0, 0]
        half = 1024
        for j in range(D_FF // half):
            a, b = j * half, (j + 1) * half
            glu = _dot(x, wi_b[:, a:b]) + bi_ref[0, 0, :, a:b]
            lin = _dot(x, wi_b[:, D_FF + a:D_FF + b]) + bi_ref[0, 0, :, D_FF + a:D_FF + b]
            glu = jnp.minimum(glu, SWIGLU_LIMIT)
            lin = jnp.clip(lin, -SWIGLU_LIMIT, SWIGLU_LIMIT)
            act = glu * jax.nn.sigmoid(SWIGLU_ALPHA * glu) * (lin + 1.0)
            y = y + _dot(act.astype(BF16), wo_b[a:b, :])
        for c in range(ROW_TILES):
            y_ref[pl.ds(c, n, stride=ROW_TILES), :] = y[:, c * LANES:(c + 1) * LANES]

    full = nv_ref[i] > tb // 2

    @pl.when(jnp.logical_and(used, full))
    def _():
        compute(tb)

    @pl.when(jnp.logical_and(used, jnp.logical_not(full)))
    def _():
        compute(tb // 2)
        y_ref[pl.ds(tb // 2 * ROW_TILES, tb // 2 * ROW_TILES), :] = jnp.zeros(
            (tb // 2 * ROW_TILES, LANES), F32)

    @pl.when(jnp.logical_not(used))
    def _():
        y_ref[...] = jnp.zeros_like(y_ref)


def _experts(xs, block_e, n_used, next_e, n_valid, layer, w_in, b_in, w_out, b_out):
    tb = EXPERT_ROWS
    n_rows = xs.shape[0] // ROW_TILES
    nb = n_rows // tb
    depth, ne, d, f2 = w_in.shape

    def row_map(i, be, nu, nx, nv):
        return (jnp.minimum(i, nu[0] - 1), 0)

    grid_spec = pltpu.PrefetchScalarGridSpec(
        num_scalar_prefetch=4,
        grid=(nb,),
        in_specs=[
            pl.BlockSpec((tb * ROW_TILES, LANES), row_map),
            pl.BlockSpec(memory_space=pl.ANY),
            pl.BlockSpec((1, 1, 1, f2), lambda i, be, nu, nx, nv: (layer, be[i], 0, 0)),
            pl.BlockSpec(memory_space=pl.ANY),
            pl.BlockSpec((1, 1, 1, d), lambda i, be, nu, nx, nv: (layer, be[i], 0, 0)),
        ],
        out_specs=pl.BlockSpec((tb * ROW_TILES, LANES), lambda i, be, nu, nx, nv: (i, 0)),
        scratch_shapes=[pltpu.VMEM((d, f2), F32), pltpu.VMEM((D_FF, d), F32),
                        pltpu.VMEM((d, f2), BF16), pltpu.VMEM((D_FF, d), BF16),
                        pltpu.SemaphoreType.DMA((2,))],
    )
    return pl.pallas_call(
        functools.partial(_expert_kernel, layer),
        grid_spec=grid_spec,
        out_shape=jax.ShapeDtypeStruct((n_rows * ROW_TILES, LANES), F32),
        compiler_params=_cp(("arbitrary",), 56),
        name="experts",
    )(block_e, n_used, next_e, n_valid, xs, w_in, b_in.reshape(depth, ne, 1, f2), w_out,
      b_out.reshape(depth, ne, 1, d))


def _combine_kernel(final, cnt_ref, off_ref, dst_ref, pos_ref, gate_ref, x_ref, mod_ref, fg_ref,
                    yb_ref, o_ref, stage, sems):
    tm = x_ref.shape[0]
    n_sorted = TOP_K * tm
    i = pl.program_id(0)
    cur = lax.rem(i, 2)

    def fetch(tile, slot):
        def make_copy(sorted_row, src_row, size):
            return pltpu.make_async_copy(_rows(yb_ref, src_row, size),
                                         _rows(stage.at[slot], sorted_row, size), sems.at[slot])

        _run_copies(tile, cnt_ref, off_ref, dst_ref, make_copy)

    @pl.when(i == 0)
    def _():
        fetch(0, 0)

    @pl.when(i + 1 < pl.num_programs(0))
    def _():
        fetch(i + 1, 1 - cur)

    pltpu.make_async_copy(_rows(yb_ref, 0, n_sorted), stage.at[cur], sems.at[cur]).wait()

    pos = [pos_ref[:, TOP_K + k:TOP_K + k + 1] for k in range(TOP_K)]
    gate = [gate_ref[:, k:k + 1] for k in range(TOP_K)]
    y = jnp.zeros((tm, D_MODEL), F32)
    for sb in range(n_sorted // SORT_SLAB):
        rows = jnp.concatenate(
            [stage[cur, pl.ds(sb * SORT_SLAB * ROW_TILES + c, SORT_SLAB, stride=ROW_TILES), :]
             for c in range(ROW_TILES)], axis=1).astype(BF16)
        j = lax.broadcasted_iota(jnp.int32, (tm, SORT_SLAB), 1) + sb * SORT_SLAB
        w = jnp.zeros((tm, SORT_SLAB), F32)
        for k in range(TOP_K):
            w = w + jnp.where(pos[k] == j, gate[k], 0.0)
        y = y + _dot(w.astype(BF16), rows)
    o = x_ref[...] + mod_ref[0, 5:6, :] * y
    if final:
        o = _rms_rows(o) * fg_ref[...]
    o_ref[...] = o


def _combine(yb, tile_cnt, tile_off, tile_dst, pos_t, gates_t, x2, mod, final_g, final, seq):
    t, d = x2.shape
    tm = min(MOE_TILE, seq)
    grid_spec = pltpu.PrefetchScalarGridSpec(
        num_scalar_prefetch=3,
        grid=(t // tm,),
        in_specs=[
            pl.BlockSpec((tm, 2 * TOP_K), lambda i, *_: (i, 0)),
            pl.BlockSpec((tm, 2 * TOP_K), lambda i, *_: (i, 0)),
            pl.BlockSpec((tm, d), lambda i, *_: (i, 0)),
            pl.BlockSpec((1, 6, d), lambda i, *_: (i // (seq // tm), 0, 0)),
            pl.BlockSpec((1, d), lambda i, *_: (0, 0)),
            pl.BlockSpec(memory_space=pl.ANY),
        ],
        out_specs=pl.BlockSpec((tm, d), lambda i, *_: (i, 0)),
        scratch_shapes=[pltpu.VMEM((2, TOP_K * tm * ROW_TILES, LANES), F32),
                        pltpu.SemaphoreType.DMA((2,))],
    )
    return pl.pallas_call(
        functools.partial(_combine_kernel, final),
        grid_spec=grid_spec,
        out_shape=jax.ShapeDtypeStruct((t, d), F32),
        compiler_params=_cp(("arbitrary",), 48),
        name="combine",
    )(tile_cnt, tile_off, tile_dst, pos_t, gates_t, x2, mod, final_g.reshape(1, d), yb)


def _moe(h2, logits, x2, mod, layer, w_in, b_in, w_out, b_out, final_g, final, seq):
    t = x2.shape[0]
    tb = EXPERT_ROWS
    idx, gates, tcnt, toff, trun, cnt = _route(logits)
    counts = cnt[:, 0].astype(jnp.int32)
    nblk = (counts + tb - 1) // tb
    blk_end = jnp.cumsum(nblk)
    pad_start = (blk_end - nblk) * tb
    n_blocks = (t * TOP_K) // tb + N_EXPERTS
    n_used = blk_end[-1:]
    experts = jnp.arange(N_EXPERTS, dtype=jnp.int32)
    last_e = jnp.max(jnp.where(nblk > 0, experts, 0))
    blk = jnp.arange(n_blocks, dtype=jnp.int32)
    block_e = jnp.minimum(
        jnp.sum((blk_end[None, :] <= blk[:, None]).astype(jnp.int32), axis=1), last_e)
    later = (experts[None, :] > block_e[:, None]) & (nblk[None, :] > 0)
    next_e = jnp.min(jnp.where(later, experts[None, :], N_EXPERTS), axis=1)
    next_e = jnp.where(next_e == N_EXPERTS, -1, next_e).astype(jnp.int32)
    tile_cnt = tcnt[:, 0].astype(jnp.int32)
    tile_off = toff[:, 0].astype(jnp.int32)
    tile_dst = (trun[:, 0].astype(jnp.int32).reshape(-1, N_EXPERTS) + pad_start[None, :]).reshape(-1)
    fill_start = jnp.concatenate([pad_start + counts, blk_end[-1:] * tb]).astype(jnp.int32)
    fill_len = jnp.concatenate([nblk * tb - counts, (n_blocks - blk_end[-1:]) * tb]).astype(jnp.int32)
    xs = _dispatch(h2, idx, tile_cnt, tile_off, tile_dst, fill_start, fill_len, n_blocks * tb)
    row_end = jnp.sum(jnp.where(block_e[:, None] == experts[None, :],
                                (pad_start + counts)[None, :], 0), axis=1)
    n_valid = jnp.clip(row_end - blk * tb, 0, tb).astype(jnp.int32)
    yb = _experts(xs, block_e, n_used.astype(jnp.int32), next_e, n_valid, layer,
                  w_in, b_in, w_out, b_out)
    return _combine(yb, tile_cnt, tile_off, tile_dst, idx.T, gates.T, x2, mod, final_g, final, seq)


def kernel(x, c, positions, w_ada, b_ada, even_w_in, gla_w_gate, gla_b_gate, gla_norm_g,
           diff_lam_q1, diff_lam_k1, diff_lam_q2, diff_lam_k2, diff_norm_g, even_w_out,
           odd_w_in, sgu_ln_g, sgu_ln_b, sgu_w, sgu_b, odd_w_out,
           router_w, router_b, expert_w_in, expert_b_in, expert_w_out, expert_b_out,
           final_norm_g):
    bsz, seq, d = x.shape
    depth = w_ada.shape[0]
    t = bsz * seq
    mods = _modulation(c, w_ada, b_ada).reshape(depth, bsz, 6, d)
    cos, sin = _rope_tables(positions)
    x2 = x.reshape(t, d)
    for layer in range(depth):
        mod = mods[layer]
        j = layer // 2
        if layer % 2 == 0:
            z, f = _proj_even(x2, mod, even_w_in[j], gla_w_gate[j], gla_b_gate[j], cos, sin, seq)
            o_gla = _gla(z, f, gla_norm_g[j], bsz, seq)
            lam_init = 0.8 - 0.6 * math.exp(-0.3 * layer)
            o_diff = _diff_attn(z, diff_lam_q1[j], diff_lam_k1[j], diff_lam_q2[j], diff_lam_k2[j],
                                diff_norm_g[j], lam_init, bsz, seq)
            mixed, w_out = (o_gla, o_diff), even_w_out[j]
        else:
            z, f = _proj_odd(x2, mod, odd_w_in[j], sgu_ln_g[j], sgu_ln_b[j], cos, sin, seq)
            mixed, w_out = (_sgu_ret(z, f, sgu_w[j], sgu_b[j], bsz, seq),), odd_w_out[j]
        x2, h2, logits = _out_proj(mixed, w_out, x2, mod, router_w[layer], router_b[layer], seq)
        x2 = _moe(h2, logits, x2, mod, layer, expert_w_in, expert_b_in, expert_w_out,
                  expert_b_out, final_norm_g, layer == depth - 1, seq)
    return x2.reshape(bsz, seq, d)
```

```python
import functools
import math

import jax
import jax.numpy as jnp
from jax import lax
from jax.experimental import pallas as pl
from jax.experimental.pallas import tpu as pltpu

F32 = jnp.float32
BF16 = jnp.bfloat16

D_MODEL = 1024
EPS = 1e-6
ROPE_THETA = 10000.0
ROPE_HALF = 32

GLA_HEADS = 4
GLA_DK = 64
GLA_DV = 128
GLA_RANK = 16
GLA_CHUNK = 64
GLA_GATE_NORMALIZER = 16.0
GLA_QK_W = GLA_HEADS * GLA_DK
GLA_V_W = GLA_HEADS * GLA_DV

DIFF_HEADS = 4
DIFF_D = 64
DIFF_DV = 128
DIFF_QK_W = DIFF_HEADS * 2 * DIFF_D
DIFF_V_W = DIFF_HEADS * DIFF_DV

SGU_GROUPS = 4
SGU_CH = 128
SGU_CHUNK = 128
SGU_W = SGU_GROUPS * SGU_CH

RET_HEADS = 4
RET_DK = 64
RET_DV = 128
RET_CHUNK = 128
RET_QK_W = RET_HEADS * RET_DK
RET_V_W = RET_HEADS * RET_DV

N_EXPERTS = 32
TOP_K = 4
D_FF = D_MODEL
SWIGLU_ALPHA = 1.702
SWIGLU_LIMIT = 7.0

LANES = 128
SUBLANES = 8
ROW_TILES = D_MODEL // LANES

PROJ_ROWS = 1024
GLA_ROWS = 512
GLA_SEQS_PER_STEP = 2
GLA_GROUP = 4
SGU_RET_ROWS = 512
ATT_Q_ROWS = 256
ATT_HEADS_PER_STEP = 4
MOE_TILE = 512
SORT_SLAB = 512
RUN_BITS = 10
FILL_SHIFT = 6
FILL_ROWS = 1 << FILL_SHIFT
EXPERT_ROWS = 512
MOD_COLS = 1536

MIB = 1024 * 1024


def _cp(semantics, vmem_mib):
    return pltpu.CompilerParams(dimension_semantics=semantics, vmem_limit_bytes=vmem_mib * MIB)


def _dot(a, b):
    return jnp.dot(a, b, preferred_element_type=F32)


def _dot_nt(a, b):
    return lax.dot_general(a, b, (((1,), (1,)), ((), ())), preferred_element_type=F32)


def _dot_tn(a, b):
    return lax.dot_general(a, b, (((0,), (0,)), ((), ())), preferred_element_type=F32)


def _rms_rows(x):
    return x * lax.rsqrt(jnp.mean(x * x, axis=-1, keepdims=True) + EPS)


def _silu(x):
    return x * jax.nn.sigmoid(x)


def _rope_chunk(v, cos, sin, lo_half):
    rot = jnp.where(lo_half, -pltpu.roll(v, 96, 1), pltpu.roll(v, 32, 1))
    return v * cos + rot * sin


def _mod_kernel(c_ref, w_ref, b_ref, o_ref):
    c = c_ref[...]
    ca = _silu(c).astype(BF16)
    o_ref[0] = _dot(ca, w_ref[0].astype(BF16)) + b_ref[0]


def _modulation(c, w_ada, b_ada):
    depth, d, n = w_ada.shape
    bsz = c.shape[0]
    return pl.pallas_call(
        _mod_kernel,
        grid=(depth, n // MOD_COLS),
        in_specs=[
            pl.BlockSpec((bsz, d), lambda l, j: (0, 0)),
            pl.BlockSpec((1, d, MOD_COLS), lambda l, j: (l, 0, j)),
            pl.BlockSpec((1, 1, MOD_COLS), lambda l, j: (l, 0, j)),
        ],
        out_specs=pl.BlockSpec((1, bsz, MOD_COLS), lambda l, j: (l, 0, j)),
        out_shape=jax.ShapeDtypeStruct((depth, bsz, n), F32),
        compiler_params=_cp(("arbitrary", "arbitrary"), 40),
        name="adaln_mod",
    )(c, w_ada, b_ada.reshape(depth, 1, n))


def _rope_table_kernel(p_ref, f_ref, c_ref, s_ref):
    ang = p_ref[...].astype(F32) * f_ref[...]
    c_ref[...] = jnp.cos(ang)
    s_ref[...] = jnp.sin(ang)


def _rope_tables(positions):
    t = positions.size
    per_row = LANES // ROPE_HALF
    rows = t // per_row
    pos_d = jnp.repeat(positions.reshape(rows, per_row), ROPE_HALF, axis=1)
    inv_freq = ROPE_THETA ** (-jnp.arange(ROPE_HALF, dtype=F32) / ROPE_HALF)
    freq_d = jnp.tile(inv_freq, per_row).reshape(1, LANES)
    tr = min(512, rows)
    cos_d, sin_d = pl.pallas_call(
        _rope_table_kernel,
        grid=(rows // tr,),
        in_specs=[pl.BlockSpec((tr, LANES), lambda i: (i, 0)),
                  pl.BlockSpec((1, LANES), lambda i: (0, 0))],
        out_specs=[pl.BlockSpec((tr, LANES), lambda i: (i, 0))] * 2,
        out_shape=[jax.ShapeDtypeStruct((rows, LANES), F32)] * 2,
        compiler_params=_cp(("arbitrary",), 32),
        name="rope_tables",
    )(pos_d, freq_d)
    cos = jnp.tile(cos_d.reshape(t, ROPE_HALF), (1, per_row))
    sin = jnp.tile(sin_d.reshape(t, ROPE_HALF), (1, per_row))
    return cos, sin


def _modulated_rms(x_ref, mod_ref, which):
    x = x_ref[...]
    shift = mod_ref[0, 3 * which:3 * which + 1, :]
    scale = mod_ref[0, 3 * which + 1:3 * which + 2, :]
    return _rms_rows(x) * (1.0 + scale) + shift


EVEN_Z = (GLA_V_W, GLA_V_W, DIFF_QK_W, DIFF_QK_W, DIFF_V_W)
EVEN_Z_W = sum(EVEN_Z)
EVEN_F_W = 3 * GLA_QK_W
EVEN_W_COLS = EVEN_Z_W + 2 * GLA_QK_W + LANES


def _proj_even_kernel(x_ref, mod_ref, w_ref, wg_ref, bg_ref, cos_ref, sin_ref, z_ref, f_ref):
    hb = _modulated_rms(x_ref, mod_ref, 0).astype(BF16)

    def sec(a, b):
        return _dot(hb, w_ref[:, a:b])

    for a, b in ((0, 512), (512, 1024), (2048, 2560)):
        z_ref[:, a:b] = sec(a, b).astype(BF16)
    cos = cos_ref[...]
    sin = sin_ref[...]
    lo_half = (lax.broadcasted_iota(jnp.int32, cos.shape, 1) % 64) < ROPE_HALF
    q_scale = DIFF_D ** -0.5 * math.log2(math.e)
    for a, scl in ((1024, q_scale), (1536, 1.0)):
        full = sec(a, a + DIFF_QK_W)
        for c in range(DIFF_QK_W // LANES):
            v = full[:, c * LANES:(c + 1) * LANES]
            z_ref[:, a + c * LANES:a + (c + 1) * LANES] = (
                _rope_chunk(v, cos, sin, lo_half) * scl).astype(BF16)
    f_ref[:, 0:256] = sec(2560, 2816) * (GLA_DK ** -0.5)
    f_ref[:, 256:512] = sec(2816, 3072)
    gr = sec(3072, 3200).astype(BF16)
    pre = _dot(gr, wg_ref[...]) + bg_ref[...]
    log_sig = jnp.minimum(pre, 0.0) - jnp.log1p(jnp.exp(-jnp.abs(pre)))
    f_ref[:, 512:768] = log_sig / GLA_GATE_NORMALIZER


def _proj_even(x2, mod, w_in, w_gate, b_gate, cos, sin, seq):
    t, d = x2.shape
    tm = min(PROJ_ROWS, seq)
    gq, gk, gv, gr, gg, dq, dk, dv = jnp.split(
        w_in, [256, 512, 1024, 1040, 1552, 2064, 2576], axis=1)
    gr_pad = jnp.pad(gr, ((0, 0), (0, LANES - GLA_RANK)))
    w_cat = jnp.concatenate([gv, gg, dq, dk, dv, gq, gk, gr_pad], axis=1).astype(BF16)
    wg_pad = jnp.pad(w_gate, ((0, LANES - GLA_RANK), (0, 0))).astype(BF16)
    return pl.pallas_call(
        _proj_even_kernel,
        grid=(t // tm,),
        in_specs=[
            pl.BlockSpec((tm, d), lambda i: (i, 0)),
            pl.BlockSpec((1, 6, d), lambda i: (i // (seq // tm), 0, 0)),
            pl.BlockSpec((d, EVEN_W_COLS), lambda i: (0, 0)),
            pl.BlockSpec((LANES, GLA_QK_W), lambda i: (0, 0)),
            pl.BlockSpec((1, GLA_QK_W), lambda i: (0, 0)),
            pl.BlockSpec((tm, LANES), lambda i: (i, 0)),
            pl.BlockSpec((tm, LANES), lambda i: (i, 0)),
        ],
        out_specs=[pl.BlockSpec((tm, EVEN_Z_W), lambda i: (i, 0)),
                   pl.BlockSpec((tm, EVEN_F_W), lambda i: (i, 0))],
        out_shape=[jax.ShapeDtypeStruct((t, EVEN_Z_W), BF16),
                   jax.ShapeDtypeStruct((t, EVEN_F_W), F32)],
        compiler_params=_cp(("arbitrary",), 48),
        name="proj_even",
    )(x2, mod, w_cat, wg_pad, b_gate.reshape(1, GLA_QK_W), cos, sin)


ODD_Z_W = 2 * SGU_W + 2 * RET_V_W
ODD_F_W = 2 * RET_QK_W
ODD_W_COLS = ODD_Z_W + ODD_F_W


def _gelu_exact(x):
    return 0.5 * x * (1.0 + lax.erf(x * (2.0 ** -0.5)))


def _proj_odd_kernel(x_ref, mod_ref, w_ref, lng_ref, lnb_ref, cos_ref, sin_ref, z_ref, f_ref):
    hb = _modulated_rms(x_ref, mod_ref, 0).astype(BF16)

    def sec(a, b):
        return _dot(hb, w_ref[:, a:b])

    z_ref[:, 0:512] = _gelu_exact(sec(0, 512)).astype(BF16)
    sv = _gelu_exact(sec(512, 1024))
    mu = jnp.mean(sv, axis=-1, keepdims=True)
    cen = sv - mu
    var = jnp.mean(cen * cen, axis=-1, keepdims=True)
    z_ref[:, 512:1024] = (cen * lax.rsqrt(var + EPS) * lng_ref[...] + lnb_ref[...]).astype(BF16)
    for a, b in ((1024, 1536), (1536, 2048)):
        z_ref[:, a:b] = sec(a, b).astype(BF16)
    cos = cos_ref[...]
    sin = sin_ref[...]
    lo_half = (lax.broadcasted_iota(jnp.int32, cos.shape, 1) % 64) < ROPE_HALF
    k_scale = RET_DK ** -0.5
    for a, scl in ((0, 1.0), (256, k_scale)):
        full = sec(ODD_Z_W + a, ODD_Z_W + a + RET_QK_W)
        for c in range(RET_QK_W // LANES):
            v = full[:, c * LANES:(c + 1) * LANES]
            f_ref[:, a + c * LANES:a + (c + 1) * LANES] = _rope_chunk(v, cos, sin, lo_half) * scl


def _proj_odd(x2, mod, w_in, ln_g, ln_b, cos, sin, seq):
    t, d = x2.shape
    tm = min(PROJ_ROWS, seq)
    su, sv, rq, rk, rv, rg = jnp.split(w_in, [512, 1024, 1280, 1536, 2048], axis=1)
    w_cat = jnp.concatenate([su, sv, rv, rg, rq, rk], axis=1).astype(BF16)
    return pl.pallas_call(
        _proj_odd_kernel,
        grid=(t // tm,),
        in_specs=[
            pl.BlockSpec((tm, d), lambda i: (i, 0)),
            pl.BlockSpec((1, 6, d), lambda i: (i // (seq // tm), 0, 0)),
            pl.BlockSpec((d, ODD_W_COLS), lambda i: (0, 0)),
            pl.BlockSpec((1, SGU_W), lambda i: (0, 0)),
            pl.BlockSpec((1, SGU_W), lambda i: (0, 0)),
            pl.BlockSpec((tm, LANES), lambda i: (i, 0)),
            pl.BlockSpec((tm, LANES), lambda i: (i, 0)),
        ],
        out_specs=[pl.BlockSpec((tm, ODD_Z_W), lambda i: (i, 0)),
                   pl.BlockSpec((tm, ODD_F_W), lambda i: (i, 0))],
        out_shape=[jax.ShapeDtypeStruct((t, ODD_Z_W), BF16),
                   jax.ShapeDtypeStruct((t, ODD_F_W), F32)],
        compiler_params=_cp(("arbitrary",), 48),
        name="proj_odd",
    )(x2, mod, w_cat, ln_g.reshape(1, SGU_W), ln_b.reshape(1, SGU_W), cos, sin)


def _gla_kernel(q_ref, k_ref, la_ref, v_ref, gg_ref, g_ref, o_ref, st_ref):
    n_seq = q_ref.shape[0]
    n_pairs = GLA_HEADS // 2

    @pl.when(pl.program_id(1) == 0)
    def _():
        st_ref[...] = jnp.zeros_like(st_ref)

    c = GLA_CHUNK
    sc = GLA_GROUP * c
    rows = q_ref.shape[1]
    r_i = lax.broadcasted_iota(jnp.int32, (sc, sc), 0)
    c_i = lax.broadcasted_iota(jnp.int32, (sc, sc), 1)
    block_causal = (r_i // c == c_i // c) & (r_i >= c_i)
    tril = block_causal.astype(BF16)
    causal2 = jnp.concatenate([block_causal, block_causal], axis=0)
    lo = lax.broadcasted_iota(jnp.int32, (sc, LANES), 1) < GLA_DK
    lane_s = lax.broadcasted_iota(jnp.int32, (LANES, LANES), 1) < GLA_DK
    g_row = g_ref[...]
    states = [st_ref[n] for n in range(n_seq * n_pairs)]
    for j, s in [(j, s) for j in range(rows // sc) for s in range(n_seq)]:
        rs = slice(j * sc, (j + 1) * sc)
        la = la_ref[s, rs, :]
        la_hi = la.astype(BF16)
        la_lo = (la - la_hi.astype(F32)).astype(BF16)
        b = _dot(tril, la_hi) + _dot(tril, la_lo)
        b_last = [b[(g + 1) * c - 1:(g + 1) * c, :] for g in range(GLA_GROUP)]
        b_last_rows = jnp.concatenate(
            [jnp.broadcast_to(bl, (c, bl.shape[1])) for bl in b_last], axis=0)
        qd = q_ref[s, rs, :] * jnp.exp(b)
        kk = k_ref[s, rs, :]
        ki = kk * jnp.exp(-b)
        kd = kk * jnp.exp(b_last_rows - b)
        for p in range(GLA_HEADS // 2):
            ls = slice(p * LANES, (p + 1) * LANES)
            qd_p = qd[:, ls]
            qm = (jnp.where(lo, qd_p, 0.0).astype(BF16), jnp.where(lo, 0.0, qd_p).astype(BF16))
            s2 = _dot_nt(jnp.concatenate(qm, axis=0), ki[:, ls].astype(BF16))
            s2 = jnp.where(causal2, s2, 0.0).astype(BF16)
            kd_p = kd[:, ls].astype(BF16)
            heads = (2 * p, 2 * p + 1)
            v = [v_ref[s, rs, h * GLA_DV:(h + 1) * GLA_DV] for h in heads]
            inter = ([], [])
            sp = s * n_pairs + p
            for g in range(GLA_GROUP):
                gs = slice(g * c, (g + 1) * c)
                st_b = states[sp].astype(BF16)
                for hh in range(2):
                    inter[hh].append(_dot_nt(qm[hh][gs, :], st_b))
                new = [_dot_tn(v[hh][gs, :], kd_p[gs, :]) for hh in range(2)]
                states[sp] = (jnp.exp(b_last[g][:, ls]) * states[sp]
                              + jnp.where(lane_s, new[0], new[1]))
            for hh in range(2):
                hs = slice(heads[hh] * GLA_DV, (heads[hh] + 1) * GLA_DV)
                o = _dot(s2[hh * sc:(hh + 1) * sc, :], v[hh]) + jnp.concatenate(inter[hh], axis=0)
                o = _rms_rows(o) * g_row
                gate = _silu(gg_ref[s, rs, hs].astype(F32))
                o_ref[s, rs, hs] = (o * gate).astype(BF16)
    for n in range(n_seq * n_pairs):
        st_ref[n] = states[n]


def _gla(z, f, norm_g, bsz, seq):
    t = z.shape[0]
    tc = min(GLA_ROWS, seq)
    nc = seq // tc
    n_seq = math.gcd(GLA_SEQS_PER_STEP, bsz)
    f3 = f.reshape(bsz, seq, f.shape[1])
    z3 = z.reshape(bsz, seq, z.shape[1])

    def spec(width, col):
        return pl.BlockSpec((n_seq, tc, width), lambda b, i: (b, i, col))

    out = pl.pallas_call(
        _gla_kernel,
        grid=(bsz // n_seq, nc),
        in_specs=[spec(GLA_QK_W, 0), spec(GLA_QK_W, 1), spec(GLA_QK_W, 2),
                  spec(GLA_V_W, 0), spec(GLA_V_W, 1),
                  pl.BlockSpec((1, GLA_DV), lambda b, i: (0, 0))],
        out_specs=spec(GLA_V_W, 0),
        out_shape=jax.ShapeDtypeStruct((bsz, seq, GLA_V_W), BF16),
        scratch_shapes=[pltpu.VMEM((n_seq * (GLA_HEADS // 2), LANES, LANES), F32)],
        compiler_params=_cp(("arbitrary", "arbitrary"), 40),
        name="gla",
    )(f3, f3, f3, z3, z3, norm_g.reshape(1, GLA_DV))
    return out.reshape(t, GLA_V_W)


def _diff_attn_kernel(lam_init, q_ref, k_ref, v_ref, lq1_ref, lk1_ref, lq2_ref, lk2_ref, g_ref,
                      o_ref, vt_ref, s_ref, m_ref, l_ref, acc_ref):
    tq = q_ref.shape[0]
    seq = k_ref.shape[0]
    n_heads = q_ref.shape[1] // LANES
    qi = pl.program_id(2)

    @pl.when(qi == 0)
    def _():
        for hh in range(n_heads):
            for cb in range(seq // tq):
                blk = v_ref[cb * tq:(cb + 1) * tq, hh * LANES:(hh + 1) * LANES]
                vt_ref[hh, :, cb * tq:(cb + 1) * tq] = blk.astype(F32).T.astype(BF16)

    lo = lax.broadcasted_iota(jnp.int32, (tq, LANES), 1) < DIFF_D
    qqs = []
    for hh in range(n_heads):
        q = q_ref[:, hh * LANES:(hh + 1) * LANES]
        zero = jnp.zeros_like(q)
        qqs.append(jnp.concatenate([jnp.where(lo, q, zero), jnp.where(lo, zero, q)], axis=0))

    m_ref[...] = jnp.full(m_ref.shape, -jnp.inf, F32)
    l_ref[...] = jnp.zeros_like(l_ref)
    acc_ref[...] = jnp.zeros_like(acc_ref)

    def scores(j, slot):
        start = pl.multiple_of(j * tq, tq)
        for hh in range(n_heads):
            kj = k_ref[pl.ds(start, tq), hh * LANES:(hh + 1) * LANES]
            s_ref[slot, hh] = _dot_nt(kj, qqs[hh])

    def update(j, slot, masked):
        start = pl.multiple_of(j * tq, tq)
        for hh in range(n_heads):
            vtj = vt_ref[hh, :, pl.ds(start, tq)]
            alpha, p = [], []
            for c in range(2 * tq // LANES):
                cs = slice(c * LANES, (c + 1) * LANES)
                s_c = s_ref[slot, hh, :, cs]
                if masked:
                    kv = lax.broadcasted_iota(jnp.int32, s_c.shape, 0)
                    q_pos = lax.broadcasted_iota(jnp.int32, s_c.shape, 1) + (c * LANES) % tq
                    s_c = jnp.where(kv <= q_pos, s_c, -jnp.inf)
                m_old = m_ref[hh, :, cs]
                m_c = jnp.maximum(m_old, jnp.max(s_c, axis=0, keepdims=True))
                a_c = jnp.exp2(m_old - m_c)
                p_c = jnp.exp2(s_c - m_c)
                m_ref[hh, :, cs] = m_c
                l_ref[hh, :, cs] = a_c * l_ref[hh, :, cs] + jnp.sum(p_c, axis=0, keepdims=True)
                alpha.append(a_c)
                p.append(p_c.astype(BF16))
            acc_ref[hh] = (jnp.concatenate(alpha, axis=1) * acc_ref[hh]
                           + _dot(vtj, jnp.concatenate(p, axis=1)))

    scores(0, 0)

    def body(i, carry):
        j = 2 * i
        scores(j + 1, 1)
        update(j, 0, False)
        scores(j + 2, 0)
        update(j + 1, 1, False)
        return carry

    n_pairs = qi // 2
    lax.fori_loop(0, n_pairs, body, 0)
    j0 = 2 * n_pairs

    @pl.when(qi % 2 == 1)
    def _():
        scores(j0 + 1, 1)
        update(j0, 0, False)
        update(j0 + 1, 1, True)

    @pl.when(qi % 2 == 0)
    def _():
        update(j0, 0, True)

    lam = (jnp.exp(jnp.sum(lq1_ref[...] * lk1_ref[...], axis=-1, keepdims=True))
           - jnp.exp(jnp.sum(lq2_ref[...] * lk2_ref[...], axis=-1, keepdims=True)) + lam_init)
    for hh in range(n_heads):
        l = l_ref[hh]
        acc = acc_ref[hh]
        o12 = acc / l
        o = o12[:, :tq] - lam * o12[:, tq:]
        o = o * lax.rsqrt(jnp.mean(o * o, axis=0, keepdims=True) + EPS)
        o = o * g_ref[...] * (1.0 - lam_init)
        o_ref[:, hh * DIFF_DV:(hh + 1) * DIFF_DV] = o.T.astype(BF16)


def _diff_attn(z, lq1, lk1, lq2, lk2, norm_g, lam_init, bsz, seq):
    t = z.shape[0]
    tq = min(ATT_Q_ROWS, seq)
    nq = seq // tq
    hw = ATT_HEADS_PER_STEP * LANES
    qb, kb, vb = 1024 // hw, 1536 // hw, 2048 // hw
    small = pl.BlockSpec((1, DIFF_D), lambda b, h, i: (0, 0))
    return pl.pallas_call(
        functools.partial(_diff_attn_kernel, lam_init),
        grid=(bsz, DIFF_HEADS // ATT_HEADS_PER_STEP, nq),
        in_specs=[
            pl.BlockSpec((tq, hw), lambda b, h, i: (b * nq + i, qb + h)),
            pl.BlockSpec((seq, hw), lambda b, h, i: (b, kb + h)),
            pl.BlockSpec((seq, hw), lambda b, h, i: (b, vb + h)),
            small, small, small, small,
            pl.BlockSpec((DIFF_DV, 1), lambda b, h, i: (0, 0)),
        ],
        out_specs=pl.BlockSpec((tq, hw), lambda b, h, i: (b * nq + i, h)),
        out_shape=jax.ShapeDtypeStruct((t, DIFF_V_W), BF16),
        scratch_shapes=[pltpu.VMEM((ATT_HEADS_PER_STEP, DIFF_DV, seq), BF16),
                        pltpu.VMEM((2, ATT_HEADS_PER_STEP, tq, 2 * tq), F32),
                        pltpu.VMEM((ATT_HEADS_PER_STEP, 1, 2 * tq), F32),
                        pltpu.VMEM((ATT_HEADS_PER_STEP, 1, 2 * tq), F32),
                        pltpu.VMEM((ATT_HEADS_PER_STEP, DIFF_DV, 2 * tq), F32)],
        compiler_params=_cp(("arbitrary", "arbitrary", "arbitrary"), 32),
        name="diff_attn",
    )(z, z, z, lq1.reshape(1, DIFF_D), lk1.reshape(1, DIFF_D), lq2.reshape(1, DIFF_D),
      lk2.reshape(1, DIFF_D), norm_g.reshape(DIFF_DV, 1))


def _sgu_ret_kernel(su_ref, sv_ref, rv_ref, rg_ref, q_ref, k_ref, ws_ref, bs_ref, o_ref, st_ref):
    @pl.when(pl.program_id(1) == 0)
    def _():
        st_ref[...] = jnp.zeros_like(st_ref)

    c = RET_CHUNK
    row = lax.broadcasted_iota(jnp.int32, (c, c), 0)
    col = lax.broadcasted_iota(jnp.int32, (c, c), 1)
    causal = row >= col
    log_g = [math.log(1.0 - 2.0 ** (-5.0 - h)) for h in range(RET_HEADS)]
    lo = col < RET_DK
    rel = (row - col).astype(F32)
    pos = row.astype(F32)
    w_sgu = [jnp.where(causal, ws_ref[g], 0.0).astype(BF16) for g in range(SGU_GROUPS)]
    decays = [jnp.where(causal, jnp.exp(log_g[h] * jnp.maximum(rel, 0.0)), 0.0)
              for h in range(RET_HEADS)]
    lgs = [jnp.where(lo, log_g[2 * p], log_g[2 * p + 1]) for p in range(RET_HEADS // 2)]
    q_decs = [jnp.exp(lg * (pos + 1.0)) for lg in lgs]
    k_decs = [jnp.exp(lg * (c - 1.0 - pos)) for lg in lgs]
    states = [st_ref[p] for p in range(RET_HEADS // 2)]
    for j in range(su_ref.shape[0] // c):
        rs = slice(j * c, (j + 1) * c)
        for g in range(SGU_GROUPS):
            gs = slice(g * SGU_CH, (g + 1) * SGU_CH)
            s = _dot(w_sgu[g], sv_ref[rs, gs]) + bs_ref[g]
            o_ref[rs, gs] = (su_ref[rs, gs].astype(F32) * s).astype(BF16)
        for p in range(RET_HEADS // 2):
            ls = slice(p * LANES, (p + 1) * LANES)
            q_p = q_ref[rs, ls]
            k_p = k_ref[rs, ls]
            qm = (jnp.where(lo, q_p, 0.0).astype(BF16), jnp.where(lo, 0.0, q_p).astype(BF16))
            s2 = _dot_nt(jnp.concatenate(qm, axis=0), k_p.astype(BF16))
            qd = q_p * q_decs[p]
            qdm = (jnp.where(lo, qd, 0.0).astype(BF16), jnp.where(lo, 0.0, qd).astype(BF16))
            kd = (k_p * k_decs[p]).astype(BF16)
            st_b = states[p].astype(BF16)
            new = []
            for hh in range(2):
                h = 2 * p + hh
                hs = slice(h * RET_DV, (h + 1) * RET_DV)
                s_h = (s2[hh * c:(hh + 1) * c, :] * decays[h]).astype(BF16)
                v_h = rv_ref[rs, hs]
                o = _dot(s_h, v_h) + _dot_nt(qdm[hh], st_b)
                gate = _silu(rg_ref[rs, hs].astype(F32))
                o_ref[rs, SGU_W + h * RET_DV:SGU_W + (h + 1) * RET_DV] = (
                    _rms_rows(o) * gate).astype(BF16)
                new.append(_dot_tn(v_h, kd))
            states[p] = jnp.exp(lgs[p] * float(c)) * states[p] + jnp.where(lo, new[0], new[1])
    for p in range(RET_HEADS // 2):
        st_ref[p] = states[p]


def _sgu_ret(z, f, w_s, b_s, bsz, seq):
    t = z.shape[0]
    cc = RET_CHUNK
    c = min(SGU_RET_ROWS, seq)
    nc = seq // c
    return pl.pallas_call(
        _sgu_ret_kernel,
        grid=(bsz, nc),
        in_specs=[
            pl.BlockSpec((c, SGU_W), lambda b, i: (b * nc + i, 0)),
            pl.BlockSpec((c, SGU_W), lambda b, i: (b * nc + i, 1)),
            pl.BlockSpec((c, RET_V_W), lambda b, i: (b * nc + i, 2)),
            pl.BlockSpec((c, RET_V_W), lambda b, i: (b * nc + i, 3)),
            pl.BlockSpec((c, RET_QK_W), lambda b, i: (b * nc + i, 0)),
            pl.BlockSpec((c, RET_QK_W), lambda b, i: (b * nc + i, 1)),
            pl.BlockSpec((SGU_GROUPS, cc, cc), lambda b, i: (0, 0, 0)),
            pl.BlockSpec((SGU_GROUPS, cc, 1), lambda b, i: (0, 0, 0)),
        ],
        out_specs=pl.BlockSpec((c, SGU_W + RET_V_W), lambda b, i: (b * nc + i, 0)),
        out_shape=jax.ShapeDtypeStruct((t, SGU_W + RET_V_W), BF16),
        scratch_shapes=[pltpu.VMEM((2, LANES, LANES), F32)],
        compiler_params=_cp(("arbitrary", "arbitrary"), 32),
        name="sgu_retention",
    )(z, z, z, z, f, f, w_s, b_s.reshape(SGU_GROUPS, cc, 1))


def _out_proj_kernel(n_in, *refs):
    o_refs = refs[:n_in]
    w_ref, x_ref, mod_ref, rw_ref, rb_ref, xn_ref, h_ref, lg_ref = refs[n_in:]
    k_each = D_MODEL // n_in
    y = _dot(o_refs[0][...], w_ref[0:k_each, :])
    for n in range(1, n_in):
        y = y + _dot(o_refs[n][...], w_ref[n * k_each:(n + 1) * k_each, :])
    xn = x_ref[...] + mod_ref[0, 2:3, :] * y
    xn_ref[...] = xn
    h = _rms_rows(xn) * (1.0 + mod_ref[0, 4:5, :]) + mod_ref[0, 3:4, :]
    hb = h.astype(BF16)
    h_ref[...] = hb
    lg_ref[...] = _dot_nt(rw_ref[...], hb) + rb_ref[...]


def _out_proj(mixed, w_out, x2, mod, router_w, router_b, seq):
    t, d = x2.shape
    tm = min(PROJ_ROWS, seq)
    n_in = len(mixed)
    k_each = d // n_in
    rw = router_w.T.astype(BF16)
    rb = router_b.reshape(N_EXPERTS, 1)
    return pl.pallas_call(
        functools.partial(_out_proj_kernel, n_in),
        grid=(t // tm,),
        in_specs=[pl.BlockSpec((tm, k_each), lambda i: (i, 0)) for _ in mixed] + [
            pl.BlockSpec((d, d), lambda i: (0, 0)),
            pl.BlockSpec((tm, d), lambda i: (i, 0)),
            pl.BlockSpec((1, 6, d), lambda i: (i // (seq // tm), 0, 0)),
            pl.BlockSpec((N_EXPERTS, d), lambda i: (0, 0)),
            pl.BlockSpec((N_EXPERTS, 1), lambda i: (0, 0)),
        ],
        out_specs=[pl.BlockSpec((tm, d), lambda i: (i, 0)),
                   pl.BlockSpec((tm, d), lambda i: (i, 0)),
                   pl.BlockSpec((N_EXPERTS, tm), lambda i: (0, i))],
        out_shape=[jax.ShapeDtypeStruct((t, d), F32),
                   jax.ShapeDtypeStruct((t, d), BF16),
                   jax.ShapeDtypeStruct((N_EXPERTS, t), F32)],
        compiler_params=_cp(("arbitrary",), 48),
        name="out_proj",
    )(*mixed, w_out.astype(BF16), x2, mod, rw, rb)


def _route_kernel(lg_ref, idx_ref, gate_ref, tcnt_ref, toff_ref, trun_ref, cnt_ref, run_ref):
    @pl.when(pl.program_id(0) == 0)
    def _():
        run_ref[...] = jnp.zeros_like(run_ref)

    tm = lg_ref.shape[1]
    row = lax.broadcasted_iota(jnp.int32, (N_EXPERTS, tm), 0)
    neg = -jnp.inf
    l = lg_ref[...]
    vals, firsts, hots = [], [], []
    for _ in range(TOP_K):
        m = jnp.max(l, axis=0, keepdims=True)
        first = jnp.min(jnp.where(l == m, row, N_EXPERTS), axis=0, keepdims=True)
        hot = row == first
        vals.append(m)
        firsts.append(first)
        hots.append(hot)
        l = jnp.where(hot, neg, l)
    sel = hots[0] | hots[1] | hots[2] | hots[3]
    ex = [jnp.exp(v - vals[0]) for v in vals]
    denom = ex[0] + ex[1] + ex[2] + ex[3]
    r_i = lax.broadcasted_iota(jnp.int32, (tm, tm), 0)
    c_i = lax.broadcasted_iota(jnp.int32, (tm, tm), 1)
    before = (r_i < c_i).astype(BF16)
    earlier = _dot(sel.astype(BF16), before)
    tile_cnt = jnp.sum(sel.astype(F32), axis=1, keepdims=True)
    lower = jnp.zeros((N_EXPERTS, tm), F32)
    for k in range(TOP_K):
        lower = lower + (firsts[k] < row).astype(F32)
    tile_off = jnp.sum(lower, axis=1, keepdims=True)
    slot = tile_off + earlier
    row8 = lax.broadcasted_iota(jnp.int32, (2 * TOP_K, tm), 0)
    idx_out = jnp.zeros((2 * TOP_K, tm), jnp.int32)
    gate_out = jnp.zeros((2 * TOP_K, tm), F32)
    for k in range(TOP_K):
        pos_k = jnp.sum(jnp.where(hots[k], slot, 0.0), axis=0, keepdims=True).astype(jnp.int32)
        idx_out = jnp.where(row8 == k, firsts[k], idx_out)
        idx_out = jnp.where(row8 == TOP_K + k, pos_k, idx_out)
        gate_out = jnp.where(row8 == k, ex[k] / denom, gate_out)
    idx_ref[...] = idx_out
    gate_ref[...] = gate_out
    run = run_ref[:, 0:1]
    tcnt_ref[...] = jnp.broadcast_to(tile_cnt, tcnt_ref.shape)
    toff_ref[...] = jnp.broadcast_to(tile_off, toff_ref.shape)
    trun_ref[...] = jnp.broadcast_to(run, trun_ref.shape)
    total = run + tile_cnt
    run_ref[...] = jnp.broadcast_to(total, run_ref.shape)
    cnt_ref[...] = jnp.broadcast_to(total, cnt_ref.shape)


def _route(logits_t):
    t = logits_t.shape[1]
    tm = min(MOE_TILE, t)
    nt = t // tm
    per_tile = pl.BlockSpec((N_EXPERTS, LANES), lambda i: (i, 0))
    per_tile_shape = jax.ShapeDtypeStruct((nt * N_EXPERTS, LANES), F32)
    return pl.pallas_call(
        _route_kernel,
        grid=(nt,),
        in_specs=[pl.BlockSpec((N_EXPERTS, tm), lambda i: (0, i))],
        out_specs=[pl.BlockSpec((2 * TOP_K, tm), lambda i: (0, i)),
                   pl.BlockSpec((2 * TOP_K, tm), lambda i: (0, i)),
                   per_tile, per_tile, per_tile,
                   pl.BlockSpec((N_EXPERTS, LANES), lambda i: (0, 0))],
        out_shape=[jax.ShapeDtypeStruct((2 * TOP_K, t), jnp.int32),
                   jax.ShapeDtypeStruct((2 * TOP_K, t), F32),
                   per_tile_shape, per_tile_shape, per_tile_shape,
                   jax.ShapeDtypeStruct((N_EXPERTS, LANES), F32)],
        scratch_shapes=[pltpu.VMEM((N_EXPERTS, LANES), F32)],
        compiler_params=_cp(("arbitrary",), 32),
        name="route",
    )(logits_t)


def _rows(ref, start, n):
    return ref.at[pl.ds(pl.multiple_of(start * ROW_TILES, ROW_TILES), n * ROW_TILES), :]


def _run_copies(tile, cnt_ref, off_ref, dst_ref, make_copy):
    def per_expert(e, carry):
        j = tile * N_EXPERTS + e
        cnt, off, dst = cnt_ref[j], off_ref[j], dst_ref[j]

        for b in reversed(range(RUN_BITS)):
            size = 1 << b

            @pl.when((cnt & size) != 0)
            def _():
                done = lax.shift_left(lax.shift_right_logical(cnt, b + 1), b + 1)
                make_copy(off + done, dst + done, size).start()
        return carry

    lax.fori_loop(0, N_EXPERTS, per_expert, 0)


def _dispatch_kernel(fs_ref, fl_ref, cnt_ref, off_ref, dst_ref, h_ref, idx_ref, xs_ref,
                     stage, zero_buf, sems, zsem):
    i = pl.program_id(0)
    n_steps = pl.num_programs(0)
    cur = lax.rem(i, 2)
    tm = h_ref.shape[0]
    n_sorted = TOP_K * tm
    n_fill = fs_ref.shape[0]

    def fill_copies(f, wait):
        start, n = fs_ref[f], fl_ref[f]
        n_chunks = lax.shift_right_logical(n, FILL_SHIFT)
        tail = start + n_chunks * FILL_ROWS

        def chunk(j, carry):
            cp = pltpu.make_async_copy(zero_buf, _rows(xs_ref, start + j * FILL_ROWS, FILL_ROWS),
                                       zsem)
            cp.wait() if wait else cp.start()
            return carry

        def single(j, carry):
            cp = pltpu.make_async_copy(_rows(zero_buf, 0, 1), _rows(xs_ref, tail + j, 1), zsem)
            cp.wait() if wait else cp.start()
            return carry

        lax.fori_loop(0, n_chunks, chunk, 0)
        lax.fori_loop(0, n - n_chunks * FILL_ROWS, single, 0)

    @pl.when(i == 0)
    def _():
        zero_buf[...] = jnp.zeros_like(zero_buf)
        lax.fori_loop(0, n_fill, lambda f, c: (fill_copies(f, False), c)[1], 0)

    h = h_ref[...]
    pos = [idx_ref[TOP_K + k:TOP_K + k + 1, :] for k in range(TOP_K)]
    for sb in range(n_sorted // SORT_SLAB):
        j = lax.broadcasted_iota(jnp.int32, (SORT_SLAB, tm), 0) + sb * SORT_SLAB
        hit = (pos[0] == j) | (pos[1] == j) | (pos[2] == j) | (pos[3] == j)
        perm = jnp.where(hit, 1.0, 0.0).astype(BF16)
        rows = _dot(perm, h)
        for c in range(ROW_TILES):
            stage[cur, pl.ds(sb * SORT_SLAB * ROW_TILES + c, SORT_SLAB, stride=ROW_TILES), :] = (
                rows[:, c * LANES:(c + 1) * LANES])

    def make_copy(src_row, dst_row, size):
        return pltpu.make_async_copy(_rows(stage.at[cur], src_row, size),
                                     _rows(xs_ref, dst_row, size), sems.at[cur])

    _run_copies(i, cnt_ref, off_ref, dst_ref, make_copy)

    def wait_slot(slot):
        pltpu.make_async_copy(stage.at[slot], _rows(xs_ref, 0, n_sorted), sems.at[slot]).wait()

    @pl.when(i > 0)
    def _():
        wait_slot(1 - cur)

    @pl.when(i == n_steps - 1)
    def _():
        wait_slot(cur)

    @pl.when(i == 0)
    def _():
        lax.fori_loop(0, n_fill, lambda f, c: (fill_copies(f, True), c)[1], 0)


def _dispatch(h2, idx, tile_cnt, tile_off, tile_dst, fill_start, fill_len, n_rows):
    t, d = h2.shape
    tm = min(MOE_TILE, t)
    grid_spec = pltpu.PrefetchScalarGridSpec(
        num_scalar_prefetch=5,
        grid=(t // tm,),
        in_specs=[pl.BlockSpec((tm, d), lambda i, *_: (i, 0)),
                  pl.BlockSpec((2 * TOP_K, tm), lambda i, *_: (0, i))],
        out_specs=pl.BlockSpec(memory_space=pl.ANY),
        scratch_shapes=[pltpu.VMEM((2, TOP_K * tm * ROW_TILES, LANES), F32),
                        pltpu.VMEM((FILL_ROWS * ROW_TILES, LANES), F32),
                        pltpu.SemaphoreType.DMA((2,)), pltpu.SemaphoreType.DMA(())],
    )
    return pl.pallas_call(
        _dispatch_kernel,
        grid_spec=grid_spec,
        out_shape=jax.ShapeDtypeStruct((n_rows * ROW_TILES, LANES), F32),
        compiler_params=_cp(("arbitrary",), 48),
        name="dispatch",
    )(fill_start, fill_len, tile_cnt, tile_off, tile_dst, h2, idx)


def _expert_kernel(layer, be_ref, nu_ref, nx_ref, nv_ref, xs_ref, wi_hbm, bi_ref, wo_hbm, bo_ref,
                   y_ref, wi_st, wo_st, wi_b, wo_b, sems):
    i = pl.program_id(0)
    tb = xs_ref.shape[0] // ROW_TILES
    e = be_ref[i]
    fresh = jnp.logical_or(i == 0, e != be_ref[jnp.maximum(i - 1, 0)])
    used = i < nu_ref[0]

    def fetch(ex):
        return (pltpu.make_async_copy(wi_hbm.at[layer, ex], wi_st, sems.at[0]),
                pltpu.make_async_copy(wo_hbm.at[layer, ex], wo_st, sems.at[1]))

    @pl.when(i == 0)
    def _():
        for cp in fetch(e):
            cp.start()

    def take_weights():
        for cp in fetch(e):
            cp.wait()
        wi_b[...] = wi_st[...].astype(BF16)
        wo_b[...] = wo_st[...].astype(BF16)

    def fetch_next():
        @pl.when(nx_ref[i] >= 0)
        def _():
            for cp in fetch(nx_ref[i]):
                cp.start()

    def compute(n):
        x = jnp.concatenate(
            [xs_ref[pl.ds(c, n, stride=ROW_TILES), :] for c in range(ROW_TILES)],
            axis=1).astype(BF16)
        y = jnp.zeros((n, D_MODEL), F32) + bo_ref[0, 0]
        half = 1024
        for j in range(D_FF // half):
            a, b = j * half, (j + 1) * half
            glu = _dot(x, wi_b[:, a:b]) + bi_ref[0, 0, :, a:b]
            lin = _dot(x, wi_b[:, D_FF + a:D_FF + b]) + bi_ref[0, 0, :, D_FF + a:D_FF + b]
            glu = jnp.minimum(glu, SWIGLU_LIMIT)
            lin = jnp.clip(lin, -SWIGLU_LIMIT, SWIGLU_LIMIT)
            act = glu * jax.nn.sigmoid(SWIGLU_ALPHA * glu) * (lin + 1.0)
            y = y + _dot(act.astype(BF16), wo_b[a:b, :])
        for c in range(ROW_TILES):
            y_ref[pl.ds(c, n, stride=ROW_TILES), :] = y[:, c * LANES:(c + 1) * LANES]

    full = nv_ref[i] > tb // 2

    @pl.when(jnp.logical_and(used, jnp.logical_and(full, fresh)))
    def _():
        take_weights()
        compute(tb)
        fetch_next()

    @pl.when(jnp.logical_and(used, jnp.logical_and(full, jnp.logical_not(fresh))))
    def _():
        compute(tb)

    @pl.when(jnp.logical_and(used, jnp.logical_not(full)))
    def _():
        @pl.when(fresh)
        def _():
            take_weights()
            fetch_next()

        compute(tb // 2)
        y_ref[pl.ds(tb // 2 * ROW_TILES, tb // 2 * ROW_TILES), :] = jnp.zeros(
            (tb // 2 * ROW_TILES, LANES), F32)

    @pl.when(jnp.logical_not(used))
    def _():
        y_ref[...] = jnp.zeros_like(y_ref)


def _experts(xs, block_e, n_used, next_e, n_valid, layer, w_in, b_in, w_out, b_out):
    tb = EXPERT_ROWS
    n_rows = xs.shape[0] // ROW_TILES
    nb = n_rows // tb
    depth, ne, d, f2 = w_in.shape

    def row_map(i, be, nu, nx, nv):
        return (jnp.minimum(i, nu[0] - 1), 0)

    grid_spec = pltpu.PrefetchScalarGridSpec(
        num_scalar_prefetch=4,
        grid=(nb,),
        in_specs=[
            pl.BlockSpec((tb * ROW_TILES, LANES), row_map),
            pl.BlockSpec(memory_space=pl.ANY),
            pl.BlockSpec((1, 1, 1, f2), lambda i, be, nu, nx, nv: (layer, be[i], 0, 0)),
            pl.BlockSpec(memory_space=pl.ANY),
            pl.BlockSpec((1, 1, 1, d), lambda i, be, nu, nx, nv: (layer, be[i], 0, 0)),
        ],
        out_specs=pl.BlockSpec((tb * ROW_TILES, LANES), lambda i, be, nu, nx, nv: (i, 0)),
        scratch_shapes=[pltpu.VMEM((d, f2), F32), pltpu.VMEM((D_FF, d), F32),
                        pltpu.VMEM((d, f2), BF16), pltpu.VMEM((D_FF, d), BF16),
                        pltpu.SemaphoreType.DMA((2,))],
    )
    return pl.pallas_call(
        functools.partial(_expert_kernel, layer),
        grid_spec=grid_spec,
        out_shape=jax.ShapeDtypeStruct((n_rows * ROW_TILES, LANES), F32),
        compiler_params=_cp(("arbitrary",), 56),
        name="experts",
    )(block_e, n_used, next_e, n_valid, xs, w_in, b_in.reshape(depth, ne, 1, f2), w_out,
      b_out.reshape(depth, ne, 1, d))


def _combine_kernel(final, cnt_ref, off_ref, dst_ref, pos_ref, gate_ref, x_ref, mod_ref, fg_ref,
                    yb_ref, o_ref, stage, sems):
    tm = x_ref.shape[0]
    n_sorted = TOP_K * tm
    i = pl.program_id(0)
    cur = lax.rem(i, 2)

    def fetch(tile, slot):
        def make_copy(sorted_row, src_row, size):
            return pltpu.make_async_copy(_rows(yb_ref, src_row, size),
                                         _rows(stage.at[slot], sorted_row, size), sems.at[slot])

        _run_copies(tile, cnt_ref, off_ref, dst_ref, make_copy)

    @pl.when(i == 0)
    def _():
        fetch(0, 0)

    @pl.when(i + 1 < pl.num_programs(0))
    def _():
        fetch(i + 1, 1 - cur)

    pltpu.make_async_copy(_rows(yb_ref, 0, n_sorted), stage.at[cur], sems.at[cur]).wait()

    pos = [pos_ref[:, TOP_K + k:TOP_K + k + 1] for k in range(TOP_K)]
    gate = [gate_ref[:, k:k + 1] for k in range(TOP_K)]
    y = jnp.zeros((tm, D_MODEL), F32)
    for sb in range(n_sorted // SORT_SLAB):
        rows = jnp.concatenate(
            [stage[cur, pl.ds(sb * SORT_SLAB * ROW_TILES + c, SORT_SLAB, stride=ROW_TILES), :]
             for c in range(ROW_TILES)], axis=1).astype(BF16)
        j = lax.broadcasted_iota(jnp.int32, (tm, SORT_SLAB), 1) + sb * SORT_SLAB
        w = jnp.zeros((tm, SORT_SLAB), F32)
        for k in range(TOP_K):
            w = w + jnp.where(pos[k] == j, gate[k], 0.0)
        y = y + _dot(w.astype(BF16), rows)
    o = x_ref[...] + mod_ref[0, 5:6, :] * y
    if final:
        o = _rms_rows(o) * fg_ref[...]
    o_ref[...] = o


def _combine(yb, tile_cnt, tile_off, tile_dst, pos_t, gates_t, x2, mod, final_g, final, seq):
    t, d = x2.shape
    tm = min(MOE_TILE, seq)
    grid_spec = pltpu.PrefetchScalarGridSpec(
        num_scalar_prefetch=3,
        grid=(t // tm,),
        in_specs=[
            pl.BlockSpec((tm, 2 * TOP_K), lambda i, *_: (i, 0)),
            pl.BlockSpec((tm, 2 * TOP_K), lambda i, *_: (i, 0)),
            pl.BlockSpec((tm, d), lambda i, *_: (i, 0)),
            pl.BlockSpec((1, 6, d), lambda i, *_: (i // (seq // tm), 0, 0)),
            pl.BlockSpec((1, d), lambda i, *_: (0, 0)),
            pl.BlockSpec(memory_space=pl.ANY),
        ],
        out_specs=pl.BlockSpec((tm, d), lambda i, *_: (i, 0)),
        scratch_shapes=[pltpu.VMEM((2, TOP_K * tm * ROW_TILES, LANES), F32),
                        pltpu.SemaphoreType.DMA((2,))],
    )
    return pl.pallas_call(
        functools.partial(_combine_kernel, final),
        grid_spec=grid_spec,
        out_shape=jax.ShapeDtypeStruct((t, d), F32),
        compiler_params=_cp(("arbitrary",), 48),
        name="combine",
    )(tile_cnt, tile_off, tile_dst, pos_t, gates_t, x2, mod, final_g.reshape(1, d), yb)


def _moe(h2, logits, x2, mod, layer, w_in, b_in, w_out, b_out, final_g, final, seq):
    t = x2.shape[0]
    tb = EXPERT_ROWS
    idx, gates, tcnt, toff, trun, cnt = _route(logits)
    counts = cnt[:, 0].astype(jnp.int32)
    nblk = (counts + tb - 1) // tb
    blk_end = jnp.cumsum(nblk)
    pad_start = (blk_end - nblk) * tb
    n_blocks = (t * TOP_K) // tb + N_EXPERTS
    n_used = blk_end[-1:]
    experts = jnp.arange(N_EXPERTS, dtype=jnp.int32)
    last_e = jnp.max(jnp.where(nblk > 0, experts, 0))
    blk = jnp.arange(n_blocks, dtype=jnp.int32)
    block_e = jnp.minimum(
        jnp.sum((blk_end[None, :] <= blk[:, None]).astype(jnp.int32), axis=1), last_e)
    later = (experts[None, :] > block_e[:, None]) & (nblk[None, :] > 0)
    next_e = jnp.min(jnp.where(later, experts[None, :], N_EXPERTS), axis=1)
    next_e = jnp.where(next_e == N_EXPERTS, -1, next_e).astype(jnp.int32)
    tile_cnt = tcnt[:, 0].astype(jnp.int32)
    tile_off = toff[:, 0].astype(jnp.int32)
    tile_dst = (trun[:, 0].astype(jnp.int32).reshape(-1, N_EXPERTS) + pad_start[None, :]).reshape(-1)
    fill_start = jnp.concatenate([pad_start + counts, blk_end[-1:] * tb]).astype(jnp.int32)
    fill_len = jnp.concatenate([nblk * tb - counts, (n_blocks - blk_end[-1:]) * tb]).astype(jnp.int32)
    xs = _dispatch(h2, idx, tile_cnt, tile_off, tile_dst, fill_start, fill_len, n_blocks * tb)
    row_end = jnp.sum(jnp.where(block_e[:, None] == experts[None, :],
                                (pad_start + counts)[None, :], 0), axis=1)
    n_valid = jnp.clip(row_end - blk * tb, 0, tb).astype(jnp.int32)
    yb = _experts(xs, block_e, n_used.astype(jnp.int32), next_e, n_valid, layer,
                  w_in, b_in, w_out, b_out)
    return _combine(yb, tile_cnt, tile_off, tile_dst, idx.T, gates.T, x2, mod, final_g, final, seq)


def kernel(x, c, positions, w_ada, b_ada, even_w_in, gla_w_gate, gla_b_gate, gla_norm_g,
           diff_lam_q1, diff_lam_k1, diff_lam_q2, diff_lam_k2, diff_norm_g, even_w_out,
           odd_w_in, sgu_ln_g, sgu_ln_b, sgu_w, sgu_b, odd_w_out,
           router_w, router_b, expert_w_in, expert_b_in, expert_w_out, expert_b_out,
           final_norm_g):
    bsz, seq, d = x.shape
    depth = w_ada.shape[0]
    t = bsz * seq
    mods = _modulation(c, w_ada, b_ada).reshape(depth, bsz, 6, d)
    cos, sin = _rope_tables(positions)
    x2 = x.reshape(t, d)
    for layer in range(depth):
        mod = mods[layer]
        j = layer // 2
        if layer % 2 == 0:
            z, f = _proj_even(x2, mod, even_w_in[j], gla_w_gate[j], gla_b_gate[j], cos, sin, seq)
            o_gla = _gla(z, f, gla_norm_g[j], bsz, seq)
            lam_init = 0.8 - 0.6 * math.exp(-0.3 * layer)
            o_diff = _diff_attn(z, diff_lam_q1[j], diff_lam_k1[j], diff_lam_q2[j], diff_lam_k2[j],
                                diff_norm_g[j], lam_init, bsz, seq)
            mixed, w_out = (o_gla, o_diff), even_w_out[j]
        else:
            z, f = _proj_odd(x2, mod, odd_w_in[j], sgu_ln_g[j], sgu_ln_b[j], cos, sin, seq)
            mixed, w_out = (_sgu_ret(z, f, sgu_w[j], sgu_b[j], bsz, seq),), odd_w_out[j]
        x2, h2, logits = _out_proj(mixed, w_out, x2, mod, router_w[layer], router_b[layer], seq)
        x2 = _moe(h2, logits, x2, mod, layer, expert_w_in, expert_b_in, expert_w_out,
                  expert_b_out, final_norm_g, layer == depth - 1, seq)
    return x2.reshape(bsz, seq, d)
```

```python
import functools
import math

import jax
import jax.numpy as jnp
from jax import lax
from jax.experimental import pallas as pl
from jax.experimental.pallas import tpu as pltpu

F32 = jnp.float32
BF16 = jnp.bfloat16

D_MODEL = 1024
EPS = 1e-6
ROPE_THETA = 10000.0
ROPE_HALF = 32

GLA_HEADS = 4
GLA_DK = 64
GLA_DV = 128
GLA_RANK = 16
GLA_CHUNK = 64
GLA_GATE_NORMALIZER = 16.0
GLA_QK_W = GLA_HEADS * GLA_DK
GLA_V_W = GLA_HEADS * GLA_DV

DIFF_HEADS = 4
DIFF_D = 64
DIFF_DV = 128
DIFF_QK_W = DIFF_HEADS * 2 * DIFF_D
DIFF_V_W = DIFF_HEADS * DIFF_DV

SGU_GROUPS = 4
SGU_CH = 128
SGU_CHUNK = 128
SGU_W = SGU_GROUPS * SGU_CH

RET_HEADS = 4
RET_DK = 64
RET_DV = 128
RET_CHUNK = 128
RET_QK_W = RET_HEADS * RET_DK
RET_V_W = RET_HEADS * RET_DV

N_EXPERTS = 32
TOP_K = 4
D_FF = D_MODEL
SWIGLU_ALPHA = 1.702
SWIGLU_LIMIT = 7.0

LANES = 128
SUBLANES = 8
ROW_TILES = D_MODEL // LANES

PROJ_ROWS = 1024
GLA_ROWS = 512
GLA_SEQS_PER_STEP = 2
GLA_GROUP = 4
SGU_RET_ROWS = 512
ATT_Q_ROWS = 256
ATT_HEADS_PER_STEP = 4
MOE_TILE = 512
SORT_SLAB = 512
RUN_BITS = 10
FILL_SHIFT = 6
FILL_ROWS = 1 << FILL_SHIFT
EXPERT_ROWS = 512
MOD_COLS = 1536

MIB = 1024 * 1024


def _cp(semantics, vmem_mib):
    return pltpu.CompilerParams(dimension_semantics=semantics, vmem_limit_bytes=vmem_mib * MIB)


def _dot(a, b):
    return jnp.dot(a, b, preferred_element_type=F32)


def _dot_nt(a, b):
    return lax.dot_general(a, b, (((1,), (1,)), ((), ())), preferred_element_type=F32)


def _dot_tn(a, b):
    return lax.dot_general(a, b, (((0,), (0,)), ((), ())), preferred_element_type=F32)


def _rms_rows(x):
    return x * lax.rsqrt(jnp.mean(x * x, axis=-1, keepdims=True) + EPS)


def _silu(x):
    return x * jax.nn.sigmoid(x)


def _rope_chunk(v, cos, sin, lo_half):
    rot = jnp.where(lo_half, -pltpu.roll(v, 96, 1), pltpu.roll(v, 32, 1))
    return v * cos + rot * sin


def _mod_kernel(c_ref, w_ref, b_ref, o_ref):
    c = c_ref[...]
    ca = _silu(c).astype(BF16)
    o_ref[0] = _dot(ca, w_ref[0].astype(BF16)) + b_ref[0]


def _modulation(c, w_ada, b_ada):
    depth, d, n = w_ada.shape
    bsz = c.shape[0]
    return pl.pallas_call(
        _mod_kernel,
        grid=(depth, n // MOD_COLS),
        in_specs=[
            pl.BlockSpec((bsz, d), lambda l, j: (0, 0)),
            pl.BlockSpec((1, d, MOD_COLS), lambda l, j: (l, 0, j)),
            pl.BlockSpec((1, 1, MOD_COLS), lambda l, j: (l, 0, j)),
        ],
        out_specs=pl.BlockSpec((1, bsz, MOD_COLS), lambda l, j: (l, 0, j)),
        out_shape=jax.ShapeDtypeStruct((depth, bsz, n), F32),
        compiler_params=_cp(("arbitrary", "arbitrary"), 40),
        name="adaln_mod",
    )(c, w_ada, b_ada.reshape(depth, 1, n))


def _rope_table_kernel(p_ref, f_ref, c_ref, s_ref):
    ang = p_ref[...].astype(F32) * f_ref[...]
    c_ref[...] = jnp.cos(ang)
    s_ref[...] = jnp.sin(ang)


def _rope_tables(positions):
    t = positions.size
    per_row = LANES // ROPE_HALF
    rows = t // per_row
    pos_d = jnp.repeat(positions.reshape(rows, per_row), ROPE_HALF, axis=1)
    inv_freq = ROPE_THETA ** (-jnp.arange(ROPE_HALF, dtype=F32) / ROPE_HALF)
    freq_d = jnp.tile(inv_freq, per_row).reshape(1, LANES)
    tr = min(512, rows)
    cos_d, sin_d = pl.pallas_call(
        _rope_table_kernel,
        grid=(rows // tr,),
        in_specs=[pl.BlockSpec((tr, LANES), lambda i: (i, 0)),
                  pl.BlockSpec((1, LANES), lambda i: (0, 0))],
        out_specs=[pl.BlockSpec((tr, LANES), lambda i: (i, 0))] * 2,
        out_shape=[jax.ShapeDtypeStruct((rows, LANES), F32)] * 2,
        compiler_params=_cp(("arbitrary",), 32),
        name="rope_tables",
    )(pos_d, freq_d)
    cos = jnp.tile(cos_d.reshape(t, ROPE_HALF), (1, per_row))
    sin = jnp.tile(sin_d.reshape(t, ROPE_HALF), (1, per_row))
    return cos, sin


def _modulated_rms(x_ref, mod_ref, which):
    x = x_ref[...]
    shift = mod_ref[0, 3 * which:3 * which + 1, :]
    scale = mod_ref[0, 3 * which + 1:3 * which + 2, :]
    return _rms_rows(x) * (1.0 + scale) + shift


EVEN_Z = (GLA_V_W, GLA_V_W, DIFF_QK_W, DIFF_QK_W, DIFF_V_W)
EVEN_Z_W = sum(EVEN_Z)
EVEN_F_W = 3 * GLA_QK_W
EVEN_W_COLS = EVEN_Z_W + 2 * GLA_QK_W + LANES


def _proj_even_kernel(x_ref, mod_ref, w_ref, wg_ref, bg_ref, cos_ref, sin_ref, z_ref, f_ref):
    hb = _modulated_rms(x_ref, mod_ref, 0).astype(BF16)

    def sec(a, b):
        return _dot(hb, w_ref[:, a:b])

    for a, b in ((0, 512), (512, 1024), (2048, 2560)):
        z_ref[:, a:b] = sec(a, b).astype(BF16)
    cos = cos_ref[...]
    sin = sin_ref[...]
    lo_half = (lax.broadcasted_iota(jnp.int32, cos.shape, 1) % 64) < ROPE_HALF
    q_scale = DIFF_D ** -0.5 * math.log2(math.e)
    for a, scl in ((1024, q_scale), (1536, 1.0)):
        full = sec(a, a + DIFF_QK_W)
        for c in range(DIFF_QK_W // LANES):
            v = full[:, c * LANES:(c + 1) * LANES]
            z_ref[:, a + c * LANES:a + (c + 1) * LANES] = (
                _rope_chunk(v, cos, sin, lo_half) * scl).astype(BF16)
    f_ref[:, 0:256] = sec(2560, 2816) * (GLA_DK ** -0.5)
    f_ref[:, 256:512] = sec(2816, 3072)
    gr = sec(3072, 3200).astype(BF16)
    pre = _dot(gr, wg_ref[...]) + bg_ref[...]
    log_sig = jnp.minimum(pre, 0.0) - jnp.log1p(jnp.exp(-jnp.abs(pre)))
    f_ref[:, 512:768] = log_sig / GLA_GATE_NORMALIZER


def _proj_even(x2, mod, w_in, w_gate, b_gate, cos, sin, seq):
    t, d = x2.shape
    tm = min(PROJ_ROWS, seq)
    gq, gk, gv, gr, gg, dq, dk, dv = jnp.split(
        w_in, [256, 512, 1024, 1040, 1552, 2064, 2576], axis=1)
    gr_pad = jnp.pad(gr, ((0, 0), (0, LANES - GLA_RANK)))
    w_cat = jnp.concatenate([gv, gg, dq, dk, dv, gq, gk, gr_pad], axis=1).astype(BF16)
    wg_pad = jnp.pad(w_gate, ((0, LANES - GLA_RANK), (0, 0))).astype(BF16)
    return pl.pallas_call(
        _proj_even_kernel,
        grid=(t // tm,),
        in_specs=[
            pl.BlockSpec((tm, d), lambda i: (i, 0)),
            pl.BlockSpec((1, 6, d), lambda i: (i // (seq // tm), 0, 0)),
            pl.BlockSpec((d, EVEN_W_COLS), lambda i: (0, 0)),
            pl.BlockSpec((LANES, GLA_QK_W), lambda i: (0, 0)),
            pl.BlockSpec((1, GLA_QK_W), lambda i: (0, 0)),
            pl.BlockSpec((tm, LANES), lambda i: (i, 0)),
            pl.BlockSpec((tm, LANES), lambda i: (i, 0)),
        ],
        out_specs=[pl.BlockSpec((tm, EVEN_Z_W), lambda i: (i, 0)),
                   pl.BlockSpec((tm, EVEN_F_W), lambda i: (i, 0))],
        out_shape=[jax.ShapeDtypeStruct((t, EVEN_Z_W), BF16),
                   jax.ShapeDtypeStruct((t, EVEN_F_W), F32)],
        compiler_params=_cp(("arbitrary",), 48),
        name="proj_even",
    )(x2, mod, w_cat, wg_pad, b_gate.reshape(1, GLA_QK_W), cos, sin)


ODD_Z_W = 2 * SGU_W + 2 * RET_V_W
ODD_F_W = 2 * RET_QK_W
ODD_W_COLS = ODD_Z_W + ODD_F_W


def _gelu_exact(x):
    return 0.5 * x * (1.0 + lax.erf(x * (2.0 ** -0.5)))


def _proj_odd_kernel(x_ref, mod_ref, w_ref, lng_ref, lnb_ref, cos_ref, sin_ref, z_ref, f_ref):
    hb = _modulated_rms(x_ref, mod_ref, 0).astype(BF16)

    def sec(a, b):
        return _dot(hb, w_ref[:, a:b])

    z_ref[:, 0:512] = _gelu_exact(sec(0, 512)).astype(BF16)
    sv = _gelu_exact(sec(512, 1024))
    mu = jnp.mean(sv, axis=-1, keepdims=True)
    cen = sv - mu
    var = jnp.mean(cen * cen, axis=-1, keepdims=True)
    z_ref[:, 512:1024] = (cen * lax.rsqrt(var + EPS) * lng_ref[...] + lnb_ref[...]).astype(BF16)
    for a, b in ((1024, 1536), (1536, 2048)):
        z_ref[:, a:b] = sec(a, b).astype(BF16)
    cos = cos_ref[...]
    sin = sin_ref[...]
    lo_half = (lax.broadcasted_iota(jnp.int32, cos.shape, 1) % 64) < ROPE_HALF
    k_scale = RET_DK ** -0.5
    for a, scl in ((0, 1.0), (256, k_scale)):
        full = sec(ODD_Z_W + a, ODD_Z_W + a + RET_QK_W)
        for c in range(RET_QK_W // LANES):
            v = full[:, c * LANES:(c + 1) * LANES]
            f_ref[:, a + c * LANES:a + (c + 1) * LANES] = _rope_chunk(v, cos, sin, lo_half) * scl


def _proj_odd(x2, mod, w_in, ln_g, ln_b, cos, sin, seq):
    t, d = x2.shape
    tm = min(PROJ_ROWS, seq)
    su, sv, rq, rk, rv, rg = jnp.split(w_in, [512, 1024, 1280, 1536, 2048], axis=1)
    w_cat = jnp.concatenate([su, sv, rv, rg, rq, rk], axis=1).astype(BF16)
    return pl.pallas_call(
        _proj_odd_kernel,
        grid=(t // tm,),
        in_specs=[
            pl.BlockSpec((tm, d), lambda i: (i, 0)),
            pl.BlockSpec((1, 6, d), lambda i: (i // (seq // tm), 0, 0)),
            pl.BlockSpec((d, ODD_W_COLS), lambda i: (0, 0)),
            pl.BlockSpec((1, SGU_W), lambda i: (0, 0)),
            pl.BlockSpec((1, SGU_W), lambda i: (0, 0)),
            pl.BlockSpec((tm, LANES), lambda i: (i, 0)),
            pl.BlockSpec((tm, LANES), lambda i: (i, 0)),
        ],
        out_specs=[pl.BlockSpec((tm, ODD_Z_W), lambda i: (i, 0)),
                   pl.BlockSpec((tm, ODD_F_W), lambda i: (i, 0))],
        out_shape=[jax.ShapeDtypeStruct((t, ODD_Z_W), BF16),
                   jax.ShapeDtypeStruct((t, ODD_F_W), F32)],
        compiler_params=_cp(("arbitrary",), 48),
        name="proj_odd",
    )(x2, mod, w_cat, ln_g.reshape(1, SGU_W), ln_b.reshape(1, SGU_W), cos, sin)


def _gla_kernel(q_ref, k_ref, la_ref, v_ref, gg_ref, g_ref, o_ref, st_ref):
    n_seq = q_ref.shape[0]
    n_pairs = GLA_HEADS // 2

    @pl.when(pl.program_id(1) == 0)
    def _():
        st_ref[...] = jnp.zeros_like(st_ref)

    c = GLA_CHUNK
    sc = GLA_GROUP * c
    rows = q_ref.shape[1]
    r_i = lax.broadcasted_iota(jnp.int32, (sc, sc), 0)
    c_i = lax.broadcasted_iota(jnp.int32, (sc, sc), 1)
    block_causal = (r_i // c == c_i // c) & (r_i >= c_i)
    tril = block_causal.astype(BF16)
    causal2 = jnp.concatenate([block_causal, block_causal], axis=0)
    lo = lax.broadcasted_iota(jnp.int32, (sc, LANES), 1) < GLA_DK
    row_chunk = lax.broadcasted_iota(jnp.int32, (sc, LANES), 0) // c
    lane_s = lax.broadcasted_iota(jnp.int32, (LANES, LANES), 1) < GLA_DK
    g_row = g_ref[...]
    states = [st_ref[n] for n in range(n_seq * n_pairs)]
    for j, s in [(j, s) for j in range(rows // sc) for s in range(n_seq)]:
        rs = slice(j * sc, (j + 1) * sc)
        la = la_ref[s, rs, :]
        la_hi = la.astype(BF16)
        la_lo = (la - la_hi.astype(F32)).astype(BF16)
        b = _dot(tril, la_hi) + _dot(tril, la_lo)
        b_last = [b[(g + 1) * c - 1:(g + 1) * c, :] for g in range(GLA_GROUP)]
        b_last_rows = jnp.concatenate(
            [jnp.broadcast_to(bl, (c, bl.shape[1])) for bl in b_last], axis=0)
        qd = q_ref[s, rs, :] * jnp.exp(b)
        kk = k_ref[s, rs, :]
        ki = kk * jnp.exp(-b)
        kd = kk * jnp.exp(b_last_rows - b)
        for p in range(GLA_HEADS // 2):
            ls = slice(p * LANES, (p + 1) * LANES)
            qd_p = qd[:, ls]
            qm = (jnp.where(lo, qd_p, 0.0).astype(BF16), jnp.where(lo, 0.0, qd_p).astype(BF16))
            s2 = _dot_nt(jnp.concatenate(qm, axis=0), ki[:, ls].astype(BF16))
            s2 = jnp.where(causal2, s2, 0.0).astype(BF16)
            kd_p = kd[:, ls].astype(BF16)
            heads = (2 * p, 2 * p + 1)
            v = [v_ref[s, rs, h * GLA_DV:(h + 1) * GLA_DV] for h in heads]
            sp = s * n_pairs + p

            def by_chunk(a):
                return jnp.concatenate(
                    [jnp.where(row_chunk == g, a, jnp.zeros_like(a)) for g in range(GLA_GROUP)],
                    axis=1)

            kd_blk = by_chunk(kd_p)
            new = [_dot_tn(v[hh], kd_blk) for hh in range(2)]
            entering = []
            for g in range(GLA_GROUP):
                gl = slice(g * LANES, (g + 1) * LANES)
                entering.append(states[sp])
                states[sp] = (jnp.exp(b_last[g][:, ls]) * states[sp]
                              + jnp.where(lane_s, new[0][:, gl], new[1][:, gl]))
            st_stack = jnp.concatenate(entering, axis=1).astype(BF16)
            for hh in range(2):
                hs = slice(heads[hh] * GLA_DV, (heads[hh] + 1) * GLA_DV)
                o = (_dot(s2[hh * sc:(hh + 1) * sc, :], v[hh])
                     + _dot_nt(by_chunk(qm[hh]), st_stack))
                o = _rms_rows(o) * g_row
                gate = _silu(gg_ref[s, rs, hs].astype(F32))
                o_ref[s, rs, hs] = (o * gate).astype(BF16)
    for n in range(n_seq * n_pairs):
        st_ref[n] = states[n]


def _gla(z, f, norm_g, bsz, seq):
    t = z.shape[0]
    tc = min(GLA_ROWS, seq)
    nc = seq // tc
    n_seq = math.gcd(GLA_SEQS_PER_STEP, bsz)
    f3 = f.reshape(bsz, seq, f.shape[1])
    z3 = z.reshape(bsz, seq, z.shape[1])

    def spec(width, col):
        return pl.BlockSpec((n_seq, tc, width), lambda b, i: (b, i, col))

    out = pl.pallas_call(
        _gla_kernel,
        grid=(bsz // n_seq, nc),
        in_specs=[spec(GLA_QK_W, 0), spec(GLA_QK_W, 1), spec(GLA_QK_W, 2),
                  spec(GLA_V_W, 0), spec(GLA_V_W, 1),
                  pl.BlockSpec((1, GLA_DV), lambda b, i: (0, 0))],
        out_specs=spec(GLA_V_W, 0),
        out_shape=jax.ShapeDtypeStruct((bsz, seq, GLA_V_W), BF16),
        scratch_shapes=[pltpu.VMEM((n_seq * (GLA_HEADS // 2), LANES, LANES), F32)],
        compiler_params=_cp(("arbitrary", "arbitrary"), 40),
        name="gla",
    )(f3, f3, f3, z3, z3, norm_g.reshape(1, GLA_DV))
    return out.reshape(t, GLA_V_W)


def _diff_attn_kernel(lam_init, q_ref, k_ref, v_ref, lq1_ref, lk1_ref, lq2_ref, lk2_ref, g_ref,
                      o_ref, vt_ref, s_ref, m_ref, l_ref, acc_ref):
    tq = q_ref.shape[0]
    seq = k_ref.shape[0]
    n_heads = q_ref.shape[1] // LANES
    qi = pl.program_id(2)

    @pl.when(qi == 0)
    def _():
        for hh in range(n_heads):
            for cb in range(seq // tq):
                blk = v_ref[cb * tq:(cb + 1) * tq, hh * LANES:(hh + 1) * LANES]
                vt_ref[hh, :, cb * tq:(cb + 1) * tq] = blk.astype(F32).T.astype(BF16)

    lo = lax.broadcasted_iota(jnp.int32, (tq, LANES), 1) < DIFF_D
    qqs = []
    for hh in range(n_heads):
        q = q_ref[:, hh * LANES:(hh + 1) * LANES]
        zero = jnp.zeros_like(q)
        qqs.append(jnp.concatenate([jnp.where(lo, q, zero), jnp.where(lo, zero, q)], axis=0))

    m_ref[...] = jnp.full(m_ref.shape, -jnp.inf, F32)
    l_ref[...] = jnp.zeros_like(l_ref)
    acc_ref[...] = jnp.zeros_like(acc_ref)

    def scores(j, slot):
        start = pl.multiple_of(j * tq, tq)
        for hh in range(n_heads):
            kj = k_ref[pl.ds(start, tq), hh * LANES:(hh + 1) * LANES]
            s_ref[slot, hh] = _dot_nt(kj, qqs[hh])

    def update(j, slot, masked):
        start = pl.multiple_of(j * tq, tq)
        for hh in range(n_heads):
            vtj = vt_ref[hh, :, pl.ds(start, tq)]
            alpha, p = [], []
            for c in range(2 * tq // LANES):
                cs = slice(c * LANES, (c + 1) * LANES)
                s_c = s_ref[slot, hh, :, cs]
                if masked:
                    kv = lax.broadcasted_iota(jnp.int32, s_c.shape, 0)
                    q_pos = lax.broadcasted_iota(jnp.int32, s_c.shape, 1) + (c * LANES) % tq
                    s_c = jnp.where(kv <= q_pos, s_c, -jnp.inf)
                m_old = m_ref[hh, :, cs]
                m_c = jnp.maximum(m_old, jnp.max(s_c, axis=0, keepdims=True))
                a_c = jnp.exp2(m_old - m_c)
                p_c = jnp.exp2(s_c - m_c)
                m_ref[hh, :, cs] = m_c
                l_ref[hh, :, cs] = a_c * l_ref[hh, :, cs] + jnp.sum(p_c, axis=0, keepdims=True)
                alpha.append(a_c)
                p.append(p_c.astype(BF16))
            acc_ref[hh] = (jnp.concatenate(alpha, axis=1) * acc_ref[hh]
                           + _dot(vtj, jnp.concatenate(p, axis=1)))

    scores(0, 0)

    def body(i, carry):
        j = 2 * i
        scores(j + 1, 1)
        update(j, 0, False)
        scores(j + 2, 0)
        update(j + 1, 1, False)
        return carry

    n_pairs = qi // 2
    lax.fori_loop(0, n_pairs, body, 0)
    j0 = 2 * n_pairs

    @pl.when(qi % 2 == 1)
    def _():
        scores(j0 + 1, 1)
        update(j0, 0, False)
        update(j0 + 1, 1, True)

    @pl.when(qi % 2 == 0)
    def _():
        update(j0, 0, True)

    lam = (jnp.exp(jnp.sum(lq1_ref[...] * lk1_ref[...], axis=-1, keepdims=True))
           - jnp.exp(jnp.sum(lq2_ref[...] * lk2_ref[...], axis=-1, keepdims=True)) + lam_init)
    for hh in range(n_heads):
        l = l_ref[hh]
        acc = acc_ref[hh]
        o12 = acc / l
        o = o12[:, :tq] - lam * o12[:, tq:]
        o = o * lax.rsqrt(jnp.mean(o * o, axis=0, keepdims=True) + EPS)
        o = o * g_ref[...] * (1.0 - lam_init)
        o_ref[:, hh * DIFF_DV:(hh + 1) * DIFF_DV] = o.T.astype(BF16)


def _diff_attn(z, lq1, lk1, lq2, lk2, norm_g, lam_init, bsz, seq):
    t = z.shape[0]
    tq = min(ATT_Q_ROWS, seq)
    nq = seq // tq
    hw = ATT_HEADS_PER_STEP * LANES
    qb, kb, vb = 1024 // hw, 1536 // hw, 2048 // hw
    small = pl.BlockSpec((1, DIFF_D), lambda b, h, i: (0, 0))
    return pl.pallas_call(
        functools.partial(_diff_attn_kernel, lam_init),
        grid=(bsz, DIFF_HEADS // ATT_HEADS_PER_STEP, nq),
        in_specs=[
            pl.BlockSpec((tq, hw), lambda b, h, i: (b * nq + i, qb + h)),
            pl.BlockSpec((seq, hw), lambda b, h, i: (b, kb + h)),
            pl.BlockSpec((seq, hw), lambda b, h, i: (b, vb + h)),
            small, small, small, small,
            pl.BlockSpec((DIFF_DV, 1), lambda b, h, i: (0, 0)),
        ],
        out_specs=pl.BlockSpec((tq, hw), lambda b, h, i: (b * nq + i, h)),
        out_shape=jax.ShapeDtypeStruct((t, DIFF_V_W), BF16),
        scratch_shapes=[pltpu.VMEM((ATT_HEADS_PER_STEP, DIFF_DV, seq), BF16),
                        pltpu.VMEM((2, ATT_HEADS_PER_STEP, tq, 2 * tq), F32),
                        pltpu.VMEM((ATT_HEADS_PER_STEP, 1, 2 * tq), F32),
                        pltpu.VMEM((ATT_HEADS_PER_STEP, 1, 2 * tq), F32),
                        pltpu.VMEM((ATT_HEADS_PER_STEP, DIFF_DV, 2 * tq), F32)],
        compiler_params=_cp(("arbitrary", "arbitrary", "arbitrary"), 32),
        name="diff_attn",
    )(z, z, z, lq1.reshape(1, DIFF_D), lk1.reshape(1, DIFF_D), lq2.reshape(1, DIFF_D),
      lk2.reshape(1, DIFF_D), norm_g.reshape(DIFF_DV, 1))


def _sgu_ret_kernel(su_ref, sv_ref, rv_ref, rg_ref, q_ref, k_ref, ws_ref, bs_ref, o_ref, st_ref):
    @pl.when(pl.program_id(1) == 0)
    def _():
        st_ref[...] = jnp.zeros_like(st_ref)

    c = RET_CHUNK
    row = lax.broadcasted_iota(jnp.int32, (c, c), 0)
    col = lax.broadcasted_iota(jnp.int32, (c, c), 1)
    causal = row >= col
    log_g = [math.log(1.0 - 2.0 ** (-5.0 - h)) for h in range(RET_HEADS)]
    lo = col < RET_DK
    rel = (row - col).astype(F32)
    pos = row.astype(F32)
    w_sgu = [jnp.where(causal, ws_ref[g], 0.0).astype(BF16) for g in range(SGU_GROUPS)]
    decays = [jnp.where(causal, jnp.exp(log_g[h] * jnp.maximum(rel, 0.0)), 0.0)
              for h in range(RET_HEADS)]
    lgs = [jnp.where(lo, log_g[2 * p], log_g[2 * p + 1]) for p in range(RET_HEADS // 2)]
    q_decs = [jnp.exp(lg * (pos + 1.0)) for lg in lgs]
    k_decs = [jnp.exp(lg * (c - 1.0 - pos)) for lg in lgs]
    states = [st_ref[p] for p in range(RET_HEADS // 2)]
    for j in range(su_ref.shape[0] // c):
        rs = slice(j * c, (j + 1) * c)
        for g in range(SGU_GROUPS):
            gs = slice(g * SGU_CH, (g + 1) * SGU_CH)
            s = _dot(w_sgu[g], sv_ref[rs, gs]) + bs_ref[g]
            o_ref[rs, gs] = (su_ref[rs, gs].astype(F32) * s).astype(BF16)
        for p in range(RET_HEADS // 2):
            ls = slice(p * LANES, (p + 1) * LANES)
            q_p = q_ref[rs, ls]
            k_p = k_ref[rs, ls]
            qm = (jnp.where(lo, q_p, 0.0).astype(BF16), jnp.where(lo, 0.0, q_p).astype(BF16))
            s2 = _dot_nt(jnp.concatenate(qm, axis=0), k_p.astype(BF16))
            qd = q_p * q_decs[p]
            qdm = (jnp.where(lo, qd, 0.0).astype(BF16), jnp.where(lo, 0.0, qd).astype(BF16))
            kd = (k_p * k_decs[p]).astype(BF16)
            st_b = states[p].astype(BF16)
            new = []
            for hh in range(2):
                h = 2 * p + hh
                hs = slice(h * RET_DV, (h + 1) * RET_DV)
                s_h = (s2[hh * c:(hh + 1) * c, :] * decays[h]).astype(BF16)
                v_h = rv_ref[rs, hs]
                o = _dot(s_h, v_h) + _dot_nt(qdm[hh], st_b)
                gate = _silu(rg_ref[rs, hs].astype(F32))
                o_ref[rs, SGU_W + h * RET_DV:SGU_W + (h + 1) * RET_DV] = (
                    _rms_rows(o) * gate).astype(BF16)
                new.append(_dot_tn(v_h, kd))
            states[p] = jnp.exp(lgs[p] * float(c)) * states[p] + jnp.where(lo, new[0], new[1])
    for p in range(RET_HEADS // 2):
        st_ref[p] = states[p]


def _sgu_ret(z, f, w_s, b_s, bsz, seq):
    t = z.shape[0]
    cc = RET_CHUNK
    c = min(SGU_RET_ROWS, seq)
    nc = seq // c
    return pl.pallas_call(
        _sgu_ret_kernel,
        grid=(bsz, nc),
        in_specs=[
            pl.BlockSpec((c, SGU_W), lambda b, i: (b * nc + i, 0)),
            pl.BlockSpec((c, SGU_W), lambda b, i: (b * nc + i, 1)),
            pl.BlockSpec((c, RET_V_W), lambda b, i: (b * nc + i, 2)),
            pl.BlockSpec((c, RET_V_W), lambda b, i: (b * nc + i, 3)),
            pl.BlockSpec((c, RET_QK_W), lambda b, i: (b * nc + i, 0)),
            pl.BlockSpec((c, RET_QK_W), lambda b, i: (b * nc + i, 1)),
            pl.BlockSpec((SGU_GROUPS, cc, cc), lambda b, i: (0, 0, 0)),
            pl.BlockSpec((SGU_GROUPS, cc, 1), lambda b, i: (0, 0, 0)),
        ],
        out_specs=pl.BlockSpec((c, SGU_W + RET_V_W), lambda b, i: (b * nc + i, 0)),
        out_shape=jax.ShapeDtypeStruct((t, SGU_W + RET_V_W), BF16),
        scratch_shapes=[pltpu.VMEM((2, LANES, LANES), F32)],
        compiler_params=_cp(("arbitrary", "arbitrary"), 32),
        name="sgu_retention",
    )(z, z, z, z, f, f, w_s, b_s.reshape(SGU_GROUPS, cc, 1))


def _out_proj_kernel(n_in, *refs):
    o_refs = refs[:n_in]
    w_ref, x_ref, mod_ref, rw_ref, rb_ref, xn_ref, h_ref, lg_ref = refs[n_in:]
    k_each = D_MODEL // n_in
    y = _dot(o_refs[0][...], w_ref[0:k_each, :])
    for n in range(1, n_in):
        y = y + _dot(o_refs[n][...], w_ref[n * k_each:(n + 1) * k_each, :])
    xn = x_ref[...] + mod_ref[0, 2:3, :] * y
    xn_ref[...] = xn
    h = _rms_rows(xn) * (1.0 + mod_ref[0, 4:5, :]) + mod_ref[0, 3:4, :]
    hb = h.astype(BF16)
    h_ref[...] = hb
    lg_ref[...] = _dot_nt(rw_ref[...], hb) + rb_ref[...]


def _out_proj(mixed, w_out, x2, mod, router_w, router_b, seq):
    t, d = x2.shape
    tm = min(PROJ_ROWS, seq)
    n_in = len(mixed)
    k_each = d // n_in
    rw = router_w.T.astype(BF16)
    rb = router_b.reshape(N_EXPERTS, 1)
    return pl.pallas_call(
        functools.partial(_out_proj_kernel, n_in),
        grid=(t // tm,),
        in_specs=[pl.BlockSpec((tm, k_each), lambda i: (i, 0)) for _ in mixed] + [
            pl.BlockSpec((d, d), lambda i: (0, 0)),
            pl.BlockSpec((tm, d), lambda i: (i, 0)),
            pl.BlockSpec((1, 6, d), lambda i: (i // (seq // tm), 0, 0)),
            pl.BlockSpec((N_EXPERTS, d), lambda i: (0, 0)),
            pl.BlockSpec((N_EXPERTS, 1), lambda i: (0, 0)),
        ],
        out_specs=[pl.BlockSpec((tm, d), lambda i: (i, 0)),
                   pl.BlockSpec((tm, d), lambda i: (i, 0)),
                   pl.BlockSpec((N_EXPERTS, tm), lambda i: (0, i))],
        out_shape=[jax.ShapeDtypeStruct((t, d), F32),
                   jax.ShapeDtypeStruct((t, d), BF16),
                   jax.ShapeDtypeStruct((N_EXPERTS, t), F32)],
        compiler_params=_cp(("arbitrary",), 48),
        name="out_proj",
    )(*mixed, w_out.astype(BF16), x2, mod, rw, rb)


def _route_kernel(lg_ref, idx_ref, gate_ref, tcnt_ref, toff_ref, trun_ref, cnt_ref, run_ref):
    @pl.when(pl.program_id(0) == 0)
    def _():
        run_ref[...] = jnp.zeros_like(run_ref)

    tm = lg_ref.shape[1]
    row = lax.broadcasted_iota(jnp.int32, (N_EXPERTS, tm), 0)
    neg = -jnp.inf
    l = lg_ref[...]
    vals, firsts, hots = [], [], []
    for _ in range(TOP_K):
        m = jnp.max(l, axis=0, keepdims=True)
        first = jnp.min(jnp.where(l == m, row, N_EXPERTS), axis=0, keepdims=True)
        hot = row == first
        vals.append(m)
        firsts.append(first)
        hots.append(hot)
        l = jnp.where(hot, neg, l)
    sel = hots[0] | hots[1] | hots[2] | hots[3]
    ex = [jnp.exp(v - vals[0]) for v in vals]
    denom = ex[0] + ex[1] + ex[2] + ex[3]
    r_i = lax.broadcasted_iota(jnp.int32, (tm, tm), 0)
    c_i = lax.broadcasted_iota(jnp.int32, (tm, tm), 1)
    before = (r_i < c_i).astype(BF16)
    earlier = _dot(sel.astype(BF16), before)
    tile_cnt = jnp.sum(sel.astype(F32), axis=1, keepdims=True)
    lower = jnp.zeros((N_EXPERTS, tm), F32)
    for k in range(TOP_K):
        lower = lower + (firsts[k] < row).astype(F32)
    tile_off = jnp.sum(lower, axis=1, keepdims=True)
    slot = tile_off + earlier
    row8 = lax.broadcasted_iota(jnp.int32, (2 * TOP_K, tm), 0)
    idx_out = jnp.zeros((2 * TOP_K, tm), jnp.int32)
    gate_out = jnp.zeros((2 * TOP_K, tm), F32)
    for k in range(TOP_K):
        pos_k = jnp.sum(jnp.where(hots[k], slot, 0.0), axis=0, keepdims=True).astype(jnp.int32)
        idx_out = jnp.where(row8 == k, firsts[k], idx_out)
        idx_out = jnp.where(row8 == TOP_K + k, pos_k, idx_out)
        gate_out = jnp.where(row8 == k, ex[k] / denom, gate_out)
    idx_ref[...] = idx_out
    gate_ref[...] = gate_out
    run = run_ref[:, 0:1]
    tcnt_ref[...] = jnp.broadcast_to(tile_cnt, tcnt_ref.shape)
    toff_ref[...] = jnp.broadcast_to(tile_off, toff_ref.shape)
    trun_ref[...] = jnp.broadcast_to(run, trun_ref.shape)
    total = run + tile_cnt
    run_ref[...] = jnp.broadcast_to(total, run_ref.shape)
    cnt_ref[...] = jnp.broadcast_to(total, cnt_ref.shape)


def _route(logits_t):
    t = logits_t.shape[1]
    tm = min(MOE_TILE, t)
    nt = t // tm
    per_tile = pl.BlockSpec((N_EXPERTS, LANES), lambda i: (i, 0))
    per_tile_shape = jax.ShapeDtypeStruct((nt * N_EXPERTS, LANES), F32)
    return pl.pallas_call(
        _route_kernel,
        grid=(nt,),
        in_specs=[pl.BlockSpec((N_EXPERTS, tm), lambda i: (0, i))],
        out_specs=[pl.BlockSpec((2 * TOP_K, tm), lambda i: (0, i)),
                   pl.BlockSpec((2 * TOP_K, tm), lambda i: (0, i)),
                   per_tile, per_tile, per_tile,
                   pl.BlockSpec((N_EXPERTS, LANES), lambda i: (0, 0))],
        out_shape=[jax.ShapeDtypeStruct((2 * TOP_K, t), jnp.int32),
                   jax.ShapeDtypeStruct((2 * TOP_K, t), F32),
                   per_tile_shape, per_tile_shape, per_tile_shape,
                   jax.ShapeDtypeStruct((N_EXPERTS, LANES), F32)],
        scratch_shapes=[pltpu.VMEM((N_EXPERTS, LANES), F32)],
        compiler_params=_cp(("arbitrary",), 32),
        name="route",
    )(logits_t)


def _rows(ref, start, n):
    return ref.at[pl.ds(pl.multiple_of(start * ROW_TILES, ROW_TILES), n * ROW_TILES), :]


def _run_copies(tile, cnt_ref, off_ref, dst_ref, make_copy):
    def per_expert(e, carry):
        j = tile * N_EXPERTS + e
        cnt, off, dst = cnt_ref[j], off_ref[j], dst_ref[j]

        for b in reversed(range(RUN_BITS)):
            size = 1 << b

            @pl.when((cnt & size) != 0)
            def _():
                done = lax.shift_left(lax.shift_right_logical(cnt, b + 1), b + 1)
                make_copy(off + done, dst + done, size).start()
        return carry

    lax.fori_loop(0, N_EXPERTS, per_expert, 0)


def _dispatch_kernel(fs_ref, fl_ref, cnt_ref, off_ref, dst_ref, h_ref, idx_ref, xs_ref,
                     stage, zero_buf, sems, zsem):
    i = pl.program_id(0)
    n_steps = pl.num_programs(0)
    cur = lax.rem(i, 2)
    tm = h_ref.shape[0]
    n_sorted = TOP_K * tm
    n_fill = fs_ref.shape[0]

    def fill_copies(f, wait):
        start, n = fs_ref[f], fl_ref[f]
        n_chunks = lax.shift_right_logical(n, FILL_SHIFT)
        tail = start + n_chunks * FILL_ROWS

        def chunk(j, carry):
            cp = pltpu.make_async_copy(zero_buf, _rows(xs_ref, start + j * FILL_ROWS, FILL_ROWS),
                                       zsem)
            cp.wait() if wait else cp.start()
            return carry

        def single(j, carry):
            cp = pltpu.make_async_copy(_rows(zero_buf, 0, 1), _rows(xs_ref, tail + j, 1), zsem)
            cp.wait() if wait else cp.start()
            return carry

        lax.fori_loop(0, n_chunks, chunk, 0)
        lax.fori_loop(0, n - n_chunks * FILL_ROWS, single, 0)

    @pl.when(i == 0)
    def _():
        zero_buf[...] = jnp.zeros_like(zero_buf)
        lax.fori_loop(0, n_fill, lambda f, c: (fill_copies(f, False), c)[1], 0)

    h = h_ref[...]
    pos = [idx_ref[TOP_K + k:TOP_K + k + 1, :] for k in range(TOP_K)]
    for sb in range(n_sorted // SORT_SLAB):
        j = lax.broadcasted_iota(jnp.int32, (SORT_SLAB, tm), 0) + sb * SORT_SLAB
        hit = (pos[0] == j) | (pos[1] == j) | (pos[2] == j) | (pos[3] == j)
        perm = jnp.where(hit, 1.0, 0.0).astype(BF16)
        rows = _dot(perm, h)
        for c in range(ROW_TILES):
            stage[cur, pl.ds(sb * SORT_SLAB * ROW_TILES + c, SORT_SLAB, stride=ROW_TILES), :] = (
                rows[:, c * LANES:(c + 1) * LANES])

    def make_copy(src_row, dst_row, size):
        return pltpu.make_async_copy(_rows(stage.at[cur], src_row, size),
                                     _rows(xs_ref, dst_row, size), sems.at[cur])

    _run_copies(i, cnt_ref, off_ref, dst_ref, make_copy)

    def wait_slot(slot):
        pltpu.make_async_copy(stage.at[slot], _rows(xs_ref, 0, n_sorted), sems.at[slot]).wait()

    @pl.when(i > 0)
    def _():
        wait_slot(1 - cur)

    @pl.when(i == n_steps - 1)
    def _():
        wait_slot(cur)

    @pl.when(i == 0)
    def _():
        lax.fori_loop(0, n_fill, lambda f, c: (fill_copies(f, True), c)[1], 0)


def _dispatch(h2, idx, tile_cnt, tile_off, tile_dst, fill_start, fill_len, n_rows):
    t, d = h2.shape
    tm = min(MOE_TILE, t)
    grid_spec = pltpu.PrefetchScalarGridSpec(
        num_scalar_prefetch=5,
        grid=(t // tm,),
        in_specs=[pl.BlockSpec((tm, d), lambda i, *_: (i, 0)),
                  pl.BlockSpec((2 * TOP_K, tm), lambda i, *_: (0, i))],
        out_specs=pl.BlockSpec(memory_space=pl.ANY),
        scratch_shapes=[pltpu.VMEM((2, TOP_K * tm * ROW_TILES, LANES), F32),
                        pltpu.VMEM((FILL_ROWS * ROW_TILES, LANES), F32),
                        pltpu.SemaphoreType.DMA((2,)), pltpu.SemaphoreType.DMA(())],
    )
    return pl.pallas_call(
        _dispatch_kernel,
        grid_spec=grid_spec,
        out_shape=jax.ShapeDtypeStruct((n_rows * ROW_TILES, LANES), F32),
        compiler_params=_cp(("arbitrary",), 48),
        name="dispatch",
    )(fill_start, fill_len, tile_cnt, tile_off, tile_dst, h2, idx)


def _expert_kernel(layer, be_ref, nu_ref, nx_ref, nv_ref, xs_ref, wi_hbm, bi_ref, wo_hbm, bo_ref,
                   y_ref, wi_st, wo_st, wi_b, wo_b, sems):
    i = pl.program_id(0)
    tb = xs_ref.shape[0] // ROW_TILES
    e = be_ref[i]
    fresh = jnp.logical_or(i == 0, e != be_ref[jnp.maximum(i - 1, 0)])
    used = i < nu_ref[0]

    def fetch(ex):
        return (pltpu.make_async_copy(wi_hbm.at[layer, ex], wi_st, sems.at[0]),
                pltpu.make_async_copy(wo_hbm.at[layer, ex], wo_st, sems.at[1]))

    @pl.when(i == 0)
    def _():
        for cp in fetch(e):
            cp.start()

    def take_weights():
        for cp in fetch(e):
            cp.wait()
        wi_b[...] = wi_st[...].astype(BF16)
        wo_b[...] = wo_st[...].astype(BF16)

    def fetch_next():
        @pl.when(nx_ref[i] >= 0)
        def _():
            for cp in fetch(nx_ref[i]):
                cp.start()

    def compute(n):
        x = jnp.concatenate(
            [xs_ref[pl.ds(c, n, stride=ROW_TILES), :] for c in range(ROW_TILES)],
            axis=1).astype(BF16)
        y = jnp.zeros((n, D_MODEL), F32) + bo_ref[0, 0]
        half = 1024
        for j in range(D_FF // half):
            a, b = j * half, (j + 1) * half
            glu = _dot(x, wi_b[:, a:b]) + bi_ref[0, 0, :, a:b]
            lin = _dot(x, wi_b[:, D_FF + a:D_FF + b]) + bi_ref[0, 0, :, D_FF + a:D_FF + b]
            glu = jnp.minimum(glu, SWIGLU_LIMIT)
            lin = jnp.clip(lin, -SWIGLU_LIMIT, SWIGLU_LIMIT)
            act = glu * jax.nn.sigmoid(SWIGLU_ALPHA * glu) * (lin + 1.0)
            y = y + _dot(act.astype(BF16), wo_b[a:b, :])
        for c in range(ROW_TILES):
            y_ref[pl.ds(c, n, stride=ROW_TILES), :] = y[:, c * LANES:(c + 1) * LANES]

    full = nv_ref[i] > tb // 2

    @pl.when(jnp.logical_and(used, jnp.logical_and(full, fresh)))
    def _():
        take_weights()
        compute(tb)
        fetch_next()

    @pl.when(jnp.logical_and(used, jnp.logical_and(full, jnp.logical_not(fresh))))
    def _():
        compute(tb)

    @pl.when(jnp.logical_and(used, jnp.logical_not(full)))
    def _():
        @pl.when(fresh)
        def _():
            take_weights()
            fetch_next()

        compute(tb // 2)
        y_ref[pl.ds(tb // 2 * ROW_TILES, tb // 2 * ROW_TILES), :] = jnp.zeros(
            (tb // 2 * ROW_TILES, LANES), F32)

    @pl.when(jnp.logical_not(used))
    def _():
        y_ref[...] = jnp.zeros_like(y_ref)


def _experts(xs, block_e, n_used, next_e, n_valid, layer, w_in, b_in, w_out, b_out):
    tb = EXPERT_ROWS
    n_rows = xs.shape[0] // ROW_TILES
    nb = n_rows // tb
    depth, ne, d, f2 = w_in.shape

    def row_map(i, be, nu, nx, nv):
        return (jnp.minimum(i, nu[0] - 1), 0)

    grid_spec = pltpu.PrefetchScalarGridSpec(
        num_scalar_prefetch=4,
        grid=(nb,),
        in_specs=[
            pl.BlockSpec((tb * ROW_TILES, LANES), row_map),
            pl.BlockSpec(memory_space=pl.ANY),
            pl.BlockSpec((1, 1, 1, f2), lambda i, be, nu, nx, nv: (layer, be[i], 0, 0)),
            pl.BlockSpec(memory_space=pl.ANY),
            pl.BlockSpec((1, 1, 1, d), lambda i, be, nu, nx, nv: (layer, be[i], 0, 0)),
        ],
        out_specs=pl.BlockSpec((tb * ROW_TILES, LANES), lambda i, be, nu, nx, nv: (i, 0)),
        scratch_shapes=[pltpu.VMEM((d, f2), F32), pltpu.VMEM((D_FF, d), F32),
                        pltpu.VMEM((d, f2), BF16), pltpu.VMEM((D_FF, d), BF16),
                        pltpu.SemaphoreType.DMA((2,))],
    )
    return pl.pallas_call(
        functools.partial(_expert_kernel, layer),
        grid_spec=grid_spec,
        out_shape=jax.ShapeDtypeStruct((n_rows * ROW_TILES, LANES), F32),
        compiler_params=_cp(("arbitrary",), 56),
        name="experts",
    )(block_e, n_used, next_e, n_valid, xs, w_in, b_in.reshape(depth, ne, 1, f2), w_out,
      b_out.reshape(depth, ne, 1, d))


def _combine_kernel(final, cnt_ref, off_ref, dst_ref, pos_ref, gate_ref, x_ref, mod_ref, fg_ref,
                    yb_ref, o_ref, stage, sems):
    tm = x_ref.shape[0]
    n_sorted = TOP_K * tm
    i = pl.program_id(0)
    cur = lax.rem(i, 2)

    def fetch(tile, slot):
        def make_copy(sorted_row, src_row, size):
            return pltpu.make_async_copy(_rows(yb_ref, src_row, size),
                                         _rows(stage.at[slot], sorted_row, size), sems.at[slot])

        _run_copies(tile, cnt_ref, off_ref, dst_ref, make_copy)

    @pl.when(i == 0)
    def _():
        fetch(0, 0)

    @pl.when(i + 1 < pl.num_programs(0))
    def _():
        fetch(i + 1, 1 - cur)

    pltpu.make_async_copy(_rows(yb_ref, 0, n_sorted), stage.at[cur], sems.at[cur]).wait()

    pos = [pos_ref[:, TOP_K + k:TOP_K + k + 1] for k in range(TOP_K)]
    gate = [gate_ref[:, k:k + 1] for k in range(TOP_K)]
    y = jnp.zeros((tm, D_MODEL), F32)
    for sb in range(n_sorted // SORT_SLAB):
        rows = jnp.concatenate(
            [stage[cur, pl.ds(sb * SORT_SLAB * ROW_TILES + c, SORT_SLAB, stride=ROW_TILES), :]
             for c in range(ROW_TILES)], axis=1).astype(BF16)
        j = lax.broadcasted_iota(jnp.int32, (tm, SORT_SLAB), 1) + sb * SORT_SLAB
        w = jnp.zeros((tm, SORT_SLAB), F32)
        for k in range(TOP_K):
            w = w + jnp.where(pos[k] == j, gate[k], 0.0)
        y = y + _dot(w.astype(BF16), rows)
    o = x_ref[...] + mod_ref[0, 5:6, :] * y
    if final:
        o = _rms_rows(o) * fg_ref[...]
    o_ref[...] = o


def _combine(yb, tile_cnt, tile_off, tile_dst, pos_t, gates_t, x2, mod, final_g, final, seq):
    t, d = x2.shape
    tm = min(MOE_TILE, seq)
    grid_spec = pltpu.PrefetchScalarGridSpec(
        num_scalar_prefetch=3,
        grid=(t // tm,),
        in_specs=[
            pl.BlockSpec((tm, 2 * TOP_K), lambda i, *_: (i, 0)),
            pl.BlockSpec((tm, 2 * TOP_K), lambda i, *_: (i, 0)),
            pl.BlockSpec((tm, d), lambda i, *_: (i, 0)),
            pl.BlockSpec((1, 6, d), lambda i, *_: (i // (seq // tm), 0, 0)),
            pl.BlockSpec((1, d), lambda i, *_: (0, 0)),
            pl.BlockSpec(memory_space=pl.ANY),
        ],
        out_specs=pl.BlockSpec((tm, d), lambda i, *_: (i, 0)),
        scratch_shapes=[pltpu.VMEM((2, TOP_K * tm * ROW_TILES, LANES), F32),
                        pltpu.SemaphoreType.DMA((2,))],
    )
    return pl.pallas_call(
        functools.partial(_combine_kernel, final),
        grid_spec=grid_spec,
        out_shape=jax.ShapeDtypeStruct((t, d), F32),
        compiler_params=_cp(("arbitrary",), 48),
        name="combine",
    )(tile_cnt, tile_off, tile_dst, pos_t, gates_t, x2, mod, final_g.reshape(1, d), yb)


def _moe(h2, logits, x2, mod, layer, w_in, b_in, w_out, b_out, final_g, final, seq):
    t = x2.shape[0]
    tb = EXPERT_ROWS
    idx, gates, tcnt, toff, trun, cnt = _route(logits)
    counts = cnt[:, 0].astype(jnp.int32)
    nblk = (counts + tb - 1) // tb
    blk_end = jnp.cumsum(nblk)
    pad_start = (blk_end - nblk) * tb
    n_blocks = (t * TOP_K) // tb + N_EXPERTS
    n_used = blk_end[-1:]
    experts = jnp.arange(N_EXPERTS, dtype=jnp.int32)
    last_e = jnp.max(jnp.where(nblk > 0, experts, 0))
    blk = jnp.arange(n_blocks, dtype=jnp.int32)
    block_e = jnp.minimum(
        jnp.sum((blk_end[None, :] <= blk[:, None]).astype(jnp.int32), axis=1), last_e)
    later = (experts[None, :] > block_e[:, None]) & (nblk[None, :] > 0)
    next_e = jnp.min(jnp.where(later, experts[None, :], N_EXPERTS), axis=1)
    next_e = jnp.where(next_e == N_EXPERTS, -1, next_e).astype(jnp.int32)
    tile_cnt = tcnt[:, 0].astype(jnp.int32)
    tile_off = toff[:, 0].astype(jnp.int32)
    tile_dst = (trun[:, 0].astype(jnp.int32).reshape(-1, N_EXPERTS) + pad_start[None, :]).reshape(-1)
    fill_start = jnp.concatenate([pad_start + counts, blk_end[-1:] * tb]).astype(jnp.int32)
    fill_len = jnp.concatenate([nblk * tb - counts, (n_blocks - blk_end[-1:]) * tb]).astype(jnp.int32)
    xs = _dispatch(h2, idx, tile_cnt, tile_off, tile_dst, fill_start, fill_len, n_blocks * tb)
    row_end = jnp.sum(jnp.where(block_e[:, None] == experts[None, :],
                                (pad_start + counts)[None, :], 0), axis=1)
    n_valid = jnp.clip(row_end - blk * tb, 0, tb).astype(jnp.int32)
    yb = _experts(xs, block_e, n_used.astype(jnp.int32), next_e, n_valid, layer,
                  w_in, b_in, w_out, b_out)
    return _combine(yb, tile_cnt, tile_off, tile_dst, idx.T, gates.T, x2, mod, final_g, final, seq)


def kernel(x, c, positions, w_ada, b_ada, even_w_in, gla_w_gate, gla_b_gate, gla_norm_g,
           diff_lam_q1, diff_lam_k1, diff_lam_q2, diff_lam_k2, diff_norm_g, even_w_out,
           odd_w_in, sgu_ln_g, sgu_ln_b, sgu_w, sgu_b, odd_w_out,
           router_w, router_b, expert_w_in, expert_b_in, expert_w_out, expert_b_out,
           final_norm_g):
    bsz, seq, d = x.shape
    depth = w_ada.shape[0]
    t = bsz * seq
    mods = _modulation(c, w_ada, b_ada).reshape(depth, bsz, 6, d)
    cos, sin = _rope_tables(positions)
    x2 = x.reshape(t, d)
    for layer in range(depth):
        mod = mods[layer]
        j = layer // 2
        if layer % 2 == 0:
            z, f = _proj_even(x2, mod, even_w_in[j], gla_w_gate[j], gla_b_gate[j], cos, sin, seq)
            o_gla = _gla(z, f, gla_norm_g[j], bsz, seq)
            lam_init = 0.8 - 0.6 * math.exp(-0.3 * layer)
            o_diff = _diff_attn(z, diff_lam_q1[j], diff_lam_k1[j], diff_lam_q2[j], diff_lam_k2[j],
                                diff_norm_g[j], lam_init, bsz, seq)
            mixed, w_out = (o_gla, o_diff), even_w_out[j]
        else:
            z, f = _proj_odd(x2, mod, odd_w_in[j], sgu_ln_g[j], sgu_ln_b[j], cos, sin, seq)
            mixed, w_out = (_sgu_ret(z, f, sgu_w[j], sgu_b[j], bsz, seq),), odd_w_out[j]
        x2, h2, logits = _out_proj(mixed, w_out, x2, mod, router_w[layer], router_b[layer], seq)
        x2 = _moe(h2, logits, x2, mod, layer, expert_w_in, expert_b_in, expert_w_out,
                  expert_b_out, final_norm_g, layer == depth - 1, seq)
    return x2.reshape(bsz, seq, d)
```

```python
import functools
import math

import jax
import jax.numpy as jnp
from jax import lax
from jax.experimental import pallas as pl
from jax.experimental.pallas import tpu as pltpu

F32 = jnp.float32
BF16 = jnp.bfloat16

D_MODEL = 1024
EPS = 1e-6
ROPE_THETA = 10000.0
ROPE_HALF = 32

GLA_HEADS = 4
GLA_DK = 64
GLA_DV = 128
GLA_RANK = 16
GLA_CHUNK = 64
GLA_GATE_NORMALIZER = 16.0
GLA_QK_W = GLA_HEADS * GLA_DK
GLA_V_W = GLA_HEADS * GLA_DV

DIFF_HEADS = 4
DIFF_D = 64
DIFF_DV = 128
DIFF_QK_W = DIFF_HEADS * 2 * DIFF_D
DIFF_V_W = DIFF_HEADS * DIFF_DV

SGU_GROUPS = 4
SGU_CH = 128
SGU_CHUNK = 128
SGU_W = SGU_GROUPS * SGU_CH

RET_HEADS = 4
RET_DK = 64
RET_DV = 128
RET_CHUNK = 128
RET_QK_W = RET_HEADS * RET_DK
RET_V_W = RET_HEADS * RET_DV

N_EXPERTS = 32
TOP_K = 4
D_FF = D_MODEL
SWIGLU_ALPHA = 1.702
SWIGLU_LIMIT = 7.0

LANES = 128
SUBLANES = 8
ROW_TILES = D_MODEL // LANES

PROJ_ROWS = 1024
GLA_ROWS = 512
GLA_SEQS_PER_STEP = 2
GLA_GROUP = 4
SGU_RET_ROWS = 512
ATT_Q_ROWS = 256
ATT_HEADS_PER_STEP = 4
MOE_TILE = 512
SORT_SLAB = 512
RUN_BITS = 10
FILL_SHIFT = 6
FILL_ROWS = 1 << FILL_SHIFT
EXPERT_ROWS = 512
MOD_COLS = 1536

MIB = 1024 * 1024


def _cp(semantics, vmem_mib):
    return pltpu.CompilerParams(dimension_semantics=semantics, vmem_limit_bytes=vmem_mib * MIB)


def _dot(a, b):
    return jnp.dot(a, b, preferred_element_type=F32)


def _dot_nt(a, b):
    return lax.dot_general(a, b, (((1,), (1,)), ((), ())), preferred_element_type=F32)


def _dot_tn(a, b):
    return lax.dot_general(a, b, (((0,), (0,)), ((), ())), preferred_element_type=F32)


def _rms_rows(x):
    return x * lax.rsqrt(jnp.mean(x * x, axis=-1, keepdims=True) + EPS)


def _silu(x):
    return x * jax.nn.sigmoid(x)


def _rope_chunk(v, cos, sin, lo_half):
    rot = jnp.where(lo_half, -pltpu.roll(v, 96, 1), pltpu.roll(v, 32, 1))
    return v * cos + rot * sin


def _mod_kernel(c_ref, w_ref, b_ref, o_ref):
    c = c_ref[...]
    ca = _silu(c).astype(BF16)
    o_ref[0] = _dot(ca, w_ref[0].astype(BF16)) + b_ref[0]


def _modulation(c, w_ada, b_ada):
    depth, d, n = w_ada.shape
    bsz = c.shape[0]
    return pl.pallas_call(
        _mod_kernel,
        grid=(depth, n // MOD_COLS),
        in_specs=[
            pl.BlockSpec((bsz, d), lambda l, j: (0, 0)),
            pl.BlockSpec((1, d, MOD_COLS), lambda l, j: (l, 0, j)),
            pl.BlockSpec((1, 1, MOD_COLS), lambda l, j: (l, 0, j)),
        ],
        out_specs=pl.BlockSpec((1, bsz, MOD_COLS), lambda l, j: (l, 0, j)),
        out_shape=jax.ShapeDtypeStruct((depth, bsz, n), F32),
        compiler_params=_cp(("arbitrary", "arbitrary"), 40),
        name="adaln_mod",
    )(c, w_ada, b_ada.reshape(depth, 1, n))


def _rope_table_kernel(p_ref, f_ref, c_ref, s_ref):
    ang = p_ref[...].astype(F32) * f_ref[...]
    c_ref[...] = jnp.cos(ang)
    s_ref[...] = jnp.sin(ang)


def _rope_tables(positions):
    t = positions.size
    per_row = LANES // ROPE_HALF
    rows = t // per_row
    pos_d = jnp.repeat(positions.reshape(rows, per_row), ROPE_HALF, axis=1)
    inv_freq = ROPE_THETA ** (-jnp.arange(ROPE_HALF, dtype=F32) / ROPE_HALF)
    freq_d = jnp.tile(inv_freq, per_row).reshape(1, LANES)
    tr = min(512, rows)
    cos_d, sin_d = pl.pallas_call(
        _rope_table_kernel,
        grid=(rows // tr,),
        in_specs=[pl.BlockSpec((tr, LANES), lambda i: (i, 0)),
                  pl.BlockSpec((1, LANES), lambda i: (0, 0))],
        out_specs=[pl.BlockSpec((tr, LANES), lambda i: (i, 0))] * 2,
        out_shape=[jax.ShapeDtypeStruct((rows, LANES), F32)] * 2,
        compiler_params=_cp(("arbitrary",), 32),
        name="rope_tables",
    )(pos_d, freq_d)
    cos = jnp.tile(cos_d.reshape(t, ROPE_HALF), (1, per_row))
    sin = jnp.tile(sin_d.reshape(t, ROPE_HALF), (1, per_row))
    return cos, sin


def _modulated_rms(x_ref, mod_ref, which):
    x = x_ref[...]
    shift = mod_ref[0, 3 * which:3 * which + 1, :]
    scale = mod_ref[0, 3 * which + 1:3 * which + 2, :]
    return _rms_rows(x) * (1.0 + scale) + shift


EVEN_Z = (GLA_V_W, GLA_V_W, DIFF_QK_W, DIFF_QK_W, DIFF_V_W)
EVEN_Z_W = sum(EVEN_Z)
EVEN_F_W = 3 * GLA_QK_W
EVEN_W_COLS = EVEN_Z_W + 2 * GLA_QK_W + LANES


def _proj_even_kernel(x_ref, mod_ref, w_ref, wg_ref, bg_ref, cos_ref, sin_ref, z_ref, f_ref):
    hb = _modulated_rms(x_ref, mod_ref, 0).astype(BF16)

    def sec(a, b):
        return _dot(hb, w_ref[:, a:b])

    for a, b in ((0, 512), (512, 1024), (2048, 2560)):
        z_ref[:, a:b] = sec(a, b).astype(BF16)
    cos = cos_ref[...]
    sin = sin_ref[...]
    lo_half = (lax.broadcasted_iota(jnp.int32, cos.shape, 1) % 64) < ROPE_HALF
    q_scale = DIFF_D ** -0.5 * math.log2(math.e)
    for a, scl in ((1024, q_scale), (1536, 1.0)):
        full = sec(a, a + DIFF_QK_W)
        for c in range(DIFF_QK_W // LANES):
            v = full[:, c * LANES:(c + 1) * LANES]
            z_ref[:, a + c * LANES:a + (c + 1) * LANES] = (
                _rope_chunk(v, cos, sin, lo_half) * scl).astype(BF16)
    f_ref[:, 0:256] = sec(2560, 2816) * (GLA_DK ** -0.5)
    f_ref[:, 256:512] = sec(2816, 3072)
    gr = sec(3072, 3200).astype(BF16)
    pre = _dot(gr, wg_ref[...]) + bg_ref[...]
    log_sig = jnp.minimum(pre, 0.0) - jnp.log1p(jnp.exp(-jnp.abs(pre)))
    f_ref[:, 512:768] = log_sig / GLA_GATE_NORMALIZER


def _proj_even(x2, mod, w_in, w_gate, b_gate, cos, sin, seq):
    t, d = x2.shape
    tm = min(PROJ_ROWS, seq)
    gq, gk, gv, gr, gg, dq, dk, dv = jnp.split(
        w_in, [256, 512, 1024, 1040, 1552, 2064, 2576], axis=1)
    gr_pad = jnp.pad(gr, ((0, 0), (0, LANES - GLA_RANK)))
    w_cat = jnp.concatenate([gv, gg, dq, dk, dv, gq, gk, gr_pad], axis=1).astype(BF16)
    wg_pad = jnp.pad(w_gate, ((0, LANES - GLA_RANK), (0, 0))).astype(BF16)
    return pl.pallas_call(
        _proj_even_kernel,
        grid=(t // tm,),
        in_specs=[
            pl.BlockSpec((tm, d), lambda i: (i, 0)),
            pl.BlockSpec((1, 6, d), lambda i: (i // (seq // tm), 0, 0)),
            pl.BlockSpec((d, EVEN_W_COLS), lambda i: (0, 0)),
            pl.BlockSpec((LANES, GLA_QK_W), lambda i: (0, 0)),
            pl.BlockSpec((1, GLA_QK_W), lambda i: (0, 0)),
            pl.BlockSpec((tm, LANES), lambda i: (i, 0)),
            pl.BlockSpec((tm, LANES), lambda i: (i, 0)),
        ],
        out_specs=[pl.BlockSpec((tm, EVEN_Z_W), lambda i: (i, 0)),
                   pl.BlockSpec((tm, EVEN_F_W), lambda i: (i, 0))],
        out_shape=[jax.ShapeDtypeStruct((t, EVEN_Z_W), BF16),
                   jax.ShapeDtypeStruct((t, EVEN_F_W), F32)],
        compiler_params=_cp(("arbitrary",), 48),
        name="proj_even",
    )(x2, mod, w_cat, wg_pad, b_gate.reshape(1, GLA_QK_W), cos, sin)


ODD_Z_W = 2 * SGU_W + 2 * RET_V_W
ODD_F_W = 2 * RET_QK_W
ODD_W_COLS = ODD_Z_W + ODD_F_W


def _gelu_exact(x):
    return 0.5 * x * (1.0 + lax.erf(x * (2.0 ** -0.5)))


def _proj_odd_kernel(x_ref, mod_ref, w_ref, lng_ref, lnb_ref, cos_ref, sin_ref, z_ref, f_ref):
    hb = _modulated_rms(x_ref, mod_ref, 0).astype(BF16)

    def sec(a, b):
        return _dot(hb, w_ref[:, a:b])

    z_ref[:, 0:512] = _gelu_exact(sec(0, 512)).astype(BF16)
    sv = _gelu_exact(sec(512, 1024))
    mu = jnp.mean(sv, axis=-1, keepdims=True)
    cen = sv - mu
    var = jnp.mean(cen * cen, axis=-1, keepdims=True)
    z_ref[:, 512:1024] = (cen * lax.rsqrt(var + EPS) * lng_ref[...] + lnb_ref[...]).astype(BF16)
    for a, b in ((1024, 1536), (1536, 2048)):
        z_ref[:, a:b] = sec(a, b).astype(BF16)
    cos = cos_ref[...]
    sin = sin_ref[...]
    lo_half = (lax.broadcasted_iota(jnp.int32, cos.shape, 1) % 64) < ROPE_HALF
    k_scale = RET_DK ** -0.5
    for a, scl in ((0, 1.0), (256, k_scale)):
        full = sec(ODD_Z_W + a, ODD_Z_W + a + RET_QK_W)
        for c in range(RET_QK_W // LANES):
            v = full[:, c * LANES:(c + 1) * LANES]
            f_ref[:, a + c * LANES:a + (c + 1) * LANES] = _rope_chunk(v, cos, sin, lo_half) * scl


def _proj_odd(x2, mod, w_in, ln_g, ln_b, cos, sin, seq):
    t, d = x2.shape
    tm = min(PROJ_ROWS, seq)
    su, sv, rq, rk, rv, rg = jnp.split(w_in, [512, 1024, 1280, 1536, 2048], axis=1)
    w_cat = jnp.concatenate([su, sv, rv, rg, rq, rk], axis=1).astype(BF16)
    return pl.pallas_call(
        _proj_odd_kernel,
        grid=(t // tm,),
        in_specs=[
            pl.BlockSpec((tm, d), lambda i: (i, 0)),
            pl.BlockSpec((1, 6, d), lambda i: (i // (seq // tm), 0, 0)),
            pl.BlockSpec((d, ODD_W_COLS), lambda i: (0, 0)),
            pl.BlockSpec((1, SGU_W), lambda i: (0, 0)),
            pl.BlockSpec((1, SGU_W), lambda i: (0, 0)),
            pl.BlockSpec((tm, LANES), lambda i: (i, 0)),
            pl.BlockSpec((tm, LANES), lambda i: (i, 0)),
        ],
        out_specs=[pl.BlockSpec((tm, ODD_Z_W), lambda i: (i, 0)),
                   pl.BlockSpec((tm, ODD_F_W), lambda i: (i, 0))],
        out_shape=[jax.ShapeDtypeStruct((t, ODD_Z_W), BF16),
                   jax.ShapeDtypeStruct((t, ODD_F_W), F32)],
        compiler_params=_cp(("arbitrary",), 48),
        name="proj_odd",
    )(x2, mod, w_cat, ln_g.reshape(1, SGU_W), ln_b.reshape(1, SGU_W), cos, sin)


def _gla_kernel(q_ref, k_ref, la_ref, v_ref, gg_ref, g_ref, o_ref, st_ref):
    n_seq = q_ref.shape[0]
    n_pairs = GLA_HEADS // 2

    @pl.when(pl.program_id(1) == 0)
    def _():
        st_ref[...] = jnp.zeros_like(st_ref)

    c = GLA_CHUNK
    sc = GLA_GROUP * c
    rows = q_ref.shape[1]
    r_i = lax.broadcasted_iota(jnp.int32, (sc, sc), 0)
    c_i = lax.broadcasted_iota(jnp.int32, (sc, sc), 1)
    block_causal = (r_i // c == c_i // c) & (r_i >= c_i)
    tril = block_causal.astype(BF16)
    causal2 = jnp.concatenate([block_causal, block_causal], axis=0)
    lo = lax.broadcasted_iota(jnp.int32, (sc, LANES), 1) < GLA_DK
    row_chunk = lax.broadcasted_iota(jnp.int32, (sc, LANES), 0) // c
    lane_s = lax.broadcasted_iota(jnp.int32, (LANES, LANES), 1) < GLA_DK
    g_row = g_ref[...]
    states = [st_ref[n] for n in range(n_seq * n_pairs)]
    for j, s in [(j, s) for j in range(rows // sc) for s in range(n_seq)]:
        rs = slice(j * sc, (j + 1) * sc)
        la = la_ref[s, rs, :]
        la_hi = la.astype(BF16)
        la_lo = (la - la_hi.astype(F32)).astype(BF16)
        b = _dot(tril, la_hi) + _dot(tril, la_lo)
        b_last = [b[(g + 1) * c - 1:(g + 1) * c, :] for g in range(GLA_GROUP)]
        b_last_rows = jnp.concatenate(
            [jnp.broadcast_to(bl, (c, bl.shape[1])) for bl in b_last], axis=0)
        qd = q_ref[s, rs, :] * jnp.exp(b)
        kk = k_ref[s, rs, :]
        ki = kk * jnp.exp(-b)
        kd = kk * jnp.exp(b_last_rows - b)
        for p in range(GLA_HEADS // 2):
            ls = slice(p * LANES, (p + 1) * LANES)
            qd_p = qd[:, ls]
            qm = (jnp.where(lo, qd_p, 0.0).astype(BF16), jnp.where(lo, 0.0, qd_p).astype(BF16))
            s2 = _dot_nt(jnp.concatenate(qm, axis=0), ki[:, ls].astype(BF16))
            s2 = jnp.where(causal2, s2, 0.0).astype(BF16)
            kd_p = kd[:, ls].astype(BF16)
            heads = (2 * p, 2 * p + 1)
            v = [v_ref[s, rs, h * GLA_DV:(h + 1) * GLA_DV] for h in heads]
            sp = s * n_pairs + p

            def by_chunk(a):
                return jnp.concatenate(
                    [jnp.where(row_chunk == g, a, jnp.zeros_like(a)) for g in range(GLA_GROUP)],
                    axis=1)

            kd_blk = by_chunk(kd_p)
            new = [_dot_tn(v[hh], kd_blk) for hh in range(2)]
            entering = []
            for g in range(GLA_GROUP):
                gl = slice(g * LANES, (g + 1) * LANES)
                entering.append(states[sp])
                states[sp] = (jnp.exp(b_last[g][:, ls]) * states[sp]
                              + jnp.where(lane_s, new[0][:, gl], new[1][:, gl]))
            st_stack = jnp.concatenate(entering, axis=1).astype(BF16)
            for hh in range(2):
                hs = slice(heads[hh] * GLA_DV, (heads[hh] + 1) * GLA_DV)
                o = (_dot(s2[hh * sc:(hh + 1) * sc, :], v[hh])
                     + _dot_nt(by_chunk(qm[hh]), st_stack))
                o = _rms_rows(o) * g_row
                gate = _silu(gg_ref[s, rs, hs].astype(F32))
                o_ref[s, rs, hs] = (o * gate).astype(BF16)
    for n in range(n_seq * n_pairs):
        st_ref[n] = states[n]


def _gla(z, f, norm_g, bsz, seq):
    t = z.shape[0]
    tc = min(GLA_ROWS, seq)
    nc = seq // tc
    n_seq = math.gcd(GLA_SEQS_PER_STEP, bsz)
    f3 = f.reshape(bsz, seq, f.shape[1])
    z3 = z.reshape(bsz, seq, z.shape[1])

    def spec(width, col):
        return pl.BlockSpec((n_seq, tc, width), lambda b, i: (b, i, col))

    out = pl.pallas_call(
        _gla_kernel,
        grid=(bsz // n_seq, nc),
        in_specs=[spec(GLA_QK_W, 0), spec(GLA_QK_W, 1), spec(GLA_QK_W, 2),
                  spec(GLA_V_W, 0), spec(GLA_V_W, 1),
                  pl.BlockSpec((1, GLA_DV), lambda b, i: (0, 0))],
        out_specs=spec(GLA_V_W, 0),
        out_shape=jax.ShapeDtypeStruct((bsz, seq, GLA_V_W), BF16),
        scratch_shapes=[pltpu.VMEM((n_seq * (GLA_HEADS // 2), LANES, LANES), F32)],
        compiler_params=_cp(("arbitrary", "arbitrary"), 40),
        name="gla",
    )(f3, f3, f3, z3, z3, norm_g.reshape(1, GLA_DV))
    return out.reshape(t, GLA_V_W)


def _diff_attn_kernel(lam_init, q_ref, k_ref, v_ref, lq1_ref, lk1_ref, lq2_ref, lk2_ref, g_ref,
                      o_ref, vt_ref, s_ref, m_ref, l_ref, acc_ref):
    tq = q_ref.shape[0]
    seq = k_ref.shape[0]
    n_heads = q_ref.shape[1] // LANES
    qi = pl.program_id(2)

    @pl.when(qi == 0)
    def _():
        for hh in range(n_heads):
            for cb in range(seq // tq):
                blk = v_ref[cb * tq:(cb + 1) * tq, hh * LANES:(hh + 1) * LANES]
                vt_ref[hh, :, cb * tq:(cb + 1) * tq] = blk.astype(F32).T.astype(BF16)

    lo = lax.broadcasted_iota(jnp.int32, (tq, LANES), 1) < DIFF_D
    qqs = []
    for hh in range(n_heads):
        q = q_ref[:, hh * LANES:(hh + 1) * LANES]
        zero = jnp.zeros_like(q)
        qqs.append(jnp.concatenate([jnp.where(lo, q, zero), jnp.where(lo, zero, q)], axis=0))

    m_ref[...] = jnp.full(m_ref.shape, -jnp.inf, F32)
    l_ref[...] = jnp.zeros_like(l_ref)
    acc_ref[...] = jnp.zeros_like(acc_ref)

    def scores(j, slot):
        start = pl.multiple_of(j * tq, tq)
        for hh in range(n_heads):
            kj = k_ref[pl.ds(start, tq), hh * LANES:(hh + 1) * LANES]
            s_ref[slot, hh] = _dot_nt(kj, qqs[hh])

    def update(j, slot, masked):
        start = pl.multiple_of(j * tq, tq)
        for hh in range(n_heads):
            vtj = vt_ref[hh, :, pl.ds(start, tq)]
            alpha, p = [], []
            for c in range(2 * tq // LANES):
                cs = slice(c * LANES, (c + 1) * LANES)
                s_c = s_ref[slot, hh, :, cs]
                if masked:
                    kv = lax.broadcasted_iota(jnp.int32, s_c.shape, 0)
                    q_pos = lax.broadcasted_iota(jnp.int32, s_c.shape, 1) + (c * LANES) % tq
                    s_c = jnp.where(kv <= q_pos, s_c, -jnp.inf)
                m_old = m_ref[hh, :, cs]
                m_c = jnp.maximum(m_old, jnp.max(s_c, axis=0, keepdims=True))
                a_c = jnp.exp2(m_old - m_c)
                p_c = jnp.exp2(s_c - m_c)
                m_ref[hh, :, cs] = m_c
                l_ref[hh, :, cs] = a_c * l_ref[hh, :, cs] + jnp.sum(p_c, axis=0, keepdims=True)
                alpha.append(a_c)
                p.append(p_c.astype(BF16))
            acc_ref[hh] = (jnp.concatenate(alpha, axis=1) * acc_ref[hh]
                           + _dot(vtj, jnp.concatenate(p, axis=1)))

    scores(0, 0)

    def body(i, carry):
        j = 2 * i
        scores(j + 1, 1)
        update(j, 0, False)
        scores(j + 2, 0)
        update(j + 1, 1, False)
        return carry

    n_pairs = qi // 2
    lax.fori_loop(0, n_pairs, body, 0)
    j0 = 2 * n_pairs

    @pl.when(qi % 2 == 1)
    def _():
        scores(j0 + 1, 1)
        update(j0, 0, False)
        update(j0 + 1, 1, True)

    @pl.when(qi % 2 == 0)
    def _():
        update(j0, 0, True)

    lam = (jnp.exp(jnp.sum(lq1_ref[...] * lk1_ref[...], axis=-1, keepdims=True))
           - jnp.exp(jnp.sum(lq2_ref[...] * lk2_ref[...], axis=-1, keepdims=True)) + lam_init)
    for hh in range(n_heads):
        l = l_ref[hh]
        acc = acc_ref[hh]
        o12 = acc / l
        o = o12[:, :tq] - lam * o12[:, tq:]
        o = o * lax.rsqrt(jnp.mean(o * o, axis=0, keepdims=True) + EPS)
        o = o * g_ref[...] * (1.0 - lam_init)
        o_ref[:, hh * DIFF_DV:(hh + 1) * DIFF_DV] = o.T.astype(BF16)


def _diff_attn(z, lq1, lk1, lq2, lk2, norm_g, lam_init, bsz, seq):
    t = z.shape[0]
    tq = min(ATT_Q_ROWS, seq)
    nq = seq // tq
    hw = ATT_HEADS_PER_STEP * LANES
    qb, kb, vb = 1024 // hw, 1536 // hw, 2048 // hw
    small = pl.BlockSpec((1, DIFF_D), lambda b, h, i: (0, 0))
    return pl.pallas_call(
        functools.partial(_diff_attn_kernel, lam_init),
        grid=(bsz, DIFF_HEADS // ATT_HEADS_PER_STEP, nq),
        in_specs=[
            pl.BlockSpec((tq, hw), lambda b, h, i: (b * nq + i, qb + h)),
            pl.BlockSpec((seq, hw), lambda b, h, i: (b, kb + h)),
            pl.BlockSpec((seq, hw), lambda b, h, i: (b, vb + h)),
            small, small, small, small,
            pl.BlockSpec((DIFF_DV, 1), lambda b, h, i: (0, 0)),
        ],
        out_specs=pl.BlockSpec((tq, hw), lambda b, h, i: (b * nq + i, h)),
        out_shape=jax.ShapeDtypeStruct((t, DIFF_V_W), BF16),
        scratch_shapes=[pltpu.VMEM((ATT_HEADS_PER_STEP, DIFF_DV, seq), BF16),
                        pltpu.VMEM((2, ATT_HEADS_PER_STEP, tq, 2 * tq), F32),
                        pltpu.VMEM((ATT_HEADS_PER_STEP, 1, 2 * tq), F32),
                        pltpu.VMEM((ATT_HEADS_PER_STEP, 1, 2 * tq), F32),
                        pltpu.VMEM((ATT_HEADS_PER_STEP, DIFF_DV, 2 * tq), F32)],
        compiler_params=_cp(("arbitrary", "arbitrary", "arbitrary"), 32),
        name="diff_attn",
    )(z, z, z, lq1.reshape(1, DIFF_D), lk1.reshape(1, DIFF_D), lq2.reshape(1, DIFF_D),
      lk2.reshape(1, DIFF_D), norm_g.reshape(DIFF_DV, 1))


def _sgu_ret_kernel(su_ref, sv_ref, rv_ref, rg_ref, q_ref, k_ref, ws_ref, bs_ref, o_ref, st_ref):
    @pl.when(pl.program_id(1) == 0)
    def _():
        st_ref[...] = jnp.zeros_like(st_ref)

    c = RET_CHUNK
    row = lax.broadcasted_iota(jnp.int32, (c, c), 0)
    col = lax.broadcasted_iota(jnp.int32, (c, c), 1)
    causal = row >= col
    log_g = [math.log(1.0 - 2.0 ** (-5.0 - h)) for h in range(RET_HEADS)]
    lo = col < RET_DK
    rel = (row - col).astype(F32)
    pos = row.astype(F32)
    w_sgu = [jnp.where(causal, ws_ref[g], 0.0).astype(BF16) for g in range(SGU_GROUPS)]
    decays = [jnp.where(causal, jnp.exp(log_g[h] * jnp.maximum(rel, 0.0)), 0.0)
              for h in range(RET_HEADS)]
    lgs = [jnp.where(lo, log_g[2 * p], log_g[2 * p + 1]) for p in range(RET_HEADS // 2)]
    q_decs = [jnp.exp(lg * (pos + 1.0)) for lg in lgs]
    k_decs = [jnp.exp(lg * (c - 1.0 - pos)) for lg in lgs]
    states = [st_ref[p] for p in range(RET_HEADS // 2)]
    for j in range(su_ref.shape[0] // c):
        rs = slice(j * c, (j + 1) * c)
        for g in range(SGU_GROUPS):
            gs = slice(g * SGU_CH, (g + 1) * SGU_CH)
            s = _dot(w_sgu[g], sv_ref[rs, gs]) + bs_ref[g]
            o_ref[rs, gs] = (su_ref[rs, gs].astype(F32) * s).astype(BF16)
        for p in range(RET_HEADS // 2):
            ls = slice(p * LANES, (p + 1) * LANES)
            q_p = q_ref[rs, ls]
            k_p = k_ref[rs, ls]
            qm = (jnp.where(lo, q_p, 0.0).astype(BF16), jnp.where(lo, 0.0, q_p).astype(BF16))
            s2 = _dot_nt(jnp.concatenate(qm, axis=0), k_p.astype(BF16))
            qd = q_p * q_decs[p]
            qdm = (jnp.where(lo, qd, 0.0).astype(BF16), jnp.where(lo, 0.0, qd).astype(BF16))
            kd = (k_p * k_decs[p]).astype(BF16)
            st_b = states[p].astype(BF16)
            new = []
            for hh in range(2):
                h = 2 * p + hh
                hs = slice(h * RET_DV, (h + 1) * RET_DV)
                s_h = (s2[hh * c:(hh + 1) * c, :] * decays[h]).astype(BF16)
                v_h = rv_ref[rs, hs]
                o = _dot(s_h, v_h) + _dot_nt(qdm[hh], st_b)
                gate = _silu(rg_ref[rs, hs].astype(F32))
                o_ref[rs, SGU_W + h * RET_DV:SGU_W + (h + 1) * RET_DV] = (
                    _rms_rows(o) * gate).astype(BF16)
                new.append(_dot_tn(v_h, kd))
            states[p] = jnp.exp(lgs[p] * float(c)) * states[p] + jnp.where(lo, new[0], new[1])
    for p in range(RET_HEADS // 2):
        st_ref[p] = states[p]


def _sgu_ret(z, f, w_s, b_s, bsz, seq):
    t = z.shape[0]
    cc = RET_CHUNK
    c = min(SGU_RET_ROWS, seq)
    nc = seq // c
    return pl.pallas_call(
        _sgu_ret_kernel,
        grid=(bsz, nc),
        in_specs=[
            pl.BlockSpec((c, SGU_W), lambda b, i: (b * nc + i, 0)),
            pl.BlockSpec((c, SGU_W), lambda b, i: (b * nc + i, 1)),
            pl.BlockSpec((c, RET_V_W), lambda b, i: (b * nc + i, 2)),
            pl.BlockSpec((c, RET_V_W), lambda b, i: (b * nc + i, 3)),
            pl.BlockSpec((c, RET_QK_W), lambda b, i: (b * nc + i, 0)),
            pl.BlockSpec((c, RET_QK_W), lambda b, i: (b * nc + i, 1)),
            pl.BlockSpec((SGU_GROUPS, cc, cc), lambda b, i: (0, 0, 0)),
            pl.BlockSpec((SGU_GROUPS, cc, 1), lambda b, i: (0, 0, 0)),
        ],
        out_specs=pl.BlockSpec((c, SGU_W + RET_V_W), lambda b, i: (b * nc + i, 0)),
        out_shape=jax.ShapeDtypeStruct((t, SGU_W + RET_V_W), BF16),
        scratch_shapes=[pltpu.VMEM((2, LANES, LANES), F32)],
        compiler_params=_cp(("arbitrary", "arbitrary"), 32),
        name="sgu_retention",
    )(z, z, z, z, f, f, w_s, b_s.reshape(SGU_GROUPS, cc, 1))


def _out_proj_kernel(n_in, *refs):
    o_refs = refs[:n_in]
    w_ref, x_ref, mod_ref, rw_ref, rb_ref, xn_ref, h_ref, lg_ref = refs[n_in:]
    k_each = D_MODEL // n_in
    y = _dot(o_refs[0][...], w_ref[0:k_each, :])
    for n in range(1, n_in):
        y = y + _dot(o_refs[n][...], w_ref[n * k_each:(n + 1) * k_each, :])
    xn = x_ref[...] + mod_ref[0, 2:3, :] * y
    xn_ref[...] = xn
    h = _rms_rows(xn) * (1.0 + mod_ref[0, 4:5, :]) + mod_ref[0, 3:4, :]
    hb = h.astype(BF16)
    h_ref[...] = hb
    lg_ref[...] = _dot_nt(rw_ref[...], hb) + rb_ref[...]


def _out_proj(mixed, w_out, x2, mod, router_w, router_b, seq):
    t, d = x2.shape
    tm = min(PROJ_ROWS, seq)
    n_in = len(mixed)
    k_each = d // n_in
    rw = router_w.T.astype(BF16)
    rb = router_b.reshape(N_EXPERTS, 1)
    return pl.pallas_call(
        functools.partial(_out_proj_kernel, n_in),
        grid=(t // tm,),
        in_specs=[pl.BlockSpec((tm, k_each), lambda i: (i, 0)) for _ in mixed] + [
            pl.BlockSpec((d, d), lambda i: (0, 0)),
            pl.BlockSpec((tm, d), lambda i: (i, 0)),
            pl.BlockSpec((1, 6, d), lambda i: (i // (seq // tm), 0, 0)),
            pl.BlockSpec((N_EXPERTS, d), lambda i: (0, 0)),
            pl.BlockSpec((N_EXPERTS, 1), lambda i: (0, 0)),
        ],
        out_specs=[pl.BlockSpec((tm, d), lambda i: (i, 0)),
                   pl.BlockSpec((tm, d), lambda i: (i, 0)),
                   pl.BlockSpec((N_EXPERTS, tm), lambda i: (0, i))],
        out_shape=[jax.ShapeDtypeStruct((t, d), F32),
                   jax.ShapeDtypeStruct((t, d), BF16),
                   jax.ShapeDtypeStruct((N_EXPERTS, t), F32)],
        compiler_params=_cp(("arbitrary",), 48),
        name="out_proj",
    )(*mixed, w_out.astype(BF16), x2, mod, rw, rb)


def _route_kernel(lg_ref, idx_ref, gate_ref, tcnt_ref, toff_ref, trun_ref, cnt_ref, run_ref):
    @pl.when(pl.program_id(0) == 0)
    def _():
        run_ref[...] = jnp.zeros_like(run_ref)

    tm = lg_ref.shape[1]
    row = lax.broadcasted_iota(jnp.int32, (N_EXPERTS, tm), 0)
    neg = -jnp.inf
    l = lg_ref[...]
    vals, firsts, hots = [], [], []
    for _ in range(TOP_K):
        m = jnp.max(l, axis=0, keepdims=True)
        first = jnp.min(jnp.where(l == m, row, N_EXPERTS), axis=0, keepdims=True)
        hot = row == first
        vals.append(m)
        firsts.append(first)
        hots.append(hot)
        l = jnp.where(hot, neg, l)
    sel = hots[0] | hots[1] | hots[2] | hots[3]
    ex = [jnp.exp(v - vals[0]) for v in vals]
    denom = ex[0] + ex[1] + ex[2] + ex[3]
    r_i = lax.broadcasted_iota(jnp.int32, (tm, tm), 0)
    c_i = lax.broadcasted_iota(jnp.int32, (tm, tm), 1)
    before = (r_i < c_i).astype(BF16)
    earlier = _dot(sel.astype(BF16), before)
    tile_cnt = jnp.sum(sel.astype(F32), axis=1, keepdims=True)
    lower = jnp.zeros((N_EXPERTS, tm), F32)
    for k in range(TOP_K):
        lower = lower + (firsts[k] < row).astype(F32)
    tile_off = jnp.sum(lower, axis=1, keepdims=True)
    slot = tile_off + earlier
    row8 = lax.broadcasted_iota(jnp.int32, (2 * TOP_K, tm), 0)
    idx_out = jnp.zeros((2 * TOP_K, tm), jnp.int32)
    gate_out = jnp.zeros((2 * TOP_K, tm), F32)
    for k in range(TOP_K):
        pos_k = jnp.sum(jnp.where(hots[k], slot, 0.0), axis=0, keepdims=True).astype(jnp.int32)
        idx_out = jnp.where(row8 == k, firsts[k], idx_out)
        idx_out = jnp.where(row8 == TOP_K + k, pos_k, idx_out)
        gate_out = jnp.where(row8 == k, ex[k] / denom, gate_out)
    idx_ref[...] = idx_out
    gate_ref[...] = gate_out
    run = run_ref[:, 0:1]
    tcnt_ref[...] = jnp.broadcast_to(tile_cnt, tcnt_ref.shape)
    toff_ref[...] = jnp.broadcast_to(tile_off, toff_ref.shape)
    trun_ref[...] = jnp.broadcast_to(run, trun_ref.shape)
    total = run + tile_cnt
    run_ref[...] = jnp.broadcast_to(total, run_ref.shape)
    cnt_ref[...] = jnp.broadcast_to(total, cnt_ref.shape)


def _route(logits_t):
    t = logits_t.shape[1]
    tm = min(MOE_TILE, t)
    nt = t // tm
    per_tile = pl.BlockSpec((N_EXPERTS, LANES), lambda i: (i, 0))
    per_tile_shape = jax.ShapeDtypeStruct((nt * N_EXPERTS, LANES), F32)
    return pl.pallas_call(
        _route_kernel,
        grid=(nt,),
        in_specs=[pl.BlockSpec((N_EXPERTS, tm), lambda i: (0, i))],
        out_specs=[pl.BlockSpec((2 * TOP_K, tm), lambda i: (0, i)),
                   pl.BlockSpec((2 * TOP_K, tm), lambda i: (0, i)),
                   per_tile, per_tile, per_tile,
                   pl.BlockSpec((N_EXPERTS, LANES), lambda i: (0, 0))],
        out_shape=[jax.ShapeDtypeStruct((2 * TOP_K, t), jnp.int32),
                   jax.ShapeDtypeStruct((2 * TOP_K, t), F32),
                   per_tile_shape, per_tile_shape, per_tile_shape,
                   jax.ShapeDtypeStruct((N_EXPERTS, LANES), F32)],
        scratch_shapes=[pltpu.VMEM((N_EXPERTS, LANES), F32)],
        compiler_params=_cp(("arbitrary",), 32),
        name="route",
    )(logits_t)


def _rows(ref, start, n):
    return ref.at[pl.ds(pl.multiple_of(start * ROW_TILES, ROW_TILES), n * ROW_TILES), :]


def _run_copies(tile, cnt_ref, off_ref, dst_ref, make_copy):
    def per_expert(e, carry):
        j = tile * N_EXPERTS + e
        cnt, off, dst = cnt_ref[j], off_ref[j], dst_ref[j]

        for b in reversed(range(RUN_BITS)):
            size = 1 << b

            @pl.when((cnt & size) != 0)
            def _():
                done = lax.shift_left(lax.shift_right_logical(cnt, b + 1), b + 1)
                make_copy(off + done, dst + done, size).start()
        return carry

    lax.fori_loop(0, N_EXPERTS, per_expert, 0)


def _dispatch_kernel(fs_ref, fl_ref, cnt_ref, off_ref, dst_ref, h_ref, idx_ref, xs_ref,
                     stage, zero_buf, sems, zsem):
    i = pl.program_id(0)
    n_steps = pl.num_programs(0)
    cur = lax.rem(i, 2)
    tm = h_ref.shape[0]
    n_sorted = TOP_K * tm
    n_fill = fs_ref.shape[0]

    def fill_copies(f, wait):
        start, n = fs_ref[f], fl_ref[f]
        n_chunks = lax.shift_right_logical(n, FILL_SHIFT)
        tail = start + n_chunks * FILL_ROWS

        def chunk(j, carry):
            cp = pltpu.make_async_copy(zero_buf, _rows(xs_ref, start + j * FILL_ROWS, FILL_ROWS),
                                       zsem)
            cp.wait() if wait else cp.start()
            return carry

        def single(j, carry):
            cp = pltpu.make_async_copy(_rows(zero_buf, 0, 1), _rows(xs_ref, tail + j, 1), zsem)
            cp.wait() if wait else cp.start()
            return carry

        lax.fori_loop(0, n_chunks, chunk, 0)
        lax.fori_loop(0, n - n_chunks * FILL_ROWS, single, 0)

    @pl.when(i == 0)
    def _():
        zero_buf[...] = jnp.zeros_like(zero_buf)
        lax.fori_loop(0, n_fill, lambda f, c: (fill_copies(f, False), c)[1], 0)

    h = h_ref[...]
    pos = [idx_ref[TOP_K + k:TOP_K + k + 1, :] for k in range(TOP_K)]
    for sb in range(n_sorted // SORT_SLAB):
        j = lax.broadcasted_iota(jnp.int32, (SORT_SLAB, tm), 0) + sb * SORT_SLAB
        perm = jnp.zeros((SORT_SLAB, tm), F32)
        for k in range(TOP_K):
            perm = jnp.where(pos[k] == j, 1.0, perm)
        perm = perm.astype(BF16)
        rows = _dot(perm, h)
        for c in range(ROW_TILES):
            stage[cur, pl.ds(sb * SORT_SLAB * ROW_TILES + c, SORT_SLAB, stride=ROW_TILES), :] = (
                rows[:, c * LANES:(c + 1) * LANES])

    def make_copy(src_row, dst_row, size):
        return pltpu.make_async_copy(_rows(stage.at[cur], src_row, size),
                                     _rows(xs_ref, dst_row, size), sems.at[cur])

    _run_copies(i, cnt_ref, off_ref, dst_ref, make_copy)

    def wait_slot(slot):
        pltpu.make_async_copy(stage.at[slot], _rows(xs_ref, 0, n_sorted), sems.at[slot]).wait()

    @pl.when(i > 0)
    def _():
        wait_slot(1 - cur)

    @pl.when(i == n_steps - 1)
    def _():
        wait_slot(cur)

    @pl.when(i == 0)
    def _():
        lax.fori_loop(0, n_fill, lambda f, c: (fill_copies(f, True), c)[1], 0)


def _dispatch(h2, idx, tile_cnt, tile_off, tile_dst, fill_start, fill_len, n_rows):
    t, d = h2.shape
    tm = min(MOE_TILE, t)
    grid_spec = pltpu.PrefetchScalarGridSpec(
        num_scalar_prefetch=5,
        grid=(t // tm,),
        in_specs=[pl.BlockSpec((tm, d), lambda i, *_: (i, 0)),
                  pl.BlockSpec((2 * TOP_K, tm), lambda i, *_: (0, i))],
        out_specs=pl.BlockSpec(memory_space=pl.ANY),
        scratch_shapes=[pltpu.VMEM((2, TOP_K * tm * ROW_TILES, LANES), F32),
                        pltpu.VMEM((FILL_ROWS * ROW_TILES, LANES), F32),
                        pltpu.SemaphoreType.DMA((2,)), pltpu.SemaphoreType.DMA(())],
    )
    return pl.pallas_call(
        _dispatch_kernel,
        grid_spec=grid_spec,
        out_shape=jax.ShapeDtypeStruct((n_rows * ROW_TILES, LANES), F32),
        compiler_params=_cp(("arbitrary",), 48),
        name="dispatch",
    )(fill_start, fill_len, tile_cnt, tile_off, tile_dst, h2, idx)


def _expert_kernel(layer, be_ref, nu_ref, nx_ref, nv_ref, xs_ref, wi_hbm, bi_ref, wo_hbm, bo_ref,
                   y_ref, wi_st, wo_st, wi_b, wo_b, sems):
    i = pl.program_id(0)
    tb = xs_ref.shape[0] // ROW_TILES
    e = be_ref[i]
    fresh = jnp.logical_or(i == 0, e != be_ref[jnp.maximum(i - 1, 0)])
    used = i < nu_ref[0]

    def fetch(ex):
        return (pltpu.make_async_copy(wi_hbm.at[layer, ex], wi_st, sems.at[0]),
                pltpu.make_async_copy(wo_hbm.at[layer, ex], wo_st, sems.at[1]))

    @pl.when(i == 0)
    def _():
        for cp in fetch(e):
            cp.start()

    def take_weights():
        for cp in fetch(e):
            cp.wait()
        wi_b[...] = wi_st[...].astype(BF16)
        wo_b[...] = wo_st[...].astype(BF16)

    def fetch_next():
        @pl.when(nx_ref[i] >= 0)
        def _():
            for cp in fetch(nx_ref[i]):
                cp.start()

    def compute(n):
        x = jnp.concatenate(
            [xs_ref[pl.ds(c, n, stride=ROW_TILES), :] for c in range(ROW_TILES)],
            axis=1).astype(BF16)
        y = jnp.zeros((n, D_MODEL), F32) + bo_ref[0, 0]
        half = 1024
        for j in range(D_FF // half):
            a, b = j * half, (j + 1) * half
            glu = _dot(x, wi_b[:, a:b]) + bi_ref[0, 0, :, a:b]
            lin = _dot(x, wi_b[:, D_FF + a:D_FF + b]) + bi_ref[0, 0, :, D_FF + a:D_FF + b]
            glu = jnp.minimum(glu, SWIGLU_LIMIT)
            lin = jnp.clip(lin, -SWIGLU_LIMIT, SWIGLU_LIMIT)
            act = glu * jax.nn.sigmoid(SWIGLU_ALPHA * glu) * (lin + 1.0)
            y = y + _dot(act.astype(BF16), wo_b[a:b, :])
        for c in range(ROW_TILES):
            y_ref[pl.ds(c, n, stride=ROW_TILES), :] = y[:, c * LANES:(c + 1) * LANES]

    full = nv_ref[i] > tb // 2

    @pl.when(jnp.logical_and(used, jnp.logical_and(full, fresh)))
    def _():
        take_weights()
        compute(tb)
        fetch_next()

    @pl.when(jnp.logical_and(used, jnp.logical_and(full, jnp.logical_not(fresh))))
    def _():
        compute(tb)

    @pl.when(jnp.logical_and(used, jnp.logical_not(full)))
    def _():
        @pl.when(fresh)
        def _():
            take_weights()
            fetch_next()

        compute(tb // 2)
        y_ref[pl.ds(tb // 2 * ROW_TILES, tb // 2 * ROW_TILES), :] = jnp.zeros(
            (tb // 2 * ROW_TILES, LANES), F32)

    @pl.when(jnp.logical_not(used))
    def _():
        y_ref[...] = jnp.zeros_like(y_ref)


def _experts(xs, block_e, n_used, next_e, n_valid, layer, w_in, b_in, w_out, b_out):
    tb = EXPERT_ROWS
    n_rows = xs.shape[0] // ROW_TILES
    nb = n_rows // tb
    depth, ne, d, f2 = w_in.shape

    def row_map(i, be, nu, nx, nv):
        return (jnp.minimum(i, nu[0] - 1), 0)

    grid_spec = pltpu.PrefetchScalarGridSpec(
        num_scalar_prefetch=4,
        grid=(nb,),
        in_specs=[
            pl.BlockSpec((tb * ROW_TILES, LANES), row_map),
            pl.BlockSpec(memory_space=pl.ANY),
            pl.BlockSpec((1, 1, 1, f2), lambda i, be, nu, nx, nv: (layer, be[i], 0, 0)),
            pl.BlockSpec(memory_space=pl.ANY),
            pl.BlockSpec((1, 1, 1, d), lambda i, be, nu, nx, nv: (layer, be[i], 0, 0)),
        ],
        out_specs=pl.BlockSpec((tb * ROW_TILES, LANES), lambda i, be, nu, nx, nv: (i, 0)),
        scratch_shapes=[pltpu.VMEM((d, f2), F32), pltpu.VMEM((D_FF, d), F32),
                        pltpu.VMEM((d, f2), BF16), pltpu.VMEM((D_FF, d), BF16),
                        pltpu.SemaphoreType.DMA((2,))],
    )
    return pl.pallas_call(
        functools.partial(_expert_kernel, layer),
        grid_spec=grid_spec,
        out_shape=jax.ShapeDtypeStruct((n_rows * ROW_TILES, LANES), F32),
        compiler_params=_cp(("arbitrary",), 56),
        name="experts",
    )(block_e, n_used, next_e, n_valid, xs, w_in, b_in.reshape(depth, ne, 1, f2), w_out,
      b_out.reshape(depth, ne, 1, d))


def _combine_kernel(final, cnt_ref, off_ref, dst_ref, pos_ref, gate_ref, x_ref, mod_ref, fg_ref,
                    yb_ref, o_ref, stage, sems):
    tm = x_ref.shape[0]
    n_sorted = TOP_K * tm
    i = pl.program_id(0)
    cur = lax.rem(i, 2)

    def fetch(tile, slot):
        def make_copy(sorted_row, src_row, size):
            return pltpu.make_async_copy(_rows(yb_ref, src_row, size),
                                         _rows(stage.at[slot], sorted_row, size), sems.at[slot])

        _run_copies(tile, cnt_ref, off_ref, dst_ref, make_copy)

    @pl.when(i == 0)
    def _():
        fetch(0, 0)

    @pl.when(i + 1 < pl.num_programs(0))
    def _():
        fetch(i + 1, 1 - cur)

    pltpu.make_async_copy(_rows(yb_ref, 0, n_sorted), stage.at[cur], sems.at[cur]).wait()

    pos = [pos_ref[:, TOP_K + k:TOP_K + k + 1] for k in range(TOP_K)]
    gate = [gate_ref[:, k:k + 1] for k in range(TOP_K)]
    y = jnp.zeros((tm, D_MODEL), F32)
    for sb in range(n_sorted // SORT_SLAB):
        rows = jnp.concatenate(
            [stage[cur, pl.ds(sb * SORT_SLAB * ROW_TILES + c, SORT_SLAB, stride=ROW_TILES), :]
             for c in range(ROW_TILES)], axis=1).astype(BF16)
        j = lax.broadcasted_iota(jnp.int32, (tm, SORT_SLAB), 1) + sb * SORT_SLAB
        w = jnp.zeros((tm, SORT_SLAB), F32)
        for k in range(TOP_K):
            w = jnp.where(pos[k] == j, gate[k], w)
        y = y + _dot(w.astype(BF16), rows)
    o = x_ref[...] + mod_ref[0, 5:6, :] * y
    if final:
        o = _rms_rows(o) * fg_ref[...]
    o_ref[...] = o


def _combine(yb, tile_cnt, tile_off, tile_dst, pos_t, gates_t, x2, mod, final_g, final, seq):
    t, d = x2.shape
    tm = min(MOE_TILE, seq)
    grid_spec = pltpu.PrefetchScalarGridSpec(
        num_scalar_prefetch=3,
        grid=(t // tm,),
        in_specs=[
            pl.BlockSpec((tm, 2 * TOP_K), lambda i, *_: (i, 0)),
            pl.BlockSpec((tm, 2 * TOP_K), lambda i, *_: (i, 0)),
            pl.BlockSpec((tm, d), lambda i, *_: (i, 0)),
            pl.BlockSpec((1, 6, d), lambda i, *_: (i // (seq // tm), 0, 0)),
            pl.BlockSpec((1, d), lambda i, *_: (0, 0)),
            pl.BlockSpec(memory_space=pl.ANY),
        ],
        out_specs=pl.BlockSpec((tm, d), lambda i, *_: (i, 0)),
        scratch_shapes=[pltpu.VMEM((2, TOP_K * tm * ROW_TILES, LANES), F32),
                        pltpu.SemaphoreType.DMA((2,))],
    )
    return pl.pallas_call(
        functools.partial(_combine_kernel, final),
        grid_spec=grid_spec,
        out_shape=jax.ShapeDtypeStruct((t, d), F32),
        compiler_params=_cp(("arbitrary",), 48),
        name="combine",
    )(tile_cnt, tile_off, tile_dst, pos_t, gates_t, x2, mod, final_g.reshape(1, d), yb)


def _moe(h2, logits, x2, mod, layer, w_in, b_in, w_out, b_out, final_g, final, seq):
    t = x2.shape[0]
    tb = EXPERT_ROWS
    idx, gates, tcnt, toff, trun, cnt = _route(logits)
    counts = cnt[:, 0].astype(jnp.int32)
    nblk = (counts + tb - 1) // tb
    blk_end = jnp.cumsum(nblk)
    pad_start = (blk_end - nblk) * tb
    n_blocks = (t * TOP_K) // tb + N_EXPERTS
    n_used = blk_end[-1:]
    experts = jnp.arange(N_EXPERTS, dtype=jnp.int32)
    last_e = jnp.max(jnp.where(nblk > 0, experts, 0))
    blk = jnp.arange(n_blocks, dtype=jnp.int32)
    block_e = jnp.minimum(
        jnp.sum((blk_end[None, :] <= blk[:, None]).astype(jnp.int32), axis=1), last_e)
    later = (experts[None, :] > block_e[:, None]) & (nblk[None, :] > 0)
    next_e = jnp.min(jnp.where(later, experts[None, :], N_EXPERTS), axis=1)
    next_e = jnp.where(next_e == N_EXPERTS, -1, next_e).astype(jnp.int32)
    tile_cnt = tcnt[:, 0].astype(jnp.int32)
    tile_off = toff[:, 0].astype(jnp.int32)
    tile_dst = (trun[:, 0].astype(jnp.int32).reshape(-1, N_EXPERTS) + pad_start[None, :]).reshape(-1)
    fill_start = jnp.concatenate([pad_start + counts, blk_end[-1:] * tb]).astype(jnp.int32)
    fill_len = jnp.concatenate([nblk * tb - counts, (n_blocks - blk_end[-1:]) * tb]).astype(jnp.int32)
    xs = _dispatch(h2, idx, tile_cnt, tile_off, tile_dst, fill_start, fill_len, n_blocks * tb)
    row_end = jnp.sum(jnp.where(block_e[:, None] == experts[None, :],
                                (pad_start + counts)[None, :], 0), axis=1)
    n_valid = jnp.clip(row_end - blk * tb, 0, tb).astype(jnp.int32)
    yb = _experts(xs, block_e, n_used.astype(jnp.int32), next_e, n_valid, layer,
                  w_in, b_in, w_out, b_out)
    return _combine(yb, tile_cnt, tile_off, tile_dst, idx.T, gates.T, x2, mod, final_g, final, seq)


def kernel(x, c, positions, w_ada, b_ada, even_w_in, gla_w_gate, gla_b_gate, gla_norm_g,
           diff_lam_q1, diff_lam_k1, diff_lam_q2, diff_lam_k2, diff_norm_g, even_w_out,
           odd_w_in, sgu_ln_g, sgu_ln_b, sgu_w, sgu_b, odd_w_out,
           router_w, router_b, expert_w_in, expert_b_in, expert_w_out, expert_b_out,
           final_norm_g):
    bsz, seq, d = x.shape
    depth = w_ada.shape[0]
    t = bsz * seq
    mods = _modulation(c, w_ada, b_ada).reshape(depth, bsz, 6, d)
    cos, sin = _rope_tables(positions)
    x2 = x.reshape(t, d)
    for layer in range(depth):
        mod = mods[layer]
        j = layer // 2
        if layer % 2 == 0:
            z, f = _proj_even(x2, mod, even_w_in[j], gla_w_gate[j], gla_b_gate[j], cos, sin, seq)
            o_gla = _gla(z, f, gla_norm_g[j], bsz, seq)
            lam_init = 0.8 - 0.6 * math.exp(-0.3 * layer)
            o_diff = _diff_attn(z, diff_lam_q1[j], diff_lam_k1[j], diff_lam_q2[j], diff_lam_k2[j],
                                diff_norm_g[j], lam_init, bsz, seq)
            mixed, w_out = (o_gla, o_diff), even_w_out[j]
        else:
            z, f = _proj_odd(x2, mod, odd_w_in[j], sgu_ln_g[j], sgu_ln_b[j], cos, sin, seq)
            mixed, w_out = (_sgu_ret(z, f, sgu_w[j], sgu_b[j], bsz, seq),), odd_w_out[j]
        x2, h2, logits = _out_proj(mixed, w_out, x2, mod, router_w[layer], router_b[layer], seq)
        x2 = _moe(h2, logits, x2, mod, layer, expert_w_in, expert_b_in, expert_w_out,
                  expert_b_out, final_norm_g, layer == depth - 1, seq)
    return x2.reshape(bsz, seq, d)
```

```python
import functools
import math

import jax
import jax.numpy as jnp
from jax import lax
from jax.experimental import pallas as pl
from jax.experimental.pallas import tpu as pltpu

F32 = jnp.float32
BF16 = jnp.bfloat16

D_MODEL = 1024
EPS = 1e-6
ROPE_THETA = 10000.0
ROPE_HALF = 32

GLA_HEADS = 4
GLA_DK = 64
GLA_DV = 128
GLA_RANK = 16
GLA_CHUNK = 64
GLA_GATE_NORMALIZER = 16.0
GLA_QK_W = GLA_HEADS * GLA_DK
GLA_V_W = GLA_HEADS * GLA_DV

DIFF_HEADS = 4
DIFF_D = 64
DIFF_DV = 128
DIFF_QK_W = DIFF_HEADS * 2 * DIFF_D
DIFF_V_W = DIFF_HEADS * DIFF_DV

SGU_GROUPS = 4
SGU_CH = 128
SGU_CHUNK = 128
SGU_W = SGU_GROUPS * SGU_CH

RET_HEADS = 4
RET_DK = 64
RET_DV = 128
RET_CHUNK = 128
RET_QK_W = RET_HEADS * RET_DK
RET_V_W = RET_HEADS * RET_DV

N_EXPERTS = 32
TOP_K = 4
D_FF = D_MODEL
SWIGLU_ALPHA = 1.702
SWIGLU_LIMIT = 7.0

LANES = 128
SUBLANES = 8
ROW_TILES = D_MODEL // LANES

PROJ_ROWS = 1024
GLA_ROWS = 512
GLA_SEQS_PER_STEP = 2
GLA_GROUP = 4
SGU_RET_ROWS = 512
ATT_Q_ROWS = 256
ATT_HEADS_PER_STEP = 4
MOE_TILE = 512
SORT_SLAB = 512
RUN_BITS = 10
FILL_SHIFT = 6
FILL_ROWS = 1 << FILL_SHIFT
EXPERT_ROWS = 512
EXPERT_BLOCKS_PER_STEP = 2
MOD_COLS = 1536

MIB = 1024 * 1024


def _cp(semantics, vmem_mib):
    return pltpu.CompilerParams(dimension_semantics=semantics, vmem_limit_bytes=vmem_mib * MIB)


def _dot(a, b):
    return jnp.dot(a, b, preferred_element_type=F32)


def _dot_nt(a, b):
    return lax.dot_general(a, b, (((1,), (1,)), ((), ())), preferred_element_type=F32)


def _dot_tn(a, b):
    return lax.dot_general(a, b, (((0,), (0,)), ((), ())), preferred_element_type=F32)


def _rms_rows(x):
    return x * lax.rsqrt(jnp.mean(x * x, axis=-1, keepdims=True) + EPS)


def _silu(x):
    return x * jax.nn.sigmoid(x)


def _rope_chunk(v, cos, sin, lo_half):
    rot = jnp.where(lo_half, -pltpu.roll(v, 96, 1), pltpu.roll(v, 32, 1))
    return v * cos + rot * sin


def _mod_kernel(c_ref, w_ref, b_ref, o_ref):
    c = c_ref[...]
    ca = _silu(c).astype(BF16)
    o_ref[0] = _dot(ca, w_ref[0].astype(BF16)) + b_ref[0]


def _modulation(c, w_ada, b_ada):
    depth, d, n = w_ada.shape
    bsz = c.shape[0]
    return pl.pallas_call(
        _mod_kernel,
        grid=(depth, n // MOD_COLS),
        in_specs=[
            pl.BlockSpec((bsz, d), lambda l, j: (0, 0)),
            pl.BlockSpec((1, d, MOD_COLS), lambda l, j: (l, 0, j)),
            pl.BlockSpec((1, 1, MOD_COLS), lambda l, j: (l, 0, j)),
        ],
        out_specs=pl.BlockSpec((1, bsz, MOD_COLS), lambda l, j: (l, 0, j)),
        out_shape=jax.ShapeDtypeStruct((depth, bsz, n), F32),
        compiler_params=_cp(("arbitrary", "arbitrary"), 40),
        name="adaln_mod",
    )(c, w_ada, b_ada.reshape(depth, 1, n))


def _rope_table_kernel(p_ref, f_ref, c_ref, s_ref):
    ang = p_ref[...].astype(F32) * f_ref[...]
    c_ref[...] = jnp.cos(ang)
    s_ref[...] = jnp.sin(ang)


def _rope_tables(positions):
    t = positions.size
    per_row = LANES // ROPE_HALF
    rows = t // per_row
    pos_d = jnp.repeat(positions.reshape(rows, per_row), ROPE_HALF, axis=1)
    inv_freq = ROPE_THETA ** (-jnp.arange(ROPE_HALF, dtype=F32) / ROPE_HALF)
    freq_d = jnp.tile(inv_freq, per_row).reshape(1, LANES)
    tr = min(512, rows)
    cos_d, sin_d = pl.pallas_call(
        _rope_table_kernel,
        grid=(rows // tr,),
        in_specs=[pl.BlockSpec((tr, LANES), lambda i: (i, 0)),
                  pl.BlockSpec((1, LANES), lambda i: (0, 0))],
        out_specs=[pl.BlockSpec((tr, LANES), lambda i: (i, 0))] * 2,
        out_shape=[jax.ShapeDtypeStruct((rows, LANES), F32)] * 2,
        compiler_params=_cp(("arbitrary",), 32),
        name="rope_tables",
    )(pos_d, freq_d)
    cos = jnp.tile(cos_d.reshape(t, ROPE_HALF), (1, per_row))
    sin = jnp.tile(sin_d.reshape(t, ROPE_HALF), (1, per_row))
    return cos, sin


def _modulated_rms(x_ref, mod_ref, which):
    x = x_ref[...]
    shift = mod_ref[0, 3 * which:3 * which + 1, :]
    scale = mod_ref[0, 3 * which + 1:3 * which + 2, :]
    return _rms_rows(x) * (1.0 + scale) + shift


EVEN_Z = (GLA_V_W, GLA_V_W, DIFF_QK_W, DIFF_QK_W, DIFF_V_W)
EVEN_Z_W = sum(EVEN_Z)
EVEN_F_W = 3 * GLA_QK_W
EVEN_W_COLS = EVEN_Z_W + 2 * GLA_QK_W + LANES


def _proj_even_kernel(x_ref, mod_ref, w_ref, wg_ref, bg_ref, cos_ref, sin_ref, z_ref, f_ref):
    hb = _modulated_rms(x_ref, mod_ref, 0).astype(BF16)

    def sec(a, b):
        return _dot(hb, w_ref[:, a:b])

    for a, b in ((0, 512), (512, 1024), (2048, 2560)):
        z_ref[:, a:b] = sec(a, b).astype(BF16)
    cos = cos_ref[...]
    sin = sin_ref[...]
    lo_half = (lax.broadcasted_iota(jnp.int32, cos.shape, 1) % 64) < ROPE_HALF
    q_scale = DIFF_D ** -0.5 * math.log2(math.e)
    for a, scl in ((1024, q_scale), (1536, 1.0)):
        full = sec(a, a + DIFF_QK_W)
        for c in range(DIFF_QK_W // LANES):
            v = full[:, c * LANES:(c + 1) * LANES]
            z_ref[:, a + c * LANES:a + (c + 1) * LANES] = (
                _rope_chunk(v, cos, sin, lo_half) * scl).astype(BF16)
    f_ref[:, 0:256] = sec(2560, 2816) * (GLA_DK ** -0.5)
    f_ref[:, 256:512] = sec(2816, 3072)
    gr = sec(3072, 3200).astype(BF16)
    pre = _dot(gr, wg_ref[...]) + bg_ref[...]
    log_sig = jnp.minimum(pre, 0.0) - jnp.log1p(jnp.exp(-jnp.abs(pre)))
    f_ref[:, 512:768] = log_sig / GLA_GATE_NORMALIZER


def _proj_even(x2, mod, w_in, w_gate, b_gate, cos, sin, seq):
    t, d = x2.shape
    tm = min(PROJ_ROWS, seq)
    gq, gk, gv, gr, gg, dq, dk, dv = jnp.split(
        w_in, [256, 512, 1024, 1040, 1552, 2064, 2576], axis=1)
    gr_pad = jnp.pad(gr, ((0, 0), (0, LANES - GLA_RANK)))
    w_cat = jnp.concatenate([gv, gg, dq, dk, dv, gq, gk, gr_pad], axis=1).astype(BF16)
    wg_pad = jnp.pad(w_gate, ((0, LANES - GLA_RANK), (0, 0))).astype(BF16)
    return pl.pallas_call(
        _proj_even_kernel,
        grid=(t // tm,),
        in_specs=[
            pl.BlockSpec((tm, d), lambda i: (i, 0)),
            pl.BlockSpec((1, 6, d), lambda i: (i // (seq // tm), 0, 0)),
            pl.BlockSpec((d, EVEN_W_COLS), lambda i: (0, 0)),
            pl.BlockSpec((LANES, GLA_QK_W), lambda i: (0, 0)),
            pl.BlockSpec((1, GLA_QK_W), lambda i: (0, 0)),
            pl.BlockSpec((tm, LANES), lambda i: (i, 0)),
            pl.BlockSpec((tm, LANES), lambda i: (i, 0)),
        ],
        out_specs=[pl.BlockSpec((tm, EVEN_Z_W), lambda i: (i, 0)),
                   pl.BlockSpec((tm, EVEN_F_W), lambda i: (i, 0))],
        out_shape=[jax.ShapeDtypeStruct((t, EVEN_Z_W), BF16),
                   jax.ShapeDtypeStruct((t, EVEN_F_W), F32)],
        compiler_params=_cp(("arbitrary",), 48),
        name="proj_even",
    )(x2, mod, w_cat, wg_pad, b_gate.reshape(1, GLA_QK_W), cos, sin)


ODD_Z_W = 2 * SGU_W + 2 * RET_V_W
ODD_F_W = 2 * RET_QK_W
ODD_W_COLS = ODD_Z_W + ODD_F_W


def _gelu_exact(x):
    return 0.5 * x * (1.0 + lax.erf(x * (2.0 ** -0.5)))


def _proj_odd_kernel(x_ref, mod_ref, w_ref, lng_ref, lnb_ref, cos_ref, sin_ref, z_ref, f_ref):
    hb = _modulated_rms(x_ref, mod_ref, 0).astype(BF16)

    def sec(a, b):
        return _dot(hb, w_ref[:, a:b])

    z_ref[:, 0:512] = _gelu_exact(sec(0, 512)).astype(BF16)
    sv = _gelu_exact(sec(512, 1024))
    mu = jnp.mean(sv, axis=-1, keepdims=True)
    cen = sv - mu
    var = jnp.mean(cen * cen, axis=-1, keepdims=True)
    z_ref[:, 512:1024] = (cen * lax.rsqrt(var + EPS) * lng_ref[...] + lnb_ref[...]).astype(BF16)
    for a, b in ((1024, 1536), (1536, 2048)):
        z_ref[:, a:b] = sec(a, b).astype(BF16)
    cos = cos_ref[...]
    sin = sin_ref[...]
    lo_half = (lax.broadcasted_iota(jnp.int32, cos.shape, 1) % 64) < ROPE_HALF
    k_scale = RET_DK ** -0.5
    for a, scl in ((0, 1.0), (256, k_scale)):
        full = sec(ODD_Z_W + a, ODD_Z_W + a + RET_QK_W)
        for c in range(RET_QK_W // LANES):
            v = full[:, c * LANES:(c + 1) * LANES]
            f_ref[:, a + c * LANES:a + (c + 1) * LANES] = _rope_chunk(v, cos, sin, lo_half) * scl


def _proj_odd(x2, mod, w_in, ln_g, ln_b, cos, sin, seq):
    t, d = x2.shape
    tm = min(PROJ_ROWS, seq)
    su, sv, rq, rk, rv, rg = jnp.split(w_in, [512, 1024, 1280, 1536, 2048], axis=1)
    w_cat = jnp.concatenate([su, sv, rv, rg, rq, rk], axis=1).astype(BF16)
    return pl.pallas_call(
        _proj_odd_kernel,
        grid=(t // tm,),
        in_specs=[
            pl.BlockSpec((tm, d), lambda i: (i, 0)),
            pl.BlockSpec((1, 6, d), lambda i: (i // (seq // tm), 0, 0)),
            pl.BlockSpec((d, ODD_W_COLS), lambda i: (0, 0)),
            pl.BlockSpec((1, SGU_W), lambda i: (0, 0)),
            pl.BlockSpec((1, SGU_W), lambda i: (0, 0)),
            pl.BlockSpec((tm, LANES), lambda i: (i, 0)),
            pl.BlockSpec((tm, LANES), lambda i: (i, 0)),
        ],
        out_specs=[pl.BlockSpec((tm, ODD_Z_W), lambda i: (i, 0)),
                   pl.BlockSpec((tm, ODD_F_W), lambda i: (i, 0))],
        out_shape=[jax.ShapeDtypeStruct((t, ODD_Z_W), BF16),
                   jax.ShapeDtypeStruct((t, ODD_F_W), F32)],
        compiler_params=_cp(("arbitrary",), 48),
        name="proj_odd",
    )(x2, mod, w_cat, ln_g.reshape(1, SGU_W), ln_b.reshape(1, SGU_W), cos, sin)


def _gla_kernel(q_ref, k_ref, la_ref, v_ref, gg_ref, g_ref, o_ref, st_ref):
    n_seq = q_ref.shape[0]
    n_pairs = GLA_HEADS // 2

    @pl.when(pl.program_id(1) == 0)
    def _():
        st_ref[...] = jnp.zeros_like(st_ref)

    c = GLA_CHUNK
    sc = GLA_GROUP * c
    rows = q_ref.shape[1]
    r_i = lax.broadcasted_iota(jnp.int32, (sc, sc), 0)
    c_i = lax.broadcasted_iota(jnp.int32, (sc, sc), 1)
    block_causal = (r_i // c == c_i // c) & (r_i >= c_i)
    tril = block_causal.astype(BF16)
    causal2 = jnp.concatenate([block_causal, block_causal], axis=0)
    lo = lax.broadcasted_iota(jnp.int32, (sc, LANES), 1) < GLA_DK
    row_chunk = lax.broadcasted_iota(jnp.int32, (sc, LANES), 0) // c
    lane_s = lax.broadcasted_iota(jnp.int32, (LANES, LANES), 1) < GLA_DK
    g_row = g_ref[...]
    states = [st_ref[n] for n in range(n_seq * n_pairs)]
    for j, s in [(j, s) for j in range(rows // sc) for s in range(n_seq)]:
        rs = slice(j * sc, (j + 1) * sc)
        la = la_ref[s, rs, :]
        la_hi = la.astype(BF16)
        la_lo = (la - la_hi.astype(F32)).astype(BF16)
        b = _dot(tril, la_hi) + _dot(tril, la_lo)
        b_last = [b[(g + 1) * c - 1:(g + 1) * c, :] for g in range(GLA_GROUP)]
        b_last_rows = jnp.concatenate(
            [jnp.broadcast_to(bl, (c, bl.shape[1])) for bl in b_last], axis=0)
        qd = q_ref[s, rs, :] * jnp.exp(b)
        kk = k_ref[s, rs, :]
        ki = kk * jnp.exp(-b)
        kd = kk * jnp.exp(b_last_rows - b)
        for p in range(GLA_HEADS // 2):
            ls = slice(p * LANES, (p + 1) * LANES)
            qd_p = qd[:, ls]
            qm = (jnp.where(lo, qd_p, 0.0).astype(BF16), jnp.where(lo, 0.0, qd_p).astype(BF16))
            s2 = _dot_nt(jnp.concatenate(qm, axis=0), ki[:, ls].astype(BF16))
            s2 = jnp.where(causal2, s2, 0.0).astype(BF16)
            kd_p = kd[:, ls].astype(BF16)
            heads = (2 * p, 2 * p + 1)
            v = [v_ref[s, rs, h * GLA_DV:(h + 1) * GLA_DV] for h in heads]
            sp = s * n_pairs + p

            def by_chunk(a):
                return jnp.concatenate(
                    [jnp.where(row_chunk == g, a, jnp.zeros_like(a)) for g in range(GLA_GROUP)],
                    axis=1)

            kd_blk = by_chunk(kd_p)
            new = [_dot_tn(v[hh], kd_blk) for hh in range(2)]
            entering = []
            for g in range(GLA_GROUP):
                gl = slice(g * LANES, (g + 1) * LANES)
                entering.append(states[sp])
                states[sp] = (jnp.exp(b_last[g][:, ls]) * states[sp]
                              + jnp.where(lane_s, new[0][:, gl], new[1][:, gl]))
            st_stack = jnp.concatenate(entering, axis=1).astype(BF16)
            for hh in range(2):
                hs = slice(heads[hh] * GLA_DV, (heads[hh] + 1) * GLA_DV)
                o = (_dot(s2[hh * sc:(hh + 1) * sc, :], v[hh])
                     + _dot_nt(by_chunk(qm[hh]), st_stack))
                o = _rms_rows(o) * g_row
                gate = _silu(gg_ref[s, rs, hs].astype(F32))
                o_ref[s, rs, hs] = (o * gate).astype(BF16)
    for n in range(n_seq * n_pairs):
        st_ref[n] = states[n]


def _gla(z, f, norm_g, bsz, seq):
    t = z.shape[0]
    tc = min(GLA_ROWS, seq)
    nc = seq // tc
    n_seq = math.gcd(GLA_SEQS_PER_STEP, bsz)
    f3 = f.reshape(bsz, seq, f.shape[1])
    z3 = z.reshape(bsz, seq, z.shape[1])

    def spec(width, col):
        return pl.BlockSpec((n_seq, tc, width), lambda b, i: (b, i, col))

    out = pl.pallas_call(
        _gla_kernel,
        grid=(bsz // n_seq, nc),
        in_specs=[spec(GLA_QK_W, 0), spec(GLA_QK_W, 1), spec(GLA_QK_W, 2),
                  spec(GLA_V_W, 0), spec(GLA_V_W, 1),
                  pl.BlockSpec((1, GLA_DV), lambda b, i: (0, 0))],
        out_specs=spec(GLA_V_W, 0),
        out_shape=jax.ShapeDtypeStruct((bsz, seq, GLA_V_W), BF16),
        scratch_shapes=[pltpu.VMEM((n_seq * (GLA_HEADS // 2), LANES, LANES), F32)],
        compiler_params=_cp(("arbitrary", "arbitrary"), 40),
        name="gla",
    )(f3, f3, f3, z3, z3, norm_g.reshape(1, GLA_DV))
    return out.reshape(t, GLA_V_W)


def _diff_attn_kernel(lam_init, q_ref, k_ref, v_ref, lq1_ref, lk1_ref, lq2_ref, lk2_ref, g_ref,
                      o_ref, vt_ref, s_ref, m_ref, l_ref, acc_ref):
    tq = q_ref.shape[0]
    seq = k_ref.shape[0]
    n_heads = q_ref.shape[1] // LANES
    qi = pl.program_id(2)

    @pl.when(qi == 0)
    def _():
        for hh in range(n_heads):
            for cb in range(seq // tq):
                blk = v_ref[cb * tq:(cb + 1) * tq, hh * LANES:(hh + 1) * LANES]
                vt_ref[hh, :, cb * tq:(cb + 1) * tq] = blk.astype(F32).T.astype(BF16)

    lo = lax.broadcasted_iota(jnp.int32, (tq, LANES), 1) < DIFF_D
    qqs = []
    for hh in range(n_heads):
        q = q_ref[:, hh * LANES:(hh + 1) * LANES]
        zero = jnp.zeros_like(q)
        qqs.append(jnp.concatenate([jnp.where(lo, q, zero), jnp.where(lo, zero, q)], axis=0))

    m_ref[...] = jnp.full(m_ref.shape, -jnp.inf, F32)
    l_ref[...] = jnp.zeros_like(l_ref)
    acc_ref[...] = jnp.zeros_like(acc_ref)

    def scores(j, slot):
        start = pl.multiple_of(j * tq, tq)
        for hh in range(n_heads):
            kj = k_ref[pl.ds(start, tq), hh * LANES:(hh + 1) * LANES]
            s_ref[slot, hh] = _dot_nt(kj, qqs[hh])

    def update(j, slot, masked):
        start = pl.multiple_of(j * tq, tq)
        for hh in range(n_heads):
            vtj = vt_ref[hh, :, pl.ds(start, tq)]
            alpha, p = [], []
            for c in range(2 * tq // LANES):
                cs = slice(c * LANES, (c + 1) * LANES)
                s_c = s_ref[slot, hh, :, cs]
                if masked:
                    kv = lax.broadcasted_iota(jnp.int32, s_c.shape, 0)
                    q_pos = lax.broadcasted_iota(jnp.int32, s_c.shape, 1) + (c * LANES) % tq
                    s_c = jnp.where(kv <= q_pos, s_c, -jnp.inf)
                m_old = m_ref[hh, :, cs]
                m_c = jnp.maximum(m_old, jnp.max(s_c, axis=0, keepdims=True))
                a_c = jnp.exp2(m_old - m_c)
                p_c = jnp.exp2(s_c - m_c)
                m_ref[hh, :, cs] = m_c
                l_ref[hh, :, cs] = a_c * l_ref[hh, :, cs] + jnp.sum(p_c, axis=0, keepdims=True)
                alpha.append(a_c)
                p.append(p_c.astype(BF16))
            acc_ref[hh] = (jnp.concatenate(alpha, axis=1) * acc_ref[hh]
                           + _dot(vtj, jnp.concatenate(p, axis=1)))

    scores(0, 0)

    def body(i, carry):
        j = 2 * i
        scores(j + 1, 1)
        update(j, 0, False)
        scores(j + 2, 0)
        update(j + 1, 1, False)
        return carry

    n_pairs = qi // 2
    lax.fori_loop(0, n_pairs, body, 0)
    j0 = 2 * n_pairs

    @pl.when(qi % 2 == 1)
    def _():
        scores(j0 + 1, 1)
        update(j0, 0, False)
        update(j0 + 1, 1, True)

    @pl.when(qi % 2 == 0)
    def _():
        update(j0, 0, True)

    lam = (jnp.exp(jnp.sum(lq1_ref[...] * lk1_ref[...], axis=-1, keepdims=True))
           - jnp.exp(jnp.sum(lq2_ref[...] * lk2_ref[...], axis=-1, keepdims=True)) + lam_init)
    for hh in range(n_heads):
        l = l_ref[hh]
        acc = acc_ref[hh]
        o12 = acc / l
        o = o12[:, :tq] - lam * o12[:, tq:]
        o = o * lax.rsqrt(jnp.mean(o * o, axis=0, keepdims=True) + EPS)
        o = o * g_ref[...] * (1.0 - lam_init)
        o_ref[:, hh * DIFF_DV:(hh + 1) * DIFF_DV] = o.T.astype(BF16)


def _diff_attn(z, lq1, lk1, lq2, lk2, norm_g, lam_init, bsz, seq):
    t = z.shape[0]
    tq = min(ATT_Q_ROWS, seq)
    nq = seq // tq
    hw = ATT_HEADS_PER_STEP * LANES
    qb, kb, vb = 1024 // hw, 1536 // hw, 2048 // hw
    small = pl.BlockSpec((1, DIFF_D), lambda b, h, i: (0, 0))
    return pl.pallas_call(
        functools.partial(_diff_attn_kernel, lam_init),
        grid=(bsz, DIFF_HEADS // ATT_HEADS_PER_STEP, nq),
        in_specs=[
            pl.BlockSpec((tq, hw), lambda b, h, i: (b * nq + i, qb + h)),
            pl.BlockSpec((seq, hw), lambda b, h, i: (b, kb + h)),
            pl.BlockSpec((seq, hw), lambda b, h, i: (b, vb + h)),
            small, small, small, small,
            pl.BlockSpec((DIFF_DV, 1), lambda b, h, i: (0, 0)),
        ],
        out_specs=pl.BlockSpec((tq, hw), lambda b, h, i: (b * nq + i, h)),
        out_shape=jax.ShapeDtypeStruct((t, DIFF_V_W), BF16),
        scratch_shapes=[pltpu.VMEM((ATT_HEADS_PER_STEP, DIFF_DV, seq), BF16),
                        pltpu.VMEM((2, ATT_HEADS_PER_STEP, tq, 2 * tq), F32),
                        pltpu.VMEM((ATT_HEADS_PER_STEP, 1, 2 * tq), F32),
                        pltpu.VMEM((ATT_HEADS_PER_STEP, 1, 2 * tq), F32),
                        pltpu.VMEM((ATT_HEADS_PER_STEP, DIFF_DV, 2 * tq), F32)],
        compiler_params=_cp(("arbitrary", "arbitrary", "arbitrary"), 32),
        name="diff_attn",
    )(z, z, z, lq1.reshape(1, DIFF_D), lk1.reshape(1, DIFF_D), lq2.reshape(1, DIFF_D),
      lk2.reshape(1, DIFF_D), norm_g.reshape(DIFF_DV, 1))


def _sgu_ret_kernel(su_ref, sv_ref, rv_ref, rg_ref, q_ref, k_ref, ws_ref, bs_ref, o_ref, st_ref):
    @pl.when(pl.program_id(1) == 0)
    def _():
        st_ref[...] = jnp.zeros_like(st_ref)

    c = RET_CHUNK
    row = lax.broadcasted_iota(jnp.int32, (c, c), 0)
    col = lax.broadcasted_iota(jnp.int32, (c, c), 1)
    causal = row >= col
    log_g = [math.log(1.0 - 2.0 ** (-5.0 - h)) for h in range(RET_HEADS)]
    lo = col < RET_DK
    rel = (row - col).astype(F32)
    pos = row.astype(F32)
    w_sgu = [jnp.where(causal, ws_ref[g], 0.0).astype(BF16) for g in range(SGU_GROUPS)]
    decays = [jnp.where(causal, jnp.exp(log_g[h] * jnp.maximum(rel, 0.0)), 0.0)
              for h in range(RET_HEADS)]
    lgs = [jnp.where(lo, log_g[2 * p], log_g[2 * p + 1]) for p in range(RET_HEADS // 2)]
    q_decs = [jnp.exp(lg * (pos + 1.0)) for lg in lgs]
    k_decs = [jnp.exp(lg * (c - 1.0 - pos)) for lg in lgs]
    states = [st_ref[p] for p in range(RET_HEADS // 2)]
    for j in range(su_ref.shape[0] // c):
        rs = slice(j * c, (j + 1) * c)
        for g in range(SGU_GROUPS):
            gs = slice(g * SGU_CH, (g + 1) * SGU_CH)
            s = _dot(w_sgu[g], sv_ref[rs, gs]) + bs_ref[g]
            o_ref[rs, gs] = (su_ref[rs, gs].astype(F32) * s).astype(BF16)
        for p in range(RET_HEADS // 2):
            ls = slice(p * LANES, (p + 1) * LANES)
            q_p = q_ref[rs, ls]
            k_p = k_ref[rs, ls]
            qm = (jnp.where(lo, q_p, 0.0).astype(BF16), jnp.where(lo, 0.0, q_p).astype(BF16))
            s2 = _dot_nt(jnp.concatenate(qm, axis=0), k_p.astype(BF16))
            qd = q_p * q_decs[p]
            qdm = (jnp.where(lo, qd, 0.0).astype(BF16), jnp.where(lo, 0.0, qd).astype(BF16))
            kd = (k_p * k_decs[p]).astype(BF16)
            st_b = states[p].astype(BF16)
            new = []
            for hh in range(2):
                h = 2 * p + hh
                hs = slice(h * RET_DV, (h + 1) * RET_DV)
                s_h = (s2[hh * c:(hh + 1) * c, :] * decays[h]).astype(BF16)
                v_h = rv_ref[rs, hs]
                o = _dot(s_h, v_h) + _dot_nt(qdm[hh], st_b)
                gate = _silu(rg_ref[rs, hs].astype(F32))
                o_ref[rs, SGU_W + h * RET_DV:SGU_W + (h + 1) * RET_DV] = (
                    _rms_rows(o) * gate).astype(BF16)
                new.append(_dot_tn(v_h, kd))
            states[p] = jnp.exp(lgs[p] * float(c)) * states[p] + jnp.where(lo, new[0], new[1])
    for p in range(RET_HEADS // 2):
        st_ref[p] = states[p]


def _sgu_ret(z, f, w_s, b_s, bsz, seq):
    t = z.shape[0]
    cc = RET_CHUNK
    c = min(SGU_RET_ROWS, seq)
    nc = seq // c
    return pl.pallas_call(
        _sgu_ret_kernel,
        grid=(bsz, nc),
        in_specs=[
            pl.BlockSpec((c, SGU_W), lambda b, i: (b * nc + i, 0)),
            pl.BlockSpec((c, SGU_W), lambda b, i: (b * nc + i, 1)),
            pl.BlockSpec((c, RET_V_W), lambda b, i: (b * nc + i, 2)),
            pl.BlockSpec((c, RET_V_W), lambda b, i: (b * nc + i, 3)),
            pl.BlockSpec((c, RET_QK_W), lambda b, i: (b * nc + i, 0)),
            pl.BlockSpec((c, RET_QK_W), lambda b, i: (b * nc + i, 1)),
            pl.BlockSpec((SGU_GROUPS, cc, cc), lambda b, i: (0, 0, 0)),
            pl.BlockSpec((SGU_GROUPS, cc, 1), lambda b, i: (0, 0, 0)),
        ],
        out_specs=pl.BlockSpec((c, SGU_W + RET_V_W), lambda b, i: (b * nc + i, 0)),
        out_shape=jax.ShapeDtypeStruct((t, SGU_W + RET_V_W), BF16),
        scratch_shapes=[pltpu.VMEM((2, LANES, LANES), F32)],
        compiler_params=_cp(("arbitrary", "arbitrary"), 32),
        name="sgu_retention",
    )(z, z, z, z, f, f, w_s, b_s.reshape(SGU_GROUPS, cc, 1))


def _out_proj_kernel(n_in, *refs):
    o_refs = refs[:n_in]
    w_ref, x_ref, mod_ref, rw_ref, rb_ref, xn_ref, h_ref, lg_ref = refs[n_in:]
    k_each = D_MODEL // n_in
    y = _dot(o_refs[0][...], w_ref[0:k_each, :])
    for n in range(1, n_in):
        y = y + _dot(o_refs[n][...], w_ref[n * k_each:(n + 1) * k_each, :])
    xn = x_ref[...] + mod_ref[0, 2:3, :] * y
    xn_ref[...] = xn
    h = _rms_rows(xn) * (1.0 + mod_ref[0, 4:5, :]) + mod_ref[0, 3:4, :]
    hb = h.astype(BF16)
    h_ref[...] = hb
    lg_ref[...] = _dot_nt(rw_ref[...], hb) + rb_ref[...]


def _out_proj(mixed, w_out, x2, mod, router_w, router_b, seq):
    t, d = x2.shape
    tm = min(PROJ_ROWS, seq)
    n_in = len(mixed)
    k_each = d // n_in
    rw = router_w.T.astype(BF16)
    rb = router_b.reshape(N_EXPERTS, 1)
    return pl.pallas_call(
        functools.partial(_out_proj_kernel, n_in),
        grid=(t // tm,),
        in_specs=[pl.BlockSpec((tm, k_each), lambda i: (i, 0)) for _ in mixed] + [
            pl.BlockSpec((d, d), lambda i: (0, 0)),
            pl.BlockSpec((tm, d), lambda i: (i, 0)),
            pl.BlockSpec((1, 6, d), lambda i: (i // (seq // tm), 0, 0)),
            pl.BlockSpec((N_EXPERTS, d), lambda i: (0, 0)),
            pl.BlockSpec((N_EXPERTS, 1), lambda i: (0, 0)),
        ],
        out_specs=[pl.BlockSpec((tm, d), lambda i: (i, 0)),
                   pl.BlockSpec((tm, d), lambda i: (i, 0)),
                   pl.BlockSpec((N_EXPERTS, tm), lambda i: (0, i))],
        out_shape=[jax.ShapeDtypeStruct((t, d), F32),
                   jax.ShapeDtypeStruct((t, d), BF16),
                   jax.ShapeDtypeStruct((N_EXPERTS, t), F32)],
        compiler_params=_cp(("arbitrary",), 48),
        name="out_proj",
    )(*mixed, w_out.astype(BF16), x2, mod, rw, rb)


def _route_kernel(lg_ref, idx_ref, gate_ref, tcnt_ref, toff_ref, trun_ref, cnt_ref, run_ref):
    @pl.when(pl.program_id(0) == 0)
    def _():
        run_ref[...] = jnp.zeros_like(run_ref)

    tm = lg_ref.shape[1]
    row = lax.broadcasted_iota(jnp.int32, (N_EXPERTS, tm), 0)
    neg = -jnp.inf
    l = lg_ref[...]
    vals, firsts, hots = [], [], []
    for _ in range(TOP_K):
        m = jnp.max(l, axis=0, keepdims=True)
        first = jnp.min(jnp.where(l == m, row, N_EXPERTS), axis=0, keepdims=True)
        hot = row == first
        vals.append(m)
        firsts.append(first)
        hots.append(hot)
        l = jnp.where(hot, neg, l)
    sel = hots[0] | hots[1] | hots[2] | hots[3]
    ex = [jnp.exp(v - vals[0]) for v in vals]
    denom = ex[0] + ex[1] + ex[2] + ex[3]
    r_i = lax.broadcasted_iota(jnp.int32, (tm, tm), 0)
    c_i = lax.broadcasted_iota(jnp.int32, (tm, tm), 1)
    before = (r_i < c_i).astype(BF16)
    earlier = _dot(sel.astype(BF16), before)
    tile_cnt = jnp.sum(sel.astype(F32), axis=1, keepdims=True)
    lower = jnp.zeros((N_EXPERTS, tm), F32)
    for k in range(TOP_K):
        lower = lower + (firsts[k] < row).astype(F32)
    tile_off = jnp.sum(lower, axis=1, keepdims=True)
    slot = tile_off + earlier
    row8 = lax.broadcasted_iota(jnp.int32, (2 * TOP_K, tm), 0)
    idx_out = jnp.zeros((2 * TOP_K, tm), jnp.int32)
    gate_out = jnp.zeros((2 * TOP_K, tm), F32)
    for k in range(TOP_K):
        pos_k = jnp.sum(jnp.where(hots[k], slot, 0.0), axis=0, keepdims=True).astype(jnp.int32)
        idx_out = jnp.where(row8 == k, firsts[k], idx_out)
        idx_out = jnp.where(row8 == TOP_K + k, pos_k, idx_out)
        gate_out = jnp.where(row8 == k, ex[k] / denom, gate_out)
    idx_ref[...] = idx_out
    gate_ref[...] = gate_out
    run = run_ref[:, 0:1]
    tcnt_ref[...] = jnp.broadcast_to(tile_cnt, tcnt_ref.shape)
    toff_ref[...] = jnp.broadcast_to(tile_off, toff_ref.shape)
    trun_ref[...] = jnp.broadcast_to(run, trun_ref.shape)
    total = run + tile_cnt
    run_ref[...] = jnp.broadcast_to(total, run_ref.shape)
    cnt_ref[...] = jnp.broadcast_to(total, cnt_ref.shape)


def _route(logits_t):
    t = logits_t.shape[1]
    tm = min(MOE_TILE, t)
    nt = t // tm
    per_tile = pl.BlockSpec((N_EXPERTS, LANES), lambda i: (i, 0))
    per_tile_shape = jax.ShapeDtypeStruct((nt * N_EXPERTS, LANES), F32)
    return pl.pallas_call(
        _route_kernel,
        grid=(nt,),
        in_specs=[pl.BlockSpec((N_EXPERTS, tm), lambda i: (0, i))],
        out_specs=[pl.BlockSpec((2 * TOP_K, tm), lambda i: (0, i)),
                   pl.BlockSpec((2 * TOP_K, tm), lambda i: (0, i)),
                   per_tile, per_tile, per_tile,
                   pl.BlockSpec((N_EXPERTS, LANES), lambda i: (0, 0))],
        out_shape=[jax.ShapeDtypeStruct((2 * TOP_K, t), jnp.int32),
                   jax.ShapeDtypeStruct((2 * TOP_K, t), F32),
                   per_tile_shape, per_tile_shape, per_tile_shape,
                   jax.ShapeDtypeStruct((N_EXPERTS, LANES), F32)],
        scratch_shapes=[pltpu.VMEM((N_EXPERTS, LANES), F32)],
        compiler_params=_cp(("arbitrary",), 32),
        name="route",
    )(logits_t)


def _rows(ref, start, n):
    return ref.at[pl.ds(pl.multiple_of(start * ROW_TILES, ROW_TILES), n * ROW_TILES), :]


def _run_copies(tile, cnt_ref, off_ref, dst_ref, make_copy):
    def per_expert(e, carry):
        j = tile * N_EXPERTS + e
        cnt, off, dst = cnt_ref[j], off_ref[j], dst_ref[j]

        for b in reversed(range(RUN_BITS)):
            size = 1 << b

            @pl.when((cnt & size) != 0)
            def _():
                done = lax.shift_left(lax.shift_right_logical(cnt, b + 1), b + 1)
                make_copy(off + done, dst + done, size).start()
        return carry

    lax.fori_loop(0, N_EXPERTS, per_expert, 0)


def _dispatch_kernel(fs_ref, fl_ref, cnt_ref, off_ref, dst_ref, h_ref, idx_ref, xs_ref,
                     stage, zero_buf, sems, zsem):
    i = pl.program_id(0)
    n_steps = pl.num_programs(0)
    cur = lax.rem(i, 2)
    tm = h_ref.shape[0]
    n_sorted = TOP_K * tm
    n_fill = fs_ref.shape[0]

    def fill_copies(f, wait):
        start, n = fs_ref[f], fl_ref[f]
        n_chunks = lax.shift_right_logical(n, FILL_SHIFT)
        tail = start + n_chunks * FILL_ROWS

        def chunk(j, carry):
            cp = pltpu.make_async_copy(zero_buf, _rows(xs_ref, start + j * FILL_ROWS, FILL_ROWS),
                                       zsem)
            cp.wait() if wait else cp.start()
            return carry

        def single(j, carry):
            cp = pltpu.make_async_copy(_rows(zero_buf, 0, 1), _rows(xs_ref, tail + j, 1), zsem)
            cp.wait() if wait else cp.start()
            return carry

        lax.fori_loop(0, n_chunks, chunk, 0)
        lax.fori_loop(0, n - n_chunks * FILL_ROWS, single, 0)

    @pl.when(i == 0)
    def _():
        zero_buf[...] = jnp.zeros_like(zero_buf)
        lax.fori_loop(0, n_fill, lambda f, c: (fill_copies(f, False), c)[1], 0)

    h = h_ref[...]
    pos = [idx_ref[TOP_K + k:TOP_K + k + 1, :] for k in range(TOP_K)]
    for sb in range(n_sorted // SORT_SLAB):
        j = lax.broadcasted_iota(jnp.int32, (SORT_SLAB, tm), 0) + sb * SORT_SLAB
        perm = jnp.zeros((SORT_SLAB, tm), F32)
        for k in range(TOP_K):
            perm = jnp.where(pos[k] == j, 1.0, perm)
        perm = perm.astype(BF16)
        rows = _dot(perm, h)
        for c in range(ROW_TILES):
            stage[cur, pl.ds(sb * SORT_SLAB * ROW_TILES + c, SORT_SLAB, stride=ROW_TILES), :] = (
                rows[:, c * LANES:(c + 1) * LANES])

    def make_copy(src_row, dst_row, size):
        return pltpu.make_async_copy(_rows(stage.at[cur], src_row, size),
                                     _rows(xs_ref, dst_row, size), sems.at[cur])

    _run_copies(i, cnt_ref, off_ref, dst_ref, make_copy)

    def wait_slot(slot):
        pltpu.make_async_copy(stage.at[slot], _rows(xs_ref, 0, n_sorted), sems.at[slot]).wait()

    @pl.when(i > 0)
    def _():
        wait_slot(1 - cur)

    @pl.when(i == n_steps - 1)
    def _():
        wait_slot(cur)

    @pl.when(i == 0)
    def _():
        lax.fori_loop(0, n_fill, lambda f, c: (fill_copies(f, True), c)[1], 0)


def _dispatch(h2, idx, tile_cnt, tile_off, tile_dst, fill_start, fill_len, n_rows):
    t, d = h2.shape
    tm = min(MOE_TILE, t)
    grid_spec = pltpu.PrefetchScalarGridSpec(
        num_scalar_prefetch=5,
        grid=(t // tm,),
        in_specs=[pl.BlockSpec((tm, d), lambda i, *_: (i, 0)),
                  pl.BlockSpec((2 * TOP_K, tm), lambda i, *_: (0, i))],
        out_specs=pl.BlockSpec(memory_space=pl.ANY),
        scratch_shapes=[pltpu.VMEM((2, TOP_K * tm * ROW_TILES, LANES), F32),
                        pltpu.VMEM((FILL_ROWS * ROW_TILES, LANES), F32),
                        pltpu.SemaphoreType.DMA((2,)), pltpu.SemaphoreType.DMA(())],
    )
    return pl.pallas_call(
        _dispatch_kernel,
        grid_spec=grid_spec,
        out_shape=jax.ShapeDtypeStruct((n_rows * ROW_TILES, LANES), F32),
        compiler_params=_cp(("arbitrary",), 48),
        name="dispatch",
    )(fill_start, fill_len, tile_cnt, tile_off, tile_dst, h2, idx)


def _expert_kernel(layer, be_ref, nu_ref, nx_ref, nv_ref, xs_ref, wi_hbm, bi_ref, wo_hbm, bo_ref,
                   y_ref, wi_st, wo_st, wi_b, wo_b, sems):
    tb = EXPERT_ROWS

    def fetch(ex):
        return (pltpu.make_async_copy(wi_hbm.at[layer, ex], wi_st, sems.at[0]),
                pltpu.make_async_copy(wo_hbm.at[layer, ex], wo_st, sems.at[1]))

    @pl.when(pl.program_id(0) == 0)
    def _():
        for cp in fetch(be_ref[0]):
            cp.start()

    def block(i, r0):
        e = be_ref[i]
        fresh = jnp.logical_or(i == 0, e != be_ref[jnp.maximum(i - 1, 0)])
        used = i < nu_ref[0]

        def take_weights():
            for cp in fetch(e):
                cp.wait()
            wi_b[...] = wi_st[...].astype(BF16)
            wo_b[...] = wo_st[...].astype(BF16)

        def fetch_next():
            @pl.when(nx_ref[i] >= 0)
            def _():
                for cp in fetch(nx_ref[i]):
                    cp.start()

        def compute(n):
            x = jnp.concatenate(
                [xs_ref[pl.ds(r0 * ROW_TILES + c, n, stride=ROW_TILES), :]
                 for c in range(ROW_TILES)], axis=1).astype(BF16)
            glu = _dot(x, wi_b[:, 0:D_FF]) + bi_ref[0, e, :, 0:D_FF]
            lin = _dot(x, wi_b[:, D_FF:2 * D_FF]) + bi_ref[0, e, :, D_FF:2 * D_FF]
            glu = jnp.minimum(glu, SWIGLU_LIMIT)
            lin = jnp.clip(lin, -SWIGLU_LIMIT, SWIGLU_LIMIT)
            act = glu * jax.nn.sigmoid(SWIGLU_ALPHA * glu) * (lin + 1.0)
            y = _dot(act.astype(BF16), wo_b[...]) + bo_ref[0, e]
            for c in range(ROW_TILES):
                y_ref[pl.ds(r0 * ROW_TILES + c, n, stride=ROW_TILES), :] = (
                    y[:, c * LANES:(c + 1) * LANES])

        def zero_rows(start, n):
            y_ref[pl.ds((r0 + start) * ROW_TILES, n * ROW_TILES), :] = jnp.zeros(
                (n * ROW_TILES, LANES), F32)

        full = nv_ref[i] > tb // 2

        @pl.when(jnp.logical_and(used, jnp.logical_and(full, fresh)))
        def _():
            take_weights()
            compute(tb)
            fetch_next()

        @pl.when(jnp.logical_and(used, jnp.logical_and(full, jnp.logical_not(fresh))))
        def _():
            compute(tb)

        @pl.when(jnp.logical_and(used, jnp.logical_not(full)))
        def _():
            @pl.when(fresh)
            def _():
                take_weights()
                fetch_next()

            compute(tb // 2)
            zero_rows(tb // 2, tb // 2)

        @pl.when(jnp.logical_not(used))
        def _():
            zero_rows(0, tb)

    for sub in range(EXPERT_BLOCKS_PER_STEP):
        block(pl.program_id(0) * EXPERT_BLOCKS_PER_STEP + sub, sub * tb)


def _experts(xs, block_e, n_used, next_e, n_valid, layer, w_in, b_in, w_out, b_out):
    per_step = EXPERT_BLOCKS_PER_STEP
    win = per_step * EXPERT_ROWS * ROW_TILES
    n_steps = xs.shape[0] // win
    assert n_steps * win == xs.shape[0]
    depth, ne, d, f2 = w_in.shape

    def row_map(i, be, nu, nx, nv):
        return (jnp.minimum(i, (nu[0] + per_step - 1) // per_step - 1), 0)

    grid_spec = pltpu.PrefetchScalarGridSpec(
        num_scalar_prefetch=4,
        grid=(n_steps,),
        in_specs=[
            pl.BlockSpec((win, LANES), row_map),
            pl.BlockSpec(memory_space=pl.ANY),
            pl.BlockSpec((1, ne, 1, f2), lambda i, be, nu, nx, nv: (layer, 0, 0, 0)),
            pl.BlockSpec(memory_space=pl.ANY),
            pl.BlockSpec((1, ne, 1, d), lambda i, be, nu, nx, nv: (layer, 0, 0, 0)),
        ],
        out_specs=pl.BlockSpec((win, LANES), lambda i, be, nu, nx, nv: (i, 0)),
        scratch_shapes=[pltpu.VMEM((d, f2), F32), pltpu.VMEM((D_FF, d), F32),
                        pltpu.VMEM((d, f2), BF16), pltpu.VMEM((D_FF, d), BF16),
                        pltpu.SemaphoreType.DMA((2,))],
    )
    return pl.pallas_call(
        functools.partial(_expert_kernel, layer),
        grid_spec=grid_spec,
        out_shape=jax.ShapeDtypeStruct(xs.shape, F32),
        compiler_params=_cp(("arbitrary",), 56),
        name="experts",
    )(block_e, n_used, next_e, n_valid, xs, w_in, b_in.reshape(depth, ne, 1, f2), w_out,
      b_out.reshape(depth, ne, 1, d))


def _combine_kernel(final, cnt_ref, off_ref, dst_ref, pos_ref, gate_ref, x_ref, mod_ref, fg_ref,
                    yb_ref, o_ref, stage, sems):
    tm = x_ref.shape[0]
    n_sorted = TOP_K * tm
    i = pl.program_id(0)
    cur = lax.rem(i, 2)

    def fetch(tile, slot):
        def make_copy(sorted_row, src_row, size):
            return pltpu.make_async_copy(_rows(yb_ref, src_row, size),
                                         _rows(stage.at[slot], sorted_row, size), sems.at[slot])

        _run_copies(tile, cnt_ref, off_ref, dst_ref, make_copy)

    @pl.when(i == 0)
    def _():
        fetch(0, 0)

    @pl.when(i + 1 < pl.num_programs(0))
    def _():
        fetch(i + 1, 1 - cur)

    pltpu.make_async_copy(_rows(yb_ref, 0, n_sorted), stage.at[cur], sems.at[cur]).wait()

    pos = [pos_ref[:, TOP_K + k:TOP_K + k + 1] for k in range(TOP_K)]
    gate = [gate_ref[:, k:k + 1] for k in range(TOP_K)]
    y = jnp.zeros((tm, D_MODEL), F32)
    for sb in range(n_sorted // SORT_SLAB):
        rows = jnp.concatenate(
            [stage[cur, pl.ds(sb * SORT_SLAB * ROW_TILES + c, SORT_SLAB, stride=ROW_TILES), :]
             for c in range(ROW_TILES)], axis=1).astype(BF16)
        j = lax.broadcasted_iota(jnp.int32, (tm, SORT_SLAB), 1) + sb * SORT_SLAB
        w = jnp.zeros((tm, SORT_SLAB), F32)
        for k in range(TOP_K):
            w = jnp.where(pos[k] == j, gate[k], w)
        y = y + _dot(w.astype(BF16), rows)
    o = x_ref[...] + mod_ref[0, 5:6, :] * y
    if final:
        o = _rms_rows(o) * fg_ref[...]
    o_ref[...] = o


def _combine(yb, tile_cnt, tile_off, tile_dst, pos_t, gates_t, x2, mod, final_g, final, seq):
    t, d = x2.shape
    tm = min(MOE_TILE, seq)
    grid_spec = pltpu.PrefetchScalarGridSpec(
        num_scalar_prefetch=3,
        grid=(t // tm,),
        in_specs=[
            pl.BlockSpec((tm, 2 * TOP_K), lambda i, *_: (i, 0)),
            pl.BlockSpec((tm, 2 * TOP_K), lambda i, *_: (i, 0)),
            pl.BlockSpec((tm, d), lambda i, *_: (i, 0)),
            pl.BlockSpec((1, 6, d), lambda i, *_: (i // (seq // tm), 0, 0)),
            pl.BlockSpec((1, d), lambda i, *_: (0, 0)),
            pl.BlockSpec(memory_space=pl.ANY),
        ],
        out_specs=pl.BlockSpec((tm, d), lambda i, *_: (i, 0)),
        scratch_shapes=[pltpu.VMEM((2, TOP_K * tm * ROW_TILES, LANES), F32),
                        pltpu.SemaphoreType.DMA((2,))],
    )
    return pl.pallas_call(
        functools.partial(_combine_kernel, final),
        grid_spec=grid_spec,
        out_shape=jax.ShapeDtypeStruct((t, d), F32),
        compiler_params=_cp(("arbitrary",), 48),
        name="combine",
    )(tile_cnt, tile_off, tile_dst, pos_t, gates_t, x2, mod, final_g.reshape(1, d), yb)


def _moe(h2, logits, x2, mod, layer, w_in, b_in, w_out, b_out, final_g, final, seq):
    t = x2.shape[0]
    tb = EXPERT_ROWS
    idx, gates, tcnt, toff, trun, cnt = _route(logits)
    counts = cnt[:, 0].astype(jnp.int32)
    nblk = (counts + tb - 1) // tb
    blk_end = jnp.cumsum(nblk)
    pad_start = (blk_end - nblk) * tb
    n_blocks = (t * TOP_K) // tb + N_EXPERTS
    n_used = blk_end[-1:]
    experts = jnp.arange(N_EXPERTS, dtype=jnp.int32)
    last_e = jnp.max(jnp.where(nblk > 0, experts, 0))
    blk = jnp.arange(n_blocks, dtype=jnp.int32)
    block_e = jnp.minimum(
        jnp.sum((blk_end[None, :] <= blk[:, None]).astype(jnp.int32), axis=1), last_e)
    later = (experts[None, :] > block_e[:, None]) & (nblk[None, :] > 0)
    next_e = jnp.min(jnp.where(later, experts[None, :], N_EXPERTS), axis=1)
    next_e = jnp.where(next_e == N_EXPERTS, -1, next_e).astype(jnp.int32)
    tile_cnt = tcnt[:, 0].astype(jnp.int32)
    tile_off = toff[:, 0].astype(jnp.int32)
    tile_dst = (trun[:, 0].astype(jnp.int32).reshape(-1, N_EXPERTS) + pad_start[None, :]).reshape(-1)
    fill_start = jnp.concatenate([pad_start + counts, blk_end[-1:] * tb]).astype(jnp.int32)
    fill_len = jnp.concatenate([nblk * tb - counts, (n_blocks - blk_end[-1:]) * tb]).astype(jnp.int32)
    xs = _dispatch(h2, idx, tile_cnt, tile_off, tile_dst, fill_start, fill_len, n_blocks * tb)
    row_end = jnp.sum(jnp.where(block_e[:, None] == experts[None, :],
                                (pad_start + counts)[None, :], 0), axis=1)
    n_valid = jnp.clip(row_end - blk * tb, 0, tb).astype(jnp.int32)
    yb = _experts(xs, block_e, n_used.astype(jnp.int32), next_e, n_valid, layer,
                  w_in, b_in, w_out, b_out)
    return _combine(yb, tile_cnt, tile_off, tile_dst, idx.T, gates.T, x2, mod, final_g, final, seq)


def kernel(x, c, positions, w_ada, b_ada, even_w_in, gla_w_gate, gla_b_gate, gla_norm_g,
           diff_lam_q1, diff_lam_k1, diff_lam_q2, diff_lam_k2, diff_norm_g, even_w_out,
           odd_w_in, sgu_ln_g, sgu_ln_b, sgu_w, sgu_b, odd_w_out,
           router_w, router_b, expert_w_in, expert_b_in, expert_w_out, expert_b_out,
           final_norm_g):
    bsz, seq, d = x.shape
    depth = w_ada.shape[0]
    t = bsz * seq
    mods = _modulation(c, w_ada, b_ada).reshape(depth, bsz, 6, d)
    cos, sin = _rope_tables(positions)
    x2 = x.reshape(t, d)
    for layer in range(depth):
        mod = mods[layer]
        j = layer // 2
        if layer % 2 == 0:
            z, f = _proj_even(x2, mod, even_w_in[j], gla_w_gate[j], gla_b_gate[j], cos, sin, seq)
            o_gla = _gla(z, f, gla_norm_g[j], bsz, seq)
            lam_init = 0.8 - 0.6 * math.exp(-0.3 * layer)
            o_diff = _diff_attn(z, diff_lam_q1[j], diff_lam_k1[j], diff_lam_q2[j], diff_lam_k2[j],
                                diff_norm_g[j], lam_init, bsz, seq)
            mixed, w_out = (o_gla, o_diff), even_w_out[j]
        else:
            z, f = _proj_odd(x2, mod, odd_w_in[j], sgu_ln_g[j], sgu_ln_b[j], cos, sin, seq)
            mixed, w_out = (_sgu_ret(z, f, sgu_w[j], sgu_b[j], bsz, seq),), odd_w_out[j]
        x2, h2, logits = _out_proj(mixed, w_out, x2, mod, router_w[layer], router_b[layer], seq)
        x2 = _moe(h2, logits, x2, mod, layer, expert_w_in, expert_b_in, expert_w_out,
                  expert_b_out, final_norm_g, layer == depth - 1, seq)
    return x2.reshape(bsz, seq, d)
```

```python
import functools
import math

import jax
import jax.numpy as jnp
from jax import lax
from jax.experimental import pallas as pl
from jax.experimental.pallas import tpu as pltpu

F32 = jnp.float32
BF16 = jnp.bfloat16

D_MODEL = 1024
EPS = 1e-6
ROPE_THETA = 10000.0
ROPE_HALF = 32

GLA_HEADS = 4
GLA_DK = 64
GLA_DV = 128
GLA_RANK = 16
GLA_CHUNK = 64
GLA_GATE_NORMALIZER = 16.0
GLA_QK_W = GLA_HEADS * GLA_DK
GLA_V_W = GLA_HEADS * GLA_DV

DIFF_HEADS = 4
DIFF_D = 64
DIFF_DV = 128
DIFF_QK_W = DIFF_HEADS * 2 * DIFF_D
DIFF_V_W = DIFF_HEADS * DIFF_DV

SGU_GROUPS = 4
SGU_CH = 128
SGU_W = SGU_GROUPS * SGU_CH

RET_HEADS = 4
RET_DK = 64
RET_DV = 128
RET_CHUNK = 128
RET_QK_W = RET_HEADS * RET_DK
RET_V_W = RET_HEADS * RET_DV

N_EXPERTS = 32
TOP_K = 4
D_FF = D_MODEL
SWIGLU_ALPHA = 1.702
SWIGLU_LIMIT = 7.0

LANES = 128
ROW_TILES = D_MODEL // LANES

PROJ_ROWS = 1024
GLA_ROWS = 512
GLA_SEQS_PER_STEP = 2
GLA_GROUP = 4
SGU_RET_ROWS = 512
ATT_Q_ROWS = 256
ATT_HEADS_PER_STEP = 4
MOE_TILE = 512
SORT_SLAB = 512
RUN_BITS = 10
FILL_SHIFT = 6
FILL_ROWS = 1 << FILL_SHIFT
EXPERT_ROWS = 512
EXPERT_BLOCKS_PER_STEP = 2
MOD_COLS = 1536

MIB = 1024 * 1024


def _cp(semantics, vmem_mib):
    return pltpu.CompilerParams(dimension_semantics=semantics, vmem_limit_bytes=vmem_mib * MIB)


def _dot(a, b):
    return jnp.dot(a, b, preferred_element_type=F32)


def _dot_nt(a, b):
    return lax.dot_general(a, b, (((1,), (1,)), ((), ())), preferred_element_type=F32)


def _dot_tn(a, b):
    return lax.dot_general(a, b, (((0,), (0,)), ((), ())), preferred_element_type=F32)


def _rms_rows(x):
    return x * lax.rsqrt(jnp.mean(x * x, axis=-1, keepdims=True) + EPS)


def _silu(x):
    return x * jax.nn.sigmoid(x)


def _rope_chunk(v, cos, sin, lo_half):
    rot = jnp.where(lo_half, -pltpu.roll(v, 96, 1), pltpu.roll(v, 32, 1))
    return v * cos + rot * sin


def _mod_kernel(c_ref, w_ref, b_ref, o_ref):
    c = c_ref[...]
    ca = _silu(c).astype(BF16)
    o_ref[0] = _dot(ca, w_ref[0].astype(BF16)) + b_ref[0]


def _modulation(c, w_ada, b_ada):
    depth, d, n = w_ada.shape
    bsz = c.shape[0]
    return pl.pallas_call(
        _mod_kernel,
        grid=(depth, n // MOD_COLS),
        in_specs=[
            pl.BlockSpec((bsz, d), lambda l, j: (0, 0)),
            pl.BlockSpec((1, d, MOD_COLS), lambda l, j: (l, 0, j)),
            pl.BlockSpec((1, 1, MOD_COLS), lambda l, j: (l, 0, j)),
        ],
        out_specs=pl.BlockSpec((1, bsz, MOD_COLS), lambda l, j: (l, 0, j)),
        out_shape=jax.ShapeDtypeStruct((depth, bsz, n), F32),
        compiler_params=_cp(("arbitrary", "arbitrary"), 40),
        name="adaln_mod",
    )(c, w_ada, b_ada.reshape(depth, 1, n))


def _rope_table_kernel(p_ref, f_ref, c_ref, s_ref):
    ang = p_ref[...].astype(F32) * f_ref[...]
    c_ref[...] = jnp.cos(ang)
    s_ref[...] = jnp.sin(ang)


def _rope_tables(positions):
    t = positions.size
    per_row = LANES // ROPE_HALF
    rows = t // per_row
    pos_d = jnp.repeat(positions.reshape(rows, per_row), ROPE_HALF, axis=1)
    inv_freq = ROPE_THETA ** (-jnp.arange(ROPE_HALF, dtype=F32) / ROPE_HALF)
    freq_d = jnp.tile(inv_freq, per_row).reshape(1, LANES)
    tr = min(512, rows)
    cos_d, sin_d = pl.pallas_call(
        _rope_table_kernel,
        grid=(rows // tr,),
        in_specs=[pl.BlockSpec((tr, LANES), lambda i: (i, 0)),
                  pl.BlockSpec((1, LANES), lambda i: (0, 0))],
        out_specs=[pl.BlockSpec((tr, LANES), lambda i: (i, 0))] * 2,
        out_shape=[jax.ShapeDtypeStruct((rows, LANES), F32)] * 2,
        compiler_params=_cp(("arbitrary",), 32),
        name="rope_tables",
    )(pos_d, freq_d)
    cos = jnp.tile(cos_d.reshape(t, ROPE_HALF), (1, per_row))
    sin = jnp.tile(sin_d.reshape(t, ROPE_HALF), (1, per_row))
    return cos, sin


def _modulated_rms(x_ref, mod_ref, which):
    x = x_ref[...]
    shift = mod_ref[0, 3 * which:3 * which + 1, :]
    scale = mod_ref[0, 3 * which + 1:3 * which + 2, :]
    return _rms_rows(x) * (1.0 + scale) + shift


EVEN_Z = (GLA_V_W, GLA_V_W, DIFF_QK_W, DIFF_QK_W, DIFF_V_W)
EVEN_Z_W = sum(EVEN_Z)
EVEN_F_W = 3 * GLA_QK_W
EVEN_W_COLS = EVEN_Z_W + 2 * GLA_QK_W + LANES


def _proj_even_kernel(x_ref, mod_ref, w_ref, wg_ref, bg_ref, cos_ref, sin_ref, z_ref, f_ref):
    hb = _modulated_rms(x_ref, mod_ref, 0).astype(BF16)

    def sec(a, b):
        return _dot(hb, w_ref[:, a:b])

    for a, b in ((0, 512), (512, 1024), (2048, 2560)):
        z_ref[:, a:b] = sec(a, b).astype(BF16)
    cos = cos_ref[...]
    sin = sin_ref[...]
    lo_half = (lax.broadcasted_iota(jnp.int32, cos.shape, 1) % 64) < ROPE_HALF
    q_scale = DIFF_D ** -0.5 * math.log2(math.e)
    for a, scl in ((1024, q_scale), (1536, 1.0)):
        full = sec(a, a + DIFF_QK_W)
        for c in range(DIFF_QK_W // LANES):
            v = full[:, c * LANES:(c + 1) * LANES]
            z_ref[:, a + c * LANES:a + (c + 1) * LANES] = (
                _rope_chunk(v, cos, sin, lo_half) * scl).astype(BF16)
    f_ref[:, 0:256] = sec(2560, 2816) * (GLA_DK ** -0.5)
    f_ref[:, 256:512] = sec(2816, 3072)
    gr = sec(3072, 3200).astype(BF16)
    pre = _dot(gr, wg_ref[...]) + bg_ref[...]
    log_sig = jnp.minimum(pre, 0.0) - jnp.log1p(jnp.exp(-jnp.abs(pre)))
    f_ref[:, 512:768] = log_sig / GLA_GATE_NORMALIZER


def _proj_even(x2, mod, w_in, w_gate, b_gate, cos, sin, seq):
    t, d = x2.shape
    tm = min(PROJ_ROWS, seq)
    gq, gk, gv, gr, gg, dq, dk, dv = jnp.split(
        w_in, [256, 512, 1024, 1040, 1552, 2064, 2576], axis=1)
    gr_pad = jnp.pad(gr, ((0, 0), (0, LANES - GLA_RANK)))
    w_cat = jnp.concatenate([gv, gg, dq, dk, dv, gq, gk, gr_pad], axis=1).astype(BF16)
    wg_pad = jnp.pad(w_gate, ((0, LANES - GLA_RANK), (0, 0))).astype(BF16)
    return pl.pallas_call(
        _proj_even_kernel,
        grid=(t // tm,),
        in_specs=[
            pl.BlockSpec((tm, d), lambda i: (i, 0)),
            pl.BlockSpec((1, 6, d), lambda i: (i // (seq // tm), 0, 0)),
            pl.BlockSpec((d, EVEN_W_COLS), lambda i: (0, 0)),
            pl.BlockSpec((LANES, GLA_QK_W), lambda i: (0, 0)),
            pl.BlockSpec((1, GLA_QK_W), lambda i: (0, 0)),
            pl.BlockSpec((tm, LANES), lambda i: (i, 0)),
            pl.BlockSpec((tm, LANES), lambda i: (i, 0)),
        ],
        out_specs=[pl.BlockSpec((tm, EVEN_Z_W), lambda i: (i, 0)),
                   pl.BlockSpec((tm, EVEN_F_W), lambda i: (i, 0))],
        out_shape=[jax.ShapeDtypeStruct((t, EVEN_Z_W), BF16),
                   jax.ShapeDtypeStruct((t, EVEN_F_W), F32)],
        compiler_params=_cp(("arbitrary",), 48),
        name="proj_even",
    )(x2, mod, w_cat, wg_pad, b_gate.reshape(1, GLA_QK_W), cos, sin)


ODD_Z_W = 2 * SGU_W + 2 * RET_V_W
ODD_F_W = 2 * RET_QK_W
ODD_W_COLS = ODD_Z_W + ODD_F_W


def _gelu_exact(x):
    return 0.5 * x * (1.0 + lax.erf(x * (2.0 ** -0.5)))


def _proj_odd_kernel(x_ref, mod_ref, w_ref, lng_ref, lnb_ref, cos_ref, sin_ref, z_ref, f_ref):
    hb = _modulated_rms(x_ref, mod_ref, 0).astype(BF16)

    def sec(a, b):
        return _dot(hb, w_ref[:, a:b])

    z_ref[:, 0:512] = _gelu_exact(sec(0, 512)).astype(BF16)
    sv = _gelu_exact(sec(512, 1024))
    mu = jnp.mean(sv, axis=-1, keepdims=True)
    cen = sv - mu
    var = jnp.mean(cen * cen, axis=-1, keepdims=True)
    z_ref[:, 512:1024] = (cen * lax.rsqrt(var + EPS) * lng_ref[...] + lnb_ref[...]).astype(BF16)
    for a, b in ((1024, 1536), (1536, 2048)):
        z_ref[:, a:b] = sec(a, b).astype(BF16)
    cos = cos_ref[...]
    sin = sin_ref[...]
    lo_half = (lax.broadcasted_iota(jnp.int32, cos.shape, 1) % 64) < ROPE_HALF
    k_scale = RET_DK ** -0.5
    for a, scl in ((0, 1.0), (256, k_scale)):
        full = sec(ODD_Z_W + a, ODD_Z_W + a + RET_QK_W)
        for c in range(RET_QK_W // LANES):
            v = full[:, c * LANES:(c + 1) * LANES]
            f_ref[:, a + c * LANES:a + (c + 1) * LANES] = _rope_chunk(v, cos, sin, lo_half) * scl


def _proj_odd(x2, mod, w_in, ln_g, ln_b, cos, sin, seq):
    t, d = x2.shape
    tm = min(PROJ_ROWS, seq)
    su, sv, rq, rk, rv, rg = jnp.split(w_in, [512, 1024, 1280, 1536, 2048], axis=1)
    w_cat = jnp.concatenate([su, sv, rv, rg, rq, rk], axis=1).astype(BF16)
    return pl.pallas_call(
        _proj_odd_kernel,
        grid=(t // tm,),
        in_specs=[
            pl.BlockSpec((tm, d), lambda i: (i, 0)),
            pl.BlockSpec((1, 6, d), lambda i: (i // (seq // tm), 0, 0)),
            pl.BlockSpec((d, ODD_W_COLS), lambda i: (0, 0)),
            pl.BlockSpec((1, SGU_W), lambda i: (0, 0)),
            pl.BlockSpec((1, SGU_W), lambda i: (0, 0)),
            pl.BlockSpec((tm, LANES), lambda i: (i, 0)),
            pl.BlockSpec((tm, LANES), lambda i: (i, 0)),
        ],
        out_specs=[pl.BlockSpec((tm, ODD_Z_W), lambda i: (i, 0)),
                   pl.BlockSpec((tm, ODD_F_W), lambda i: (i, 0))],
        out_shape=[jax.ShapeDtypeStruct((t, ODD_Z_W), BF16),
                   jax.ShapeDtypeStruct((t, ODD_F_W), F32)],
        compiler_params=_cp(("arbitrary",), 48),
        name="proj_odd",
    )(x2, mod, w_cat, ln_g.reshape(1, SGU_W), ln_b.reshape(1, SGU_W), cos, sin)


def _gla_kernel(q_ref, k_ref, la_ref, v_ref, gg_ref, g_ref, o_ref, st_ref):
    n_seq = q_ref.shape[0]
    n_pairs = GLA_HEADS // 2

    @pl.when(pl.program_id(1) == 0)
    def _():
        st_ref[...] = jnp.zeros_like(st_ref)

    c = GLA_CHUNK
    sc = GLA_GROUP * c
    rows = q_ref.shape[1]
    r_i = lax.broadcasted_iota(jnp.int32, (sc, sc), 0)
    c_i = lax.broadcasted_iota(jnp.int32, (sc, sc), 1)
    block_causal = (r_i // c == c_i // c) & (r_i >= c_i)
    tril = block_causal.astype(BF16)
    causal2 = jnp.concatenate([block_causal, block_causal], axis=0)
    lo = lax.broadcasted_iota(jnp.int32, (sc, LANES), 1) < GLA_DK
    row_chunk = lax.broadcasted_iota(jnp.int32, (sc, LANES), 0) // c
    lane_s = lax.broadcasted_iota(jnp.int32, (LANES, LANES), 1) < GLA_DK
    g_row = g_ref[...]
    states = [st_ref[n] for n in range(n_seq * n_pairs)]
    for j, s in [(j, s) for j in range(rows // sc) for s in range(n_seq)]:
        rs = slice(j * sc, (j + 1) * sc)
        la = la_ref[s, rs, :]
        la_hi = la.astype(BF16)
        la_lo = (la - la_hi.astype(F32)).astype(BF16)
        b = _dot(tril, la_hi) + _dot(tril, la_lo)
        b_last = [b[(g + 1) * c - 1:(g + 1) * c, :] for g in range(GLA_GROUP)]
        b_last_rows = jnp.concatenate(
            [jnp.broadcast_to(bl, (c, bl.shape[1])) for bl in b_last], axis=0)
        qd = q_ref[s, rs, :] * jnp.exp(b)
        kk = k_ref[s, rs, :]
        ki = kk * jnp.exp(-b)
        kd = kk * jnp.exp(b_last_rows - b)
        for p in range(GLA_HEADS // 2):
            ls = slice(p * LANES, (p + 1) * LANES)
            qd_p = qd[:, ls]
            qm = (jnp.where(lo, qd_p, 0.0).astype(BF16), jnp.where(lo, 0.0, qd_p).astype(BF16))
            s2 = _dot_nt(jnp.concatenate(qm, axis=0), ki[:, ls].astype(BF16))
            s2 = jnp.where(causal2, s2, 0.0).astype(BF16)
            kd_p = kd[:, ls].astype(BF16)
            heads = (2 * p, 2 * p + 1)
            v = [v_ref[s, rs, h * GLA_DV:(h + 1) * GLA_DV] for h in heads]
            sp = s * n_pairs + p

            def by_chunk(a):
                return jnp.concatenate(
                    [jnp.where(row_chunk == g, a, jnp.zeros_like(a)) for g in range(GLA_GROUP)],
                    axis=1)

            kd_blk = by_chunk(kd_p)
            new = [_dot_tn(v[hh], kd_blk) for hh in range(2)]
            entering = []
            for g in range(GLA_GROUP):
                gl = slice(g * LANES, (g + 1) * LANES)
                entering.append(states[sp])
                states[sp] = (jnp.exp(b_last[g][:, ls]) * states[sp]
                              + jnp.where(lane_s, new[0][:, gl], new[1][:, gl]))
            st_stack = jnp.concatenate(entering, axis=1).astype(BF16)
            for hh in range(2):
                hs = slice(heads[hh] * GLA_DV, (heads[hh] + 1) * GLA_DV)
                o = (_dot(s2[hh * sc:(hh + 1) * sc, :], v[hh])
                     + _dot_nt(by_chunk(qm[hh]), st_stack))
                o = _rms_rows(o) * g_row
                gate = _silu(gg_ref[s, rs, hs].astype(F32))
                o_ref[s, rs, hs] = (o * gate).astype(BF16)
    for n in range(n_seq * n_pairs):
        st_ref[n] = states[n]


def _gla(z, f, norm_g, bsz, seq):
    t = z.shape[0]
    tc = min(GLA_ROWS, seq)
    nc = seq // tc
    n_seq = math.gcd(GLA_SEQS_PER_STEP, bsz)
    f3 = f.reshape(bsz, seq, f.shape[1])
    z3 = z.reshape(bsz, seq, z.shape[1])

    def spec(width, col):
        return pl.BlockSpec((n_seq, tc, width), lambda b, i: (b, i, col))

    out = pl.pallas_call(
        _gla_kernel,
        grid=(bsz // n_seq, nc),
        in_specs=[spec(GLA_QK_W, 0), spec(GLA_QK_W, 1), spec(GLA_QK_W, 2),
                  spec(GLA_V_W, 0), spec(GLA_V_W, 1),
                  pl.BlockSpec((1, GLA_DV), lambda b, i: (0, 0))],
        out_specs=spec(GLA_V_W, 0),
        out_shape=jax.ShapeDtypeStruct((bsz, seq, GLA_V_W), BF16),
        scratch_shapes=[pltpu.VMEM((n_seq * (GLA_HEADS // 2), LANES, LANES), F32)],
        compiler_params=_cp(("arbitrary", "arbitrary"), 40),
        name="gla",
    )(f3, f3, f3, z3, z3, norm_g.reshape(1, GLA_DV))
    return out.reshape(t, GLA_V_W)


def _diff_attn_kernel(lam_init, q_ref, k_ref, v_ref, lq1_ref, lk1_ref, lq2_ref, lk2_ref, g_ref,
                      o_ref, vt_ref, s_ref, m_ref, l_ref, acc_ref):
    tq = q_ref.shape[0]
    seq = k_ref.shape[0]
    n_heads = q_ref.shape[1] // LANES
    qi = pl.program_id(2)

    @pl.when(qi == 0)
    def _():
        for hh in range(n_heads):
            for cb in range(seq // tq):
                blk = v_ref[cb * tq:(cb + 1) * tq, hh * LANES:(hh + 1) * LANES]
                vt_ref[hh, :, cb * tq:(cb + 1) * tq] = blk.astype(F32).T.astype(BF16)

    lo = lax.broadcasted_iota(jnp.int32, (tq, LANES), 1) < DIFF_D
    qqs = []
    for hh in range(n_heads):
        q = q_ref[:, hh * LANES:(hh + 1) * LANES]
        zero = jnp.zeros_like(q)
        qqs.append(jnp.concatenate([jnp.where(lo, q, zero), jnp.where(lo, zero, q)], axis=0))

    m_ref[...] = jnp.full(m_ref.shape, -jnp.inf, F32)
    l_ref[...] = jnp.zeros_like(l_ref)
    acc_ref[...] = jnp.zeros_like(acc_ref)

    def scores(j, slot):
        start = pl.multiple_of(j * tq, tq)
        for hh in range(n_heads):
            kj = k_ref[pl.ds(start, tq), hh * LANES:(hh + 1) * LANES]
            s_ref[slot, hh] = _dot_nt(kj, qqs[hh])

    def update(j, slot, masked):
        start = pl.multiple_of(j * tq, tq)
        for hh in range(n_heads):
            vtj = vt_ref[hh, :, pl.ds(start, tq)]
            alpha, p = [], []
            for c in range(2 * tq // LANES):
                cs = slice(c * LANES, (c + 1) * LANES)
                s_c = s_ref[slot, hh, :, cs]
                if masked:
                    kv = lax.broadcasted_iota(jnp.int32, s_c.shape, 0)
                    q_pos = lax.broadcasted_iota(jnp.int32, s_c.shape, 1) + (c * LANES) % tq
                    s_c = jnp.where(kv <= q_pos, s_c, -jnp.inf)
                m_old = m_ref[hh, :, cs]
                m_c = jnp.maximum(m_old, jnp.max(s_c, axis=0, keepdims=True))
                a_c = jnp.exp2(m_old - m_c)
                p_c = jnp.exp2(s_c - m_c)
                m_ref[hh, :, cs] = m_c
                l_ref[hh, :, cs] = a_c * l_ref[hh, :, cs] + jnp.sum(p_c, axis=0, keepdims=True)
                alpha.append(a_c)
                p.append(p_c.astype(BF16))
            acc_ref[hh] = (jnp.concatenate(alpha, axis=1) * acc_ref[hh]
                           + _dot(vtj, jnp.concatenate(p, axis=1)))

    scores(0, 0)

    def body(i, carry):
        j = 2 * i
        scores(j + 1, 1)
        update(j, 0, False)
        scores(j + 2, 0)
        update(j + 1, 1, False)
        return carry

    n_pairs = qi // 2
    lax.fori_loop(0, n_pairs, body, 0)
    j0 = 2 * n_pairs

    @pl.when(qi % 2 == 1)
    def _():
        scores(j0 + 1, 1)
        update(j0, 0, False)
        update(j0 + 1, 1, True)

    @pl.when(qi % 2 == 0)
    def _():
        update(j0, 0, True)

    lam = (jnp.exp(jnp.sum(lq1_ref[...] * lk1_ref[...], axis=-1, keepdims=True))
           - jnp.exp(jnp.sum(lq2_ref[...] * lk2_ref[...], axis=-1, keepdims=True)) + lam_init)
    for hh in range(n_heads):
        l = l_ref[hh]
        acc = acc_ref[hh]
        o12 = acc / l
        o = o12[:, :tq] - lam * o12[:, tq:]
        o = o * lax.rsqrt(jnp.mean(o * o, axis=0, keepdims=True) + EPS)
        o = o * g_ref[...] * (1.0 - lam_init)
        o_ref[:, hh * DIFF_DV:(hh + 1) * DIFF_DV] = o.T.astype(BF16)


def _diff_attn(z, lq1, lk1, lq2, lk2, norm_g, lam_init, bsz, seq):
    t = z.shape[0]
    tq = min(ATT_Q_ROWS, seq)
    nq = seq // tq
    hw = ATT_HEADS_PER_STEP * LANES
    qb, kb, vb = 1024 // hw, 1536 // hw, 2048 // hw
    small = pl.BlockSpec((1, DIFF_D), lambda b, h, i: (0, 0))
    return pl.pallas_call(
        functools.partial(_diff_attn_kernel, lam_init),
        grid=(bsz, DIFF_HEADS // ATT_HEADS_PER_STEP, nq),
        in_specs=[
            pl.BlockSpec((tq, hw), lambda b, h, i: (b * nq + i, qb + h)),
            pl.BlockSpec((seq, hw), lambda b, h, i: (b, kb + h)),
            pl.BlockSpec((seq, hw), lambda b, h, i: (b, vb + h)),
            small, small, small, small,
            pl.BlockSpec((DIFF_DV, 1), lambda b, h, i: (0, 0)),
        ],
        out_specs=pl.BlockSpec((tq, hw), lambda b, h, i: (b * nq + i, h)),
        out_shape=jax.ShapeDtypeStruct((t, DIFF_V_W), BF16),
        scratch_shapes=[pltpu.VMEM((ATT_HEADS_PER_STEP, DIFF_DV, seq), BF16),
                        pltpu.VMEM((2, ATT_HEADS_PER_STEP, tq, 2 * tq), F32),
                        pltpu.VMEM((ATT_HEADS_PER_STEP, 1, 2 * tq), F32),
                        pltpu.VMEM((ATT_HEADS_PER_STEP, 1, 2 * tq), F32),
                        pltpu.VMEM((ATT_HEADS_PER_STEP, DIFF_DV, 2 * tq), F32)],
        compiler_params=_cp(("arbitrary", "arbitrary", "arbitrary"), 32),
        name="diff_attn",
    )(z, z, z, lq1.reshape(1, DIFF_D), lk1.reshape(1, DIFF_D), lq2.reshape(1, DIFF_D),
      lk2.reshape(1, DIFF_D), norm_g.reshape(DIFF_DV, 1))


def _sgu_ret_kernel(su_ref, sv_ref, rv_ref, rg_ref, q_ref, k_ref, ws_ref, bs_ref, o_ref, st_ref):
    @pl.when(pl.program_id(1) == 0)
    def _():
        st_ref[...] = jnp.zeros_like(st_ref)

    c = RET_CHUNK
    row = lax.broadcasted_iota(jnp.int32, (c, c), 0)
    col = lax.broadcasted_iota(jnp.int32, (c, c), 1)
    causal = row >= col
    log_g = [math.log(1.0 - 2.0 ** (-5.0 - h)) for h in range(RET_HEADS)]
    lo = col < RET_DK
    rel = (row - col).astype(F32)
    pos = row.astype(F32)
    w_sgu = [jnp.where(causal, ws_ref[g], 0.0).astype(BF16) for g in range(SGU_GROUPS)]
    decays = [jnp.where(causal, jnp.exp(log_g[h] * jnp.maximum(rel, 0.0)), 0.0)
              for h in range(RET_HEADS)]
    lgs = [jnp.where(lo, log_g[2 * p], log_g[2 * p + 1]) for p in range(RET_HEADS // 2)]
    q_decs = [jnp.exp(lg * (pos + 1.0)) for lg in lgs]
    k_decs = [jnp.exp(lg * (c - 1.0 - pos)) for lg in lgs]
    states = [st_ref[p] for p in range(RET_HEADS // 2)]
    for j in range(su_ref.shape[0] // c):
        rs = slice(j * c, (j + 1) * c)
        for g in range(SGU_GROUPS):
            gs = slice(g * SGU_CH, (g + 1) * SGU_CH)
            s = _dot(w_sgu[g], sv_ref[rs, gs]) + bs_ref[g]
            o_ref[rs, gs] = (su_ref[rs, gs].astype(F32) * s).astype(BF16)
        for p in range(RET_HEADS // 2):
            ls = slice(p * LANES, (p + 1) * LANES)
            q_p = q_ref[rs, ls]
            k_p = k_ref[rs, ls]
            qm = (jnp.where(lo, q_p, 0.0).astype(BF16), jnp.where(lo, 0.0, q_p).astype(BF16))
            s2 = _dot_nt(jnp.concatenate(qm, axis=0), k_p.astype(BF16))
            qd = q_p * q_decs[p]
            qdm = (jnp.where(lo, qd, 0.0).astype(BF16), jnp.where(lo, 0.0, qd).astype(BF16))
            kd = (k_p * k_decs[p]).astype(BF16)
            st_b = states[p].astype(BF16)
            new = []
            for hh in range(2):
                h = 2 * p + hh
                hs = slice(h * RET_DV, (h + 1) * RET_DV)
                s_h = (s2[hh * c:(hh + 1) * c, :] * decays[h]).astype(BF16)
                v_h = rv_ref[rs, hs]
                o = _dot(s_h, v_h) + _dot_nt(qdm[hh], st_b)
                gate = _silu(rg_ref[rs, hs].astype(F32))
                o_ref[rs, SGU_W + h * RET_DV:SGU_W + (h + 1) * RET_DV] = (
                    _rms_rows(o) * gate).astype(BF16)
                new.append(_dot_tn(v_h, kd))
            states[p] = jnp.exp(lgs[p] * float(c)) * states[p] + jnp.where(lo, new[0], new[1])
    for p in range(RET_HEADS // 2):
        st_ref[p] = states[p]


def _sgu_ret(z, f, w_s, b_s, bsz, seq):
    t = z.shape[0]
    cc = RET_CHUNK
    c = min(SGU_RET_ROWS, seq)
    nc = seq // c
    return pl.pallas_call(
        _sgu_ret_kernel,
        grid=(bsz, nc),
        in_specs=[
            pl.BlockSpec((c, SGU_W), lambda b, i: (b * nc + i, 0)),
            pl.BlockSpec((c, SGU_W), lambda b, i: (b * nc + i, 1)),
            pl.BlockSpec((c, RET_V_W), lambda b, i: (b * nc + i, 2)),
            pl.BlockSpec((c, RET_V_W), lambda b, i: (b * nc + i, 3)),
            pl.BlockSpec((c, RET_QK_W), lambda b, i: (b * nc + i, 0)),
            pl.BlockSpec((c, RET_QK_W), lambda b, i: (b * nc + i, 1)),
            pl.BlockSpec((SGU_GROUPS, cc, cc), lambda b, i: (0, 0, 0)),
            pl.BlockSpec((SGU_GROUPS, cc, 1), lambda b, i: (0, 0, 0)),
        ],
        out_specs=pl.BlockSpec((c, SGU_W + RET_V_W), lambda b, i: (b * nc + i, 0)),
        out_shape=jax.ShapeDtypeStruct((t, SGU_W + RET_V_W), BF16),
        scratch_shapes=[pltpu.VMEM((2, LANES, LANES), F32)],
        compiler_params=_cp(("arbitrary", "arbitrary"), 32),
        name="sgu_retention",
    )(z, z, z, z, f, f, w_s, b_s.reshape(SGU_GROUPS, cc, 1))


def _out_proj_kernel(n_in, *refs):
    o_refs = refs[:n_in]
    w_ref, x_ref, mod_ref, rw_ref, rb_ref, xn_ref, h_ref, lg_ref = refs[n_in:]
    k_each = D_MODEL // n_in
    y = _dot(o_refs[0][...], w_ref[0:k_each, :])
    for n in range(1, n_in):
        y = y + _dot(o_refs[n][...], w_ref[n * k_each:(n + 1) * k_each, :])
    xn = x_ref[...] + mod_ref[0, 2:3, :] * y
    xn_ref[...] = xn
    h = _rms_rows(xn) * (1.0 + mod_ref[0, 4:5, :]) + mod_ref[0, 3:4, :]
    hb = h.astype(BF16)
    h_ref[...] = hb
    lg_ref[...] = _dot_nt(rw_ref[...], hb) + rb_ref[...]


def _out_proj(mixed, w_out, x2, mod, router_w, router_b, seq):
    t, d = x2.shape
    tm = min(PROJ_ROWS, seq)
    n_in = len(mixed)
    k_each = d // n_in
    rw = router_w.T.astype(BF16)
    rb = router_b.reshape(N_EXPERTS, 1)
    return pl.pallas_call(
        functools.partial(_out_proj_kernel, n_in),
        grid=(t // tm,),
        in_specs=[pl.BlockSpec((tm, k_each), lambda i: (i, 0)) for _ in mixed] + [
            pl.BlockSpec((d, d), lambda i: (0, 0)),
            pl.BlockSpec((tm, d), lambda i: (i, 0)),
            pl.BlockSpec((1, 6, d), lambda i: (i // (seq // tm), 0, 0)),
            pl.BlockSpec((N_EXPERTS, d), lambda i: (0, 0)),
            pl.BlockSpec((N_EXPERTS, 1), lambda i: (0, 0)),
        ],
        out_specs=[pl.BlockSpec((tm, d), lambda i: (i, 0)),
                   pl.BlockSpec((tm, d), lambda i: (i, 0)),
                   pl.BlockSpec((N_EXPERTS, tm), lambda i: (0, i))],
        out_shape=[jax.ShapeDtypeStruct((t, d), F32),
                   jax.ShapeDtypeStruct((t, d), BF16),
                   jax.ShapeDtypeStruct((N_EXPERTS, t), F32)],
        compiler_params=_cp(("arbitrary",), 48),
        name="out_proj",
    )(*mixed, w_out.astype(BF16), x2, mod, rw, rb)


def _route_kernel(lg_ref, idx_ref, gate_ref, tcnt_ref, toff_ref, trun_ref, cnt_ref, run_ref):
    @pl.when(pl.program_id(0) == 0)
    def _():
        run_ref[...] = jnp.zeros_like(run_ref)

    tm = lg_ref.shape[1]
    row = lax.broadcasted_iota(jnp.int32, (N_EXPERTS, tm), 0)
    neg = -jnp.inf
    l = lg_ref[...]
    vals, firsts, hots = [], [], []
    for _ in range(TOP_K):
        m = jnp.max(l, axis=0, keepdims=True)
        first = jnp.min(jnp.where(l == m, row, N_EXPERTS), axis=0, keepdims=True)
        hot = row == first
        vals.append(m)
        firsts.append(first)
        hots.append(hot)
        l = jnp.where(hot, neg, l)
    sel = hots[0] | hots[1] | hots[2] | hots[3]
    ex = [jnp.exp(v - vals[0]) for v in vals]
    denom = ex[0] + ex[1] + ex[2] + ex[3]
    r_i = lax.broadcasted_iota(jnp.int32, (tm, tm), 0)
    c_i = lax.broadcasted_iota(jnp.int32, (tm, tm), 1)
    before = (r_i < c_i).astype(BF16)
    earlier = _dot(sel.astype(BF16), before)
    tile_cnt = jnp.sum(sel.astype(F32), axis=1, keepdims=True)
    lower = jnp.zeros((N_EXPERTS, tm), F32)
    for k in range(TOP_K):
        lower = lower + (firsts[k] < row).astype(F32)
    tile_off = jnp.sum(lower, axis=1, keepdims=True)
    slot = tile_off + earlier
    row8 = lax.broadcasted_iota(jnp.int32, (2 * TOP_K, tm), 0)
    idx_out = jnp.zeros((2 * TOP_K, tm), jnp.int32)
    gate_out = jnp.zeros((2 * TOP_K, tm), F32)
    for k in range(TOP_K):
        pos_k = jnp.sum(jnp.where(hots[k], slot, 0.0), axis=0, keepdims=True).astype(jnp.int32)
        idx_out = jnp.where(row8 == k, firsts[k], idx_out)
        idx_out = jnp.where(row8 == TOP_K + k, pos_k, idx_out)
        gate_out = jnp.where(row8 == k, ex[k] / denom, gate_out)
    idx_ref[...] = idx_out
    gate_ref[...] = gate_out
    run = run_ref[:, 0:1]
    tcnt_ref[...] = jnp.broadcast_to(tile_cnt, tcnt_ref.shape)
    toff_ref[...] = jnp.broadcast_to(tile_off, toff_ref.shape)
    trun_ref[...] = jnp.broadcast_to(run, trun_ref.shape)
    total = run + tile_cnt
    run_ref[...] = jnp.broadcast_to(total, run_ref.shape)
    cnt_ref[...] = jnp.broadcast_to(total, cnt_ref.shape)


def _route(logits_t):
    t = logits_t.shape[1]
    tm = min(MOE_TILE, t)
    nt = t // tm
    per_tile = pl.BlockSpec((N_EXPERTS, LANES), lambda i: (i, 0))
    per_tile_shape = jax.ShapeDtypeStruct((nt * N_EXPERTS, LANES), F32)
    return pl.pallas_call(
        _route_kernel,
        grid=(nt,),
        in_specs=[pl.BlockSpec((N_EXPERTS, tm), lambda i: (0, i))],
        out_specs=[pl.BlockSpec((2 * TOP_K, tm), lambda i: (0, i)),
                   pl.BlockSpec((2 * TOP_K, tm), lambda i: (0, i)),
                   per_tile, per_tile, per_tile,
                   pl.BlockSpec((N_EXPERTS, LANES), lambda i: (0, 0))],
        out_shape=[jax.ShapeDtypeStruct((2 * TOP_K, t), jnp.int32),
                   jax.ShapeDtypeStruct((2 * TOP_K, t), F32),
                   per_tile_shape, per_tile_shape, per_tile_shape,
                   jax.ShapeDtypeStruct((N_EXPERTS, LANES), F32)],
        scratch_shapes=[pltpu.VMEM((N_EXPERTS, LANES), F32)],
        compiler_params=_cp(("arbitrary",), 32),
        name="route",
    )(logits_t)


def _rows(ref, start, n):
    return ref.at[pl.ds(pl.multiple_of(start * ROW_TILES, ROW_TILES), n * ROW_TILES), :]


def _run_copies(tile, cnt_ref, off_ref, dst_ref, make_copy):
    def per_expert(e, carry):
        j = tile * N_EXPERTS + e
        cnt, off, dst = cnt_ref[j], off_ref[j], dst_ref[j]

        for b in reversed(range(RUN_BITS)):
            size = 1 << b

            @pl.when((cnt & size) != 0)
            def _():
                done = lax.shift_left(lax.shift_right_logical(cnt, b + 1), b + 1)
                make_copy(off + done, dst + done, size).start()
        return carry

    lax.fori_loop(0, N_EXPERTS, per_expert, 0)


def _dispatch_kernel(fs_ref, fl_ref, cnt_ref, off_ref, dst_ref, h_ref, idx_ref, xs_ref,
                     stage, zero_buf, sems, zsem):
    i = pl.program_id(0)
    n_steps = pl.num_programs(0)
    cur = lax.rem(i, 2)
    tm = h_ref.shape[0]
    n_sorted = TOP_K * tm
    n_fill = fs_ref.shape[0]

    def fill_copies(f, wait):
        start, n = fs_ref[f], fl_ref[f]
        n_chunks = lax.shift_right_logical(n, FILL_SHIFT)
        tail = start + n_chunks * FILL_ROWS

        def chunk(j, carry):
            cp = pltpu.make_async_copy(zero_buf, _rows(xs_ref, start + j * FILL_ROWS, FILL_ROWS),
                                       zsem)
            cp.wait() if wait else cp.start()
            return carry

        def single(j, carry):
            cp = pltpu.make_async_copy(_rows(zero_buf, 0, 1), _rows(xs_ref, tail + j, 1), zsem)
            cp.wait() if wait else cp.start()
            return carry

        lax.fori_loop(0, n_chunks, chunk, 0)
        lax.fori_loop(0, n - n_chunks * FILL_ROWS, single, 0)

    @pl.when(i == 0)
    def _():
        zero_buf[...] = jnp.zeros_like(zero_buf)
        lax.fori_loop(0, n_fill, lambda f, c: (fill_copies(f, False), c)[1], 0)

    h = h_ref[...]
    pos = [idx_ref[TOP_K + k:TOP_K + k + 1, :] for k in range(TOP_K)]
    for sb in range(n_sorted // SORT_SLAB):
        j = lax.broadcasted_iota(jnp.int32, (SORT_SLAB, tm), 0) + sb * SORT_SLAB
        perm = jnp.zeros((SORT_SLAB, tm), F32)
        for k in range(TOP_K):
            perm = jnp.where(pos[k] == j, 1.0, perm)
        perm = perm.astype(BF16)
        rows = _dot(perm, h)
        for c in range(ROW_TILES):
            stage[cur, pl.ds(sb * SORT_SLAB * ROW_TILES + c, SORT_SLAB, stride=ROW_TILES), :] = (
                rows[:, c * LANES:(c + 1) * LANES])

    def make_copy(src_row, dst_row, size):
        return pltpu.make_async_copy(_rows(stage.at[cur], src_row, size),
                                     _rows(xs_ref, dst_row, size), sems.at[cur])

    _run_copies(i, cnt_ref, off_ref, dst_ref, make_copy)

    def wait_slot(slot):
        pltpu.make_async_copy(stage.at[slot], _rows(xs_ref, 0, n_sorted), sems.at[slot]).wait()

    @pl.when(i > 0)
    def _():
        wait_slot(1 - cur)

    @pl.when(i == n_steps - 1)
    def _():
        wait_slot(cur)

    @pl.when(i == 0)
    def _():
        lax.fori_loop(0, n_fill, lambda f, c: (fill_copies(f, True), c)[1], 0)


def _dispatch(h2, idx, tile_cnt, tile_off, tile_dst, fill_start, fill_len, n_rows):
    t, d = h2.shape
    tm = min(MOE_TILE, t)
    grid_spec = pltpu.PrefetchScalarGridSpec(
        num_scalar_prefetch=5,
        grid=(t // tm,),
        in_specs=[pl.BlockSpec((tm, d), lambda i, *_: (i, 0)),
                  pl.BlockSpec((2 * TOP_K, tm), lambda i, *_: (0, i))],
        out_specs=pl.BlockSpec(memory_space=pl.ANY),
        scratch_shapes=[pltpu.VMEM((2, TOP_K * tm * ROW_TILES, LANES), F32),
                        pltpu.VMEM((FILL_ROWS * ROW_TILES, LANES), F32),
                        pltpu.SemaphoreType.DMA((2,)), pltpu.SemaphoreType.DMA(())],
    )
    return pl.pallas_call(
        _dispatch_kernel,
        grid_spec=grid_spec,
        out_shape=jax.ShapeDtypeStruct((n_rows * ROW_TILES, LANES), F32),
        compiler_params=_cp(("arbitrary",), 48),
        name="dispatch",
    )(fill_start, fill_len, tile_cnt, tile_off, tile_dst, h2, idx)


def _expert_kernel(layer, be_ref, nu_ref, nx_ref, nv_ref, xs_ref, wi_hbm, bi_ref, wo_hbm, bo_ref,
                   y_ref, wi_st, wo_st, wi_b, wo_b, sems):
    tb = EXPERT_ROWS

    def fetch(ex):
        return (pltpu.make_async_copy(wi_hbm.at[layer, ex], wi_st, sems.at[0]),
                pltpu.make_async_copy(wo_hbm.at[layer, ex], wo_st, sems.at[1]))

    @pl.when(pl.program_id(0) == 0)
    def _():
        for cp in fetch(be_ref[0]):
            cp.start()

    def block(i, r0):
        e = be_ref[i]
        fresh = jnp.logical_or(i == 0, e != be_ref[jnp.maximum(i - 1, 0)])
        used = i < nu_ref[0]

        def take_weights():
            for cp in fetch(e):
                cp.wait()
            wi_b[...] = wi_st[...].astype(BF16)
            wo_b[...] = wo_st[...].astype(BF16)

        def fetch_next():
            @pl.when(nx_ref[i] >= 0)
            def _():
                for cp in fetch(nx_ref[i]):
                    cp.start()

        def compute(n):
            x = jnp.concatenate(
                [xs_ref[pl.ds(r0 * ROW_TILES + c, n, stride=ROW_TILES), :]
                 for c in range(ROW_TILES)], axis=1).astype(BF16)
            glu = _dot(x, wi_b[:, 0:D_FF]) + bi_ref[0, e, :, 0:D_FF]
            lin = _dot(x, wi_b[:, D_FF:2 * D_FF]) + bi_ref[0, e, :, D_FF:2 * D_FF]
            glu = jnp.minimum(glu, SWIGLU_LIMIT)
            lin = jnp.clip(lin, -SWIGLU_LIMIT, SWIGLU_LIMIT)
            act = glu * jax.nn.sigmoid(SWIGLU_ALPHA * glu) * (lin + 1.0)
            y = _dot(act.astype(BF16), wo_b[...]) + bo_ref[0, e]
            for c in range(ROW_TILES):
                y_ref[pl.ds(r0 * ROW_TILES + c, n, stride=ROW_TILES), :] = (
                    y[:, c * LANES:(c + 1) * LANES])

        def zero_rows(start, n):
            y_ref[pl.ds((r0 + start) * ROW_TILES, n * ROW_TILES), :] = jnp.zeros(
                (n * ROW_TILES, LANES), F32)

        full = nv_ref[i] > tb // 2

        @pl.when(jnp.logical_and(used, jnp.logical_and(full, fresh)))
        def _():
            take_weights()
            compute(tb)
            fetch_next()

        @pl.when(jnp.logical_and(used, jnp.logical_and(full, jnp.logical_not(fresh))))
        def _():
            compute(tb)

        @pl.when(jnp.logical_and(used, jnp.logical_not(full)))
        def _():
            @pl.when(fresh)
            def _():
                take_weights()
                fetch_next()

            compute(tb // 2)
            zero_rows(tb // 2, tb // 2)

        @pl.when(jnp.logical_not(used))
        def _():
            zero_rows(0, tb)

    for sub in range(EXPERT_BLOCKS_PER_STEP):
        block(pl.program_id(0) * EXPERT_BLOCKS_PER_STEP + sub, sub * tb)


def _experts(xs, block_e, n_used, next_e, n_valid, layer, w_in, b_in, w_out, b_out):
    per_step = EXPERT_BLOCKS_PER_STEP
    win = per_step * EXPERT_ROWS * ROW_TILES
    n_steps = xs.shape[0] // win
    assert n_steps * win == xs.shape[0]
    depth, ne, d, f2 = w_in.shape

    def row_map(i, be, nu, nx, nv):
        return (jnp.minimum(i, (nu[0] + per_step - 1) // per_step - 1), 0)

    grid_spec = pltpu.PrefetchScalarGridSpec(
        num_scalar_prefetch=4,
        grid=(n_steps,),
        in_specs=[
            pl.BlockSpec((win, LANES), row_map),
            pl.BlockSpec(memory_space=pl.ANY),
            pl.BlockSpec((1, ne, 1, f2), lambda i, be, nu, nx, nv: (layer, 0, 0, 0)),
            pl.BlockSpec(memory_space=pl.ANY),
            pl.BlockSpec((1, ne, 1, d), lambda i, be, nu, nx, nv: (layer, 0, 0, 0)),
        ],
        out_specs=pl.BlockSpec((win, LANES), lambda i, be, nu, nx, nv: (i, 0)),
        scratch_shapes=[pltpu.VMEM((d, f2), F32), pltpu.VMEM((D_FF, d), F32),
                        pltpu.VMEM((d, f2), BF16), pltpu.VMEM((D_FF, d), BF16),
                        pltpu.SemaphoreType.DMA((2,))],
    )
    return pl.pallas_call(
        functools.partial(_expert_kernel, layer),
        grid_spec=grid_spec,
        out_shape=jax.ShapeDtypeStruct(xs.shape, F32),
        compiler_params=_cp(("arbitrary",), 56),
        name="experts",
    )(block_e, n_used, next_e, n_valid, xs, w_in, b_in.reshape(depth, ne, 1, f2), w_out,
      b_out.reshape(depth, ne, 1, d))


def _combine_kernel(final, cnt_ref, off_ref, dst_ref, pos_ref, gate_ref, x_ref, mod_ref, fg_ref,
                    yb_ref, o_ref, stage, sems):
    tm = x_ref.shape[0]
    n_sorted = TOP_K * tm
    i = pl.program_id(0)
    cur = lax.rem(i, 2)

    def fetch(tile, slot):
        def make_copy(sorted_row, src_row, size):
            return pltpu.make_async_copy(_rows(yb_ref, src_row, size),
                                         _rows(stage.at[slot], sorted_row, size), sems.at[slot])

        _run_copies(tile, cnt_ref, off_ref, dst_ref, make_copy)

    @pl.when(i == 0)
    def _():
        fetch(0, 0)

    @pl.when(i + 1 < pl.num_programs(0))
    def _():
        fetch(i + 1, 1 - cur)

    pltpu.make_async_copy(_rows(yb_ref, 0, n_sorted), stage.at[cur], sems.at[cur]).wait()

    pos = [pos_ref[:, TOP_K + k:TOP_K + k + 1] for k in range(TOP_K)]
    gate = [gate_ref[:, k:k + 1] for k in range(TOP_K)]
    y = jnp.zeros((tm, D_MODEL), F32)
    for sb in range(n_sorted // SORT_SLAB):
        rows = jnp.concatenate(
            [stage[cur, pl.ds(sb * SORT_SLAB * ROW_TILES + c, SORT_SLAB, stride=ROW_TILES), :]
             for c in range(ROW_TILES)], axis=1).astype(BF16)
        j = lax.broadcasted_iota(jnp.int32, (tm, SORT_SLAB), 1) + sb * SORT_SLAB
        w = jnp.zeros((tm, SORT_SLAB), F32)
        for k in range(TOP_K):
            w = jnp.where(pos[k] == j, gate[k], w)
        y = y + _dot(w.astype(BF16), rows)
    o = x_ref[...] + mod_ref[0, 5:6, :] * y
    if final:
        o = _rms_rows(o) * fg_ref[...]
    o_ref[...] = o


def _combine(yb, tile_cnt, tile_off, tile_dst, pos_t, gates_t, x2, mod, final_g, final, seq):
    t, d = x2.shape
    tm = min(MOE_TILE, seq)
    grid_spec = pltpu.PrefetchScalarGridSpec(
        num_scalar_prefetch=3,
        grid=(t // tm,),
        in_specs=[
            pl.BlockSpec((tm, 2 * TOP_K), lambda i, *_: (i, 0)),
            pl.BlockSpec((tm, 2 * TOP_K), lambda i, *_: (i, 0)),
            pl.BlockSpec((tm, d), lambda i, *_: (i, 0)),
            pl.BlockSpec((1, 6, d), lambda i, *_: (i // (seq // tm), 0, 0)),
            pl.BlockSpec((1, d), lambda i, *_: (0, 0)),
            pl.BlockSpec(memory_space=pl.ANY),
        ],
        out_specs=pl.BlockSpec((tm, d), lambda i, *_: (i, 0)),
        scratch_shapes=[pltpu.VMEM((2, TOP_K * tm * ROW_TILES, LANES), F32),
                        pltpu.SemaphoreType.DMA((2,))],
    )
    return pl.pallas_call(
        functools.partial(_combine_kernel, final),
        grid_spec=grid_spec,
        out_shape=jax.ShapeDtypeStruct((t, d), F32),
        compiler_params=_cp(("arbitrary",), 48),
        name="combine",
    )(tile_cnt, tile_off, tile_dst, pos_t, gates_t, x2, mod, final_g.reshape(1, d), yb)


def _moe(h2, logits, x2, mod, layer, w_in, b_in, w_out, b_out, final_g, final, seq):
    t = x2.shape[0]
    tb = EXPERT_ROWS
    idx, gates, tcnt, toff, trun, cnt = _route(logits)
    counts = cnt[:, 0].astype(jnp.int32)
    nblk = (counts + tb - 1) // tb
    blk_end = jnp.cumsum(nblk)
    pad_start = (blk_end - nblk) * tb
    n_blocks = (t * TOP_K) // tb + N_EXPERTS
    n_used = blk_end[-1:]
    experts = jnp.arange(N_EXPERTS, dtype=jnp.int32)
    last_e = jnp.max(jnp.where(nblk > 0, experts, 0))
    blk = jnp.arange(n_blocks, dtype=jnp.int32)
    block_e = jnp.minimum(
        jnp.sum((blk_end[None, :] <= blk[:, None]).astype(jnp.int32), axis=1), last_e)
    later = (experts[None, :] > block_e[:, None]) & (nblk[None, :] > 0)
    next_e = jnp.min(jnp.where(later, experts[None, :], N_EXPERTS), axis=1)
    next_e = jnp.where(next_e == N_EXPERTS, -1, next_e).astype(jnp.int32)
    tile_cnt = tcnt[:, 0].astype(jnp.int32)
    tile_off = toff[:, 0].astype(jnp.int32)
    tile_dst = (trun[:, 0].astype(jnp.int32).reshape(-1, N_EXPERTS) + pad_start[None, :]).reshape(-1)
    fill_start = jnp.concatenate([pad_start + counts, blk_end[-1:] * tb]).astype(jnp.int32)
    fill_len = jnp.concatenate([nblk * tb - counts, (n_blocks - blk_end[-1:]) * tb]).astype(jnp.int32)
    xs = _dispatch(h2, idx, tile_cnt, tile_off, tile_dst, fill_start, fill_len, n_blocks * tb)
    row_end = jnp.sum(jnp.where(block_e[:, None] == experts[None, :],
                                (pad_start + counts)[None, :], 0), axis=1)
    n_valid = jnp.clip(row_end - blk * tb, 0, tb).astype(jnp.int32)
    yb = _experts(xs, block_e, n_used.astype(jnp.int32), next_e, n_valid, layer,
                  w_in, b_in, w_out, b_out)
    return _combine(yb, tile_cnt, tile_off, tile_dst, idx.T, gates.T, x2, mod, final_g, final, seq)


def kernel(x, c, positions, w_ada, b_ada, even_w_in, gla_w_gate, gla_b_gate, gla_norm_g,
           diff_lam_q1, diff_lam_k1, diff_lam_q2, diff_lam_k2, diff_norm_g, even_w_out,
           odd_w_in, sgu_ln_g, sgu_ln_b, sgu_w, sgu_b, odd_w_out,
           router_w, router_b, expert_w_in, expert_b_in, expert_w_out, expert_b_out,
           final_norm_g):
    bsz, seq, d = x.shape
    depth = w_ada.shape[0]
    t = bsz * seq
    mods = _modulation(c, w_ada, b_ada).reshape(depth, bsz, 6, d)
    cos, sin = _rope_tables(positions)
    x2 = x.reshape(t, d)
    for layer in range(depth):
        mod = mods[layer]
        j = layer // 2
        if layer % 2 == 0:
            z, f = _proj_even(x2, mod, even_w_in[j], gla_w_gate[j], gla_b_gate[j], cos, sin, seq)
            o_gla = _gla(z, f, gla_norm_g[j], bsz, seq)
            lam_init = 0.8 - 0.6 * math.exp(-0.3 * layer)
            o_diff = _diff_attn(z, diff_lam_q1[j], diff_lam_k1[j], diff_lam_q2[j], diff_lam_k2[j],
                                diff_norm_g[j], lam_init, bsz, seq)
            mixed, w_out = (o_gla, o_diff), even_w_out[j]
        else:
            z, f = _proj_odd(x2, mod, odd_w_in[j], sgu_ln_g[j], sgu_ln_b[j], cos, sin, seq)
            mixed, w_out = (_sgu_ret(z, f, sgu_w[j], sgu_b[j], bsz, seq),), odd_w_out[j]
        x2, h2, logits = _out_proj(mixed, w_out, x2, mod, router_w[layer], router_b[layer], seq)
        x2 = _moe(h2, logits, x2, mod, layer, expert_w_in, expert_b_in, expert_w_out,
                  expert_b_out, final_norm_g, layer == depth - 1, seq)
    return x2.reshape(bsz, seq, d)
```
